```python
import jax, jax.numpy as jnp
from jax import lax
import numpy as np

D_MODEL = 1024
BATCH = 32
SEQ = 256
DEPTH = 2
DEC_BATCH = 2
DEC_SEQ = 1024
PAST_LEN = 512

GRID_W = 64
CHUNK = 128
QBLOCK = 128
WINDOW = 128
ROPE_THETA = 10000.0
EPS = 1e-6
NEG_INF = -1e30
A_WIDTH = 512
A_GROUPS = 4
A_GDIM = A_WIDTH // A_GROUPS
B_HEADS = 8
B_KV_HEADS = 2
B_GROUP = B_HEADS // B_KV_HEADS
B_HEAD_DIM = 64
B_WIDTH = B_HEADS * B_HEAD_DIM
B_SCALE = B_HEAD_DIM ** -0.5
EVEN_IN = 2 * A_WIDTH + B_WIDTH + 2 * B_KV_HEADS * B_HEAD_DIM
EVEN_SPLIT = (A_WIDTH, 2 * A_WIDTH, 2 * A_WIDTH + B_WIDTH, 2 * A_WIDTH + B_WIDTH + B_KV_HEADS * B_HEAD_DIM)
C_HEADS = 16
C_Q_LORA = 384
C_KV_LORA = 256
C_NOPE = 64
C_ROPE = 32
C_QK = C_NOPE + C_ROPE
C_V = 64
C_SCALE = C_QK ** -0.5
ODD_IN = C_Q_LORA + C_KV_LORA + C_ROPE
N_GROUPS = 4
EXPERTS_PER_GROUP = 8
N_EXPERTS = N_GROUPS * EXPERTS_PER_GROUP
TOP_K = 2
D_EXPERT = 256

kernel_name = "hybrid_diffusion_prefix_trunk_step"

f32 = jnp.float32


def rms_norm(x, g):
    xf = x.astype(f32)
    y = xf * lax.rsqrt(jnp.mean(xf * xf, -1, keepdims=True) + EPS)
    return (y * g.astype(f32)).astype(x.dtype)


def layer_norm(x, g):
    xf = x.astype(f32)
    mu = jnp.mean(xf, -1, keepdims=True)
    var = jnp.mean(jnp.square(xf - mu), -1, keepdims=True)
    return ((xf - mu) * lax.rsqrt(var + EPS) * g.astype(f32)).astype(x.dtype)


def adaln(cond, w, b):
    m = jax.nn.silu(cond) @ w + b
    return jnp.split(m[..., None, :], 6, axis=-1)


def axial_angles(n, rot_dim):
    rows_count = n // GRID_W
    rows = jnp.repeat(jnp.arange(rows_count), GRID_W).astype(f32)
    cols = jnp.tile(jnp.arange(GRID_W), rows_count).astype(f32)
    d_axis = rot_dim // 2
    inv = ROPE_THETA ** (-jnp.arange(0, d_axis, 2, dtype=f32) / d_axis)
    ang = jnp.concatenate([rows[:, None] * inv, cols[:, None] * inv], -1)
    return jnp.cos(ang), jnp.sin(ang)


def apply_rope(x, cos, sin):
    xf = x.astype(f32).reshape(x.shape[:-1] + (x.shape[-1] // 2, 2))
    x1, x2 = xf[..., 0], xf[..., 1]
    c, s = cos[:, None, :], sin[:, None, :]
    out = jnp.stack([x1 * c - x2 * s, x1 * s + x2 * c], -1).reshape(x.shape)
    return out.astype(x.dtype)


def softmax_sink(s, sink):
    if sink is None:
        return jax.nn.softmax(s, -1)
    sk = sink.astype(f32)[None, :, :, None, None]
    m = jnp.maximum(jnp.max(s, -1, keepdims=True), sk)
    e = jnp.exp(s - m)
    return e / (jnp.sum(e, -1, keepdims=True) + jnp.exp(sk - m))


def dense_attention(q, k, v, scale, sink):
    bn, nq, kvh, grp, _ = q.shape
    dv = v.shape[-1]
    nb = nq // QBLOCK
    kf, vf = k.astype(f32), v.astype(f32)
    qb = jnp.moveaxis(q.reshape(bn, nb, QBLOCK, kvh, grp, q.shape[-1]), 1, 0)

    def one(qi):
        s = jnp.einsum('bqkgd,bskd->bkgqs', qi.astype(f32), kf) * scale
        p = softmax_sink(s, sink)
        return jnp.einsum('bkgqs,bskd->bqkgd', p, vf).astype(v.dtype)

    o = lax.map(one, qb)
    return jnp.moveaxis(o, 0, 1).reshape(bn, nq, kvh * grp * dv)


def window_attention(q, k, v, kc, vc, scale, sink):
    bn, n, kvh, grp, dh = q.shape
    dv = v.shape[-1]
    nb = n // QBLOCK
    span = QBLOCK + 2 * WINDOW
    pad = ((0, 0), (WINDOW, WINDOW), (0, 0), (0, 0))
    kp = jnp.pad(k, pad).astype(f32)
    vp = jnp.pad(v, pad).astype(f32)
    kcf, vcf = kc.astype(f32), vc.astype(f32)
    qb = jnp.moveaxis(q.reshape(bn, nb, QBLOCK, kvh, grp, dh), 1, 0)

    def one(args):
        qi, start = args
        kw = lax.dynamic_slice_in_dim(kp, start, span, axis=1)
        vw = lax.dynamic_slice_in_dim(vp, start, span, axis=1)
        qpos = start + jnp.arange(QBLOCK)
        kpos = start - WINDOW + jnp.arange(span)
        valid = (jnp.abs(qpos[:, None] - kpos[None, :]) <= WINDOW) & (kpos >= 0) & (kpos < n)
        qf = qi.astype(f32)
        s_loc = jnp.where(valid, jnp.einsum('bqkgd,bskd->bkgqs', qf, kw) * scale, NEG_INF)
        s_ctx = jnp.einsum('bqkgd,bskd->bkgqs', qf, kcf) * scale
        p = softmax_sink(jnp.concatenate([s_loc, s_ctx], -1), sink)
        o = (jnp.einsum('bkgqs,bskd->bqkgd', p[..., :span], vw)
             + jnp.einsum('bkgqs,bskd->bqkgd', p[..., span:], vcf))
        return o.astype(v.dtype)

    o = lax.map(one, (qb, jnp.arange(nb) * QBLOCK))
    return jnp.moveaxis(o, 0, 1).reshape(bn, n, kvh * grp * dv)


def chunk_gmlp(u, v, g_vnorm, w_s, b_s):
    bn, n, _ = u.shape
    u = jax.nn.gelu(u)
    v = layer_norm(jax.nn.gelu(v), g_vnorm)
    vc = v.reshape(bn, n // CHUNK, CHUNK, A_GROUPS, A_GDIM)
    mixed = jnp.einsum('gts,bnsgc->bntgc', w_s, vc) + b_s.T[None, None, :, :, None]
    return u * mixed.reshape(bn, n, A_WIDTH)


def even_mixer(h, p, rope, ctx):
    bn, n, _ = h.shape
    z = h @ p['w_in']
    u, v, q, k, vv = jnp.split(z, EVEN_SPLIT, axis=-1)
    a_out = chunk_gmlp(u, v, p['g_vnorm'], p['w_s'], p['b_s'])
    q = rms_norm(q.reshape(bn, n, B_HEADS, B_HEAD_DIM), p['g_q'])
    k = rms_norm(k.reshape(bn, n, B_KV_HEADS, B_HEAD_DIM), p['g_k'])
    vv = vv.reshape(bn, n, B_KV_HEADS, B_HEAD_DIM)
    sink = p['sink'].reshape(B_KV_HEADS, B_GROUP)
    if ctx is None:
        b_out = dense_attention(q.reshape(bn, n, B_KV_HEADS, B_GROUP, B_HEAD_DIM), k, vv, B_SCALE, sink)
        state = (k, vv)
    else:
        cos, sin = rope
        q = apply_rope(q, cos, sin)
        k = apply_rope(k, cos, sin)
        b_out = window_attention(q.reshape(bn, n, B_KV_HEADS, B_GROUP, B_HEAD_DIM), k, vv,
                                 ctx[0], ctx[1], B_SCALE, sink)
        state = None
    return jnp.concatenate([a_out, b_out], -1) @ p['w_o'], state


def mla_expand(ckv, kpe, w_ukv, g_k):
    bn, n, _ = ckv.shape
    kv = (ckv @ w_ukv).reshape(bn, n, C_HEADS, C_NOPE + C_V)
    k_nope, v = kv[..., :C_NOPE], kv[..., C_NOPE:]
    k = jnp.concatenate([k_nope, jnp.broadcast_to(kpe[:, :, None, :], (bn, n, C_HEADS, C_ROPE))], -1)
    return rms_norm(k, g_k), v


def rope_tail(x, cos, sin):
    return jnp.concatenate([x[..., :C_NOPE], apply_rope(x[..., C_NOPE:], cos, sin)], -1)


def odd_mixer(h, p, rope, ctx):
    bn, n, _ = h.shape
    z = h @ p['w_in']
    cq, ckv, kpe = jnp.split(z, (C_Q_LORA, C_Q_LORA + C_KV_LORA), axis=-1)
    q = (rms_norm(cq, p['g_qa']) @ p['w_uq']).reshape(bn, n, C_HEADS, C_QK)
    q = rms_norm(q, p['g_q'])
    ckv = rms_norm(ckv, p['g_kva'])
    k, v = mla_expand(ckv, kpe, p['w_ukv'], p['g_k'])
    if ctx is None:
        out = dense_attention(q[:, :, :, None, :], k, v, C_SCALE, None)
        state = (ckv, kpe)
    else:
        cos, sin = rope
        q = rope_tail(q, cos, sin)
        k = rope_tail(k, cos, sin)
        kc, vc = mla_expand(ctx[0], ctx[1], p['w_ukv'], p['g_k'])
        out = dense_attention(q[:, :, :, None, :], jnp.concatenate([kc, k], 1),
                              jnp.concatenate([vc, v], 1), C_SCALE, None)
        state = None
    return out @ p['w_o'], state


def hier_moe(h, p):
    bn, n, d = h.shape
    t = h.reshape(-1, d)
    lg = (t @ p['w_rg']).astype(f32) + p['b_rg'].astype(f32)
    pg = jax.nn.softmax(lg, -1)
    _, gsel = lax.top_k(lg, 1)
    gmask = jax.nn.one_hot(gsel[:, 0], N_GROUPS, dtype=f32)
    pg_sel = jnp.sum(pg * gmask, -1)
    le = ((t @ p['w_re']).astype(f32) + p['b_re'].astype(f32)).reshape(-1, N_GROUPS, EXPERTS_PER_GROUP)
    pe = jax.nn.softmax(jnp.einsum('tg,tge->te', gmask, le), -1)
    wk, ik = lax.top_k(pe, TOP_K)
    wk = wk / jnp.sum(wk, -1, keepdims=True)
    w_grp = jnp.sum(jax.nn.one_hot(ik, EXPERTS_PER_GROUP, dtype=f32) * wk[..., None], 1)
    gate = (gmask[:, :, None] * pg_sel[:, None, None] * w_grp[:, None, :]).reshape(-1, N_EXPERTS)
    h1 = jnp.einsum('td,edf->tef', t, p['w1'])
    h3 = jnp.einsum('td,edf->tef', t, p['w3'])
    hh = jax.nn.silu(h1) * h3 * gate[:, :, None].astype(t.dtype)
    return jnp.einsum('tef,efd->td', hh, p['w2']).reshape(bn, n, d)


def trunk_layer(x, cond, p, mixer, rope, ctx):
    sh1, sc1, g1, sh2, sc2, g2 = adaln(cond, p['w_ada'], p['b_ada'])
    h = rms_norm(x, p['g_norm1']) * (1 + sc1) + sh1
    y, state = mixer(h, p, rope, ctx)
    x = x + g1 * y
    h = rms_norm(x, p['g_norm2']) * (1 + sc2) + sh2
    x = x + g2 * hier_moe(h, p)
    return x, state


def setup_inputs(seed: int = 0) -> dict:
    key = jax.random.key(seed)
    ks = iter(jax.random.split(key, 64))

    def nrm(shape, s):
        return jax.random.normal(next(ks), shape, f32) * s

    def gain(n):
        return 1.0 + nrm((n,), 0.1)

    D = D_MODEL
    inp = {}
    inp['x_prompt'] = nrm((BATCH, SEQ, D), 1.0)
    inp['x_sample'] = nrm((DEC_BATCH, DEC_SEQ, D), 1.0)
    inp['cache_l0_k'] = nrm((DEC_BATCH, PAST_LEN, B_KV_HEADS, B_HEAD_DIM), 1.0)
    inp['cache_l0_v'] = nrm((DEC_BATCH, PAST_LEN, B_KV_HEADS, B_HEAD_DIM), 1.0)
    inp['cache_l1_ckv'] = nrm((DEC_BATCH, PAST_LEN, C_KV_LORA), 1.0)
    inp['cache_l1_kpe'] = nrm((DEC_BATCH, PAST_LEN, C_ROPE), 1.0)
    inp['c'] = nrm((DEC_BATCH, D), 1.0)
    inp['c_ctx'] = nrm((D,), 1.0)
    inp['l0_g_norm1'] = gain(D)
    inp['l0_g_norm2'] = gain(D)
    inp['l0_w_ada'] = nrm((D, 6 * D), 0.5 * D ** -0.5)
    inp['l0_b_ada'] = nrm((6 * D,), 0.05)
    inp['l0_w_in'] = nrm((D, EVEN_IN), D ** -0.5)
    inp['l0_g_vnorm'] = gain(A_WIDTH)
    inp['l0_w_s'] = nrm((A_GROUPS, CHUNK, CHUNK), CHUNK ** -0.5)
    inp['l0_b_s'] = 1.0 + nrm((A_GROUPS, CHUNK), 0.1)
    inp['l0_g_q'] = gain(B_HEAD_DIM)
    inp['l0_g_k'] = gain(B_HEAD_DIM)
    inp['l0_sink'] = nrm((B_HEADS,), 0.5)
    inp['l0_w_o'] = nrm((A_WIDTH + B_WIDTH, D), (A_WIDTH + B_WIDTH) ** -0.5)
    inp['l0_w_rg'] = nrm((D, N_GROUPS), D ** -0.5)
    inp['l0_b_rg'] = nrm((N_GROUPS,), 0.01)
    inp['l0_w_re'] = nrm((D, N_EXPERTS), D ** -0.5)
    inp['l0_b_re'] = nrm((N_EXPERTS,), 0.01)
    inp['l0_w1'] = nrm((N_EXPERTS, D, D_EXPERT), D ** -0.5)
    inp['l0_w3'] = nrm((N_EXPERTS, D, D_EXPERT), D ** -0.5)
    inp['l0_w2'] = nrm((N_EXPERTS, D_EXPERT, D), D_EXPERT ** -0.5)
    inp['l1_g_norm1'] = gain(D)
    inp['l1_g_norm2'] = gain(D)
    inp['l1_w_ada'] = nrm((D, 6 * D), 0.5 * D ** -0.5)
    inp['l1_b_ada'] = nrm((6 * D,), 0.05)
    inp['l1_w_in'] = nrm((D, ODD_IN), D ** -0.5)
    inp['l1_g_qa'] = gain(C_Q_LORA)
    inp['l1_w_uq'] = nrm((C_Q_LORA, C_HEADS * C_QK), C_Q_LORA ** -0.5)
    inp['l1_g_kva'] = gain(C_KV_LORA)
    inp['l1_w_ukv'] = nrm((C_KV_LORA, C_HEADS * (C_NOPE + C_V)), C_KV_LORA ** -0.5)
    inp['l1_g_q'] = gain(C_QK)
    inp['l1_g_k'] = gain(C_QK)
    inp['l1_w_o'] = nrm((C_HEADS * C_V, D), (C_HEADS * C_V) ** -0.5)
    inp['l1_w_rg'] = nrm((D, N_GROUPS), D ** -0.5)
    inp['l1_b_rg'] = nrm((N_GROUPS,), 0.01)
    inp['l1_w_re'] = nrm((D, N_EXPERTS), D ** -0.5)
    inp['l1_b_re'] = nrm((N_EXPERTS,), 0.01)
    inp['l1_w1'] = nrm((N_EXPERTS, D, D_EXPERT), D ** -0.5)
    inp['l1_w3'] = nrm((N_EXPERTS, D, D_EXPERT), D ** -0.5)
    inp['l1_w2'] = nrm((N_EXPERTS, D_EXPERT, D), D_EXPERT ** -0.5)
    return inp


def reference(x_prompt, x_sample, cache_l0_k, cache_l0_v, cache_l1_ckv, cache_l1_kpe, c, c_ctx,
              l0_g_norm1, l0_g_norm2, l0_w_ada, l0_b_ada, l0_w_in, l0_g_vnorm, l0_w_s, l0_b_s,
              l0_g_q, l0_g_k, l0_sink, l0_w_o, l0_w_rg, l0_b_rg, l0_w_re, l0_b_re, l0_w1, l0_w3, l0_w2,
              l1_g_norm1, l1_g_norm2, l1_w_ada, l1_b_ada, l1_w_in, l1_g_qa, l1_w_uq, l1_g_kva, l1_w_ukv,
              l1_g_q, l1_g_k, l1_w_o, l1_w_rg, l1_b_rg, l1_w_re, l1_b_re, l1_w1, l1_w3, l1_w2):
    p0 = dict(g_norm1=l0_g_norm1, g_norm2=l0_g_norm2, w_ada=l0_w_ada, b_ada=l0_b_ada, w_in=l0_w_in,
              g_vnorm=l0_g_vnorm, w_s=l0_w_s, b_s=l0_b_s, g_q=l0_g_q, g_k=l0_g_k, sink=l0_sink,
              w_o=l0_w_o, w_rg=l0_w_rg, b_rg=l0_b_rg, w_re=l0_w_re, b_re=l0_b_re,
              w1=l0_w1, w3=l0_w3, w2=l0_w2)
    p1 = dict(g_norm1=l1_g_norm1, g_norm2=l1_g_norm2, w_ada=l1_w_ada, b_ada=l1_b_ada, w_in=l1_w_in,
              g_qa=l1_g_qa, w_uq=l1_w_uq, g_kva=l1_g_kva, w_ukv=l1_w_ukv, g_q=l1_g_q, g_k=l1_g_k,
              w_o=l1_w_o, w_rg=l1_w_rg, b_rg=l1_b_rg, w_re=l1_w_re, b_re=l1_b_re,
              w1=l1_w1, w3=l1_w3, w2=l1_w2)
    params = [p0, p1]
    mixers = [even_mixer, odd_mixer]
    caches = [(cache_l0_k, cache_l0_v), (cache_l1_ckv, cache_l1_kpe)]
    n_lat = x_sample.shape[1]
    ropes = [axial_angles(n_lat, B_HEAD_DIM), axial_angles(n_lat, C_ROPE)]

    y_prompt = x_prompt
    states = []
    for l in range(DEPTH):
        y_prompt, st = trunk_layer(y_prompt, c_ctx, params[l], mixers[l % 2], None, None)
        states.append(st)

    y_sample = x_sample
    for l in range(DEPTH):
        y_sample, _ = trunk_layer(y_sample, c, params[l], mixers[l % 2], ropes[l % 2], caches[l])

    new_l0_k, new_l0_v = states[0]
    new_l1_ckv, new_l1_kpe = states[1]
    return (y_prompt, y_sample, new_l0_k, new_l0_v, new_l1_ckv, new_l1_kpe)
```

```python
import functools

import jax
import jax.numpy as jnp
from jax import lax
from jax.experimental import pallas as pl
from jax.experimental.pallas import tpu as pltpu

f32 = jnp.float32
bf16 = jnp.bfloat16
HIGHEST = lax.Precision.HIGHEST

D = 1024
BATCH, SEQ = 32, 256
DEC_BATCH, DEC_SEQ = 2, 1024
PAST = 512
T_CTX = BATCH * SEQ
T_LAT = DEC_BATCH * DEC_SEQ
T = T_CTX + T_LAT
GRID_W = 64
CHUNK = 128
WINDOW = 128
ROPE_THETA = 10000.0
EPS = 1e-6
NEG_INF = -1e30

A_WIDTH = 512
A_GROUPS = 4
B_HEADS, B_KV, B_GROUP, B_HD = 8, 2, 4, 64
B_SCALE = B_HD ** -0.5
EVEN_IN = 1792

C_HEADS, C_Q_LORA, C_KV_LORA, C_NOPE, C_ROPE, C_V = 16, 384, 256, 64, 32, 64
C_QK = C_NOPE + C_ROPE
C_SCALE = C_QK ** -0.5
ODD_IN_PAD = 768
SLOT = 128

N_GROUPS, EPG, N_EXPERTS, D_EXPERT = 4, 8, 32, 256

ROW_TILE = 512
MOE_TILE = 128
MOE_ROWS = 2 * T + N_EXPERTS * MOE_TILE
MOE_TILES = MOE_ROWS // MOE_TILE
VMEM_CAP = 56 * 1024 * 1024


def _cp(sem, vmem=VMEM_CAP):
    return pltpu.CompilerParams(dimension_semantics=sem, vmem_limit_bytes=vmem)


def _const_spec(shape):
    nd = len(shape)
    return pl.BlockSpec(shape, lambda *_: (0,) * nd)


def _sample_of_tile(i, tile):
    n_ctx = T_CTX // tile
    per_lat = DEC_SEQ // tile
    return jnp.where(i < n_ctx, 0, 1 + (i - n_ctx) // per_lat)


def _silu(x):
    return x * jax.nn.sigmoid(x)


def _rms_rows(x, g):
    return x * lax.rsqrt(jnp.mean(x * x, -1, keepdims=True) + EPS) * g


def _swap_pairs(x):
    lane = lax.broadcasted_iota(jnp.int32, x.shape, x.ndim - 1)
    nxt = pltpu.roll(x, x.shape[-1] - 1, x.ndim - 1)
    prv = pltpu.roll(x, 1, x.ndim - 1)
    return jnp.where((lane & 1) == 0, nxt, prv)


def _adaln_kernel(c_ref, w_ref, b_ref, o_ref):
    s = _silu(c_ref[...])
    o_ref[...] = jnp.dot(s, w_ref[...], precision=HIGHEST, preferred_element_type=f32) + b_ref[...]


def _adaln(cond8, w, b):
    n = w.shape[1]
    tn = 1536
    return pl.pallas_call(
        _adaln_kernel,
        grid=(n // tn,),
        in_specs=[_const_spec((8, D)), pl.BlockSpec((D, tn), lambda j: (0, j)),
                  pl.BlockSpec((1, tn), lambda j: (0, j))],
        out_specs=pl.BlockSpec((8, tn), lambda j: (0, j)),
        out_shape=jax.ShapeDtypeStruct((8, n), f32),
        compiler_params=_cp(("arbitrary",)),
        name="adaln",
    )(cond8, w, b.reshape(1, n))


def _mod_rows(m8):
    m = m8[:3].reshape(3, 6, D)
    return jnp.pad(m, ((0, 0), (0, 2), (0, 0))).reshape(24, D)


N_CTX_TILES = T_CTX // ROW_TILE


def _token_specs(width):
    return [pl.BlockSpec((ROW_TILE, width), lambda i: (jnp.minimum(i, N_CTX_TILES - 1), 0)),
            pl.BlockSpec((ROW_TILE, width), lambda i: (jnp.maximum(i - N_CTX_TILES, 0), 0))]


def _token_rows(xc_ref, xl_ref):
    return jnp.where(pl.program_id(0) < N_CTX_TILES, xc_ref[...], xl_ref[...])


def _proj_kernel(*refs, residual):
    if residual:
        xc_ref, xl_ref, d_ref, gn_ref, mod_ref, w_ref, z_ref, xo_ref = refs
        x = _token_rows(xc_ref, xl_ref) + mod_ref[5:6, :] * d_ref[...]
        xo_ref[...] = x
    else:
        xc_ref, xl_ref, gn_ref, mod_ref, w_ref, z_ref = refs
        x = _token_rows(xc_ref, xl_ref)
    h = _rms_rows(x, gn_ref[...]) * (1.0 + mod_ref[1:2, :]) + mod_ref[0:1, :]
    z_ref[...] = jnp.dot(h.astype(bf16), w_ref[...], preferred_element_type=f32)


def _proj(xc, xl, delta, g_norm, mods, w_bf16):
    n = w_bf16.shape[1]
    residual = delta is not None
    row = pl.BlockSpec((ROW_TILE, D), lambda i: (i, 0))
    in_specs = _token_specs(D) + ([row] if residual else []) + [
        _const_spec((1, D)),
        pl.BlockSpec((8, D), lambda i: (_sample_of_tile(i, ROW_TILE), 0)),
        _const_spec((D, n))]
    out_specs = [pl.BlockSpec((ROW_TILE, n), lambda i: (i, 0))]
    out_shape = [jax.ShapeDtypeStruct((T, n), f32)]
    if residual:
        out_specs.append(row)
        out_shape.append(jax.ShapeDtypeStruct((T, D), f32))
    args = (xc, xl) + ((delta,) if residual else ()) + (g_norm.reshape(1, D), mods, w_bf16)
    out = pl.pallas_call(
        functools.partial(_proj_kernel, residual=residual),
        grid=(T // ROW_TILE,), in_specs=in_specs, out_specs=out_specs, out_shape=out_shape,
        compiler_params=_cp(("parallel",)), name="proj_res" if residual else "proj",
    )(*args)
    return out if residual else out[0]


def _l0_kernel(*refs, n, latent):
    if latent:
        (sink_ref, z_ref, x_ref, mod_ref, gvn_ref, ws_ref, bsb_ref, gq_ref, gk_ref, bdq_ref, bdk_ref,
         wo_ref, cos_ref, sin_ref, kc_ref, vc_ref, xo_ref, cat, qs, ks, vs) = refs
        key_off = WINDOW
    else:
        (sink_ref, z_ref, x_ref, mod_ref, gvn_ref, ws_ref, bsb_ref, gq_ref, gk_ref, bdq_ref, bdk_ref,
         wo_ref, xo_ref, ko_ref, vo_ref, cat, qs, ks, vs) = refs
        key_off = 0
    n_chunks = n // CHUNK

    if latent:
        zpad = jnp.zeros((WINDOW, 2 * B_HD), bf16)
        for r0 in (0, WINDOW + n):
            ks[r0:r0 + WINDOW, :] = zpad
            vs[r0:r0 + WINDOW, :] = zpad

    def head_norm(v, bd_ref, g):
        sq = v * v
        hi = sq.astype(bf16)
        lo = (sq - hi.astype(f32)).astype(bf16)
        ss = (jnp.dot(hi, bd_ref[...], preferred_element_type=f32)
              + jnp.dot(lo, bd_ref[...], preferred_element_type=f32))
        return v * lax.rsqrt(ss * (1.0 / B_HD) + EPS) * g

    def phase1(c, carry):
        r = pl.ds(pl.multiple_of(c * CHUNK, CHUNK), CHUNK)
        u = jax.nn.gelu(z_ref[r, 0:A_WIDTH])
        v = jax.nn.gelu(z_ref[r, A_WIDTH:2 * A_WIDTH])
        mu = jnp.mean(v, -1, keepdims=True)
        var = jnp.mean(jnp.square(v - mu), -1, keepdims=True)
        vn = ((v - mu) * lax.rsqrt(var + EPS) * gvn_ref[...]).astype(bf16)
        for g in range(A_GROUPS):
            cs = slice(g * CHUNK, (g + 1) * CHUNK)
            mixed = jnp.dot(ws_ref[g], vn[:, cs], preferred_element_type=f32) + bsb_ref[g]
            cat[r, cs] = (u[:, cs] * mixed).astype(bf16)
        q = head_norm(z_ref[r, 1024:1536], bdq_ref, gq_ref[...])
        k = head_norm(z_ref[r, 1536:1664], bdk_ref, gk_ref[...])
        vv = z_ref[r, 1664:1792]
        if latent:
            cs_, sn_ = cos_ref[r, :], sin_ref[r, :]
            k = k * cs_ + _swap_pairs(k) * sn_
            q = jnp.concatenate(
                [q[:, j * 128:(j + 1) * 128] * cs_ + _swap_pairs(q[:, j * 128:(j + 1) * 128]) * sn_
                 for j in range(4)], axis=1)
        else:
            ko_ref[r, :] = k
            vo_ref[r, :] = vv
        qs[r, :] = (q * B_SCALE).astype(bf16)
        kr = pl.ds(pl.multiple_of(c * CHUNK + key_off, CHUNK), CHUNK)
        ks[kr, :] = k.astype(bf16)
        vs[kr, :] = vv.astype(bf16)
        return carry

    lax.fori_loop(0, n_chunks, phase1, 0)

    def phase2(c, carry):
        start = pl.multiple_of(c * CHUNK, CHUNK)
        r = pl.ds(start, CHUNK)
        if latent:
            span = CHUNK + 2 * WINDOW
            kr = pl.ds(start, span)
            qi = lax.broadcasted_iota(jnp.int32, (B_GROUP * CHUNK, span), 0) & (CHUNK - 1)
            kj = lax.broadcasted_iota(jnp.int32, (B_GROUP * CHUNK, span), 1)
            kpos = start - WINDOW + kj
            ok = (jnp.abs(kj - WINDOW - qi) <= WINDOW) & (kpos >= 0) & (kpos < n)
        else:
            kr = pl.ds(0, n)
        for kvh in range(B_KV):
            hs = slice(kvh * B_HD, (kvh + 1) * B_HD)
            qst = jnp.concatenate(
                [qs[r, (kvh * B_GROUP + g) * B_HD:(kvh * B_GROUP + g + 1) * B_HD] for g in range(B_GROUP)], axis=0)
            sk = jnp.concatenate(
                [jnp.full((CHUNK, 1), sink_ref[kvh * B_GROUP + g], f32) for g in range(B_GROUP)], axis=0)
            s_loc = lax.dot_general(qst, ks[kr, hs], (((1,), (1,)), ((), ())), preferred_element_type=f32)
            if latent:
                s_loc = jnp.where(ok, s_loc, NEG_INF)
                s_ctx = lax.dot_general(qst, kc_ref[:, hs], (((1,), (1,)), ((), ())),
                                        preferred_element_type=f32)
                m = jnp.maximum(jnp.maximum(jnp.max(s_loc, -1, keepdims=True),
                                            jnp.max(s_ctx, -1, keepdims=True)), sk)
                e_loc = jnp.exp(s_loc - m)
                e_ctx = jnp.exp(s_ctx - m)
                den = jnp.sum(e_loc, -1, keepdims=True) + jnp.sum(e_ctx, -1, keepdims=True) + jnp.exp(sk - m)
                o = (jnp.dot(e_loc.astype(bf16), vs[kr, hs], preferred_element_type=f32)
                     + jnp.dot(e_ctx.astype(bf16), vc_ref[:, hs], preferred_element_type=f32))
            else:
                m = jnp.maximum(jnp.max(s_loc, -1, keepdims=True), sk)
                e_loc = jnp.exp(s_loc - m)
                den = jnp.sum(e_loc, -1, keepdims=True) + jnp.exp(sk - m)
                o = jnp.dot(e_loc.astype(bf16), vs[kr, hs], preferred_element_type=f32)
            o = o / den
            for g in range(B_GROUP):
                h = kvh * B_GROUP + g
                cat[r, A_WIDTH + h * B_HD:A_WIDTH + (h + 1) * B_HD] = o[g * CHUNK:(g + 1) * CHUNK].astype(bf16)
        y = jnp.dot(cat[r, :], wo_ref[...], preferred_element_type=f32)
        xo_ref[r, :] = x_ref[r, :] + mod_ref[2:3, :] * y
        return carry

    lax.fori_loop(0, n_chunks, phase2, 0)


def _rope_tables(n, rot_dim, lanes, lane0):
    rows_count = n // GRID_W
    rows = jnp.repeat(jnp.arange(rows_count), GRID_W).astype(f32)
    cols = jnp.tile(jnp.arange(GRID_W), rows_count).astype(f32)
    d_axis = rot_dim // 2
    inv = ROPE_THETA ** (-jnp.arange(0, d_axis, 2, dtype=f32) / d_axis)
    ang = jnp.concatenate([rows[:, None] * inv, cols[:, None] * inv], -1)
    cos = jnp.repeat(jnp.cos(ang), 2, axis=1)
    sin = jnp.repeat(jnp.sin(ang), 2, axis=1) * jnp.tile(jnp.array([-1.0, 1.0], f32), rot_dim // 2)
    c = jnp.ones((n, lanes), f32).at[:, lane0:lane0 + rot_dim].set(cos)
    s = jnp.zeros((n, lanes), f32).at[:, lane0:lane0 + rot_dim].set(sin)
    return c, s


def _block_diag_ones(width, block):
    i = jnp.arange(width) // block
    return (i[:, None] == i[None, :]).astype(bf16)


def _l0_mixer(z, xc, xl, mods, p, cache_k, cache_v):
    gvn = p['g_vnorm'].reshape(1, A_WIDTH)
    ws = p['w_s'].astype(bf16)
    bsb = jnp.broadcast_to(p['b_s'][:, :, None], (A_GROUPS, CHUNK, CHUNK))
    gq = jnp.tile(p['g_q'], B_HEADS).reshape(1, B_HEADS * B_HD)
    gk = jnp.tile(p['g_k'], B_KV).reshape(1, B_KV * B_HD)
    bdq = _block_diag_ones(B_HEADS * B_HD, B_HD)
    bdk = _block_diag_ones(B_KV * B_HD, B_HD)
    wo = p['w_o'].astype(bf16)
    sink = p['sink']
    weights = (gvn, ws, bsb, gq, gk, bdq, bdk, wo)
    w_specs = [_const_spec(a.shape) for a in weights]
    smem = pl.BlockSpec(memory_space=pltpu.SMEM)

    def scratch(n, pad):
        return [pltpu.VMEM((n, D), bf16), pltpu.VMEM((n, B_HEADS * B_HD), bf16),
                pltpu.VMEM((n + pad, B_KV * B_HD), bf16), pltpu.VMEM((n + pad, B_KV * B_HD), bf16)]

    kv_shape = jax.ShapeDtypeStruct((T_CTX, B_KV * B_HD), f32)
    xo_ctx, k_new, v_new = pl.pallas_call(
        functools.partial(_l0_kernel, n=SEQ, latent=False),
        grid=(BATCH,),
        in_specs=[smem, pl.BlockSpec((SEQ, EVEN_IN), lambda b: (b, 0)), pl.BlockSpec((SEQ, D), lambda b: (b, 0)),
                  pl.BlockSpec((8, D), lambda b: (0, 0))] + w_specs,
        out_specs=[pl.BlockSpec((SEQ, D), lambda b: (b, 0)),
                   pl.BlockSpec((SEQ, B_KV * B_HD), lambda b: (b, 0)),
                   pl.BlockSpec((SEQ, B_KV * B_HD), lambda b: (b, 0))],
        out_shape=[jax.ShapeDtypeStruct((T_CTX, D), f32), kv_shape, kv_shape],
        scratch_shapes=scratch(SEQ, 0),
        compiler_params=_cp(("parallel",)), name="l0_mixer_ctx",
    )(sink, z, xc, mods, *weights)

    cos, sin = _rope_tables(DEC_SEQ, B_HD, 2 * B_HD, 0)
    cos = cos.at[:, B_HD:].set(cos[:, :B_HD])
    sin = sin.at[:, B_HD:].set(sin[:, :B_HD])
    kc = cache_k.reshape(DEC_BATCH, PAST, B_KV * B_HD).astype(bf16)
    vc = cache_v.reshape(DEC_BATCH, PAST, B_KV * B_HD).astype(bf16)
    off = T_CTX // DEC_SEQ
    xo_lat = pl.pallas_call(
        functools.partial(_l0_kernel, n=DEC_SEQ, latent=True),
        grid=(DEC_BATCH,),
        in_specs=[smem, pl.BlockSpec((DEC_SEQ, EVEN_IN), lambda b: (off + b, 0)),
                  pl.BlockSpec((DEC_SEQ, D), lambda b: (b, 0)),
                  pl.BlockSpec((8, D), lambda b: (1 + b, 0))] + w_specs + [
                  _const_spec(cos.shape), _const_spec(sin.shape),
                  pl.BlockSpec((None, PAST, B_KV * B_HD), lambda b: (b, 0, 0)),
                  pl.BlockSpec((None, PAST, B_KV * B_HD), lambda b: (b, 0, 0))],
        out_specs=pl.BlockSpec((DEC_SEQ, D), lambda b: (b, 0)),
        out_shape=jax.ShapeDtypeStruct((T_LAT, D), f32),
        scratch_shapes=scratch(DEC_SEQ, 2 * WINDOW),
        compiler_params=_cp(("parallel",)), name="l0_mixer_lat",
    )(sink, z, xl, mods, *weights, cos, sin, kc, vc)
    return xo_ctx, xo_lat, k_new, v_new


def _l1_kernel(*refs, n, latent):
    if latent:
        (z_ref, x_ref, mod_ref, gqa_ref, wuq_ref, gq_ref, gkva_ref, wuk_ref, wuv_ref, gk_ref, wo_ref,
         cos_ref, sin_ref, cckv_ref, ckpe_ref, xo_ref, cat, qs, ks, vs) = refs
        n_ctx = PAST
    else:
        (z_ref, x_ref, mod_ref, gqa_ref, wuq_ref, gq_ref, gkva_ref, wuk_ref, wuv_ref, gk_ref, wo_ref,
         xo_ref, ckvo_ref, cat, qs, ks, vs) = refs
        n_ctx = 0
    n_chunks = n // CHUNK

    def slot_norm(v, g):
        return v * lax.rsqrt(jnp.sum(v * v, -1, keepdims=True) * (1.0 / C_QK) + EPS) * g

    def expand_keys(ckv_n, kslot, key_rows, rope):
        cb = ckv_n.astype(bf16)
        kn = jnp.dot(cb, wuk_ref[...], preferred_element_type=f32)
        for h in range(C_HEADS):
            hs = slice(h * SLOT, (h + 1) * SLOT)
            kh = slot_norm(kn[:, hs] + kslot, gk_ref[...])
            if rope is not None:
                kh = kh * rope[0] + _swap_pairs(kh) * rope[1]
            ks[key_rows, hs] = kh.astype(bf16)
        vs[key_rows, :] = jnp.dot(cb, wuv_ref[...], preferred_element_type=f32).astype(bf16)

    if latent:
        def ctx_keys(c, carry):
            r = pl.ds(pl.multiple_of(c * CHUNK, CHUNK), CHUNK)
            expand_keys(cckv_ref[r, :], ckpe_ref[r, :], r, None)
            return carry
        lax.fori_loop(0, PAST // CHUNK, ctx_keys, 0)

    def phase1(c, carry):
        r = pl.ds(pl.multiple_of(c * CHUNK, CHUNK), CHUNK)
        rope = (cos_ref[r, :], sin_ref[r, :]) if latent else None
        qa = _rms_rows(z_ref[r, 0:C_Q_LORA], gqa_ref[...]).astype(bf16)
        q = jnp.dot(qa, wuq_ref[...], preferred_element_type=f32)
        for h in range(C_HEADS):
            hs = slice(h * SLOT, (h + 1) * SLOT)
            qh = slot_norm(q[:, hs], gq_ref[...])
            if latent:
                qh = qh * rope[0] + _swap_pairs(qh) * rope[1]
            qs[r, hs] = (qh * C_SCALE).astype(bf16)
        ckv_n = _rms_rows(z_ref[r, C_Q_LORA:C_Q_LORA + C_KV_LORA], gkva_ref[...])
        if not latent:
            ckvo_ref[r, :] = ckv_n
        kr = pl.ds(pl.multiple_of(c * CHUNK + n_ctx, CHUNK), CHUNK)
        expand_keys(ckv_n, z_ref[r, C_Q_LORA + C_KV_LORA:ODD_IN_PAD], kr, rope)
        return carry

    lax.fori_loop(0, n_chunks, phase1, 0)

    def phase2(c, carry):
        r = pl.ds(pl.multiple_of(c * CHUNK, CHUNK), CHUNK)
        lane = lax.broadcasted_iota(jnp.int32, (CHUNK, 2 * C_V), 1)
        for pair in range(C_HEADS // 2):
            outs = []
            for h in (2 * pair, 2 * pair + 1):
                hs = slice(h * SLOT, (h + 1) * SLOT)
                s = lax.dot_general(qs[r, hs], ks[:, hs], (((1,), (1,)), ((), ())), preferred_element_type=f32)
                e = jnp.exp(s - jnp.max(s, -1, keepdims=True))
                den = jnp.sum(e, -1, keepdims=True)
                o = jnp.dot(e.astype(bf16), vs[:, pair * 2 * C_V:(pair + 1) * 2 * C_V], preferred_element_type=f32)
                outs.append(o / den)
            cat[r, pair * 2 * C_V:(pair + 1) * 2 * C_V] = jnp.where(lane < C_V, outs[0], outs[1]).astype(bf16)
        y = jnp.dot(cat[r, :], wo_ref[...], preferred_element_type=f32)
        xo_ref[r, :] = x_ref[r, :] + mod_ref[2:3, :] * y
        return carry

    lax.fori_loop(0, n_chunks, phase2, 0)


def _slot_cols(w, heads, width, lo, hi, lane0):
    k = w.shape[0]
    w3 = w.reshape(k, heads, width)[:, :, lo:hi]
    out = jnp.zeros((k, heads, SLOT), w.dtype).at[:, :, lane0:lane0 + (hi - lo)].set(w3)
    return out.reshape(k, heads * SLOT)


def _l1_weights(p):
    wuq = _slot_cols(p['w_uq'], C_HEADS, C_QK, 0, C_QK, 0).astype(bf16)
    wuk = _slot_cols(p['w_ukv'], C_HEADS, C_NOPE + C_V, 0, C_NOPE, 0).astype(bf16)
    wuv = p['w_ukv'].reshape(C_KV_LORA, C_HEADS, C_NOPE + C_V)[:, :, C_NOPE:].reshape(
        C_KV_LORA, C_HEADS * C_V).astype(bf16)
    gq = jnp.zeros((1, SLOT), f32).at[0, :C_QK].set(p['g_q'])
    gk = jnp.zeros((1, SLOT), f32).at[0, :C_QK].set(p['g_k'])
    return (p['g_qa'].reshape(1, C_Q_LORA), wuq, gq, p['g_kva'].reshape(1, C_KV_LORA), wuk, wuv, gk,
            p['w_o'].astype(bf16))


def _l1_mixer(z, x, mods, p, cache_ckv, cache_kpe):
    weights = _l1_weights(p)
    w_specs = [_const_spec(a.shape) for a in weights]

    def scratch(n, n_keys):
        return [pltpu.VMEM((n, D), bf16), pltpu.VMEM((n, C_HEADS * SLOT), bf16),
                pltpu.VMEM((n_keys, C_HEADS * SLOT), bf16), pltpu.VMEM((n_keys, C_HEADS * C_V), bf16)]

    xo_ctx, ckv_new = pl.pallas_call(
        functools.partial(_l1_kernel, n=SEQ, latent=False),
        grid=(BATCH,),
        in_specs=[pl.BlockSpec((SEQ, ODD_IN_PAD), lambda b: (b, 0)), pl.BlockSpec((SEQ, D), lambda b: (b, 0)),
                  pl.BlockSpec((8, D), lambda b: (0, 0))] + w_specs,
        out_specs=[pl.BlockSpec((SEQ, D), lambda b: (b, 0)), pl.BlockSpec((SEQ, C_KV_LORA), lambda b: (b, 0))],
        out_shape=[jax.ShapeDtypeStruct((T_CTX, D), f32), jax.ShapeDtypeStruct((T_CTX, C_KV_LORA), f32)],
        scratch_shapes=scratch(SEQ, SEQ),
        compiler_params=_cp(("parallel",)), name="l1_mixer_ctx",
    )(z, x, mods, *weights)

    cos, sin = _rope_tables(DEC_SEQ, C_ROPE, SLOT, C_NOPE)
    ckpe = jnp.zeros((DEC_BATCH, PAST, SLOT), f32).at[:, :, C_NOPE:C_QK].set(cache_kpe)
    off = T_CTX // DEC_SEQ
    xo_lat = pl.pallas_call(
        functools.partial(_l1_kernel, n=DEC_SEQ, latent=True),
        grid=(DEC_BATCH,),
        in_specs=[pl.BlockSpec((DEC_SEQ, ODD_IN_PAD), lambda b: (off + b, 0)),
                  pl.BlockSpec((DEC_SEQ, D), lambda b: (off + b, 0)),
                  pl.BlockSpec((8, D), lambda b: (1 + b, 0))] + w_specs + [
                  _const_spec(cos.shape), _const_spec(sin.shape),
                  pl.BlockSpec((None, PAST, C_KV_LORA), lambda b: (b, 0, 0)),
                  pl.BlockSpec((None, PAST, SLOT), lambda b: (b, 0, 0))],
        out_specs=pl.BlockSpec((DEC_SEQ, D), lambda b: (b, 0)),
        out_shape=jax.ShapeDtypeStruct((T_LAT, D), f32),
        scratch_shapes=scratch(DEC_SEQ, PAST + DEC_SEQ),
        compiler_params=_cp(("parallel",)), name="l1_mixer_lat",
    )(z, x, mods, *weights, cos, sin, cache_ckv, ckpe)
    return xo_ctx, xo_lat, ckv_new


def _router_kernel(xc_ref, xl_ref, gn_ref, mod_ref, wr_ref, br_ref, h_ref, route_ref):
    h = _rms_rows(_token_rows(xc_ref, xl_ref), gn_ref[...]) * (1.0 + mod_ref[4:5, :]) + mod_ref[3:4, :]
    h_ref[...] = h
    logits = jnp.dot(h, wr_ref[...], precision=HIGHEST, preferred_element_type=f32) + br_ref[...]
    lane_i = lax.broadcasted_iota(jnp.int32, logits.shape, 1)
    lane = lane_i.astype(f32)
    big = 1e6
    is_g = (lane_i >= N_EXPERTS) & (lane_i < N_EXPERTS + N_GROUPS)
    lg = jnp.where(is_g, logits, -jnp.inf)
    mg = jnp.max(lg, -1, keepdims=True)
    gsel = jnp.min(jnp.where(lg == mg, lane, big), -1, keepdims=True) - N_EXPERTS
    pg_sel = 1.0 / jnp.sum(jnp.where(is_g, jnp.exp(lg - mg), 0.0), -1, keepdims=True)
    in_grp = (lane_i < N_EXPERTS) & ((lane_i >> 3).astype(f32) == gsel)
    le = jnp.where(in_grp, logits, -jnp.inf)
    m1 = jnp.max(le, -1, keepdims=True)
    i1 = jnp.min(jnp.where(le == m1, lane, big), -1, keepdims=True)
    le2 = jnp.where(lane == i1, -jnp.inf, le)
    m2 = jnp.max(le2, -1, keepdims=True)
    i2 = jnp.min(jnp.where(le2 == m2, lane, big), -1, keepdims=True)
    e2 = jnp.exp(m2 - m1)
    w1 = pg_sel / (1.0 + e2)
    w2 = pg_sel * e2 / (1.0 + e2)
    route_ref[...] = jnp.where(lane_i == 0, i1,
                               jnp.where(lane_i == 1, i2,
                                         jnp.where(lane_i == 2, w1, jnp.where(lane_i == 3, w2, 0.0))))


def _router(xc, xl, g_norm, mods, p):
    wr = jnp.zeros((D, 128), f32).at[:, :N_EXPERTS].set(p['w_re']).at[
        :, N_EXPERTS:N_EXPERTS + N_GROUPS].set(p['w_rg'])
    br = jnp.zeros((1, 128), f32).at[0, :N_EXPERTS].set(p['b_re']).at[
        0, N_EXPERTS:N_EXPERTS + N_GROUPS].set(p['b_rg'])
    row = pl.BlockSpec((ROW_TILE, D), lambda i: (i, 0))
    return pl.pallas_call(
        _router_kernel,
        grid=(T // ROW_TILE,),
        in_specs=_token_specs(D) + [
                  _const_spec((1, D)),
                  pl.BlockSpec((8, D), lambda i: (_sample_of_tile(i, ROW_TILE), 0)),
                  _const_spec((D, 128)), _const_spec((1, 128))],
        out_specs=[row, pl.BlockSpec((ROW_TILE, 128), lambda i: (i, 0))],
        out_shape=[jax.ShapeDtypeStruct((T, D), f32), jax.ShapeDtypeStruct((T, 128), f32)],
        compiler_params=_cp(("parallel",)), name="router",
    )(xc, xl, g_norm.reshape(1, D), mods, wr, br)


def _route_plan(route):
    ids = route[:, 0:2].astype(jnp.int32).reshape(-1)
    gates = route[:, 2:4].reshape(-1)
    order = jnp.argsort(ids, stable=True).astype(jnp.int32)
    counts = jnp.sum((ids[:, None] == jnp.arange(N_EXPERTS)[None, :]).astype(jnp.int32), 0)
    tiles = (counts + MOE_TILE - 1) // MOE_TILE
    tile_end = jnp.cumsum(tiles)
    n_used = tile_end[-1]
    start = jnp.cumsum(counts) - counts
    pad_start = (tile_end - tiles) * MOE_TILE
    tile_expert = jnp.minimum(jnp.searchsorted(tile_end, jnp.arange(MOE_TILES), side='right'),
                              N_EXPERTS - 1).astype(jnp.int32)
    slot = jnp.arange(MOE_ROWS)
    e = tile_expert[slot // MOE_TILE]
    j = slot - pad_start[e]
    valid = (j < counts[e]) & (slot // MOE_TILE < n_used)
    a = order[jnp.clip(start[e] + j, 0, 2 * T - 1)]
    tok = a // 2
    tok_gather = jnp.where(valid, tok, 0).astype(jnp.int32)
    tok_scatter = jnp.where(valid, tok, T).astype(jnp.int32)
    gate = jnp.where(valid, gates[a], 0.0).reshape(MOE_ROWS, 1)
    return tile_expert, n_used.reshape(1).astype(jnp.int32), tok_gather, tok_scatter, gate


def _moe_up_kernel(te_ref, nu_ref, tok_ref, h_hbm, w13_ref, gate_ref, o_ref, xs, gbuf, sem):
    i = pl.program_id(0)

    @pl.when(i == 0)
    def _():
        cp = pltpu.make_async_copy(h_hbm, xs, sem)
        cp.start()
        cp.wait()

    @pl.when(i < nu_ref[0])
    def _():
        base = i * MOE_TILE
        for r in range(MOE_TILE):
            gbuf[pl.ds(r, 1), :] = xs[pl.ds(tok_ref[base + r], 1), :]
        h13 = jnp.dot(gbuf[...].astype(bf16), w13_ref[0], preferred_element_type=f32)
        hh = _silu(h13[:, :D_EXPERT]) * h13[:, D_EXPERT:] * gate_ref[...]
        o_ref[...] = hh.astype(bf16)

    @pl.when(i >= nu_ref[0])
    def _():
        o_ref[...] = jnp.zeros_like(o_ref)


def _moe_down_kernel(te_ref, nu_ref, tok_ref, hh_ref, w2_ref, o_hbm, acc, ybuf, sem):
    i = pl.program_id(0)

    @pl.when(i == 0)
    def _():
        def zero(c, carry):
            acc[pl.ds(pl.multiple_of(c * 256, 256), 256), :] = jnp.zeros((256, D), f32)
            return carry
        lax.fori_loop(0, T // 256, zero, 0)
        acc[T:T + 8, :] = jnp.zeros((8, D), f32)

    @pl.when(i < nu_ref[0])
    def _():
        ybuf[...] = jnp.dot(hh_ref[...], w2_ref[0], preferred_element_type=f32)
        base = i * MOE_TILE
        for r in range(MOE_TILE):
            t = tok_ref[base + r]
            acc[pl.ds(t, 1), :] = acc[pl.ds(t, 1), :] + ybuf[pl.ds(r, 1), :]

    @pl.when(i == MOE_TILES - 1)
    def _():
        cp = pltpu.make_async_copy(acc.at[pl.ds(0, T), :], o_hbm, sem)
        cp.start()
        cp.wait()


def _moe(h, route, p):
    tile_expert, n_used, tok_g, tok_s, gate = _route_plan(route)
    w13 = jnp.concatenate([p['w1'], p['w3']], axis=2).astype(bf16)
    w2 = p['w2'].astype(bf16)
    hh = pl.pallas_call(
        _moe_up_kernel,
        grid_spec=pltpu.PrefetchScalarGridSpec(
            num_scalar_prefetch=3, grid=(MOE_TILES,),
            in_specs=[pl.BlockSpec(memory_space=pl.ANY),
                      pl.BlockSpec((1, D, 2 * D_EXPERT), lambda i, te, nu, tk: (te[i], 0, 0)),
                      pl.BlockSpec((MOE_TILE, 1), lambda i, te, nu, tk: (i, 0))],
            out_specs=pl.BlockSpec((MOE_TILE, D_EXPERT), lambda i, te, nu, tk: (i, 0)),
            scratch_shapes=[pltpu.VMEM((T, D), f32), pltpu.VMEM((MOE_TILE, D), f32),
                            pltpu.SemaphoreType.DMA(())]),
        out_shape=jax.ShapeDtypeStruct((MOE_ROWS, D_EXPERT), bf16),
        compiler_params=_cp(("arbitrary",)), name="moe_up",
    )(tile_expert, n_used, tok_g, h, w13, gate)
    return pl.pallas_call(
        _moe_down_kernel,
        grid_spec=pltpu.PrefetchScalarGridSpec(
            num_scalar_prefetch=3, grid=(MOE_TILES,),
            in_specs=[pl.BlockSpec((MOE_TILE, D_EXPERT), lambda i, te, nu, tk: (i, 0)),
                      pl.BlockSpec((1, D_EXPERT, D), lambda i, te, nu, tk: (te[i], 0, 0))],
            out_specs=pl.BlockSpec(memory_space=pl.ANY),
            scratch_shapes=[pltpu.VMEM((T + 8, D), f32), pltpu.VMEM((MOE_TILE, D), f32),
                            pltpu.SemaphoreType.DMA(())]),
        out_shape=jax.ShapeDtypeStruct((T, D), f32),
        compiler_params=_cp(("arbitrary",)), name="moe_down",
    )(tile_expert, n_used, tok_s, hh, w2)


def _final_kernel(x_ref, d_ref, mod_ref, o_ref):
    o_ref[...] = x_ref[...] + mod_ref[5:6, :] * d_ref[...]


def _final(x, delta, mods, row0, seq):
    n_rows = x.shape[0]
    t0 = row0 // ROW_TILE
    return pl.pallas_call(
        _final_kernel,
        grid=(n_rows // ROW_TILE,),
        in_specs=[pl.BlockSpec((ROW_TILE, D), lambda i: (i, 0)),
                  pl.BlockSpec((ROW_TILE, D), lambda i: (t0 + i, 0)),
                  pl.BlockSpec((8, D), lambda i: (_sample_of_tile(t0 + i, ROW_TILE), 0))],
        out_specs=pl.BlockSpec((ROW_TILE, D), lambda i: (i, 0)),
        out_shape=jax.ShapeDtypeStruct((n_rows, D), f32),
        compiler_params=_cp(("parallel",)), name="final",
    )(x, delta, mods).reshape(n_rows // seq, seq, D)


def kernel(x_prompt, x_sample, cache_l0_k, cache_l0_v, cache_l1_ckv, cache_l1_kpe, c, c_ctx, l0_g_norm1, l0_g_norm2, l0_w_ada, l0_b_ada, l0_w_in, l0_g_vnorm, l0_w_s, l0_b_s, l0_g_q, l0_g_k, l0_sink, l0_w_o, l0_w_rg, l0_b_rg, l0_w_re, l0_b_re, l0_w1, l0_w3, l0_w2, l1_g_norm1, l1_g_norm2, l1_w_ada, l1_b_ada, l1_w_in, l1_g_qa, l1_w_uq, l1_g_kva, l1_w_ukv, l1_g_q, l1_g_k, l1_w_o, l1_w_rg, l1_b_rg, l1_w_re, l1_b_re, l1_w1, l1_w3, l1_w2):
    p0 = dict(g_vnorm=l0_g_vnorm, w_s=l0_w_s, b_s=l0_b_s, g_q=l0_g_q, g_k=l0_g_k, sink=l0_sink, w_o=l0_w_o,
              w_rg=l0_w_rg, b_rg=l0_b_rg, w_re=l0_w_re, b_re=l0_b_re, w1=l0_w1, w3=l0_w3, w2=l0_w2)
    p1 = dict(g_qa=l1_g_qa, w_uq=l1_w_uq, g_kva=l1_g_kva, w_ukv=l1_w_ukv, g_q=l1_g_q, g_k=l1_g_k, w_o=l1_w_o,
              w_rg=l1_w_rg, b_rg=l1_b_rg, w_re=l1_w_re, b_re=l1_b_re, w1=l1_w1, w3=l1_w3, w2=l1_w2)

    cond8 = jnp.zeros((8, D), f32).at[0].set(c_ctx).at[1:1 + DEC_BATCH].set(c)
    mods0 = _mod_rows(_adaln(cond8, l0_w_ada, l0_b_ada))
    mods1 = _mod_rows(_adaln(cond8, l1_w_ada, l1_b_ada))

    xc0 = x_prompt.reshape(T_CTX, D)
    xl0 = x_sample.reshape(T_LAT, D)

    z0 = _proj(xc0, xl0, None, l0_g_norm1, mods0, l0_w_in.astype(bf16))
    xc0m, xl0m, k_new, v_new = _l0_mixer(z0, xc0, xl0, mods0, p0, cache_l0_k, cache_l0_v)
    h0, route0 = _router(xc0m, xl0m, l0_g_norm2, mods0, p0)
    moe0 = _moe(h0, route0, p0)

    w_in1 = jnp.zeros((D, ODD_IN_PAD), f32).at[:, :C_Q_LORA + C_KV_LORA].set(
        l1_w_in[:, :C_Q_LORA + C_KV_LORA]).at[
        :, C_Q_LORA + C_KV_LORA + C_NOPE:C_Q_LORA + C_KV_LORA + C_QK].set(l1_w_in[:, C_Q_LORA + C_KV_LORA:])
    mods01 = mods1.reshape(3, 8, D).at[:, 5].set(mods0.reshape(3, 8, D)[:, 5]).reshape(24, D)
    z1, x1 = _proj(xc0m, xl0m, moe0, l1_g_norm1, mods01, w_in1.astype(bf16))
    xc1m, xl1m, ckv_new = _l1_mixer(z1, x1, mods1, p1, cache_l1_ckv, cache_l1_kpe)
    h1, route1 = _router(xc1m, xl1m, l1_g_norm2, mods1, p1)
    moe1 = _moe(h1, route1, p1)

    y_prompt = _final(xc1m, moe1, mods1, 0, SEQ)
    y_sample = _final(xl1m, moe1, mods1, T_CTX, DEC_SEQ)
    kpe_new = z1[:T_CTX, C_Q_LORA + C_KV_LORA + C_NOPE:C_Q_LORA + C_KV_LORA + C_QK]
    return (y_prompt, y_sample,
            k_new.reshape(BATCH, SEQ, B_KV, B_HD), v_new.reshape(BATCH, SEQ, B_KV, B_HD),
            ckv_new.reshape(BATCH, SEQ, C_KV_LORA), kpe_new.reshape(BATCH, SEQ, C_ROPE))
```

```python
import functools

import jax
import jax.numpy as jnp
from jax import lax
from jax.experimental import pallas as pl
from jax.experimental.pallas import tpu as pltpu

f32 = jnp.float32
bf16 = jnp.bfloat16
HIGHEST = lax.Precision.HIGHEST

D = 1024
BATCH, SEQ = 32, 256
DEC_BATCH, DEC_SEQ = 2, 1024
PAST = 512
T_CTX = BATCH * SEQ
T_LAT = DEC_BATCH * DEC_SEQ
T = T_CTX + T_LAT
GRID_W = 64
CHUNK = 128
WINDOW = 128
ROPE_THETA = 10000.0
EPS = 1e-6
NEG_INF = -1e30

A_WIDTH = 512
A_GROUPS = 4
B_HEADS, B_KV, B_GROUP, B_HD = 8, 2, 4, 64
B_SCALE = B_HD ** -0.5
EVEN_IN = 1792

C_HEADS, C_Q_LORA, C_KV_LORA, C_NOPE, C_ROPE, C_V = 16, 384, 256, 64, 32, 64
C_QK = C_NOPE + C_ROPE
C_SCALE = C_QK ** -0.5
ODD_IN_PAD = 768
SLOT = 128

N_GROUPS, EPG, N_EXPERTS, D_EXPERT = 4, 8, 32, 256

ROW_TILE = 512
MOE_TILE = 128
MOE_ROWS = 2 * T + N_EXPERTS * MOE_TILE
MOE_TILES = MOE_ROWS // MOE_TILE
VMEM_CAP = 56 * 1024 * 1024


def _cp(sem, vmem=VMEM_CAP):
    return pltpu.CompilerParams(dimension_semantics=sem, vmem_limit_bytes=vmem)


def _const_spec(shape):
    nd = len(shape)
    return pl.BlockSpec(shape, lambda *_: (0,) * nd)


def _sample_of_tile(i, tile):
    n_ctx = T_CTX // tile
    per_lat = DEC_SEQ // tile
    return jnp.where(i < n_ctx, 0, 1 + (i - n_ctx) // per_lat)


def _silu(x):
    return x * jax.nn.sigmoid(x)


def _rms_rows(x, g):
    return x * lax.rsqrt(jnp.mean(x * x, -1, keepdims=True) + EPS) * g


def _swap_pairs(x):
    lane = lax.broadcasted_iota(jnp.int32, x.shape, x.ndim - 1)
    nxt = pltpu.roll(x, x.shape[-1] - 1, x.ndim - 1)
    prv = pltpu.roll(x, 1, x.ndim - 1)
    return jnp.where((lane & 1) == 0, nxt, prv)


def _adaln_kernel(c_ref, w_ref, b_ref, o_ref):
    s = _silu(c_ref[...])
    o_ref[...] = jnp.dot(s, w_ref[...], precision=HIGHEST, preferred_element_type=f32) + b_ref[...]


def _adaln(cond8, w, b):
    n = w.shape[1]
    tn = 1536
    return pl.pallas_call(
        _adaln_kernel,
        grid=(n // tn,),
        in_specs=[_const_spec((8, D)), pl.BlockSpec((D, tn), lambda j: (0, j)),
                  pl.BlockSpec((1, tn), lambda j: (0, j))],
        out_specs=pl.BlockSpec((8, tn), lambda j: (0, j)),
        out_shape=jax.ShapeDtypeStruct((8, n), f32),
        compiler_params=_cp(("arbitrary",)),
        name="adaln",
    )(cond8, w, b.reshape(1, n))


def _mod_rows(m8):
    m = m8[:3].reshape(3, 6, D)
    return jnp.pad(m, ((0, 0), (0, 2), (0, 0))).reshape(24, D)


N_CTX_TILES = T_CTX // ROW_TILE


def _token_specs(width):
    return [pl.BlockSpec((ROW_TILE, width), lambda i: (jnp.minimum(i, N_CTX_TILES - 1), 0)),
            pl.BlockSpec((ROW_TILE, width), lambda i: (jnp.maximum(i - N_CTX_TILES, 0), 0))]


def _token_rows(xc_ref, xl_ref):
    return jnp.where(pl.program_id(0) < N_CTX_TILES, xc_ref[...], xl_ref[...])


def _proj_kernel(*refs, residual):
    if residual:
        xc_ref, xl_ref, d_ref, gn_ref, mod_ref, w_ref, z_ref, xo_ref = refs
        x = _token_rows(xc_ref, xl_ref) + mod_ref[5:6, :] * d_ref[...]
        xo_ref[...] = x
    else:
        xc_ref, xl_ref, gn_ref, mod_ref, w_ref, z_ref = refs
        x = _token_rows(xc_ref, xl_ref)
    h = _rms_rows(x, gn_ref[...]) * (1.0 + mod_ref[1:2, :]) + mod_ref[0:1, :]
    z_ref[...] = jnp.dot(h.astype(bf16), w_ref[...], preferred_element_type=f32)


def _proj(xc, xl, delta, g_norm, mods, w_bf16):
    n = w_bf16.shape[1]
    residual = delta is not None
    row = pl.BlockSpec((ROW_TILE, D), lambda i: (i, 0))
    in_specs = _token_specs(D) + ([row] if residual else []) + [
        _const_spec((1, D)),
        pl.BlockSpec((8, D), lambda i: (_sample_of_tile(i, ROW_TILE), 0)),
        _const_spec((D, n))]
    out_specs = [pl.BlockSpec((ROW_TILE, n), lambda i: (i, 0))]
    out_shape = [jax.ShapeDtypeStruct((T, n), f32)]
    if residual:
        out_specs.append(row)
        out_shape.append(jax.ShapeDtypeStruct((T, D), f32))
    args = (xc, xl) + ((delta,) if residual else ()) + (g_norm.reshape(1, D), mods, w_bf16)
    out = pl.pallas_call(
        functools.partial(_proj_kernel, residual=residual),
        grid=(T // ROW_TILE,), in_specs=in_specs, out_specs=out_specs, out_shape=out_shape,
        compiler_params=_cp(("parallel",)), name="proj_res" if residual else "proj",
    )(*args)
    return out if residual else out[0]


def _l0_kernel(*refs, n, latent):
    if latent:
        (sink_ref, z_ref, x_ref, mod_ref, gvn_ref, ws_ref, bsb_ref, gq_ref, gk_ref, bdq_ref, bdk_ref,
         wo_ref, cos_ref, sin_ref, kc_ref, vc_ref, xo_ref, cat, qs, ks, vs) = refs
        key_off = WINDOW
    else:
        (sink_ref, z_ref, x_ref, mod_ref, gvn_ref, ws_ref, bsb_ref, gq_ref, gk_ref, bdq_ref, bdk_ref,
         wo_ref, xo_ref, ko_ref, vo_ref, cat, qs, ks, vs) = refs
        key_off = 0
    n_chunks = n // CHUNK

    if latent:
        zpad = jnp.zeros((WINDOW, 2 * B_HD), bf16)
        for r0 in (0, WINDOW + n):
            ks[r0:r0 + WINDOW, :] = zpad
            vs[r0:r0 + WINDOW, :] = zpad

    def head_norm(v, bd_ref, g):
        sq = v * v
        hi = sq.astype(bf16)
        lo = (sq - hi.astype(f32)).astype(bf16)
        ss = (jnp.dot(hi, bd_ref[...], preferred_element_type=f32)
              + jnp.dot(lo, bd_ref[...], preferred_element_type=f32))
        return v * lax.rsqrt(ss * (1.0 / B_HD) + EPS) * g

    def phase1(c, carry):
        r = pl.ds(pl.multiple_of(c * CHUNK, CHUNK), CHUNK)
        u = jax.nn.gelu(z_ref[r, 0:A_WIDTH])
        v = jax.nn.gelu(z_ref[r, A_WIDTH:2 * A_WIDTH])
        mu = jnp.mean(v, -1, keepdims=True)
        var = jnp.mean(jnp.square(v - mu), -1, keepdims=True)
        vn = ((v - mu) * lax.rsqrt(var + EPS) * gvn_ref[...]).astype(bf16)
        for g in range(A_GROUPS):
            cs = slice(g * CHUNK, (g + 1) * CHUNK)
            mixed = jnp.dot(ws_ref[g], vn[:, cs], preferred_element_type=f32) + bsb_ref[g]
            cat[r, cs] = (u[:, cs] * mixed).astype(bf16)
        q = head_norm(z_ref[r, 1024:1536], bdq_ref, gq_ref[...])
        k = head_norm(z_ref[r, 1536:1664], bdk_ref, gk_ref[...])
        vv = z_ref[r, 1664:1792]
        if latent:
            cs_, sn_ = cos_ref[r, :], sin_ref[r, :]
            k = k * cs_ + _swap_pairs(k) * sn_
            q = jnp.concatenate(
                [q[:, j * 128:(j + 1) * 128] * cs_ + _swap_pairs(q[:, j * 128:(j + 1) * 128]) * sn_
                 for j in range(4)], axis=1)
        else:
            ko_ref[r, :] = k
            vo_ref[r, :] = vv
        qs[r, :] = (q * B_SCALE).astype(bf16)
        kr = pl.ds(pl.multiple_of(c * CHUNK + key_off, CHUNK), CHUNK)
        ks[kr, :] = k.astype(bf16)
        vs[kr, :] = vv.astype(bf16)
        return carry

    lax.fori_loop(0, n_chunks, phase1, 0)

    def phase2(c, carry):
        start = pl.multiple_of(c * CHUNK, CHUNK)
        r = pl.ds(start, CHUNK)
        if latent:
            span = CHUNK + 2 * WINDOW
            kr = pl.ds(start, span)
            qi = lax.broadcasted_iota(jnp.int32, (B_GROUP * CHUNK, span), 0) & (CHUNK - 1)
            kj = lax.broadcasted_iota(jnp.int32, (B_GROUP * CHUNK, span), 1)
            kpos = start - WINDOW + kj
            ok = (jnp.abs(kj - WINDOW - qi) <= WINDOW) & (kpos >= 0) & (kpos < n)
        else:
            kr = pl.ds(0, n)
        for kvh in range(B_KV):
            hs = slice(kvh * B_HD, (kvh + 1) * B_HD)
            qst = jnp.concatenate(
                [qs[r, (kvh * B_GROUP + g) * B_HD:(kvh * B_GROUP + g + 1) * B_HD] for g in range(B_GROUP)], axis=0)
            sk = jnp.concatenate(
                [jnp.full((CHUNK, 1), sink_ref[kvh * B_GROUP + g], f32) for g in range(B_GROUP)], axis=0)
            s_loc = lax.dot_general(qst, ks[kr, hs], (((1,), (1,)), ((), ())), preferred_element_type=f32)
            if latent:
                s_loc = jnp.where(ok, s_loc, NEG_INF)
                s_ctx = lax.dot_general(qst, kc_ref[:, hs], (((1,), (1,)), ((), ())),
                                        preferred_element_type=f32)
                m = jnp.maximum(jnp.maximum(jnp.max(s_loc, -1, keepdims=True),
                                            jnp.max(s_ctx, -1, keepdims=True)), sk)
                e_loc = jnp.exp(s_loc - m)
                e_ctx = jnp.exp(s_ctx - m)
                den = jnp.sum(e_loc, -1, keepdims=True) + jnp.sum(e_ctx, -1, keepdims=True) + jnp.exp(sk - m)
                o = (jnp.dot(e_loc.astype(bf16), vs[kr, hs], preferred_element_type=f32)
                     + jnp.dot(e_ctx.astype(bf16), vc_ref[:, hs], preferred_element_type=f32))
            else:
                m = jnp.maximum(jnp.max(s_loc, -1, keepdims=True), sk)
                e_loc = jnp.exp(s_loc - m)
                den = jnp.sum(e_loc, -1, keepdims=True) + jnp.exp(sk - m)
                o = jnp.dot(e_loc.astype(bf16), vs[kr, hs], preferred_element_type=f32)
            o = o / den
            for g in range(B_GROUP):
                h = kvh * B_GROUP + g
                cat[r, A_WIDTH + h * B_HD:A_WIDTH + (h + 1) * B_HD] = o[g * CHUNK:(g + 1) * CHUNK].astype(bf16)
        y = jnp.dot(cat[r, :], wo_ref[...], preferred_element_type=f32)
        xo_ref[r, :] = x_ref[r, :] + mod_ref[2:3, :] * y
        return carry

    lax.fori_loop(0, n_chunks, phase2, 0)


def _rope_tables(n, rot_dim, lanes, lane0):
    rows_count = n // GRID_W
    rows = jnp.repeat(jnp.arange(rows_count), GRID_W).astype(f32)
    cols = jnp.tile(jnp.arange(GRID_W), rows_count).astype(f32)
    d_axis = rot_dim // 2
    inv = ROPE_THETA ** (-jnp.arange(0, d_axis, 2, dtype=f32) / d_axis)
    ang = jnp.concatenate([rows[:, None] * inv, cols[:, None] * inv], -1)
    cos = jnp.repeat(jnp.cos(ang), 2, axis=1)
    sin = jnp.repeat(jnp.sin(ang), 2, axis=1) * jnp.tile(jnp.array([-1.0, 1.0], f32), rot_dim // 2)
    c = jnp.ones((n, lanes), f32).at[:, lane0:lane0 + rot_dim].set(cos)
    s = jnp.zeros((n, lanes), f32).at[:, lane0:lane0 + rot_dim].set(sin)
    return c, s


def _block_diag_ones(width, block):
    i = jnp.arange(width) // block
    return (i[:, None] == i[None, :]).astype(bf16)


def _l0_mixer(z, xc, xl, mods, p, cache_k, cache_v):
    gvn = p['g_vnorm'].reshape(1, A_WIDTH)
    ws = p['w_s'].astype(bf16)
    bsb = jnp.broadcast_to(p['b_s'][:, :, None], (A_GROUPS, CHUNK, CHUNK))
    gq = jnp.tile(p['g_q'], B_HEADS).reshape(1, B_HEADS * B_HD)
    gk = jnp.tile(p['g_k'], B_KV).reshape(1, B_KV * B_HD)
    bdq = _block_diag_ones(B_HEADS * B_HD, B_HD)
    bdk = _block_diag_ones(B_KV * B_HD, B_HD)
    wo = p['w_o'].astype(bf16)
    sink = p['sink']
    weights = (gvn, ws, bsb, gq, gk, bdq, bdk, wo)
    w_specs = [_const_spec(a.shape) for a in weights]
    smem = pl.BlockSpec(memory_space=pltpu.SMEM)

    def scratch(n, pad):
        return [pltpu.VMEM((n, D), bf16), pltpu.VMEM((n, B_HEADS * B_HD), bf16),
                pltpu.VMEM((n + pad, B_KV * B_HD), bf16), pltpu.VMEM((n + pad, B_KV * B_HD), bf16)]

    kv_shape = jax.ShapeDtypeStruct((T_CTX, B_KV * B_HD), f32)
    xo_ctx, k_new, v_new = pl.pallas_call(
        functools.partial(_l0_kernel, n=SEQ, latent=False),
        grid=(BATCH,),
        in_specs=[smem, pl.BlockSpec((SEQ, EVEN_IN), lambda b: (b, 0)), pl.BlockSpec((SEQ, D), lambda b: (b, 0)),
                  pl.BlockSpec((8, D), lambda b: (0, 0))] + w_specs,
        out_specs=[pl.BlockSpec((SEQ, D), lambda b: (b, 0)),
                   pl.BlockSpec((SEQ, B_KV * B_HD), lambda b: (b, 0)),
                   pl.BlockSpec((SEQ, B_KV * B_HD), lambda b: (b, 0))],
        out_shape=[jax.ShapeDtypeStruct((T_CTX, D), f32), kv_shape, kv_shape],
        scratch_shapes=scratch(SEQ, 0),
        compiler_params=_cp(("parallel",)), name="l0_mixer_ctx",
    )(sink, z, xc, mods, *weights)

    cos, sin = _rope_tables(DEC_SEQ, B_HD, 2 * B_HD, 0)
    cos = cos.at[:, B_HD:].set(cos[:, :B_HD])
    sin = sin.at[:, B_HD:].set(sin[:, :B_HD])
    kc = cache_k.reshape(DEC_BATCH, PAST, B_KV * B_HD).astype(bf16)
    vc = cache_v.reshape(DEC_BATCH, PAST, B_KV * B_HD).astype(bf16)
    off = T_CTX // DEC_SEQ
    xo_lat = pl.pallas_call(
        functools.partial(_l0_kernel, n=DEC_SEQ, latent=True),
        grid=(DEC_BATCH,),
        in_specs=[smem, pl.BlockSpec((DEC_SEQ, EVEN_IN), lambda b: (off + b, 0)),
                  pl.BlockSpec((DEC_SEQ, D), lambda b: (b, 0)),
                  pl.BlockSpec((8, D), lambda b: (1 + b, 0))] + w_specs + [
                  _const_spec(cos.shape), _const_spec(sin.shape),
                  pl.BlockSpec((None, PAST, B_KV * B_HD), lambda b: (b, 0, 0)),
                  pl.BlockSpec((None, PAST, B_KV * B_HD), lambda b: (b, 0, 0))],
        out_specs=pl.BlockSpec((DEC_SEQ, D), lambda b: (b, 0)),
        out_shape=jax.ShapeDtypeStruct((T_LAT, D), f32),
        scratch_shapes=scratch(DEC_SEQ, 2 * WINDOW),
        compiler_params=_cp(("parallel",)), name="l0_mixer_lat",
    )(sink, z, xl, mods, *weights, cos, sin, kc, vc)
    return xo_ctx, xo_lat, k_new, v_new


def _l1_kernel(*refs, n, latent):
    if latent:
        (z_ref, x_ref, mod_ref, gqa_ref, wuq_ref, gq_ref, gkva_ref, wuk_ref, wuv_ref, gk_ref, wo_ref,
         cos_ref, sin_ref, cckv_ref, ckpe_ref, xo_ref, cat, qs, ks, vs) = refs
        n_ctx = PAST
    else:
        (z_ref, x_ref, mod_ref, gqa_ref, wuq_ref, gq_ref, gkva_ref, wuk_ref, wuv_ref, gk_ref, wo_ref,
         xo_ref, ckvo_ref, cat, qs, ks, vs) = refs
        n_ctx = 0
    n_chunks = n // CHUNK

    def slot_norm(v, g):
        return v * lax.rsqrt(jnp.sum(v * v, -1, keepdims=True) * (1.0 / C_QK) + EPS) * g

    def expand_keys(ckv_n, kslot, key_rows, rope):
        cb = ckv_n.astype(bf16)
        kn = jnp.dot(cb, wuk_ref[...], preferred_element_type=f32)
        for h in range(C_HEADS):
            hs = slice(h * SLOT, (h + 1) * SLOT)
            kh = slot_norm(kn[:, hs] + kslot, gk_ref[...])
            if rope is not None:
                kh = kh * rope[0] + _swap_pairs(kh) * rope[1]
            ks[key_rows, hs] = kh.astype(bf16)
        vs[key_rows, :] = jnp.dot(cb, wuv_ref[...], preferred_element_type=f32).astype(bf16)

    if latent:
        def ctx_keys(c, carry):
            r = pl.ds(pl.multiple_of(c * CHUNK, CHUNK), CHUNK)
            expand_keys(cckv_ref[r, :], ckpe_ref[r, :], r, None)
            return carry
        lax.fori_loop(0, PAST // CHUNK, ctx_keys, 0)

    def phase1(c, carry):
        r = pl.ds(pl.multiple_of(c * CHUNK, CHUNK), CHUNK)
        rope = (cos_ref[r, :], sin_ref[r, :]) if latent else None
        qa = _rms_rows(z_ref[r, 0:C_Q_LORA], gqa_ref[...]).astype(bf16)
        q = jnp.dot(qa, wuq_ref[...], preferred_element_type=f32)
        for h in range(C_HEADS):
            hs = slice(h * SLOT, (h + 1) * SLOT)
            qh = slot_norm(q[:, hs], gq_ref[...])
            if latent:
                qh = qh * rope[0] + _swap_pairs(qh) * rope[1]
            qs[r, hs] = (qh * C_SCALE).astype(bf16)
        ckv_n = _rms_rows(z_ref[r, C_Q_LORA:C_Q_LORA + C_KV_LORA], gkva_ref[...])
        if not latent:
            ckvo_ref[r, :] = ckv_n
        kr = pl.ds(pl.multiple_of(c * CHUNK + n_ctx, CHUNK), CHUNK)
        expand_keys(ckv_n, z_ref[r, C_Q_LORA + C_KV_LORA:ODD_IN_PAD], kr, rope)
        return carry

    lax.fori_loop(0, n_chunks, phase1, 0)

    def phase2(c, carry):
        r = pl.ds(pl.multiple_of(c * CHUNK, CHUNK), CHUNK)
        lane = lax.broadcasted_iota(jnp.int32, (CHUNK, 2 * C_V), 1)
        for pair in range(C_HEADS // 2):
            outs = []
            for h in (2 * pair, 2 * pair + 1):
                hs = slice(h * SLOT, (h + 1) * SLOT)
                s = lax.dot_general(qs[r, hs], ks[:, hs], (((1,), (1,)), ((), ())), preferred_element_type=f32)
                e = jnp.exp(s - jnp.max(s, -1, keepdims=True))
                den = jnp.sum(e, -1, keepdims=True)
                o = jnp.dot(e.astype(bf16), vs[:, pair * 2 * C_V:(pair + 1) * 2 * C_V], preferred_element_type=f32)
                outs.append(o / den)
            cat[r, pair * 2 * C_V:(pair + 1) * 2 * C_V] = jnp.where(lane < C_V, outs[0], outs[1]).astype(bf16)
        y = jnp.dot(cat[r, :], wo_ref[...], preferred_element_type=f32)
        xo_ref[r, :] = x_ref[r, :] + mod_ref[2:3, :] * y
        return carry

    lax.fori_loop(0, n_chunks, phase2, 0)


def _slot_cols(w, heads, width, lo, hi, lane0):
    k = w.shape[0]
    w3 = w.reshape(k, heads, width)[:, :, lo:hi]
    out = jnp.zeros((k, heads, SLOT), w.dtype).at[:, :, lane0:lane0 + (hi - lo)].set(w3)
    return out.reshape(k, heads * SLOT)


def _l1_weights(p):
    wuq = _slot_cols(p['w_uq'], C_HEADS, C_QK, 0, C_QK, 0).astype(bf16)
    wuk = _slot_cols(p['w_ukv'], C_HEADS, C_NOPE + C_V, 0, C_NOPE, 0).astype(bf16)
    wuv = p['w_ukv'].reshape(C_KV_LORA, C_HEADS, C_NOPE + C_V)[:, :, C_NOPE:].reshape(
        C_KV_LORA, C_HEADS * C_V).astype(bf16)
    gq = jnp.zeros((1, SLOT), f32).at[0, :C_QK].set(p['g_q'])
    gk = jnp.zeros((1, SLOT), f32).at[0, :C_QK].set(p['g_k'])
    return (p['g_qa'].reshape(1, C_Q_LORA), wuq, gq, p['g_kva'].reshape(1, C_KV_LORA), wuk, wuv, gk,
            p['w_o'].astype(bf16))


def _l1_mixer(z, x, mods, p, cache_ckv, cache_kpe):
    weights = _l1_weights(p)
    w_specs = [_const_spec(a.shape) for a in weights]

    def scratch(n, n_keys):
        return [pltpu.VMEM((n, D), bf16), pltpu.VMEM((n, C_HEADS * SLOT), bf16),
                pltpu.VMEM((n_keys, C_HEADS * SLOT), bf16), pltpu.VMEM((n_keys, C_HEADS * C_V), bf16)]

    xo_ctx, ckv_new = pl.pallas_call(
        functools.partial(_l1_kernel, n=SEQ, latent=False),
        grid=(BATCH,),
        in_specs=[pl.BlockSpec((SEQ, ODD_IN_PAD), lambda b: (b, 0)), pl.BlockSpec((SEQ, D), lambda b: (b, 0)),
                  pl.BlockSpec((8, D), lambda b: (0, 0))] + w_specs,
        out_specs=[pl.BlockSpec((SEQ, D), lambda b: (b, 0)), pl.BlockSpec((SEQ, C_KV_LORA), lambda b: (b, 0))],
        out_shape=[jax.ShapeDtypeStruct((T_CTX, D), f32), jax.ShapeDtypeStruct((T_CTX, C_KV_LORA), f32)],
        scratch_shapes=scratch(SEQ, SEQ),
        compiler_params=_cp(("parallel",)), name="l1_mixer_ctx",
    )(z, x, mods, *weights)

    cos, sin = _rope_tables(DEC_SEQ, C_ROPE, SLOT, C_NOPE)
    ckpe = jnp.zeros((DEC_BATCH, PAST, SLOT), f32).at[:, :, C_NOPE:C_QK].set(cache_kpe)
    off = T_CTX // DEC_SEQ
    xo_lat = pl.pallas_call(
        functools.partial(_l1_kernel, n=DEC_SEQ, latent=True),
        grid=(DEC_BATCH,),
        in_specs=[pl.BlockSpec((DEC_SEQ, ODD_IN_PAD), lambda b: (off + b, 0)),
                  pl.BlockSpec((DEC_SEQ, D), lambda b: (off + b, 0)),
                  pl.BlockSpec((8, D), lambda b: (1 + b, 0))] + w_specs + [
                  _const_spec(cos.shape), _const_spec(sin.shape),
                  pl.BlockSpec((None, PAST, C_KV_LORA), lambda b: (b, 0, 0)),
                  pl.BlockSpec((None, PAST, SLOT), lambda b: (b, 0, 0))],
        out_specs=pl.BlockSpec((DEC_SEQ, D), lambda b: (b, 0)),
        out_shape=jax.ShapeDtypeStruct((T_LAT, D), f32),
        scratch_shapes=scratch(DEC_SEQ, PAST + DEC_SEQ),
        compiler_params=_cp(("parallel",)), name="l1_mixer_lat",
    )(z, x, mods, *weights, cos, sin, cache_ckv, ckpe)
    return xo_ctx, xo_lat, ckv_new


def _router_kernel(xc_ref, xl_ref, gn_ref, mod_ref, wr_ref, br_ref, h_ref, route_ref):
    h = _rms_rows(_token_rows(xc_ref, xl_ref), gn_ref[...]) * (1.0 + mod_ref[4:5, :]) + mod_ref[3:4, :]
    h_ref[...] = h
    logits = jnp.dot(h, wr_ref[...], precision=HIGHEST, preferred_element_type=f32) + br_ref[...]
    lane_i = lax.broadcasted_iota(jnp.int32, logits.shape, 1)
    lane = lane_i.astype(f32)
    big = 1e6
    is_g = (lane_i >= N_EXPERTS) & (lane_i < N_EXPERTS + N_GROUPS)
    lg = jnp.where(is_g, logits, -jnp.inf)
    mg = jnp.max(lg, -1, keepdims=True)
    gsel = jnp.min(jnp.where(lg == mg, lane, big), -1, keepdims=True) - N_EXPERTS
    pg_sel = 1.0 / jnp.sum(jnp.where(is_g, jnp.exp(lg - mg), 0.0), -1, keepdims=True)
    in_grp = (lane_i < N_EXPERTS) & ((lane_i >> 3).astype(f32) == gsel)
    le = jnp.where(in_grp, logits, -jnp.inf)
    m1 = jnp.max(le, -1, keepdims=True)
    i1 = jnp.min(jnp.where(le == m1, lane, big), -1, keepdims=True)
    le2 = jnp.where(lane == i1, -jnp.inf, le)
    m2 = jnp.max(le2, -1, keepdims=True)
    i2 = jnp.min(jnp.where(le2 == m2, lane, big), -1, keepdims=True)
    e2 = jnp.exp(m2 - m1)
    w1 = pg_sel / (1.0 + e2)
    w2 = pg_sel * e2 / (1.0 + e2)
    route = jnp.where(lane_i == 0, i1,
                      jnp.where(lane_i == 1, i2, jnp.where(lane_i == 2, w1, jnp.where(lane_i == 3, w2, 0.0))))
    route_ref[...] = route.T[0:8, :]


def _router(xc, xl, g_norm, mods, p):
    wr = jnp.zeros((D, 128), f32).at[:, :N_EXPERTS].set(p['w_re']).at[
        :, N_EXPERTS:N_EXPERTS + N_GROUPS].set(p['w_rg'])
    br = jnp.zeros((1, 128), f32).at[0, :N_EXPERTS].set(p['b_re']).at[
        0, N_EXPERTS:N_EXPERTS + N_GROUPS].set(p['b_rg'])
    row = pl.BlockSpec((ROW_TILE, D), lambda i: (i, 0))
    return pl.pallas_call(
        _router_kernel,
        grid=(T // ROW_TILE,),
        in_specs=_token_specs(D) + [
                  _const_spec((1, D)),
                  pl.BlockSpec((8, D), lambda i: (_sample_of_tile(i, ROW_TILE), 0)),
                  _const_spec((D, 128)), _const_spec((1, 128))],
        out_specs=[row, pl.BlockSpec((8, ROW_TILE), lambda i: (0, i))],
        out_shape=[jax.ShapeDtypeStruct((T, D), f32), jax.ShapeDtypeStruct((8, T), f32)],
        compiler_params=_cp(("parallel",)), name="router",
    )(xc, xl, g_norm.reshape(1, D), mods, wr, br)


PAD_ASSIGN = 2 * T
TE_LANES = 256


def _plan_kernel(rt_ref, pos_ref, te_ref, rank):
    n_blk = T // 128
    e_col = lax.broadcasted_iota(jnp.int32, (N_EXPERTS, 128), 0).astype(f32)
    ri = lax.broadcasted_iota(jnp.int32, (128, 128), 0)
    ci = lax.broadcasted_iota(jnp.int32, (128, 128), 1)
    before = jnp.where(ri < ci, 1.0, 0.0).astype(bf16)

    def picks(b):
        cs = slice(b * 128, (b + 1) * 128)
        return rt_ref[0:1, cs] == e_col, rt_ref[1:2, cs] == e_col

    counts = jnp.zeros((N_EXPERTS, 1), f32)
    for b in range(n_blk):
        m0, m1 = picks(b)
        m = jnp.where(m0, 1.0, 0.0) + jnp.where(m1, 1.0, 0.0)
        rank[:, b * 128:(b + 1) * 128] = jnp.dot(m.astype(bf16), before, preferred_element_type=f32) + counts
        counts = counts + jnp.sum(m, axis=1, keepdims=True)

    tiles = jnp.floor((counts + (MOE_TILE - 1.0)) * (1.0 / MOE_TILE))
    er = lax.broadcasted_iota(jnp.int32, (N_EXPERTS, N_EXPERTS), 0)
    ec = lax.broadcasted_iota(jnp.int32, (N_EXPERTS, N_EXPERTS), 1)
    earlier = jnp.where(ec < er, 1.0, 0.0).astype(bf16)
    tile_start = jnp.dot(earlier, jnp.broadcast_to(tiles, (N_EXPERTS, 128)).astype(bf16),
                         preferred_element_type=f32)
    row_start = tile_start * MOE_TILE

    sub = lax.broadcasted_iota(jnp.int32, (8, 128), 0)
    for b in range(n_blk):
        m0, m1 = picks(b)
        base = rank[:, b * 128:(b + 1) * 128] + row_start
        p0 = jnp.sum(jnp.where(m0, base, 0.0), axis=0, keepdims=True)
        p1 = jnp.sum(jnp.where(m1, base, 0.0), axis=0, keepdims=True)
        pos_ref[:, b * 128:(b + 1) * 128] = jnp.where(sub == 0, p0, jnp.where(sub == 1, p1, 0.0)).astype(jnp.int32)

    tile_end = tile_start + tiles
    n_used = jnp.max(tile_end, axis=0, keepdims=True)
    sub = lax.broadcasted_iota(jnp.int32, (8, 128), 0)
    for j in range(TE_LANES // 128):
        t = (lax.broadcasted_iota(jnp.int32, (N_EXPERTS, 128), 1) + j * 128).astype(f32)
        te = jnp.minimum(jnp.sum(jnp.where(tile_end <= t, 1.0, 0.0), axis=0, keepdims=True), N_EXPERTS - 1.0)
        te_ref[:, j * 128:(j + 1) * 128] = jnp.where(sub == 0, te, jnp.where(sub == 1, n_used, 0.0)).astype(jnp.int32)


def _invert_kernel(pos_ref, a_ref):
    def init(i, carry):
        for u in range(16):
            a_ref[i * 16 + u] = PAD_ASSIGN
        return carry
    lax.fori_loop(0, MOE_ROWS // 16, init, 0)

    def place(i, carry):
        for u in range(8):
            a = i * 8 + u
            a_ref[pos_ref[a]] = a
        return carry
    lax.fori_loop(0, 2 * T // 8, place, 0)


def _route_plan(route_t):
    pos, te = pl.pallas_call(
        _plan_kernel,
        out_shape=[jax.ShapeDtypeStruct((8, T), jnp.int32), jax.ShapeDtypeStruct((8, TE_LANES), jnp.int32)],
        scratch_shapes=[pltpu.VMEM((N_EXPERTS, T), f32)],
        compiler_params=_cp(None), name="route_plan",
    )(route_t)
    a_slot = pl.pallas_call(
        _invert_kernel,
        in_specs=[pl.BlockSpec(memory_space=pltpu.SMEM)],
        out_specs=pl.BlockSpec(memory_space=pltpu.SMEM),
        out_shape=jax.ShapeDtypeStruct((MOE_ROWS,), jnp.int32),
        name="route_invert",
    )(pos[0:2].reshape(2 * T))
    gates = jnp.pad(route_t[2:4].reshape(2 * T), (0, 8))
    return te[0, :MOE_TILES], te[1, :1], a_slot, gates


def _slot_token(a):
    return a - jnp.where(a >= T, T, 0)


def _first_tile_of_expert(i, te_ref):
    return (i == 0) | (te_ref[i] != te_ref[jnp.maximum(i - 1, 0)])


def _moe_up_kernel(te_ref, nu_ref, a_ref, h_hbm, w1_ref, w3_ref, o_ref, xs, gbuf, w13, sem):
    i = pl.program_id(0)
    used = i < nu_ref[0]

    @pl.when(i == 0)
    def _():
        cp = pltpu.make_async_copy(h_hbm, xs.at[pl.ds(0, T), :], sem)
        cp.start()
        xs[T:T + 8, :] = jnp.zeros((8, D), f32)
        cp.wait()

    @pl.when(used & _first_tile_of_expert(i, te_ref))
    def _():
        w13[:, :D_EXPERT] = w1_ref[0].astype(bf16)
        w13[:, D_EXPERT:] = w3_ref[0].astype(bf16)

    @pl.when(used)
    def _():
        base = i * MOE_TILE
        for r in range(MOE_TILE):
            gbuf[pl.ds(r, 1), :] = xs[pl.ds(_slot_token(a_ref[base + r]), 1), :]
        h13 = jnp.dot(gbuf[...].astype(bf16), w13[...], preferred_element_type=f32)
        o_ref[...] = (_silu(h13[:, :D_EXPERT]) * h13[:, D_EXPERT:]).astype(bf16)

    @pl.when(jnp.logical_not(used))
    def _():
        o_ref[...] = jnp.zeros_like(o_ref)


def _moe_down_kernel(te_ref, nu_ref, a_ref, gate_ref, hh_ref, w2_ref, o_hbm, acc, ybuf, w2b, sem):
    i = pl.program_id(0)
    used = i < nu_ref[0]

    @pl.when(i == 0)
    def _():
        def zero(c, carry):
            acc[pl.ds(pl.multiple_of(c * 256, 256), 256), :] = jnp.zeros((256, D), f32)
            return carry
        lax.fori_loop(0, T // 256, zero, 0)
        acc[T:T + 8, :] = jnp.zeros((8, D), f32)

    @pl.when(used & _first_tile_of_expert(i, te_ref))
    def _():
        w2b[...] = w2_ref[0].astype(bf16)

    @pl.when(used)
    def _():
        ybuf[...] = jnp.dot(hh_ref[...], w2b[...], preferred_element_type=f32)
        base = i * MOE_TILE
        for r in range(MOE_TILE):
            a = a_ref[base + r]
            t = _slot_token(a)
            acc[pl.ds(t, 1), :] = acc[pl.ds(t, 1), :] + gate_ref[a] * ybuf[pl.ds(r, 1), :]

    @pl.when(i == MOE_TILES - 1)
    def _():
        cp = pltpu.make_async_copy(acc.at[pl.ds(0, T), :], o_hbm, sem)
        cp.start()
        cp.wait()


def _moe(h, route_t, p):
    tile_expert, n_used, a_slot, gates = _route_plan(route_t)
    hh = pl.pallas_call(
        _moe_up_kernel,
        grid_spec=pltpu.PrefetchScalarGridSpec(
            num_scalar_prefetch=3, grid=(MOE_TILES,),
            in_specs=[pl.BlockSpec(memory_space=pl.ANY),
                      pl.BlockSpec((1, D, D_EXPERT), lambda i, te, nu, a: (te[i], 0, 0)),
                      pl.BlockSpec((1, D, D_EXPERT), lambda i, te, nu, a: (te[i], 0, 0))],
            out_specs=pl.BlockSpec((MOE_TILE, D_EXPERT), lambda i, te, nu, a: (i, 0)),
            scratch_shapes=[pltpu.VMEM((T + 8, D), f32), pltpu.VMEM((MOE_TILE, D), f32),
                            pltpu.VMEM((D, 2 * D_EXPERT), bf16), pltpu.SemaphoreType.DMA(())]),
        out_shape=jax.ShapeDtypeStruct((MOE_ROWS, D_EXPERT), bf16),
        compiler_params=_cp(("arbitrary",)), name="moe_up",
    )(tile_expert, n_used, a_slot, h, p['w1'], p['w3'])
    return pl.pallas_call(
        _moe_down_kernel,
        grid_spec=pltpu.PrefetchScalarGridSpec(
            num_scalar_prefetch=4, grid=(MOE_TILES,),
            in_specs=[pl.BlockSpec((MOE_TILE, D_EXPERT), lambda i, te, nu, a, g: (i, 0)),
                      pl.BlockSpec((1, D_EXPERT, D), lambda i, te, nu, a, g: (te[i], 0, 0))],
            out_specs=pl.BlockSpec(memory_space=pl.ANY),
            scratch_shapes=[pltpu.VMEM((T + 8, D), f32), pltpu.VMEM((MOE_TILE, D), f32),
                            pltpu.VMEM((D_EXPERT, D), bf16), pltpu.SemaphoreType.DMA(())]),
        out_shape=jax.ShapeDtypeStruct((T, D), f32),
        compiler_params=_cp(("arbitrary",)), name="moe_down",
    )(tile_expert, n_used, a_slot, gates, hh, p['w2'])


def _final_kernel(x_ref, d_ref, mod_ref, o_ref):
    o_ref[...] = x_ref[...] + mod_ref[5:6, :] * d_ref[...]


def _final(x, delta, mods, row0, seq):
    n_rows = x.shape[0]
    t0 = row0 // ROW_TILE
    return pl.pallas_call(
        _final_kernel,
        grid=(n_rows // ROW_TILE,),
        in_specs=[pl.BlockSpec((ROW_TILE, D), lambda i: (i, 0)),
                  pl.BlockSpec((ROW_TILE, D), lambda i: (t0 + i, 0)),
                  pl.BlockSpec((8, D), lambda i: (_sample_of_tile(t0 + i, ROW_TILE), 0))],
        out_specs=pl.BlockSpec((ROW_TILE, D), lambda i: (i, 0)),
        out_shape=jax.ShapeDtypeStruct((n_rows, D), f32),
        compiler_params=_cp(("parallel",)), name="final",
    )(x, delta, mods).reshape(n_rows // seq, seq, D)


def kernel(x_prompt, x_sample, cache_l0_k, cache_l0_v, cache_l1_ckv, cache_l1_kpe, c, c_ctx, l0_g_norm1, l0_g_norm2, l0_w_ada, l0_b_ada, l0_w_in, l0_g_vnorm, l0_w_s, l0_b_s, l0_g_q, l0_g_k, l0_sink, l0_w_o, l0_w_rg, l0_b_rg, l0_w_re, l0_b_re, l0_w1, l0_w3, l0_w2, l1_g_norm1, l1_g_norm2, l1_w_ada, l1_b_ada, l1_w_in, l1_g_qa, l1_w_uq, l1_g_kva, l1_w_ukv, l1_g_q, l1_g_k, l1_w_o, l1_w_rg, l1_b_rg, l1_w_re, l1_b_re, l1_w1, l1_w3, l1_w2):
    p0 = dict(g_vnorm=l0_g_vnorm, w_s=l0_w_s, b_s=l0_b_s, g_q=l0_g_q, g_k=l0_g_k, sink=l0_sink, w_o=l0_w_o,
              w_rg=l0_w_rg, b_rg=l0_b_rg, w_re=l0_w_re, b_re=l0_b_re, w1=l0_w1, w3=l0_w3, w2=l0_w2)
    p1 = dict(g_qa=l1_g_qa, w_uq=l1_w_uq, g_kva=l1_g_kva, w_ukv=l1_w_ukv, g_q=l1_g_q, g_k=l1_g_k, w_o=l1_w_o,
              w_rg=l1_w_rg, b_rg=l1_b_rg, w_re=l1_w_re, b_re=l1_b_re, w1=l1_w1, w3=l1_w3, w2=l1_w2)

    cond8 = jnp.zeros((8, D), f32).at[0].set(c_ctx).at[1:1 + DEC_BATCH].set(c)
    mods0 = _mod_rows(_adaln(cond8, l0_w_ada, l0_b_ada))
    mods1 = _mod_rows(_adaln(cond8, l1_w_ada, l1_b_ada))

    xc0 = x_prompt.reshape(T_CTX, D)
    xl0 = x_sample.reshape(T_LAT, D)

    z0 = _proj(xc0, xl0, None, l0_g_norm1, mods0, l0_w_in.astype(bf16))
    xc0m, xl0m, k_new, v_new = _l0_mixer(z0, xc0, xl0, mods0, p0, cache_l0_k, cache_l0_v)
    h0, route0 = _router(xc0m, xl0m, l0_g_norm2, mods0, p0)
    moe0 = _moe(h0, route0, p0)

    w_in1 = jnp.zeros((D, ODD_IN_PAD), f32).at[:, :C_Q_LORA + C_KV_LORA].set(
        l1_w_in[:, :C_Q_LORA + C_KV_LORA]).at[
        :, C_Q_LORA + C_KV_LORA + C_NOPE:C_Q_LORA + C_KV_LORA + C_QK].set(l1_w_in[:, C_Q_LORA + C_KV_LORA:])
    mods01 = mods1.reshape(3, 8, D).at[:, 5].set(mods0.reshape(3, 8, D)[:, 5]).reshape(24, D)
    z1, x1 = _proj(xc0m, xl0m, moe0, l1_g_norm1, mods01, w_in1.astype(bf16))
    xc1m, xl1m, ckv_new = _l1_mixer(z1, x1, mods1, p1, cache_l1_ckv, cache_l1_kpe)
    h1, route1 = _router(xc1m, xl1m, l1_g_norm2, mods1, p1)
    moe1 = _moe(h1, route1, p1)

    y_prompt = _final(xc1m, moe1, mods1, 0, SEQ)
    y_sample = _final(xl1m, moe1, mods1, T_CTX, DEC_SEQ)
    kpe_new = z1[:T_CTX, C_Q_LORA + C_KV_LORA + C_NOPE:C_Q_LORA + C_KV_LORA + C_QK]
    return (y_prompt, y_sample,
            k_new.reshape(BATCH, SEQ, B_KV, B_HD), v_new.reshape(BATCH, SEQ, B_KV, B_HD),
            ckv_new.reshape(BATCH, SEQ, C_KV_LORA), kpe_new.reshape(BATCH, SEQ, C_ROPE))
```

```python
import functools

import jax
import jax.numpy as jnp
from jax import lax
from jax.experimental import pallas as pl
from jax.experimental.pallas import tpu as pltpu

f32 = jnp.float32
bf16 = jnp.bfloat16
HIGHEST = lax.Precision.HIGHEST

D = 1024
BATCH, SEQ = 32, 256
DEC_BATCH, DEC_SEQ = 2, 1024
PAST = 512
T_CTX = BATCH * SEQ
T_LAT = DEC_BATCH * DEC_SEQ
T = T_CTX + T_LAT
GRID_W = 64
CHUNK = 128
WINDOW = 128
ROPE_THETA = 10000.0
EPS = 1e-6
NEG_INF = -1e30

A_WIDTH = 512
A_GROUPS = 4
B_HEADS, B_KV, B_GROUP, B_HD = 8, 2, 4, 64
B_SCALE = B_HD ** -0.5
EVEN_IN = 1792

C_HEADS, C_Q_LORA, C_KV_LORA, C_NOPE, C_ROPE, C_V = 16, 384, 256, 64, 32, 64
C_QK = C_NOPE + C_ROPE
C_SCALE = C_QK ** -0.5
ODD_IN_PAD = 768
SLOT = 128

N_GROUPS, EPG, N_EXPERTS, D_EXPERT = 4, 8, 32, 256

ROW_TILE = 512
MOE_TILE = 128
MOE_ROWS = 2 * T + N_EXPERTS * MOE_TILE
MOE_TILES = MOE_ROWS // MOE_TILE
VMEM_CAP = 56 * 1024 * 1024


def _cp(sem, vmem=VMEM_CAP):
    return pltpu.CompilerParams(dimension_semantics=sem, vmem_limit_bytes=vmem)


def _const_spec(shape):
    nd = len(shape)
    return pl.BlockSpec(shape, lambda *_: (0,) * nd)


def _sample_of_tile(i, tile):
    n_ctx = T_CTX // tile
    per_lat = DEC_SEQ // tile
    return jnp.where(i < n_ctx, 0, 1 + (i - n_ctx) // per_lat)


def _silu(x):
    return x * jax.nn.sigmoid(x)


def _rms_rows(x, g):
    return x * lax.rsqrt(jnp.mean(x * x, -1, keepdims=True) + EPS) * g


def _swap_pairs(x):
    lane = lax.broadcasted_iota(jnp.int32, x.shape, x.ndim - 1)
    nxt = pltpu.roll(x, x.shape[-1] - 1, x.ndim - 1)
    prv = pltpu.roll(x, 1, x.ndim - 1)
    return jnp.where((lane & 1) == 0, nxt, prv)


def _adaln_kernel(c_ref, w_ref, b_ref, o_ref):
    s = _silu(c_ref[...])
    o_ref[...] = jnp.dot(s, w_ref[...], precision=HIGHEST, preferred_element_type=f32) + b_ref[...]


def _adaln(cond8, w, b):
    n = w.shape[1]
    tn = 1536
    return pl.pallas_call(
        _adaln_kernel,
        grid=(n // tn,),
        in_specs=[_const_spec((8, D)), pl.BlockSpec((D, tn), lambda j: (0, j)),
                  pl.BlockSpec((1, tn), lambda j: (0, j))],
        out_specs=pl.BlockSpec((8, tn), lambda j: (0, j)),
        out_shape=jax.ShapeDtypeStruct((8, n), f32),
        compiler_params=_cp(("arbitrary",)),
        name="adaln",
    )(cond8, w, b.reshape(1, n))


def _mod_rows(m8):
    m = m8[:3].reshape(3, 6, D)
    return jnp.pad(m, ((0, 0), (0, 2), (0, 0))).reshape(24, D)


N_CTX_TILES = T_CTX // ROW_TILE


def _token_specs(width):
    return [pl.BlockSpec((ROW_TILE, width), lambda i: (jnp.minimum(i, N_CTX_TILES - 1), 0)),
            pl.BlockSpec((ROW_TILE, width), lambda i: (jnp.maximum(i - N_CTX_TILES, 0), 0))]


def _token_rows(xc_ref, xl_ref):
    return jnp.where(pl.program_id(0) < N_CTX_TILES, xc_ref[...], xl_ref[...])


LANES = 128
SUBS = D // LANES


def _store_token_major(ref, x):
    n = x.shape[0]
    for s in range(SUBS):
        ref[pl.ds(s, n, stride=SUBS), :] = x[:, s * LANES:(s + 1) * LANES]


def _load_token_major(ref, n):
    return jnp.concatenate([ref[pl.ds(s, n, stride=SUBS), :] for s in range(SUBS)], axis=1)


def _token_major_spec(rows, index_map):
    return pl.BlockSpec((rows * SUBS, LANES), index_map)


def _proj_kernel(*refs, residual):
    if residual:
        xc_ref, xl_ref, d_ref, gn_ref, mod_ref, w_ref, z_ref, xo_ref = refs
        x = _token_rows(xc_ref, xl_ref) + mod_ref[5:6, :] * _load_token_major(d_ref, ROW_TILE)
        xo_ref[...] = x
    else:
        xc_ref, xl_ref, gn_ref, mod_ref, w_ref, z_ref = refs
        x = _token_rows(xc_ref, xl_ref)
    h = _rms_rows(x, gn_ref[...]) * (1.0 + mod_ref[1:2, :]) + mod_ref[0:1, :]
    z_ref[...] = jnp.dot(h.astype(bf16), w_ref[...], preferred_element_type=f32)


def _proj(xc, xl, delta, g_norm, mods, w_bf16):
    n = w_bf16.shape[1]
    residual = delta is not None
    row = pl.BlockSpec((ROW_TILE, D), lambda i: (i, 0))
    in_specs = _token_specs(D) + ([_token_major_spec(ROW_TILE, lambda i: (i, 0))] if residual else []) + [
        _const_spec((1, D)),
        pl.BlockSpec((8, D), lambda i: (_sample_of_tile(i, ROW_TILE), 0)),
        _const_spec((D, n))]
    out_specs = [pl.BlockSpec((ROW_TILE, n), lambda i: (i, 0))]
    out_shape = [jax.ShapeDtypeStruct((T, n), f32)]
    if residual:
        out_specs.append(row)
        out_shape.append(jax.ShapeDtypeStruct((T, D), f32))
    args = (xc, xl) + ((delta,) if residual else ()) + (g_norm.reshape(1, D), mods, w_bf16)
    out = pl.pallas_call(
        functools.partial(_proj_kernel, residual=residual),
        grid=(T // ROW_TILE,), in_specs=in_specs, out_specs=out_specs, out_shape=out_shape,
        compiler_params=_cp(("parallel",)), name="proj_res" if residual else "proj",
    )(*args)
    return out if residual else out[0]


def _l0_kernel(*refs, n, latent):
    if latent:
        (sink_ref, z_ref, x_ref, mod_ref, gvn_ref, ws_ref, bsb_ref, gq_ref, gk_ref, bdq_ref, bdk_ref,
         wo_ref, cos_ref, sin_ref, kc_ref, vc_ref, xo_ref, cat, qs, ks, vs) = refs
        key_off = WINDOW
    else:
        (sink_ref, z_ref, x_ref, mod_ref, gvn_ref, ws_ref, bsb_ref, gq_ref, gk_ref, bdq_ref, bdk_ref,
         wo_ref, xo_ref, ko_ref, vo_ref, cat, qs, ks, vs) = refs
        key_off = 0
    n_chunks = n // CHUNK

    if latent:
        zpad = jnp.zeros((WINDOW, 2 * B_HD), bf16)
        for r0 in (0, WINDOW + n):
            ks[r0:r0 + WINDOW, :] = zpad
            vs[r0:r0 + WINDOW, :] = zpad

    def head_norm(v, bd_ref, g):
        sq = v * v
        hi = sq.astype(bf16)
        lo = (sq - hi.astype(f32)).astype(bf16)
        ss = (jnp.dot(hi, bd_ref[...], preferred_element_type=f32)
              + jnp.dot(lo, bd_ref[...], preferred_element_type=f32))
        return v * lax.rsqrt(ss * (1.0 / B_HD) + EPS) * g

    def phase1(c, carry):
        r = pl.ds(pl.multiple_of(c * CHUNK, CHUNK), CHUNK)
        u = jax.nn.gelu(z_ref[r, 0:A_WIDTH])
        v = jax.nn.gelu(z_ref[r, A_WIDTH:2 * A_WIDTH])
        mu = jnp.mean(v, -1, keepdims=True)
        var = jnp.mean(jnp.square(v - mu), -1, keepdims=True)
        vn = ((v - mu) * lax.rsqrt(var + EPS) * gvn_ref[...]).astype(bf16)
        for g in range(A_GROUPS):
            cs = slice(g * CHUNK, (g + 1) * CHUNK)
            mixed = jnp.dot(ws_ref[g], vn[:, cs], preferred_element_type=f32) + bsb_ref[g]
            cat[r, cs] = (u[:, cs] * mixed).astype(bf16)
        q = head_norm(z_ref[r, 1024:1536], bdq_ref, gq_ref[...])
        k = head_norm(z_ref[r, 1536:1664], bdk_ref, gk_ref[...])
        vv = z_ref[r, 1664:1792]
        if latent:
            cs_, sn_ = cos_ref[r, :], sin_ref[r, :]
            k = k * cs_ + _swap_pairs(k) * sn_
            q = jnp.concatenate(
                [q[:, j * 128:(j + 1) * 128] * cs_ + _swap_pairs(q[:, j * 128:(j + 1) * 128]) * sn_
                 for j in range(4)], axis=1)
        else:
            ko_ref[r, :] = k
            vo_ref[r, :] = vv
        qs[r, :] = (q * B_SCALE).astype(bf16)
        kr = pl.ds(pl.multiple_of(c * CHUNK + key_off, CHUNK), CHUNK)
        ks[kr, :] = k.astype(bf16)
        vs[kr, :] = vv.astype(bf16)
        return carry

    lax.fori_loop(0, n_chunks, phase1, 0)

    def phase2(c, carry):
        start = pl.multiple_of(c * CHUNK, CHUNK)
        r = pl.ds(start, CHUNK)
        if latent:
            span = CHUNK + 2 * WINDOW
            kr = pl.ds(start, span)
            qi = lax.broadcasted_iota(jnp.int32, (B_GROUP * CHUNK, span), 0) & (CHUNK - 1)
            kj = lax.broadcasted_iota(jnp.int32, (B_GROUP * CHUNK, span), 1)
            kpos = start - WINDOW + kj
            ok = (jnp.abs(kj - WINDOW - qi) <= WINDOW) & (kpos >= 0) & (kpos < n)
        else:
            kr = pl.ds(0, n)
        for kvh in range(B_KV):
            hs = slice(kvh * B_HD, (kvh + 1) * B_HD)
            qst = jnp.concatenate(
                [qs[r, (kvh * B_GROUP + g) * B_HD:(kvh * B_GROUP + g + 1) * B_HD] for g in range(B_GROUP)], axis=0)
            sk = jnp.concatenate(
                [jnp.full((CHUNK, 1), sink_ref[kvh * B_GROUP + g], f32) for g in range(B_GROUP)], axis=0)
            s_loc = lax.dot_general(qst, ks[kr, hs], (((1,), (1,)), ((), ())), preferred_element_type=f32)
            if latent:
                s_loc = jnp.where(ok, s_loc, NEG_INF)
                s_ctx = lax.dot_general(qst, kc_ref[:, hs], (((1,), (1,)), ((), ())),
                                        preferred_element_type=f32)
                m = jnp.maximum(jnp.maximum(jnp.max(s_loc, -1, keepdims=True),
                                            jnp.max(s_ctx, -1, keepdims=True)), sk)
                e_loc = jnp.exp(s_loc - m)
                e_ctx = jnp.exp(s_ctx - m)
                den = jnp.sum(e_loc, -1, keepdims=True) + jnp.sum(e_ctx, -1, keepdims=True) + jnp.exp(sk - m)
                o = (jnp.dot(e_loc.astype(bf16), vs[kr, hs], preferred_element_type=f32)
                     + jnp.dot(e_ctx.astype(bf16), vc_ref[:, hs], preferred_element_type=f32))
            else:
                m = jnp.maximum(jnp.max(s_loc, -1, keepdims=True), sk)
                e_loc = jnp.exp(s_loc - m)
                den = jnp.sum(e_loc, -1, keepdims=True) + jnp.exp(sk - m)
                o = jnp.dot(e_loc.astype(bf16), vs[kr, hs], preferred_element_type=f32)
            o = o / den
            for g in range(B_GROUP):
                h = kvh * B_GROUP + g
                cat[r, A_WIDTH + h * B_HD:A_WIDTH + (h + 1) * B_HD] = o[g * CHUNK:(g + 1) * CHUNK].astype(bf16)
        y = jnp.dot(cat[r, :], wo_ref[...], preferred_element_type=f32)
        xo_ref[r, :] = x_ref[r, :] + mod_ref[2:3, :] * y
        return carry

    lax.fori_loop(0, n_chunks, phase2, 0)


def _rope_tables(n, rot_dim, lanes, lane0):
    rows_count = n // GRID_W
    rows = jnp.repeat(jnp.arange(rows_count), GRID_W).astype(f32)
    cols = jnp.tile(jnp.arange(GRID_W), rows_count).astype(f32)
    d_axis = rot_dim // 2
    inv = ROPE_THETA ** (-jnp.arange(0, d_axis, 2, dtype=f32) / d_axis)
    ang = jnp.concatenate([rows[:, None] * inv, cols[:, None] * inv], -1)
    cos = jnp.repeat(jnp.cos(ang), 2, axis=1)
    sin = jnp.repeat(jnp.sin(ang), 2, axis=1) * jnp.tile(jnp.array([-1.0, 1.0], f32), rot_dim // 2)
    c = jnp.ones((n, lanes), f32).at[:, lane0:lane0 + rot_dim].set(cos)
    s = jnp.zeros((n, lanes), f32).at[:, lane0:lane0 + rot_dim].set(sin)
    return c, s


def _block_diag_ones(width, block):
    i = jnp.arange(width) // block
    return (i[:, None] == i[None, :]).astype(bf16)


def _l0_mixer(z, xc, xl, mods, p, cache_k, cache_v):
    gvn = p['g_vnorm'].reshape(1, A_WIDTH)
    ws = p['w_s'].astype(bf16)
    bsb = jnp.broadcast_to(p['b_s'][:, :, None], (A_GROUPS, CHUNK, CHUNK))
    gq = jnp.tile(p['g_q'], B_HEADS).reshape(1, B_HEADS * B_HD)
    gk = jnp.tile(p['g_k'], B_KV).reshape(1, B_KV * B_HD)
    bdq = _block_diag_ones(B_HEADS * B_HD, B_HD)
    bdk = _block_diag_ones(B_KV * B_HD, B_HD)
    wo = p['w_o'].astype(bf16)
    sink = p['sink']
    weights = (gvn, ws, bsb, gq, gk, bdq, bdk, wo)
    w_specs = [_const_spec(a.shape) for a in weights]
    smem = pl.BlockSpec(memory_space=pltpu.SMEM)

    def scratch(n, pad):
        return [pltpu.VMEM((n, D), bf16), pltpu.VMEM((n, B_HEADS * B_HD), bf16),
                pltpu.VMEM((n + pad, B_KV * B_HD), bf16), pltpu.VMEM((n + pad, B_KV * B_HD), bf16)]

    kv_shape = jax.ShapeDtypeStruct((T_CTX, B_KV * B_HD), f32)
    xo_ctx, k_new, v_new = pl.pallas_call(
        functools.partial(_l0_kernel, n=SEQ, latent=False),
        grid=(BATCH,),
        in_specs=[smem, pl.BlockSpec((SEQ, EVEN_IN), lambda b: (b, 0)), pl.BlockSpec((SEQ, D), lambda b: (b, 0)),
                  pl.BlockSpec((8, D), lambda b: (0, 0))] + w_specs,
        out_specs=[pl.BlockSpec((SEQ, D), lambda b: (b, 0)),
                   pl.BlockSpec((SEQ, B_KV * B_HD), lambda b: (b, 0)),
                   pl.BlockSpec((SEQ, B_KV * B_HD), lambda b: (b, 0))],
        out_shape=[jax.ShapeDtypeStruct((T_CTX, D), f32), kv_shape, kv_shape],
        scratch_shapes=scratch(SEQ, 0),
        compiler_params=_cp(("parallel",)), name="l0_mixer_ctx",
    )(sink, z, xc, mods, *weights)

    cos, sin = _rope_tables(DEC_SEQ, B_HD, 2 * B_HD, 0)
    cos = cos.at[:, B_HD:].set(cos[:, :B_HD])
    sin = sin.at[:, B_HD:].set(sin[:, :B_HD])
    kc = cache_k.reshape(DEC_BATCH, PAST, B_KV * B_HD).astype(bf16)
    vc = cache_v.reshape(DEC_BATCH, PAST, B_KV * B_HD).astype(bf16)
    off = T_CTX // DEC_SEQ
    xo_lat = pl.pallas_call(
        functools.partial(_l0_kernel, n=DEC_SEQ, latent=True),
        grid=(DEC_BATCH,),
        in_specs=[smem, pl.BlockSpec((DEC_SEQ, EVEN_IN), lambda b: (off + b, 0)),
                  pl.BlockSpec((DEC_SEQ, D), lambda b: (b, 0)),
                  pl.BlockSpec((8, D), lambda b: (1 + b, 0))] + w_specs + [
                  _const_spec(cos.shape), _const_spec(sin.shape),
                  pl.BlockSpec((None, PAST, B_KV * B_HD), lambda b: (b, 0, 0)),
                  pl.BlockSpec((None, PAST, B_KV * B_HD), lambda b: (b, 0, 0))],
        out_specs=pl.BlockSpec((DEC_SEQ, D), lambda b: (b, 0)),
        out_shape=jax.ShapeDtypeStruct((T_LAT, D), f32),
        scratch_shapes=scratch(DEC_SEQ, 2 * WINDOW),
        compiler_params=_cp(("parallel",)), name="l0_mixer_lat",
    )(sink, z, xl, mods, *weights, cos, sin, kc, vc)
    return xo_ctx, xo_lat, k_new, v_new


def _l1_kernel(*refs, n, latent):
    if latent:
        (z_ref, x_ref, mod_ref, gqa_ref, wuq_ref, gq_ref, gkva_ref, wuk_ref, wuv_ref, gk_ref, wo_ref,
         cos_ref, sin_ref, cckv_ref, ckpe_ref, xo_ref, cat, qs, ks, vs) = refs
        n_ctx = PAST
    else:
        (z_ref, x_ref, mod_ref, gqa_ref, wuq_ref, gq_ref, gkva_ref, wuk_ref, wuv_ref, gk_ref, wo_ref,
         xo_ref, ckvo_ref, cat, qs, ks, vs) = refs
        n_ctx = 0
    n_chunks = n // CHUNK

    def slot_norm(v, g):
        return v * lax.rsqrt(jnp.sum(v * v, -1, keepdims=True) * (1.0 / C_QK) + EPS) * g

    def expand_keys(ckv_n, kslot, key_rows, rope):
        cb = ckv_n.astype(bf16)
        kn = jnp.dot(cb, wuk_ref[...], preferred_element_type=f32)
        for h in range(C_HEADS):
            hs = slice(h * SLOT, (h + 1) * SLOT)
            kh = slot_norm(kn[:, hs] + kslot, gk_ref[...])
            if rope is not None:
                kh = kh * rope[0] + _swap_pairs(kh) * rope[1]
            ks[key_rows, hs] = kh.astype(bf16)
        vs[key_rows, :] = jnp.dot(cb, wuv_ref[...], preferred_element_type=f32).astype(bf16)

    if latent:
        def ctx_keys(c, carry):
            r = pl.ds(pl.multiple_of(c * CHUNK, CHUNK), CHUNK)
            expand_keys(cckv_ref[r, :], ckpe_ref[r, :], r, None)
            return carry
        lax.fori_loop(0, PAST // CHUNK, ctx_keys, 0)

    def phase1(c, carry):
        r = pl.ds(pl.multiple_of(c * CHUNK, CHUNK), CHUNK)
        rope = (cos_ref[r, :], sin_ref[r, :]) if latent else None
        qa = _rms_rows(z_ref[r, 0:C_Q_LORA], gqa_ref[...]).astype(bf16)
        q = jnp.dot(qa, wuq_ref[...], preferred_element_type=f32)
        for h in range(C_HEADS):
            hs = slice(h * SLOT, (h + 1) * SLOT)
            qh = slot_norm(q[:, hs], gq_ref[...])
            if latent:
                qh = qh * rope[0] + _swap_pairs(qh) * rope[1]
            qs[r, hs] = (qh * C_SCALE).astype(bf16)
        ckv_n = _rms_rows(z_ref[r, C_Q_LORA:C_Q_LORA + C_KV_LORA], gkva_ref[...])
        if not latent:
            ckvo_ref[r, :] = ckv_n
        kr = pl.ds(pl.multiple_of(c * CHUNK + n_ctx, CHUNK), CHUNK)
        expand_keys(ckv_n, z_ref[r, C_Q_LORA + C_KV_LORA:ODD_IN_PAD], kr, rope)
        return carry

    lax.fori_loop(0, n_chunks, phase1, 0)

    def phase2(c, carry):
        r = pl.ds(pl.multiple_of(c * CHUNK, CHUNK), CHUNK)
        lane = lax.broadcasted_iota(jnp.int32, (CHUNK, 2 * C_V), 1)
        for pair in range(C_HEADS // 2):
            outs = []
            for h in (2 * pair, 2 * pair + 1):
                hs = slice(h * SLOT, (h + 1) * SLOT)
                s = lax.dot_general(qs[r, hs], ks[:, hs], (((1,), (1,)), ((), ())), preferred_element_type=f32)
                e = jnp.exp(s - jnp.max(s, -1, keepdims=True))
                den = jnp.sum(e, -1, keepdims=True)
                o = jnp.dot(e.astype(bf16), vs[:, pair * 2 * C_V:(pair + 1) * 2 * C_V], preferred_element_type=f32)
                outs.append(o / den)
            cat[r, pair * 2 * C_V:(pair + 1) * 2 * C_V] = jnp.where(lane < C_V, outs[0], outs[1]).astype(bf16)
        y = jnp.dot(cat[r, :], wo_ref[...], preferred_element_type=f32)
        xo_ref[r, :] = x_ref[r, :] + mod_ref[2:3, :] * y
        return carry

    lax.fori_loop(0, n_chunks, phase2, 0)


def _slot_cols(w, heads, width, lo, hi, lane0):
    k = w.shape[0]
    w3 = w.reshape(k, heads, width)[:, :, lo:hi]
    out = jnp.zeros((k, heads, SLOT), w.dtype).at[:, :, lane0:lane0 + (hi - lo)].set(w3)
    return out.reshape(k, heads * SLOT)


def _l1_weights(p):
    wuq = _slot_cols(p['w_uq'], C_HEADS, C_QK, 0, C_QK, 0).astype(bf16)
    wuk = _slot_cols(p['w_ukv'], C_HEADS, C_NOPE + C_V, 0, C_NOPE, 0).astype(bf16)
    wuv = p['w_ukv'].reshape(C_KV_LORA, C_HEADS, C_NOPE + C_V)[:, :, C_NOPE:].reshape(
        C_KV_LORA, C_HEADS * C_V).astype(bf16)
    gq = jnp.zeros((1, SLOT), f32).at[0, :C_QK].set(p['g_q'])
    gk = jnp.zeros((1, SLOT), f32).at[0, :C_QK].set(p['g_k'])
    return (p['g_qa'].reshape(1, C_Q_LORA), wuq, gq, p['g_kva'].reshape(1, C_KV_LORA), wuk, wuv, gk,
            p['w_o'].astype(bf16))


def _l1_mixer(z, x, mods, p, cache_ckv, cache_kpe):
    weights = _l1_weights(p)
    w_specs = [_const_spec(a.shape) for a in weights]

    def scratch(n, n_keys):
        return [pltpu.VMEM((n, D), bf16), pltpu.VMEM((n, C_HEADS * SLOT), bf16),
                pltpu.VMEM((n_keys, C_HEADS * SLOT), bf16), pltpu.VMEM((n_keys, C_HEADS * C_V), bf16)]

    xo_ctx, ckv_new = pl.pallas_call(
        functools.partial(_l1_kernel, n=SEQ, latent=False),
        grid=(BATCH,),
        in_specs=[pl.BlockSpec((SEQ, ODD_IN_PAD), lambda b: (b, 0)), pl.BlockSpec((SEQ, D), lambda b: (b, 0)),
                  pl.BlockSpec((8, D), lambda b: (0, 0))] + w_specs,
        out_specs=[pl.BlockSpec((SEQ, D), lambda b: (b, 0)), pl.BlockSpec((SEQ, C_KV_LORA), lambda b: (b, 0))],
        out_shape=[jax.ShapeDtypeStruct((T_CTX, D), f32), jax.ShapeDtypeStruct((T_CTX, C_KV_LORA), f32)],
        scratch_shapes=scratch(SEQ, SEQ),
        compiler_params=_cp(("parallel",)), name="l1_mixer_ctx",
    )(z, x, mods, *weights)

    cos, sin = _rope_tables(DEC_SEQ, C_ROPE, SLOT, C_NOPE)
    ckpe = jnp.zeros((DEC_BATCH, PAST, SLOT), f32).at[:, :, C_NOPE:C_QK].set(cache_kpe)
    off = T_CTX // DEC_SEQ
    xo_lat = pl.pallas_call(
        functools.partial(_l1_kernel, n=DEC_SEQ, latent=True),
        grid=(DEC_BATCH,),
        in_specs=[pl.BlockSpec((DEC_SEQ, ODD_IN_PAD), lambda b: (off + b, 0)),
                  pl.BlockSpec((DEC_SEQ, D), lambda b: (off + b, 0)),
                  pl.BlockSpec((8, D), lambda b: (1 + b, 0))] + w_specs + [
                  _const_spec(cos.shape), _const_spec(sin.shape),
                  pl.BlockSpec((None, PAST, C_KV_LORA), lambda b: (b, 0, 0)),
                  pl.BlockSpec((None, PAST, SLOT), lambda b: (b, 0, 0))],
        out_specs=pl.BlockSpec((DEC_SEQ, D), lambda b: (b, 0)),
        out_shape=jax.ShapeDtypeStruct((T_LAT, D), f32),
        scratch_shapes=scratch(DEC_SEQ, PAST + DEC_SEQ),
        compiler_params=_cp(("parallel",)), name="l1_mixer_lat",
    )(z, x, mods, *weights, cos, sin, cache_ckv, ckpe)
    return xo_ctx, xo_lat, ckv_new


def _router_kernel(xc_ref, xl_ref, gn_ref, mod_ref, wr_ref, br_ref, h_ref, route_ref):
    h = _rms_rows(_token_rows(xc_ref, xl_ref), gn_ref[...]) * (1.0 + mod_ref[4:5, :]) + mod_ref[3:4, :]
    _store_token_major(h_ref, h)
    logits = jnp.dot(h, wr_ref[...], precision=HIGHEST, preferred_element_type=f32) + br_ref[...]
    lane_i = lax.broadcasted_iota(jnp.int32, logits.shape, 1)
    lane = lane_i.astype(f32)
    big = 1e6
    is_g = (lane_i >= N_EXPERTS) & (lane_i < N_EXPERTS + N_GROUPS)
    lg = jnp.where(is_g, logits, -jnp.inf)
    mg = jnp.max(lg, -1, keepdims=True)
    gsel = jnp.min(jnp.where(lg == mg, lane, big), -1, keepdims=True) - N_EXPERTS
    pg_sel = 1.0 / jnp.sum(jnp.where(is_g, jnp.exp(lg - mg), 0.0), -1, keepdims=True)
    in_grp = (lane_i < N_EXPERTS) & ((lane_i >> 3).astype(f32) == gsel)
    le = jnp.where(in_grp, logits, -jnp.inf)
    m1 = jnp.max(le, -1, keepdims=True)
    i1 = jnp.min(jnp.where(le == m1, lane, big), -1, keepdims=True)
    le2 = jnp.where(lane == i1, -jnp.inf, le)
    m2 = jnp.max(le2, -1, keepdims=True)
    i2 = jnp.min(jnp.where(le2 == m2, lane, big), -1, keepdims=True)
    e2 = jnp.exp(m2 - m1)
    w1 = pg_sel / (1.0 + e2)
    w2 = pg_sel * e2 / (1.0 + e2)
    route = jnp.where(lane_i == 0, i1,
                      jnp.where(lane_i == 1, i2, jnp.where(lane_i == 2, w1, jnp.where(lane_i == 3, w2, 0.0))))
    route_ref[...] = route.T[0:8, :]


def _router(xc, xl, g_norm, mods, p):
    wr = jnp.zeros((D, 128), f32).at[:, :N_EXPERTS].set(p['w_re']).at[
        :, N_EXPERTS:N_EXPERTS + N_GROUPS].set(p['w_rg'])
    br = jnp.zeros((1, 128), f32).at[0, :N_EXPERTS].set(p['b_re']).at[
        0, N_EXPERTS:N_EXPERTS + N_GROUPS].set(p['b_rg'])
    row = pl.BlockSpec((ROW_TILE, D), lambda i: (i, 0))
    return pl.pallas_call(
        _router_kernel,
        grid=(T // ROW_TILE,),
        in_specs=_token_specs(D) + [
                  _const_spec((1, D)),
                  pl.BlockSpec((8, D), lambda i: (_sample_of_tile(i, ROW_TILE), 0)),
                  _const_spec((D, 128)), _const_spec((1, 128))],
        out_specs=[_token_major_spec(ROW_TILE, lambda i: (i, 0)), pl.BlockSpec((8, ROW_TILE), lambda i: (0, i))],
        out_shape=[jax.ShapeDtypeStruct((T * SUBS, LANES), f32), jax.ShapeDtypeStruct((8, T), f32)],
        compiler_params=_cp(("parallel",)), name="router",
    )(xc, xl, g_norm.reshape(1, D), mods, wr, br)


PAD_ASSIGN = 2 * T
TE_LANES = 256
PLAN_TILE_EXPERT, PLAN_N_USED, PLAN_PAD_LO, PLAN_PAD_HI = 0, 1, 2, 3


def _plan_kernel(rt_ref, pos_ref, te_ref, rank):
    n_blk = T // 128
    e_col = lax.broadcasted_iota(jnp.int32, (N_EXPERTS, 128), 0).astype(f32)
    ri = lax.broadcasted_iota(jnp.int32, (128, 128), 0)
    ci = lax.broadcasted_iota(jnp.int32, (128, 128), 1)
    before = jnp.where(ri < ci, 1.0, 0.0).astype(bf16)

    def picks(b):
        cs = slice(b * 128, (b + 1) * 128)
        return rt_ref[0:1, cs] == e_col, rt_ref[1:2, cs] == e_col

    counts = jnp.zeros((N_EXPERTS, 1), f32)
    for b in range(n_blk):
        m0, m1 = picks(b)
        m = jnp.where(m0, 1.0, 0.0) + jnp.where(m1, 1.0, 0.0)
        rank[:, b * 128:(b + 1) * 128] = jnp.dot(m.astype(bf16), before, preferred_element_type=f32) + counts
        counts = counts + jnp.sum(m, axis=1, keepdims=True)

    tiles = jnp.floor((counts + (MOE_TILE - 1.0)) * (1.0 / MOE_TILE))
    er = lax.broadcasted_iota(jnp.int32, (N_EXPERTS, N_EXPERTS), 0)
    ec = lax.broadcasted_iota(jnp.int32, (N_EXPERTS, N_EXPERTS), 1)
    earlier = jnp.where(ec < er, 1.0, 0.0).astype(bf16)
    tile_start = jnp.dot(earlier, jnp.broadcast_to(tiles, (N_EXPERTS, 128)).astype(bf16),
                         preferred_element_type=f32)
    row_start = tile_start * MOE_TILE

    sub = lax.broadcasted_iota(jnp.int32, (8, 128), 0)
    for b in range(n_blk):
        m0, m1 = picks(b)
        base = rank[:, b * 128:(b + 1) * 128] + row_start
        p0 = jnp.sum(jnp.where(m0, base, 0.0), axis=0, keepdims=True)
        p1 = jnp.sum(jnp.where(m1, base, 0.0), axis=0, keepdims=True)
        pos_ref[:, b * 128:(b + 1) * 128] = jnp.where(sub == 0, p0, jnp.where(sub == 1, p1, 0.0)).astype(jnp.int32)

    tile_end = tile_start + tiles
    n_used = jnp.max(tile_end, axis=0, keepdims=True)
    diag = (lax.broadcasted_iota(jnp.int32, (N_EXPERTS, 128), 0)
            == lax.broadcasted_iota(jnp.int32, (N_EXPERTS, 128), 1))
    pad_lo = jnp.sum(jnp.where(diag, row_start + counts, 0.0), axis=0, keepdims=True)
    pad_hi = jnp.sum(jnp.where(diag, tile_end * MOE_TILE, 0.0), axis=0, keepdims=True)
    for j in range(TE_LANES // 128):
        t = (lax.broadcasted_iota(jnp.int32, (N_EXPERTS, 128), 1) + j * 128).astype(f32)
        te = jnp.minimum(jnp.sum(jnp.where(tile_end <= t, 1.0, 0.0), axis=0, keepdims=True), N_EXPERTS - 1.0)
        rows = jnp.where(sub == PLAN_TILE_EXPERT, te, jnp.where(sub == PLAN_N_USED, n_used, 0.0))
        if j == 0:
            rows = jnp.where(sub == PLAN_PAD_LO, pad_lo, jnp.where(sub == PLAN_PAD_HI, pad_hi, rows))
        te_ref[:, j * 128:(j + 1) * 128] = rows.astype(jnp.int32)


def _invert_kernel(pos_ref, plan_ref, a_ref, off_ref):
    def pad(s, carry):
        a_ref[s] = PAD_ASSIGN
        off_ref[s] = T * SUBS
        return carry

    def pads(e, carry):
        return lax.fori_loop(plan_ref[PLAN_PAD_LO, e], plan_ref[PLAN_PAD_HI, e], pad, carry)
    lax.fori_loop(0, N_EXPERTS, pads, 0)
    lax.fori_loop(plan_ref[PLAN_N_USED, 0] * MOE_TILE, MOE_ROWS, pad, 0)

    for k in range(2):
        def place(i, carry):
            for u in range(8):
                t = i * 8 + u
                s = pos_ref[k * T + t]
                a_ref[s] = k * T + t
                off_ref[s] = t * SUBS
            return carry
        lax.fori_loop(0, T // 8, place, 0)


def _route_plan(route_t):
    pos, plan = pl.pallas_call(
        _plan_kernel,
        out_shape=[jax.ShapeDtypeStruct((8, T), jnp.int32), jax.ShapeDtypeStruct((8, TE_LANES), jnp.int32)],
        scratch_shapes=[pltpu.VMEM((N_EXPERTS, T), f32)],
        compiler_params=_cp(None), name="route_plan",
    )(route_t)
    smem = pl.BlockSpec(memory_space=pltpu.SMEM)
    a_slot, off_slot = pl.pallas_call(
        _invert_kernel,
        in_specs=[smem, smem], out_specs=[smem, smem],
        out_shape=[jax.ShapeDtypeStruct((MOE_ROWS,), jnp.int32)] * 2,
        name="route_invert",
    )(pos[0:2].reshape(2 * T), plan)
    gates = jnp.pad(route_t[2:4].reshape(2 * T), (0, 8))
    return plan[PLAN_TILE_EXPERT, :MOE_TILES], plan[PLAN_N_USED, :1], a_slot, off_slot, gates


def _first_tile_of_expert(i, te_ref):
    return (i == 0) | (te_ref[i] != te_ref[jnp.maximum(i - 1, 0)])


TM_ROWS = T * SUBS
SCATTER_GROUP = 16


def _moe_up_kernel(te_ref, nu_ref, off_ref, h_hbm, w1_ref, w3_ref, o_ref, xs, gbuf, w13, sem):
    i = pl.program_id(0)
    used = i < nu_ref[0]

    @pl.when(i == 0)
    def _():
        cp = pltpu.make_async_copy(h_hbm, xs.at[pl.ds(0, TM_ROWS), :], sem)
        cp.start()
        xs[TM_ROWS:TM_ROWS + SUBS, :] = jnp.zeros((SUBS, LANES), f32)
        cp.wait()

    @pl.when(used & _first_tile_of_expert(i, te_ref))
    def _():
        w13[:, :D_EXPERT] = w1_ref[0].astype(bf16)
        w13[:, D_EXPERT:] = w3_ref[0].astype(bf16)

    @pl.when(used)
    def _():
        base = i * MOE_TILE
        for r in range(MOE_TILE):
            off = pl.multiple_of(off_ref[base + r], SUBS)
            gbuf[r * SUBS:(r + 1) * SUBS, :] = xs[pl.ds(off, SUBS), :]
        x = _load_token_major(gbuf, MOE_TILE).astype(bf16)
        h13 = jnp.dot(x, w13[...], preferred_element_type=f32)
        o_ref[...] = (_silu(h13[:, :D_EXPERT]) * h13[:, D_EXPERT:]).astype(bf16)

    @pl.when(jnp.logical_not(used))
    def _():
        o_ref[...] = jnp.zeros_like(o_ref)


def _moe_down_kernel(te_ref, nu_ref, a_ref, off_ref, gate_ref, hh_ref, w2_ref, o_hbm, acc, ybuf, w2b, sem):
    i = pl.program_id(0)
    used = i < nu_ref[0]

    @pl.when(i == 0)
    def _():
        def zero(c, carry):
            acc[pl.ds(pl.multiple_of(c * 1024, 1024), 1024), :] = jnp.zeros((1024, LANES), f32)
            return carry
        lax.fori_loop(0, TM_ROWS // 1024, zero, 0)
        acc[TM_ROWS:TM_ROWS + SUBS, :] = jnp.zeros((SUBS, LANES), f32)

    @pl.when(used & _first_tile_of_expert(i, te_ref))
    def _():
        w2b[...] = w2_ref[0].astype(bf16)

    @pl.when(used)
    def _():
        _store_token_major(ybuf, jnp.dot(hh_ref[...], w2b[...], preferred_element_type=f32))
        base = i * MOE_TILE
        for g0 in range(0, MOE_TILE, SCATTER_GROUP):
            rows = range(g0, g0 + SCATTER_GROUP)
            offs = [pl.multiple_of(off_ref[base + r], SUBS) for r in rows]
            new = [acc[pl.ds(o, SUBS), :] + gate_ref[a_ref[base + r]] * ybuf[r * SUBS:(r + 1) * SUBS, :]
                   for r, o in zip(rows, offs)]
            for o, v in zip(offs, new):
                acc[pl.ds(o, SUBS), :] = v

    @pl.when(i == MOE_TILES - 1)
    def _():
        cp = pltpu.make_async_copy(acc.at[pl.ds(0, TM_ROWS), :], o_hbm, sem)
        cp.start()
        cp.wait()


def _moe(h_tm, route_t, p):
    tile_expert, n_used, a_slot, off_slot, gates = _route_plan(route_t)
    hh = pl.pallas_call(
        _moe_up_kernel,
        grid_spec=pltpu.PrefetchScalarGridSpec(
            num_scalar_prefetch=3, grid=(MOE_TILES,),
            in_specs=[pl.BlockSpec(memory_space=pl.ANY),
                      pl.BlockSpec((1, D, D_EXPERT), lambda i, te, nu, o: (te[i], 0, 0)),
                      pl.BlockSpec((1, D, D_EXPERT), lambda i, te, nu, o: (te[i], 0, 0))],
            out_specs=pl.BlockSpec((MOE_TILE, D_EXPERT), lambda i, te, nu, o: (i, 0)),
            scratch_shapes=[pltpu.VMEM((TM_ROWS + SUBS, LANES), f32), pltpu.VMEM((MOE_TILE * SUBS, LANES), f32),
                            pltpu.VMEM((D, 2 * D_EXPERT), bf16), pltpu.SemaphoreType.DMA(())]),
        out_shape=jax.ShapeDtypeStruct((MOE_ROWS, D_EXPERT), bf16),
        compiler_params=_cp(("arbitrary",)), name="moe_up",
    )(tile_expert, n_used, off_slot, h_tm, p['w1'], p['w3'])
    return pl.pallas_call(
        _moe_down_kernel,
        grid_spec=pltpu.PrefetchScalarGridSpec(
            num_scalar_prefetch=5, grid=(MOE_TILES,),
            in_specs=[pl.BlockSpec((MOE_TILE, D_EXPERT), lambda i, te, nu, a, o, g: (i, 0)),
                      pl.BlockSpec((1, D_EXPERT, D), lambda i, te, nu, a, o, g: (te[i], 0, 0))],
            out_specs=pl.BlockSpec(memory_space=pl.ANY),
            scratch_shapes=[pltpu.VMEM((TM_ROWS + SUBS, LANES), f32), pltpu.VMEM((MOE_TILE * SUBS, LANES), f32),
                            pltpu.VMEM((D_EXPERT, D), bf16), pltpu.SemaphoreType.DMA(())]),
        out_shape=jax.ShapeDtypeStruct((TM_ROWS, LANES), f32),
        compiler_params=_cp(("arbitrary",)), name="moe_down",
    )(tile_expert, n_used, a_slot, off_slot, gates, hh, p['w2'])


def _final_kernel(x_ref, d_ref, mod_ref, o_ref):
    o_ref[...] = x_ref[...] + mod_ref[5:6, :] * _load_token_major(d_ref, ROW_TILE)


def _final(x, delta, mods, row0, seq):
    n_rows = x.shape[0]
    t0 = row0 // ROW_TILE
    return pl.pallas_call(
        _final_kernel,
        grid=(n_rows // ROW_TILE,),
        in_specs=[pl.BlockSpec((ROW_TILE, D), lambda i: (i, 0)),
                  _token_major_spec(ROW_TILE, lambda i: (t0 + i, 0)),
                  pl.BlockSpec((8, D), lambda i: (_sample_of_tile(t0 + i, ROW_TILE), 0))],
        out_specs=pl.BlockSpec((ROW_TILE, D), lambda i: (i, 0)),
        out_shape=jax.ShapeDtypeStruct((n_rows, D), f32),
        compiler_params=_cp(("parallel",)), name="final",
    )(x, delta, mods).reshape(n_rows // seq, seq, D)


def kernel(x_prompt, x_sample, cache_l0_k, cache_l0_v, cache_l1_ckv, cache_l1_kpe, c, c_ctx, l0_g_norm1, l0_g_norm2, l0_w_ada, l0_b_ada, l0_w_in, l0_g_vnorm, l0_w_s, l0_b_s, l0_g_q, l0_g_k, l0_sink, l0_w_o, l0_w_rg, l0_b_rg, l0_w_re, l0_b_re, l0_w1, l0_w3, l0_w2, l1_g_norm1, l1_g_norm2, l1_w_ada, l1_b_ada, l1_w_in, l1_g_qa, l1_w_uq, l1_g_kva, l1_w_ukv, l1_g_q, l1_g_k, l1_w_o, l1_w_rg, l1_b_rg, l1_w_re, l1_b_re, l1_w1, l1_w3, l1_w2):
    p0 = dict(g_vnorm=l0_g_vnorm, w_s=l0_w_s, b_s=l0_b_s, g_q=l0_g_q, g_k=l0_g_k, sink=l0_sink, w_o=l0_w_o,
              w_rg=l0_w_rg, b_rg=l0_b_rg, w_re=l0_w_re, b_re=l0_b_re, w1=l0_w1, w3=l0_w3, w2=l0_w2)
    p1 = dict(g_qa=l1_g_qa, w_uq=l1_w_uq, g_kva=l1_g_kva, w_ukv=l1_w_ukv, g_q=l1_g_q, g_k=l1_g_k, w_o=l1_w_o,
              w_rg=l1_w_rg, b_rg=l1_b_rg, w_re=l1_w_re, b_re=l1_b_re, w1=l1_w1, w3=l1_w3, w2=l1_w2)

    cond8 = jnp.zeros((8, D), f32).at[0].set(c_ctx).at[1:1 + DEC_BATCH].set(c)
    mods0 = _mod_rows(_adaln(cond8, l0_w_ada, l0_b_ada))
    mods1 = _mod_rows(_adaln(cond8, l1_w_ada, l1_b_ada))

    xc0 = x_prompt.reshape(T_CTX, D)
    xl0 = x_sample.reshape(T_LAT, D)

    z0 = _proj(xc0, xl0, None, l0_g_norm1, mods0, l0_w_in.astype(bf16))
    xc0m, xl0m, k_new, v_new = _l0_mixer(z0, xc0, xl0, mods0, p0, cache_l0_k, cache_l0_v)
    h0, route0 = _router(xc0m, xl0m, l0_g_norm2, mods0, p0)
    moe0 = _moe(h0, route0, p0)

    w_in1 = jnp.zeros((D, ODD_IN_PAD), f32).at[:, :C_Q_LORA + C_KV_LORA].set(
        l1_w_in[:, :C_Q_LORA + C_KV_LORA]).at[
        :, C_Q_LORA + C_KV_LORA + C_NOPE:C_Q_LORA + C_KV_LORA + C_QK].set(l1_w_in[:, C_Q_LORA + C_KV_LORA:])
    mods01 = mods1.reshape(3, 8, D).at[:, 5].set(mods0.reshape(3, 8, D)[:, 5]).reshape(24, D)
    z1, x1 = _proj(xc0m, xl0m, moe0, l1_g_norm1, mods01, w_in1.astype(bf16))
    xc1m, xl1m, ckv_new = _l1_mixer(z1, x1, mods1, p1, cache_l1_ckv, cache_l1_kpe)
    h1, route1 = _router(xc1m, xl1m, l1_g_norm2, mods1, p1)
    moe1 = _moe(h1, route1, p1)

    y_prompt = _final(xc1m, moe1, mods1, 0, SEQ)
    y_sample = _final(xl1m, moe1, mods1, T_CTX, DEC_SEQ)
    kpe_new = z1[:T_CTX, C_Q_LORA + C_KV_LORA + C_NOPE:C_Q_LORA + C_KV_LORA + C_QK]
    return (y_prompt, y_sample,
            k_new.reshape(BATCH, SEQ, B_KV, B_HD), v_new.reshape(BATCH, SEQ, B_KV, B_HD),
            ckv_new.reshape(BATCH, SEQ, C_KV_LORA), kpe_new.reshape(BATCH, SEQ, C_ROPE))
```

```python
import functools

import jax
import jax.numpy as jnp
from jax import lax
from jax.experimental import pallas as pl
from jax.experimental.pallas import tpu as pltpu

f32 = jnp.float32
bf16 = jnp.bfloat16
HIGHEST = lax.Precision.HIGHEST

D = 1024
BATCH, SEQ = 32, 256
DEC_BATCH, DEC_SEQ = 2, 1024
PAST = 512
T_CTX = BATCH * SEQ
T_LAT = DEC_BATCH * DEC_SEQ
T = T_CTX + T_LAT
GRID_W = 64
CHUNK = 128
WINDOW = 128
ROPE_THETA = 10000.0
EPS = 1e-6
NEG_INF = -1e30
LANES = 128
SUBS = D // LANES

A_WIDTH = 512
A_GROUPS = 4
B_HEADS, B_KV, B_GROUP, B_HD = 8, 2, 4, 64
B_SCALE = B_HD ** -0.5

C_HEADS, C_Q_LORA, C_KV_LORA, C_NOPE, C_ROPE, C_V = 16, 384, 256, 64, 32, 64
C_QK = C_NOPE + C_ROPE
C_SCALE = C_QK ** -0.5
ODD_IN_PAD = 768
SLOT = 128

N_GROUPS, EPG, N_EXPERTS, D_EXPERT = 4, 8, 32, 256

ROW_TILE = 512
BLOCK_ROWS = 1024
PROJ_ROWS = 512
MOE_TILE = 256
MOE_ROWS = 2 * T + N_EXPERTS * MOE_TILE
MOE_TILES = MOE_ROWS // MOE_TILE
VMEM_CAP = 56 * 1024 * 1024


def _cp(sem, vmem=VMEM_CAP):
    return pltpu.CompilerParams(dimension_semantics=sem, vmem_limit_bytes=vmem)


def _const_spec(shape):
    nd = len(shape)
    return pl.BlockSpec(shape, lambda *_: (0,) * nd, pipeline_mode=pl.Buffered(1))


def _sample_of_tile(i, tile):
    n_ctx = T_CTX // tile
    per_lat = DEC_SEQ // tile
    return jnp.where(i < n_ctx, 0, 1 + (i - n_ctx) // per_lat)


def _silu(x):
    return x * jax.nn.sigmoid(x)


def _rms_rows(x, g):
    return x * lax.rsqrt(jnp.mean(x * x, -1, keepdims=True) + EPS) * g


def _swap_pairs(x):
    lane = lax.broadcasted_iota(jnp.int32, x.shape, x.ndim - 1)
    nxt = pltpu.roll(x, x.shape[-1] - 1, x.ndim - 1)
    prv = pltpu.roll(x, 1, x.ndim - 1)
    return jnp.where((lane & 1) == 0, nxt, prv)


def _swap_halves(x):
    return pltpu.roll(x, LANES // 2, x.ndim - 1)


def _adaln_kernel(c_ref, w_ref, b_ref, o_ref):
    s = _silu(c_ref[...])
    o_ref[...] = jnp.dot(s, w_ref[...], precision=HIGHEST, preferred_element_type=f32) + b_ref[...]


def _adaln(cond8, w, b):
    n = w.shape[1]
    tn = 1536
    return pl.pallas_call(
        _adaln_kernel,
        grid=(n // tn,),
        in_specs=[_const_spec((8, D)), pl.BlockSpec((D, tn), lambda j: (0, j)),
                  pl.BlockSpec((1, tn), lambda j: (0, j))],
        out_specs=pl.BlockSpec((8, tn), lambda j: (0, j)),
        out_shape=jax.ShapeDtypeStruct((8, n), f32),
        compiler_params=_cp(("arbitrary",)),
        name="adaln",
    )(cond8, w, b.reshape(1, n))


def _mod_rows(m8):
    m = m8[:3].reshape(3, 6, D)
    return jnp.pad(m, ((0, 0), (0, 2), (0, 0))).reshape(24, D)


N_CTX_TILES = T_CTX // ROW_TILE


def _token_specs(width):
    return [pl.BlockSpec((ROW_TILE, width), lambda i: (jnp.minimum(i, N_CTX_TILES - 1), 0)),
            pl.BlockSpec((ROW_TILE, width), lambda i: (jnp.maximum(i - N_CTX_TILES, 0), 0))]


def _token_rows(xc_ref, xl_ref):
    return jnp.where(pl.program_id(0) < N_CTX_TILES, xc_ref[...], xl_ref[...])


def _store_token_major(ref, x):
    n = x.shape[0]
    for s in range(SUBS):
        ref[pl.ds(s, n, stride=SUBS), :] = x[:, s * LANES:(s + 1) * LANES]


def _load_token_major(ref, n, row0=0):
    return jnp.concatenate([ref[pl.ds(row0 * SUBS + s, n, stride=SUBS), :] for s in range(SUBS)], axis=1)


def _token_major_spec(rows, index_map):
    return pl.BlockSpec((rows * SUBS, LANES), index_map)


def _rope_tables(n, rot_dim, lanes, lane0):
    rows_count = n // GRID_W
    rows = jnp.repeat(jnp.arange(rows_count), GRID_W).astype(f32)
    cols = jnp.tile(jnp.arange(GRID_W), rows_count).astype(f32)
    d_axis = rot_dim // 2
    inv = ROPE_THETA ** (-jnp.arange(0, d_axis, 2, dtype=f32) / d_axis)
    ang = jnp.concatenate([rows[:, None] * inv, cols[:, None] * inv], -1)
    cos = jnp.repeat(jnp.cos(ang), 2, axis=1)
    sin = jnp.repeat(jnp.sin(ang), 2, axis=1) * jnp.tile(jnp.array([-1.0, 1.0], f32), rot_dim // 2)
    c = jnp.ones((n, lanes), f32).at[:, lane0:lane0 + rot_dim].set(cos)
    s = jnp.zeros((n, lanes), f32).at[:, lane0:lane0 + rot_dim].set(sin)
    return c, s


L0_Q0 = 2 * A_WIDTH
L0_K0 = L0_Q0 + B_HEADS * LANES
L0_V0 = L0_K0 + B_KV * B_HD
L0_IN = L0_V0 + B_KV * B_HD


def _l0_kernel(*refs, latent):
    if latent:
        (sink_ref, x_ref, gn_ref, mod_ref, win_ref, gvn_ref, ws_ref, bsb_ref, gq_ref, gk_ref, wo_ref,
         cos_ref, sin_ref, kc_ref, vc_ref, xo_ref, zs, cat, qs, ks, vs, vw, kcb, vcb, vcw) = refs
        key_off = WINDOW
    else:
        (sink_ref, x_ref, gn_ref, mod_ref, win_ref, gvn_ref, ws_ref, bsb_ref, gq_ref, gk_ref, wo_ref,
         xo_ref, ko_ref, vo_ref, zs, cat, qs, ks, vs, vw) = refs
        key_off = 0
    n = BLOCK_ROWS
    low = lax.broadcasted_iota(jnp.int32, (CHUNK, LANES), 1) < B_HD

    if latent:
        zpad = jnp.zeros((WINDOW, LANES), bf16)
        for buf in (ks, vs, vw):
            buf[0:WINDOW, :] = zpad
            buf[WINDOW + n:2 * WINDOW + n, :] = zpad
        kcb[...] = kc_ref[...].astype(bf16)
        vcb[...] = vc_ref[...].astype(bf16)
        vcw[...] = _swap_halves(vc_ref[...]).astype(bf16)

    def project(c, carry):
        r = pl.ds(pl.multiple_of(c * PROJ_ROWS, PROJ_ROWS), PROJ_ROWS)
        h = _rms_rows(x_ref[r, :], gn_ref[...]) * (1.0 + mod_ref[1:2, :]) + mod_ref[0:1, :]
        zs[r, :] = jnp.dot(h.astype(bf16), win_ref[...], preferred_element_type=f32)
        return carry

    lax.fori_loop(0, n // PROJ_ROWS, project, 0)

    def prepare(c, carry):
        r = pl.ds(pl.multiple_of(c * CHUNK, CHUNK), CHUNK)
        u = jax.nn.gelu(zs[r, 0:A_WIDTH])
        v = jax.nn.gelu(zs[r, A_WIDTH:2 * A_WIDTH])
        mu = jnp.mean(v, -1, keepdims=True)
        var = jnp.mean(jnp.square(v - mu), -1, keepdims=True)
        vn = ((v - mu) * lax.rsqrt(var + EPS) * gvn_ref[...]).astype(bf16)
        for g in range(A_GROUPS):
            cs = slice(g * CHUNK, (g + 1) * CHUNK)
            mixed = jnp.dot(ws_ref[g], vn[:, cs], preferred_element_type=f32) + bsb_ref[g]
            cat[r, cs] = (u[:, cs] * mixed).astype(bf16)
        if latent:
            cs_, sn_ = cos_ref[r, :], sin_ref[r, :]
        for h in range(B_HEADS):
            hs = slice(h * LANES, (h + 1) * LANES)
            qh = zs[r, L0_Q0 + h * LANES:L0_Q0 + (h + 1) * LANES]
            qh = qh * lax.rsqrt(jnp.sum(qh * qh, -1, keepdims=True) * (1.0 / B_HD) + EPS) * gq_ref[:, hs]
            if latent:
                qh = qh * cs_ + _swap_pairs(qh) * sn_
            qs[r, hs] = (qh * B_SCALE).astype(bf16)
        k = zs[r, L0_K0:L0_K0 + LANES]
        k2 = k * k
        s0 = jnp.sum(jnp.where(low, k2, 0.0), -1, keepdims=True)
        s1 = jnp.sum(jnp.where(low, 0.0, k2), -1, keepdims=True)
        k = k * lax.rsqrt(jnp.where(low, s0, s1) * (1.0 / B_HD) + EPS) * gk_ref[...]
        vv = zs[r, L0_V0:L0_V0 + LANES]
        if latent:
            k = k * cs_ + _swap_pairs(k) * sn_
        else:
            ko_ref[r, :] = k
            vo_ref[r, :] = vv
        kr = pl.ds(pl.multiple_of(c * CHUNK + key_off, CHUNK), CHUNK)
        ks[kr, :] = k.astype(bf16)
        vs[kr, :] = vv.astype(bf16)
        vw[kr, :] = _swap_halves(vv).astype(bf16)
        return carry

    lax.fori_loop(0, n // CHUNK, prepare, 0)

    def attend(r, rows, key_sets):
        low_r = lax.broadcasted_iota(jnp.int32, (rows, LANES), 1) < B_HD
        for kvh in range(B_KV):
            heads = [kvh * B_GROUP + g for g in range(B_GROUP)]
            q = jnp.concatenate([qs[r, h * LANES:(h + 1) * LANES] for h in heads], axis=0)
            sk = jnp.concatenate([jnp.full((rows, 1), sink_ref[h], f32) for h in heads], axis=0)
            scores = []
            m = sk
            for k, _, _, keep in key_sets:
                s = lax.dot_general(q, k, (((1,), (1,)), ((), ())), preferred_element_type=f32)
                if keep is not None:
                    s = jnp.where(keep, s, NEG_INF)
                scores.append(s)
                m = jnp.maximum(m, jnp.max(s, -1, keepdims=True))
            den = jnp.exp(sk - m)
            o_plain = o_swapped = None
            for s, (_, v, w, _) in zip(scores, key_sets):
                e = jnp.exp(s - m)
                den = den + jnp.sum(e, -1, keepdims=True)
                eb = e.astype(bf16)
                pv = jnp.dot(eb, v, preferred_element_type=f32)
                pw = jnp.dot(eb, w, preferred_element_type=f32)
                o_plain = pv if o_plain is None else o_plain + pv
                o_swapped = pw if o_swapped is None else o_swapped + pw
            inv = 1.0 / den
            even_src, odd_src = (o_plain, o_swapped) if kvh == 0 else (o_swapped, o_plain)
            for j in range(B_GROUP // 2):
                ev = slice(2 * j * rows, (2 * j + 1) * rows)
                od = slice((2 * j + 1) * rows, (2 * j + 2) * rows)
                pair = kvh * (B_GROUP // 2) + j
                cat[r, A_WIDTH + pair * LANES:A_WIDTH + (pair + 1) * LANES] = jnp.where(
                    low_r, even_src[ev] * inv[ev], odd_src[od] * inv[od]).astype(bf16)

    if latent:
        span = CHUNK + 2 * WINDOW

        def attend_block(c, carry):
            start = pl.multiple_of(c * CHUNK, CHUNK)
            kr = pl.ds(start, span)
            qi = lax.broadcasted_iota(jnp.int32, (B_GROUP * CHUNK, span), 0) & (CHUNK - 1)
            kj = lax.broadcasted_iota(jnp.int32, (B_GROUP * CHUNK, span), 1)
            kpos = start - WINDOW + kj
            keep = (jnp.abs(kj - WINDOW - qi) <= WINDOW) & (kpos >= 0) & (kpos < n)
            attend(pl.ds(start, CHUNK), CHUNK,
                   [(ks[kr, :], vs[kr, :], vw[kr, :], keep), (kcb[...], vcb[...], vcw[...], None)])
            return carry

        lax.fori_loop(0, n // CHUNK, attend_block, 0)
    else:
        def attend_seq(sq, carry):
            r = pl.ds(pl.multiple_of(sq * SEQ, SEQ), SEQ)
            attend(r, SEQ, [(ks[r, :], vs[r, :], vw[r, :], None)])
            return carry

        lax.fori_loop(0, n // SEQ, attend_seq, 0)

    def output(c, carry):
        r = pl.ds(pl.multiple_of(c * PROJ_ROWS, PROJ_ROWS), PROJ_ROWS)
        y = jnp.dot(cat[r, :], wo_ref[...], preferred_element_type=f32)
        xo_ref[r, :] = x_ref[r, :] + mod_ref[2:3, :] * y
        return carry

    lax.fori_loop(0, n // PROJ_ROWS, output, 0)


def _l0_mixer(xc, xl, g_norm, mods, p, cache_k, cache_v):
    w = p['w_in']
    q = w[:, 2 * A_WIDTH:2 * A_WIDTH + B_HEADS * B_HD].reshape(D, B_HEADS, B_HD)
    zero = jnp.zeros((D, B_GROUP, B_HD), f32)
    q_slots = jnp.concatenate([jnp.concatenate([q[:, :B_GROUP], zero], axis=2),
                               jnp.concatenate([zero, q[:, B_GROUP:]], axis=2)], axis=1).reshape(D, B_HEADS * LANES)
    win = jnp.concatenate([w[:, :2 * A_WIDTH], q_slots, w[:, 2 * A_WIDTH + B_HEADS * B_HD:]], axis=1).astype(bf16)
    zg = jnp.zeros((B_HD,), f32)
    gq = jnp.concatenate([jnp.tile(jnp.concatenate([p['g_q'], zg]), B_GROUP),
                          jnp.tile(jnp.concatenate([zg, p['g_q']]), B_GROUP)]).reshape(1, B_HEADS * LANES)
    gk = jnp.tile(p['g_k'], B_KV).reshape(1, LANES)
    gvn = p['g_vnorm'].reshape(1, A_WIDTH)
    ws = p['w_s'].astype(bf16)
    bsb = jnp.broadcast_to(p['b_s'][:, :, None], (A_GROUPS, CHUNK, CHUNK))
    wo = p['w_o'].astype(bf16)
    weights = (g_norm.reshape(1, D),)
    consts = (win, gvn, ws, bsb, gq, gk, wo)
    c_specs = [_const_spec(a.shape) for a in consts]
    smem = pl.BlockSpec(memory_space=pltpu.SMEM)
    row = pl.BlockSpec((BLOCK_ROWS, D), lambda b: (b, 0))
    kv = pl.BlockSpec((BLOCK_ROWS, LANES), lambda b: (b, 0))

    def scratch(pad):
        keys = pltpu.VMEM((BLOCK_ROWS + pad, LANES), bf16)
        return [pltpu.VMEM((BLOCK_ROWS, L0_IN), f32), pltpu.VMEM((BLOCK_ROWS, D), bf16),
                pltpu.VMEM((BLOCK_ROWS, B_HEADS * LANES), bf16), keys, keys, keys]

    kv_shape = jax.ShapeDtypeStruct((T_CTX, LANES), f32)
    xo_ctx, k_new, v_new = pl.pallas_call(
        functools.partial(_l0_kernel, latent=False),
        grid=(T_CTX // BLOCK_ROWS,),
        in_specs=[smem, row, _const_spec((1, D)), pl.BlockSpec((8, D), lambda b: (0, 0))] + c_specs,
        out_specs=[row, kv, kv],
        out_shape=[jax.ShapeDtypeStruct((T_CTX, D), f32), kv_shape, kv_shape],
        scratch_shapes=scratch(0),
        compiler_params=_cp(("parallel",)), name="l0_mixer_ctx",
    )(p['sink'], xc, *weights, mods, *consts)

    cos, sin = _rope_tables(DEC_SEQ, B_HD, LANES, 0)
    cos = cos.at[:, B_HD:].set(cos[:, :B_HD])
    sin = sin.at[:, B_HD:].set(sin[:, :B_HD])
    cache = pl.BlockSpec((None, PAST, LANES), lambda b: (b, 0, 0))
    past = pltpu.VMEM((PAST, LANES), bf16)
    xo_lat = pl.pallas_call(
        functools.partial(_l0_kernel, latent=True),
        grid=(DEC_BATCH,),
        in_specs=[smem, row, _const_spec((1, D)), pl.BlockSpec((8, D), lambda b: (1 + b, 0))] + c_specs + [
                  _const_spec(cos.shape), _const_spec(sin.shape), cache, cache],
        out_specs=row,
        out_shape=jax.ShapeDtypeStruct((T_LAT, D), f32),
        scratch_shapes=scratch(2 * WINDOW) + [past, past, past],
        compiler_params=_cp(("parallel",)), name="l0_mixer_lat",
    )(p['sink'], xl, *weights, mods, *consts, cos, sin,
      cache_k.reshape(DEC_BATCH, PAST, LANES), cache_v.reshape(DEC_BATCH, PAST, LANES))
    return xo_ctx, xo_lat, k_new, v_new


C_SLOTS = C_HEADS * SLOT
C_PAIRS = C_HEADS // 2
L1_ROWS = 256


def _l1_kernel(*refs, latent):
    if latent:
        (x_ref, d_ref, gn_ref, mod_ref, win_ref, gqa_ref, wuq_ref, gq_ref, gkva_ref, wuk_ref, wuv_ref, gk_ref,
         wo_ref, cos_ref, sin_ref, cckv_ref, ckpe_ref, xo_ref, zs, cat, qs, ks, vs, wide) = refs
        n_ctx = PAST
    else:
        (x_ref, d_ref, gn_ref, mod_ref, win_ref, gqa_ref, wuq_ref, gq_ref, gkva_ref, wuk_ref, wuv_ref, gk_ref,
         wo_ref, xo_ref, ckvo_ref, kpeo_ref, zs, cat, qs, ks, vs, wide) = refs
        n_ctx = 0
    n = BLOCK_ROWS

    def slot_norm(v, g):
        return v * lax.rsqrt(jnp.sum(v * v, -1, keepdims=True) * (1.0 / C_QK) + EPS) * g

    def expand_keys(ckv_n, kslot, key_rows, rope):
        cb = ckv_n.astype(bf16)
        wide[...] = jnp.dot(cb, wuk_ref[...], preferred_element_type=f32)
        for h in range(C_HEADS):
            hs = slice(h * SLOT, (h + 1) * SLOT)
            kh = slot_norm(wide[:, hs] + kslot, gk_ref[...])
            if rope is not None:
                kh = kh * rope[0] + _swap_pairs(kh) * rope[1]
            ks[h, key_rows, :] = kh.astype(bf16)
        v = jnp.dot(cb, wuv_ref[...], preferred_element_type=f32).astype(bf16)
        for pair in range(C_PAIRS):
            vs[pair, key_rows, :] = v[:, pair * LANES:(pair + 1) * LANES]

    if latent:
        def past_keys(c, carry):
            r = pl.ds(pl.multiple_of(c * L1_ROWS, L1_ROWS), L1_ROWS)
            expand_keys(cckv_ref[r, :], ckpe_ref[r, :], r, None)
            return carry

        lax.fori_loop(0, PAST // L1_ROWS, past_keys, 0)

    for c in range(n // PROJ_ROWS):
        r = pl.ds(c * PROJ_ROWS, PROJ_ROWS)
        x1 = x_ref[r, :] + mod_ref[5:6, :] * _load_token_major(d_ref, PROJ_ROWS, c * PROJ_ROWS)
        xo_ref[r, :] = x1
        h = _rms_rows(x1, gn_ref[...]) * (1.0 + mod_ref[1:2, :]) + mod_ref[0:1, :]
        zs[r, :] = jnp.dot(h.astype(bf16), win_ref[...], preferred_element_type=f32)

    def prepare(c, carry):
        r = pl.ds(pl.multiple_of(c * L1_ROWS, L1_ROWS), L1_ROWS)
        rope = (cos_ref[r, :], sin_ref[r, :]) if latent else None
        qa = _rms_rows(zs[r, 0:C_Q_LORA], gqa_ref[...]).astype(bf16)
        wide[...] = jnp.dot(qa, wuq_ref[...], preferred_element_type=f32)
        for h in range(C_HEADS):
            hs = slice(h * SLOT, (h + 1) * SLOT)
            qh = slot_norm(wide[:, hs], gq_ref[...])
            if latent:
                qh = qh * rope[0] + _swap_pairs(qh) * rope[1]
            qs[h, r, :] = (qh * C_SCALE).astype(bf16)
        ckv_n = _rms_rows(zs[r, C_Q_LORA:C_Q_LORA + C_KV_LORA], gkva_ref[...])
        kslot = zs[r, C_Q_LORA + C_KV_LORA:ODD_IN_PAD]
        if not latent:
            ckvo_ref[r, :] = ckv_n
            kpeo_ref[r, :] = kslot
        kr = pl.ds(pl.multiple_of(c * L1_ROWS + n_ctx, L1_ROWS), L1_ROWS)
        expand_keys(ckv_n, kslot, kr, rope)
        return carry

    lax.fori_loop(0, n // L1_ROWS, prepare, 0)

    low = lax.broadcasted_iota(jnp.int32, (L1_ROWS, 2 * C_V), 1) < C_V

    def attend(c, carry):
        r = pl.ds(pl.multiple_of(c * L1_ROWS, L1_ROWS), L1_ROWS)
        kr = slice(None) if latent else r

        def head_pair(pair, carry2):
            outs = []
            for h in (2 * pair, 2 * pair + 1):
                s = lax.dot_general(qs[h, r, :], ks[h, kr, :], (((1,), (1,)), ((), ())),
                                    preferred_element_type=f32)
                e = jnp.exp(s - jnp.max(s, -1, keepdims=True))
                den = jnp.sum(e, -1, keepdims=True)
                outs.append(jnp.dot(e.astype(bf16), vs[pair, kr, :], preferred_element_type=f32) / den)
            cat[pair, r, :] = jnp.where(low, outs[0], outs[1]).astype(bf16)
            return carry2

        return lax.fori_loop(0, C_PAIRS, head_pair, carry)

    lax.fori_loop(0, n // L1_ROWS, attend, 0)

    def output(c, carry):
        r = pl.ds(pl.multiple_of(c * PROJ_ROWS, PROJ_ROWS), PROJ_ROWS)
        heads = jnp.concatenate([cat[pair, r, :] for pair in range(C_PAIRS)], axis=1)
        y = jnp.dot(heads, wo_ref[...], preferred_element_type=f32)
        xo_ref[r, :] = xo_ref[r, :] + mod_ref[2:3, :] * y
        return carry

    lax.fori_loop(0, n // PROJ_ROWS, output, 0)


def _slot_cols(w, heads, width, lo, hi, lane0):
    k = w.shape[0]
    w3 = w.reshape(k, heads, width)[:, :, lo:hi]
    out = jnp.zeros((k, heads, SLOT), w.dtype).at[:, :, lane0:lane0 + (hi - lo)].set(w3)
    return out.reshape(k, heads * SLOT)


def _l1_mixer(xc, xl, moe_prev, g_norm, mods, p, cache_ckv, cache_kpe):
    w_in = jnp.zeros((D, ODD_IN_PAD), f32).at[:, :C_Q_LORA + C_KV_LORA].set(
        p['w_in'][:, :C_Q_LORA + C_KV_LORA]).at[
        :, C_Q_LORA + C_KV_LORA + C_NOPE:C_Q_LORA + C_KV_LORA + C_QK].set(p['w_in'][:, C_Q_LORA + C_KV_LORA:])
    wuq = _slot_cols(p['w_uq'], C_HEADS, C_QK, 0, C_QK, 0).astype(bf16)
    wuk = _slot_cols(p['w_ukv'], C_HEADS, C_NOPE + C_V, 0, C_NOPE, 0).astype(bf16)
    wuv = p['w_ukv'].reshape(C_KV_LORA, C_HEADS, C_NOPE + C_V)[:, :, C_NOPE:].reshape(
        C_KV_LORA, C_HEADS * C_V).astype(bf16)
    gq = jnp.zeros((1, SLOT), f32).at[0, :C_QK].set(p['g_q'])
    gk = jnp.zeros((1, SLOT), f32).at[0, :C_QK].set(p['g_k'])
    consts = (g_norm.reshape(1, D), w_in.astype(bf16), p['g_qa'].reshape(1, C_Q_LORA), wuq, gq,
              p['g_kva'].reshape(1, C_KV_LORA), wuk, wuv, gk, p['w_o'].astype(bf16))
    c_specs = [_const_spec(a.shape) for a in consts]
    row = pl.BlockSpec((BLOCK_ROWS, D), lambda b: (b, 0))
    n_ctx_blocks = T_CTX // BLOCK_ROWS

    def scratch(n_keys):
        return [pltpu.VMEM((BLOCK_ROWS, ODD_IN_PAD), f32), pltpu.VMEM((C_PAIRS, BLOCK_ROWS, LANES), bf16),
                pltpu.VMEM((C_HEADS, BLOCK_ROWS, SLOT), bf16), pltpu.VMEM((C_HEADS, n_keys, SLOT), bf16),
                pltpu.VMEM((C_PAIRS, n_keys, LANES), bf16), pltpu.VMEM((L1_ROWS, C_SLOTS), f32)]

    xo_ctx, ckv_new, kpe_slot = pl.pallas_call(
        functools.partial(_l1_kernel, latent=False),
        grid=(n_ctx_blocks,),
        in_specs=[row, _token_major_spec(BLOCK_ROWS, lambda b: (b, 0)), c_specs[0],
                  pl.BlockSpec((8, D), lambda b: (0, 0))] + c_specs[1:],
        out_specs=[row, pl.BlockSpec((BLOCK_ROWS, C_KV_LORA), lambda b: (b, 0)),
                   pl.BlockSpec((BLOCK_ROWS, SLOT), lambda b: (b, 0))],
        out_shape=[jax.ShapeDtypeStruct((T_CTX, D), f32), jax.ShapeDtypeStruct((T_CTX, C_KV_LORA), f32),
                   jax.ShapeDtypeStruct((T_CTX, SLOT), f32)],
        scratch_shapes=scratch(BLOCK_ROWS),
        compiler_params=_cp(("parallel",)), name="l1_mixer_ctx",
    )(xc, moe_prev, consts[0], mods, *consts[1:])

    cos, sin = _rope_tables(DEC_SEQ, C_ROPE, SLOT, C_NOPE)
    ckpe = jnp.zeros((DEC_BATCH, PAST, SLOT), f32).at[:, :, C_NOPE:C_QK].set(cache_kpe)
    xo_lat = pl.pallas_call(
        functools.partial(_l1_kernel, latent=True),
        grid=(DEC_BATCH,),
        in_specs=[pl.BlockSpec((BLOCK_ROWS, D), lambda b: (b, 0), pipeline_mode=pl.Buffered(1)),
                  pl.BlockSpec((BLOCK_ROWS * SUBS, LANES), lambda b: (n_ctx_blocks + b, 0),
                               pipeline_mode=pl.Buffered(1)), c_specs[0],
                  pl.BlockSpec((8, D), lambda b: (1 + b, 0))] + c_specs[1:] + [
                  _const_spec(cos.shape), _const_spec(sin.shape),
                  pl.BlockSpec((None, PAST, C_KV_LORA), lambda b: (b, 0, 0)),
                  pl.BlockSpec((None, PAST, SLOT), lambda b: (b, 0, 0))],
        out_specs=row,
        out_shape=jax.ShapeDtypeStruct((T_LAT, D), f32),
        scratch_shapes=scratch(PAST + BLOCK_ROWS),
        compiler_params=_cp(("parallel",)), name="l1_mixer_lat",
    )(xl, moe_prev, consts[0], mods, *consts[1:], cos, sin, cache_ckv, ckpe)
    return xo_ctx, xo_lat, ckv_new, kpe_slot[:, C_NOPE:C_QK]


def _router_kernel(xc_ref, xl_ref, gn_ref, mod_ref, wr_ref, br_ref, h_ref, route_ref):
    h = _rms_rows(_token_rows(xc_ref, xl_ref), gn_ref[...]) * (1.0 + mod_ref[4:5, :]) + mod_ref[3:4, :]
    _store_token_major(h_ref, h)
    logits = jnp.dot(h, wr_ref[...], precision=HIGHEST, preferred_element_type=f32) + br_ref[...]
    lane_i = lax.broadcasted_iota(jnp.int32, logits.shape, 1)
    lane = lane_i.astype(f32)
    big = 1e6
    is_g = (lane_i >= N_EXPERTS) & (lane_i < N_EXPERTS + N_GROUPS)
    lg = jnp.where(is_g, logits, -jnp.inf)
    mg = jnp.max(lg, -1, keepdims=True)
    gsel = jnp.min(jnp.where(lg == mg, lane, big), -1, keepdims=True) - N_EXPERTS
    pg_sel = 1.0 / jnp.sum(jnp.where(is_g, jnp.exp(lg - mg), 0.0), -1, keepdims=True)
    in_grp = (lane_i < N_EXPERTS) & ((lane_i >> 3).astype(f32) == gsel)
    le = jnp.where(in_grp, logits, -jnp.inf)
    m1 = jnp.max(le, -1, keepdims=True)
    i1 = jnp.min(jnp.where(le == m1, lane, big), -1, keepdims=True)
    le2 = jnp.where(lane == i1, -jnp.inf, le)
    m2 = jnp.max(le2, -1, keepdims=True)
    i2 = jnp.min(jnp.where(le2 == m2, lane, big), -1, keepdims=True)
    e2 = jnp.exp(m2 - m1)
    w1 = pg_sel / (1.0 + e2)
    w2 = pg_sel * e2 / (1.0 + e2)
    route = jnp.where(lane_i == 0, i1,
                      jnp.where(lane_i == 1, i2, jnp.where(lane_i == 2, w1, jnp.where(lane_i == 3, w2, 0.0))))
    route_ref[...] = route.T[0:8, :]


def _router(xc, xl, g_norm, mods, p):
    wr = jnp.zeros((D, 128), f32).at[:, :N_EXPERTS].set(p['w_re']).at[
        :, N_EXPERTS:N_EXPERTS + N_GROUPS].set(p['w_rg'])
    br = jnp.zeros((1, 128), f32).at[0, :N_EXPERTS].set(p['b_re']).at[
        0, N_EXPERTS:N_EXPERTS + N_GROUPS].set(p['b_rg'])
    return pl.pallas_call(
        _router_kernel,
        grid=(T // ROW_TILE,),
        in_specs=_token_specs(D) + [
                  _const_spec((1, D)),
                  pl.BlockSpec((8, D), lambda i: (_sample_of_tile(i, ROW_TILE), 0)),
                  _const_spec((D, 128)), _const_spec((1, 128))],
        out_specs=[_token_major_spec(ROW_TILE, lambda i: (i, 0)), pl.BlockSpec((8, ROW_TILE), lambda i: (0, i))],
        out_shape=[jax.ShapeDtypeStruct((T * SUBS, LANES), f32), jax.ShapeDtypeStruct((8, T), f32)],
        compiler_params=_cp(("parallel",)), name="router",
    )(xc, xl, g_norm.reshape(1, D), mods, wr, br)


TE_LANES = 256
PLAN_TILE_EXPERT, PLAN_N_USED, PLAN_PAD_LO, PLAN_PAD_HI = 0, 1, 2, 3


def _plan_kernel(rt_ref, pos_ref, te_ref, rank):
    n_blk = T // 128
    e_col = lax.broadcasted_iota(jnp.int32, (N_EXPERTS, 128), 0).astype(f32)
    ri = lax.broadcasted_iota(jnp.int32, (128, 128), 0)
    ci = lax.broadcasted_iota(jnp.int32, (128, 128), 1)
    before = jnp.where(ri < ci, 1.0, 0.0).astype(bf16)

    def picks(b):
        cs = slice(b * 128, (b + 1) * 128)
        return rt_ref[0:1, cs] == e_col, rt_ref[1:2, cs] == e_col

    counts = jnp.zeros((N_EXPERTS, 1), f32)
    for b in range(n_blk):
        m0, m1 = picks(b)
        m = jnp.where(m0, 1.0, 0.0) + jnp.where(m1, 1.0, 0.0)
        rank[:, b * 128:(b + 1) * 128] = jnp.dot(m.astype(bf16), before, preferred_element_type=f32) + counts
        counts = counts + jnp.sum(m, axis=1, keepdims=True)

    tiles = jnp.floor((counts + (MOE_TILE - 1.0)) * (1.0 / MOE_TILE))
    er = lax.broadcasted_iota(jnp.int32, (N_EXPERTS, N_EXPERTS), 0)
    ec = lax.broadcasted_iota(jnp.int32, (N_EXPERTS, N_EXPERTS), 1)
    earlier = jnp.where(ec < er, 1.0, 0.0).astype(bf16)
    tile_start = jnp.dot(earlier, jnp.broadcast_to(tiles, (N_EXPERTS, 128)).astype(bf16),
                         preferred_element_type=f32)
    row_start = tile_start * MOE_TILE

    sub = lax.broadcasted_iota(jnp.int32, (8, 128), 0)
    for b in range(n_blk):
        m0, m1 = picks(b)
        base = rank[:, b * 128:(b + 1) * 128] + row_start
        p0 = jnp.sum(jnp.where(m0, base, 0.0), axis=0, keepdims=True)
        p1 = jnp.sum(jnp.where(m1, base, 0.0), axis=0, keepdims=True)
        pos_ref[:, b * 128:(b + 1) * 128] = jnp.where(sub == 0, p0, jnp.where(sub == 1, p1, 0.0)).astype(jnp.int32)

    tile_end = tile_start + tiles
    n_used = jnp.max(tile_end, axis=0, keepdims=True)
    diag = (lax.broadcasted_iota(jnp.int32, (N_EXPERTS, 128), 0)
            == lax.broadcasted_iota(jnp.int32, (N_EXPERTS, 128), 1))
    pad_lo = jnp.sum(jnp.where(diag, row_start + counts, 0.0), axis=0, keepdims=True)
    pad_hi = jnp.sum(jnp.where(diag, tile_end * MOE_TILE, 0.0), axis=0, keepdims=True)
    for j in range(TE_LANES // 128):
        t = (lax.broadcasted_iota(jnp.int32, (N_EXPERTS, 128), 1) + j * 128).astype(f32)
        te = jnp.minimum(jnp.sum(jnp.where(tile_end <= t, 1.0, 0.0), axis=0, keepdims=True), N_EXPERTS - 1.0)
        rows = jnp.where(sub == PLAN_TILE_EXPERT, te, jnp.where(sub == PLAN_N_USED, n_used, 0.0))
        if j == 0:
            rows = jnp.where(sub == PLAN_PAD_LO, pad_lo, jnp.where(sub == PLAN_PAD_HI, pad_hi, rows))
        te_ref[:, j * 128:(j + 1) * 128] = rows.astype(jnp.int32)


def _slot_code(t, k):
    return t * SUBS + k * (SUBS // 2)


def _code_offset(code):
    return pl.multiple_of(code & ~(SUBS - 1), SUBS)


def _code_gate_index(code):
    return code >> 2


PAD_CODE = T * SUBS


def _invert_kernel(pos_ref, plan_ref, code_ref):
    def pad(s, carry):
        code_ref[s] = PAD_CODE
        return carry

    def pads(e, carry):
        return lax.fori_loop(plan_ref[PLAN_PAD_LO, e], plan_ref[PLAN_PAD_HI, e], pad, carry)
    lax.fori_loop(0, N_EXPERTS, pads, 0)
    lax.fori_loop(plan_ref[PLAN_N_USED, 0] * MOE_TILE, MOE_ROWS, pad, 0)

    for k in range(2):
        def place(i, carry):
            for u in range(8):
                t = i * 8 + u
                code_ref[pos_ref[k * T + t]] = _slot_code(t, k)
            return carry
        lax.fori_loop(0, T // 8, place, 0)


def _route_plan(route_t):
    pos, plan = pl.pallas_call(
        _plan_kernel,
        out_shape=[jax.ShapeDtypeStruct((8, T), jnp.int32), jax.ShapeDtypeStruct((8, TE_LANES), jnp.int32)],
        scratch_shapes=[pltpu.VMEM((N_EXPERTS, T), f32)],
        compiler_params=_cp(None), name="route_plan",
    )(route_t)
    smem = pl.BlockSpec(memory_space=pltpu.SMEM)
    codes = pl.pallas_call(
        _invert_kernel,
        in_specs=[smem, smem], out_specs=smem,
        out_shape=jax.ShapeDtypeStruct((MOE_ROWS,), jnp.int32),
        name="route_invert",
    )(pos[0:2].reshape(2 * T), plan)
    gates = jnp.pad(route_t[2:4].T.reshape(2 * T), (0, 8))
    return plan[PLAN_TILE_EXPERT, :MOE_TILES], plan[PLAN_N_USED, :1], codes, gates


def _first_tile_of_expert(i, te_ref):
    return (i == 0) | (te_ref[i] != te_ref[jnp.maximum(i - 1, 0)])


TM_ROWS = T * SUBS
SCATTER_GROUP = 16


def _moe_up_kernel(te_ref, nu_ref, code_ref, h_hbm, w1_ref, w3_ref, o_ref, xs, gbuf, w13, sem):
    i = pl.program_id(0)
    used = i < nu_ref[0]

    @pl.when(i == 0)
    def _():
        cp = pltpu.make_async_copy(h_hbm, xs.at[pl.ds(0, TM_ROWS), :], sem)
        cp.start()
        xs[TM_ROWS:TM_ROWS + SUBS, :] = jnp.zeros((SUBS, LANES), f32)
        cp.wait()

    @pl.when(used & _first_tile_of_expert(i, te_ref))
    def _():
        w13[:, :D_EXPERT] = w1_ref[0].astype(bf16)
        w13[:, D_EXPERT:] = w3_ref[0].astype(bf16)

    @pl.when(used)
    def _():
        base = i * MOE_TILE
        for r in range(MOE_TILE):
            gbuf[r * SUBS:(r + 1) * SUBS, :] = xs[pl.ds(_code_offset(code_ref[base + r]), SUBS), :]
        x = _load_token_major(gbuf, MOE_TILE).astype(bf16)
        h13 = jnp.dot(x, w13[...], preferred_element_type=f32)
        o_ref[...] = (_silu(h13[:, :D_EXPERT]) * h13[:, D_EXPERT:]).astype(bf16)

    @pl.when(jnp.logical_not(used))
    def _():
        o_ref[...] = jnp.zeros_like(o_ref)


def _moe_down_kernel(te_ref, nu_ref, code_ref, gate_ref, hh_ref, w2_ref, o_hbm, acc, ybuf, w2b, sem):
    i = pl.program_id(0)
    used = i < nu_ref[0]

    @pl.when(i == 0)
    def _():
        def zero(c, carry):
            acc[pl.ds(pl.multiple_of(c * 1024, 1024), 1024), :] = jnp.zeros((1024, LANES), f32)
            return carry
        lax.fori_loop(0, TM_ROWS // 1024, zero, 0)
        acc[TM_ROWS:TM_ROWS + SUBS, :] = jnp.zeros((SUBS, LANES), f32)

    @pl.when(used & _first_tile_of_expert(i, te_ref))
    def _():
        w2b[...] = w2_ref[0].astype(bf16)

    @pl.when(used)
    def _():
        _store_token_major(ybuf, jnp.dot(hh_ref[...], w2b[...], preferred_element_type=f32))
        base = i * MOE_TILE
        for g0 in range(0, MOE_TILE, SCATTER_GROUP):
            rows = range(g0, g0 + SCATTER_GROUP)
            codes = [code_ref[base + r] for r in rows]
            new = [acc[pl.ds(_code_offset(c), SUBS), :]
                   + gate_ref[_code_gate_index(c)] * ybuf[r * SUBS:(r + 1) * SUBS, :] for r, c in zip(rows, codes)]
            for c, v in zip(codes, new):
                acc[pl.ds(_code_offset(c), SUBS), :] = v

    @pl.when(i == MOE_TILES - 1)
    def _():
        cp = pltpu.make_async_copy(acc.at[pl.ds(0, TM_ROWS), :], o_hbm, sem)
        cp.start()
        cp.wait()


def _moe(h_tm, route_t, p):
    tile_expert, n_used, codes, gates = _route_plan(route_t)
    hh = pl.pallas_call(
        _moe_up_kernel,
        grid_spec=pltpu.PrefetchScalarGridSpec(
            num_scalar_prefetch=3, grid=(MOE_TILES,),
            in_specs=[pl.BlockSpec(memory_space=pl.ANY),
                      pl.BlockSpec((1, D, D_EXPERT), lambda i, te, nu, c: (te[i], 0, 0)),
                      pl.BlockSpec((1, D, D_EXPERT), lambda i, te, nu, c: (te[i], 0, 0))],
            out_specs=pl.BlockSpec((MOE_TILE, D_EXPERT), lambda i, te, nu, c: (i, 0)),
            scratch_shapes=[pltpu.VMEM((TM_ROWS + SUBS, LANES), f32), pltpu.VMEM((MOE_TILE * SUBS, LANES), f32),
                            pltpu.VMEM((D, 2 * D_EXPERT), bf16), pltpu.SemaphoreType.DMA(())]),
        out_shape=jax.ShapeDtypeStruct((MOE_ROWS, D_EXPERT), bf16),
        compiler_params=_cp(("arbitrary",)), name="moe_up",
    )(tile_expert, n_used, codes, h_tm, p['w1'], p['w3'])
    return pl.pallas_call(
        _moe_down_kernel,
        grid_spec=pltpu.PrefetchScalarGridSpec(
            num_scalar_prefetch=4, grid=(MOE_TILES,),
            in_specs=[pl.BlockSpec((MOE_TILE, D_EXPERT), lambda i, te, nu, c, g: (i, 0)),
                      pl.BlockSpec((1, D_EXPERT, D), lambda i, te, nu, c, g: (te[i], 0, 0))],
            out_specs=pl.BlockSpec(memory_space=pl.ANY),
            scratch_shapes=[pltpu.VMEM((TM_ROWS + SUBS, LANES), f32), pltpu.VMEM((MOE_TILE * SUBS, LANES), f32),
                            pltpu.VMEM((D_EXPERT, D), bf16), pltpu.SemaphoreType.DMA(())]),
        out_shape=jax.ShapeDtypeStruct((TM_ROWS, LANES), f32),
        compiler_params=_cp(("arbitrary",)), name="moe_down",
    )(tile_expert, n_used, codes, gates, hh, p['w2'])


def _final_kernel(x_ref, d_ref, mod_ref, o_ref):
    o_ref[...] = x_ref[...] + mod_ref[5:6, :] * _load_token_major(d_ref, ROW_TILE)


def _final(x, delta, mods, row0, seq):
    n_rows = x.shape[0]
    t0 = row0 // ROW_TILE
    return pl.pallas_call(
        _final_kernel,
        grid=(n_rows // ROW_TILE,),
        in_specs=[pl.BlockSpec((ROW_TILE, D), lambda i: (i, 0)),
                  _token_major_spec(ROW_TILE, lambda i: (t0 + i, 0)),
                  pl.BlockSpec((8, D), lambda i: (_sample_of_tile(t0 + i, ROW_TILE), 0))],
        out_specs=pl.BlockSpec((ROW_TILE, D), lambda i: (i, 0)),
        out_shape=jax.ShapeDtypeStruct((n_rows, D), f32),
        compiler_params=_cp(("parallel",)), name="final",
    )(x, delta, mods).reshape(n_rows // seq, seq, D)


def kernel(x_prompt, x_sample, cache_l0_k, cache_l0_v, cache_l1_ckv, cache_l1_kpe, c, c_ctx, l0_g_norm1, l0_g_norm2, l0_w_ada, l0_b_ada, l0_w_in, l0_g_vnorm, l0_w_s, l0_b_s, l0_g_q, l0_g_k, l0_sink, l0_w_o, l0_w_rg, l0_b_rg, l0_w_re, l0_b_re, l0_w1, l0_w3, l0_w2, l1_g_norm1, l1_g_norm2, l1_w_ada, l1_b_ada, l1_w_in, l1_g_qa, l1_w_uq, l1_g_kva, l1_w_ukv, l1_g_q, l1_g_k, l1_w_o, l1_w_rg, l1_b_rg, l1_w_re, l1_b_re, l1_w1, l1_w3, l1_w2):
    p0 = dict(w_in=l0_w_in, g_vnorm=l0_g_vnorm, w_s=l0_w_s, b_s=l0_b_s, g_q=l0_g_q, g_k=l0_g_k, sink=l0_sink,
              w_o=l0_w_o, w_rg=l0_w_rg, b_rg=l0_b_rg, w_re=l0_w_re, b_re=l0_b_re, w1=l0_w1, w3=l0_w3, w2=l0_w2)
    p1 = dict(w_in=l1_w_in, g_qa=l1_g_qa, w_uq=l1_w_uq, g_kva=l1_g_kva, w_ukv=l1_w_ukv, g_q=l1_g_q, g_k=l1_g_k,
              w_o=l1_w_o, w_rg=l1_w_rg, b_rg=l1_b_rg, w_re=l1_w_re, b_re=l1_b_re, w1=l1_w1, w3=l1_w3, w2=l1_w2)

    cond8 = jnp.zeros((8, D), f32).at[0].set(c_ctx).at[1:1 + DEC_BATCH].set(c)
    mods0 = _mod_rows(_adaln(cond8, l0_w_ada, l0_b_ada))
    mods1 = _mod_rows(_adaln(cond8, l1_w_ada, l1_b_ada))

    xc0 = x_prompt.reshape(T_CTX, D)
    xl0 = x_sample.reshape(T_LAT, D)

    xc0m, xl0m, k_new, v_new = _l0_mixer(xc0, xl0, l0_g_norm1, mods0, p0, cache_l0_k, cache_l0_v)
    h0, route0 = _router(xc0m, xl0m, l0_g_norm2, mods0, p0)
    moe0 = _moe(h0, route0, p0)

    mods01 = mods1.reshape(3, 8, D).at[:, 5].set(mods0.reshape(3, 8, D)[:, 5]).reshape(24, D)
    xc1m, xl1m, ckv_new, kpe_new = _l1_mixer(xc0m, xl0m, moe0, l1_g_norm1, mods01, p1, cache_l1_ckv, cache_l1_kpe)
    h1, route1 = _router(xc1m, xl1m, l1_g_norm2, mods1, p1)
    moe1 = _moe(h1, route1, p1)

    y_prompt = _final(xc1m, moe1, mods1, 0, SEQ)
    y_sample = _final(xl1m, moe1, mods1, T_CTX, DEC_SEQ)
    return (y_prompt, y_sample,
            k_new.reshape(BATCH, SEQ, B_KV, B_HD), v_new.reshape(BATCH, SEQ, B_KV, B_HD),
            ckv_new.reshape(BATCH, SEQ, C_KV_LORA), kpe_new.reshape(BATCH, SEQ, C_ROPE))
```

```python
import functools

import jax
import jax.numpy as jnp
from jax import lax
from jax.experimental import pallas as pl
from jax.experimental.pallas import tpu as pltpu

f32 = jnp.float32
bf16 = jnp.bfloat16
HIGHEST = lax.Precision.HIGHEST

D = 1024
BATCH, SEQ = 32, 256
DEC_BATCH, DEC_SEQ = 2, 1024
PAST = 512
T_CTX = BATCH * SEQ
T_LAT = DEC_BATCH * DEC_SEQ
T = T_CTX + T_LAT
GRID_W = 64
CHUNK = 128
WINDOW = 128
ROPE_THETA = 10000.0
EPS = 1e-6
NEG_INF = -1e30
LANES = 128
SUBS = D // LANES

A_WIDTH = 512
A_GROUPS = 4
B_HEADS, B_KV, B_GROUP, B_HD = 8, 2, 4, 64
B_SCALE = B_HD ** -0.5

C_HEADS, C_Q_LORA, C_KV_LORA, C_NOPE, C_ROPE, C_V = 16, 384, 256, 64, 32, 64
C_QK = C_NOPE + C_ROPE
C_SCALE = C_QK ** -0.5
ODD_IN_PAD = 768
SLOT = 128

N_GROUPS, EPG, N_EXPERTS, D_EXPERT = 4, 8, 32, 256

ROW_TILE = 512
BLOCK_ROWS = 1024
PROJ_ROWS = 512
MOE_TILE = 256
MOE_ROWS = 2 * T + N_EXPERTS * MOE_TILE
MOE_TILES = MOE_ROWS // MOE_TILE
VMEM_CAP = 56 * 1024 * 1024


def _cp(sem, vmem=VMEM_CAP):
    return pltpu.CompilerParams(dimension_semantics=sem, vmem_limit_bytes=vmem)


def _const_spec(shape):
    nd = len(shape)
    return pl.BlockSpec(shape, lambda *_: (0,) * nd, pipeline_mode=pl.Buffered(1))


def _sample_of_tile(i, tile):
    n_ctx = T_CTX // tile
    per_lat = DEC_SEQ // tile
    return jnp.where(i < n_ctx, 0, 1 + (i - n_ctx) // per_lat)


def _silu(x):
    return x * jax.nn.sigmoid(x)


def _rms_rows(x, g):
    return x * lax.rsqrt(jnp.mean(x * x, -1, keepdims=True) + EPS) * g


def _swap_pairs(x):
    lane = lax.broadcasted_iota(jnp.int32, x.shape, x.ndim - 1)
    nxt = pltpu.roll(x, x.shape[-1] - 1, x.ndim - 1)
    prv = pltpu.roll(x, 1, x.ndim - 1)
    return jnp.where((lane & 1) == 0, nxt, prv)


def _swap_halves(x):
    return pltpu.roll(x, LANES // 2, x.ndim - 1)


def _adaln_kernel(c_ref, w_ref, b_ref, o_ref):
    s = _silu(c_ref[...])
    o_ref[...] = jnp.dot(s, w_ref[...], precision=HIGHEST, preferred_element_type=f32) + b_ref[...]


def _adaln(cond8, w, b):
    n = w.shape[1]
    tn = 1536
    return pl.pallas_call(
        _adaln_kernel,
        grid=(n // tn,),
        in_specs=[_const_spec((8, D)), pl.BlockSpec((D, tn), lambda j: (0, j)),
                  pl.BlockSpec((1, tn), lambda j: (0, j))],
        out_specs=pl.BlockSpec((8, tn), lambda j: (0, j)),
        out_shape=jax.ShapeDtypeStruct((8, n), f32),
        compiler_params=_cp(("arbitrary",)),
        name="adaln",
    )(cond8, w, b.reshape(1, n))


def _mod_rows(m8):
    m = m8[:3].reshape(3, 6, D)
    return jnp.pad(m, ((0, 0), (0, 2), (0, 0))).reshape(24, D)


N_CTX_TILES = T_CTX // ROW_TILE


def _token_specs(width):
    return [pl.BlockSpec((ROW_TILE, width), lambda i: (jnp.minimum(i, N_CTX_TILES - 1), 0)),
            pl.BlockSpec((ROW_TILE, width), lambda i: (jnp.maximum(i - N_CTX_TILES, 0), 0))]


def _token_rows(xc_ref, xl_ref):
    return jnp.where(pl.program_id(0) < N_CTX_TILES, xc_ref[...], xl_ref[...])


def _store_token_major(ref, x):
    n = x.shape[0]
    for s in range(SUBS):
        ref[pl.ds(s, n, stride=SUBS), :] = x[:, s * LANES:(s + 1) * LANES]


def _load_token_major(ref, n, row0=0):
    return jnp.concatenate([ref[pl.ds(row0 * SUBS + s, n, stride=SUBS), :] for s in range(SUBS)], axis=1)


def _token_major_spec(rows, index_map):
    return pl.BlockSpec((rows * SUBS, LANES), index_map)


def _rope_tables(n, rot_dim, lanes, lane0):
    rows_count = n // GRID_W
    rows = jnp.repeat(jnp.arange(rows_count), GRID_W).astype(f32)
    cols = jnp.tile(jnp.arange(GRID_W), rows_count).astype(f32)
    d_axis = rot_dim // 2
    inv = ROPE_THETA ** (-jnp.arange(0, d_axis, 2, dtype=f32) / d_axis)
    ang = jnp.concatenate([rows[:, None] * inv, cols[:, None] * inv], -1)
    cos = jnp.repeat(jnp.cos(ang), 2, axis=1)
    sin = jnp.repeat(jnp.sin(ang), 2, axis=1) * jnp.tile(jnp.array([-1.0, 1.0], f32), rot_dim // 2)
    c = jnp.ones((n, lanes), f32).at[:, lane0:lane0 + rot_dim].set(cos)
    s = jnp.zeros((n, lanes), f32).at[:, lane0:lane0 + rot_dim].set(sin)
    return c, s


L0_Q0 = 2 * A_WIDTH
L0_K0 = L0_Q0 + B_HEADS * LANES
L0_V0 = L0_K0 + B_KV * B_HD
L0_IN = L0_V0 + B_KV * B_HD


def _l0_kernel(*refs, latent):
    if latent:
        (sink_ref, x_ref, gn_ref, mod_ref, win_ref, gvn_ref, ws_ref, bsb_ref, gq_ref, gk_ref, wo_ref,
         cos_ref, sin_ref, kc_ref, vc_ref, xo_ref, zs, cat, qs, ks, vt, kcb, vct) = refs
        key_off = WINDOW
    else:
        (sink_ref, x_ref, gn_ref, mod_ref, win_ref, gvn_ref, ws_ref, bsb_ref, gq_ref, gk_ref, wo_ref,
         xo_ref, ko_ref, vo_ref, zs, cat, qs, ks, vt) = refs
        key_off = 0
    n = BLOCK_ROWS
    n_chunks = n // CHUNK
    low = lax.broadcasted_iota(jnp.int32, (CHUNK, LANES), 1) < B_HD

    if latent:
        zpad = jnp.zeros((WINDOW, LANES), bf16)
        for c0 in (0, 1 + n_chunks):
            ks[c0 * CHUNK:(c0 + 1) * CHUNK, :] = zpad
            vt[c0] = zpad
        kcb[...] = kc_ref[...].astype(bf16)
        for i in range(PAST // CHUNK):
            vct[i] = vc_ref[i * CHUNK:(i + 1) * CHUNK, :].T.astype(bf16)

    def project(c, carry):
        r = pl.ds(pl.multiple_of(c * PROJ_ROWS, PROJ_ROWS), PROJ_ROWS)
        h = _rms_rows(x_ref[r, :], gn_ref[...]) * (1.0 + mod_ref[1:2, :]) + mod_ref[0:1, :]
        zs[r, :] = jnp.dot(h.astype(bf16), win_ref[...], preferred_element_type=f32)
        return carry

    lax.fori_loop(0, n // PROJ_ROWS, project, 0)

    def prepare(c, carry):
        r = pl.ds(pl.multiple_of(c * CHUNK, CHUNK), CHUNK)
        u = jax.nn.gelu(zs[r, 0:A_WIDTH])
        v = jax.nn.gelu(zs[r, A_WIDTH:2 * A_WIDTH])
        mu = jnp.mean(v, -1, keepdims=True)
        var = jnp.mean(jnp.square(v - mu), -1, keepdims=True)
        vn = ((v - mu) * lax.rsqrt(var + EPS) * gvn_ref[...]).astype(bf16)
        for g in range(A_GROUPS):
            cs = slice(g * CHUNK, (g + 1) * CHUNK)
            mixed = jnp.dot(ws_ref[g], vn[:, cs], preferred_element_type=f32) + bsb_ref[g]
            cat[r, cs] = (u[:, cs] * mixed).astype(bf16)
        if latent:
            cs_, sn_ = cos_ref[r, :], sin_ref[r, :]
        for h in range(B_HEADS):
            hs = slice(h * LANES, (h + 1) * LANES)
            qh = zs[r, L0_Q0 + h * LANES:L0_Q0 + (h + 1) * LANES]
            qh = qh * lax.rsqrt(jnp.sum(qh * qh, -1, keepdims=True) * (1.0 / B_HD) + EPS) * gq_ref[:, hs]
            if latent:
                qh = qh * cs_ + _swap_pairs(qh) * sn_
            qs[r, hs] = (qh * B_SCALE).astype(bf16)
        k = zs[r, L0_K0:L0_K0 + LANES]
        k2 = k * k
        s0 = jnp.sum(jnp.where(low, k2, 0.0), -1, keepdims=True)
        s1 = jnp.sum(jnp.where(low, 0.0, k2), -1, keepdims=True)
        k = k * lax.rsqrt(jnp.where(low, s0, s1) * (1.0 / B_HD) + EPS) * gk_ref[...]
        vv = zs[r, L0_V0:L0_V0 + LANES]
        if latent:
            k = k * cs_ + _swap_pairs(k) * sn_
        else:
            ko_ref[r, :] = k
            vo_ref[r, :] = vv
        kr = pl.ds(pl.multiple_of(c * CHUNK + key_off, CHUNK), CHUNK)
        ks[kr, :] = k.astype(bf16)
        vt[c + key_off // CHUNK] = vv.T.astype(bf16)
        return carry

    lax.fori_loop(0, n_chunks, prepare, 0)

    def attend(r, rows, key_sets):
        q = jnp.concatenate([qs[r, h * LANES:(h + 1) * LANES] for h in range(B_HEADS)], axis=0)
        sk = jnp.concatenate([jnp.full((1, rows), sink_ref[h], f32) for h in range(B_HEADS)], axis=1)
        scores = []
        m = sk
        for k, _, keep in key_sets:
            s = lax.dot_general(k, q, (((1,), (1,)), ((), ())), preferred_element_type=f32)
            if keep is not None:
                s = jnp.where(keep, s, NEG_INF)
            scores.append(s)
            m = jnp.maximum(m, jnp.max(s, 0, keepdims=True))
        den = jnp.exp(sk - m)
        ot = None
        for s, (_, vts, _) in zip(scores, key_sets):
            e = jnp.exp(s - m)
            den = den + jnp.sum(e, 0, keepdims=True)
            eb = e.astype(bf16)
            for i, v_t in enumerate(vts):
                pv = jnp.dot(v_t, eb[i * CHUNK:(i + 1) * CHUNK, :], preferred_element_type=f32)
                ot = pv if ot is None else ot + pv
        ot = ot * (1.0 / den)
        for pair in range(B_HEADS // 2):
            f0 = (2 * pair // B_GROUP) * B_HD
            pair_t = jnp.concatenate([ot[f0:f0 + B_HD, 2 * pair * rows:(2 * pair + 1) * rows],
                                      ot[f0:f0 + B_HD, (2 * pair + 1) * rows:(2 * pair + 2) * rows]], axis=0)
            cat[r, A_WIDTH + pair * LANES:A_WIDTH + (pair + 1) * LANES] = pair_t.T.astype(bf16)

    if latent:
        span = CHUNK + 2 * WINDOW

        def attend_block(c, carry):
            start = pl.multiple_of(c * CHUNK, CHUNK)
            kr = pl.ds(start, span)
            kj = lax.broadcasted_iota(jnp.int32, (span, B_HEADS * CHUNK), 0)
            qi = lax.broadcasted_iota(jnp.int32, (span, B_HEADS * CHUNK), 1) & (CHUNK - 1)
            kpos = start - WINDOW + kj
            keep = (jnp.abs(kj - WINDOW - qi) <= WINDOW) & (kpos >= 0) & (kpos < n)
            attend(pl.ds(start, CHUNK), CHUNK,
                   [(ks[kr, :], [vt[c + i] for i in range(span // CHUNK)], keep),
                    (kcb[...], [vct[i] for i in range(PAST // CHUNK)], None)])
            return carry

        lax.fori_loop(0, n_chunks, attend_block, 0)
    else:
        def attend_seq(sq, carry):
            r = pl.ds(pl.multiple_of(sq * SEQ, SEQ), SEQ)
            attend(r, SEQ, [(ks[r, :], [vt[sq * (SEQ // CHUNK) + i] for i in range(SEQ // CHUNK)], None)])
            return carry

        lax.fori_loop(0, n // SEQ, attend_seq, 0)

    def output(c, carry):
        r = pl.ds(pl.multiple_of(c * PROJ_ROWS, PROJ_ROWS), PROJ_ROWS)
        y = jnp.dot(cat[r, :], wo_ref[...], preferred_element_type=f32)
        xo_ref[r, :] = x_ref[r, :] + mod_ref[2:3, :] * y
        return carry

    lax.fori_loop(0, n // PROJ_ROWS, output, 0)


def _l0_mixer(xc, xl, g_norm, mods, p, cache_k, cache_v):
    w = p['w_in']
    q = w[:, 2 * A_WIDTH:2 * A_WIDTH + B_HEADS * B_HD].reshape(D, B_HEADS, B_HD)
    zero = jnp.zeros((D, B_GROUP, B_HD), f32)
    q_slots = jnp.concatenate([jnp.concatenate([q[:, :B_GROUP], zero], axis=2),
                               jnp.concatenate([zero, q[:, B_GROUP:]], axis=2)], axis=1).reshape(D, B_HEADS * LANES)
    win = jnp.concatenate([w[:, :2 * A_WIDTH], q_slots, w[:, 2 * A_WIDTH + B_HEADS * B_HD:]], axis=1).astype(bf16)
    zg = jnp.zeros((B_HD,), f32)
    gq = jnp.concatenate([jnp.tile(jnp.concatenate([p['g_q'], zg]), B_GROUP),
                          jnp.tile(jnp.concatenate([zg, p['g_q']]), B_GROUP)]).reshape(1, B_HEADS * LANES)
    gk = jnp.tile(p['g_k'], B_KV).reshape(1, LANES)
    gvn = p['g_vnorm'].reshape(1, A_WIDTH)
    ws = p['w_s'].astype(bf16)
    bsb = jnp.broadcast_to(p['b_s'][:, :, None], (A_GROUPS, CHUNK, CHUNK))
    wo = p['w_o'].astype(bf16)
    weights = (g_norm.reshape(1, D),)
    consts = (win, gvn, ws, bsb, gq, gk, wo)
    c_specs = [_const_spec(a.shape) for a in consts]
    smem = pl.BlockSpec(memory_space=pltpu.SMEM)
    row = pl.BlockSpec((BLOCK_ROWS, D), lambda b: (b, 0))
    kv = pl.BlockSpec((BLOCK_ROWS, LANES), lambda b: (b, 0))

    def scratch(pad):
        return [pltpu.VMEM((BLOCK_ROWS, L0_IN), f32), pltpu.VMEM((BLOCK_ROWS, D), bf16),
                pltpu.VMEM((BLOCK_ROWS, B_HEADS * LANES), bf16), pltpu.VMEM((BLOCK_ROWS + pad, LANES), bf16),
                pltpu.VMEM(((BLOCK_ROWS + pad) // CHUNK, LANES, CHUNK), bf16)]

    kv_shape = jax.ShapeDtypeStruct((T_CTX, LANES), f32)
    xo_ctx, k_new, v_new = pl.pallas_call(
        functools.partial(_l0_kernel, latent=False),
        grid=(T_CTX // BLOCK_ROWS,),
        in_specs=[smem, row, _const_spec((1, D)), pl.BlockSpec((8, D), lambda b: (0, 0))] + c_specs,
        out_specs=[row, kv, kv],
        out_shape=[jax.ShapeDtypeStruct((T_CTX, D), f32), kv_shape, kv_shape],
        scratch_shapes=scratch(0),
        compiler_params=_cp(("parallel",)), name="l0_mixer_ctx",
    )(p['sink'], xc, *weights, mods, *consts)

    cos, sin = _rope_tables(DEC_SEQ, B_HD, LANES, 0)
    cos = cos.at[:, B_HD:].set(cos[:, :B_HD])
    sin = sin.at[:, B_HD:].set(sin[:, :B_HD])
    cache = pl.BlockSpec((None, PAST, LANES), lambda b: (b, 0, 0))
    past = [pltpu.VMEM((PAST, LANES), bf16), pltpu.VMEM((PAST // CHUNK, LANES, CHUNK), bf16)]
    xo_lat = pl.pallas_call(
        functools.partial(_l0_kernel, latent=True),
        grid=(DEC_BATCH,),
        in_specs=[smem, row, _const_spec((1, D)), pl.BlockSpec((8, D), lambda b: (1 + b, 0))] + c_specs + [
                  _const_spec(cos.shape), _const_spec(sin.shape), cache, cache],
        out_specs=row,
        out_shape=jax.ShapeDtypeStruct((T_LAT, D), f32),
        scratch_shapes=scratch(2 * WINDOW) + past,
        compiler_params=_cp(("parallel",)), name="l0_mixer_lat",
    )(p['sink'], xl, *weights, mods, *consts, cos, sin,
      cache_k.reshape(DEC_BATCH, PAST, LANES), cache_v.reshape(DEC_BATCH, PAST, LANES))
    return xo_ctx, xo_lat, k_new, v_new


C_SLOTS = C_HEADS * SLOT
C_PAIRS = C_HEADS // 2
L1_ROWS = 256


def _l1_kernel(*refs, latent):
    if latent:
        (x_ref, d_ref, gn_ref, mod_ref, win_ref, gqa_ref, wuq_ref, gq_ref, gkva_ref, wuk_ref, wuvt_ref, gk_ref,
         wo_ref, wuqs_ref, qcos_ref, qsin_ref, kcos_ref, ksin_ref, cckv_ref, ckpe_ref, xo_ref,
         zs, cat, qs, ks, vt, wide, wide2) = refs
        n_ctx = PAST
    else:
        (x_ref, d_ref, gn_ref, mod_ref, win_ref, gqa_ref, wuq_ref, gq_ref, gkva_ref, wuk_ref, wuvt_ref, gk_ref,
         wo_ref, xo_ref, ckvo_ref, kpeo_ref, zs, cat, qs, ks, vt, wide) = refs
        n_ctx = 0
    n = BLOCK_ROWS
    nt_dims = (((1,), (1,)), ((), ()))

    def inv_rms(v):
        return lax.rsqrt(jnp.sum(v * v, -1, keepdims=True) * (1.0 / C_QK) + EPS)

    def expand_keys(ckv_n, kslot, kb, rope_rows):
        cb = ckv_n.astype(bf16)
        key_rows = pl.ds(pl.multiple_of(kb * L1_ROWS, L1_ROWS), L1_ROWS)
        wide[...] = jnp.dot(cb, wuk_ref[...], preferred_element_type=f32)
        if rope_rows is not None:
            kcos = kcos_ref[rope_rows, :]
            turned = _swap_pairs(kslot) * ksin_ref[rope_rows, :]
        for h in range(C_HEADS):
            kh = wide[:, h * SLOT:(h + 1) * SLOT] + kslot
            if rope_rows is not None:
                kh = inv_rms(kh) * (kh * kcos + turned)
            else:
                kh = kh * inv_rms(kh) * gk_ref[...]
            ks[h, key_rows, :] = kh.astype(bf16)
        v_t = lax.dot_general(wuvt_ref[...], cb, nt_dims, preferred_element_type=f32).astype(bf16)
        for pair in range(C_PAIRS):
            vt[pair, kb] = v_t[pair * LANES:(pair + 1) * LANES, :]

    if latent:
        def past_keys(c, carry):
            r = pl.ds(pl.multiple_of(c * L1_ROWS, L1_ROWS), L1_ROWS)
            expand_keys(cckv_ref[r, :], ckpe_ref[r, :], c, None)
            return carry

        lax.fori_loop(0, PAST // L1_ROWS, past_keys, 0)

    for c in range(n // PROJ_ROWS):
        r = pl.ds(c * PROJ_ROWS, PROJ_ROWS)
        x1 = x_ref[r, :] + mod_ref[5:6, :] * _load_token_major(d_ref, PROJ_ROWS, c * PROJ_ROWS)
        xo_ref[r, :] = x1
        h = _rms_rows(x1, gn_ref[...]) * (1.0 + mod_ref[1:2, :]) + mod_ref[0:1, :]
        zs[r, :] = jnp.dot(h.astype(bf16), win_ref[...], preferred_element_type=f32)

    def prepare(c, carry):
        r = pl.ds(pl.multiple_of(c * L1_ROWS, L1_ROWS), L1_ROWS)
        qa = _rms_rows(zs[r, 0:C_Q_LORA], gqa_ref[...]).astype(bf16)
        wide[...] = jnp.dot(qa, wuq_ref[...], preferred_element_type=f32)
        if latent:
            wide2[...] = jnp.dot(qa, wuqs_ref[...], preferred_element_type=f32)
            qcos, qsin = qcos_ref[r, :], qsin_ref[r, :]
        for h in range(C_HEADS):
            hs = slice(h * SLOT, (h + 1) * SLOT)
            qh = wide[:, hs]
            if latent:
                qh = inv_rms(qh) * (qh * qcos + wide2[:, hs] * qsin)
            else:
                qh = qh * inv_rms(qh) * gq_ref[...]
            qs[h, r, :] = (qh * C_SCALE).astype(bf16)
        ckv_n = _rms_rows(zs[r, C_Q_LORA:C_Q_LORA + C_KV_LORA], gkva_ref[...])
        kslot = zs[r, C_Q_LORA + C_KV_LORA:ODD_IN_PAD]
        if not latent:
            ckvo_ref[r, :] = ckv_n
            kpeo_ref[r, :] = kslot
        expand_keys(ckv_n, kslot, c + n_ctx // L1_ROWS, r if latent else None)
        return carry

    lax.fori_loop(0, n // L1_ROWS, prepare, 0)

    low = lax.broadcasted_iota(jnp.int32, (2 * C_V, L1_ROWS), 0) < C_V
    n_kblocks = (n_ctx + n) // L1_ROWS
    pairs_per_step = 2 if latent else 4

    def attend(c, carry):
        r = pl.ds(pl.multiple_of(c * L1_ROWS, L1_ROWS), L1_ROWS)

        def values_t(pair, eb):
            if not latent:
                return jnp.dot(vt[pair, c], eb, preferred_element_type=f32)
            o_t = None
            for b in range(n_kblocks):
                pv = jnp.dot(vt[pair, b], eb[b * L1_ROWS:(b + 1) * L1_ROWS, :], preferred_element_type=f32)
                o_t = pv if o_t is None else o_t + pv
            return o_t

        def pairs_step(i, carry2):
            pairs = [i * pairs_per_step + j for j in range(pairs_per_step)]
            heads = [2 * p + hh for p in pairs for hh in range(2)]
            scores = [lax.dot_general(ks[h] if latent else ks[h, r, :], qs[h, r, :], nt_dims,
                                      preferred_element_type=f32) for h in heads]
            exps = [jnp.exp(s - jnp.max(s, 0, keepdims=True)) for s in scores]
            dens = [jnp.sum(e, 0, keepdims=True) for e in exps]
            outs = [values_t(h // 2, e.astype(bf16)) / den for h, e, den in zip(heads, exps, dens)]
            for j, pair in enumerate(pairs):
                cat[pair, r, :] = jnp.where(low, outs[2 * j], outs[2 * j + 1]).T.astype(bf16)
            return carry2

        return lax.fori_loop(0, C_PAIRS // pairs_per_step, pairs_step, carry)

    lax.fori_loop(0, n // L1_ROWS, attend, 0)

    def output(c, carry):
        r = pl.ds(pl.multiple_of(c * PROJ_ROWS, PROJ_ROWS), PROJ_ROWS)
        heads = jnp.concatenate([cat[pair, r, :] for pair in range(C_PAIRS)], axis=1)
        y = jnp.dot(heads, wo_ref[...], preferred_element_type=f32)
        xo_ref[r, :] = xo_ref[r, :] + mod_ref[2:3, :] * y
        return carry

    lax.fori_loop(0, n // PROJ_ROWS, output, 0)


def _slot_cols(w, heads, width, lo, hi, lane0):
    k = w.shape[0]
    w3 = w.reshape(k, heads, width)[:, :, lo:hi]
    out = jnp.zeros((k, heads, SLOT), w.dtype).at[:, :, lane0:lane0 + (hi - lo)].set(w3)
    return out.reshape(k, heads * SLOT)


def _l1_mixer(xc, xl, moe_prev, g_norm, mods, p, cache_ckv, cache_kpe):
    w_in = jnp.zeros((D, ODD_IN_PAD), f32).at[:, :C_Q_LORA + C_KV_LORA].set(
        p['w_in'][:, :C_Q_LORA + C_KV_LORA]).at[
        :, C_Q_LORA + C_KV_LORA + C_NOPE:C_Q_LORA + C_KV_LORA + C_QK].set(p['w_in'][:, C_Q_LORA + C_KV_LORA:])
    wuq = _slot_cols(p['w_uq'], C_HEADS, C_QK, 0, C_QK, 0).astype(bf16)
    wuk = _slot_cols(p['w_ukv'], C_HEADS, C_NOPE + C_V, 0, C_NOPE, 0).astype(bf16)
    wuv_t = p['w_ukv'].reshape(C_KV_LORA, C_HEADS, C_NOPE + C_V)[:, :, C_NOPE:].reshape(
        C_KV_LORA, C_HEADS * C_V).T.astype(bf16)
    gq = jnp.zeros((1, SLOT), f32).at[0, :C_QK].set(p['g_q'])
    gk = jnp.zeros((1, SLOT), f32).at[0, :C_QK].set(p['g_k'])
    consts = (g_norm.reshape(1, D), w_in.astype(bf16), p['g_qa'].reshape(1, C_Q_LORA), wuq, gq,
              p['g_kva'].reshape(1, C_KV_LORA), wuk, wuv_t, gk, p['w_o'].astype(bf16))
    c_specs = [_const_spec(a.shape) for a in consts]
    row = pl.BlockSpec((BLOCK_ROWS, D), lambda b: (b, 0))
    n_ctx_blocks = T_CTX // BLOCK_ROWS

    def scratch(n_keys):
        return [pltpu.VMEM((BLOCK_ROWS, ODD_IN_PAD), f32), pltpu.VMEM((C_PAIRS, BLOCK_ROWS, LANES), bf16),
                pltpu.VMEM((C_HEADS, BLOCK_ROWS, SLOT), bf16), pltpu.VMEM((C_HEADS, n_keys, SLOT), bf16),
                pltpu.VMEM((C_PAIRS, n_keys // L1_ROWS, LANES, L1_ROWS), bf16),
                pltpu.VMEM((L1_ROWS, C_SLOTS), f32)]

    xo_ctx, ckv_new, kpe_slot = pl.pallas_call(
        functools.partial(_l1_kernel, latent=False),
        grid=(n_ctx_blocks,),
        in_specs=[row, _token_major_spec(BLOCK_ROWS, lambda b: (b, 0)), c_specs[0],
                  pl.BlockSpec((8, D), lambda b: (0, 0))] + c_specs[1:],
        out_specs=[row, pl.BlockSpec((BLOCK_ROWS, C_KV_LORA), lambda b: (b, 0)),
                   pl.BlockSpec((BLOCK_ROWS, SLOT), lambda b: (b, 0))],
        out_shape=[jax.ShapeDtypeStruct((T_CTX, D), f32), jax.ShapeDtypeStruct((T_CTX, C_KV_LORA), f32),
                   jax.ShapeDtypeStruct((T_CTX, SLOT), f32)],
        scratch_shapes=scratch(BLOCK_ROWS),
        compiler_params=_cp(("parallel",)), name="l1_mixer_ctx",
    )(xc, moe_prev, consts[0], mods, *consts[1:])

    cos, sin = _rope_tables(DEC_SEQ, C_ROPE, SLOT, C_NOPE)
    pair_swap = lambda a: a.reshape(a.shape[:-1] + (a.shape[-1] // 2, 2))[..., ::-1].reshape(a.shape)
    rope = (pair_swap(wuq), gq * cos, pair_swap(gq) * sin, gk * cos, pair_swap(gk) * sin)
    ckpe = jnp.zeros((DEC_BATCH, PAST, SLOT), f32).at[:, :, C_NOPE:C_QK].set(cache_kpe)
    xo_lat = pl.pallas_call(
        functools.partial(_l1_kernel, latent=True),
        grid=(DEC_BATCH,),
        in_specs=[pl.BlockSpec((BLOCK_ROWS, D), lambda b: (b, 0), pipeline_mode=pl.Buffered(1)),
                  pl.BlockSpec((BLOCK_ROWS * SUBS, LANES), lambda b: (n_ctx_blocks + b, 0),
                               pipeline_mode=pl.Buffered(1)), c_specs[0],
                  pl.BlockSpec((8, D), lambda b: (1 + b, 0))] + c_specs[1:] + [_const_spec(a.shape) for a in rope] + [
                  pl.BlockSpec((None, PAST, C_KV_LORA), lambda b: (b, 0, 0)),
                  pl.BlockSpec((None, PAST, SLOT), lambda b: (b, 0, 0))],
        out_specs=row,
        out_shape=jax.ShapeDtypeStruct((T_LAT, D), f32),
        scratch_shapes=scratch(PAST + BLOCK_ROWS) + [pltpu.VMEM((L1_ROWS, C_SLOTS), f32)],
        compiler_params=_cp(("parallel",)), name="l1_mixer_lat",
    )(xl, moe_prev, consts[0], mods, *consts[1:], *rope, cache_ckv, ckpe)
    return xo_ctx, xo_lat, ckv_new, kpe_slot[:, C_NOPE:C_QK]


def _router_kernel(xc_ref, xl_ref, gn_ref, mod_ref, wr_ref, br_ref, h_ref, route_ref):
    h = _rms_rows(_token_rows(xc_ref, xl_ref), gn_ref[...]) * (1.0 + mod_ref[4:5, :]) + mod_ref[3:4, :]
    _store_token_major(h_ref, h)
    logits = jnp.dot(h, wr_ref[...], precision=HIGHEST, preferred_element_type=f32) + br_ref[...]
    lane_i = lax.broadcasted_iota(jnp.int32, logits.shape, 1)
    lane = lane_i.astype(f32)
    big = 1e6
    is_g = (lane_i >= N_EXPERTS) & (lane_i < N_EXPERTS + N_GROUPS)
    lg = jnp.where(is_g, logits, -jnp.inf)
    mg = jnp.max(lg, -1, keepdims=True)
    gsel = jnp.min(jnp.where(lg == mg, lane, big), -1, keepdims=True) - N_EXPERTS
    pg_sel = 1.0 / jnp.sum(jnp.where(is_g, jnp.exp(lg - mg), 0.0), -1, keepdims=True)
    in_grp = (lane_i < N_EXPERTS) & ((lane_i >> 3).astype(f32) == gsel)
    le = jnp.where(in_grp, logits, -jnp.inf)
    m1 = jnp.max(le, -1, keepdims=True)
    i1 = jnp.min(jnp.where(le == m1, lane, big), -1, keepdims=True)
    le2 = jnp.where(lane == i1, -jnp.inf, le)
    m2 = jnp.max(le2, -1, keepdims=True)
    i2 = jnp.min(jnp.where(le2 == m2, lane, big), -1, keepdims=True)
    e2 = jnp.exp(m2 - m1)
    w1 = pg_sel / (1.0 + e2)
    w2 = pg_sel * e2 / (1.0 + e2)
    route = jnp.where(lane_i == 0, i1,
                      jnp.where(lane_i == 1, i2, jnp.where(lane_i == 2, w1, jnp.where(lane_i == 3, w2, 0.0))))
    route_ref[...] = route.T[0:8, :]


def _router(xc, xl, g_norm, mods, p):
    wr = jnp.zeros((D, 128), f32).at[:, :N_EXPERTS].set(p['w_re']).at[
        :, N_EXPERTS:N_EXPERTS + N_GROUPS].set(p['w_rg'])
    br = jnp.zeros((1, 128), f32).at[0, :N_EXPERTS].set(p['b_re']).at[
        0, N_EXPERTS:N_EXPERTS + N_GROUPS].set(p['b_rg'])
    return pl.pallas_call(
        _router_kernel,
        grid=(T // ROW_TILE,),
        in_specs=_token_specs(D) + [
                  _const_spec((1, D)),
                  pl.BlockSpec((8, D), lambda i: (_sample_of_tile(i, ROW_TILE), 0)),
                  _const_spec((D, 128)), _const_spec((1, 128))],
        out_specs=[_token_major_spec(ROW_TILE, lambda i: (i, 0)), pl.BlockSpec((8, ROW_TILE), lambda i: (0, i))],
        out_shape=[jax.ShapeDtypeStruct((T * SUBS, LANES), f32), jax.ShapeDtypeStruct((8, T), f32)],
        compiler_params=_cp(("parallel",)), name="router",
    )(xc, xl, g_norm.reshape(1, D), mods, wr, br)


TE_LANES = 256
PLAN_TILE_EXPERT, PLAN_N_USED, PLAN_PAD_LO, PLAN_PAD_HI = 0, 1, 2, 3


def _plan_kernel(rt_ref, pos_ref, te_ref, rank):
    n_blk = T // 128
    e_col = lax.broadcasted_iota(jnp.int32, (N_EXPERTS, 128), 0).astype(f32)
    ri = lax.broadcasted_iota(jnp.int32, (128, 128), 0)
    ci = lax.broadcasted_iota(jnp.int32, (128, 128), 1)
    before = jnp.where(ri < ci, 1.0, 0.0).astype(bf16)

    def picks(b):
        cs = slice(b * 128, (b + 1) * 128)
        return rt_ref[0:1, cs] == e_col, rt_ref[1:2, cs] == e_col

    counts = jnp.zeros((N_EXPERTS, 1), f32)
    for b in range(n_blk):
        m0, m1 = picks(b)
        m = jnp.where(m0, 1.0, 0.0) + jnp.where(m1, 1.0, 0.0)
        rank[:, b * 128:(b + 1) * 128] = jnp.dot(m.astype(bf16), before, preferred_element_type=f32) + counts
        counts = counts + jnp.sum(m, axis=1, keepdims=True)

    tiles = jnp.floor((counts + (MOE_TILE - 1.0)) * (1.0 / MOE_TILE))
    er = lax.broadcasted_iota(jnp.int32, (N_EXPERTS, N_EXPERTS), 0)
    ec = lax.broadcasted_iota(jnp.int32, (N_EXPERTS, N_EXPERTS), 1)
    earlier = jnp.where(ec < er, 1.0, 0.0).astype(bf16)
    tile_start = jnp.dot(earlier, jnp.broadcast_to(tiles, (N_EXPERTS, 128)).astype(bf16),
                         preferred_element_type=f32)
    row_start = tile_start * MOE_TILE

    sub = lax.broadcasted_iota(jnp.int32, (8, 128), 0)
    for b in range(n_blk):
        m0, m1 = picks(b)
        base = rank[:, b * 128:(b + 1) * 128] + row_start
        p0 = jnp.sum(jnp.where(m0, base, 0.0), axis=0, keepdims=True)
        p1 = jnp.sum(jnp.where(m1, base, 0.0), axis=0, keepdims=True)
        pos_ref[:, b * 128:(b + 1) * 128] = jnp.where(sub == 0, p0, jnp.where(sub == 1, p1, 0.0)).astype(jnp.int32)

    tile_end = tile_start + tiles
    n_used = jnp.max(tile_end, axis=0, keepdims=True)
    diag = (lax.broadcasted_iota(jnp.int32, (N_EXPERTS, 128), 0)
            == lax.broadcasted_iota(jnp.int32, (N_EXPERTS, 128), 1))
    pad_lo = jnp.sum(jnp.where(diag, row_start + counts, 0.0), axis=0, keepdims=True)
    pad_hi = jnp.sum(jnp.where(diag, tile_end * MOE_TILE, 0.0), axis=0, keepdims=True)
    for j in range(TE_LANES // 128):
        t = (lax.broadcasted_iota(jnp.int32, (N_EXPERTS, 128), 1) + j * 128).astype(f32)
        te = jnp.minimum(jnp.sum(jnp.where(tile_end <= t, 1.0, 0.0), axis=0, keepdims=True), N_EXPERTS - 1.0)
        rows = jnp.where(sub == PLAN_TILE_EXPERT, te, jnp.where(sub == PLAN_N_USED, n_used, 0.0))
        if j == 0:
            rows = jnp.where(sub == PLAN_PAD_LO, pad_lo, jnp.where(sub == PLAN_PAD_HI, pad_hi, rows))
        te_ref[:, j * 128:(j + 1) * 128] = rows.astype(jnp.int32)


def _slot_code(t, k):
    return t * SUBS + k * (SUBS // 2)


def _code_offset(code):
    return pl.multiple_of(code & ~(SUBS - 1), SUBS)


def _code_gate_index(code):
    return code >> 2


PAD_CODE = T * SUBS


def _invert_kernel(pos_ref, plan_ref, code_ref):
    def pad(s, carry):
        code_ref[s] = PAD_CODE
        return carry

    def pads(e, carry):
        return lax.fori_loop(plan_ref[PLAN_PAD_LO, e], plan_ref[PLAN_PAD_HI, e], pad, carry)
    lax.fori_loop(0, N_EXPERTS, pads, 0)
    lax.fori_loop(plan_ref[PLAN_N_USED, 0] * MOE_TILE, MOE_ROWS, pad, 0)

    for k in range(2):
        def place(i, carry):
            for u in range(8):
                t = i * 8 + u
                code_ref[pos_ref[k * T + t]] = _slot_code(t, k)
            return carry
        lax.fori_loop(0, T // 8, place, 0)


def _route_plan(route_t):
    pos, plan = pl.pallas_call(
        _plan_kernel,
        out_shape=[jax.ShapeDtypeStruct((8, T), jnp.int32), jax.ShapeDtypeStruct((8, TE_LANES), jnp.int32)],
        scratch_shapes=[pltpu.VMEM((N_EXPERTS, T), f32)],
        compiler_params=_cp(None), name="route_plan",
    )(route_t)
    smem = pl.BlockSpec(memory_space=pltpu.SMEM)
    codes = pl.pallas_call(
        _invert_kernel,
        in_specs=[smem, smem], out_specs=smem,
        out_shape=jax.ShapeDtypeStruct((MOE_ROWS,), jnp.int32),
        name="route_invert",
    )(pos[0:2].reshape(2 * T), plan)
    gates = jnp.pad(route_t[2:4].T.reshape(2 * T), (0, 8))
    return plan[PLAN_TILE_EXPERT, :MOE_TILES], plan[PLAN_N_USED, :1], codes, gates


def _first_tile_of_expert(i, te_ref):
    return (i == 0) | (te_ref[i] != te_ref[jnp.maximum(i - 1, 0)])


TM_ROWS = T * SUBS
SCATTER_GROUP = 16


def _moe_up_kernel(te_ref, nu_ref, code_ref, h_hbm, w1_ref, w3_ref, o_ref, xs, gbuf, w13, sem):
    i = pl.program_id(0)
    used = i < nu_ref[0]

    @pl.when(i == 0)
    def _():
        cp = pltpu.make_async_copy(h_hbm, xs.at[pl.ds(0, TM_ROWS), :], sem)
        cp.start()
        xs[TM_ROWS:TM_ROWS + SUBS, :] = jnp.zeros((SUBS, LANES), f32)
        cp.wait()

    @pl.when(used & _first_tile_of_expert(i, te_ref))
    def _():
        w13[:, :D_EXPERT] = w1_ref[0].astype(bf16)
        w13[:, D_EXPERT:] = w3_ref[0].astype(bf16)

    @pl.when(used)
    def _():
        base = i * MOE_TILE
        for r in range(MOE_TILE):
            gbuf[r * SUBS:(r + 1) * SUBS, :] = xs[pl.ds(_code_offset(code_ref[base + r]), SUBS), :]
        x = _load_token_major(gbuf, MOE_TILE).astype(bf16)
        h13 = jnp.dot(x, w13[...], preferred_element_type=f32)
        o_ref[...] = (_silu(h13[:, :D_EXPERT]) * h13[:, D_EXPERT:]).astype(bf16)

    @pl.when(jnp.logical_not(used))
    def _():
        o_ref[...] = jnp.zeros_like(o_ref)


def _moe_down_kernel(te_ref, nu_ref, code_ref, gate_ref, hh_ref, w2_ref, o_hbm, acc, ybuf, w2b, sem):
    i = pl.program_id(0)
    used = i < nu_ref[0]

    @pl.when(i == 0)
    def _():
        def zero(c, carry):
            acc[pl.ds(pl.multiple_of(c * 1024, 1024), 1024), :] = jnp.zeros((1024, LANES), f32)
            return carry
        lax.fori_loop(0, TM_ROWS // 1024, zero, 0)
        acc[TM_ROWS:TM_ROWS + SUBS, :] = jnp.zeros((SUBS, LANES), f32)

    @pl.when(used & _first_tile_of_expert(i, te_ref))
    def _():
        w2b[...] = w2_ref[0].astype(bf16)

    @pl.when(used)
    def _():
        _store_token_major(ybuf, jnp.dot(hh_ref[...], w2b[...], preferred_element_type=f32))
        base = i * MOE_TILE
        for g0 in range(0, MOE_TILE, SCATTER_GROUP):
            rows = range(g0, g0 + SCATTER_GROUP)
            codes = [code_ref[base + r] for r in rows]
            new = [acc[pl.ds(_code_offset(c), SUBS), :]
                   + gate_ref[_code_gate_index(c)] * ybuf[r * SUBS:(r + 1) * SUBS, :] for r, c in zip(rows, codes)]
            for c, v in zip(codes, new):
                acc[pl.ds(_code_offset(c), SUBS), :] = v

    @pl.when(i == MOE_TILES - 1)
    def _():
        cp = pltpu.make_async_copy(acc.at[pl.ds(0, TM_ROWS), :], o_hbm, sem)
        cp.start()
        cp.wait()


def _moe(h_tm, route_t, p):
    tile_expert, n_used, codes, gates = _route_plan(route_t)
    hh = pl.pallas_call(
        _moe_up_kernel,
        grid_spec=pltpu.PrefetchScalarGridSpec(
            num_scalar_prefetch=3, grid=(MOE_TILES,),
            in_specs=[pl.BlockSpec(memory_space=pl.ANY),
                      pl.BlockSpec((1, D, D_EXPERT), lambda i, te, nu, c: (te[i], 0, 0)),
                      pl.BlockSpec((1, D, D_EXPERT), lambda i, te, nu, c: (te[i], 0, 0))],
            out_specs=pl.BlockSpec((MOE_TILE, D_EXPERT), lambda i, te, nu, c: (i, 0)),
            scratch_shapes=[pltpu.VMEM((TM_ROWS + SUBS, LANES), f32), pltpu.VMEM((MOE_TILE * SUBS, LANES), f32),
                            pltpu.VMEM((D, 2 * D_EXPERT), bf16), pltpu.SemaphoreType.DMA(())]),
        out_shape=jax.ShapeDtypeStruct((MOE_ROWS, D_EXPERT), bf16),
        compiler_params=_cp(("arbitrary",)), name="moe_up",
    )(tile_expert, n_used, codes, h_tm, p['w1'], p['w3'])
    return pl.pallas_call(
        _moe_down_kernel,
        grid_spec=pltpu.PrefetchScalarGridSpec(
            num_scalar_prefetch=4, grid=(MOE_TILES,),
            in_specs=[pl.BlockSpec((MOE_TILE, D_EXPERT), lambda i, te, nu, c, g: (i, 0)),
                      pl.BlockSpec((1, D_EXPERT, D), lambda i, te, nu, c, g: (te[i], 0, 0))],
            out_specs=pl.BlockSpec(memory_space=pl.ANY),
            scratch_shapes=[pltpu.VMEM((TM_ROWS + SUBS, LANES), f32), pltpu.VMEM((MOE_TILE * SUBS, LANES), f32),
                            pltpu.VMEM((D_EXPERT, D), bf16), pltpu.SemaphoreType.DMA(())]),
        out_shape=jax.ShapeDtypeStruct((TM_ROWS, LANES), f32),
        compiler_params=_cp(("arbitrary",)), name="moe_down",
    )(tile_expert, n_used, codes, gates, hh, p['w2'])


def _final_kernel(x_ref, d_ref, mod_ref, o_ref):
    o_ref[...] = x_ref[...] + mod_ref[5:6, :] * _load_token_major(d_ref, ROW_TILE)


def _final(x, delta, mods, row0, seq):
    n_rows = x.shape[0]
    t0 = row0 // ROW_TILE
    return pl.pallas_call(
        _final_kernel,
        grid=(n_rows // ROW_TILE,),
        in_specs=[pl.BlockSpec((ROW_TILE, D), lambda i: (i, 0)),
                  _token_major_spec(ROW_TILE, lambda i: (t0 + i, 0)),
                  pl.BlockSpec((8, D), lambda i: (_sample_of_tile(t0 + i, ROW_TILE), 0))],
        out_specs=pl.BlockSpec((ROW_TILE, D), lambda i: (i, 0)),
        out_shape=jax.ShapeDtypeStruct((n_rows, D), f32),
        compiler_params=_cp(("parallel",)), name="final",
    )(x, delta, mods).reshape(n_rows // seq, seq, D)


def kernel(x_prompt, x_sample, cache_l0_k, cache_l0_v, cache_l1_ckv, cache_l1_kpe, c, c_ctx, l0_g_norm1, l0_g_norm2, l0_w_ada, l0_b_ada, l0_w_in, l0_g_vnorm, l0_w_s, l0_b_s, l0_g_q, l0_g_k, l0_sink, l0_w_o, l0_w_rg, l0_b_rg, l0_w_re, l0_b_re, l0_w1, l0_w3, l0_w2, l1_g_norm1, l1_g_norm2, l1_w_ada, l1_b_ada, l1_w_in, l1_g_qa, l1_w_uq, l1_g_kva, l1_w_ukv, l1_g_q, l1_g_k, l1_w_o, l1_w_rg, l1_b_rg, l1_w_re, l1_b_re, l1_w1, l1_w3, l1_w2):
    p0 = dict(w_in=l0_w_in, g_vnorm=l0_g_vnorm, w_s=l0_w_s, b_s=l0_b_s, g_q=l0_g_q, g_k=l0_g_k, sink=l0_sink,
              w_o=l0_w_o, w_rg=l0_w_rg, b_rg=l0_b_rg, w_re=l0_w_re, b_re=l0_b_re, w1=l0_w1, w3=l0_w3, w2=l0_w2)
    p1 = dict(w_in=l1_w_in, g_qa=l1_g_qa, w_uq=l1_w_uq, g_kva=l1_g_kva, w_ukv=l1_w_ukv, g_q=l1_g_q, g_k=l1_g_k,
              w_o=l1_w_o, w_rg=l1_w_rg, b_rg=l1_b_rg, w_re=l1_w_re, b_re=l1_b_re, w1=l1_w1, w3=l1_w3, w2=l1_w2)

    cond8 = jnp.zeros((8, D), f32).at[0].set(c_ctx).at[1:1 + DEC_BATCH].set(c)
    mods0 = _mod_rows(_adaln(cond8, l0_w_ada, l0_b_ada))
    mods1 = _mod_rows(_adaln(cond8, l1_w_ada, l1_b_ada))

    xc0 = x_prompt.reshape(T_CTX, D)
    xl0 = x_sample.reshape(T_LAT, D)

    xc0m, xl0m, k_new, v_new = _l0_mixer(xc0, xl0, l0_g_norm1, mods0, p0, cache_l0_k, cache_l0_v)
    h0, route0 = _router(xc0m, xl0m, l0_g_norm2, mods0, p0)
    moe0 = _moe(h0, route0, p0)

    mods01 = mods1.reshape(3, 8, D).at[:, 5].set(mods0.reshape(3, 8, D)[:, 5]).reshape(24, D)
    xc1m, xl1m, ckv_new, kpe_new = _l1_mixer(xc0m, xl0m, moe0, l1_g_norm1, mods01, p1, cache_l1_ckv, cache_l1_kpe)
    h1, route1 = _router(xc1m, xl1m, l1_g_norm2, mods1, p1)
    moe1 = _moe(h1, route1, p1)

    y_prompt = _final(xc1m, moe1, mods1, 0, SEQ)
    y_sample = _final(xl1m, moe1, mods1, T_CTX, DEC_SEQ)
    return (y_prompt, y_sample,
            k_new.reshape(BATCH, SEQ, B_KV, B_HD), v_new.reshape(BATCH, SEQ, B_KV, B_HD),
            ckv_new.reshape(BATCH, SEQ, C_KV_LORA), kpe_new.reshape(BATCH, SEQ, C_ROPE))
```

```python
import functools

import jax
import jax.numpy as jnp
from jax import lax
from jax.experimental import pallas as pl
from jax.experimental.pallas import tpu as pltpu

f32 = jnp.float32
bf16 = jnp.bfloat16
HIGHEST = lax.Precision.HIGHEST

D = 1024
BATCH, SEQ = 32, 256
DEC_BATCH, DEC_SEQ = 2, 1024
PAST = 512
T_CTX = BATCH * SEQ
T_LAT = DEC_BATCH * DEC_SEQ
T = T_CTX + T_LAT
GRID_W = 64
CHUNK = 128
WINDOW = 128
ROPE_THETA = 10000.0
EPS = 1e-6
NEG_INF = -1e30
LANES = 128
SUBS = D // LANES

A_WIDTH = 512
A_GROUPS = 4
B_HEADS, B_KV, B_GROUP, B_HD = 8, 2, 4, 64
B_SCALE = B_HD ** -0.5

C_HEADS, C_Q_LORA, C_KV_LORA, C_NOPE, C_ROPE, C_V = 16, 384, 256, 64, 32, 64
C_QK = C_NOPE + C_ROPE
C_SCALE = C_QK ** -0.5
ODD_IN_PAD = 768
SLOT = 128

N_GROUPS, EPG, N_EXPERTS, D_EXPERT = 4, 8, 32, 256

ROW_TILE = 512
BLOCK_ROWS = 1024
PROJ_ROWS = 512
MOE_TILE = 256
MOE_ROWS = 2 * T + N_EXPERTS * MOE_TILE
MOE_TILES = MOE_ROWS // MOE_TILE
VMEM_CAP = 56 * 1024 * 1024


def _cp(sem, vmem=VMEM_CAP):
    return pltpu.CompilerParams(dimension_semantics=sem, vmem_limit_bytes=vmem)


def _const_spec(shape):
    nd = len(shape)
    return pl.BlockSpec(shape, lambda *_: (0,) * nd, pipeline_mode=pl.Buffered(1))


def _sample_of_tile(i, tile):
    n_ctx = T_CTX // tile
    per_lat = DEC_SEQ // tile
    return jnp.where(i < n_ctx, 0, 1 + (i - n_ctx) // per_lat)


def _silu(x):
    return x * jax.nn.sigmoid(x)


def _rms_rows(x, g):
    return x * lax.rsqrt(jnp.mean(x * x, -1, keepdims=True) + EPS) * g


def _swap_pairs(x):
    lane = lax.broadcasted_iota(jnp.int32, x.shape, x.ndim - 1)
    nxt = pltpu.roll(x, x.shape[-1] - 1, x.ndim - 1)
    prv = pltpu.roll(x, 1, x.ndim - 1)
    return jnp.where((lane & 1) == 0, nxt, prv)


def _swap_halves(x):
    return pltpu.roll(x, LANES // 2, x.ndim - 1)


def _adaln_kernel(c_ref, w_ref, b_ref, o_ref):
    s = _silu(c_ref[...])
    o_ref[...] = jnp.dot(s, w_ref[...], precision=HIGHEST, preferred_element_type=f32) + b_ref[...]


def _adaln(cond8, w, b):
    n = w.shape[1]
    tn = 1536
    return pl.pallas_call(
        _adaln_kernel,
        grid=(n // tn,),
        in_specs=[_const_spec((8, D)), pl.BlockSpec((D, tn), lambda j: (0, j)),
                  pl.BlockSpec((1, tn), lambda j: (0, j))],
        out_specs=pl.BlockSpec((8, tn), lambda j: (0, j)),
        out_shape=jax.ShapeDtypeStruct((8, n), f32),
        compiler_params=_cp(("arbitrary",)),
        name="adaln",
    )(cond8, w, b.reshape(1, n))


def _mod_rows(m8):
    m = m8[:3].reshape(3, 6, D)
    return jnp.pad(m, ((0, 0), (0, 2), (0, 0))).reshape(24, D)


N_CTX_TILES = T_CTX // ROW_TILE


def _token_specs(width):
    return [pl.BlockSpec((ROW_TILE, width), lambda i: (jnp.minimum(i, N_CTX_TILES - 1), 0)),
            pl.BlockSpec((ROW_TILE, width), lambda i: (jnp.maximum(i - N_CTX_TILES, 0), 0))]


def _token_rows(xc_ref, xl_ref):
    return jnp.where(pl.program_id(0) < N_CTX_TILES, xc_ref[...], xl_ref[...])


def _store_token_major(ref, x):
    n = x.shape[0]
    for s in range(SUBS):
        ref[pl.ds(s, n, stride=SUBS), :] = x[:, s * LANES:(s + 1) * LANES]


def _load_token_major(ref, n, row0=0):
    return jnp.concatenate([ref[pl.ds(row0 * SUBS + s, n, stride=SUBS), :] for s in range(SUBS)], axis=1)


def _token_major_spec(rows, index_map):
    return pl.BlockSpec((rows * SUBS, LANES), index_map)


def _rope_tables(n, rot_dim, lanes, lane0):
    rows_count = n // GRID_W
    rows = jnp.repeat(jnp.arange(rows_count), GRID_W).astype(f32)
    cols = jnp.tile(jnp.arange(GRID_W), rows_count).astype(f32)
    d_axis = rot_dim // 2
    inv = ROPE_THETA ** (-jnp.arange(0, d_axis, 2, dtype=f32) / d_axis)
    ang = jnp.concatenate([rows[:, None] * inv, cols[:, None] * inv], -1)
    cos = jnp.repeat(jnp.cos(ang), 2, axis=1)
    sin = jnp.repeat(jnp.sin(ang), 2, axis=1) * jnp.tile(jnp.array([-1.0, 1.0], f32), rot_dim // 2)
    c = jnp.ones((n, lanes), f32).at[:, lane0:lane0 + rot_dim].set(cos)
    s = jnp.zeros((n, lanes), f32).at[:, lane0:lane0 + rot_dim].set(sin)
    return c, s


L0_Q0 = 2 * A_WIDTH
L0_K0 = L0_Q0 + B_HEADS * LANES
L0_V0 = L0_K0 + B_KV * B_HD
L0_IN = L0_V0 + B_KV * B_HD


def _l0_kernel(*refs, latent):
    if latent:
        (sink_ref, x_ref, gn_ref, mod_ref, win_ref, gvn_ref, ws_ref, bsb_ref, gq_ref, gk_ref, wo_ref,
         cos_ref, sin_ref, kc_ref, vc_ref, xo_ref, zs, cat, qs, ks, vt, kcb, vct) = refs
        key_off = WINDOW
    else:
        (sink_ref, x_ref, gn_ref, mod_ref, win_ref, gvn_ref, ws_ref, bsb_ref, gq_ref, gk_ref, wo_ref,
         xo_ref, ko_ref, vo_ref, zs, cat, qs, ks, vt, kf, vf) = refs
        key_off = 0
    n = BLOCK_ROWS
    n_chunks = n // CHUNK
    low = lax.broadcasted_iota(jnp.int32, (CHUNK, LANES), 1) < B_HD

    if latent:
        zpad = jnp.zeros((WINDOW, LANES), bf16)
        for c0 in (0, 1 + n_chunks):
            ks[c0 * CHUNK:(c0 + 1) * CHUNK, :] = zpad
            vt[c0] = zpad
        kcb[...] = kc_ref[...].astype(bf16)
        for i in range(PAST // CHUNK):
            vct[i] = vc_ref[i * CHUNK:(i + 1) * CHUNK, :].T.astype(bf16)

    def project(c, carry):
        r = pl.ds(pl.multiple_of(c * PROJ_ROWS, PROJ_ROWS), PROJ_ROWS)
        h = _rms_rows(x_ref[r, :], gn_ref[...]) * (1.0 + mod_ref[1:2, :]) + mod_ref[0:1, :]
        zs[r, :] = jnp.dot(h.astype(bf16), win_ref[...], preferred_element_type=f32)
        return carry

    lax.fori_loop(0, n // PROJ_ROWS, project, 0)

    def prepare(c, carry):
        r = pl.ds(pl.multiple_of(c * CHUNK, CHUNK), CHUNK)
        u = jax.nn.gelu(zs[r, 0:A_WIDTH])
        v = jax.nn.gelu(zs[r, A_WIDTH:2 * A_WIDTH])
        mu = jnp.mean(v, -1, keepdims=True)
        var = jnp.mean(jnp.square(v - mu), -1, keepdims=True)
        vn = ((v - mu) * lax.rsqrt(var + EPS) * gvn_ref[...]).astype(bf16)
        for g in range(A_GROUPS):
            cs = slice(g * CHUNK, (g + 1) * CHUNK)
            mixed = jnp.dot(ws_ref[g], vn[:, cs], preferred_element_type=f32) + bsb_ref[g]
            cat[r, cs] = (u[:, cs] * mixed).astype(bf16)
        if latent:
            cs_, sn_ = cos_ref[r, :], sin_ref[r, :]
        for h in range(B_HEADS):
            hs = slice(h * LANES, (h + 1) * LANES)
            qh = zs[r, L0_Q0 + h * LANES:L0_Q0 + (h + 1) * LANES]
            qh = qh * lax.rsqrt(jnp.sum(qh * qh, -1, keepdims=True) * (1.0 / B_HD) + EPS) * gq_ref[:, hs]
            if latent:
                qh = qh * cs_ + _swap_pairs(qh) * sn_
            qs[r, hs] = (qh * B_SCALE).astype(bf16)
        k = zs[r, L0_K0:L0_K0 + LANES]
        k2 = k * k
        s0 = jnp.sum(jnp.where(low, k2, 0.0), -1, keepdims=True)
        s1 = jnp.sum(jnp.where(low, 0.0, k2), -1, keepdims=True)
        k = k * lax.rsqrt(jnp.where(low, s0, s1) * (1.0 / B_HD) + EPS) * gk_ref[...]
        vv = zs[r, L0_V0:L0_V0 + LANES]
        if latent:
            k = k * cs_ + _swap_pairs(k) * sn_
        else:
            kf[r, :] = k
            vf[r, :] = vv
        kr = pl.ds(pl.multiple_of(c * CHUNK + key_off, CHUNK), CHUNK)
        ks[kr, :] = k.astype(bf16)
        vt[c + key_off // CHUNK] = vv.T.astype(bf16)
        return carry

    lax.fori_loop(0, n_chunks, prepare, 0)

    def attend(r, rows, key_sets):
        q = jnp.concatenate([qs[r, h * LANES:(h + 1) * LANES] for h in range(B_HEADS)], axis=0)
        sk = jnp.concatenate([jnp.full((1, rows), sink_ref[h], f32) for h in range(B_HEADS)], axis=1)
        scores = []
        m = sk
        for k, _, keep in key_sets:
            s = lax.dot_general(k, q, (((1,), (1,)), ((), ())), preferred_element_type=f32)
            if keep is not None:
                s = jnp.where(keep, s, NEG_INF)
            scores.append(s)
            m = jnp.maximum(m, jnp.max(s, 0, keepdims=True))
        den = jnp.exp(sk - m)
        ot = None
        for s, (_, vts, _) in zip(scores, key_sets):
            e = jnp.exp(s - m)
            den = den + jnp.sum(e, 0, keepdims=True)
            eb = e.astype(bf16)
            for i, v_t in enumerate(vts):
                pv = jnp.dot(v_t, eb[i * CHUNK:(i + 1) * CHUNK, :], preferred_element_type=f32)
                ot = pv if ot is None else ot + pv
        ot = ot * (1.0 / den)
        for pair in range(B_HEADS // 2):
            f0 = (2 * pair // B_GROUP) * B_HD
            pair_t = jnp.concatenate([ot[f0:f0 + B_HD, 2 * pair * rows:(2 * pair + 1) * rows],
                                      ot[f0:f0 + B_HD, (2 * pair + 1) * rows:(2 * pair + 2) * rows]], axis=0)
            cat[r, A_WIDTH + pair * LANES:A_WIDTH + (pair + 1) * LANES] = pair_t.T.astype(bf16)

    if latent:
        span = CHUNK + 2 * WINDOW

        def attend_block(c, carry):
            start = pl.multiple_of(c * CHUNK, CHUNK)
            kr = pl.ds(start, span)
            kj = lax.broadcasted_iota(jnp.int32, (span, B_HEADS * CHUNK), 0)
            qi = lax.broadcasted_iota(jnp.int32, (span, B_HEADS * CHUNK), 1) & (CHUNK - 1)
            kpos = start - WINDOW + kj
            keep = (jnp.abs(kj - WINDOW - qi) <= WINDOW) & (kpos >= 0) & (kpos < n)
            attend(pl.ds(start, CHUNK), CHUNK,
                   [(ks[kr, :], [vt[c + i] for i in range(span // CHUNK)], keep),
                    (kcb[...], [vct[i] for i in range(PAST // CHUNK)], None)])
            return carry

        lax.fori_loop(0, n_chunks, attend_block, 0)
    else:
        def attend_seq(sq, carry):
            r = pl.ds(pl.multiple_of(sq * SEQ, SEQ), SEQ)
            attend(r, SEQ, [(ks[r, :], [vt[sq * (SEQ // CHUNK) + i] for i in range(SEQ // CHUNK)], None)])
            ko_ref[sq] = kf[r, :].T
            vo_ref[sq] = vf[r, :].T
            return carry

        lax.fori_loop(0, n // SEQ, attend_seq, 0)

    def output(c, carry):
        r = pl.ds(pl.multiple_of(c * PROJ_ROWS, PROJ_ROWS), PROJ_ROWS)
        y = jnp.dot(cat[r, :], wo_ref[...], preferred_element_type=f32)
        xo_ref[r, :] = x_ref[r, :] + mod_ref[2:3, :] * y
        return carry

    lax.fori_loop(0, n // PROJ_ROWS, output, 0)


def _l0_mixer(xc, xl, g_norm, mods, p, cache_k, cache_v):
    w = p['w_in']
    q = w[:, 2 * A_WIDTH:2 * A_WIDTH + B_HEADS * B_HD].reshape(D, B_HEADS, B_HD)
    zero = jnp.zeros((D, B_GROUP, B_HD), f32)
    q_slots = jnp.concatenate([jnp.concatenate([q[:, :B_GROUP], zero], axis=2),
                               jnp.concatenate([zero, q[:, B_GROUP:]], axis=2)], axis=1).reshape(D, B_HEADS * LANES)
    win = jnp.concatenate([w[:, :2 * A_WIDTH], q_slots, w[:, 2 * A_WIDTH + B_HEADS * B_HD:]], axis=1).astype(bf16)
    zg = jnp.zeros((B_HD,), f32)
    gq = jnp.concatenate([jnp.tile(jnp.concatenate([p['g_q'], zg]), B_GROUP),
                          jnp.tile(jnp.concatenate([zg, p['g_q']]), B_GROUP)]).reshape(1, B_HEADS * LANES)
    gk = jnp.tile(p['g_k'], B_KV).reshape(1, LANES)
    gvn = p['g_vnorm'].reshape(1, A_WIDTH)
    ws = p['w_s'].astype(bf16)
    bsb = jnp.broadcast_to(p['b_s'][:, :, None], (A_GROUPS, CHUNK, CHUNK))
    wo = p['w_o'].astype(bf16)
    weights = (g_norm.reshape(1, D),)
    consts = (win, gvn, ws, bsb, gq, gk, wo)
    c_specs = [_const_spec(a.shape) for a in consts]
    smem = pl.BlockSpec(memory_space=pltpu.SMEM)
    row = pl.BlockSpec((BLOCK_ROWS, D), lambda b: (b, 0))
    kv = pl.BlockSpec((BLOCK_ROWS // SEQ, LANES, SEQ), lambda b: (b, 0, 0))

    def scratch(pad):
        return [pltpu.VMEM((BLOCK_ROWS, L0_IN), f32), pltpu.VMEM((BLOCK_ROWS, D), bf16),
                pltpu.VMEM((BLOCK_ROWS, B_HEADS * LANES), bf16), pltpu.VMEM((BLOCK_ROWS + pad, LANES), bf16),
                pltpu.VMEM(((BLOCK_ROWS + pad) // CHUNK, LANES, CHUNK), bf16)]

    kv_shape = jax.ShapeDtypeStruct((BATCH, LANES, SEQ), f32)
    state = pltpu.VMEM((BLOCK_ROWS, LANES), f32)
    xo_ctx, k_t, v_t = pl.pallas_call(
        functools.partial(_l0_kernel, latent=False),
        grid=(T_CTX // BLOCK_ROWS,),
        in_specs=[smem, row, _const_spec((1, D)), pl.BlockSpec((8, D), lambda b: (0, 0))] + c_specs,
        out_specs=[row, kv, kv],
        out_shape=[jax.ShapeDtypeStruct((T_CTX, D), f32), kv_shape, kv_shape],
        scratch_shapes=scratch(0) + [state, state],
        compiler_params=_cp(("parallel",)), name="l0_mixer_ctx",
    )(p['sink'], xc, *weights, mods, *consts)
    k_new = k_t.reshape(BATCH, B_KV, B_HD, SEQ).transpose(0, 3, 1, 2)
    v_new = v_t.reshape(BATCH, B_KV, B_HD, SEQ).transpose(0, 3, 1, 2)

    cos, sin = _rope_tables(DEC_SEQ, B_HD, LANES, 0)
    cos = cos.at[:, B_HD:].set(cos[:, :B_HD])
    sin = sin.at[:, B_HD:].set(sin[:, :B_HD])
    cache = pl.BlockSpec((None, PAST, LANES), lambda b: (b, 0, 0))
    past = [pltpu.VMEM((PAST, LANES), bf16), pltpu.VMEM((PAST // CHUNK, LANES, CHUNK), bf16)]
    xo_lat = pl.pallas_call(
        functools.partial(_l0_kernel, latent=True),
        grid=(DEC_BATCH,),
        in_specs=[smem, row, _const_spec((1, D)), pl.BlockSpec((8, D), lambda b: (1 + b, 0))] + c_specs + [
                  _const_spec(cos.shape), _const_spec(sin.shape), cache, cache],
        out_specs=row,
        out_shape=jax.ShapeDtypeStruct((T_LAT, D), f32),
        scratch_shapes=scratch(2 * WINDOW) + past,
        compiler_params=_cp(("parallel",)), name="l0_mixer_lat",
    )(p['sink'], xl, *weights, mods, *consts, cos, sin,
      cache_k.reshape(DEC_BATCH, PAST, LANES), cache_v.reshape(DEC_BATCH, PAST, LANES))
    return xo_ctx, xo_lat, k_new, v_new


C_SLOTS = C_HEADS * SLOT
C_PAIRS = C_HEADS // 2
L1_ROWS = 256


def _l1_kernel(*refs, latent):
    if latent:
        (x_ref, d_ref, gn_ref, mod_ref, win_ref, gqa_ref, wuq_ref, gq_ref, gkva_ref, wuk_ref, wuvt_ref, gk_ref,
         wo_ref, wuqs_ref, qcos_ref, qsin_ref, kcos_ref, ksin_ref, cckv_ref, ckpe_ref, xo_ref,
         zs, cat, qs, ks, vt, wide, wide2) = refs
        n_ctx = PAST
    else:
        (x_ref, d_ref, gn_ref, mod_ref, win_ref, gqa_ref, wuq_ref, gq_ref, gkva_ref, wuk_ref, wuvt_ref, gk_ref,
         wo_ref, xo_ref, ckvo_ref, kpeo_ref, zs, cat, qs, ks, vt, wide) = refs
        n_ctx = 0
    n = BLOCK_ROWS
    nt_dims = (((1,), (1,)), ((), ()))

    def inv_rms(v):
        return lax.rsqrt(jnp.sum(v * v, -1, keepdims=True) * (1.0 / C_QK) + EPS)

    def expand_keys(ckv_n, kslot, kb, rope_rows):
        cb = ckv_n.astype(bf16)
        key_rows = pl.ds(pl.multiple_of(kb * L1_ROWS, L1_ROWS), L1_ROWS)
        wide[...] = jnp.dot(cb, wuk_ref[...], preferred_element_type=f32)
        if rope_rows is not None:
            kcos = kcos_ref[rope_rows, :]
            turned = _swap_pairs(kslot) * ksin_ref[rope_rows, :]
        for h in range(C_HEADS):
            kh = wide[:, h * SLOT:(h + 1) * SLOT] + kslot
            if rope_rows is not None:
                kh = inv_rms(kh) * (kh * kcos + turned)
            else:
                kh = kh * inv_rms(kh) * gk_ref[...]
            ks[h, key_rows, :] = kh.astype(bf16)
        v_t = lax.dot_general(wuvt_ref[...], cb, nt_dims, preferred_element_type=f32).astype(bf16)
        for pair in range(C_PAIRS):
            vt[pair, kb] = v_t[pair * LANES:(pair + 1) * LANES, :]

    if latent:
        def past_keys(c, carry):
            r = pl.ds(pl.multiple_of(c * L1_ROWS, L1_ROWS), L1_ROWS)
            expand_keys(cckv_ref[r, :], ckpe_ref[r, :], c, None)
            return carry

        lax.fori_loop(0, PAST // L1_ROWS, past_keys, 0)

    for c in range(n // PROJ_ROWS):
        r = pl.ds(c * PROJ_ROWS, PROJ_ROWS)
        x1 = x_ref[r, :] + mod_ref[5:6, :] * _load_token_major(d_ref, PROJ_ROWS, c * PROJ_ROWS)
        xo_ref[r, :] = x1
        h = _rms_rows(x1, gn_ref[...]) * (1.0 + mod_ref[1:2, :]) + mod_ref[0:1, :]
        zs[r, :] = jnp.dot(h.astype(bf16), win_ref[...], preferred_element_type=f32)

    def prepare(c, carry):
        r = pl.ds(pl.multiple_of(c * L1_ROWS, L1_ROWS), L1_ROWS)
        qa = _rms_rows(zs[r, 0:C_Q_LORA], gqa_ref[...]).astype(bf16)
        wide[...] = jnp.dot(qa, wuq_ref[...], preferred_element_type=f32)
        if latent:
            wide2[...] = jnp.dot(qa, wuqs_ref[...], preferred_element_type=f32)
            qcos, qsin = qcos_ref[r, :], qsin_ref[r, :]
        for h in range(C_HEADS):
            hs = slice(h * SLOT, (h + 1) * SLOT)
            qh = wide[:, hs]
            if latent:
                qh = inv_rms(qh) * (qh * qcos + wide2[:, hs] * qsin)
            else:
                qh = qh * inv_rms(qh) * gq_ref[...]
            qs[h, r, :] = (qh * C_SCALE).astype(bf16)
        ckv_n = _rms_rows(zs[r, C_Q_LORA:C_Q_LORA + C_KV_LORA], gkva_ref[...])
        kslot = zs[r, C_Q_LORA + C_KV_LORA:ODD_IN_PAD]
        if not latent:
            ckvo_ref[r, :] = ckv_n
            kpeo_ref[r, :] = kslot
        expand_keys(ckv_n, kslot, c + n_ctx // L1_ROWS, r if latent else None)
        return carry

    lax.fori_loop(0, n // L1_ROWS, prepare, 0)

    low = lax.broadcasted_iota(jnp.int32, (2 * C_V, L1_ROWS), 0) < C_V
    n_kblocks = (n_ctx + n) // L1_ROWS
    pairs_per_step = 2 if latent else 4

    def attend(c, carry):
        r = pl.ds(pl.multiple_of(c * L1_ROWS, L1_ROWS), L1_ROWS)

        def values_t(pair, eb):
            if not latent:
                return jnp.dot(vt[pair, c], eb, preferred_element_type=f32)
            o_t = None
            for b in range(n_kblocks):
                pv = jnp.dot(vt[pair, b], eb[b * L1_ROWS:(b + 1) * L1_ROWS, :], preferred_element_type=f32)
                o_t = pv if o_t is None else o_t + pv
            return o_t

        def pairs_step(i, carry2):
            pairs = [i * pairs_per_step + j for j in range(pairs_per_step)]
            heads = [2 * p + hh for p in pairs for hh in range(2)]
            scores = [lax.dot_general(ks[h] if latent else ks[h, r, :], qs[h, r, :], nt_dims,
                                      preferred_element_type=f32) for h in heads]
            exps = [jnp.exp(s - jnp.max(s, 0, keepdims=True)) for s in scores]
            dens = [jnp.sum(e, 0, keepdims=True) for e in exps]
            outs = [values_t(h // 2, e.astype(bf16)) / den for h, e, den in zip(heads, exps, dens)]
            for j, pair in enumerate(pairs):
                cat[pair, r, :] = jnp.where(low, outs[2 * j], outs[2 * j + 1]).T.astype(bf16)
            return carry2

        return lax.fori_loop(0, C_PAIRS // pairs_per_step, pairs_step, carry)

    lax.fori_loop(0, n // L1_ROWS, attend, 0)

    def output(c, carry):
        r = pl.ds(pl.multiple_of(c * PROJ_ROWS, PROJ_ROWS), PROJ_ROWS)
        heads = jnp.concatenate([cat[pair, r, :] for pair in range(C_PAIRS)], axis=1)
        y = jnp.dot(heads, wo_ref[...], preferred_element_type=f32)
        xo_ref[r, :] = xo_ref[r, :] + mod_ref[2:3, :] * y
        return carry

    lax.fori_loop(0, n // PROJ_ROWS, output, 0)


def _slot_cols(w, heads, width, lo, hi, lane0):
    k = w.shape[0]
    w3 = w.reshape(k, heads, width)[:, :, lo:hi]
    out = jnp.zeros((k, heads, SLOT), w.dtype).at[:, :, lane0:lane0 + (hi - lo)].set(w3)
    return out.reshape(k, heads * SLOT)


def _l1_mixer(xc, xl, moe_prev, g_norm, mods, p, cache_ckv, cache_kpe):
    w_in = jnp.zeros((D, ODD_IN_PAD), f32).at[:, :C_Q_LORA + C_KV_LORA].set(
        p['w_in'][:, :C_Q_LORA + C_KV_LORA]).at[
        :, C_Q_LORA + C_KV_LORA + C_NOPE:C_Q_LORA + C_KV_LORA + C_QK].set(p['w_in'][:, C_Q_LORA + C_KV_LORA:])
    wuq = _slot_cols(p['w_uq'], C_HEADS, C_QK, 0, C_QK, 0).astype(bf16)
    wuk = _slot_cols(p['w_ukv'], C_HEADS, C_NOPE + C_V, 0, C_NOPE, 0).astype(bf16)
    wuv_t = p['w_ukv'].reshape(C_KV_LORA, C_HEADS, C_NOPE + C_V)[:, :, C_NOPE:].reshape(
        C_KV_LORA, C_HEADS * C_V).T.astype(bf16)
    gq = jnp.zeros((1, SLOT), f32).at[0, :C_QK].set(p['g_q'])
    gk = jnp.zeros((1, SLOT), f32).at[0, :C_QK].set(p['g_k'])
    consts = (g_norm.reshape(1, D), w_in.astype(bf16), p['g_qa'].reshape(1, C_Q_LORA), wuq, gq,
              p['g_kva'].reshape(1, C_KV_LORA), wuk, wuv_t, gk, p['w_o'].astype(bf16))
    c_specs = [_const_spec(a.shape) for a in consts]
    row = pl.BlockSpec((BLOCK_ROWS, D), lambda b: (b, 0))
    n_ctx_blocks = T_CTX // BLOCK_ROWS

    def scratch(n_keys):
        return [pltpu.VMEM((BLOCK_ROWS, ODD_IN_PAD), f32), pltpu.VMEM((C_PAIRS, BLOCK_ROWS, LANES), bf16),
                pltpu.VMEM((C_HEADS, BLOCK_ROWS, SLOT), bf16), pltpu.VMEM((C_HEADS, n_keys, SLOT), bf16),
                pltpu.VMEM((C_PAIRS, n_keys // L1_ROWS, LANES, L1_ROWS), bf16),
                pltpu.VMEM((L1_ROWS, C_SLOTS), f32)]

    xo_ctx, ckv_new, kpe_slot = pl.pallas_call(
        functools.partial(_l1_kernel, latent=False),
        grid=(n_ctx_blocks,),
        in_specs=[row, _token_major_spec(BLOCK_ROWS, lambda b: (b, 0)), c_specs[0],
                  pl.BlockSpec((8, D), lambda b: (0, 0))] + c_specs[1:],
        out_specs=[row, pl.BlockSpec((BLOCK_ROWS, C_KV_LORA), lambda b: (b, 0)),
                   pl.BlockSpec((BLOCK_ROWS, SLOT), lambda b: (b, 0))],
        out_shape=[jax.ShapeDtypeStruct((T_CTX, D), f32), jax.ShapeDtypeStruct((T_CTX, C_KV_LORA), f32),
                   jax.ShapeDtypeStruct((T_CTX, SLOT), f32)],
        scratch_shapes=scratch(BLOCK_ROWS),
        compiler_params=_cp(("parallel",)), name="l1_mixer_ctx",
    )(xc, moe_prev, consts[0], mods, *consts[1:])

    cos, sin = _rope_tables(DEC_SEQ, C_ROPE, SLOT, C_NOPE)
    pair_swap = lambda a: a.reshape(a.shape[:-1] + (a.shape[-1] // 2, 2))[..., ::-1].reshape(a.shape)
    rope = (pair_swap(wuq), gq * cos, pair_swap(gq) * sin, gk * cos, pair_swap(gk) * sin)
    ckpe = jnp.zeros((DEC_BATCH, PAST, SLOT), f32).at[:, :, C_NOPE:C_QK].set(cache_kpe)
    xo_lat = pl.pallas_call(
        functools.partial(_l1_kernel, latent=True),
        grid=(DEC_BATCH,),
        in_specs=[pl.BlockSpec((BLOCK_ROWS, D), lambda b: (b, 0), pipeline_mode=pl.Buffered(1)),
                  pl.BlockSpec((BLOCK_ROWS * SUBS, LANES), lambda b: (n_ctx_blocks + b, 0),
                               pipeline_mode=pl.Buffered(1)), c_specs[0],
                  pl.BlockSpec((8, D), lambda b: (1 + b, 0))] + c_specs[1:] + [_const_spec(a.shape) for a in rope] + [
                  pl.BlockSpec((None, PAST, C_KV_LORA), lambda b: (b, 0, 0)),
                  pl.BlockSpec((None, PAST, SLOT), lambda b: (b, 0, 0))],
        out_specs=row,
        out_shape=jax.ShapeDtypeStruct((T_LAT, D), f32),
        scratch_shapes=scratch(PAST + BLOCK_ROWS) + [pltpu.VMEM((L1_ROWS, C_SLOTS), f32)],
        compiler_params=_cp(("parallel",)), name="l1_mixer_lat",
    )(xl, moe_prev, consts[0], mods, *consts[1:], *rope, cache_ckv, ckpe)
    return xo_ctx, xo_lat, ckv_new, kpe_slot[:, C_NOPE:C_QK]


ROUTER_ROWS = 40


def _router_kernel(xc_ref, xl_ref, gn_ref, mod_ref, whi_ref, wlo_ref, br_ref, h_ref, route_ref):
    h = _rms_rows(_token_rows(xc_ref, xl_ref), gn_ref[...]) * (1.0 + mod_ref[4:5, :]) + mod_ref[3:4, :]
    _store_token_major(h_ref, h)
    h_hi = h.astype(bf16)
    h_lo = (h - h_hi.astype(f32)).astype(bf16)
    nt = (((1,), (1,)), ((), ()))
    logits = (lax.dot_general(whi_ref[...], h_hi, nt, preferred_element_type=f32)
              + lax.dot_general(whi_ref[...], h_lo, nt, preferred_element_type=f32)
              + lax.dot_general(wlo_ref[...], h_hi, nt, preferred_element_type=f32))
    logits = logits[0:ROUTER_ROWS, :] + br_ref[0:ROUTER_ROWS, :]
    row_i = lax.broadcasted_iota(jnp.int32, logits.shape, 0)
    row = row_i.astype(f32)
    big = 1e6
    is_g = (row_i >= N_EXPERTS) & (row_i < N_EXPERTS + N_GROUPS)
    lg = jnp.where(is_g, logits, -jnp.inf)
    mg = jnp.max(lg, 0, keepdims=True)
    gsel = jnp.min(jnp.where(lg == mg, row, big), 0, keepdims=True) - N_EXPERTS
    pg_sel = 1.0 / jnp.sum(jnp.where(is_g, jnp.exp(lg - mg), 0.0), 0, keepdims=True)
    in_grp = (row_i < N_EXPERTS) & ((row_i >> 3).astype(f32) == gsel)
    le = jnp.where(in_grp, logits, -jnp.inf)
    m1 = jnp.max(le, 0, keepdims=True)
    i1 = jnp.min(jnp.where(le == m1, row, big), 0, keepdims=True)
    le2 = jnp.where(row == i1, -jnp.inf, le)
    m2 = jnp.max(le2, 0, keepdims=True)
    i2 = jnp.min(jnp.where(le2 == m2, row, big), 0, keepdims=True)
    e2 = jnp.exp(m2 - m1)
    w1 = pg_sel / (1.0 + e2)
    w2 = pg_sel * e2 / (1.0 + e2)
    sub = lax.broadcasted_iota(jnp.int32, route_ref.shape, 0)
    route_ref[...] = jnp.where(sub == 0, i1, jnp.where(sub == 1, i2, jnp.where(sub == 2, w1,
                                                                                jnp.where(sub == 3, w2, 0.0))))


def _router(xc, xl, g_norm, mods, p):
    wr = jnp.zeros((128, D), f32).at[:N_EXPERTS].set(p['w_re'].T).at[
        N_EXPERTS:N_EXPERTS + N_GROUPS].set(p['w_rg'].T)
    w_hi = wr.astype(bf16)
    w_lo = (wr - w_hi.astype(f32)).astype(bf16)
    br = jnp.zeros((128, 1), f32).at[:N_EXPERTS, 0].set(p['b_re']).at[
        N_EXPERTS:N_EXPERTS + N_GROUPS, 0].set(p['b_rg'])
    return pl.pallas_call(
        _router_kernel,
        grid=(T // ROW_TILE,),
        in_specs=_token_specs(D) + [
                  _const_spec((1, D)),
                  pl.BlockSpec((8, D), lambda i: (_sample_of_tile(i, ROW_TILE), 0)),
                  _const_spec((128, D)), _const_spec((128, D)), _const_spec((128, 1))],
        out_specs=[_token_major_spec(ROW_TILE, lambda i: (i, 0)), pl.BlockSpec((8, ROW_TILE), lambda i: (0, i))],
        out_shape=[jax.ShapeDtypeStruct((T * SUBS, LANES), f32), jax.ShapeDtypeStruct((8, T), f32)],
        compiler_params=_cp(("parallel",)), name="router",
    )(xc, xl, g_norm.reshape(1, D), mods, w_hi, w_lo, br)


TE_LANES = 256
PLAN_TILE_EXPERT, PLAN_N_USED, PLAN_PAD_LO, PLAN_PAD_HI = 0, 1, 2, 3


def _plan_kernel(rt_ref, pos_ref, te_ref, rank):
    n_blk = T // 128
    e_col = lax.broadcasted_iota(jnp.int32, (N_EXPERTS, 128), 0).astype(f32)
    ri = lax.broadcasted_iota(jnp.int32, (128, 128), 0)
    ci = lax.broadcasted_iota(jnp.int32, (128, 128), 1)
    before = jnp.where(ri < ci, 1.0, 0.0).astype(bf16)

    def picks(b):
        cs = slice(b * 128, (b + 1) * 128)
        return rt_ref[0:1, cs] == e_col, rt_ref[1:2, cs] == e_col

    counts = jnp.zeros((N_EXPERTS, 1), f32)
    for b in range(n_blk):
        m0, m1 = picks(b)
        m = jnp.where(m0, 1.0, 0.0) + jnp.where(m1, 1.0, 0.0)
        rank[:, b * 128:(b + 1) * 128] = jnp.dot(m.astype(bf16), before, preferred_element_type=f32) + counts
        counts = counts + jnp.sum(m, axis=1, keepdims=True)

    tiles = jnp.floor((counts + (MOE_TILE - 1.0)) * (1.0 / MOE_TILE))
    er = lax.broadcasted_iota(jnp.int32, (N_EXPERTS, N_EXPERTS), 0)
    ec = lax.broadcasted_iota(jnp.int32, (N_EXPERTS, N_EXPERTS), 1)
    earlier = jnp.where(ec < er, 1.0, 0.0).astype(bf16)
    tile_start = jnp.dot(earlier, jnp.broadcast_to(tiles, (N_EXPERTS, 128)).astype(bf16),
                         preferred_element_type=f32)
    row_start = tile_start * MOE_TILE

    sub = lax.broadcasted_iota(jnp.int32, (8, 128), 0)
    for b in range(n_blk):
        m0, m1 = picks(b)
        base = rank[:, b * 128:(b + 1) * 128] + row_start
        p0 = jnp.sum(jnp.where(m0, base, 0.0), axis=0, keepdims=True)
        p1 = jnp.sum(jnp.where(m1, base, 0.0), axis=0, keepdims=True)
        pos_ref[:, b * 128:(b + 1) * 128] = jnp.where(sub == 0, p0, jnp.where(sub == 1, p1, 0.0)).astype(jnp.int32)

    tile_end = tile_start + tiles
    n_used = jnp.max(tile_end, axis=0, keepdims=True)
    diag = (lax.broadcasted_iota(jnp.int32, (N_EXPERTS, 128), 0)
            == lax.broadcasted_iota(jnp.int32, (N_EXPERTS, 128), 1))
    pad_lo = jnp.sum(jnp.where(diag, row_start + counts, 0.0), axis=0, keepdims=True)
    pad_hi = jnp.sum(jnp.where(diag, tile_end * MOE_TILE, 0.0), axis=0, keepdims=True)
    for j in range(TE_LANES // 128):
        t = (lax.broadcasted_iota(jnp.int32, (N_EXPERTS, 128), 1) + j * 128).astype(f32)
        te = jnp.minimum(jnp.sum(jnp.where(tile_end <= t, 1.0, 0.0), axis=0, keepdims=True), N_EXPERTS - 1.0)
        rows = jnp.where(sub == PLAN_TILE_EXPERT, te, jnp.where(sub == PLAN_N_USED, n_used, 0.0))
        if j == 0:
            rows = jnp.where(sub == PLAN_PAD_LO, pad_lo, jnp.where(sub == PLAN_PAD_HI, pad_hi, rows))
        te_ref[:, j * 128:(j + 1) * 128] = rows.astype(jnp.int32)


def _slot_code(t, k):
    return t * SUBS + k * (SUBS // 2)


def _code_offset(code):
    return pl.multiple_of(code & ~(SUBS - 1), SUBS)


def _code_gate_index(code):
    return code >> 2


PAD_CODE = T * SUBS


def _invert_kernel(pos_ref, plan_ref, code_ref):
    def pad(s, carry):
        code_ref[s] = PAD_CODE
        return carry

    def pads(e, carry):
        return lax.fori_loop(plan_ref[PLAN_PAD_LO, e], plan_ref[PLAN_PAD_HI, e], pad, carry)
    lax.fori_loop(0, N_EXPERTS, pads, 0)
    lax.fori_loop(plan_ref[PLAN_N_USED, 0] * MOE_TILE, MOE_ROWS, pad, 0)

    for k in range(2):
        def place(i, carry):
            for u in range(8):
                t = i * 8 + u
                code_ref[pos_ref[k * T + t]] = _slot_code(t, k)
            return carry
        lax.fori_loop(0, T // 8, place, 0)


def _route_plan(route_t):
    pos, plan = pl.pallas_call(
        _plan_kernel,
        out_shape=[jax.ShapeDtypeStruct((8, T), jnp.int32), jax.ShapeDtypeStruct((8, TE_LANES), jnp.int32)],
        scratch_shapes=[pltpu.VMEM((N_EXPERTS, T), f32)],
        compiler_params=_cp(None), name="route_plan",
    )(route_t)
    smem = pl.BlockSpec(memory_space=pltpu.SMEM)
    codes = pl.pallas_call(
        _invert_kernel,
        in_specs=[smem, smem], out_specs=smem,
        out_shape=jax.ShapeDtypeStruct((MOE_ROWS,), jnp.int32),
        name="route_invert",
    )(pos[0:2].reshape(2 * T), plan)
    gates = jnp.pad(route_t[2:4].T.reshape(2 * T), (0, 8))
    return plan[PLAN_TILE_EXPERT, :MOE_TILES], plan[PLAN_N_USED, :1], codes, gates


def _tile_index(i):
    return jnp.minimum(i, MOE_TILES - 1)


def _first_tile_of_expert(i, te_ref):
    return (i == 0) | (te_ref[_tile_index(i)] != te_ref[_tile_index(jnp.maximum(i - 1, 0))])


TM_ROWS = T * SUBS
SCATTER_GROUP = 16


def _gather_tile(code_ref, tile, xs, gbuf):
    base = tile * MOE_TILE
    for r in range(MOE_TILE):
        gbuf[r * SUBS:(r + 1) * SUBS, :] = xs[pl.ds(_code_offset(code_ref[base + r]), SUBS), :]


def _moe_up_kernel(te_ref, nu_ref, code_ref, h_hbm, w1_ref, w3_ref, o_ref, xs, gbuf_a, gbuf_b, w13, sem):
    i = pl.program_id(0)
    used = i < nu_ref[0]

    @pl.when(i == 0)
    def _():
        cp = pltpu.make_async_copy(h_hbm, xs.at[pl.ds(0, TM_ROWS), :], sem)
        cp.start()
        xs[TM_ROWS:TM_ROWS + SUBS, :] = jnp.zeros((SUBS, LANES), f32)
        cp.wait()
        _gather_tile(code_ref, 0, xs, gbuf_a)

    @pl.when(used & _first_tile_of_expert(i, te_ref))
    def _():
        w13[:, :D_EXPERT] = w1_ref[0].astype(bf16)
        w13[:, D_EXPERT:] = w3_ref[0].astype(bf16)

    def step(cur, nxt):
        _gather_tile(code_ref, _tile_index(i + 1), xs, nxt)
        x = _load_token_major(cur, MOE_TILE).astype(bf16)
        h13 = jnp.dot(x, w13[...], preferred_element_type=f32)
        o_ref[...] = (_silu(h13[:, :D_EXPERT]) * h13[:, D_EXPERT:]).astype(bf16)

    pl.when(used & (i % 2 == 0))(functools.partial(step, gbuf_a, gbuf_b))
    pl.when(used & (i % 2 == 1))(functools.partial(step, gbuf_b, gbuf_a))

    @pl.when(jnp.logical_not(used))
    def _():
        o_ref[...] = jnp.zeros_like(o_ref)


def _scatter_tile(code_ref, gate_ref, tile, ybuf, acc):
    base = tile * MOE_TILE
    for g0 in range(0, MOE_TILE, SCATTER_GROUP):
        rows = range(g0, g0 + SCATTER_GROUP)
        codes = [code_ref[base + r] for r in rows]
        new = [acc[pl.ds(_code_offset(c), SUBS), :]
               + gate_ref[_code_gate_index(c)] * ybuf[r * SUBS:(r + 1) * SUBS, :]
               for r, c in zip(rows, codes)]
        for c, v in zip(codes, new):
            acc[pl.ds(_code_offset(c), SUBS), :] = v


def _moe_down_kernel(te_ref, nu_ref, code_ref, gate_ref, hh_ref, w2_ref, o_hbm, acc, ybuf_a, ybuf_b, w2b, sem):
    i = pl.program_id(0)
    n_used = nu_ref[0]

    @pl.when(i == 0)
    def _():
        def zero(c, carry):
            acc[pl.ds(pl.multiple_of(c * 1024, 1024), 1024), :] = jnp.zeros((1024, LANES), f32)
            return carry
        lax.fori_loop(0, TM_ROWS // 1024, zero, 0)
        acc[TM_ROWS:TM_ROWS + SUBS, :] = jnp.zeros((SUBS, LANES), f32)
        ybuf_b[...] = jnp.zeros_like(ybuf_b)

    @pl.when((i < n_used) & _first_tile_of_expert(i, te_ref))
    def _():
        w2b[...] = w2_ref[0].astype(bf16)

    def step(cur, prev):
        _store_token_major(cur, jnp.dot(hh_ref[...], w2b[...], preferred_element_type=f32))
        _scatter_tile(code_ref, gate_ref, jnp.maximum(i - 1, 0), prev, acc)

    pl.when((i <= n_used) & (i % 2 == 0))(functools.partial(step, ybuf_a, ybuf_b))
    pl.when((i <= n_used) & (i % 2 == 1))(functools.partial(step, ybuf_b, ybuf_a))

    @pl.when(i == MOE_TILES)
    def _():
        cp = pltpu.make_async_copy(acc.at[pl.ds(0, TM_ROWS), :], o_hbm, sem)
        cp.start()
        cp.wait()


def _moe(h_tm, route_t, p):
    tile_expert, n_used, codes, gates = _route_plan(route_t)
    tile_rows = pltpu.VMEM((MOE_TILE * SUBS, LANES), f32)
    hh = pl.pallas_call(
        _moe_up_kernel,
        grid_spec=pltpu.PrefetchScalarGridSpec(
            num_scalar_prefetch=3, grid=(MOE_TILES,),
            in_specs=[pl.BlockSpec(memory_space=pl.ANY),
                      pl.BlockSpec((1, D, D_EXPERT), lambda i, te, nu, c: (te[i], 0, 0)),
                      pl.BlockSpec((1, D, D_EXPERT), lambda i, te, nu, c: (te[i], 0, 0))],
            out_specs=pl.BlockSpec((MOE_TILE, D_EXPERT), lambda i, te, nu, c: (i, 0)),
            scratch_shapes=[pltpu.VMEM((TM_ROWS + SUBS, LANES), f32), tile_rows, tile_rows,
                            pltpu.VMEM((D, 2 * D_EXPERT), bf16), pltpu.SemaphoreType.DMA(())]),
        out_shape=jax.ShapeDtypeStruct((MOE_ROWS, D_EXPERT), bf16),
        compiler_params=_cp(("arbitrary",)), name="moe_up",
    )(tile_expert, n_used, codes, h_tm, p['w1'], p['w3'])
    return pl.pallas_call(
        _moe_down_kernel,
        grid_spec=pltpu.PrefetchScalarGridSpec(
            num_scalar_prefetch=4, grid=(MOE_TILES + 1,),
            in_specs=[pl.BlockSpec((MOE_TILE, D_EXPERT), lambda i, te, nu, c, g: (_tile_index(i), 0)),
                      pl.BlockSpec((1, D_EXPERT, D), lambda i, te, nu, c, g: (te[_tile_index(i)], 0, 0))],
            out_specs=pl.BlockSpec(memory_space=pl.ANY),
            scratch_shapes=[pltpu.VMEM((TM_ROWS + SUBS, LANES), f32), tile_rows, tile_rows,
                            pltpu.VMEM((D_EXPERT, D), bf16), pltpu.SemaphoreType.DMA(())]),
        out_shape=jax.ShapeDtypeStruct((TM_ROWS, LANES), f32),
        compiler_params=_cp(("arbitrary",)), name="moe_down",
    )(tile_expert, n_used, codes, gates, hh, p['w2'])


def _final_kernel(x_ref, d_ref, mod_ref, o_ref):
    o_ref[...] = x_ref[...] + mod_ref[5:6, :] * _load_token_major(d_ref, ROW_TILE)


def _final(x, delta, mods, row0, seq):
    n_rows = x.shape[0]
    t0 = row0 // ROW_TILE
    return pl.pallas_call(
        _final_kernel,
        grid=(n_rows // ROW_TILE,),
        in_specs=[pl.BlockSpec((ROW_TILE, D), lambda i: (i, 0)),
                  _token_major_spec(ROW_TILE, lambda i: (t0 + i, 0)),
                  pl.BlockSpec((8, D), lambda i: (_sample_of_tile(t0 + i, ROW_TILE), 0))],
        out_specs=pl.BlockSpec((ROW_TILE, D), lambda i: (i, 0)),
        out_shape=jax.ShapeDtypeStruct((n_rows, D), f32),
        compiler_params=_cp(("parallel",)), name="final",
    )(x, delta, mods).reshape(n_rows // seq, seq, D)


def kernel(x_prompt, x_sample, cache_l0_k, cache_l0_v, cache_l1_ckv, cache_l1_kpe, c, c_ctx, l0_g_norm1, l0_g_norm2, l0_w_ada, l0_b_ada, l0_w_in, l0_g_vnorm, l0_w_s, l0_b_s, l0_g_q, l0_g_k, l0_sink, l0_w_o, l0_w_rg, l0_b_rg, l0_w_re, l0_b_re, l0_w1, l0_w3, l0_w2, l1_g_norm1, l1_g_norm2, l1_w_ada, l1_b_ada, l1_w_in, l1_g_qa, l1_w_uq, l1_g_kva, l1_w_ukv, l1_g_q, l1_g_k, l1_w_o, l1_w_rg, l1_b_rg, l1_w_re, l1_b_re, l1_w1, l1_w3, l1_w2):
    p0 = dict(w_in=l0_w_in, g_vnorm=l0_g_vnorm, w_s=l0_w_s, b_s=l0_b_s, g_q=l0_g_q, g_k=l0_g_k, sink=l0_sink,
              w_o=l0_w_o, w_rg=l0_w_rg, b_rg=l0_b_rg, w_re=l0_w_re, b_re=l0_b_re, w1=l0_w1, w3=l0_w3, w2=l0_w2)
    p1 = dict(w_in=l1_w_in, g_qa=l1_g_qa, w_uq=l1_w_uq, g_kva=l1_g_kva, w_ukv=l1_w_ukv, g_q=l1_g_q, g_k=l1_g_k,
              w_o=l1_w_o, w_rg=l1_w_rg, b_rg=l1_b_rg, w_re=l1_w_re, b_re=l1_b_re, w1=l1_w1, w3=l1_w3, w2=l1_w2)

    cond8 = jnp.zeros((8, D), f32).at[0].set(c_ctx).at[1:1 + DEC_BATCH].set(c)
    mods0 = _mod_rows(_adaln(cond8, l0_w_ada, l0_b_ada))
    mods1 = _mod_rows(_adaln(cond8, l1_w_ada, l1_b_ada))

    xc0 = x_prompt.reshape(T_CTX, D)
    xl0 = x_sample.reshape(T_LAT, D)

    xc0m, xl0m, k_new, v_new = _l0_mixer(xc0, xl0, l0_g_norm1, mods0, p0, cache_l0_k, cache_l0_v)
    h0, route0 = _router(xc0m, xl0m, l0_g_norm2, mods0, p0)
    moe0 = _moe(h0, route0, p0)

    mods01 = mods1.reshape(3, 8, D).at[:, 5].set(mods0.reshape(3, 8, D)[:, 5]).reshape(24, D)
    xc1m, xl1m, ckv_new, kpe_new = _l1_mixer(xc0m, xl0m, moe0, l1_g_norm1, mods01, p1, cache_l1_ckv, cache_l1_kpe)
    h1, route1 = _router(xc1m, xl1m, l1_g_norm2, mods1, p1)
    moe1 = _moe(h1, route1, p1)

    y_prompt = _final(xc1m, moe1, mods1, 0, SEQ)
    y_sample = _final(xl1m, moe1, mods1, T_CTX, DEC_SEQ)
    return (y_prompt, y_sample, k_new, v_new,
            ckv_new.reshape(BATCH, SEQ, C_KV_LORA), kpe_new.reshape(BATCH, SEQ, C_ROPE))
```

```python
import functools

import jax
import jax.numpy as jnp
import numpy as np
from jax import lax
from jax.experimental import pallas as pl
from jax.experimental.pallas import tpu as pltpu

f32 = jnp.float32
bf16 = jnp.bfloat16

D = 1024
BATCH, SEQ = 32, 256
DEC_BATCH, DEC_SEQ = 2, 1024
PAST = 512
T_CTX = BATCH * SEQ
T_LAT = DEC_BATCH * DEC_SEQ
T = T_CTX + T_LAT
GRID_W = 64
CHUNK = 128
WINDOW = 128
ROPE_THETA = 10000.0
EPS = 1e-6
NEG_INF = -1e30
LANES = 128
SUBS = D // LANES

A_WIDTH = 512
A_GROUPS = 4
B_HEADS, B_KV, B_GROUP, B_HD = 8, 2, 4, 64
B_SCALE = B_HD ** -0.5

C_HEADS, C_Q_LORA, C_KV_LORA, C_NOPE, C_ROPE, C_V = 16, 384, 256, 64, 32, 64
C_QK = C_NOPE + C_ROPE
C_SCALE = C_QK ** -0.5
ODD_IN_PAD = 768
SLOT = 128

N_GROUPS, EPG, N_EXPERTS, D_EXPERT = 4, 8, 32, 256

ROW_TILE = 512
BLOCK_ROWS = 1024
PROJ_ROWS = 512
MOE_TILE = 256
MOE_ROWS = 2 * T + N_EXPERTS * MOE_TILE
MOE_TILES = MOE_ROWS // MOE_TILE
VMEM_CAP = 56 * 1024 * 1024


def _cp(sem, vmem=VMEM_CAP):
    return pltpu.CompilerParams(dimension_semantics=sem, vmem_limit_bytes=vmem)


def _const_spec(shape):
    nd = len(shape)
    return pl.BlockSpec(shape, lambda *_: (0,) * nd, pipeline_mode=pl.Buffered(1))


def _sample_of_tile(i, tile):
    n_ctx = T_CTX // tile
    per_lat = DEC_SEQ // tile
    return jnp.where(i < n_ctx, 0, 1 + (i - n_ctx) // per_lat)


def _silu(x):
    return x * jax.nn.sigmoid(x)


def _rms_rows(x, g):
    return x * lax.rsqrt(jnp.mean(x * x, -1, keepdims=True) + EPS) * g


def _swap_pairs(x):
    lane = lax.broadcasted_iota(jnp.int32, x.shape, x.ndim - 1)
    nxt = pltpu.roll(x, x.shape[-1] - 1, x.ndim - 1)
    prv = pltpu.roll(x, 1, x.ndim - 1)
    return jnp.where((lane & 1) == 0, nxt, prv)


def _split_bf16(x):
    hi = x.astype(bf16)
    return hi, (x - hi.astype(f32)).astype(bf16)


def _adaln_kernel(c_ref, w_ref, b_ref, o_ref):
    s_hi, s_lo = _split_bf16(_silu(c_ref[...]))
    w_hi, w_lo = _split_bf16(w_ref[...])
    o_ref[...] = (jnp.dot(s_hi, w_hi, preferred_element_type=f32) + jnp.dot(s_lo, w_hi, preferred_element_type=f32)
                  + jnp.dot(s_hi, w_lo, preferred_element_type=f32) + b_ref[...])


def _adaln(cond8, w, b):
    n = w.shape[1]
    tn = 1536
    return pl.pallas_call(
        _adaln_kernel,
        grid=(n // tn,),
        in_specs=[_const_spec((8, D)), pl.BlockSpec((D, tn), lambda j: (0, j)),
                  pl.BlockSpec((1, tn), lambda j: (0, j))],
        out_specs=pl.BlockSpec((8, tn), lambda j: (0, j)),
        out_shape=jax.ShapeDtypeStruct((8, n), f32),
        compiler_params=_cp(("arbitrary",)),
        name="adaln",
    )(cond8, w, b.reshape(1, n))


def _mod_rows(m8):
    m = m8[:3].reshape(3, 6, D)
    return jnp.pad(m, ((0, 0), (0, 2), (0, 0))).reshape(24, D)


N_CTX_TILES = T_CTX // ROW_TILE


def _token_specs(width):
    return [pl.BlockSpec((ROW_TILE, width), lambda i: (jnp.minimum(i, N_CTX_TILES - 1), 0)),
            pl.BlockSpec((ROW_TILE, width), lambda i: (jnp.maximum(i - N_CTX_TILES, 0), 0))]


def _token_rows(xc_ref, xl_ref):
    return jnp.where(pl.program_id(0) < N_CTX_TILES, xc_ref[...], xl_ref[...])


def _store_token_major(ref, x):
    n = x.shape[0]
    for s in range(SUBS):
        ref[pl.ds(s, n, stride=SUBS), :] = x[:, s * LANES:(s + 1) * LANES]


def _load_token_major(ref, n, row0=0):
    return jnp.concatenate([ref[pl.ds(row0 * SUBS + s, n, stride=SUBS), :] for s in range(SUBS)], axis=1)


def _token_major_spec(rows, index_map):
    return pl.BlockSpec((rows * SUBS, LANES), index_map)


def _rope_tables(n, rot_dim, lanes, lane0, copies=1):
    rows_count = n // GRID_W
    rows = np.repeat(np.arange(rows_count), GRID_W).astype(np.float64)
    cols = np.tile(np.arange(GRID_W), rows_count).astype(np.float64)
    d_axis = rot_dim // 2
    inv = ROPE_THETA ** (-np.arange(0, d_axis, 2, dtype=np.float64) / d_axis)
    ang = np.concatenate([rows[:, None] * inv, cols[:, None] * inv], -1)
    c = np.ones((n, lanes), np.float32)
    s = np.zeros((n, lanes), np.float32)
    for j in range(copies):
        lo = lane0 + j * rot_dim
        c[:, lo:lo + rot_dim] = np.repeat(np.cos(ang), 2, axis=1)
        s[:, lo:lo + rot_dim] = np.repeat(np.sin(ang), 2, axis=1) * np.tile(np.array([-1.0, 1.0]), rot_dim // 2)
    return jnp.asarray(c), jnp.asarray(s)


L0_Q0 = 2 * A_WIDTH
L0_K0 = L0_Q0 + B_HEADS * LANES
L0_V0 = L0_K0 + B_KV * B_HD
L0_IN = L0_V0 + B_KV * B_HD


def _l0_kernel(*refs, latent):
    if latent:
        (sink_ref, x_ref, gn_ref, mod_ref, win_ref, gvn_ref, ws_ref, bsb_ref, gq_ref, gk_ref, wo_ref,
         cos_ref, sin_ref, kc_ref, vc_ref, xo_ref, zs, cat, qs, ks, vt, kcb, vct) = refs
        key_off = WINDOW
    else:
        (sink_ref, x_ref, gn_ref, mod_ref, win_ref, gvn_ref, ws_ref, bsb_ref, gq_ref, gk_ref, wo_ref,
         xo_ref, ko_ref, vo_ref, zs, cat, qs, ks, vt, kf, vf) = refs
        key_off = 0
    n = BLOCK_ROWS
    n_chunks = n // CHUNK
    low = lax.broadcasted_iota(jnp.int32, (CHUNK, LANES), 1) < B_HD

    if latent:
        zpad = jnp.zeros((WINDOW, LANES), bf16)
        for c0 in (0, 1 + n_chunks):
            ks[c0 * CHUNK:(c0 + 1) * CHUNK, :] = zpad
            vt[c0] = zpad
        kcb[...] = kc_ref[...].astype(bf16)
        for i in range(PAST // CHUNK):
            vct[i] = vc_ref[i * CHUNK:(i + 1) * CHUNK, :].T.astype(bf16)

    def project(c, carry):
        r = pl.ds(pl.multiple_of(c * PROJ_ROWS, PROJ_ROWS), PROJ_ROWS)
        h = _rms_rows(x_ref[r, :], gn_ref[...]) * (1.0 + mod_ref[1:2, :]) + mod_ref[0:1, :]
        zs[r, :] = jnp.dot(h.astype(bf16), win_ref[...], preferred_element_type=f32)
        return carry

    lax.fori_loop(0, n // PROJ_ROWS, project, 0)

    def prepare(c, carry):
        r = pl.ds(pl.multiple_of(c * CHUNK, CHUNK), CHUNK)
        u = jax.nn.gelu(zs[r, 0:A_WIDTH])
        v = jax.nn.gelu(zs[r, A_WIDTH:2 * A_WIDTH])
        mu = jnp.mean(v, -1, keepdims=True)
        var = jnp.mean(jnp.square(v - mu), -1, keepdims=True)
        vn = ((v - mu) * lax.rsqrt(var + EPS) * gvn_ref[...]).astype(bf16)
        for g in range(A_GROUPS):
            cs = slice(g * CHUNK, (g + 1) * CHUNK)
            mixed = jnp.dot(ws_ref[g], vn[:, cs], preferred_element_type=f32) + bsb_ref[g]
            cat[r, cs] = (u[:, cs] * mixed).astype(bf16)
        if latent:
            cs_, sn_ = cos_ref[r, :], sin_ref[r, :]
        for h in range(B_HEADS):
            hs = slice(h * LANES, (h + 1) * LANES)
            qh = zs[r, L0_Q0 + h * LANES:L0_Q0 + (h + 1) * LANES]
            qh = qh * lax.rsqrt(jnp.sum(qh * qh, -1, keepdims=True) * (1.0 / B_HD) + EPS) * gq_ref[:, hs]
            if latent:
                qh = qh * cs_ + _swap_pairs(qh) * sn_
            qs[r, hs] = (qh * B_SCALE).astype(bf16)
        k = zs[r, L0_K0:L0_K0 + LANES]
        k2 = k * k
        s0 = jnp.sum(jnp.where(low, k2, 0.0), -1, keepdims=True)
        s1 = jnp.sum(jnp.where(low, 0.0, k2), -1, keepdims=True)
        k = k * lax.rsqrt(jnp.where(low, s0, s1) * (1.0 / B_HD) + EPS) * gk_ref[...]
        vv = zs[r, L0_V0:L0_V0 + LANES]
        if latent:
            k = k * cs_ + _swap_pairs(k) * sn_
        else:
            kf[r, :] = k
            vf[r, :] = vv
        kr = pl.ds(pl.multiple_of(c * CHUNK + key_off, CHUNK), CHUNK)
        ks[kr, :] = k.astype(bf16)
        vt[c + key_off // CHUNK] = vv.T.astype(bf16)
        return carry

    lax.fori_loop(0, n_chunks, prepare, 0)

    def attend(r, rows, key_sets):
        q = jnp.concatenate([qs[r, h * LANES:(h + 1) * LANES] for h in range(B_HEADS)], axis=0)
        sk = jnp.concatenate([jnp.full((1, rows), sink_ref[h], f32) for h in range(B_HEADS)], axis=1)
        scores = []
        m = sk
        for k, _, keep in key_sets:
            s = lax.dot_general(k, q, (((1,), (1,)), ((), ())), preferred_element_type=f32)
            if keep is not None:
                s = jnp.where(keep, s, NEG_INF)
            scores.append(s)
            m = jnp.maximum(m, jnp.max(s, 0, keepdims=True))
        den = jnp.exp(sk - m)
        ot = None
        for s, (_, vts, _) in zip(scores, key_sets):
            e = jnp.exp(s - m)
            den = den + jnp.sum(e, 0, keepdims=True)
            eb = e.astype(bf16)
            for i, v_t in enumerate(vts):
                pv = jnp.dot(v_t, eb[i * CHUNK:(i + 1) * CHUNK, :], preferred_element_type=f32)
                ot = pv if ot is None else ot + pv
        ot = ot * (1.0 / den)
        for pair in range(B_HEADS // 2):
            f0 = (2 * pair // B_GROUP) * B_HD
            pair_t = jnp.concatenate([ot[f0:f0 + B_HD, 2 * pair * rows:(2 * pair + 1) * rows],
                                      ot[f0:f0 + B_HD, (2 * pair + 1) * rows:(2 * pair + 2) * rows]], axis=0)
            cat[r, A_WIDTH + pair * LANES:A_WIDTH + (pair + 1) * LANES] = pair_t.T.astype(bf16)

    if latent:
        span = CHUNK + 2 * WINDOW

        def attend_block(c, carry):
            start = pl.multiple_of(c * CHUNK, CHUNK)
            kr = pl.ds(start, span)
            kj = lax.broadcasted_iota(jnp.int32, (span, B_HEADS * CHUNK), 0)
            qi = lax.broadcasted_iota(jnp.int32, (span, B_HEADS * CHUNK), 1) & (CHUNK - 1)
            kpos = start - WINDOW + kj
            keep = (jnp.abs(kj - WINDOW - qi) <= WINDOW) & (kpos >= 0) & (kpos < n)
            attend(pl.ds(start, CHUNK), CHUNK,
                   [(ks[kr, :], [vt[c + i] for i in range(span // CHUNK)], keep),
                    (kcb[...], [vct[i] for i in range(PAST // CHUNK)], None)])
            return carry

        lax.fori_loop(0, n_chunks, attend_block, 0)
    else:
        def attend_seq(sq, carry):
            r = pl.ds(pl.multiple_of(sq * SEQ, SEQ), SEQ)
            attend(r, SEQ, [(ks[r, :], [vt[sq * (SEQ // CHUNK) + i] for i in range(SEQ // CHUNK)], None)])
            ko_ref[sq] = kf[r, :].T
            vo_ref[sq] = vf[r, :].T
            return carry

        lax.fori_loop(0, n // SEQ, attend_seq, 0)

    def output(c, carry):
        r = pl.ds(pl.multiple_of(c * PROJ_ROWS, PROJ_ROWS), PROJ_ROWS)
        y = jnp.dot(cat[r, :], wo_ref[...], preferred_element_type=f32)
        xo_ref[r, :] = x_ref[r, :] + mod_ref[2:3, :] * y
        return carry

    lax.fori_loop(0, n // PROJ_ROWS, output, 0)


def _l0_mixer(xc, xl, g_norm, mods, p, cache_k, cache_v):
    w = p['w_in']
    q = w[:, 2 * A_WIDTH:2 * A_WIDTH + B_HEADS * B_HD].reshape(D, B_HEADS, B_HD)
    zero = jnp.zeros((D, B_GROUP, B_HD), f32)
    q_slots = jnp.concatenate([jnp.concatenate([q[:, :B_GROUP], zero], axis=2),
                               jnp.concatenate([zero, q[:, B_GROUP:]], axis=2)], axis=1).reshape(D, B_HEADS * LANES)
    win = jnp.concatenate([w[:, :2 * A_WIDTH], q_slots, w[:, 2 * A_WIDTH + B_HEADS * B_HD:]], axis=1).astype(bf16)
    zg = jnp.zeros((B_HD,), f32)
    gq = jnp.concatenate([jnp.tile(jnp.concatenate([p['g_q'], zg]), B_GROUP),
                          jnp.tile(jnp.concatenate([zg, p['g_q']]), B_GROUP)]).reshape(1, B_HEADS * LANES)
    gk = jnp.tile(p['g_k'], B_KV).reshape(1, LANES)
    gvn = p['g_vnorm'].reshape(1, A_WIDTH)
    ws = p['w_s'].astype(bf16)
    bsb = jnp.broadcast_to(p['b_s'][:, :, None], (A_GROUPS, CHUNK, CHUNK))
    wo = p['w_o'].astype(bf16)
    weights = (g_norm.reshape(1, D),)
    consts = (win, gvn, ws, bsb, gq, gk, wo)
    c_specs = [_const_spec(a.shape) for a in consts]
    smem = pl.BlockSpec(memory_space=pltpu.SMEM)
    row = pl.BlockSpec((BLOCK_ROWS, D), lambda b: (b, 0))
    kv = pl.BlockSpec((BLOCK_ROWS // SEQ, LANES, SEQ), lambda b: (b, 0, 0))

    def scratch(pad):
        return [pltpu.VMEM((BLOCK_ROWS, L0_IN), f32), pltpu.VMEM((BLOCK_ROWS, D), bf16),
                pltpu.VMEM((BLOCK_ROWS, B_HEADS * LANES), bf16), pltpu.VMEM((BLOCK_ROWS + pad, LANES), bf16),
                pltpu.VMEM(((BLOCK_ROWS + pad) // CHUNK, LANES, CHUNK), bf16)]

    kv_shape = jax.ShapeDtypeStruct((BATCH, LANES, SEQ), f32)
    state = pltpu.VMEM((BLOCK_ROWS, LANES), f32)
    xo_ctx, k_t, v_t = pl.pallas_call(
        functools.partial(_l0_kernel, latent=False),
        grid=(T_CTX // BLOCK_ROWS,),
        in_specs=[smem, row, _const_spec((1, D)), pl.BlockSpec((8, D), lambda b: (0, 0))] + c_specs,
        out_specs=[row, kv, kv],
        out_shape=[jax.ShapeDtypeStruct((T_CTX, D), f32), kv_shape, kv_shape],
        scratch_shapes=scratch(0) + [state, state],
        compiler_params=_cp(("parallel",)), name="l0_mixer_ctx",
    )(p['sink'], xc, *weights, mods, *consts)
    k_new = k_t.reshape(BATCH, B_KV, B_HD, SEQ).transpose(0, 3, 1, 2)
    v_new = v_t.reshape(BATCH, B_KV, B_HD, SEQ).transpose(0, 3, 1, 2)

    cos, sin = _rope_tables(DEC_SEQ, B_HD, LANES, 0, copies=LANES // B_HD)
    cache = pl.BlockSpec((None, PAST, LANES), lambda b: (b, 0, 0))
    past = [pltpu.VMEM((PAST, LANES), bf16), pltpu.VMEM((PAST // CHUNK, LANES, CHUNK), bf16)]
    xo_lat = pl.pallas_call(
        functools.partial(_l0_kernel, latent=True),
        grid=(DEC_BATCH,),
        in_specs=[smem, row, _const_spec((1, D)), pl.BlockSpec((8, D), lambda b: (1 + b, 0))] + c_specs + [
                  _const_spec(cos.shape), _const_spec(sin.shape), cache, cache],
        out_specs=row,
        out_shape=jax.ShapeDtypeStruct((T_LAT, D), f32),
        scratch_shapes=scratch(2 * WINDOW) + past,
        compiler_params=_cp(("parallel",)), name="l0_mixer_lat",
    )(p['sink'], xl, *weights, mods, *consts, cos, sin,
      cache_k.reshape(DEC_BATCH, PAST, LANES), cache_v.reshape(DEC_BATCH, PAST, LANES))
    return xo_ctx, xo_lat, k_new, v_new


C_SLOTS = C_HEADS * SLOT
C_PAIRS = C_HEADS // 2
L1_ROWS = 256


def _l1_kernel(*refs, latent):
    if latent:
        (x_ref, d_ref, gn_ref, mod_ref, win_ref, gqa_ref, wuq_ref, gq_ref, gkva_ref, wuk_ref, wuvt_ref, gk_ref,
         wo_ref, wuqs_ref, qcos_ref, qsin_ref, kcos_ref, ksin_ref, cckv_ref, ckpe_ref, xo_ref,
         zs, cat, qs, ks, vt, wide, wide2) = refs
        n_ctx = PAST
    else:
        (x_ref, d_ref, gn_ref, mod_ref, win_ref, gqa_ref, wuq_ref, gq_ref, gkva_ref, wuk_ref, wuvt_ref, gk_ref,
         wo_ref, xo_ref, ckvo_ref, kpeo_ref, zs, cat, qs, ks, vt, wide) = refs
        n_ctx = 0
    n = BLOCK_ROWS
    nt_dims = (((1,), (1,)), ((), ()))

    def inv_rms(v):
        return lax.rsqrt(jnp.sum(v * v, -1, keepdims=True) * (1.0 / C_QK) + EPS)

    def expand_keys(ckv_n, kslot, kb, rope_rows):
        cb = ckv_n.astype(bf16)
        key_rows = pl.ds(pl.multiple_of(kb * L1_ROWS, L1_ROWS), L1_ROWS)
        wide[...] = jnp.dot(cb, wuk_ref[...], preferred_element_type=f32)
        if rope_rows is not None:
            kcos = kcos_ref[rope_rows, :]
            turned = _swap_pairs(kslot) * ksin_ref[rope_rows, :]
        for h in range(C_HEADS):
            kh = wide[:, h * SLOT:(h + 1) * SLOT] + kslot
            if rope_rows is not None:
                kh = inv_rms(kh) * (kh * kcos + turned)
            else:
                kh = kh * inv_rms(kh) * gk_ref[...]
            ks[h, key_rows, :] = kh.astype(bf16)
        v_t = lax.dot_general(wuvt_ref[...], cb, nt_dims, preferred_element_type=f32).astype(bf16)
        for pair in range(C_PAIRS):
            vt[pair, kb] = v_t[pair * LANES:(pair + 1) * LANES, :]

    if latent:
        def past_keys(c, carry):
            r = pl.ds(pl.multiple_of(c * L1_ROWS, L1_ROWS), L1_ROWS)
            expand_keys(cckv_ref[r, :], ckpe_ref[r, :], c, None)
            return carry

        lax.fori_loop(0, PAST // L1_ROWS, past_keys, 0)

    for c in range(n // PROJ_ROWS):
        r = pl.ds(c * PROJ_ROWS, PROJ_ROWS)
        x1 = x_ref[r, :] + mod_ref[5:6, :] * _load_token_major(d_ref, PROJ_ROWS, c * PROJ_ROWS)
        xo_ref[r, :] = x1
        h = _rms_rows(x1, gn_ref[...]) * (1.0 + mod_ref[1:2, :]) + mod_ref[0:1, :]
        zs[r, :] = jnp.dot(h.astype(bf16), win_ref[...], preferred_element_type=f32)

    def prepare(c, carry):
        r = pl.ds(pl.multiple_of(c * L1_ROWS, L1_ROWS), L1_ROWS)
        qa = _rms_rows(zs[r, 0:C_Q_LORA], gqa_ref[...]).astype(bf16)
        wide[...] = jnp.dot(qa, wuq_ref[...], preferred_element_type=f32)
        if latent:
            wide2[...] = jnp.dot(qa, wuqs_ref[...], preferred_element_type=f32)
            qcos, qsin = qcos_ref[r, :], qsin_ref[r, :]
        for h in range(C_HEADS):
            hs = slice(h * SLOT, (h + 1) * SLOT)
            qh = wide[:, hs]
            if latent:
                qh = inv_rms(qh) * (qh * qcos + wide2[:, hs] * qsin)
            else:
                qh = qh * inv_rms(qh) * gq_ref[...]
            qs[h, r, :] = (qh * C_SCALE).astype(bf16)
        ckv_n = _rms_rows(zs[r, C_Q_LORA:C_Q_LORA + C_KV_LORA], gkva_ref[...])
        kslot = zs[r, C_Q_LORA + C_KV_LORA:ODD_IN_PAD]
        if not latent:
            ckvo_ref[r, :] = ckv_n
            kpeo_ref[r, :] = kslot
        expand_keys(ckv_n, kslot, c + n_ctx // L1_ROWS, r if latent else None)
        return carry

    lax.fori_loop(0, n // L1_ROWS, prepare, 0)

    low = lax.broadcasted_iota(jnp.int32, (2 * C_V, L1_ROWS), 0) < C_V
    n_kblocks = (n_ctx + n) // L1_ROWS
    pairs_per_step = 2 if latent else 4

    def attend(c, carry):
        r = pl.ds(pl.multiple_of(c * L1_ROWS, L1_ROWS), L1_ROWS)

        def values_t(pair, eb):
            if not latent:
                return jnp.dot(vt[pair, c], eb, preferred_element_type=f32)
            o_t = None
            for b in range(n_kblocks):
                pv = jnp.dot(vt[pair, b], eb[b * L1_ROWS:(b + 1) * L1_ROWS, :], preferred_element_type=f32)
                o_t = pv if o_t is None else o_t + pv
            return o_t

        def pairs_step(i, carry2):
            pairs = [i * pairs_per_step + j for j in range(pairs_per_step)]
            heads = [2 * p + hh for p in pairs for hh in range(2)]
            scores = [lax.dot_general(ks[h] if latent else ks[h, r, :], qs[h, r, :], nt_dims,
                                      preferred_element_type=f32) for h in heads]
            exps = [jnp.exp(s - jnp.max(s, 0, keepdims=True)) for s in scores]
            dens = [jnp.sum(e, 0, keepdims=True) for e in exps]
            outs = [values_t(h // 2, e.astype(bf16)) / den for h, e, den in zip(heads, exps, dens)]
            for j, pair in enumerate(pairs):
                cat[pair, r, :] = jnp.where(low, outs[2 * j], outs[2 * j + 1]).T.astype(bf16)
            return carry2

        return lax.fori_loop(0, C_PAIRS // pairs_per_step, pairs_step, carry)

    lax.fori_loop(0, n // L1_ROWS, attend, 0)

    def output(c, carry):
        r = pl.ds(pl.multiple_of(c * PROJ_ROWS, PROJ_ROWS), PROJ_ROWS)
        heads = jnp.concatenate([cat[pair, r, :] for pair in range(C_PAIRS)], axis=1)
        y = jnp.dot(heads, wo_ref[...], preferred_element_type=f32)
        xo_ref[r, :] = xo_ref[r, :] + mod_ref[2:3, :] * y
        return carry

    lax.fori_loop(0, n // PROJ_ROWS, output, 0)


def _slot_cols(w, heads, width, lo, hi, lane0):
    k = w.shape[0]
    w3 = w.reshape(k, heads, width)[:, :, lo:hi]
    out = jnp.zeros((k, heads, SLOT), w.dtype).at[:, :, lane0:lane0 + (hi - lo)].set(w3)
    return out.reshape(k, heads * SLOT)


def _l1_mixer(xc, xl, moe_prev, g_norm, mods, p, cache_ckv, cache_kpe):
    w_in = jnp.zeros((D, ODD_IN_PAD), f32).at[:, :C_Q_LORA + C_KV_LORA].set(
        p['w_in'][:, :C_Q_LORA + C_KV_LORA]).at[
        :, C_Q_LORA + C_KV_LORA + C_NOPE:C_Q_LORA + C_KV_LORA + C_QK].set(p['w_in'][:, C_Q_LORA + C_KV_LORA:])
    wuq = _slot_cols(p['w_uq'], C_HEADS, C_QK, 0, C_QK, 0).astype(bf16)
    wuk = _slot_cols(p['w_ukv'], C_HEADS, C_NOPE + C_V, 0, C_NOPE, 0).astype(bf16)
    wuv_t = p['w_ukv'].reshape(C_KV_LORA, C_HEADS, C_NOPE + C_V)[:, :, C_NOPE:].reshape(
        C_KV_LORA, C_HEADS * C_V).T.astype(bf16)
    gq = jnp.zeros((1, SLOT), f32).at[0, :C_QK].set(p['g_q'])
    gk = jnp.zeros((1, SLOT), f32).at[0, :C_QK].set(p['g_k'])
    consts = (g_norm.reshape(1, D), w_in.astype(bf16), p['g_qa'].reshape(1, C_Q_LORA), wuq, gq,
              p['g_kva'].reshape(1, C_KV_LORA), wuk, wuv_t, gk, p['w_o'].astype(bf16))
    c_specs = [_const_spec(a.shape) for a in consts]
    row = pl.BlockSpec((BLOCK_ROWS, D), lambda b: (b, 0))
    n_ctx_blocks = T_CTX // BLOCK_ROWS

    def scratch(n_keys):
        return [pltpu.VMEM((BLOCK_ROWS, ODD_IN_PAD), f32), pltpu.VMEM((C_PAIRS, BLOCK_ROWS, LANES), bf16),
                pltpu.VMEM((C_HEADS, BLOCK_ROWS, SLOT), bf16), pltpu.VMEM((C_HEADS, n_keys, SLOT), bf16),
                pltpu.VMEM((C_PAIRS, n_keys // L1_ROWS, LANES, L1_ROWS), bf16),
                pltpu.VMEM((L1_ROWS, C_SLOTS), f32)]

    xo_ctx, ckv_new, kpe_slot = pl.pallas_call(
        functools.partial(_l1_kernel, latent=False),
        grid=(n_ctx_blocks,),
        in_specs=[row, _token_major_spec(BLOCK_ROWS, lambda b: (b, 0)), c_specs[0],
                  pl.BlockSpec((8, D), lambda b: (0, 0))] + c_specs[1:],
        out_specs=[row, pl.BlockSpec((BLOCK_ROWS, C_KV_LORA), lambda b: (b, 0)),
                   pl.BlockSpec((BLOCK_ROWS, SLOT), lambda b: (b, 0))],
        out_shape=[jax.ShapeDtypeStruct((T_CTX, D), f32), jax.ShapeDtypeStruct((T_CTX, C_KV_LORA), f32),
                   jax.ShapeDtypeStruct((T_CTX, SLOT), f32)],
        scratch_shapes=scratch(BLOCK_ROWS),
        compiler_params=_cp(("parallel",)), name="l1_mixer_ctx",
    )(xc, moe_prev, consts[0], mods, *consts[1:])

    cos, sin = _rope_tables(DEC_SEQ, C_ROPE, SLOT, C_NOPE)
    pair_swap = lambda a: a.reshape(a.shape[:-1] + (a.shape[-1] // 2, 2))[..., ::-1].reshape(a.shape)
    rope = (pair_swap(wuq), gq * cos, pair_swap(gq) * sin, gk * cos, pair_swap(gk) * sin)
    ckpe = jnp.zeros((DEC_BATCH, PAST, SLOT), f32).at[:, :, C_NOPE:C_QK].set(cache_kpe)
    xo_lat = pl.pallas_call(
        functools.partial(_l1_kernel, latent=True),
        grid=(DEC_BATCH,),
        in_specs=[pl.BlockSpec((BLOCK_ROWS, D), lambda b: (b, 0), pipeline_mode=pl.Buffered(1)),
                  pl.BlockSpec((BLOCK_ROWS * SUBS, LANES), lambda b: (n_ctx_blocks + b, 0),
                               pipeline_mode=pl.Buffered(1)), c_specs[0],
                  pl.BlockSpec((8, D), lambda b: (1 + b, 0))] + c_specs[1:] + [_const_spec(a.shape) for a in rope] + [
                  pl.BlockSpec((None, PAST, C_KV_LORA), lambda b: (b, 0, 0)),
                  pl.BlockSpec((None, PAST, SLOT), lambda b: (b, 0, 0))],
        out_specs=row,
        out_shape=jax.ShapeDtypeStruct((T_LAT, D), f32),
        scratch_shapes=scratch(PAST + BLOCK_ROWS) + [pltpu.VMEM((L1_ROWS, C_SLOTS), f32)],
        compiler_params=_cp(("parallel",)), name="l1_mixer_lat",
    )(xl, moe_prev, consts[0], mods, *consts[1:], *rope, cache_ckv, ckpe)
    return xo_ctx, xo_lat, ckv_new, kpe_slot[:, C_NOPE:C_QK]


ROUTER_ROWS = 40


def _router_kernel(xc_ref, xl_ref, gn_ref, mod_ref, whi_ref, wlo_ref, br_ref, h_ref, route_ref):
    h = _rms_rows(_token_rows(xc_ref, xl_ref), gn_ref[...]) * (1.0 + mod_ref[4:5, :]) + mod_ref[3:4, :]
    _store_token_major(h_ref, h)
    h_hi, h_lo = _split_bf16(h)
    nt = (((1,), (1,)), ((), ()))
    logits = (lax.dot_general(whi_ref[...], h_hi, nt, preferred_element_type=f32)
              + lax.dot_general(whi_ref[...], h_lo, nt, preferred_element_type=f32)
              + lax.dot_general(wlo_ref[...], h_hi, nt, preferred_element_type=f32))
    logits = logits[0:ROUTER_ROWS, :] + br_ref[0:ROUTER_ROWS, :]
    row_i = lax.broadcasted_iota(jnp.int32, logits.shape, 0)
    row = row_i.astype(f32)
    big = 1e6
    is_g = (row_i >= N_EXPERTS) & (row_i < N_EXPERTS + N_GROUPS)
    lg = jnp.where(is_g, logits, -jnp.inf)
    mg = jnp.max(lg, 0, keepdims=True)
    gsel = jnp.min(jnp.where(lg == mg, row, big), 0, keepdims=True) - N_EXPERTS
    pg_sel = 1.0 / jnp.sum(jnp.where(is_g, jnp.exp(lg - mg), 0.0), 0, keepdims=True)
    in_grp = (row_i < N_EXPERTS) & ((row_i >> 3).astype(f32) == gsel)
    le = jnp.where(in_grp, logits, -jnp.inf)
    m1 = jnp.max(le, 0, keepdims=True)
    i1 = jnp.min(jnp.where(le == m1, row, big), 0, keepdims=True)
    le2 = jnp.where(row == i1, -jnp.inf, le)
    m2 = jnp.max(le2, 0, keepdims=True)
    i2 = jnp.min(jnp.where(le2 == m2, row, big), 0, keepdims=True)
    e2 = jnp.exp(m2 - m1)
    w1 = pg_sel / (1.0 + e2)
    w2 = pg_sel * e2 / (1.0 + e2)
    sub = lax.broadcasted_iota(jnp.int32, route_ref.shape, 0)
    route_ref[...] = jnp.where(sub == 0, i1, jnp.where(sub == 1, i2, jnp.where(sub == 2, w1,
                                                                                jnp.where(sub == 3, w2, 0.0))))


def _router(xc, xl, g_norm, mods, p):
    wr = jnp.zeros((128, D), f32).at[:N_EXPERTS].set(p['w_re'].T).at[
        N_EXPERTS:N_EXPERTS + N_GROUPS].set(p['w_rg'].T)
    w_hi, w_lo = _split_bf16(wr)
    br =jnp.zeros((128, 1), f32).at[:N_EXPERTS, 0].set(p['b_re']).at[
        N_EXPERTS:N_EXPERTS + N_GROUPS, 0].set(p['b_rg'])
    return pl.pallas_call(
        _router_kernel,
        grid=(T // ROW_TILE,),
        in_specs=_token_specs(D) + [
                  _const_spec((1, D)),
                  pl.BlockSpec((8, D), lambda i: (_sample_of_tile(i, ROW_TILE), 0)),
                  _const_spec((128, D)), _const_spec((128, D)), _const_spec((128, 1))],
        out_specs=[_token_major_spec(ROW_TILE, lambda i: (i, 0)), pl.BlockSpec((8, ROW_TILE), lambda i: (0, i))],
        out_shape=[jax.ShapeDtypeStruct((T * SUBS, LANES), f32), jax.ShapeDtypeStruct((8, T), f32)],
        compiler_params=_cp(("parallel",)), name="router",
    )(xc, xl, g_norm.reshape(1, D), mods, w_hi, w_lo, br)


TE_LANES = 256
PLAN_TILE_EXPERT, PLAN_N_USED, PLAN_PAD_LO, PLAN_PAD_HI = 0, 1, 2, 3


def _plan_kernel(rt_ref, pos_ref, te_ref, rank):
    n_blk = T // 128
    e_col = lax.broadcasted_iota(jnp.int32, (N_EXPERTS, 128), 0).astype(f32)
    ri = lax.broadcasted_iota(jnp.int32, (128, 128), 0)
    ci = lax.broadcasted_iota(jnp.int32, (128, 128), 1)
    before = jnp.where(ri < ci, 1.0, 0.0).astype(bf16)

    def picks(b):
        cs = slice(b * 128, (b + 1) * 128)
        return rt_ref[0:1, cs] == e_col, rt_ref[1:2, cs] == e_col

    counts = jnp.zeros((N_EXPERTS, 1), f32)
    for b in range(n_blk):
        m0, m1 = picks(b)
        m = jnp.where(m0, 1.0, 0.0) + jnp.where(m1, 1.0, 0.0)
        rank[:, b * 128:(b + 1) * 128] = jnp.dot(m.astype(bf16), before, preferred_element_type=f32) + counts
        counts = counts + jnp.sum(m, axis=1, keepdims=True)

    tiles = jnp.floor((counts + (MOE_TILE - 1.0)) * (1.0 / MOE_TILE))
    er = lax.broadcasted_iota(jnp.int32, (N_EXPERTS, N_EXPERTS), 0)
    ec = lax.broadcasted_iota(jnp.int32, (N_EXPERTS, N_EXPERTS), 1)
    earlier = jnp.where(ec < er, 1.0, 0.0).astype(bf16)
    tile_start = jnp.dot(earlier, jnp.broadcast_to(tiles, (N_EXPERTS, 128)).astype(bf16),
                         preferred_element_type=f32)
    row_start = tile_start * MOE_TILE

    sub = lax.broadcasted_iota(jnp.int32, (8, 128), 0)
    for b in range(n_blk):
        m0, m1 = picks(b)
        base = rank[:, b * 128:(b + 1) * 128] + row_start
        p0 = jnp.sum(jnp.where(m0, base, 0.0), axis=0, keepdims=True)
        p1 = jnp.sum(jnp.where(m1, base, 0.0), axis=0, keepdims=True)
        pos_ref[:, b * 128:(b + 1) * 128] = jnp.where(sub == 0, p0, jnp.where(sub == 1, p1, 0.0)).astype(jnp.int32)

    tile_end = tile_start + tiles
    n_used = jnp.max(tile_end, axis=0, keepdims=True)
    diag = (lax.broadcasted_iota(jnp.int32, (N_EXPERTS, 128), 0)
            == lax.broadcasted_iota(jnp.int32, (N_EXPERTS, 128), 1))
    pad_lo = jnp.sum(jnp.where(diag, row_start + counts, 0.0), axis=0, keepdims=True)
    pad_hi = jnp.sum(jnp.where(diag, tile_end * MOE_TILE, 0.0), axis=0, keepdims=True)
    for j in range(TE_LANES // 128):
        t = (lax.broadcasted_iota(jnp.int32, (N_EXPERTS, 128), 1) + j * 128).astype(f32)
        te = jnp.minimum(jnp.sum(jnp.where(tile_end <= t, 1.0, 0.0), axis=0, keepdims=True), N_EXPERTS - 1.0)
        rows = jnp.where(sub == PLAN_TILE_EXPERT, te, jnp.where(sub == PLAN_N_USED, n_used, 0.0))
        if j == 0:
            rows = jnp.where(sub == PLAN_PAD_LO, pad_lo, jnp.where(sub == PLAN_PAD_HI, pad_hi, rows))
        te_ref[:, j * 128:(j + 1) * 128] = rows.astype(jnp.int32)


def _slot_code(t, k):
    return t * SUBS + k * (SUBS // 2)


def _code_offset(code):
    return pl.multiple_of(code & ~(SUBS - 1), SUBS)


def _code_gate_index(code):
    return code >> 2


PAD_CODE = T * SUBS


def _invert_kernel(pos_ref, plan_ref, code_ref):
    def pad_tile(tile, carry):
        for u in range(MOE_TILE):
            code_ref[tile * MOE_TILE + u] = PAD_CODE
        return carry

    def pad_last_tile(e, carry):
        return pad_tile(jnp.maximum(plan_ref[PLAN_PAD_HI, e] // MOE_TILE - 1, 0), carry)
    lax.fori_loop(0, N_EXPERTS, pad_last_tile, 0)
    lax.fori_loop(plan_ref[PLAN_N_USED, 0], MOE_TILES, pad_tile, 0)

    group = 32
    for k in range(2):
        def place(i, carry):
            t0 = i * group
            slots = [pos_ref[k * T + t0 + u] for u in range(group)]
            for u, s in enumerate(slots):
                code_ref[s] = _slot_code(t0 + u, k)
            return carry
        lax.fori_loop(0, T // group, place, 0)


def _route_plan(route_t):
    pos, plan = pl.pallas_call(
        _plan_kernel,
        out_shape=[jax.ShapeDtypeStruct((8, T), jnp.int32), jax.ShapeDtypeStruct((8, TE_LANES), jnp.int32)],
        scratch_shapes=[pltpu.VMEM((N_EXPERTS, T), f32)],
        compiler_params=_cp(None), name="route_plan",
    )(route_t)
    smem = pl.BlockSpec(memory_space=pltpu.SMEM)
    codes = pl.pallas_call(
        _invert_kernel,
        in_specs=[smem, smem], out_specs=smem,
        out_shape=jax.ShapeDtypeStruct((MOE_ROWS,), jnp.int32),
        name="route_invert",
    )(pos[0:2].reshape(2 * T), plan)
    gates = jnp.pad(route_t[2:4].T.reshape(2 * T), (0, 8))
    return plan[PLAN_TILE_EXPERT, :MOE_TILES], plan[PLAN_N_USED, :1], codes, gates


def _tile_index(i):
    return jnp.minimum(i, MOE_TILES - 1)


def _first_tile_of_expert(i, te_ref):
    return (i == 0) | (te_ref[_tile_index(i)] != te_ref[_tile_index(jnp.maximum(i - 1, 0))])


TM_ROWS = T * SUBS
SCATTER_GROUP = 16


def _gather_tile(code_ref, tile, xs, gbuf):
    base = tile * MOE_TILE
    for r in range(MOE_TILE):
        gbuf[r * SUBS:(r + 1) * SUBS, :] = xs[pl.ds(_code_offset(code_ref[base + r]), SUBS), :]


def _moe_up_kernel(te_ref, nu_ref, code_ref, h_hbm, w1_ref, w3_ref, o_ref, xs, gbuf_a, gbuf_b, w13, sem):
    i = pl.program_id(0)
    used = i < nu_ref[0]

    @pl.when(i == 0)
    def _():
        cp = pltpu.make_async_copy(h_hbm, xs.at[pl.ds(0, TM_ROWS), :], sem)
        cp.start()
        xs[TM_ROWS:TM_ROWS + SUBS, :] = jnp.zeros((SUBS, LANES), f32)
        cp.wait()
        _gather_tile(code_ref, 0, xs, gbuf_a)

    @pl.when(used & _first_tile_of_expert(i, te_ref))
    def _():
        w13[:, :D_EXPERT] = w1_ref[0].astype(bf16)
        w13[:, D_EXPERT:] = w3_ref[0].astype(bf16)

    def step(cur, nxt):
        _gather_tile(code_ref, _tile_index(i + 1), xs, nxt)
        x3 = jnp.swapaxes(cur[...].reshape(MOE_TILE, SUBS, LANES), 0, 1)
        x = jnp.concatenate([x3[s] for s in range(SUBS)], axis=1).astype(bf16)
        h13 = jnp.dot(x, w13[...], preferred_element_type=f32)
        o_ref[...] = (_silu(h13[:, :D_EXPERT]) * h13[:, D_EXPERT:]).astype(bf16)

    pl.when(used & (i % 2 == 0))(functools.partial(step, gbuf_a, gbuf_b))
    pl.when(used & (i % 2 == 1))(functools.partial(step, gbuf_b, gbuf_a))

    @pl.when(jnp.logical_not(used))
    def _():
        o_ref[...] = jnp.zeros_like(o_ref)


def _scatter_tile(code_ref, gate_ref, tile, ybuf, acc):
    base = tile * MOE_TILE
    for g0 in range(0, MOE_TILE, SCATTER_GROUP):
        rows = range(g0, g0 + SCATTER_GROUP)
        codes = [code_ref[base + r] for r in rows]
        new = [acc[pl.ds(_code_offset(c), SUBS), :]
               + gate_ref[_code_gate_index(c)] * ybuf[r * SUBS:(r + 1) * SUBS, :]
               for r, c in zip(rows, codes)]
        for c, v in zip(codes, new):
            acc[pl.ds(_code_offset(c), SUBS), :] = v


def _moe_down_kernel(te_ref, nu_ref, code_ref, gate_ref, hh_ref, w2_ref, o_hbm, acc, ybuf_a, ybuf_b, w2b, sem):
    i = pl.program_id(0)
    n_used = nu_ref[0]

    @pl.when(i == 0)
    def _():
        def zero(c, carry):
            acc[pl.ds(pl.multiple_of(c * 1024, 1024), 1024), :] = jnp.zeros((1024, LANES), f32)
            return carry
        lax.fori_loop(0, TM_ROWS // 1024, zero, 0)
        acc[TM_ROWS:TM_ROWS + SUBS, :] = jnp.zeros((SUBS, LANES), f32)
        ybuf_b[...] = jnp.zeros_like(ybuf_b)

    @pl.when((i < n_used) & _first_tile_of_expert(i, te_ref))
    def _():
        w2b[...] = w2_ref[0].astype(bf16)

    def step(cur, prev):
        _store_token_major(cur, jnp.dot(hh_ref[...], w2b[...], preferred_element_type=f32))
        _scatter_tile(code_ref, gate_ref, jnp.maximum(i - 1, 0), prev, acc)

    pl.when((i <= n_used) & (i % 2 == 0))(functools.partial(step, ybuf_a, ybuf_b))
    pl.when((i <= n_used) & (i % 2 == 1))(functools.partial(step, ybuf_b, ybuf_a))

    @pl.when(i == MOE_TILES)
    def _():
        cp = pltpu.make_async_copy(acc.at[pl.ds(0, TM_ROWS), :], o_hbm, sem)
        cp.start()
        cp.wait()


def _moe(h_tm, route_t, p):
    tile_expert, n_used, codes, gates = _route_plan(route_t)
    tile_rows = pltpu.VMEM((MOE_TILE * SUBS, LANES), f32)
    hh = pl.pallas_call(
        _moe_up_kernel,
        grid_spec=pltpu.PrefetchScalarGridSpec(
            num_scalar_prefetch=3, grid=(MOE_TILES,),
            in_specs=[pl.BlockSpec(memory_space=pl.ANY),
                      pl.BlockSpec((1, D, D_EXPERT), lambda i, te, nu, c: (te[i], 0, 0)),
                      pl.BlockSpec((1, D, D_EXPERT), lambda i, te, nu, c: (te[i], 0, 0))],
            out_specs=pl.BlockSpec((MOE_TILE, D_EXPERT), lambda i, te, nu, c: (i, 0)),
            scratch_shapes=[pltpu.VMEM((TM_ROWS + SUBS, LANES), f32), tile_rows, tile_rows,
                            pltpu.VMEM((D, 2 * D_EXPERT), bf16), pltpu.SemaphoreType.DMA(())]),
        out_shape=jax.ShapeDtypeStruct((MOE_ROWS, D_EXPERT), bf16),
        compiler_params=_cp(("arbitrary",)), name="moe_up",
    )(tile_expert, n_used, codes, h_tm, p['w1'], p['w3'])
    return pl.pallas_call(
        _moe_down_kernel,
        grid_spec=pltpu.PrefetchScalarGridSpec(
            num_scalar_prefetch=4, grid=(MOE_TILES + 1,),
            in_specs=[pl.BlockSpec((MOE_TILE, D_EXPERT), lambda i, te, nu, c, g: (_tile_index(i), 0)),
                      pl.BlockSpec((1, D_EXPERT, D), lambda i, te, nu, c, g: (te[_tile_index(i)], 0, 0))],
            out_specs=pl.BlockSpec(memory_space=pl.ANY),
            scratch_shapes=[pltpu.VMEM((TM_ROWS + SUBS, LANES), f32), tile_rows, tile_rows,
                            pltpu.VMEM((D_EXPERT, D), bf16), pltpu.SemaphoreType.DMA(())]),
        out_shape=jax.ShapeDtypeStruct((TM_ROWS, LANES), f32),
        compiler_params=_cp(("arbitrary",)), name="moe_down",
    )(tile_expert, n_used, codes, gates, hh, p['w2'])


def _final_kernel(x_ref, d_ref, mod_ref, o_ref):
    o_ref[...] = x_ref[...] + mod_ref[5:6, :] * _load_token_major(d_ref, ROW_TILE)


def _final(x, delta, mods, row0, seq):
    n_rows = x.shape[0]
    t0 = row0 // ROW_TILE
    return pl.pallas_call(
        _final_kernel,
        grid=(n_rows // ROW_TILE,),
        in_specs=[pl.BlockSpec((ROW_TILE, D), lambda i: (i, 0)),
                  _token_major_spec(ROW_TILE, lambda i: (t0 + i, 0)),
                  pl.BlockSpec((8, D), lambda i: (_sample_of_tile(t0 + i, ROW_TILE), 0))],
        out_specs=pl.BlockSpec((ROW_TILE, D), lambda i: (i, 0)),
        out_shape=jax.ShapeDtypeStruct((n_rows, D), f32),
        compiler_params=_cp(("parallel",)), name="final",
    )(x, delta, mods).reshape(n_rows // seq, seq, D)


def kernel(x_prompt, x_sample, cache_l0_k, cache_l0_v, cache_l1_ckv, cache_l1_kpe, c, c_ctx, l0_g_norm1, l0_g_norm2, l0_w_ada, l0_b_ada, l0_w_in, l0_g_vnorm, l0_w_s, l0_b_s, l0_g_q, l0_g_k, l0_sink, l0_w_o, l0_w_rg, l0_b_rg, l0_w_re, l0_b_re, l0_w1, l0_w3, l0_w2, l1_g_norm1, l1_g_norm2, l1_w_ada, l1_b_ada, l1_w_in, l1_g_qa, l1_w_uq, l1_g_kva, l1_w_ukv, l1_g_q, l1_g_k, l1_w_o, l1_w_rg, l1_b_rg, l1_w_re, l1_b_re, l1_w1, l1_w3, l1_w2):
    p0 = dict(w_in=l0_w_in, g_vnorm=l0_g_vnorm, w_s=l0_w_s, b_s=l0_b_s, g_q=l0_g_q, g_k=l0_g_k, sink=l0_sink,
              w_o=l0_w_o, w_rg=l0_w_rg, b_rg=l0_b_rg, w_re=l0_w_re, b_re=l0_b_re, w1=l0_w1, w3=l0_w3, w2=l0_w2)
    p1 = dict(w_in=l1_w_in, g_qa=l1_g_qa, w_uq=l1_w_uq, g_kva=l1_g_kva, w_ukv=l1_w_ukv, g_q=l1_g_q, g_k=l1_g_k,
              w_o=l1_w_o, w_rg=l1_w_rg, b_rg=l1_b_rg, w_re=l1_w_re, b_re=l1_b_re, w1=l1_w1, w3=l1_w3, w2=l1_w2)

    cond8 = jnp.zeros((8, D), f32).at[0].set(c_ctx).at[1:1 + DEC_BATCH].set(c)
    mods0 = _mod_rows(_adaln(cond8, l0_w_ada, l0_b_ada))
    mods1 = _mod_rows(_adaln(cond8, l1_w_ada, l1_b_ada))

    xc0 = x_prompt.reshape(T_CTX, D)
    xl0 = x_sample.reshape(T_LAT, D)

    xc0m, xl0m, k_new, v_new = _l0_mixer(xc0, xl0, l0_g_norm1, mods0, p0, cache_l0_k, cache_l0_v)
    h0, route0 = _router(xc0m, xl0m, l0_g_norm2, mods0, p0)
    moe0 = _moe(h0, route0, p0)

    mods01 = mods1.reshape(3, 8, D).at[:, 5].set(mods0.reshape(3, 8, D)[:, 5]).reshape(24, D)
    xc1m, xl1m, ckv_new, kpe_new = _l1_mixer(xc0m, xl0m, moe0, l1_g_norm1, mods01, p1, cache_l1_ckv, cache_l1_kpe)
    h1, route1 = _router(xc1m, xl1m, l1_g_norm2, mods1, p1)
    moe1 = _moe(h1, route1, p1)

    y_prompt = _final(xc1m, moe1, mods1, 0, SEQ)
    y_sample = _final(xl1m, moe1, mods1, T_CTX, DEC_SEQ)
    return (y_prompt, y_sample, k_new, v_new,
            ckv_new.reshape(BATCH, SEQ, C_KV_LORA), kpe_new.reshape(BATCH, SEQ, C_ROPE))
```

```python
import functools

import jax
import jax.numpy as jnp
import numpy as np
from jax import lax
from jax.experimental import pallas as pl
from jax.experimental.pallas import tpu as pltpu

f32 = jnp.float32
bf16 = jnp.bfloat16

D = 1024
BATCH, SEQ = 32, 256
DEC_BATCH, DEC_SEQ = 2, 1024
PAST = 512
T_CTX = BATCH * SEQ
T_LAT = DEC_BATCH * DEC_SEQ
T = T_CTX + T_LAT
GRID_W = 64
CHUNK = 128
WINDOW = 128
ROPE_THETA = 10000.0
EPS = 1e-6
NEG_INF = -1e30
LANES = 128
SUBS = D // LANES

A_WIDTH = 512
A_GROUPS = 4
B_HEADS, B_KV, B_GROUP, B_HD = 8, 2, 4, 64
B_SCALE = B_HD ** -0.5

C_HEADS, C_Q_LORA, C_KV_LORA, C_NOPE, C_ROPE, C_V = 16, 384, 256, 64, 32, 64
C_QK = C_NOPE + C_ROPE
C_SCALE = C_QK ** -0.5
ODD_IN_PAD = 768
SLOT = 128

N_GROUPS, EPG, N_EXPERTS, D_EXPERT = 4, 8, 32, 256

ROW_TILE = 512
BLOCK_ROWS = 1024
PROJ_ROWS = 512
MOE_TILE = 256
MOE_ROWS = 2 * T + N_EXPERTS * MOE_TILE
MOE_TILES = MOE_ROWS // MOE_TILE
VMEM_CAP = 56 * 1024 * 1024


def _cp(sem, vmem=VMEM_CAP):
    return pltpu.CompilerParams(dimension_semantics=sem, vmem_limit_bytes=vmem)


def _const_spec(shape):
    nd = len(shape)
    return pl.BlockSpec(shape, lambda *_: (0,) * nd, pipeline_mode=pl.Buffered(1))


def _sample_of_tile(i, tile):
    n_ctx = T_CTX // tile
    per_lat = DEC_SEQ // tile
    return jnp.where(i < n_ctx, 0, 1 + (i - n_ctx) // per_lat)


def _silu(x):
    return x * jax.nn.sigmoid(x)


def _rms_rows(x, g):
    return x * lax.rsqrt(jnp.mean(x * x, -1, keepdims=True) + EPS) * g


def _swap_pairs(x):
    lane = lax.broadcasted_iota(jnp.int32, x.shape, x.ndim - 1)
    nxt = pltpu.roll(x, x.shape[-1] - 1, x.ndim - 1)
    prv = pltpu.roll(x, 1, x.ndim - 1)
    return jnp.where((lane & 1) == 0, nxt, prv)


def _split_bf16(x):
    hi = x.astype(bf16)
    return hi, (x - hi.astype(f32)).astype(bf16)


def _adaln_kernel(c_ref, w_ref, b_ref, o_ref):
    s_hi, s_lo = _split_bf16(_silu(c_ref[...]))
    w_hi, w_lo = _split_bf16(w_ref[...])
    o_ref[...] = (jnp.dot(s_hi, w_hi, preferred_element_type=f32) + jnp.dot(s_lo, w_hi, preferred_element_type=f32)
                  + jnp.dot(s_hi, w_lo, preferred_element_type=f32) + b_ref[...])


def _adaln(cond8, w, b):
    n = w.shape[1]
    tn = 1536
    return pl.pallas_call(
        _adaln_kernel,
        grid=(n // tn,),
        in_specs=[_const_spec((8, D)), pl.BlockSpec((D, tn), lambda j: (0, j)),
                  pl.BlockSpec((1, tn), lambda j: (0, j))],
        out_specs=pl.BlockSpec((8, tn), lambda j: (0, j)),
        out_shape=jax.ShapeDtypeStruct((8, n), f32),
        compiler_params=_cp(("arbitrary",)),
        name="adaln",
    )(cond8, w, b.reshape(1, n))


def _mod_rows(m8):
    m = m8[:3].reshape(3, 6, D)
    return jnp.pad(m, ((0, 0), (0, 2), (0, 0))).reshape(24, D)


N_CTX_TILES = T_CTX // ROW_TILE


def _token_specs(width):
    return [pl.BlockSpec((ROW_TILE, width), lambda i: (jnp.minimum(i, N_CTX_TILES - 1), 0)),
            pl.BlockSpec((ROW_TILE, width), lambda i: (jnp.maximum(i - N_CTX_TILES, 0), 0))]


def _token_rows(xc_ref, xl_ref):
    return jnp.where(pl.program_id(0) < N_CTX_TILES, xc_ref[...], xl_ref[...])


def _store_token_major(ref, x):
    n = x.shape[0]
    for s in range(SUBS):
        ref[pl.ds(s, n, stride=SUBS), :] = x[:, s * LANES:(s + 1) * LANES]


def _load_token_major(ref, n, row0=0):
    return jnp.concatenate([ref[pl.ds(row0 * SUBS + s, n, stride=SUBS), :] for s in range(SUBS)], axis=1)


def _token_major_spec(rows, index_map):
    return pl.BlockSpec((rows * SUBS, LANES), index_map)


def _rope_tables(n, rot_dim, lanes, lane0, copies=1):
    rows_count = n // GRID_W
    rows = np.repeat(np.arange(rows_count), GRID_W).astype(np.float64)
    cols = np.tile(np.arange(GRID_W), rows_count).astype(np.float64)
    d_axis = rot_dim // 2
    inv = ROPE_THETA ** (-np.arange(0, d_axis, 2, dtype=np.float64) / d_axis)
    ang = np.concatenate([rows[:, None] * inv, cols[:, None] * inv], -1)
    c = np.ones((n, lanes), np.float32)
    s = np.zeros((n, lanes), np.float32)
    for j in range(copies):
        lo = lane0 + j * rot_dim
        c[:, lo:lo + rot_dim] = np.repeat(np.cos(ang), 2, axis=1)
        s[:, lo:lo + rot_dim] = np.repeat(np.sin(ang), 2, axis=1) * np.tile(np.array([-1.0, 1.0]), rot_dim // 2)
    return jnp.asarray(c), jnp.asarray(s)


L0_Q0 = 2 * A_WIDTH
L0_K0 = L0_Q0 + B_HEADS * LANES
L0_V0 = L0_K0 + B_KV * B_HD
L0_IN = L0_V0 + B_KV * B_HD


def _l0_kernel(*refs, latent):
    if latent:
        (sink_ref, x_ref, gn_ref, mod_ref, win_ref, gvn_ref, ws_ref, bsb_ref, gq_ref, gk_ref, wo_ref,
         cos_ref, sin_ref, kc_ref, vc_ref, xo_ref, zs, cat, qs, ks, vt, kcb, vct) = refs
        key_off = WINDOW
    else:
        (sink_ref, x_ref, gn_ref, mod_ref, win_ref, gvn_ref, ws_ref, bsb_ref, gq_ref, gk_ref, wo_ref,
         xo_ref, ko_ref, vo_ref, zs, cat, qs, ks, vt, kf, vf) = refs
        key_off = 0
    n = BLOCK_ROWS
    n_chunks = n // CHUNK
    low = lax.broadcasted_iota(jnp.int32, (CHUNK, LANES), 1) < B_HD

    if latent:
        zpad = jnp.zeros((WINDOW, LANES), bf16)
        for c0 in (0, 1 + n_chunks):
            ks[c0 * CHUNK:(c0 + 1) * CHUNK, :] = zpad
            vt[c0] = zpad
        kcb[...] = kc_ref[...].astype(bf16)
        for i in range(PAST // CHUNK):
            vct[i] = vc_ref[i * CHUNK:(i + 1) * CHUNK, :].T.astype(bf16)

    def project(c, carry):
        r = pl.ds(pl.multiple_of(c * PROJ_ROWS, PROJ_ROWS), PROJ_ROWS)
        h = _rms_rows(x_ref[r, :], gn_ref[...]) * (1.0 + mod_ref[1:2, :]) + mod_ref[0:1, :]
        zs[r, :] = jnp.dot(h.astype(bf16), win_ref[...], preferred_element_type=f32)
        return carry

    lax.fori_loop(0, n // PROJ_ROWS, project, 0)

    def prepare(c, carry):
        r = pl.ds(pl.multiple_of(c * CHUNK, CHUNK), CHUNK)
        u = jax.nn.gelu(zs[r, 0:A_WIDTH])
        v = jax.nn.gelu(zs[r, A_WIDTH:2 * A_WIDTH])
        mu = jnp.mean(v, -1, keepdims=True)
        var = jnp.mean(jnp.square(v - mu), -1, keepdims=True)
        vn = ((v - mu) * lax.rsqrt(var + EPS) * gvn_ref[...]).astype(bf16)
        for g in range(A_GROUPS):
            cs = slice(g * CHUNK, (g + 1) * CHUNK)
            mixed = jnp.dot(ws_ref[g], vn[:, cs], preferred_element_type=f32) + bsb_ref[g]
            cat[r, cs] = (u[:, cs] * mixed).astype(bf16)
        if latent:
            cs_, sn_ = cos_ref[r, :], sin_ref[r, :]
        for h in range(B_HEADS):
            hs = slice(h * LANES, (h + 1) * LANES)
            qh = zs[r, L0_Q0 + h * LANES:L0_Q0 + (h + 1) * LANES]
            qh = qh * lax.rsqrt(jnp.sum(qh * qh, -1, keepdims=True) * (1.0 / B_HD) + EPS) * gq_ref[:, hs]
            if latent:
                qh = qh * cs_ + _swap_pairs(qh) * sn_
            qs[r, hs] = (qh * B_SCALE).astype(bf16)
        k = zs[r, L0_K0:L0_K0 + LANES]
        k2 = k * k
        s0 = jnp.sum(jnp.where(low, k2, 0.0), -1, keepdims=True)
        s1 = jnp.sum(jnp.where(low, 0.0, k2), -1, keepdims=True)
        k = k * lax.rsqrt(jnp.where(low, s0, s1) * (1.0 / B_HD) + EPS) * gk_ref[...]
        vv = zs[r, L0_V0:L0_V0 + LANES]
        if latent:
            k = k * cs_ + _swap_pairs(k) * sn_
        else:
            kf[r, :] = k
            vf[r, :] = vv
        kr = pl.ds(pl.multiple_of(c * CHUNK + key_off, CHUNK), CHUNK)
        ks[kr, :] = k.astype(bf16)
        vt[c + key_off // CHUNK] = vv.T.astype(bf16)
        return carry

    lax.fori_loop(0, n_chunks, prepare, 0)

    def attend(r, rows, key_sets):
        q = jnp.concatenate([qs[r, h * LANES:(h + 1) * LANES] for h in range(B_HEADS)], axis=0)
        sk = jnp.concatenate([jnp.full((1, rows), sink_ref[h], f32) for h in range(B_HEADS)], axis=1)
        scores = []
        m = sk
        for k, _, keep in key_sets:
            s = lax.dot_general(k, q, (((1,), (1,)), ((), ())), preferred_element_type=f32)
            if keep is not None:
                s = jnp.where(keep, s, NEG_INF)
            scores.append(s)
            m = jnp.maximum(m, jnp.max(s, 0, keepdims=True))
        den = jnp.exp(sk - m)
        ot = None
        for s, (_, vts, _) in zip(scores, key_sets):
            e = jnp.exp(s - m)
            den = den + jnp.sum(e, 0, keepdims=True)
            eb = e.astype(bf16)
            for i, v_t in enumerate(vts):
                pv = jnp.dot(v_t, eb[i * CHUNK:(i + 1) * CHUNK, :], preferred_element_type=f32)
                ot = pv if ot is None else ot + pv
        ot = ot * (1.0 / den)
        for pair in range(B_HEADS // 2):
            f0 = (2 * pair // B_GROUP) * B_HD
            pair_t = jnp.concatenate([ot[f0:f0 + B_HD, 2 * pair * rows:(2 * pair + 1) * rows],
                                      ot[f0:f0 + B_HD, (2 * pair + 1) * rows:(2 * pair + 2) * rows]], axis=0)
            cat[r, A_WIDTH + pair * LANES:A_WIDTH + (pair + 1) * LANES] = pair_t.T.astype(bf16)

    if latent:
        span = CHUNK + 2 * WINDOW

        def attend_block(c, carry):
            start = pl.multiple_of(c * CHUNK, CHUNK)
            kr = pl.ds(start, span)
            kj = lax.broadcasted_iota(jnp.int32, (span, B_HEADS * CHUNK), 0)
            qi = lax.broadcasted_iota(jnp.int32, (span, B_HEADS * CHUNK), 1) & (CHUNK - 1)
            kpos = start - WINDOW + kj
            keep = (jnp.abs(kj - WINDOW - qi) <= WINDOW) & (kpos >= 0) & (kpos < n)
            attend(pl.ds(start, CHUNK), CHUNK,
                   [(ks[kr, :], [vt[c + i] for i in range(span // CHUNK)], keep),
                    (kcb[...], [vct[i] for i in range(PAST // CHUNK)], None)])
            return carry

        lax.fori_loop(0, n_chunks, attend_block, 0)
    else:
        def attend_seq(sq, carry):
            r = pl.ds(pl.multiple_of(sq * SEQ, SEQ), SEQ)
            attend(r, SEQ, [(ks[r, :], [vt[sq * (SEQ // CHUNK) + i] for i in range(SEQ // CHUNK)], None)])
            ko_ref[sq] = kf[r, :].T
            vo_ref[sq] = vf[r, :].T
            return carry

        lax.fori_loop(0, n // SEQ, attend_seq, 0)

    def output(c, carry):
        r = pl.ds(pl.multiple_of(c * PROJ_ROWS, PROJ_ROWS), PROJ_ROWS)
        y = jnp.dot(cat[r, :], wo_ref[...], preferred_element_type=f32)
        xo_ref[r, :] = x_ref[r, :] + mod_ref[2:3, :] * y
        return carry

    lax.fori_loop(0, n // PROJ_ROWS, output, 0)


def _l0_mixer(xc, xl, g_norm, mods, p, cache_k, cache_v):
    w = p['w_in']
    q = w[:, 2 * A_WIDTH:2 * A_WIDTH + B_HEADS * B_HD].reshape(D, B_HEADS, B_HD)
    zero = jnp.zeros((D, B_GROUP, B_HD), f32)
    q_slots = jnp.concatenate([jnp.concatenate([q[:, :B_GROUP], zero], axis=2),
                               jnp.concatenate([zero, q[:, B_GROUP:]], axis=2)], axis=1).reshape(D, B_HEADS * LANES)
    win = jnp.concatenate([w[:, :2 * A_WIDTH], q_slots, w[:, 2 * A_WIDTH + B_HEADS * B_HD:]], axis=1).astype(bf16)
    zg = jnp.zeros((B_HD,), f32)
    gq = jnp.concatenate([jnp.tile(jnp.concatenate([p['g_q'], zg]), B_GROUP),
                          jnp.tile(jnp.concatenate([zg, p['g_q']]), B_GROUP)]).reshape(1, B_HEADS * LANES)
    gk = jnp.tile(p['g_k'], B_KV).reshape(1, LANES)
    gvn = p['g_vnorm'].reshape(1, A_WIDTH)
    ws = p['w_s'].astype(bf16)
    bsb = jnp.broadcast_to(p['b_s'][:, :, None], (A_GROUPS, CHUNK, CHUNK))
    wo = p['w_o'].astype(bf16)
    weights = (g_norm.reshape(1, D),)
    consts = (win, gvn, ws, bsb, gq, gk, wo)
    c_specs = [_const_spec(a.shape) for a in consts]
    smem = pl.BlockSpec(memory_space=pltpu.SMEM)
    row = pl.BlockSpec((BLOCK_ROWS, D), lambda b: (b, 0))
    kv = pl.BlockSpec((BLOCK_ROWS // SEQ, LANES, SEQ), lambda b: (b, 0, 0))

    def scratch(pad):
        return [pltpu.VMEM((BLOCK_ROWS, L0_IN), f32), pltpu.VMEM((BLOCK_ROWS, D), bf16),
                pltpu.VMEM((BLOCK_ROWS, B_HEADS * LANES), bf16), pltpu.VMEM((BLOCK_ROWS + pad, LANES), bf16),
                pltpu.VMEM(((BLOCK_ROWS + pad) // CHUNK, LANES, CHUNK), bf16)]

    kv_shape = jax.ShapeDtypeStruct((BATCH, LANES, SEQ), f32)
    state = pltpu.VMEM((BLOCK_ROWS, LANES), f32)
    xo_ctx, k_t, v_t = pl.pallas_call(
        functools.partial(_l0_kernel, latent=False),
        grid=(T_CTX // BLOCK_ROWS,),
        in_specs=[smem, row, _const_spec((1, D)), pl.BlockSpec((8, D), lambda b: (0, 0))] + c_specs,
        out_specs=[row, kv, kv],
        out_shape=[jax.ShapeDtypeStruct((T_CTX, D), f32), kv_shape, kv_shape],
        scratch_shapes=scratch(0) + [state, state],
        compiler_params=_cp(("parallel",)), name="l0_mixer_ctx",
    )(p['sink'], xc, *weights, mods, *consts)
    k_new = k_t.reshape(BATCH, B_KV, B_HD, SEQ).transpose(0, 3, 1, 2)
    v_new = v_t.reshape(BATCH, B_KV, B_HD, SEQ).transpose(0, 3, 1, 2)

    cos, sin = _rope_tables(DEC_SEQ, B_HD, LANES, 0, copies=LANES // B_HD)
    cache = pl.BlockSpec((None, PAST, LANES), lambda b: (b, 0, 0))
    past = [pltpu.VMEM((PAST, LANES), bf16), pltpu.VMEM((PAST // CHUNK, LANES, CHUNK), bf16)]
    xo_lat = pl.pallas_call(
        functools.partial(_l0_kernel, latent=True),
        grid=(DEC_BATCH,),
        in_specs=[smem, row, _const_spec((1, D)), pl.BlockSpec((8, D), lambda b: (1 + b, 0))] + c_specs + [
                  _const_spec(cos.shape), _const_spec(sin.shape), cache, cache],
        out_specs=row,
        out_shape=jax.ShapeDtypeStruct((T_LAT, D), f32),
        scratch_shapes=scratch(2 * WINDOW) + past,
        compiler_params=_cp(("parallel",)), name="l0_mixer_lat",
    )(p['sink'], xl, *weights, mods, *consts, cos, sin,
      cache_k.reshape(DEC_BATCH, PAST, LANES), cache_v.reshape(DEC_BATCH, PAST, LANES))
    return xo_ctx, xo_lat, k_new, v_new


C_SLOTS = C_HEADS * SLOT
C_PAIRS = C_HEADS // 2
L1_ROWS = 256


def _l1_kernel(*refs, latent):
    if latent:
        (x_ref, d_ref, gn_ref, mod_ref, win_ref, gqa_ref, wuq_ref, gq_ref, gkva_ref, wuk_ref, wuvt_ref, gk_ref,
         wo_ref, wuqs_ref, qcos_ref, qsin_ref, kcos_ref, ksin_ref, cckv_ref, ckpe_ref, xo_ref,
         zs, cat, qs, ks, vt, wide, wide2) = refs
        n_ctx = PAST
    else:
        (x_ref, d_ref, gn_ref, mod_ref, win_ref, gqa_ref, wuq_ref, gq_ref, gkva_ref, wuk_ref, wuvt_ref, gk_ref,
         wo_ref, xo_ref, ckvo_ref, kpeo_ref, zs, cat, qs, ks, vt, wide) = refs
        n_ctx = 0
    n = BLOCK_ROWS
    nt_dims = (((1,), (1,)), ((), ()))

    def inv_rms(v):
        return lax.rsqrt(jnp.sum(v * v, -1, keepdims=True) * (1.0 / C_QK) + EPS)

    def expand_keys(ckv_n, kslot, kb, rope_rows):
        cb = ckv_n.astype(bf16)
        key_rows = pl.ds(pl.multiple_of(kb * L1_ROWS, L1_ROWS), L1_ROWS)
        wide[...] = jnp.dot(cb, wuk_ref[...], preferred_element_type=f32)
        if rope_rows is not None:
            kcos = kcos_ref[rope_rows, :]
            turned = _swap_pairs(kslot) * ksin_ref[rope_rows, :]
        for h in range(C_HEADS):
            kh = wide[:, h * SLOT:(h + 1) * SLOT] + kslot
            if rope_rows is not None:
                kh = inv_rms(kh) * (kh * kcos + turned)
            else:
                kh = kh * inv_rms(kh) * gk_ref[...]
            ks[h, key_rows, :] = kh.astype(bf16)
        v_t = lax.dot_general(wuvt_ref[...], cb, nt_dims, preferred_element_type=f32).astype(bf16)
        for pair in range(C_PAIRS):
            vt[pair, kb] = v_t[pair * LANES:(pair + 1) * LANES, :]

    if latent:
        def past_keys(c, carry):
            r = pl.ds(pl.multiple_of(c * L1_ROWS, L1_ROWS), L1_ROWS)
            expand_keys(cckv_ref[r, :], ckpe_ref[r, :], c, None)
            return carry

        lax.fori_loop(0, PAST // L1_ROWS, past_keys, 0)

    for c in range(n // PROJ_ROWS):
        r = pl.ds(c * PROJ_ROWS, PROJ_ROWS)
        x1 = x_ref[r, :] + mod_ref[5:6, :] * _load_token_major(d_ref, PROJ_ROWS, c * PROJ_ROWS)
        xo_ref[r, :] = x1
        h = _rms_rows(x1, gn_ref[...]) * (1.0 + mod_ref[1:2, :]) + mod_ref[0:1, :]
        zs[r, :] = jnp.dot(h.astype(bf16), win_ref[...], preferred_element_type=f32)

    def prepare(c, carry):
        r = pl.ds(pl.multiple_of(c * L1_ROWS, L1_ROWS), L1_ROWS)
        qa = _rms_rows(zs[r, 0:C_Q_LORA], gqa_ref[...]).astype(bf16)
        wide[...] = jnp.dot(qa, wuq_ref[...], preferred_element_type=f32)
        if latent:
            wide2[...] = jnp.dot(qa, wuqs_ref[...], preferred_element_type=f32)
            qcos, qsin = qcos_ref[r, :], qsin_ref[r, :]
        for h in range(C_HEADS):
            hs = slice(h * SLOT, (h + 1) * SLOT)
            qh = wide[:, hs]
            if latent:
                qh = inv_rms(qh) * (qh * qcos + wide2[:, hs] * qsin)
            else:
                qh = qh * inv_rms(qh) * gq_ref[...]
            qs[h, r, :] = (qh * C_SCALE).astype(bf16)
        ckv_n = _rms_rows(zs[r, C_Q_LORA:C_Q_LORA + C_KV_LORA], gkva_ref[...])
        kslot = zs[r, C_Q_LORA + C_KV_LORA:ODD_IN_PAD]
        if not latent:
            ckvo_ref[r, :] = ckv_n
            kpeo_ref[r, :] = kslot
        expand_keys(ckv_n, kslot, c + n_ctx // L1_ROWS, r if latent else None)
        return carry

    lax.fori_loop(0, n // L1_ROWS, prepare, 0)

    low = lax.broadcasted_iota(jnp.int32, (2 * C_V, L1_ROWS), 0) < C_V
    n_kblocks = (n_ctx + n) // L1_ROWS
    pairs_per_step = 2 if latent else 4

    def attend(c, carry):
        r = pl.ds(pl.multiple_of(c * L1_ROWS, L1_ROWS), L1_ROWS)

        def values_t(pair, eb):
            if not latent:
                return jnp.dot(vt[pair, c], eb, preferred_element_type=f32)
            o_t = None
            for b in range(n_kblocks):
                pv = jnp.dot(vt[pair, b], eb[b * L1_ROWS:(b + 1) * L1_ROWS, :], preferred_element_type=f32)
                o_t = pv if o_t is None else o_t + pv
            return o_t

        def pairs_step(i, carry2):
            pairs = [i * pairs_per_step + j for j in range(pairs_per_step)]
            heads = [2 * p + hh for p in pairs for hh in range(2)]
            scores = [lax.dot_general(ks[h] if latent else ks[h, r, :], qs[h, r, :], nt_dims,
                                      preferred_element_type=f32) for h in heads]
            exps = [jnp.exp(s - jnp.max(s, 0, keepdims=True)) for s in scores]
            dens = [jnp.sum(e, 0, keepdims=True) for e in exps]
            outs = [values_t(h // 2, e.astype(bf16)) / den for h, e, den in zip(heads, exps, dens)]
            for j, pair in enumerate(pairs):
                cat[pair, r, :] = jnp.where(low, outs[2 * j], outs[2 * j + 1]).T.astype(bf16)
            return carry2

        return lax.fori_loop(0, C_PAIRS // pairs_per_step, pairs_step, carry)

    lax.fori_loop(0, n // L1_ROWS, attend, 0)

    def output(c, carry):
        r = pl.ds(pl.multiple_of(c * PROJ_ROWS, PROJ_ROWS), PROJ_ROWS)
        heads = jnp.concatenate([cat[pair, r, :] for pair in range(C_PAIRS)], axis=1)
        y = jnp.dot(heads, wo_ref[...], preferred_element_type=f32)
        xo_ref[r, :] = xo_ref[r, :] + mod_ref[2:3, :] * y
        return carry

    lax.fori_loop(0, n // PROJ_ROWS, output, 0)


def _slot_cols(w, heads, width, lo, hi, lane0):
    k = w.shape[0]
    w3 = w.reshape(k, heads, width)[:, :, lo:hi]
    out = jnp.zeros((k, heads, SLOT), w.dtype).at[:, :, lane0:lane0 + (hi - lo)].set(w3)
    return out.reshape(k, heads * SLOT)


def _l1_mixer(xc, xl, moe_prev, g_norm, mods, p, cache_ckv, cache_kpe):
    w_in = jnp.zeros((D, ODD_IN_PAD), f32).at[:, :C_Q_LORA + C_KV_LORA].set(
        p['w_in'][:, :C_Q_LORA + C_KV_LORA]).at[
        :, C_Q_LORA + C_KV_LORA + C_NOPE:C_Q_LORA + C_KV_LORA + C_QK].set(p['w_in'][:, C_Q_LORA + C_KV_LORA:])
    wuq = _slot_cols(p['w_uq'], C_HEADS, C_QK, 0, C_QK, 0).astype(bf16)
    wuk = _slot_cols(p['w_ukv'], C_HEADS, C_NOPE + C_V, 0, C_NOPE, 0).astype(bf16)
    wuv_t = p['w_ukv'].reshape(C_KV_LORA, C_HEADS, C_NOPE + C_V)[:, :, C_NOPE:].reshape(
        C_KV_LORA, C_HEADS * C_V).T.astype(bf16)
    gq = jnp.zeros((1, SLOT), f32).at[0, :C_QK].set(p['g_q'])
    gk = jnp.zeros((1, SLOT), f32).at[0, :C_QK].set(p['g_k'])
    consts = (g_norm.reshape(1, D), w_in.astype(bf16), p['g_qa'].reshape(1, C_Q_LORA), wuq, gq,
              p['g_kva'].reshape(1, C_KV_LORA), wuk, wuv_t, gk, p['w_o'].astype(bf16))
    c_specs = [_const_spec(a.shape) for a in consts]
    row = pl.BlockSpec((BLOCK_ROWS, D), lambda b: (b, 0))
    n_ctx_blocks = T_CTX // BLOCK_ROWS

    def scratch(n_keys):
        return [pltpu.VMEM((BLOCK_ROWS, ODD_IN_PAD), f32), pltpu.VMEM((C_PAIRS, BLOCK_ROWS, LANES), bf16),
                pltpu.VMEM((C_HEADS, BLOCK_ROWS, SLOT), bf16), pltpu.VMEM((C_HEADS, n_keys, SLOT), bf16),
                pltpu.VMEM((C_PAIRS, n_keys // L1_ROWS, LANES, L1_ROWS), bf16),
                pltpu.VMEM((L1_ROWS, C_SLOTS), f32)]

    xo_ctx, ckv_new, kpe_slot = pl.pallas_call(
        functools.partial(_l1_kernel, latent=False),
        grid=(n_ctx_blocks,),
        in_specs=[row, _token_major_spec(BLOCK_ROWS, lambda b: (b, 0)), c_specs[0],
                  pl.BlockSpec((8, D), lambda b: (0, 0))] + c_specs[1:],
        out_specs=[row, pl.BlockSpec((BLOCK_ROWS, C_KV_LORA), lambda b: (b, 0)),
                   pl.BlockSpec((BLOCK_ROWS, SLOT), lambda b: (b, 0))],
        out_shape=[jax.ShapeDtypeStruct((T_CTX, D), f32), jax.ShapeDtypeStruct((T_CTX, C_KV_LORA), f32),
                   jax.ShapeDtypeStruct((T_CTX, SLOT), f32)],
        scratch_shapes=scratch(BLOCK_ROWS),
        compiler_params=_cp(("parallel",)), name="l1_mixer_ctx",
    )(xc, moe_prev, consts[0], mods, *consts[1:])

    cos, sin = _rope_tables(DEC_SEQ, C_ROPE, SLOT, C_NOPE)
    pair_swap = lambda a: a.reshape(a.shape[:-1] + (a.shape[-1] // 2, 2))[..., ::-1].reshape(a.shape)
    rope = (pair_swap(wuq), gq * cos, pair_swap(gq) * sin, gk * cos, pair_swap(gk) * sin)
    ckpe = jnp.zeros((DEC_BATCH, PAST, SLOT), f32).at[:, :, C_NOPE:C_QK].set(cache_kpe)
    xo_lat = pl.pallas_call(
        functools.partial(_l1_kernel, latent=True),
        grid=(DEC_BATCH,),
        in_specs=[pl.BlockSpec((BLOCK_ROWS, D), lambda b: (b, 0), pipeline_mode=pl.Buffered(1)),
                  pl.BlockSpec((BLOCK_ROWS * SUBS, LANES), lambda b: (n_ctx_blocks + b, 0),
                               pipeline_mode=pl.Buffered(1)), c_specs[0],
                  pl.BlockSpec((8, D), lambda b: (1 + b, 0))] + c_specs[1:] + [_const_spec(a.shape) for a in rope] + [
                  pl.BlockSpec((None, PAST, C_KV_LORA), lambda b: (b, 0, 0)),
                  pl.BlockSpec((None, PAST, SLOT), lambda b: (b, 0, 0))],
        out_specs=row,
        out_shape=jax.ShapeDtypeStruct((T_LAT, D), f32),
        scratch_shapes=scratch(PAST + BLOCK_ROWS) + [pltpu.VMEM((L1_ROWS, C_SLOTS), f32)],
        compiler_params=_cp(("parallel",)), name="l1_mixer_lat",
    )(xl, moe_prev, consts[0], mods, *consts[1:], *rope, cache_ckv, ckpe)
    return xo_ctx, xo_lat, ckv_new, kpe_slot[:, C_NOPE:C_QK]


ROUTER_ROWS = 40


def _router_kernel(xc_ref, xl_ref, gn_ref, mod_ref, whi_ref, wlo_ref, br_ref, h_ref, route_ref):
    h = _rms_rows(_token_rows(xc_ref, xl_ref), gn_ref[...]) * (1.0 + mod_ref[4:5, :]) + mod_ref[3:4, :]
    _store_token_major(h_ref, h)
    h_hi, h_lo = _split_bf16(h)
    nt = (((1,), (1,)), ((), ()))
    logits = (lax.dot_general(whi_ref[...], h_hi, nt, preferred_element_type=f32)
              + lax.dot_general(whi_ref[...], h_lo, nt, preferred_element_type=f32)
              + lax.dot_general(wlo_ref[...], h_hi, nt, preferred_element_type=f32))
    logits = logits[0:ROUTER_ROWS, :] + br_ref[0:ROUTER_ROWS, :]
    row_i = lax.broadcasted_iota(jnp.int32, logits.shape, 0)
    row = row_i.astype(f32)
    big = 1e6
    is_g = (row_i >= N_EXPERTS) & (row_i < N_EXPERTS + N_GROUPS)
    lg = jnp.where(is_g, logits, -jnp.inf)
    mg = jnp.max(lg, 0, keepdims=True)
    gsel = jnp.min(jnp.where(lg == mg, row, big), 0, keepdims=True) - N_EXPERTS
    pg_sel = 1.0 / jnp.sum(jnp.where(is_g, jnp.exp(lg - mg), 0.0), 0, keepdims=True)
    in_grp = (row_i < N_EXPERTS) & ((row_i >> 3).astype(f32) == gsel)
    le = jnp.where(in_grp, logits, -jnp.inf)
    m1 = jnp.max(le, 0, keepdims=True)
    i1 = jnp.min(jnp.where(le == m1, row, big), 0, keepdims=True)
    le2 = jnp.where(row == i1, -jnp.inf, le)
    m2 = jnp.max(le2, 0, keepdims=True)
    i2 = jnp.min(jnp.where(le2 == m2, row, big), 0, keepdims=True)
    e2 = jnp.exp(m2 - m1)
    w1 = pg_sel / (1.0 + e2)
    w2 = pg_sel * e2 / (1.0 + e2)
    sub = lax.broadcasted_iota(jnp.int32, route_ref.shape, 0)
    route_ref[...] = jnp.where(sub == 0, i1, jnp.where(sub == 1, i2, jnp.where(sub == 2, w1,
                                                                                jnp.where(sub == 3, w2, 0.0))))


def _router(xc, xl, g_norm, mods, p):
    wr = jnp.zeros((128, D), f32).at[:N_EXPERTS].set(p['w_re'].T).at[
        N_EXPERTS:N_EXPERTS + N_GROUPS].set(p['w_rg'].T)
    w_hi, w_lo = _split_bf16(wr)
    br =jnp.zeros((128, 1), f32).at[:N_EXPERTS, 0].set(p['b_re']).at[
        N_EXPERTS:N_EXPERTS + N_GROUPS, 0].set(p['b_rg'])
    return pl.pallas_call(
        _router_kernel,
        grid=(T // ROW_TILE,),
        in_specs=_token_specs(D) + [
                  _const_spec((1, D)),
                  pl.BlockSpec((8, D), lambda i: (_sample_of_tile(i, ROW_TILE), 0)),
                  _const_spec((128, D)), _const_spec((128, D)), _const_spec((128, 1))],
        out_specs=[_token_major_spec(ROW_TILE, lambda i: (i, 0)), pl.BlockSpec((8, ROW_TILE), lambda i: (0, i))],
        out_shape=[jax.ShapeDtypeStruct((T * SUBS, LANES), f32), jax.ShapeDtypeStruct((8, T), f32)],
        compiler_params=_cp(("parallel",)), name="router",
    )(xc, xl, g_norm.reshape(1, D), mods, w_hi, w_lo, br)


PLAN_FIRST_TILE, PLAN_TILES, PLAN_N_USED = 0, 1, 2


def _plan_kernel(rt_ref, pos_ref, plan_ref, rank):
    n_blk = T // 128
    e_col = lax.broadcasted_iota(jnp.int32, (N_EXPERTS, 128), 0).astype(f32)
    ri = lax.broadcasted_iota(jnp.int32, (128, 128), 0)
    ci = lax.broadcasted_iota(jnp.int32, (128, 128), 1)
    before = jnp.where(ri < ci, 1.0, 0.0).astype(bf16)

    def picks(b):
        cs = slice(b * 128, (b + 1) * 128)
        return rt_ref[0:1, cs] == e_col, rt_ref[1:2, cs] == e_col

    counts = jnp.zeros((N_EXPERTS, 1), f32)
    for b in range(n_blk):
        m0, m1 = picks(b)
        m = jnp.where(m0, 1.0, 0.0) + jnp.where(m1, 1.0, 0.0)
        rank[:, b * 128:(b + 1) * 128] = jnp.dot(m.astype(bf16), before, preferred_element_type=f32) + counts
        counts = counts + jnp.sum(m, axis=1, keepdims=True)

    tiles = jnp.floor((counts + (MOE_TILE - 1.0)) * (1.0 / MOE_TILE))
    er = lax.broadcasted_iota(jnp.int32, (N_EXPERTS, N_EXPERTS), 0)
    ec = lax.broadcasted_iota(jnp.int32, (N_EXPERTS, N_EXPERTS), 1)
    earlier = jnp.where(ec < er, 1.0, 0.0).astype(bf16)
    tile_start = jnp.dot(earlier, jnp.broadcast_to(tiles, (N_EXPERTS, 128)).astype(bf16),
                         preferred_element_type=f32)
    row_start = tile_start * MOE_TILE

    sub = lax.broadcasted_iota(jnp.int32, (8, 128), 0)
    for b in range(n_blk):
        m0, m1 = picks(b)
        base = rank[:, b * 128:(b + 1) * 128] + row_start
        p0 = jnp.sum(jnp.where(m0, base, 0.0), axis=0, keepdims=True)
        p1 = jnp.sum(jnp.where(m1, base, 0.0), axis=0, keepdims=True)
        pos_ref[:, b * 128:(b + 1) * 128] = jnp.where(sub == 0, p0, jnp.where(sub == 1, p1, 0.0)).astype(jnp.int32)

    tile_end = tile_start + tiles
    n_used = jnp.max(tile_end, axis=0, keepdims=True)
    diag =(lax.broadcasted_iota(jnp.int32, (N_EXPERTS, 128), 0)
            == lax.broadcasted_iota(jnp.int32, (N_EXPERTS, 128), 1))
    first = jnp.sum(jnp.where(diag, tile_start, 0.0), axis=0, keepdims=True)
    count = jnp.sum(jnp.where(diag, tiles, 0.0), axis=0, keepdims=True)
    rows = jnp.where(sub == PLAN_FIRST_TILE, first,
                     jnp.where(sub == PLAN_TILES, count, jnp.where(sub == PLAN_N_USED, n_used, 0.0)))
    plan_ref[...] = rows.astype(jnp.int32)


def _slot_code(t, k):
    return t * SUBS + k * (SUBS // 2)


def _code_offset(code):
    return pl.multiple_of(code & ~(SUBS - 1), SUBS)


def _code_gate_index(code):
    return code >> 2


PAD_CODE = T * SUBS


def _invert_kernel(pos_ref, plan_ref, code_ref):
    def pad_tile(tile, carry):
        for u in range(MOE_TILE):
            code_ref[tile * MOE_TILE + u] = PAD_CODE
        return carry

    def pad_last_tile(e, carry):
        return pad_tile(jnp.maximum(plan_ref[PLAN_FIRST_TILE, e] + plan_ref[PLAN_TILES, e] - 1, 0), carry)
    lax.fori_loop(0, N_EXPERTS, pad_last_tile, 0)
    lax.fori_loop(plan_ref[PLAN_N_USED, 0], MOE_TILES, pad_tile, 0)

    group = 32
    for k in range(2):
        def place(i, carry):
            t0 = i * group
            slots = [pos_ref[k * T + t0 + u] for u in range(group)]
            for u, s in enumerate(slots):
                code_ref[s] = _slot_code(t0 + u, k)
            return carry
        lax.fori_loop(0, T // group, place, 0)


def _route_plan(route_t):
    pos, plan = pl.pallas_call(
        _plan_kernel,
        out_shape=[jax.ShapeDtypeStruct((8, T), jnp.int32), jax.ShapeDtypeStruct((8, LANES), jnp.int32)],
        scratch_shapes=[pltpu.VMEM((N_EXPERTS, T), f32)],
        compiler_params=_cp(None), name="route_plan",
    )(route_t)
    smem = pl.BlockSpec(memory_space=pltpu.SMEM)
    codes = pl.pallas_call(
        _invert_kernel,
        in_specs=[smem, smem], out_specs=smem,
        out_shape=jax.ShapeDtypeStruct((MOE_ROWS,), jnp.int32),
        name="route_invert",
    )(pos[0:2].reshape(2 * T), plan)
    gates = jnp.pad(route_t[2:4].T.reshape(2 * T), (0, 8))
    return plan[PLAN_FIRST_TILE, :N_EXPERTS], plan[PLAN_TILES, :N_EXPERTS], codes, gates


def _tile_index(i):
    return jnp.minimum(i, MOE_TILES - 1)


TM_ROWS = T * SUBS
SCATTER_GROUP = 16
LAST_EXPERT = N_EXPERTS - 1


def _tile_rows(g):
    return pl.ds(pl.multiple_of(g * MOE_TILE, MOE_TILE), MOE_TILE)


def _expert_tile_pairs(first, count, tile_step):
    def pair(pp, carry):
        for parity in range(2):
            g = 2 * pp + parity
            pl.when((g >= first) & (g < first + count))(functools.partial(tile_step, g, parity))
        return carry
    lax.fori_loop(first // 2, (first + count + 1) // 2, pair, 0)


def _gather_tile(code_ref, tile, xs, gbuf):
    base = tile * MOE_TILE
    for r in range(MOE_TILE):
        gbuf[r * SUBS:(r + 1) * SUBS, :] = xs[pl.ds(_code_offset(code_ref[base + r]), SUBS), :]


def _moe_up_kernel(first_ref, count_ref, code_ref, h_hbm, w1_ref, w3_ref, hh_hbm,
                   xs, gbuf_a, gbuf_b, w13, obuf, sem_x, sem_o):
    e = pl.program_id(0)
    first, count = first_ref[e], count_ref[e]
    gbufs = (gbuf_a, gbuf_b)

    def out_copy(slot, g):
        return pltpu.make_async_copy(obuf.at[slot], hh_hbm.at[_tile_rows(g), :], sem_o.at[slot])

    @pl.when(e == 0)
    def _():
        cp = pltpu.make_async_copy(h_hbm, xs.at[pl.ds(0, TM_ROWS), :], sem_x)
        cp.start()
        xs[TM_ROWS:TM_ROWS + SUBS, :] = jnp.zeros((SUBS, LANES), f32)
        cp.wait()
        _gather_tile(code_ref, 0, xs, gbuf_a)

    @pl.when(count > 0)
    def _():
        w13[:, :D_EXPERT] = w1_ref[0].astype(bf16)
        w13[:, D_EXPERT:] = w3_ref[0].astype(bf16)

    def tile_step(g, parity):
        _gather_tile(code_ref, _tile_index(g + 1), xs, gbufs[1 - parity])
        x3 = jnp.swapaxes(gbufs[parity][...].reshape(MOE_TILE, SUBS, LANES), 0, 1)
        x = jnp.concatenate([x3[s] for s in range(SUBS)], axis=1).astype(bf16)
        h13 = jnp.dot(x, w13[...], preferred_element_type=f32)
        hh = (_silu(h13[:, :D_EXPERT]) * h13[:, D_EXPERT:]).astype(bf16)

        @pl.when(g >= 2)
        def _():
            out_copy(parity, g).wait()
        obuf[parity] = hh
        out_copy(parity, g).start()

    _expert_tile_pairs(first, count, tile_step)

    @pl.when(e == LAST_EXPERT)
    def _():
        n_used = first + count
        for slot in range(2):
            pl.when(n_used > slot)(lambda slot=slot: out_copy(slot, 0).wait())
        obuf[0] = jnp.zeros((MOE_TILE, D_EXPERT), bf16)

        def zero_tile(g, carry):
            cp = out_copy(0, g)
            cp.start()
            cp.wait()
            return carry
        lax.fori_loop(n_used, MOE_TILES, zero_tile, 0)


def _scatter_tile(code_ref, gate_ref, tile, ybuf, acc):
    base = tile * MOE_TILE
    for g0 in range(0, MOE_TILE, SCATTER_GROUP):
        rows = range(g0, g0 + SCATTER_GROUP)
        codes = [code_ref[base + r] for r in rows]
        new = [acc[pl.ds(_code_offset(c), SUBS), :]
               + gate_ref[_code_gate_index(c)] * ybuf[r * SUBS:(r + 1) * SUBS, :]
               for r, c in zip(rows, codes)]
        for c, v in zip(codes, new):
            acc[pl.ds(_code_offset(c), SUBS), :] = v


def _moe_down_kernel(first_ref, count_ref, code_ref, gate_ref, hh_hbm, w2_ref, o_hbm,
                     acc, ybuf_a, ybuf_b, w2b, ibuf, sem_i, sem_o):
    e = pl.program_id(0)
    first, count = first_ref[e], count_ref[e]
    n_used = first_ref[LAST_EXPERT] + count_ref[LAST_EXPERT]
    ybufs = (ybuf_a, ybuf_b)

    def in_copy(slot, g):
        return pltpu.make_async_copy(hh_hbm.at[_tile_rows(g), :], ibuf.at[slot], sem_i.at[slot])

    @pl.when(e == 0)
    def _():
        in_copy(0, 0).start()

        def zero(c, carry):
            acc[pl.ds(pl.multiple_of(c * 1024, 1024), 1024), :] = jnp.zeros((1024, LANES), f32)
            return carry
        lax.fori_loop(0, TM_ROWS // 1024, zero, 0)
        acc[TM_ROWS:TM_ROWS + SUBS, :] = jnp.zeros((SUBS, LANES), f32)
        ybuf_b[...] = jnp.zeros_like(ybuf_b)

    @pl.when(count > 0)
    def _():
        w2b[...] = w2_ref[0].astype(bf16)

    def tile_step(g, parity):
        in_copy(parity, g).wait()

        @pl.when(g + 1 < n_used)
        def _():
            in_copy(1 - parity, g + 1).start()
        _store_token_major(ybufs[parity], jnp.dot(ibuf[parity], w2b[...], preferred_element_type=f32))
        _scatter_tile(code_ref, gate_ref, jnp.maximum(g - 1, 0), ybufs[1 - parity], acc)

    _expert_tile_pairs(first, count, tile_step)

    @pl.when(e == LAST_EXPERT)
    def _():
        for parity in range(2):
            pl.when((n_used > 0) & ((n_used - 1) % 2 == parity))(
                functools.partial(_scatter_tile, code_ref, gate_ref, n_used - 1, ybufs[parity], acc))
        cp = pltpu.make_async_copy(acc.at[pl.ds(0, TM_ROWS), :], o_hbm, sem_o)
        cp.start()
        cp.wait()


def _moe(h_tm, route_t, p):
    first_tile, n_tiles, codes, gates = _route_plan(route_t)
    tile_rows = pltpu.VMEM((MOE_TILE * SUBS, LANES), f32)
    staging = pltpu.VMEM((2, MOE_TILE, D_EXPERT), bf16)
    hh = pl.pallas_call(
        _moe_up_kernel,
        grid_spec=pltpu.PrefetchScalarGridSpec(
            num_scalar_prefetch=3, grid=(N_EXPERTS,),
            in_specs=[pl.BlockSpec(memory_space=pl.ANY),
                      pl.BlockSpec((1, D, D_EXPERT), lambda e, f, n, c: (e, 0, 0)),
                      pl.BlockSpec((1, D, D_EXPERT), lambda e, f, n, c: (e, 0, 0))],
            out_specs=pl.BlockSpec(memory_space=pl.ANY),
            scratch_shapes=[pltpu.VMEM((TM_ROWS + SUBS, LANES), f32), tile_rows, tile_rows,
                            pltpu.VMEM((D, 2 * D_EXPERT), bf16), staging,
                            pltpu.SemaphoreType.DMA(()), pltpu.SemaphoreType.DMA((2,))]),
        out_shape=jax.ShapeDtypeStruct((MOE_ROWS, D_EXPERT), bf16),
        compiler_params=_cp(("arbitrary",)), name="moe_up",
    )(first_tile, n_tiles, codes, h_tm, p['w1'], p['w3'])
    return pl.pallas_call(
        _moe_down_kernel,
        grid_spec=pltpu.PrefetchScalarGridSpec(
            num_scalar_prefetch=4, grid=(N_EXPERTS,),
            in_specs=[pl.BlockSpec(memory_space=pl.ANY),
                      pl.BlockSpec((1, D_EXPERT, D), lambda e, f, n, c, g: (e, 0, 0))],
            out_specs=pl.BlockSpec(memory_space=pl.ANY),
            scratch_shapes=[pltpu.VMEM((TM_ROWS + SUBS, LANES), f32), tile_rows, tile_rows,
                            pltpu.VMEM((D_EXPERT, D), bf16), staging,
                            pltpu.SemaphoreType.DMA((2,)), pltpu.SemaphoreType.DMA(())]),
        out_shape=jax.ShapeDtypeStruct((TM_ROWS, LANES), f32),
        compiler_params=_cp(("arbitrary",)), name="moe_down",
    )(first_tile, n_tiles, codes, gates, hh, p['w2'])


def _final_kernel(x_ref, d_ref, mod_ref, o_ref):
    o_ref[...] = x_ref[...] + mod_ref[5:6, :] * _load_token_major(d_ref, ROW_TILE)


def _final(x, delta, mods, row0, seq):
    n_rows = x.shape[0]
    t0 = row0 // ROW_TILE
    return pl.pallas_call(
        _final_kernel,
        grid=(n_rows // ROW_TILE,),
        in_specs=[pl.BlockSpec((ROW_TILE, D), lambda i: (i, 0)),
                  _token_major_spec(ROW_TILE, lambda i: (t0 + i, 0)),
                  pl.BlockSpec((8, D), lambda i: (_sample_of_tile(t0 + i, ROW_TILE), 0))],
        out_specs=pl.BlockSpec((ROW_TILE, D), lambda i: (i, 0)),
        out_shape=jax.ShapeDtypeStruct((n_rows, D), f32),
        compiler_params=_cp(("parallel",)), name="final",
    )(x, delta, mods).reshape(n_rows // seq, seq, D)


def kernel(x_prompt, x_sample, cache_l0_k, cache_l0_v, cache_l1_ckv, cache_l1_kpe, c, c_ctx, l0_g_norm1, l0_g_norm2, l0_w_ada, l0_b_ada, l0_w_in, l0_g_vnorm, l0_w_s, l0_b_s, l0_g_q, l0_g_k, l0_sink, l0_w_o, l0_w_rg, l0_b_rg, l0_w_re, l0_b_re, l0_w1, l0_w3, l0_w2, l1_g_norm1, l1_g_norm2, l1_w_ada, l1_b_ada, l1_w_in, l1_g_qa, l1_w_uq, l1_g_kva, l1_w_ukv, l1_g_q, l1_g_k, l1_w_o, l1_w_rg, l1_b_rg, l1_w_re, l1_b_re, l1_w1, l1_w3, l1_w2):
    p0 = dict(w_in=l0_w_in, g_vnorm=l0_g_vnorm, w_s=l0_w_s, b_s=l0_b_s, g_q=l0_g_q, g_k=l0_g_k, sink=l0_sink,
              w_o=l0_w_o, w_rg=l0_w_rg, b_rg=l0_b_rg, w_re=l0_w_re, b_re=l0_b_re, w1=l0_w1, w3=l0_w3, w2=l0_w2)
    p1 = dict(w_in=l1_w_in, g_qa=l1_g_qa, w_uq=l1_w_uq, g_kva=l1_g_kva, w_ukv=l1_w_ukv, g_q=l1_g_q, g_k=l1_g_k,
              w_o=l1_w_o, w_rg=l1_w_rg, b_rg=l1_b_rg, w_re=l1_w_re, b_re=l1_b_re, w1=l1_w1, w3=l1_w3, w2=l1_w2)

    cond8 = jnp.zeros((8, D), f32).at[0].set(c_ctx).at[1:1 + DEC_BATCH].set(c)
    mods0 = _mod_rows(_adaln(cond8, l0_w_ada, l0_b_ada))
    mods1 = _mod_rows(_adaln(cond8, l1_w_ada, l1_b_ada))

    xc0 = x_prompt.reshape(T_CTX, D)
    xl0 = x_sample.reshape(T_LAT, D)

    xc0m, xl0m, k_new, v_new = _l0_mixer(xc0, xl0, l0_g_norm1, mods0, p0, cache_l0_k, cache_l0_v)
    h0, route0 = _router(xc0m, xl0m, l0_g_norm2, mods0, p0)
    moe0 = _moe(h0, route0, p0)

    mods01 = mods1.reshape(3, 8, D).at[:, 5].set(mods0.reshape(3, 8, D)[:, 5]).reshape(24, D)
    xc1m, xl1m, ckv_new, kpe_new = _l1_mixer(xc0m, xl0m, moe0, l1_g_norm1, mods01, p1, cache_l1_ckv, cache_l1_kpe)
    h1, route1 = _router(xc1m, xl1m, l1_g_norm2, mods1, p1)
    moe1 = _moe(h1, route1, p1)

    y_prompt = _final(xc1m, moe1, mods1, 0, SEQ)
    y_sample = _final(xl1m, moe1, mods1, T_CTX, DEC_SEQ)
    return (y_prompt, y_sample, k_new, v_new,
            ckv_new.reshape(BATCH, SEQ, C_KV_LORA), kpe_new.reshape(BATCH, SEQ, C_ROPE))
```

```python
import functools

import jax
import jax.numpy as jnp
import numpy as np
from jax import lax
from jax.experimental import pallas as pl
from jax.experimental.pallas import tpu as pltpu

f32 = jnp.float32
bf16 = jnp.bfloat16

D = 1024
BATCH, SEQ = 32, 256
DEC_BATCH, DEC_SEQ = 2, 1024
PAST = 512
T_CTX = BATCH * SEQ
T_LAT = DEC_BATCH * DEC_SEQ
T = T_CTX + T_LAT
GRID_W = 64
CHUNK = 128
WINDOW = 128
ROPE_THETA = 10000.0
EPS = 1e-6
NEG_INF = -1e30
LANES = 128
SUBS = D // LANES

A_WIDTH = 512
A_GROUPS = 4
B_HEADS, B_KV, B_GROUP, B_HD = 8, 2, 4, 64
B_SCALE = B_HD ** -0.5

C_HEADS, C_Q_LORA, C_KV_LORA, C_NOPE, C_ROPE, C_V = 16, 384, 256, 64, 32, 64
C_QK = C_NOPE + C_ROPE
C_SCALE = C_QK ** -0.5
ODD_IN_PAD = 768
SLOT = 128

N_GROUPS, EPG, N_EXPERTS, D_EXPERT = 4, 8, 32, 256

ROW_TILE = 512
BLOCK_ROWS = 1024
PROJ_ROWS = 512
MOE_TILE = 256
MOE_ROWS = 2 * T + N_EXPERTS * MOE_TILE
MOE_TILES = MOE_ROWS // MOE_TILE
VMEM_CAP = 56 * 1024 * 1024


def _cp(sem, vmem=VMEM_CAP):
    return pltpu.CompilerParams(dimension_semantics=sem, vmem_limit_bytes=vmem)


def _const_spec(shape):
    nd = len(shape)
    return pl.BlockSpec(shape, lambda *_: (0,) * nd, pipeline_mode=pl.Buffered(1))


def _sample_of_tile(i, tile):
    n_ctx = T_CTX // tile
    per_lat = DEC_SEQ // tile
    return jnp.where(i < n_ctx, 0, 1 + (i - n_ctx) // per_lat)


def _silu(x):
    return x * jax.nn.sigmoid(x)


def _rms_rows(x, g):
    return x * lax.rsqrt(jnp.mean(x * x, -1, keepdims=True) + EPS) * g


def _swap_pairs(x):
    lane = lax.broadcasted_iota(jnp.int32, x.shape, x.ndim - 1)
    nxt = pltpu.roll(x, x.shape[-1] - 1, x.ndim - 1)
    prv = pltpu.roll(x, 1, x.ndim - 1)
    return jnp.where((lane & 1) == 0, nxt, prv)


def _split_bf16(x):
    hi = x.astype(bf16)
    return hi, (x - hi.astype(f32)).astype(bf16)


def _adaln_kernel(c_ref, w_ref, b_ref, o_ref):
    s_hi, s_lo = _split_bf16(_silu(c_ref[...]))
    w_hi, w_lo = _split_bf16(w_ref[...])
    o_ref[...] = (jnp.dot(s_hi, w_hi, preferred_element_type=f32) + jnp.dot(s_lo, w_hi, preferred_element_type=f32)
                  + jnp.dot(s_hi, w_lo, preferred_element_type=f32) + b_ref[...])


def _adaln(cond8, w, b):
    n = w.shape[1]
    tn = 1536
    return pl.pallas_call(
        _adaln_kernel,
        grid=(n // tn,),
        in_specs=[_const_spec((8, D)), pl.BlockSpec((D, tn), lambda j: (0, j)),
                  pl.BlockSpec((1, tn), lambda j: (0, j))],
        out_specs=pl.BlockSpec((8, tn), lambda j: (0, j)),
        out_shape=jax.ShapeDtypeStruct((8, n), f32),
        compiler_params=_cp(("arbitrary",)),
        name="adaln",
    )(cond8, w, b.reshape(1, n))


def _mod_rows(m8):
    m = m8[:3].reshape(3, 6, D)
    return jnp.pad(m, ((0, 0), (0, 2), (0, 0))).reshape(24, D)


N_CTX_TILES = T_CTX // ROW_TILE


def _token_specs(width):
    return [pl.BlockSpec((ROW_TILE, width), lambda i: (jnp.minimum(i, N_CTX_TILES - 1), 0)),
            pl.BlockSpec((ROW_TILE, width), lambda i: (jnp.maximum(i - N_CTX_TILES, 0), 0))]


def _token_rows(xc_ref, xl_ref):
    return jnp.where(pl.program_id(0) < N_CTX_TILES, xc_ref[...], xl_ref[...])


def _store_token_major(ref, x):
    n = x.shape[0]
    for s in range(SUBS):
        ref[pl.ds(s, n, stride=SUBS), :] = x[:, s * LANES:(s + 1) * LANES]


def _load_token_major(ref, n, row0=0):
    return jnp.concatenate([ref[pl.ds(row0 * SUBS + s, n, stride=SUBS), :] for s in range(SUBS)], axis=1)


def _token_major_spec(rows, index_map):
    return pl.BlockSpec((rows * SUBS, LANES), index_map)


def _rope_tables(n, rot_dim, lanes, lane0, copies=1):
    rows_count = n // GRID_W
    rows = np.repeat(np.arange(rows_count), GRID_W).astype(np.float64)
    cols = np.tile(np.arange(GRID_W), rows_count).astype(np.float64)
    d_axis = rot_dim // 2
    inv = ROPE_THETA ** (-np.arange(0, d_axis, 2, dtype=np.float64) / d_axis)
    ang = np.concatenate([rows[:, None] * inv, cols[:, None] * inv], -1)
    c = np.ones((n, lanes), np.float32)
    s = np.zeros((n, lanes), np.float32)
    for j in range(copies):
        lo = lane0 + j * rot_dim
        c[:, lo:lo + rot_dim] = np.repeat(np.cos(ang), 2, axis=1)
        s[:, lo:lo + rot_dim] = np.repeat(np.sin(ang), 2, axis=1) * np.tile(np.array([-1.0, 1.0]), rot_dim // 2)
    return jnp.asarray(c), jnp.asarray(s)


L0_Q0 = 2 * A_WIDTH
L0_K0 = L0_Q0 + B_HEADS * LANES
L0_V0 = L0_K0 + B_KV * B_HD
L0_IN = L0_V0 + B_KV * B_HD


def _l0_kernel(*refs, latent):
    if latent:
        (sink_ref, x_ref, gn_ref, mod_ref, win_ref, gvn_ref, ws_ref, bsb_ref, gq_ref, gk_ref, wo_ref,
         cos_ref, sin_ref, kc_ref, vc_ref, xo_ref, zs, cat, qs, ks, vt, kcb, vct) = refs
        key_off = WINDOW
    else:
        (sink_ref, x_ref, gn_ref, mod_ref, win_ref, gvn_ref, ws_ref, bsb_ref, gq_ref, gk_ref, wo_ref,
         xo_ref, ko_ref, vo_ref, zs, cat, qs, ks, vt, kf, vf) = refs
        key_off = 0
    n = BLOCK_ROWS
    n_chunks = n // CHUNK
    low = lax.broadcasted_iota(jnp.int32, (CHUNK, LANES), 1) < B_HD

    if latent:
        zpad = jnp.zeros((WINDOW, LANES), bf16)
        for c0 in (0, 1 + n_chunks):
            ks[c0 * CHUNK:(c0 + 1) * CHUNK, :] = zpad
            vt[c0] = zpad
        kcb[...] = kc_ref[...].astype(bf16)
        for i in range(PAST // CHUNK):
            vct[i] = vc_ref[i * CHUNK:(i + 1) * CHUNK, :].T.astype(bf16)

    def project(c, carry):
        r = pl.ds(pl.multiple_of(c * PROJ_ROWS, PROJ_ROWS), PROJ_ROWS)
        h = _rms_rows(x_ref[r, :], gn_ref[...]) * (1.0 + mod_ref[1:2, :]) + mod_ref[0:1, :]
        zs[r, :] = jnp.dot(h.astype(bf16), win_ref[...], preferred_element_type=f32)
        return carry

    lax.fori_loop(0, n // PROJ_ROWS, project, 0)

    def prepare(c, carry):
        r = pl.ds(pl.multiple_of(c * CHUNK, CHUNK), CHUNK)
        u = jax.nn.gelu(zs[r, 0:A_WIDTH])
        v = jax.nn.gelu(zs[r, A_WIDTH:2 * A_WIDTH])
        mu = jnp.mean(v, -1, keepdims=True)
        var = jnp.mean(jnp.square(v - mu), -1, keepdims=True)
        vn = ((v - mu) * lax.rsqrt(var + EPS) * gvn_ref[...]).astype(bf16)
        for g in range(A_GROUPS):
            cs = slice(g * CHUNK, (g + 1) * CHUNK)
            mixed = jnp.dot(ws_ref[g], vn[:, cs], preferred_element_type=f32) + bsb_ref[g]
            cat[r, cs] = (u[:, cs] * mixed).astype(bf16)
        if latent:
            cs_, sn_ = cos_ref[r, :], sin_ref[r, :]
        for h in range(B_HEADS):
            hs = slice(h * LANES, (h + 1) * LANES)
            qh = zs[r, L0_Q0 + h * LANES:L0_Q0 + (h + 1) * LANES]
            qh = qh * lax.rsqrt(jnp.sum(qh * qh, -1, keepdims=True) * (1.0 / B_HD) + EPS) * gq_ref[:, hs]
            if latent:
                qh = qh * cs_ + _swap_pairs(qh) * sn_
            qs[r, hs] = (qh * B_SCALE).astype(bf16)
        k = zs[r, L0_K0:L0_K0 + LANES]
        k2 = k * k
        s0 = jnp.sum(jnp.where(low, k2, 0.0), -1, keepdims=True)
        s1 = jnp.sum(jnp.where(low, 0.0, k2), -1, keepdims=True)
        k = k * lax.rsqrt(jnp.where(low, s0, s1) * (1.0 / B_HD) + EPS) * gk_ref[...]
        vv = zs[r, L0_V0:L0_V0 + LANES]
        if latent:
            k = k * cs_ + _swap_pairs(k) * sn_
        else:
            kf[r, :] = k
            vf[r, :] = vv
        kr = pl.ds(pl.multiple_of(c * CHUNK + key_off, CHUNK), CHUNK)
        ks[kr, :] = k.astype(bf16)
        vt[c + key_off // CHUNK] = vv.T.astype(bf16)
        return carry

    lax.fori_loop(0, n_chunks, prepare, 0)

    def attend(r, rows, key_sets):
        q = jnp.concatenate([qs[r, h * LANES:(h + 1) * LANES] for h in range(B_HEADS)], axis=0)
        sk = jnp.concatenate([jnp.full((1, rows), sink_ref[h], f32) for h in range(B_HEADS)], axis=1)
        scores = []
        m = sk
        for k, _, keep in key_sets:
            s = lax.dot_general(k, q, (((1,), (1,)), ((), ())), preferred_element_type=f32)
            if keep is not None:
                s = jnp.where(keep, s, NEG_INF)
            scores.append(s)
            m = jnp.maximum(m, jnp.max(s, 0, keepdims=True))
        den = jnp.exp(sk - m)
        ot = None
        for s, (_, vts, _) in zip(scores, key_sets):
            e = jnp.exp(s - m)
            den = den + jnp.sum(e, 0, keepdims=True)
            eb = e.astype(bf16)
            for i, v_t in enumerate(vts):
                pv = jnp.dot(v_t, eb[i * CHUNK:(i + 1) * CHUNK, :], preferred_element_type=f32)
                ot = pv if ot is None else ot + pv
        ot = ot * (1.0 / den)
        for pair in range(B_HEADS // 2):
            f0 = (2 * pair // B_GROUP) * B_HD
            pair_t = jnp.concatenate([ot[f0:f0 + B_HD, 2 * pair * rows:(2 * pair + 1) * rows],
                                      ot[f0:f0 + B_HD, (2 * pair + 1) * rows:(2 * pair + 2) * rows]], axis=0)
            cat[r, A_WIDTH + pair * LANES:A_WIDTH + (pair + 1) * LANES] = pair_t.T.astype(bf16)

    if latent:
        span = CHUNK + 2 * WINDOW

        def attend_block(c, carry):
            start = pl.multiple_of(c * CHUNK, CHUNK)
            kr = pl.ds(start, span)
            kj = lax.broadcasted_iota(jnp.int32, (span, B_HEADS * CHUNK), 0)
            qi = lax.broadcasted_iota(jnp.int32, (span, B_HEADS * CHUNK), 1) & (CHUNK - 1)
            kpos = start - WINDOW + kj
            keep = (jnp.abs(kj - WINDOW - qi) <= WINDOW) & (kpos >= 0) & (kpos < n)
            attend(pl.ds(start, CHUNK), CHUNK,
                   [(ks[kr, :], [vt[c + i] for i in range(span // CHUNK)], keep),
                    (kcb[...], [vct[i] for i in range(PAST // CHUNK)], None)])
            return carry

        lax.fori_loop(0, n_chunks, attend_block, 0)
    else:
        def attend_seq(sq, carry):
            r = pl.ds(pl.multiple_of(sq * SEQ, SEQ), SEQ)
            attend(r, SEQ, [(ks[r, :], [vt[sq * (SEQ // CHUNK) + i] for i in range(SEQ // CHUNK)], None)])
            ko_ref[sq] = kf[r, :].T
            vo_ref[sq] = vf[r, :].T
            return carry

        lax.fori_loop(0, n // SEQ, attend_seq, 0)

    def output(c, carry):
        r = pl.ds(pl.multiple_of(c * PROJ_ROWS, PROJ_ROWS), PROJ_ROWS)
        y = jnp.dot(cat[r, :], wo_ref[...], preferred_element_type=f32)
        xo_ref[r, :] = x_ref[r, :] + mod_ref[2:3, :] * y
        return carry

    lax.fori_loop(0, n // PROJ_ROWS, output, 0)


def _l0_mixer(xc, xl, g_norm, mods, p, cache_k, cache_v):
    w = p['w_in']
    q = w[:, 2 * A_WIDTH:2 * A_WIDTH + B_HEADS * B_HD].reshape(D, B_HEADS, B_HD)
    zero = jnp.zeros((D, B_GROUP, B_HD), f32)
    q_slots = jnp.concatenate([jnp.concatenate([q[:, :B_GROUP], zero], axis=2),
                               jnp.concatenate([zero, q[:, B_GROUP:]], axis=2)], axis=1).reshape(D, B_HEADS * LANES)
    win = jnp.concatenate([w[:, :2 * A_WIDTH], q_slots, w[:, 2 * A_WIDTH + B_HEADS * B_HD:]], axis=1).astype(bf16)
    zg = jnp.zeros((B_HD,), f32)
    gq = jnp.concatenate([jnp.tile(jnp.concatenate([p['g_q'], zg]), B_GROUP),
                          jnp.tile(jnp.concatenate([zg, p['g_q']]), B_GROUP)]).reshape(1, B_HEADS * LANES)
    gk = jnp.tile(p['g_k'], B_KV).reshape(1, LANES)
    gvn = p['g_vnorm'].reshape(1, A_WIDTH)
    ws = p['w_s'].astype(bf16)
    bsb = jnp.broadcast_to(p['b_s'][:, :, None], (A_GROUPS, CHUNK, CHUNK))
    wo = p['w_o'].astype(bf16)
    weights = (g_norm.reshape(1, D),)
    consts = (win, gvn, ws, bsb, gq, gk, wo)
    c_specs = [_const_spec(a.shape) for a in consts]
    smem = pl.BlockSpec(memory_space=pltpu.SMEM)
    row = pl.BlockSpec((BLOCK_ROWS, D), lambda b: (b, 0))
    kv = pl.BlockSpec((BLOCK_ROWS // SEQ, LANES, SEQ), lambda b: (b, 0, 0))

    def scratch(pad):
        return [pltpu.VMEM((BLOCK_ROWS, L0_IN), f32), pltpu.VMEM((BLOCK_ROWS, D), bf16),
                pltpu.VMEM((BLOCK_ROWS, B_HEADS * LANES), bf16), pltpu.VMEM((BLOCK_ROWS + pad, LANES), bf16),
                pltpu.VMEM(((BLOCK_ROWS + pad) // CHUNK, LANES, CHUNK), bf16)]

    kv_shape = jax.ShapeDtypeStruct((BATCH, LANES, SEQ), f32)
    state = pltpu.VMEM((BLOCK_ROWS, LANES), f32)
    xo_ctx, k_t, v_t = pl.pallas_call(
        functools.partial(_l0_kernel, latent=False),
        grid=(T_CTX // BLOCK_ROWS,),
        in_specs=[smem, row, _const_spec((1, D)), pl.BlockSpec((8, D), lambda b: (0, 0))] + c_specs,
        out_specs=[row, kv, kv],
        out_shape=[jax.ShapeDtypeStruct((T_CTX, D), f32), kv_shape, kv_shape],
        scratch_shapes=scratch(0) + [state, state],
        compiler_params=_cp(("parallel",)), name="l0_mixer_ctx",
    )(p['sink'], xc, *weights, mods, *consts)
    k_new = k_t.reshape(BATCH, B_KV, B_HD, SEQ).transpose(0, 3, 1, 2)
    v_new = v_t.reshape(BATCH, B_KV, B_HD, SEQ).transpose(0, 3, 1, 2)

    cos, sin = _rope_tables(DEC_SEQ, B_HD, LANES, 0, copies=LANES // B_HD)
    cache = pl.BlockSpec((None, PAST, LANES), lambda b: (b, 0, 0))
    past = [pltpu.VMEM((PAST, LANES), bf16), pltpu.VMEM((PAST // CHUNK, LANES, CHUNK), bf16)]
    xo_lat = pl.pallas_call(
        functools.partial(_l0_kernel, latent=True),
        grid=(DEC_BATCH,),
        in_specs=[smem, row, _const_spec((1, D)), pl.BlockSpec((8, D), lambda b: (1 + b, 0))] + c_specs + [
                  _const_spec(cos.shape), _const_spec(sin.shape), cache, cache],
        out_specs=row,
        out_shape=jax.ShapeDtypeStruct((T_LAT, D), f32),
        scratch_shapes=scratch(2 * WINDOW) + past,
        compiler_params=_cp(("parallel",)), name="l0_mixer_lat",
    )(p['sink'], xl, *weights, mods, *consts, cos, sin,
      cache_k.reshape(DEC_BATCH, PAST, LANES), cache_v.reshape(DEC_BATCH, PAST, LANES))
    return xo_ctx, xo_lat, k_new, v_new


C_SLOTS = C_HEADS * SLOT
C_PAIRS = C_HEADS // 2
L1_ROWS = 256


def _l1_kernel(*refs, latent):
    if latent:
        (x_ref, d_ref, gn_ref, mod_ref, win_ref, gqa_ref, wuq_ref, gq_ref, gkva_ref, wuk_ref, wuvt_ref, gk_ref,
         wo_ref, wuqs_ref, qcos_ref, qsin_ref, kcos_ref, ksin_ref, cckv_ref, ckpe_ref, xo_ref,
         zs, cat, qs, ks, vt, wide, wide2) = refs
        n_ctx = PAST
    else:
        (x_ref, d_ref, gn_ref, mod_ref, win_ref, gqa_ref, wuq_ref, gq_ref, gkva_ref, wuk_ref, wuvt_ref, gk_ref,
         wo_ref, xo_ref, ckvo_ref, kpeo_ref, zs, cat, qs, ks, vt, wide) = refs
        n_ctx = 0
    n = BLOCK_ROWS
    nt_dims = (((1,), (1,)), ((), ()))

    def inv_rms(v):
        return lax.rsqrt(jnp.sum(v * v, -1, keepdims=True) * (1.0 / C_QK) + EPS)

    def expand_keys(ckv_n, kslot, kb, rope_rows):
        cb = ckv_n.astype(bf16)
        key_rows = pl.ds(pl.multiple_of(kb * L1_ROWS, L1_ROWS), L1_ROWS)
        wide[...] = jnp.dot(cb, wuk_ref[...], preferred_element_type=f32)
        if rope_rows is not None:
            kcos = kcos_ref[rope_rows, :]
            turned = _swap_pairs(kslot) * ksin_ref[rope_rows, :]
        for h in range(C_HEADS):
            kh = wide[:, h * SLOT:(h + 1) * SLOT] + kslot
            if rope_rows is not None:
                kh = inv_rms(kh) * (kh * kcos + turned)
            else:
                kh = kh * inv_rms(kh) * gk_ref[...]
            ks[h, key_rows, :] = kh.astype(bf16)
        v_t = lax.dot_general(wuvt_ref[...], cb, nt_dims, preferred_element_type=f32).astype(bf16)
        for pair in range(C_PAIRS):
            vt[pair, kb] = v_t[pair * LANES:(pair + 1) * LANES, :]

    if latent:
        def past_keys(c, carry):
            r = pl.ds(pl.multiple_of(c * L1_ROWS, L1_ROWS), L1_ROWS)
            expand_keys(cckv_ref[r, :], ckpe_ref[r, :], c, None)
            return carry

        lax.fori_loop(0, PAST // L1_ROWS, past_keys, 0)

    for c in range(n // PROJ_ROWS):
        r = pl.ds(c * PROJ_ROWS, PROJ_ROWS)
        x1 = x_ref[r, :] + mod_ref[5:6, :] * _load_token_major(d_ref, PROJ_ROWS, c * PROJ_ROWS)
        xo_ref[r, :] = x1
        h = _rms_rows(x1, gn_ref[...]) * (1.0 + mod_ref[1:2, :]) + mod_ref[0:1, :]
        zs[r, :] = jnp.dot(h.astype(bf16), win_ref[...], preferred_element_type=f32)

    def prepare(c, carry):
        r = pl.ds(pl.multiple_of(c * L1_ROWS, L1_ROWS), L1_ROWS)
        qa = _rms_rows(zs[r, 0:C_Q_LORA], gqa_ref[...]).astype(bf16)
        wide[...] = jnp.dot(qa, wuq_ref[...], preferred_element_type=f32)
        if latent:
            wide2[...] = jnp.dot(qa, wuqs_ref[...], preferred_element_type=f32)
            qcos, qsin = qcos_ref[r, :], qsin_ref[r, :]
        for h in range(C_HEADS):
            hs = slice(h * SLOT, (h + 1) * SLOT)
            qh = wide[:, hs]
            if latent:
                qh = inv_rms(qh) * (qh * qcos + wide2[:, hs] * qsin)
            else:
                qh = qh * inv_rms(qh) * gq_ref[...]
            qs[h, r, :] = (qh * C_SCALE).astype(bf16)
        ckv_n = _rms_rows(zs[r, C_Q_LORA:C_Q_LORA + C_KV_LORA], gkva_ref[...])
        kslot = zs[r, C_Q_LORA + C_KV_LORA:ODD_IN_PAD]
        if not latent:
            ckvo_ref[r, :] = ckv_n
            kpeo_ref[r, :] = kslot
        expand_keys(ckv_n, kslot, c + n_ctx // L1_ROWS, r if latent else None)
        return carry

    lax.fori_loop(0, n // L1_ROWS, prepare, 0)

    low = lax.broadcasted_iota(jnp.int32, (2 * C_V, L1_ROWS), 0) < C_V
    n_kblocks = (n_ctx + n) // L1_ROWS
    pairs_per_step = 2 if latent else 4

    def attend(c, carry):
        r = pl.ds(pl.multiple_of(c * L1_ROWS, L1_ROWS), L1_ROWS)

        def values_t(pair, eb):
            if not latent:
                return jnp.dot(vt[pair, c], eb, preferred_element_type=f32)
            o_t = None
            for b in range(n_kblocks):
                pv = jnp.dot(vt[pair, b], eb[b * L1_ROWS:(b + 1) * L1_ROWS, :], preferred_element_type=f32)
                o_t = pv if o_t is None else o_t + pv
            return o_t

        def pairs_step(i, carry2):
            pairs = [i * pairs_per_step + j for j in range(pairs_per_step)]
            heads = [2 * p + hh for p in pairs for hh in range(2)]
            scores = [lax.dot_general(ks[h] if latent else ks[h, r, :], qs[h, r, :], nt_dims,
                                      preferred_element_type=f32) for h in heads]
            exps = [jnp.exp(s - jnp.max(s, 0, keepdims=True)) for s in scores]
            dens = [jnp.sum(e, 0, keepdims=True) for e in exps]
            outs = [values_t(h // 2, e.astype(bf16)) / den for h, e, den in zip(heads, exps, dens)]
            for j, pair in enumerate(pairs):
                cat[pair, r, :] = jnp.where(low, outs[2 * j], outs[2 * j + 1]).T.astype(bf16)
            return carry2

        return lax.fori_loop(0, C_PAIRS // pairs_per_step, pairs_step, carry)

    lax.fori_loop(0, n // L1_ROWS, attend, 0)

    def output(c, carry):
        r = pl.ds(pl.multiple_of(c * PROJ_ROWS, PROJ_ROWS), PROJ_ROWS)
        heads = jnp.concatenate([cat[pair, r, :] for pair in range(C_PAIRS)], axis=1)
        y = jnp.dot(heads, wo_ref[...], preferred_element_type=f32)
        xo_ref[r, :] = xo_ref[r, :] + mod_ref[2:3, :] * y
        return carry

    lax.fori_loop(0, n // PROJ_ROWS, output, 0)


def _slot_cols(w, heads, width, lo, hi, lane0):
    k = w.shape[0]
    w3 = w.reshape(k, heads, width)[:, :, lo:hi]
    out = jnp.zeros((k, heads, SLOT), w.dtype).at[:, :, lane0:lane0 + (hi - lo)].set(w3)
    return out.reshape(k, heads * SLOT)


def _l1_mixer(xc, xl, moe_prev, g_norm, mods, p, cache_ckv, cache_kpe):
    w_in = jnp.zeros((D, ODD_IN_PAD), f32).at[:, :C_Q_LORA + C_KV_LORA].set(
        p['w_in'][:, :C_Q_LORA + C_KV_LORA]).at[
        :, C_Q_LORA + C_KV_LORA + C_NOPE:C_Q_LORA + C_KV_LORA + C_QK].set(p['w_in'][:, C_Q_LORA + C_KV_LORA:])
    wuq = _slot_cols(p['w_uq'], C_HEADS, C_QK, 0, C_QK, 0).astype(bf16)
    wuk = _slot_cols(p['w_ukv'], C_HEADS, C_NOPE + C_V, 0, C_NOPE, 0).astype(bf16)
    wuv_t = p['w_ukv'].reshape(C_KV_LORA, C_HEADS, C_NOPE + C_V)[:, :, C_NOPE:].reshape(
        C_KV_LORA, C_HEADS * C_V).T.astype(bf16)
    gq = jnp.zeros((1, SLOT), f32).at[0, :C_QK].set(p['g_q'])
    gk = jnp.zeros((1, SLOT), f32).at[0, :C_QK].set(p['g_k'])
    consts = (g_norm.reshape(1, D), w_in.astype(bf16), p['g_qa'].reshape(1, C_Q_LORA), wuq, gq,
              p['g_kva'].reshape(1, C_KV_LORA), wuk, wuv_t, gk, p['w_o'].astype(bf16))
    c_specs = [_const_spec(a.shape) for a in consts]
    row = pl.BlockSpec((BLOCK_ROWS, D), lambda b: (b, 0))
    n_ctx_blocks = T_CTX // BLOCK_ROWS

    def scratch(n_keys):
        return [pltpu.VMEM((BLOCK_ROWS, ODD_IN_PAD), f32), pltpu.VMEM((C_PAIRS, BLOCK_ROWS, LANES), bf16),
                pltpu.VMEM((C_HEADS, BLOCK_ROWS, SLOT), bf16), pltpu.VMEM((C_HEADS, n_keys, SLOT), bf16),
                pltpu.VMEM((C_PAIRS, n_keys // L1_ROWS, LANES, L1_ROWS), bf16),
                pltpu.VMEM((L1_ROWS, C_SLOTS), f32)]

    xo_ctx, ckv_new, kpe_slot = pl.pallas_call(
        functools.partial(_l1_kernel, latent=False),
        grid=(n_ctx_blocks,),
        in_specs=[row, _token_major_spec(BLOCK_ROWS, lambda b: (b, 0)), c_specs[0],
                  pl.BlockSpec((8, D), lambda b: (0, 0))] + c_specs[1:],
        out_specs=[row, pl.BlockSpec((BLOCK_ROWS, C_KV_LORA), lambda b: (b, 0)),
                   pl.BlockSpec((BLOCK_ROWS, SLOT), lambda b: (b, 0))],
        out_shape=[jax.ShapeDtypeStruct((T_CTX, D), f32), jax.ShapeDtypeStruct((T_CTX, C_KV_LORA), f32),
                   jax.ShapeDtypeStruct((T_CTX, SLOT), f32)],
        scratch_shapes=scratch(BLOCK_ROWS),
        compiler_params=_cp(("parallel",)), name="l1_mixer_ctx",
    )(xc, moe_prev, consts[0], mods, *consts[1:])

    cos, sin = _rope_tables(DEC_SEQ, C_ROPE, SLOT, C_NOPE)
    pair_swap = lambda a: a.reshape(a.shape[:-1] + (a.shape[-1] // 2, 2))[..., ::-1].reshape(a.shape)
    rope = (pair_swap(wuq), gq * cos, pair_swap(gq) * sin, gk * cos, pair_swap(gk) * sin)
    ckpe = jnp.zeros((DEC_BATCH, PAST, SLOT), f32).at[:, :, C_NOPE:C_QK].set(cache_kpe)
    xo_lat = pl.pallas_call(
        functools.partial(_l1_kernel, latent=True),
        grid=(DEC_BATCH,),
        in_specs=[pl.BlockSpec((BLOCK_ROWS, D), lambda b: (b, 0), pipeline_mode=pl.Buffered(1)),
                  pl.BlockSpec((BLOCK_ROWS * SUBS, LANES), lambda b: (n_ctx_blocks + b, 0),
                               pipeline_mode=pl.Buffered(1)), c_specs[0],
                  pl.BlockSpec((8, D), lambda b: (1 + b, 0))] + c_specs[1:] + [_const_spec(a.shape) for a in rope] + [
                  pl.BlockSpec((None, PAST, C_KV_LORA), lambda b: (b, 0, 0)),
                  pl.BlockSpec((None, PAST, SLOT), lambda b: (b, 0, 0))],
        out_specs=row,
        out_shape=jax.ShapeDtypeStruct((T_LAT, D), f32),
        scratch_shapes=scratch(PAST + BLOCK_ROWS) + [pltpu.VMEM((L1_ROWS, C_SLOTS), f32)],
        compiler_params=_cp(("parallel",)), name="l1_mixer_lat",
    )(xl, moe_prev, consts[0], mods, *consts[1:], *rope, cache_ckv, ckpe)
    return xo_ctx, xo_lat, ckv_new, kpe_slot[:, C_NOPE:C_QK]


ROUTER_ROWS = 40


def _router_kernel(xc_ref, xl_ref, gn_ref, mod_ref, whi_ref, wlo_ref, br_ref, h_ref, route_ref):
    h = _rms_rows(_token_rows(xc_ref, xl_ref), gn_ref[...]) * (1.0 + mod_ref[4:5, :]) + mod_ref[3:4, :]
    _store_token_major(h_ref, h)
    h_hi, h_lo = _split_bf16(h)
    nt = (((1,), (1,)), ((), ()))
    logits = (lax.dot_general(whi_ref[...], h_hi, nt, preferred_element_type=f32)
              + lax.dot_general(whi_ref[...], h_lo, nt, preferred_element_type=f32)
              + lax.dot_general(wlo_ref[...], h_hi, nt, preferred_element_type=f32))
    logits = logits[0:ROUTER_ROWS, :] + br_ref[0:ROUTER_ROWS, :]
    row_i = lax.broadcasted_iota(jnp.int32, logits.shape, 0)
    row = row_i.astype(f32)
    big = 1e6
    is_g = (row_i >= N_EXPERTS) & (row_i < N_EXPERTS + N_GROUPS)
    lg = jnp.where(is_g, logits, -jnp.inf)
    mg = jnp.max(lg, 0, keepdims=True)
    gsel = jnp.min(jnp.where(lg == mg, row, big), 0, keepdims=True) - N_EXPERTS
    pg_sel = 1.0 / jnp.sum(jnp.where(is_g, jnp.exp(lg - mg), 0.0), 0, keepdims=True)
    in_grp = (row_i < N_EXPERTS) & ((row_i >> 3).astype(f32) == gsel)
    le = jnp.where(in_grp, logits, -jnp.inf)
    m1 = jnp.max(le, 0, keepdims=True)
    i1 = jnp.min(jnp.where(le == m1, row, big), 0, keepdims=True)
    le2 = jnp.where(row == i1, -jnp.inf, le)
    m2 = jnp.max(le2, 0, keepdims=True)
    i2 = jnp.min(jnp.where(le2 == m2, row, big), 0, keepdims=True)
    e2 = jnp.exp(m2 - m1)
    w1 = pg_sel / (1.0 + e2)
    w2 = pg_sel * e2 / (1.0 + e2)
    sub = lax.broadcasted_iota(jnp.int32, route_ref.shape, 0)
    route_ref[...] = jnp.where(sub == 0, i1, jnp.where(sub == 1, i2, jnp.where(sub == 2, w1,
                                                                                jnp.where(sub == 3, w2, 0.0))))


def _router(xc, xl, g_norm, mods, p):
    wr = jnp.zeros((128, D), f32).at[:N_EXPERTS].set(p['w_re'].T).at[
        N_EXPERTS:N_EXPERTS + N_GROUPS].set(p['w_rg'].T)
    w_hi, w_lo = _split_bf16(wr)
    br =jnp.zeros((128, 1), f32).at[:N_EXPERTS, 0].set(p['b_re']).at[
        N_EXPERTS:N_EXPERTS + N_GROUPS, 0].set(p['b_rg'])
    return pl.pallas_call(
        _router_kernel,
        grid=(T // ROW_TILE,),
        in_specs=_token_specs(D) + [
                  _const_spec((1, D)),
                  pl.BlockSpec((8, D), lambda i: (_sample_of_tile(i, ROW_TILE), 0)),
                  _const_spec((128, D)), _const_spec((128, D)), _const_spec((128, 1))],
        out_specs=[_token_major_spec(ROW_TILE, lambda i: (i, 0)), pl.BlockSpec((8, ROW_TILE), lambda i: (0, i))],
        out_shape=[jax.ShapeDtypeStruct((T * SUBS, LANES), f32), jax.ShapeDtypeStruct((8, T), f32)],
        compiler_params=_cp(("parallel",)), name="router",
    )(xc, xl, g_norm.reshape(1, D), mods, w_hi, w_lo, br)


PLAN_FIRST_TILE, PLAN_TILES, PLAN_N_USED = 0, 1, 2


def _plan_kernel(rt_ref, pos_ref, plan_ref, rank):
    n_blk = T // 128
    e_col = lax.broadcasted_iota(jnp.int32, (N_EXPERTS, 128), 0).astype(f32)
    ri = lax.broadcasted_iota(jnp.int32, (128, 128), 0)
    ci = lax.broadcasted_iota(jnp.int32, (128, 128), 1)
    before = jnp.where(ri < ci, 1.0, 0.0).astype(bf16)

    def picks(b):
        cs = slice(b * 128, (b + 1) * 128)
        return rt_ref[0:1, cs] == e_col, rt_ref[1:2, cs] == e_col

    counts = jnp.zeros((N_EXPERTS, 1), f32)
    for b in range(n_blk):
        m0, m1 = picks(b)
        m = jnp.where(m0, 1.0, 0.0) + jnp.where(m1, 1.0, 0.0)
        rank[:, b * 128:(b + 1) * 128] = jnp.dot(m.astype(bf16), before, preferred_element_type=f32) + counts
        counts = counts + jnp.sum(m, axis=1, keepdims=True)

    tiles = jnp.floor((counts + (MOE_TILE - 1.0)) * (1.0 / MOE_TILE))
    er = lax.broadcasted_iota(jnp.int32, (N_EXPERTS, N_EXPERTS), 0)
    ec = lax.broadcasted_iota(jnp.int32, (N_EXPERTS, N_EXPERTS), 1)
    earlier = jnp.where(ec < er, 1.0, 0.0).astype(bf16)
    tile_start = jnp.dot(earlier, jnp.broadcast_to(tiles, (N_EXPERTS, 128)).astype(bf16),
                         preferred_element_type=f32)
    row_start = tile_start * MOE_TILE

    sub = lax.broadcasted_iota(jnp.int32, (8, 128), 0)
    for b in range(n_blk):
        m0, m1 = picks(b)
        base = rank[:, b * 128:(b + 1) * 128] + row_start
        p0 = jnp.sum(jnp.where(m0, base, 0.0), axis=0, keepdims=True)
        p1 = jnp.sum(jnp.where(m1, base, 0.0), axis=0, keepdims=True)
        pos_ref[:, b * 128:(b + 1) * 128] = jnp.where(sub == 0, p0, jnp.where(sub == 1, p1, 0.0)).astype(jnp.int32)

    tile_end = tile_start + tiles
    n_used = jnp.max(tile_end, axis=0, keepdims=True)
    diag =(lax.broadcasted_iota(jnp.int32, (N_EXPERTS, 128), 0)
            == lax.broadcasted_iota(jnp.int32, (N_EXPERTS, 128), 1))
    first = jnp.sum(jnp.where(diag, tile_start, 0.0), axis=0, keepdims=True)
    count = jnp.sum(jnp.where(diag, tiles, 0.0), axis=0, keepdims=True)
    rows = jnp.where(sub == PLAN_FIRST_TILE, first,
                     jnp.where(sub == PLAN_TILES, count, jnp.where(sub == PLAN_N_USED, n_used, 0.0)))
    plan_ref[...] = rows.astype(jnp.int32)


def _slot_code(t, k):
    return t * SUBS + k * (SUBS // 2)


def _code_offset(code):
    return pl.multiple_of(code & ~(SUBS - 1), SUBS)


def _code_gate_index(code):
    return code >> 2


PAD_CODE = T * SUBS


def _invert_kernel(pos_ref, plan_ref, code_ref):
    def pad_tile(tile, carry):
        for u in range(MOE_TILE):
            code_ref[tile * MOE_TILE + u] = PAD_CODE
        return carry

    def pad_last_tile(e, carry):
        return pad_tile(jnp.maximum(plan_ref[PLAN_FIRST_TILE, e] + plan_ref[PLAN_TILES, e] - 1, 0), carry)
    lax.fori_loop(0, N_EXPERTS, pad_last_tile, 0)
    lax.fori_loop(plan_ref[PLAN_N_USED, 0], MOE_TILES, pad_tile, 0)

    group = 32
    for k in range(2):
        def place(i, carry):
            t0 = i * group
            slots = [pos_ref[k * T + t0 + u] for u in range(group)]
            for u, s in enumerate(slots):
                code_ref[s] = _slot_code(t0 + u, k)
            return carry
        lax.fori_loop(0, T // group, place, 0)


def _route_plan(route_t):
    pos, plan = pl.pallas_call(
        _plan_kernel,
        out_shape=[jax.ShapeDtypeStruct((8, T), jnp.int32), jax.ShapeDtypeStruct((8, LANES), jnp.int32)],
        scratch_shapes=[pltpu.VMEM((N_EXPERTS, T), f32)],
        compiler_params=_cp(None), name="route_plan",
    )(route_t)
    smem = pl.BlockSpec(memory_space=pltpu.SMEM)
    codes = pl.pallas_call(
        _invert_kernel,
        in_specs=[smem, smem], out_specs=smem,
        out_shape=jax.ShapeDtypeStruct((MOE_ROWS,), jnp.int32),
        name="route_invert",
    )(pos[0:2].reshape(2 * T), plan)
    gates = jnp.pad(route_t[2:4].T.reshape(2 * T), (0, 8))
    return plan[PLAN_FIRST_TILE, :N_EXPERTS], plan[PLAN_TILES, :N_EXPERTS], codes, gates


def _tile_index(i):
    return jnp.minimum(i, MOE_TILES - 1)


TM_ROWS = T * SUBS
SCATTER_GROUP = 16
STAGES = 4
LAST_EXPERT = N_EXPERTS - 1


def _tile_rows(g):
    return pl.ds(pl.multiple_of(g * MOE_TILE, MOE_TILE), MOE_TILE)


def _expert_tile_pairs(first, count, tile_step):
    def pair(pp, carry):
        for parity in range(2):
            g = 2 * pp + parity
            pl.when((g >= first) & (g < first + count))(functools.partial(tile_step, g, parity))
        return carry
    lax.fori_loop(first // 2, (first + count + 1) // 2, pair, 0)


def _gather_tile(code_ref, tile, xs, gbuf):
    base = tile * MOE_TILE
    for r in range(MOE_TILE):
        gbuf[r * SUBS:(r + 1) * SUBS, :] = xs[pl.ds(_code_offset(code_ref[base + r]), SUBS), :]


def _moe_up_kernel(first_ref, count_ref, code_ref, h_hbm, w1_ref, w3_ref, hh_hbm,
                   xs, gbuf_a, gbuf_b, w13, obuf, sem_x, sem_o):
    e = pl.program_id(0)
    first, count = first_ref[e], count_ref[e]
    gbufs = (gbuf_a, gbuf_b)

    def out_copy(slot, g):
        return pltpu.make_async_copy(obuf.at[slot], hh_hbm.at[_tile_rows(g), :], sem_o.at[slot])

    @pl.when(e == 0)
    def _():
        cp = pltpu.make_async_copy(h_hbm, xs.at[pl.ds(0, TM_ROWS), :], sem_x)
        cp.start()
        xs[TM_ROWS:TM_ROWS + SUBS, :] = jnp.zeros((SUBS, LANES), f32)
        cp.wait()
        _gather_tile(code_ref, 0, xs, gbuf_a)

    @pl.when(count > 0)
    def _():
        w13[:, :D_EXPERT] = w1_ref[0].astype(bf16)
        w13[:, D_EXPERT:] = w3_ref[0].astype(bf16)

    def tile_step(g, parity):
        _gather_tile(code_ref, _tile_index(g + 1), xs, gbufs[1 - parity])
        x3 = jnp.swapaxes(gbufs[parity][...].reshape(MOE_TILE, SUBS, LANES), 0, 1)
        x = jnp.concatenate([x3[s] for s in range(SUBS)], axis=1).astype(bf16)
        h13 = jnp.dot(x, w13[...], preferred_element_type=f32)
        hh = (_silu(h13[:, :D_EXPERT]) * h13[:, D_EXPERT:]).astype(bf16)

        slot = g % STAGES

        @pl.when(g >= STAGES)
        def _():
            out_copy(slot, g).wait()
        obuf[slot] = hh
        out_copy(slot, g).start()

    _expert_tile_pairs(first, count, tile_step)

    @pl.when(e == LAST_EXPERT)
    def _():
        n_used = first + count
        for slot in range(STAGES):
            pl.when(n_used > slot)(lambda slot=slot: out_copy(slot, 0).wait())
        obuf[0] = jnp.zeros((MOE_TILE, D_EXPERT), bf16)

        def zero_tile(g, carry):
            cp = out_copy(0, g)
            cp.start()
            cp.wait()
            return carry
        lax.fori_loop(n_used, MOE_TILES, zero_tile, 0)


def _scatter_tile(code_ref, gate_ref, tile, ybuf, acc):
    base = tile * MOE_TILE
    for g0 in range(0, MOE_TILE, SCATTER_GROUP):
        rows = range(g0, g0 + SCATTER_GROUP)
        codes = [code_ref[base + r] for r in rows]
        new = [acc[pl.ds(_code_offset(c), SUBS), :]
               + gate_ref[_code_gate_index(c)] * ybuf[r * SUBS:(r + 1) * SUBS, :]
               for r, c in zip(rows, codes)]
        for c, v in zip(codes, new):
            acc[pl.ds(_code_offset(c), SUBS), :] = v


def _moe_down_kernel(first_ref, count_ref, code_ref, gate_ref, hh_hbm, w2_ref, o_hbm,
                     acc, ybuf_a, ybuf_b, w2b, ibuf, sem_i, sem_o):
    e = pl.program_id(0)
    first, count = first_ref[e], count_ref[e]
    n_used = first_ref[LAST_EXPERT] + count_ref[LAST_EXPERT]
    ybufs = (ybuf_a, ybuf_b)

    def in_copy(slot, g):
        return pltpu.make_async_copy(hh_hbm.at[_tile_rows(g), :], ibuf.at[slot], sem_i.at[slot])

    @pl.when(e == 0)
    def _():
        for g in range(STAGES - 1):
            in_copy(g, g).start()

        def zero(c, carry):
            acc[pl.ds(pl.multiple_of(c * 1024, 1024), 1024), :] = jnp.zeros((1024, LANES), f32)
            return carry
        lax.fori_loop(0, TM_ROWS // 1024, zero, 0)
        acc[TM_ROWS:TM_ROWS + SUBS, :] = jnp.zeros((SUBS, LANES), f32)
        ybuf_b[...] = jnp.zeros_like(ybuf_b)

    @pl.when(count > 0)
    def _():
        w2b[...] = w2_ref[0].astype(bf16)

    def tile_step(g, parity):
        slot = g % STAGES
        in_copy(slot, g).wait()
        ahead = g + STAGES - 1

        @pl.when(ahead < n_used)
        def _():
            in_copy(ahead % STAGES, ahead).start()
        _store_token_major(ybufs[parity], jnp.dot(ibuf[slot], w2b[...], preferred_element_type=f32))
        _scatter_tile(code_ref, gate_ref, jnp.maximum(g - 1, 0), ybufs[1 - parity], acc)

    _expert_tile_pairs(first, count, tile_step)

    @pl.when(e == LAST_EXPERT)
    def _():
        for parity in range(2):
            pl.when((n_used > 0) & ((n_used - 1) % 2 == parity))(
                functools.partial(_scatter_tile, code_ref, gate_ref, n_used - 1, ybufs[parity], acc))
        cp = pltpu.make_async_copy(acc.at[pl.ds(0, TM_ROWS), :], o_hbm, sem_o)
        cp.start()
        cp.wait()


def _moe(h_tm, route_t, p):
    first_tile, n_tiles, codes, gates = _route_plan(route_t)
    tile_rows = pltpu.VMEM((MOE_TILE * SUBS, LANES), f32)
    staging = pltpu.VMEM((STAGES, MOE_TILE, D_EXPERT), bf16)
    hh = pl.pallas_call(
        _moe_up_kernel,
        grid_spec=pltpu.PrefetchScalarGridSpec(
            num_scalar_prefetch=3, grid=(N_EXPERTS,),
            in_specs=[pl.BlockSpec(memory_space=pl.ANY),
                      pl.BlockSpec((1, D, D_EXPERT), lambda e, f, n, c: (e, 0, 0)),
                      pl.BlockSpec((1, D, D_EXPERT), lambda e, f, n, c: (e, 0, 0))],
            out_specs=pl.BlockSpec(memory_space=pl.ANY),
            scratch_shapes=[pltpu.VMEM((TM_ROWS + SUBS, LANES), f32), tile_rows, tile_rows,
                            pltpu.VMEM((D, 2 * D_EXPERT), bf16), staging,
                            pltpu.SemaphoreType.DMA(()), pltpu.SemaphoreType.DMA((STAGES,))]),
        out_shape=jax.ShapeDtypeStruct((MOE_ROWS, D_EXPERT), bf16),
        compiler_params=_cp(("arbitrary",)), name="moe_up",
    )(first_tile, n_tiles, codes, h_tm, p['w1'], p['w3'])
    return pl.pallas_call(
        _moe_down_kernel,
        grid_spec=pltpu.PrefetchScalarGridSpec(
            num_scalar_prefetch=4, grid=(N_EXPERTS,),
            in_specs=[pl.BlockSpec(memory_space=pl.ANY),
                      pl.BlockSpec((1, D_EXPERT, D), lambda e, f, n, c, g: (e, 0, 0))],
            out_specs=pl.BlockSpec(memory_space=pl.ANY),
            scratch_shapes=[pltpu.VMEM((TM_ROWS + SUBS, LANES), f32), tile_rows, tile_rows,
                            pltpu.VMEM((D_EXPERT, D), bf16), staging,
                            pltpu.SemaphoreType.DMA((STAGES,)), pltpu.SemaphoreType.DMA(())]),
        out_shape=jax.ShapeDtypeStruct((TM_ROWS, LANES), f32),
        compiler_params=_cp(("arbitrary",)), name="moe_down",
    )(first_tile, n_tiles, codes, gates, hh, p['w2'])


def _final_kernel(x_ref, d_ref, mod_ref, o_ref):
    o_ref[...] = x_ref[...] + mod_ref[5:6, :] * _load_token_major(d_ref, ROW_TILE)


def _final(x, delta, mods, row0, seq):
    n_rows = x.shape[0]
    t0 = row0 // ROW_TILE
    return pl.pallas_call(
        _final_kernel,
        grid=(n_rows // ROW_TILE,),
        in_specs=[pl.BlockSpec((ROW_TILE, D), lambda i: (i, 0)),
                  _token_major_spec(ROW_TILE, lambda i: (t0 + i, 0)),
                  pl.BlockSpec((8, D), lambda i: (_sample_of_tile(t0 + i, ROW_TILE), 0))],
        out_specs=pl.BlockSpec((ROW_TILE, D), lambda i: (i, 0)),
        out_shape=jax.ShapeDtypeStruct((n_rows, D), f32),
        compiler_params=_cp(("parallel",)), name="final",
    )(x, delta, mods).reshape(n_rows // seq, seq, D)


def kernel(x_prompt, x_sample, cache_l0_k, cache_l0_v, cache_l1_ckv, cache_l1_kpe, c, c_ctx, l0_g_norm1, l0_g_norm2, l0_w_ada, l0_b_ada, l0_w_in, l0_g_vnorm, l0_w_s, l0_b_s, l0_g_q, l0_g_k, l0_sink, l0_w_o, l0_w_rg, l0_b_rg, l0_w_re, l0_b_re, l0_w1, l0_w3, l0_w2, l1_g_norm1, l1_g_norm2, l1_w_ada, l1_b_ada, l1_w_in, l1_g_qa, l1_w_uq, l1_g_kva, l1_w_ukv, l1_g_q, l1_g_k, l1_w_o, l1_w_rg, l1_b_rg, l1_w_re, l1_b_re, l1_w1, l1_w3, l1_w2):
    p0 = dict(w_in=l0_w_in, g_vnorm=l0_g_vnorm, w_s=l0_w_s, b_s=l0_b_s, g_q=l0_g_q, g_k=l0_g_k, sink=l0_sink,
              w_o=l0_w_o, w_rg=l0_w_rg, b_rg=l0_b_rg, w_re=l0_w_re, b_re=l0_b_re, w1=l0_w1, w3=l0_w3, w2=l0_w2)
    p1 = dict(w_in=l1_w_in, g_qa=l1_g_qa, w_uq=l1_w_uq, g_kva=l1_g_kva, w_ukv=l1_w_ukv, g_q=l1_g_q, g_k=l1_g_k,
              w_o=l1_w_o, w_rg=l1_w_rg, b_rg=l1_b_rg, w_re=l1_w_re, b_re=l1_b_re, w1=l1_w1, w3=l1_w3, w2=l1_w2)

    cond8 = jnp.zeros((8, D), f32).at[0].set(c_ctx).at[1:1 + DEC_BATCH].set(c)
    mods0 = _mod_rows(_adaln(cond8, l0_w_ada, l0_b_ada))
    mods1 = _mod_rows(_adaln(cond8, l1_w_ada, l1_b_ada))

    xc0 = x_prompt.reshape(T_CTX, D)
    xl0 = x_sample.reshape(T_LAT, D)

    xc0m, xl0m, k_new, v_new = _l0_mixer(xc0, xl0, l0_g_norm1, mods0, p0, cache_l0_k, cache_l0_v)
    h0, route0 = _router(xc0m, xl0m, l0_g_norm2, mods0, p0)
    moe0 = _moe(h0, route0, p0)

    mods01 = mods1.reshape(3, 8, D).at[:, 5].set(mods0.reshape(3, 8, D)[:, 5]).reshape(24, D)
    xc1m, xl1m, ckv_new, kpe_new = _l1_mixer(xc0m, xl0m, moe0, l1_g_norm1, mods01, p1, cache_l1_ckv, cache_l1_kpe)
    h1, route1 = _router(xc1m, xl1m, l1_g_norm2, mods1, p1)
    moe1 = _moe(h1, route1, p1)

    y_prompt = _final(xc1m, moe1, mods1, 0, SEQ)
    y_sample = _final(xl1m, moe1, mods1, T_CTX, DEC_SEQ)
    return (y_prompt, y_sample, k_new, v_new,
            ckv_new.reshape(BATCH, SEQ, C_KV_LORA), kpe_new.reshape(BATCH, SEQ, C_ROPE))
```

```python
import functools

import jax
import jax.numpy as jnp
import numpy as np
from jax import lax
from jax.experimental import pallas as pl
from jax.experimental.pallas import tpu as pltpu

f32 = jnp.float32
bf16 = jnp.bfloat16

D = 1024
BATCH, SEQ = 32, 256
DEC_BATCH, DEC_SEQ = 2, 1024
PAST = 512
T_CTX = BATCH * SEQ
T_LAT = DEC_BATCH * DEC_SEQ
T = T_CTX + T_LAT
GRID_W = 64
CHUNK = 128
WINDOW = 128
ROPE_THETA = 10000.0
EPS = 1e-6
NEG_INF = -1e30
LANES = 128
SUBS = D // LANES

A_WIDTH = 512
A_GROUPS = 4
B_HEADS, B_KV, B_GROUP, B_HD = 8, 2, 4, 64
B_SCALE = B_HD ** -0.5

C_HEADS, C_Q_LORA, C_KV_LORA, C_NOPE, C_ROPE, C_V = 16, 384, 256, 64, 32, 64
C_QK = C_NOPE + C_ROPE
C_SCALE = C_QK ** -0.5
ODD_IN_PAD = 768
SLOT = 128

N_GROUPS, EPG, N_EXPERTS, D_EXPERT = 4, 8, 32, 256

ROW_TILE = 512
BLOCK_ROWS = 1024
PROJ_ROWS = 512
MOE_TILE = 256
MOE_ROWS = 2 * T + N_EXPERTS * MOE_TILE
MOE_TILES = MOE_ROWS // MOE_TILE
VMEM_CAP = 56 * 1024 * 1024


def _cp(sem, vmem=VMEM_CAP):
    return pltpu.CompilerParams(dimension_semantics=sem, vmem_limit_bytes=vmem)


def _const_spec(shape):
    nd = len(shape)
    return pl.BlockSpec(shape, lambda *_: (0,) * nd, pipeline_mode=pl.Buffered(1))


def _sample_of_tile(i, tile):
    n_ctx = T_CTX // tile
    per_lat = DEC_SEQ // tile
    return jnp.where(i < n_ctx, 0, 1 + (i - n_ctx) // per_lat)


def _silu(x):
    return x * jax.nn.sigmoid(x)


def _rms_rows(x, g):
    return x * lax.rsqrt(jnp.mean(x * x, -1, keepdims=True) + EPS) * g


def _swap_pairs(x):
    lane = lax.broadcasted_iota(jnp.int32, x.shape, x.ndim - 1)
    nxt = pltpu.roll(x, x.shape[-1] - 1, x.ndim - 1)
    prv = pltpu.roll(x, 1, x.ndim - 1)
    return jnp.where((lane & 1) == 0, nxt, prv)


def _split_bf16(x):
    hi = x.astype(bf16)
    return hi, (x - hi.astype(f32)).astype(bf16)


def _adaln_kernel(c_ref, w_ref, b_ref, o_ref):
    s_hi, s_lo = _split_bf16(_silu(c_ref[...]))
    w_hi, w_lo = _split_bf16(w_ref[...])
    o_ref[...] = (jnp.dot(s_hi, w_hi, preferred_element_type=f32) + jnp.dot(s_lo, w_hi, preferred_element_type=f32)
                  + jnp.dot(s_hi, w_lo, preferred_element_type=f32) + b_ref[...])


def _adaln(cond8, w, b):
    n = w.shape[1]
    tn = 1536
    return pl.pallas_call(
        _adaln_kernel,
        grid=(n // tn,),
        in_specs=[_const_spec((8, D)), pl.BlockSpec((D, tn), lambda j: (0, j)),
                  pl.BlockSpec((1, tn), lambda j: (0, j))],
        out_specs=pl.BlockSpec((8, tn), lambda j: (0, j)),
        out_shape=jax.ShapeDtypeStruct((8, n), f32),
        compiler_params=_cp(("arbitrary",)),
        name="adaln",
    )(cond8, w, b.reshape(1, n))


def _mod_rows(m8):
    m = m8[:3].reshape(3, 6, D)
    return jnp.pad(m, ((0, 0), (0, 2), (0, 0))).reshape(24, D)


N_CTX_TILES = T_CTX // ROW_TILE


def _token_specs(width):
    return [pl.BlockSpec((ROW_TILE, width), lambda i: (jnp.minimum(i, N_CTX_TILES - 1), 0)),
            pl.BlockSpec((ROW_TILE, width), lambda i: (jnp.maximum(i - N_CTX_TILES, 0), 0))]


def _token_rows(xc_ref, xl_ref):
    return jnp.where(pl.program_id(0) < N_CTX_TILES, xc_ref[...], xl_ref[...])


def _store_token_major(ref, x):
    n = x.shape[0]
    for s in range(SUBS):
        ref[pl.ds(s, n, stride=SUBS), :] = x[:, s * LANES:(s + 1) * LANES]


def _load_token_major(ref, n, row0=0):
    return jnp.concatenate([ref[pl.ds(row0 * SUBS + s, n, stride=SUBS), :] for s in range(SUBS)], axis=1)


def _token_major_spec(rows, index_map):
    return pl.BlockSpec((rows * SUBS, LANES), index_map)


def _rope_tables(n, rot_dim, lanes, lane0, copies=1):
    rows_count = n // GRID_W
    rows = np.repeat(np.arange(rows_count), GRID_W).astype(np.float64)
    cols = np.tile(np.arange(GRID_W), rows_count).astype(np.float64)
    d_axis = rot_dim // 2
    inv = ROPE_THETA ** (-np.arange(0, d_axis, 2, dtype=np.float64) / d_axis)
    ang = np.concatenate([rows[:, None] * inv, cols[:, None] * inv], -1)
    c = np.ones((n, lanes), np.float32)
    s = np.zeros((n, lanes), np.float32)
    for j in range(copies):
        lo = lane0 + j * rot_dim
        c[:, lo:lo + rot_dim] = np.repeat(np.cos(ang), 2, axis=1)
        s[:, lo:lo + rot_dim] = np.repeat(np.sin(ang), 2, axis=1) * np.tile(np.array([-1.0, 1.0]), rot_dim // 2)
    return jnp.asarray(c), jnp.asarray(s)


L0_Q0 = 2 * A_WIDTH
L0_K0 = L0_Q0 + B_HEADS * LANES
L0_V0 = L0_K0 + B_KV * B_HD
L0_IN = L0_V0 + B_KV * B_HD


def _l0_kernel(*refs, latent):
    if latent:
        (sink_ref, x_ref, gn_ref, mod_ref, win_ref, gvn_ref, ws_ref, bsb_ref, gq_ref, gk_ref, wo_ref,
         cos_ref, sin_ref, kc_ref, vc_ref, xo_ref, zs, cat, qs, ks, vt, kcb, vct) = refs
        key_off = WINDOW
    else:
        (sink_ref, x_ref, gn_ref, mod_ref, win_ref, gvn_ref, ws_ref, bsb_ref, gq_ref, gk_ref, wo_ref,
         xo_ref, ko_ref, vo_ref, zs, cat, qs, ks, vt, kf, vf) = refs
        key_off = 0
    n = BLOCK_ROWS
    n_chunks = n // CHUNK
    low = lax.broadcasted_iota(jnp.int32, (CHUNK, LANES), 1) < B_HD

    if latent:
        zpad = jnp.zeros((WINDOW, LANES), bf16)
        for c0 in (0, 1 + n_chunks):
            ks[c0 * CHUNK:(c0 + 1) * CHUNK, :] = zpad
            vt[c0] = zpad
        kcb[...] = kc_ref[...].astype(bf16)
        for i in range(PAST // CHUNK):
            vct[i] = vc_ref[i * CHUNK:(i + 1) * CHUNK, :].T.astype(bf16)

    def project(c, carry):
        r = pl.ds(pl.multiple_of(c * PROJ_ROWS, PROJ_ROWS), PROJ_ROWS)
        h = _rms_rows(x_ref[r, :], gn_ref[...]) * (1.0 + mod_ref[1:2, :]) + mod_ref[0:1, :]
        zs[r, :] = jnp.dot(h.astype(bf16), win_ref[...], preferred_element_type=f32)
        return carry

    lax.fori_loop(0, n // PROJ_ROWS, project, 0)

    def prepare(c, carry):
        r = pl.ds(pl.multiple_of(c * CHUNK, CHUNK), CHUNK)
        u = jax.nn.gelu(zs[r, 0:A_WIDTH])
        v = jax.nn.gelu(zs[r, A_WIDTH:2 * A_WIDTH])
        mu = jnp.mean(v, -1, keepdims=True)
        var = jnp.mean(jnp.square(v - mu), -1, keepdims=True)
        vn = ((v - mu) * lax.rsqrt(var + EPS) * gvn_ref[...]).astype(bf16)
        for g in range(A_GROUPS):
            cs = slice(g * CHUNK, (g + 1) * CHUNK)
            mixed = jnp.dot(ws_ref[g], vn[:, cs], preferred_element_type=f32) + bsb_ref[g]
            cat[r, cs] = (u[:, cs] * mixed).astype(bf16)
        if latent:
            cs_, sn_ = cos_ref[r, :], sin_ref[r, :]
        for h in range(B_HEADS):
            hs = slice(h * LANES, (h + 1) * LANES)
            qh = zs[r, L0_Q0 + h * LANES:L0_Q0 + (h + 1) * LANES]
            qh = qh * lax.rsqrt(jnp.sum(qh * qh, -1, keepdims=True) * (1.0 / B_HD) + EPS) * gq_ref[:, hs]
            if latent:
                qh = qh * cs_ + _swap_pairs(qh) * sn_
            qs[r, hs] = (qh * B_SCALE).astype(bf16)
        k = zs[r, L0_K0:L0_K0 + LANES]
        k2 = k * k
        s0 = jnp.sum(jnp.where(low, k2, 0.0), -1, keepdims=True)
        s1 = jnp.sum(jnp.where(low, 0.0, k2), -1, keepdims=True)
        k = k * lax.rsqrt(jnp.where(low, s0, s1) * (1.0 / B_HD) + EPS) * gk_ref[...]
        vv = zs[r, L0_V0:L0_V0 + LANES]
        if latent:
            k = k * cs_ + _swap_pairs(k) * sn_
        else:
            kf[r, :] = k
            vf[r, :] = vv
        kr = pl.ds(pl.multiple_of(c * CHUNK + key_off, CHUNK), CHUNK)
        ks[kr, :] = k.astype(bf16)
        vt[c + key_off // CHUNK] = vv.T.astype(bf16)
        return carry

    lax.fori_loop(0, n_chunks, prepare, 0)

    def attend(r, rows, key_sets):
        q = jnp.concatenate([qs[r, h * LANES:(h + 1) * LANES] for h in range(B_HEADS)], axis=0)
        sk = jnp.concatenate([jnp.full((1, rows), sink_ref[h], f32) for h in range(B_HEADS)], axis=1)
        scores = []
        m = sk
        for k, _, keep in key_sets:
            s = lax.dot_general(k, q, (((1,), (1,)), ((), ())), preferred_element_type=f32)
            if keep is not None:
                s = jnp.where(keep, s, NEG_INF)
            scores.append(s)
            m = jnp.maximum(m, jnp.max(s, 0, keepdims=True))
        den = jnp.exp(sk - m)
        ot = None
        for s, (_, vts, _) in zip(scores, key_sets):
            e = jnp.exp(s - m)
            den = den + jnp.sum(e, 0, keepdims=True)
            eb = e.astype(bf16)
            for i, v_t in enumerate(vts):
                pv = jnp.dot(v_t, eb[i * CHUNK:(i + 1) * CHUNK, :], preferred_element_type=f32)
                ot = pv if ot is None else ot + pv
        ot = ot * (1.0 / den)
        for pair in range(B_HEADS // 2):
            f0 = (2 * pair // B_GROUP) * B_HD
            pair_t = jnp.concatenate([ot[f0:f0 + B_HD, 2 * pair * rows:(2 * pair + 1) * rows],
                                      ot[f0:f0 + B_HD, (2 * pair + 1) * rows:(2 * pair + 2) * rows]], axis=0)
            cat[r, A_WIDTH + pair * LANES:A_WIDTH + (pair + 1) * LANES] = pair_t.T.astype(bf16)

    if latent:
        span = CHUNK + 2 * WINDOW

        def attend_block(c, carry):
            start = pl.multiple_of(c * CHUNK, CHUNK)
            kr = pl.ds(start, span)
            kj = lax.broadcasted_iota(jnp.int32, (span, B_HEADS * CHUNK), 0)
            qi = lax.broadcasted_iota(jnp.int32, (span, B_HEADS * CHUNK), 1) & (CHUNK - 1)
            kpos = start - WINDOW + kj
            keep = (jnp.abs(kj - WINDOW - qi) <= WINDOW) & (kpos >= 0) & (kpos < n)
            attend(pl.ds(start, CHUNK), CHUNK,
                   [(ks[kr, :], [vt[c + i] for i in range(span // CHUNK)], keep),
                    (kcb[...], [vct[i] for i in range(PAST // CHUNK)], None)])
            return carry

        lax.fori_loop(0, n_chunks, attend_block, 0)
    else:
        def attend_seq(sq, carry):
            r = pl.ds(pl.multiple_of(sq * SEQ, SEQ), SEQ)
            attend(r, SEQ, [(ks[r, :], [vt[sq * (SEQ // CHUNK) + i] for i in range(SEQ // CHUNK)], None)])
            ko_ref[sq] = kf[r, :].T
            vo_ref[sq] = vf[r, :].T
            return carry

        lax.fori_loop(0, n // SEQ, attend_seq, 0)

    def output(c, carry):
        r = pl.ds(pl.multiple_of(c * PROJ_ROWS, PROJ_ROWS), PROJ_ROWS)
        y = jnp.dot(cat[r, :], wo_ref[...], preferred_element_type=f32)
        xo_ref[r, :] = x_ref[r, :] + mod_ref[2:3, :] * y
        return carry

    lax.fori_loop(0, n // PROJ_ROWS, output, 0)


def _l0_mixer(xc, xl, g_norm, mods, p, cache_k, cache_v):
    w = p['w_in']
    q = w[:, 2 * A_WIDTH:2 * A_WIDTH + B_HEADS * B_HD].reshape(D, B_HEADS, B_HD)
    zero = jnp.zeros((D, B_GROUP, B_HD), f32)
    q_slots = jnp.concatenate([jnp.concatenate([q[:, :B_GROUP], zero], axis=2),
                               jnp.concatenate([zero, q[:, B_GROUP:]], axis=2)], axis=1).reshape(D, B_HEADS * LANES)
    win = jnp.concatenate([w[:, :2 * A_WIDTH], q_slots, w[:, 2 * A_WIDTH + B_HEADS * B_HD:]], axis=1).astype(bf16)
    zg = jnp.zeros((B_HD,), f32)
    gq = jnp.concatenate([jnp.tile(jnp.concatenate([p['g_q'], zg]), B_GROUP),
                          jnp.tile(jnp.concatenate([zg, p['g_q']]), B_GROUP)]).reshape(1, B_HEADS * LANES)
    gk = jnp.tile(p['g_k'], B_KV).reshape(1, LANES)
    gvn = p['g_vnorm'].reshape(1, A_WIDTH)
    ws = p['w_s'].astype(bf16)
    bsb = jnp.broadcast_to(p['b_s'][:, :, None], (A_GROUPS, CHUNK, CHUNK))
    wo = p['w_o'].astype(bf16)
    weights = (g_norm.reshape(1, D),)
    consts = (win, gvn, ws, bsb, gq, gk, wo)
    c_specs = [_const_spec(a.shape) for a in consts]
    smem = pl.BlockSpec(memory_space=pltpu.SMEM)
    row = pl.BlockSpec((BLOCK_ROWS, D), lambda b: (b, 0))
    kv = pl.BlockSpec((BLOCK_ROWS // SEQ, LANES, SEQ), lambda b: (b, 0, 0))

    def scratch(pad):
        return [pltpu.VMEM((BLOCK_ROWS, L0_IN), f32), pltpu.VMEM((BLOCK_ROWS, D), bf16),
                pltpu.VMEM((BLOCK_ROWS, B_HEADS * LANES), bf16), pltpu.VMEM((BLOCK_ROWS + pad, LANES), bf16),
                pltpu.VMEM(((BLOCK_ROWS + pad) // CHUNK, LANES, CHUNK), bf16)]

    kv_shape = jax.ShapeDtypeStruct((BATCH, LANES, SEQ), f32)
    state = pltpu.VMEM((BLOCK_ROWS, LANES), f32)
    xo_ctx, k_t, v_t = pl.pallas_call(
        functools.partial(_l0_kernel, latent=False),
        grid=(T_CTX // BLOCK_ROWS,),
        in_specs=[smem, row, _const_spec((1, D)), pl.BlockSpec((8, D), lambda b: (0, 0))] + c_specs,
        out_specs=[row, kv, kv],
        out_shape=[jax.ShapeDtypeStruct((T_CTX, D), f32), kv_shape, kv_shape],
        scratch_shapes=scratch(0) + [state, state],
        compiler_params=_cp(("parallel",)), name="l0_mixer_ctx",
    )(p['sink'], xc, *weights, mods, *consts)
    k_new = k_t.reshape(BATCH, B_KV, B_HD, SEQ).transpose(0, 3, 1, 2)
    v_new = v_t.reshape(BATCH, B_KV, B_HD, SEQ).transpose(0, 3, 1, 2)

    cos, sin = _rope_tables(DEC_SEQ, B_HD, LANES, 0, copies=LANES // B_HD)
    cache = pl.BlockSpec((None, PAST, LANES), lambda b: (b, 0, 0))
    past = [pltpu.VMEM((PAST, LANES), bf16), pltpu.VMEM((PAST // CHUNK, LANES, CHUNK), bf16)]
    xo_lat = pl.pallas_call(
        functools.partial(_l0_kernel, latent=True),
        grid=(DEC_BATCH,),
        in_specs=[smem, row, _const_spec((1, D)), pl.BlockSpec((8, D), lambda b: (1 + b, 0))] + c_specs + [
                  _const_spec(cos.shape), _const_spec(sin.shape), cache, cache],
        out_specs=row,
        out_shape=jax.ShapeDtypeStruct((T_LAT, D), f32),
        scratch_shapes=scratch(2 * WINDOW) + past,
        compiler_params=_cp(("parallel",)), name="l0_mixer_lat",
    )(p['sink'], xl, *weights, mods, *consts, cos, sin,
      cache_k.reshape(DEC_BATCH, PAST, LANES), cache_v.reshape(DEC_BATCH, PAST, LANES))
    return xo_ctx, xo_lat, k_new, v_new


C_SLOTS = C_HEADS * SLOT
C_PAIRS = C_HEADS // 2
L1_ROWS = 256


def _l1_kernel(*refs, latent):
    if latent:
        (x_ref, gn_ref, mod_ref, win_ref, gqa_ref, wuq_ref, gq_ref, gkva_ref, wuk_ref, wuvt_ref, gk_ref,
         wo_ref, wuqs_ref, qcos_ref, qsin_ref, kcos_ref, ksin_ref, cckv_ref, ckpe_ref, xo_ref,
         zs, cat, qs, ks, vt, wide, wide2) = refs
        n_ctx = PAST
    else:
        (x_ref, gn_ref, mod_ref, win_ref, gqa_ref, wuq_ref, gq_ref, gkva_ref, wuk_ref, wuvt_ref, gk_ref,
         wo_ref, xo_ref, ckvo_ref, kpeo_ref, zs, cat, qs, ks, vt, wide) = refs
        n_ctx = 0
    n = BLOCK_ROWS
    nt_dims = (((1,), (1,)), ((), ()))

    def inv_rms(v):
        return lax.rsqrt(jnp.sum(v * v, -1, keepdims=True) * (1.0 / C_QK) + EPS)

    def expand_keys(ckv_n, kslot, kb, rope_rows):
        cb = ckv_n.astype(bf16)
        key_rows = pl.ds(pl.multiple_of(kb * L1_ROWS, L1_ROWS), L1_ROWS)
        wide[...] = jnp.dot(cb, wuk_ref[...], preferred_element_type=f32)
        if rope_rows is not None:
            kcos = kcos_ref[rope_rows, :]
            turned = _swap_pairs(kslot) * ksin_ref[rope_rows, :]
        for h in range(C_HEADS):
            kh = wide[:, h * SLOT:(h + 1) * SLOT] + kslot
            if rope_rows is not None:
                kh = inv_rms(kh) * (kh * kcos + turned)
            else:
                kh = kh * inv_rms(kh) * gk_ref[...]
            ks[h, key_rows, :] = kh.astype(bf16)
        v_t = lax.dot_general(wuvt_ref[...], cb, nt_dims, preferred_element_type=f32).astype(bf16)
        for pair in range(C_PAIRS):
            vt[pair, kb] = v_t[pair * LANES:(pair + 1) * LANES, :]

    if latent:
        def past_keys(c, carry):
            r = pl.ds(pl.multiple_of(c * L1_ROWS, L1_ROWS), L1_ROWS)
            expand_keys(cckv_ref[r, :], ckpe_ref[r, :], c, None)
            return carry

        lax.fori_loop(0, PAST // L1_ROWS, past_keys, 0)

    def project(c, carry):
        r = pl.ds(pl.multiple_of(c * PROJ_ROWS, PROJ_ROWS), PROJ_ROWS)
        h = _rms_rows(x_ref[r, :], gn_ref[...]) * (1.0 + mod_ref[1:2, :]) + mod_ref[0:1, :]
        zs[r, :] = jnp.dot(h.astype(bf16), win_ref[...], preferred_element_type=f32)
        return carry

    lax.fori_loop(0, n // PROJ_ROWS, project, 0)

    def prepare(c, carry):
        r = pl.ds(pl.multiple_of(c * L1_ROWS, L1_ROWS), L1_ROWS)
        qa = _rms_rows(zs[r, 0:C_Q_LORA], gqa_ref[...]).astype(bf16)
        wide[...] = jnp.dot(qa, wuq_ref[...], preferred_element_type=f32)
        if latent:
            wide2[...] = jnp.dot(qa, wuqs_ref[...], preferred_element_type=f32)
            qcos, qsin = qcos_ref[r, :], qsin_ref[r, :]
        for h in range(C_HEADS):
            hs = slice(h * SLOT, (h + 1) * SLOT)
            qh = wide[:, hs]
            if latent:
                qh = inv_rms(qh) * (qh * qcos + wide2[:, hs] * qsin)
            else:
                qh = qh * inv_rms(qh) * gq_ref[...]
            qs[h, r, :] = (qh * C_SCALE).astype(bf16)
        ckv_n = _rms_rows(zs[r, C_Q_LORA:C_Q_LORA + C_KV_LORA], gkva_ref[...])
        kslot = zs[r, C_Q_LORA + C_KV_LORA:ODD_IN_PAD]
        if not latent:
            ckvo_ref[r, :] = ckv_n
            kpeo_ref[r, :] = kslot
        expand_keys(ckv_n, kslot, c + n_ctx // L1_ROWS, r if latent else None)
        return carry

    lax.fori_loop(0, n // L1_ROWS, prepare, 0)

    low = lax.broadcasted_iota(jnp.int32, (2 * C_V, L1_ROWS), 0) < C_V
    n_kblocks = (n_ctx + n) // L1_ROWS
    pairs_per_step = 2 if latent else 4

    def attend(c, carry):
        r = pl.ds(pl.multiple_of(c * L1_ROWS, L1_ROWS), L1_ROWS)

        def values_t(pair, eb):
            if not latent:
                return jnp.dot(vt[pair, c], eb, preferred_element_type=f32)
            o_t = None
            for b in range(n_kblocks):
                pv = jnp.dot(vt[pair, b], eb[b * L1_ROWS:(b + 1) * L1_ROWS, :], preferred_element_type=f32)
                o_t = pv if o_t is None else o_t + pv
            return o_t

        def pairs_step(i, carry2):
            pairs = [i * pairs_per_step + j for j in range(pairs_per_step)]
            heads = [2 * p + hh for p in pairs for hh in range(2)]
            scores = [lax.dot_general(ks[h] if latent else ks[h, r, :], qs[h, r, :], nt_dims,
                                      preferred_element_type=f32) for h in heads]
            exps = [jnp.exp(s - jnp.max(s, 0, keepdims=True)) for s in scores]
            dens = [jnp.sum(e, 0, keepdims=True) for e in exps]
            outs = [values_t(h // 2, e.astype(bf16)) / den for h, e, den in zip(heads, exps, dens)]
            for j, pair in enumerate(pairs):
                cat[pair, r, :] = jnp.where(low, outs[2 * j], outs[2 * j + 1]).T.astype(bf16)
            return carry2

        return lax.fori_loop(0, C_PAIRS // pairs_per_step, pairs_step, carry)

    lax.fori_loop(0, n // L1_ROWS, attend, 0)

    def output(c, carry):
        r = pl.ds(pl.multiple_of(c * PROJ_ROWS, PROJ_ROWS), PROJ_ROWS)
        heads = jnp.concatenate([cat[pair, r, :] for pair in range(C_PAIRS)], axis=1)
        y = jnp.dot(heads, wo_ref[...], preferred_element_type=f32)
        xo_ref[r, :] = x_ref[r, :] + mod_ref[2:3, :] * y
        return carry

    lax.fori_loop(0, n // PROJ_ROWS, output, 0)


def _slot_cols(w, heads, width, lo, hi, lane0):
    k = w.shape[0]
    w3 = w.reshape(k, heads, width)[:, :, lo:hi]
    out = jnp.zeros((k, heads, SLOT), w.dtype).at[:, :, lane0:lane0 + (hi - lo)].set(w3)
    return out.reshape(k, heads * SLOT)


def _l1_mixer(xc, xl, g_norm, mods, p, cache_ckv, cache_kpe):
    w_in = jnp.zeros((D, ODD_IN_PAD), f32).at[:, :C_Q_LORA + C_KV_LORA].set(
        p['w_in'][:, :C_Q_LORA + C_KV_LORA]).at[
        :, C_Q_LORA + C_KV_LORA + C_NOPE:C_Q_LORA + C_KV_LORA + C_QK].set(p['w_in'][:, C_Q_LORA + C_KV_LORA:])
    wuq = _slot_cols(p['w_uq'], C_HEADS, C_QK, 0, C_QK, 0).astype(bf16)
    wuk = _slot_cols(p['w_ukv'], C_HEADS, C_NOPE + C_V, 0, C_NOPE, 0).astype(bf16)
    wuv_t = p['w_ukv'].reshape(C_KV_LORA, C_HEADS, C_NOPE + C_V)[:, :, C_NOPE:].reshape(
        C_KV_LORA, C_HEADS * C_V).T.astype(bf16)
    gq = jnp.zeros((1, SLOT), f32).at[0, :C_QK].set(p['g_q'])
    gk = jnp.zeros((1, SLOT), f32).at[0, :C_QK].set(p['g_k'])
    consts = (g_norm.reshape(1, D), w_in.astype(bf16), p['g_qa'].reshape(1, C_Q_LORA), wuq, gq,
              p['g_kva'].reshape(1, C_KV_LORA), wuk, wuv_t, gk, p['w_o'].astype(bf16))
    c_specs = [_const_spec(a.shape) for a in consts]
    row = pl.BlockSpec((BLOCK_ROWS, D), lambda b: (b, 0))
    n_ctx_blocks = T_CTX // BLOCK_ROWS

    def scratch(n_keys):
        return [pltpu.VMEM((BLOCK_ROWS, ODD_IN_PAD), f32), pltpu.VMEM((C_PAIRS, BLOCK_ROWS, LANES), bf16),
                pltpu.VMEM((C_HEADS, BLOCK_ROWS, SLOT), bf16), pltpu.VMEM((C_HEADS, n_keys, SLOT), bf16),
                pltpu.VMEM((C_PAIRS, n_keys // L1_ROWS, LANES, L1_ROWS), bf16),
                pltpu.VMEM((L1_ROWS, C_SLOTS), f32)]

    xo_ctx, ckv_new, kpe_slot = pl.pallas_call(
        functools.partial(_l1_kernel, latent=False),
        grid=(n_ctx_blocks,),
        in_specs=[row, c_specs[0], pl.BlockSpec((8, D), lambda b: (0, 0))] + c_specs[1:],
        out_specs=[row, pl.BlockSpec((BLOCK_ROWS, C_KV_LORA), lambda b: (b, 0)),
                   pl.BlockSpec((BLOCK_ROWS, SLOT), lambda b: (b, 0))],
        out_shape=[jax.ShapeDtypeStruct((T_CTX, D), f32), jax.ShapeDtypeStruct((T_CTX, C_KV_LORA), f32),
                   jax.ShapeDtypeStruct((T_CTX, SLOT), f32)],
        scratch_shapes=scratch(BLOCK_ROWS),
        compiler_params=_cp(("parallel",)), name="l1_mixer_ctx",
    )(xc, consts[0], mods, *consts[1:])

    cos, sin = _rope_tables(DEC_SEQ, C_ROPE, SLOT, C_NOPE)
    pair_swap = lambda a: a.reshape(a.shape[:-1] + (a.shape[-1] // 2, 2))[..., ::-1].reshape(a.shape)
    rope = (pair_swap(wuq), gq * cos, pair_swap(gq) * sin, gk * cos, pair_swap(gk) * sin)
    ckpe = jnp.zeros((DEC_BATCH, PAST, SLOT), f32).at[:, :, C_NOPE:C_QK].set(cache_kpe)
    xo_lat = pl.pallas_call(
        functools.partial(_l1_kernel, latent=True),
        grid=(DEC_BATCH,),
        in_specs=[pl.BlockSpec((BLOCK_ROWS, D), lambda b: (b, 0), pipeline_mode=pl.Buffered(1)), c_specs[0],
                  pl.BlockSpec((8, D), lambda b: (1 + b, 0))] + c_specs[1:] + [_const_spec(a.shape) for a in rope] + [
                  pl.BlockSpec((None, PAST, C_KV_LORA), lambda b: (b, 0, 0)),
                  pl.BlockSpec((None, PAST, SLOT), lambda b: (b, 0, 0))],
        out_specs=row,
        out_shape=jax.ShapeDtypeStruct((T_LAT, D), f32),
        scratch_shapes=scratch(PAST + BLOCK_ROWS) + [pltpu.VMEM((L1_ROWS, C_SLOTS), f32)],
        compiler_params=_cp(("parallel",)), name="l1_mixer_lat",
    )(xl, consts[0], mods, *consts[1:], *rope, cache_ckv, ckpe)
    return xo_ctx, xo_lat, ckv_new, kpe_slot[:, C_NOPE:C_QK]


ROUTER_ROWS = 40


def _router_kernel(xc_ref, xl_ref, gn_ref, mod_ref, whi_ref, wlo_ref, br_ref, h_ref, route_ref):
    h = _rms_rows(_token_rows(xc_ref, xl_ref), gn_ref[...]) * (1.0 + mod_ref[4:5, :]) + mod_ref[3:4, :]
    _store_token_major(h_ref, h)
    h_hi, h_lo = _split_bf16(h)
    nt = (((1,), (1,)), ((), ()))
    logits = (lax.dot_general(whi_ref[...], h_hi, nt, preferred_element_type=f32)
              + lax.dot_general(whi_ref[...], h_lo, nt, preferred_element_type=f32)
              + lax.dot_general(wlo_ref[...], h_hi, nt, preferred_element_type=f32))
    logits = logits[0:ROUTER_ROWS, :] + br_ref[0:ROUTER_ROWS, :]
    row_i = lax.broadcasted_iota(jnp.int32, logits.shape, 0)
    row = row_i.astype(f32)
    big = 1e6
    is_g = (row_i >= N_EXPERTS) & (row_i < N_EXPERTS + N_GROUPS)
    lg = jnp.where(is_g, logits, -jnp.inf)
    mg = jnp.max(lg, 0, keepdims=True)
    gsel = jnp.min(jnp.where(lg == mg, row, big), 0, keepdims=True) - N_EXPERTS
    pg_sel = 1.0 / jnp.sum(jnp.where(is_g, jnp.exp(lg - mg), 0.0), 0, keepdims=True)
    in_grp = (row_i < N_EXPERTS) & ((row_i >> 3).astype(f32) == gsel)
    le = jnp.where(in_grp, logits, -jnp.inf)
    m1 = jnp.max(le, 0, keepdims=True)
    i1 = jnp.min(jnp.where(le == m1, row, big), 0, keepdims=True)
    le2 = jnp.where(row == i1, -jnp.inf, le)
    m2 = jnp.max(le2, 0, keepdims=True)
    i2 = jnp.min(jnp.where(le2 == m2, row, big), 0, keepdims=True)
    e2 = jnp.exp(m2 - m1)
    w1 = pg_sel / (1.0 + e2)
    w2 = pg_sel * e2 / (1.0 + e2)
    sub = lax.broadcasted_iota(jnp.int32, route_ref.shape, 0)
    route_ref[...] = jnp.where(sub == 0, i1, jnp.where(sub == 1, i2, jnp.where(sub == 2, w1,
                                                                                jnp.where(sub == 3, w2, 0.0))))


def _router(xc, xl, g_norm, mods, p):
    wr = jnp.zeros((128, D), f32).at[:N_EXPERTS].set(p['w_re'].T).at[
        N_EXPERTS:N_EXPERTS + N_GROUPS].set(p['w_rg'].T)
    w_hi, w_lo = _split_bf16(wr)
    br =jnp.zeros((128, 1), f32).at[:N_EXPERTS, 0].set(p['b_re']).at[
        N_EXPERTS:N_EXPERTS + N_GROUPS, 0].set(p['b_rg'])
    return pl.pallas_call(
        _router_kernel,
        grid=(T // ROW_TILE,),
        in_specs=_token_specs(D) + [
                  _const_spec((1, D)),
                  pl.BlockSpec((8, D), lambda i: (_sample_of_tile(i, ROW_TILE), 0)),
                  _const_spec((128, D)), _const_spec((128, D)), _const_spec((128, 1))],
        out_specs=[_token_major_spec(ROW_TILE, lambda i: (i, 0)), pl.BlockSpec((8, ROW_TILE), lambda i: (0, i))],
        out_shape=[jax.ShapeDtypeStruct((T * SUBS, LANES), f32), jax.ShapeDtypeStruct((8, T), f32)],
        compiler_params=_cp(("parallel",)), name="router",
    )(xc, xl, g_norm.reshape(1, D), mods, w_hi, w_lo, br)


PLAN_FIRST_TILE, PLAN_TILES, PLAN_N_USED = 0, 1, 2


def _plan_kernel(rt_ref, pos_ref, plan_ref, rank):
    n_blk = T // 128
    e_col = lax.broadcasted_iota(jnp.int32, (N_EXPERTS, 128), 0).astype(f32)
    ri = lax.broadcasted_iota(jnp.int32, (128, 128), 0)
    ci = lax.broadcasted_iota(jnp.int32, (128, 128), 1)
    before = jnp.where(ri < ci, 1.0, 0.0).astype(bf16)

    def picks(b):
        cs = slice(b * 128, (b + 1) * 128)
        return rt_ref[0:1, cs] == e_col, rt_ref[1:2, cs] == e_col

    counts = jnp.zeros((N_EXPERTS, 1), f32)
    for b in range(n_blk):
        m0, m1 = picks(b)
        m = jnp.where(m0, 1.0, 0.0) + jnp.where(m1, 1.0, 0.0)
        rank[:, b * 128:(b + 1) * 128] = jnp.dot(m.astype(bf16), before, preferred_element_type=f32) + counts
        counts = counts + jnp.sum(m, axis=1, keepdims=True)

    tiles = jnp.floor((counts + (MOE_TILE - 1.0)) * (1.0 / MOE_TILE))
    er = lax.broadcasted_iota(jnp.int32, (N_EXPERTS, N_EXPERTS), 0)
    ec = lax.broadcasted_iota(jnp.int32, (N_EXPERTS, N_EXPERTS), 1)
    earlier = jnp.where(ec < er, 1.0, 0.0).astype(bf16)
    tile_start = jnp.dot(earlier, jnp.broadcast_to(tiles, (N_EXPERTS, 128)).astype(bf16),
                         preferred_element_type=f32)
    row_start = tile_start * MOE_TILE

    sub = lax.broadcasted_iota(jnp.int32, (8, 128), 0)
    for b in range(n_blk):
        m0, m1 = picks(b)
        base = rank[:, b * 128:(b + 1) * 128] + row_start
        p0 = jnp.sum(jnp.where(m0, base, 0.0), axis=0, keepdims=True)
        p1 = jnp.sum(jnp.where(m1, base, 0.0), axis=0, keepdims=True)
        pos_ref[:, b * 128:(b + 1) * 128] = jnp.where(sub == 0, p0, jnp.where(sub == 1, p1, 0.0)).astype(jnp.int32)

    tile_end = tile_start + tiles
    n_used = jnp.max(tile_end, axis=0, keepdims=True)
    diag =(lax.broadcasted_iota(jnp.int32, (N_EXPERTS, 128), 0)
            == lax.broadcasted_iota(jnp.int32, (N_EXPERTS, 128), 1))
    first = jnp.sum(jnp.where(diag, tile_start, 0.0), axis=0, keepdims=True)
    count = jnp.sum(jnp.where(diag, tiles, 0.0), axis=0, keepdims=True)
    rows = jnp.where(sub == PLAN_FIRST_TILE, first,
                     jnp.where(sub == PLAN_TILES, count, jnp.where(sub == PLAN_N_USED, n_used, 0.0)))
    plan_ref[...] = rows.astype(jnp.int32)


def _slot_code(t, k):
    return t * SUBS + k * (SUBS // 2)


def _code_offset(code):
    return pl.multiple_of(code & ~(SUBS - 1), SUBS)


def _code_gate_index(code):
    return code >> 2


PAD_CODE = T * SUBS


def _invert_kernel(pos_ref, plan_ref, code_ref):
    def pad_tile(tile, carry):
        for u in range(MOE_TILE):
            code_ref[tile * MOE_TILE + u] = PAD_CODE
        return carry

    def pad_last_tile(e, carry):
        return pad_tile(jnp.maximum(plan_ref[PLAN_FIRST_TILE, e] + plan_ref[PLAN_TILES, e] - 1, 0), carry)
    lax.fori_loop(0, N_EXPERTS, pad_last_tile, 0)
    lax.fori_loop(plan_ref[PLAN_N_USED, 0], MOE_TILES, pad_tile, 0)

    group = 32
    for k in range(2):
        def place(i, carry):
            t0 = i * group
            slots = [pos_ref[k * T + t0 + u] for u in range(group)]
            for u, s in enumerate(slots):
                code_ref[s] = _slot_code(t0 + u, k)
            return carry
        lax.fori_loop(0, T // group, place, 0)


def _route_plan(route_t):
    pos, plan = pl.pallas_call(
        _plan_kernel,
        out_shape=[jax.ShapeDtypeStruct((8, T), jnp.int32), jax.ShapeDtypeStruct((8, LANES), jnp.int32)],
        scratch_shapes=[pltpu.VMEM((N_EXPERTS, T), f32)],
        compiler_params=_cp(None), name="route_plan",
    )(route_t)
    smem = pl.BlockSpec(memory_space=pltpu.SMEM)
    codes = pl.pallas_call(
        _invert_kernel,
        in_specs=[smem, smem], out_specs=smem,
        out_shape=jax.ShapeDtypeStruct((MOE_ROWS,), jnp.int32),
        name="route_invert",
    )(pos[0:2].reshape(2 * T), plan)
    gates = jnp.pad(route_t[2:4].T.reshape(2 * T), (0, 8))
    return plan[PLAN_FIRST_TILE, :N_EXPERTS], plan[PLAN_TILES, :N_EXPERTS], codes, gates


def _tile_index(i):
    return jnp.minimum(i, MOE_TILES - 1)


TM_ROWS = T * SUBS
SCATTER_GROUP = 16
STAGES = 4
LAST_EXPERT = N_EXPERTS - 1


def _tile_rows(g):
    return pl.ds(pl.multiple_of(g * MOE_TILE, MOE_TILE), MOE_TILE)


def _expert_tile_pairs(first, count, tile_step):
    def pair(pp, carry):
        for parity in range(2):
            g = 2 * pp + parity
            pl.when((g >= first) & (g < first + count))(functools.partial(tile_step, g, parity))
        return carry
    lax.fori_loop(first // 2, (first + count + 1) // 2, pair, 0)


def _gather_tile(code_ref, tile, xs, gbuf):
    base = tile * MOE_TILE
    for r in range(MOE_TILE):
        gbuf[r * SUBS:(r + 1) * SUBS, :] = xs[pl.ds(_code_offset(code_ref[base + r]), SUBS), :]


def _moe_up_kernel(first_ref, count_ref, code_ref, h_hbm, w1_ref, w3_ref, hh_hbm,
                   xs, gbuf_a, gbuf_b, w13, obuf, sem_x, sem_o):
    e = pl.program_id(0)
    first, count = first_ref[e], count_ref[e]
    gbufs = (gbuf_a, gbuf_b)

    def out_copy(slot, g):
        return pltpu.make_async_copy(obuf.at[slot], hh_hbm.at[_tile_rows(g), :], sem_o.at[slot])

    @pl.when(e == 0)
    def _():
        cp = pltpu.make_async_copy(h_hbm, xs.at[pl.ds(0, TM_ROWS), :], sem_x)
        cp.start()
        xs[TM_ROWS:TM_ROWS + SUBS, :] = jnp.zeros((SUBS, LANES), f32)
        cp.wait()
        _gather_tile(code_ref, 0, xs, gbuf_a)

    @pl.when(count > 0)
    def _():
        w13[:, :D_EXPERT] = w1_ref[0].astype(bf16)
        w13[:, D_EXPERT:] = w3_ref[0].astype(bf16)

    def tile_step(g, parity):
        _gather_tile(code_ref, _tile_index(g + 1), xs, gbufs[1 - parity])
        x3 = jnp.swapaxes(gbufs[parity][...].reshape(MOE_TILE, SUBS, LANES), 0, 1)
        x = jnp.concatenate([x3[s] for s in range(SUBS)], axis=1).astype(bf16)
        h13 = jnp.dot(x, w13[...], preferred_element_type=f32)
        hh = (_silu(h13[:, :D_EXPERT]) * h13[:, D_EXPERT:]).astype(bf16)

        slot = g % STAGES

        @pl.when(g >= STAGES)
        def _():
            out_copy(slot, g).wait()
        obuf[slot] = hh
        out_copy(slot, g).start()

    _expert_tile_pairs(first, count, tile_step)

    @pl.when(e == LAST_EXPERT)
    def _():
        n_used = first + count
        for slot in range(STAGES):
            pl.when(n_used > slot)(lambda slot=slot: out_copy(slot, 0).wait())
        obuf[0] = jnp.zeros((MOE_TILE, D_EXPERT), bf16)

        def zero_tile(g, carry):
            cp = out_copy(0, g)
            cp.start()
            cp.wait()
            return carry
        lax.fori_loop(n_used, MOE_TILES, zero_tile, 0)


def _scatter_tile(code_ref, gate_ref, tile, ybuf, acc):
    base = tile * MOE_TILE
    for g0 in range(0, MOE_TILE, SCATTER_GROUP):
        rows = range(g0, g0 + SCATTER_GROUP)
        codes = [code_ref[base + r] for r in rows]
        new = [acc[pl.ds(_code_offset(c), SUBS), :]
               + gate_ref[_code_gate_index(c)] * ybuf[r * SUBS:(r + 1) * SUBS, :]
               for r, c in zip(rows, codes)]
        for c, v in zip(codes, new):
            acc[pl.ds(_code_offset(c), SUBS), :] = v


RES_ROWS = 256


def _residual_out(x_hbm, y_hbm, tok0, sample_of_chunk, mod_ref, acc, rin, rout, sem_r, sem_w):
    n_chunks = x_hbm.shape[0] // RES_ROWS

    def rows(c):
        return pl.ds(pl.multiple_of(c * RES_ROWS, RES_ROWS), RES_ROWS)

    def in_copy(slot, c):
        return pltpu.make_async_copy(x_hbm.at[rows(c), :], rin.at[slot], sem_r.at[slot])

    def out_copy(slot, c):
        return pltpu.make_async_copy(rout.at[slot], y_hbm.at[rows(c), :], sem_w.at[slot])

    in_copy(0, 0).start()

    def pair(cc, carry):
        for slot in range(2):
            c = 2 * cc + slot
            in_copy(slot, c).wait()

            @pl.when(c + 1 < n_chunks)
            def _():
                in_copy(1 - slot, c + 1).start()

            @pl.when(c >= 2)
            def _():
                out_copy(slot, c).wait()
            delta = _load_token_major(acc, RES_ROWS, tok0 + c * RES_ROWS)
            gate = mod_ref[pl.ds(sample_of_chunk(c) * 8 + 5, 1), :]
            rout[slot] = rin[slot] + gate * delta
            out_copy(slot, c).start()
        return carry

    lax.fori_loop(0, n_chunks // 2, pair, 0)
    for slot in range(2):
        out_copy(slot, 0).wait()


def _moe_down_kernel(first_ref, count_ref, code_ref, gate_ref, hh_hbm, w2_ref, xc_hbm, xl_hbm, mod_ref,
                     yc_hbm, yl_hbm, acc, ybuf_a, ybuf_b, w2b, ibuf, rin, rout, sem_i, sem_r, sem_w):
    e = pl.program_id(0)
    first, count = first_ref[e], count_ref[e]
    n_used = first_ref[LAST_EXPERT] + count_ref[LAST_EXPERT]
    ybufs = (ybuf_a, ybuf_b)

    def in_copy(slot, g):
        return pltpu.make_async_copy(hh_hbm.at[_tile_rows(g), :], ibuf.at[slot], sem_i.at[slot])

    @pl.when(e == 0)
    def _():
        for g in range(STAGES - 1):
            in_copy(g, g).start()

        def zero(c, carry):
            acc[pl.ds(pl.multiple_of(c * 1024, 1024), 1024), :] = jnp.zeros((1024, LANES), f32)
            return carry
        lax.fori_loop(0, TM_ROWS // 1024, zero, 0)
        acc[TM_ROWS:TM_ROWS + SUBS, :] = jnp.zeros((SUBS, LANES), f32)
        ybuf_b[...] = jnp.zeros_like(ybuf_b)

    @pl.when(count > 0)
    def _():
        w2b[...] = w2_ref[0].astype(bf16)

    def tile_step(g, parity):
        slot = g % STAGES
        in_copy(slot, g).wait()
        ahead = g + STAGES - 1

        @pl.when(ahead < n_used)
        def _():
            in_copy(ahead % STAGES, ahead).start()
        _store_token_major(ybufs[parity], jnp.dot(ibuf[slot], w2b[...], preferred_element_type=f32))
        _scatter_tile(code_ref, gate_ref, jnp.maximum(g - 1, 0), ybufs[1 - parity], acc)

    _expert_tile_pairs(first, count, tile_step)

    @pl.when(e == LAST_EXPERT)
    def _():
        for parity in range(2):
            pl.when((n_used > 0) & ((n_used - 1) % 2 == parity))(
                functools.partial(_scatter_tile, code_ref, gate_ref, n_used - 1, ybufs[parity], acc))
        _residual_out(xc_hbm, yc_hbm, 0, lambda c: 0, mod_ref, acc, rin, rout, sem_r, sem_w)
        _residual_out(xl_hbm, yl_hbm, T_CTX, lambda c: 1 + c // (DEC_SEQ // RES_ROWS),
                      mod_ref, acc, rin, rout, sem_r, sem_w)


def _moe(h_tm, route_t, p, xc, xl, mods):
    first_tile, n_tiles, codes, gates = _route_plan(route_t)
    tile_rows = pltpu.VMEM((MOE_TILE * SUBS, LANES), f32)
    staging = pltpu.VMEM((STAGES, MOE_TILE, D_EXPERT), bf16)
    res_rows = pltpu.VMEM((2, RES_ROWS, D), f32)
    hbm = pl.BlockSpec(memory_space=pl.ANY)
    hh = pl.pallas_call(
        _moe_up_kernel,
        grid_spec=pltpu.PrefetchScalarGridSpec(
            num_scalar_prefetch=3, grid=(N_EXPERTS,),
            in_specs=[pl.BlockSpec(memory_space=pl.ANY),
                      pl.BlockSpec((1, D, D_EXPERT), lambda e, f, n, c: (e, 0, 0)),
                      pl.BlockSpec((1, D, D_EXPERT), lambda e, f, n, c: (e, 0, 0))],
            out_specs=pl.BlockSpec(memory_space=pl.ANY),
            scratch_shapes=[pltpu.VMEM((TM_ROWS + SUBS, LANES), f32), tile_rows, tile_rows,
                            pltpu.VMEM((D, 2 * D_EXPERT), bf16), staging,
                            pltpu.SemaphoreType.DMA(()), pltpu.SemaphoreType.DMA((STAGES,))]),
        out_shape=jax.ShapeDtypeStruct((MOE_ROWS, D_EXPERT), bf16),
        compiler_params=_cp(("arbitrary",)), name="moe_up",
    )(first_tile, n_tiles, codes, h_tm, p['w1'], p['w3'])
    return pl.pallas_call(
        _moe_down_kernel,
        grid_spec=pltpu.PrefetchScalarGridSpec(
            num_scalar_prefetch=4, grid=(N_EXPERTS,),
            in_specs=[hbm, pl.BlockSpec((1, D_EXPERT, D), lambda e, f, n, c, g: (e, 0, 0)), hbm, hbm,
                      pl.BlockSpec((24, D), lambda e, f, n, c, g: (0, 0), pipeline_mode=pl.Buffered(1))],
            out_specs=[hbm, hbm],
            scratch_shapes=[pltpu.VMEM((TM_ROWS + SUBS, LANES), f32), tile_rows, tile_rows,
                            pltpu.VMEM((D_EXPERT, D), bf16), staging, res_rows, res_rows,
                            pltpu.SemaphoreType.DMA((STAGES,)), pltpu.SemaphoreType.DMA((2,)),
                            pltpu.SemaphoreType.DMA((2,))]),
        out_shape=[jax.ShapeDtypeStruct((T_CTX, D), f32), jax.ShapeDtypeStruct((T_LAT, D), f32)],
        compiler_params=_cp(("arbitrary",)), name="moe_down",
    )(first_tile, n_tiles, codes, gates, hh, p['w2'], xc, xl, mods)


def kernel(x_prompt, x_sample, cache_l0_k, cache_l0_v, cache_l1_ckv, cache_l1_kpe, c, c_ctx, l0_g_norm1, l0_g_norm2, l0_w_ada, l0_b_ada, l0_w_in, l0_g_vnorm, l0_w_s, l0_b_s, l0_g_q, l0_g_k, l0_sink, l0_w_o, l0_w_rg, l0_b_rg, l0_w_re, l0_b_re, l0_w1, l0_w3, l0_w2, l1_g_norm1, l1_g_norm2, l1_w_ada, l1_b_ada, l1_w_in, l1_g_qa, l1_w_uq, l1_g_kva, l1_w_ukv, l1_g_q, l1_g_k, l1_w_o, l1_w_rg, l1_b_rg, l1_w_re, l1_b_re, l1_w1, l1_w3, l1_w2):
    p0 = dict(w_in=l0_w_in, g_vnorm=l0_g_vnorm, w_s=l0_w_s, b_s=l0_b_s, g_q=l0_g_q, g_k=l0_g_k, sink=l0_sink,
              w_o=l0_w_o, w_rg=l0_w_rg, b_rg=l0_b_rg, w_re=l0_w_re, b_re=l0_b_re, w1=l0_w1, w3=l0_w3, w2=l0_w2)
    p1 = dict(w_in=l1_w_in, g_qa=l1_g_qa, w_uq=l1_w_uq, g_kva=l1_g_kva, w_ukv=l1_w_ukv, g_q=l1_g_q, g_k=l1_g_k,
              w_o=l1_w_o, w_rg=l1_w_rg, b_rg=l1_b_rg, w_re=l1_w_re, b_re=l1_b_re, w1=l1_w1, w3=l1_w3, w2=l1_w2)

    cond8 = jnp.zeros((8, D), f32).at[0].set(c_ctx).at[1:1 + DEC_BATCH].set(c)
    mods0 = _mod_rows(_adaln(cond8, l0_w_ada, l0_b_ada))
    mods1 = _mod_rows(_adaln(cond8, l1_w_ada, l1_b_ada))

    xc0 = x_prompt.reshape(T_CTX, D)
    xl0 = x_sample.reshape(T_LAT, D)

    xc0m, xl0m, k_new, v_new = _l0_mixer(xc0, xl0, l0_g_norm1, mods0, p0, cache_l0_k, cache_l0_v)
    h0, route0 = _router(xc0m, xl0m, l0_g_norm2, mods0, p0)
    xc1, xl1 = _moe(h0, route0, p0, xc0m, xl0m, mods0)

    xc1m, xl1m, ckv_new, kpe_new = _l1_mixer(xc1, xl1, l1_g_norm1, mods1, p1, cache_l1_ckv, cache_l1_kpe)
    h1, route1 = _router(xc1m, xl1m, l1_g_norm2, mods1, p1)
    y_prompt, y_sample = _moe(h1, route1, p1, xc1m, xl1m, mods1)
    return (y_prompt.reshape(BATCH, SEQ, D), y_sample.reshape(DEC_BATCH, DEC_SEQ, D), k_new, v_new,
            ckv_new.reshape(BATCH, SEQ, C_KV_LORA), kpe_new.reshape(BATCH, SEQ, C_ROPE))
```

```python
import functools

import jax
import jax.numpy as jnp
import numpy as np
from jax import lax
from jax.experimental import pallas as pl
from jax.experimental.pallas import tpu as pltpu

f32 = jnp.float32
bf16 = jnp.bfloat16

D = 1024
BATCH, SEQ = 32, 256
DEC_BATCH, DEC_SEQ = 2, 1024
PAST = 512
T_CTX = BATCH * SEQ
T_LAT = DEC_BATCH * DEC_SEQ
T = T_CTX + T_LAT
GRID_W = 64
CHUNK = 128
WINDOW = 128
ROPE_THETA = 10000.0
EPS = 1e-6
NEG_INF = -1e30
LANES = 128
SUBS = D // LANES

A_WIDTH = 512
A_GROUPS = 4
B_HEADS, B_KV, B_GROUP, B_HD = 8, 2, 4, 64
B_SCALE = B_HD ** -0.5

C_HEADS, C_Q_LORA, C_KV_LORA, C_NOPE, C_ROPE, C_V = 16, 384, 256, 64, 32, 64
C_QK = C_NOPE + C_ROPE
C_SCALE = C_QK ** -0.5
ODD_IN_PAD = 768
SLOT = 128

N_GROUPS, EPG, N_EXPERTS, D_EXPERT = 4, 8, 32, 256

ROW_TILE = 512
BLOCK_ROWS = 1024
PROJ_ROWS = 512
MOE_TILE = 256
MOE_ROWS = 2 * T + N_EXPERTS * MOE_TILE
MOE_TILES = MOE_ROWS // MOE_TILE
VMEM_CAP = 56 * 1024 * 1024


def _cp(sem, vmem=VMEM_CAP):
    return pltpu.CompilerParams(dimension_semantics=sem, vmem_limit_bytes=vmem)


def _const_spec(shape):
    nd = len(shape)
    return pl.BlockSpec(shape, lambda *_: (0,) * nd, pipeline_mode=pl.Buffered(1))


def _sample_of_tile(i, tile):
    n_ctx = T_CTX // tile
    per_lat = DEC_SEQ // tile
    return jnp.where(i < n_ctx, 0, 1 + (i - n_ctx) // per_lat)


def _silu(x):
    return x * jax.nn.sigmoid(x)


def _rms_rows(x, g):
    return x * lax.rsqrt(jnp.mean(x * x, -1, keepdims=True) + EPS) * g


def _swap_pairs(x):
    lane = lax.broadcasted_iota(jnp.int32, x.shape, x.ndim - 1)
    nxt = pltpu.roll(x, x.shape[-1] - 1, x.ndim - 1)
    prv = pltpu.roll(x, 1, x.ndim - 1)
    return jnp.where((lane & 1) == 0, nxt, prv)


def _split_bf16(x):
    hi = x.astype(bf16)
    return hi, (x - hi.astype(f32)).astype(bf16)


def _adaln_kernel(c_ref, w_ref, b_ref, o_ref):
    s_hi, s_lo = _split_bf16(_silu(c_ref[...]))
    w_hi, w_lo = _split_bf16(w_ref[...])
    o_ref[...] = (jnp.dot(s_hi, w_hi, preferred_element_type=f32) + jnp.dot(s_lo, w_hi, preferred_element_type=f32)
                  + jnp.dot(s_hi, w_lo, preferred_element_type=f32) + b_ref[...])


def _adaln(cond8, w, b):
    n = w.shape[1]
    tn = 1536
    return pl.pallas_call(
        _adaln_kernel,
        grid=(n // tn,),
        in_specs=[_const_spec((8, D)), pl.BlockSpec((D, tn), lambda j: (0, j)),
                  pl.BlockSpec((1, tn), lambda j: (0, j))],
        out_specs=pl.BlockSpec((8, tn), lambda j: (0, j)),
        out_shape=jax.ShapeDtypeStruct((8, n), f32),
        compiler_params=_cp(("arbitrary",)),
        name="adaln",
    )(cond8, w, b.reshape(1, n))


def _mod_rows(m8):
    m = m8[:3].reshape(3, 6, D)
    return jnp.pad(m, ((0, 0), (0, 2), (0, 0))).reshape(24, D)


N_CTX_TILES = T_CTX // ROW_TILE


def _token_specs(width):
    return [pl.BlockSpec((ROW_TILE, width), lambda i: (jnp.minimum(i, N_CTX_TILES - 1), 0)),
            pl.BlockSpec((ROW_TILE, width), lambda i: (jnp.maximum(i - N_CTX_TILES, 0), 0))]


def _token_rows(xc_ref, xl_ref):
    return jnp.where(pl.program_id(0) < N_CTX_TILES, xc_ref[...], xl_ref[...])


def _store_token_major(ref, x):
    n = x.shape[0]
    for s in range(SUBS):
        ref[pl.ds(s, n, stride=SUBS), :] = x[:, s * LANES:(s + 1) * LANES]


def _load_token_major(ref, n, row0=0):
    return jnp.concatenate([ref[pl.ds(row0 * SUBS + s, n, stride=SUBS), :] for s in range(SUBS)], axis=1)


def _token_major_spec(rows, index_map):
    return pl.BlockSpec((rows * SUBS, LANES), index_map)


def _rope_tables(n, rot_dim, lanes, lane0, copies=1):
    rows_count = n // GRID_W
    rows = np.repeat(np.arange(rows_count), GRID_W).astype(np.float64)
    cols = np.tile(np.arange(GRID_W), rows_count).astype(np.float64)
    d_axis = rot_dim // 2
    inv = ROPE_THETA ** (-np.arange(0, d_axis, 2, dtype=np.float64) / d_axis)
    ang = np.concatenate([rows[:, None] * inv, cols[:, None] * inv], -1)
    c = np.ones((n, lanes), np.float32)
    s = np.zeros((n, lanes), np.float32)
    for j in range(copies):
        lo = lane0 + j * rot_dim
        c[:, lo:lo + rot_dim] = np.repeat(np.cos(ang), 2, axis=1)
        s[:, lo:lo + rot_dim] = np.repeat(np.sin(ang), 2, axis=1) * np.tile(np.array([-1.0, 1.0]), rot_dim // 2)
    return jnp.asarray(c), jnp.asarray(s)


L0_Q0 = 2 * A_WIDTH
L0_K0 = L0_Q0 + B_HEADS * LANES
L0_V0 = L0_K0 + B_KV * B_HD
L0_IN = L0_V0 + B_KV * B_HD


def _l0_kernel(*refs, latent):
    if latent:
        (sink_ref, x_ref, gn_ref, mod_ref, win_ref, gvn_ref, ws_ref, bsb_ref, gq_ref, gk_ref, wo_ref,
         cos_ref, sin_ref, kc_ref, vc_ref, xo_ref, zs, cat, qs, ks, vt, kcb, vct) = refs
        key_off = WINDOW
    else:
        (sink_ref, x_ref, gn_ref, mod_ref, win_ref, gvn_ref, ws_ref, bsb_ref, gq_ref, gk_ref, wo_ref,
         xo_ref, ko_ref, vo_ref, zs, cat, qs, ks, vt, kf, vf) = refs
        key_off = 0
    n = BLOCK_ROWS
    n_chunks = n // CHUNK
    low = lax.broadcasted_iota(jnp.int32, (CHUNK, LANES), 1) < B_HD

    if latent:
        zpad = jnp.zeros((WINDOW, LANES), bf16)
        for c0 in (0, 1 + n_chunks):
            ks[c0 * CHUNK:(c0 + 1) * CHUNK, :] = zpad
            vt[c0] = zpad
        kcb[...] = kc_ref[...].astype(bf16)
        for i in range(PAST // CHUNK):
            vct[i] = vc_ref[i * CHUNK:(i + 1) * CHUNK, :].T.astype(bf16)

    def project(c, carry):
        r = pl.ds(pl.multiple_of(c * PROJ_ROWS, PROJ_ROWS), PROJ_ROWS)
        h = _rms_rows(x_ref[r, :], gn_ref[...]) * (1.0 + mod_ref[1:2, :]) + mod_ref[0:1, :]
        zs[r, :] = jnp.dot(h.astype(bf16), win_ref[...], preferred_element_type=f32)
        return carry

    lax.fori_loop(0, n // PROJ_ROWS, project, 0)

    def prepare(c, carry):
        r = pl.ds(pl.multiple_of(c * CHUNK, CHUNK), CHUNK)
        u = jax.nn.gelu(zs[r, 0:A_WIDTH])
        v = jax.nn.gelu(zs[r, A_WIDTH:2 * A_WIDTH])
        mu = jnp.mean(v, -1, keepdims=True)
        var = jnp.mean(jnp.square(v - mu), -1, keepdims=True)
        vn = ((v - mu) * lax.rsqrt(var + EPS) * gvn_ref[...]).astype(bf16)
        for g in range(A_GROUPS):
            cs = slice(g * CHUNK, (g + 1) * CHUNK)
            mixed = jnp.dot(ws_ref[g], vn[:, cs], preferred_element_type=f32) + bsb_ref[g]
            cat[r, cs] = (u[:, cs] * mixed).astype(bf16)
        if latent:
            cs_, sn_ = cos_ref[r, :], sin_ref[r, :]
        for h in range(B_HEADS):
            hs = slice(h * LANES, (h + 1) * LANES)
            qh = zs[r, L0_Q0 + h * LANES:L0_Q0 + (h + 1) * LANES]
            qh = qh * lax.rsqrt(jnp.sum(qh * qh, -1, keepdims=True) * (1.0 / B_HD) + EPS) * gq_ref[:, hs]
            if latent:
                qh = qh * cs_ + _swap_pairs(qh) * sn_
            qs[r, hs] = (qh * B_SCALE).astype(bf16)
        k = zs[r, L0_K0:L0_K0 + LANES]
        k2 = k * k
        s0 = jnp.sum(jnp.where(low, k2, 0.0), -1, keepdims=True)
        s1 = jnp.sum(jnp.where(low, 0.0, k2), -1, keepdims=True)
        k = k * lax.rsqrt(jnp.where(low, s0, s1) * (1.0 / B_HD) + EPS) * gk_ref[...]
        vv = zs[r, L0_V0:L0_V0 + LANES]
        if latent:
            k = k * cs_ + _swap_pairs(k) * sn_
        else:
            kf[r, :] = k
            vf[r, :] = vv
        kr = pl.ds(pl.multiple_of(c * CHUNK + key_off, CHUNK), CHUNK)
        ks[kr, :] = k.astype(bf16)
        vt[c + key_off // CHUNK] = vv.T.astype(bf16)
        return carry

    lax.fori_loop(0, n_chunks, prepare, 0)

    def attend(r, rows, key_sets):
        q = jnp.concatenate([qs[r, h * LANES:(h + 1) * LANES] for h in range(B_HEADS)], axis=0)
        sk = jnp.concatenate([jnp.full((1, rows), sink_ref[h], f32) for h in range(B_HEADS)], axis=1)
        scores = []
        m = sk
        for k, _, keep in key_sets:
            s = lax.dot_general(k, q, (((1,), (1,)), ((), ())), preferred_element_type=f32)
            if keep is not None:
                s = jnp.where(keep, s, NEG_INF)
            scores.append(s)
            m = jnp.maximum(m, jnp.max(s, 0, keepdims=True))
        den = jnp.exp(sk - m)
        ot = None
        for s, (_, vts, _) in zip(scores, key_sets):
            e = jnp.exp(s - m)
            den = den + jnp.sum(e, 0, keepdims=True)
            eb = e.astype(bf16)
            for i, v_t in enumerate(vts):
                pv = jnp.dot(v_t, eb[i * CHUNK:(i + 1) * CHUNK, :], preferred_element_type=f32)
                ot = pv if ot is None else ot + pv
        ot = ot * (1.0 / den)
        for pair in range(B_HEADS // 2):
            f0 = (2 * pair // B_GROUP) * B_HD
            pair_t = jnp.concatenate([ot[f0:f0 + B_HD, 2 * pair * rows:(2 * pair + 1) * rows],
                                      ot[f0:f0 + B_HD, (2 * pair + 1) * rows:(2 * pair + 2) * rows]], axis=0)
            cat[r, A_WIDTH + pair * LANES:A_WIDTH + (pair + 1) * LANES] = pair_t.T.astype(bf16)

    if latent:
        span = CHUNK + 2 * WINDOW

        def attend_block(c, carry):
            start = pl.multiple_of(c * CHUNK, CHUNK)
            kr = pl.ds(start, span)
            kj = lax.broadcasted_iota(jnp.int32, (span, B_HEADS * CHUNK), 0)
            qi = lax.broadcasted_iota(jnp.int32, (span, B_HEADS * CHUNK), 1) & (CHUNK - 1)
            kpos = start - WINDOW + kj
            keep = (jnp.abs(kj - WINDOW - qi) <= WINDOW) & (kpos >= 0) & (kpos < n)
            attend(pl.ds(start, CHUNK), CHUNK,
                   [(ks[kr, :], [vt[c + i] for i in range(span // CHUNK)], keep),
                    (kcb[...], [vct[i] for i in range(PAST // CHUNK)], None)])
            return carry

        lax.fori_loop(0, n_chunks, attend_block, 0)
    else:
        def attend_seq(sq, carry):
            r = pl.ds(pl.multiple_of(sq * SEQ, SEQ), SEQ)
            attend(r, SEQ, [(ks[r, :], [vt[sq * (SEQ // CHUNK) + i] for i in range(SEQ // CHUNK)], None)])
            ko_ref[sq] = kf[r, :].T
            vo_ref[sq] = vf[r, :].T
            return carry

        lax.fori_loop(0, n // SEQ, attend_seq, 0)

    def output(c, carry):
        r = pl.ds(pl.multiple_of(c * PROJ_ROWS, PROJ_ROWS), PROJ_ROWS)
        y = jnp.dot(cat[r, :], wo_ref[...], preferred_element_type=f32)
        xo_ref[r, :] = x_ref[r, :] + mod_ref[2:3, :] * y
        return carry

    lax.fori_loop(0, n // PROJ_ROWS, output, 0)


def _l0_mixer(xc, xl, g_norm, mods, p, cache_k, cache_v):
    w = p['w_in']
    q = w[:, 2 * A_WIDTH:2 * A_WIDTH + B_HEADS * B_HD].reshape(D, B_HEADS, B_HD)
    zero = jnp.zeros((D, B_GROUP, B_HD), f32)
    q_slots = jnp.concatenate([jnp.concatenate([q[:, :B_GROUP], zero], axis=2),
                               jnp.concatenate([zero, q[:, B_GROUP:]], axis=2)], axis=1).reshape(D, B_HEADS * LANES)
    win = jnp.concatenate([w[:, :2 * A_WIDTH], q_slots, w[:, 2 * A_WIDTH + B_HEADS * B_HD:]], axis=1).astype(bf16)
    zg = jnp.zeros((B_HD,), f32)
    gq = jnp.concatenate([jnp.tile(jnp.concatenate([p['g_q'], zg]), B_GROUP),
                          jnp.tile(jnp.concatenate([zg, p['g_q']]), B_GROUP)]).reshape(1, B_HEADS * LANES)
    gk = jnp.tile(p['g_k'], B_KV).reshape(1, LANES)
    gvn = p['g_vnorm'].reshape(1, A_WIDTH)
    ws = p['w_s'].astype(bf16)
    bsb = jnp.broadcast_to(p['b_s'][:, :, None], (A_GROUPS, CHUNK, CHUNK))
    wo = p['w_o'].astype(bf16)
    weights = (g_norm.reshape(1, D),)
    consts = (win, gvn, ws, bsb, gq, gk, wo)
    c_specs = [_const_spec(a.shape) for a in consts]
    smem = pl.BlockSpec(memory_space=pltpu.SMEM)
    row = pl.BlockSpec((BLOCK_ROWS, D), lambda b: (b, 0))
    kv = pl.BlockSpec((BLOCK_ROWS // SEQ, LANES, SEQ), lambda b: (b, 0, 0))

    def scratch(pad):
        return [pltpu.VMEM((BLOCK_ROWS, L0_IN), f32), pltpu.VMEM((BLOCK_ROWS, D), bf16),
                pltpu.VMEM((BLOCK_ROWS, B_HEADS * LANES), bf16), pltpu.VMEM((BLOCK_ROWS + pad, LANES), bf16),
                pltpu.VMEM(((BLOCK_ROWS + pad) // CHUNK, LANES, CHUNK), bf16)]

    kv_shape = jax.ShapeDtypeStruct((BATCH, LANES, SEQ), f32)
    state = pltpu.VMEM((BLOCK_ROWS, LANES), f32)
    xo_ctx, k_t, v_t = pl.pallas_call(
        functools.partial(_l0_kernel, latent=False),
        grid=(T_CTX // BLOCK_ROWS,),
        in_specs=[smem, row, _const_spec((1, D)), pl.BlockSpec((8, D), lambda b: (0, 0))] + c_specs,
        out_specs=[row, kv, kv],
        out_shape=[jax.ShapeDtypeStruct((T_CTX, D), f32), kv_shape, kv_shape],
        scratch_shapes=scratch(0) + [state, state],
        compiler_params=_cp(("parallel",)), name="l0_mixer_ctx",
    )(p['sink'], xc, *weights, mods, *consts)
    k_new = k_t.reshape(BATCH, B_KV, B_HD, SEQ).transpose(0, 3, 1, 2)
    v_new = v_t.reshape(BATCH, B_KV, B_HD, SEQ).transpose(0, 3, 1, 2)

    cos, sin = _rope_tables(DEC_SEQ, B_HD, LANES, 0, copies=LANES // B_HD)
    cache = pl.BlockSpec((None, PAST, LANES), lambda b: (b, 0, 0))
    past = [pltpu.VMEM((PAST, LANES), bf16), pltpu.VMEM((PAST // CHUNK, LANES, CHUNK), bf16)]
    xo_lat = pl.pallas_call(
        functools.partial(_l0_kernel, latent=True),
        grid=(DEC_BATCH,),
        in_specs=[smem, row, _const_spec((1, D)), pl.BlockSpec((8, D), lambda b: (1 + b, 0))] + c_specs + [
                  _const_spec(cos.shape), _const_spec(sin.shape), cache, cache],
        out_specs=row,
        out_shape=jax.ShapeDtypeStruct((T_LAT, D), f32),
        scratch_shapes=scratch(2 * WINDOW) + past,
        compiler_params=_cp(("parallel",)), name="l0_mixer_lat",
    )(p['sink'], xl, *weights, mods, *consts, cos, sin,
      cache_k.reshape(DEC_BATCH, PAST, LANES), cache_v.reshape(DEC_BATCH, PAST, LANES))
    return xo_ctx, xo_lat, k_new, v_new


C_SLOTS = C_HEADS * SLOT
C_PAIRS = C_HEADS // 2
L1_ROWS = 256


def _l1_kernel(*refs, latent):
    if latent:
        (x_ref, gn_ref, mod_ref, win_ref, gqa_ref, wuq_ref, gq_ref, gkva_ref, wuk_ref, wuvt_ref, gk_ref,
         wo_ref, wuqs_ref, qcos_ref, qsin_ref, kcos_ref, ksin_ref, cckv_ref, ckpe_ref, xo_ref,
         zs, cat, qs, ks, vt, wide, wide2) = refs
        n_ctx = PAST
    else:
        (x_ref, gn_ref, mod_ref, win_ref, gqa_ref, wuq_ref, gq_ref, gkva_ref, wuk_ref, wuvt_ref, gk_ref,
         wo_ref, xo_ref, ckvo_ref, kpeo_ref, zs, cat, qs, ks, vt, wide) = refs
        n_ctx = 0
    n = BLOCK_ROWS
    nt_dims = (((1,), (1,)), ((), ()))

    def inv_rms(v):
        return lax.rsqrt(jnp.sum(v * v, -1, keepdims=True) * (1.0 / C_QK) + EPS)

    def expand_keys(ckv_n, kslot, kb, rope_rows):
        cb = ckv_n.astype(bf16)
        key_rows = pl.ds(pl.multiple_of(kb * L1_ROWS, L1_ROWS), L1_ROWS)
        wide[...] = jnp.dot(cb, wuk_ref[...], preferred_element_type=f32)
        if rope_rows is not None:
            kcos = kcos_ref[rope_rows, :]
            turned = _swap_pairs(kslot) * ksin_ref[rope_rows, :]
        for h in range(C_HEADS):
            kh = wide[:, h * SLOT:(h + 1) * SLOT] + kslot
            if rope_rows is not None:
                kh = inv_rms(kh) * (kh * kcos + turned)
            else:
                kh = kh * inv_rms(kh) * gk_ref[...]
            ks[h, key_rows, :] = kh.astype(bf16)
        v_t = lax.dot_general(wuvt_ref[...], cb, nt_dims, preferred_element_type=f32).astype(bf16)
        for pair in range(C_PAIRS):
            vt[pair, kb] = v_t[pair * LANES:(pair + 1) * LANES, :]

    if latent:
        def past_keys(c, carry):
            r = pl.ds(pl.multiple_of(c * L1_ROWS, L1_ROWS), L1_ROWS)
            expand_keys(cckv_ref[r, :], ckpe_ref[r, :], c, None)
            return carry

        lax.fori_loop(0, PAST // L1_ROWS, past_keys, 0)

    def project(c, carry):
        r = pl.ds(pl.multiple_of(c * PROJ_ROWS, PROJ_ROWS), PROJ_ROWS)
        h = _rms_rows(x_ref[r, :], gn_ref[...]) * (1.0 + mod_ref[1:2, :]) + mod_ref[0:1, :]
        zs[r, :] = jnp.dot(h.astype(bf16), win_ref[...], preferred_element_type=f32)
        return carry

    lax.fori_loop(0, n // PROJ_ROWS, project, 0)

    def prepare(c, carry):
        r = pl.ds(pl.multiple_of(c * L1_ROWS, L1_ROWS), L1_ROWS)
        qa = _rms_rows(zs[r, 0:C_Q_LORA], gqa_ref[...]).astype(bf16)
        wide[...] = jnp.dot(qa, wuq_ref[...], preferred_element_type=f32)
        if latent:
            wide2[...] = jnp.dot(qa, wuqs_ref[...], preferred_element_type=f32)
            qcos, qsin = qcos_ref[r, :], qsin_ref[r, :]
        for h in range(C_HEADS):
            hs = slice(h * SLOT, (h + 1) * SLOT)
            qh = wide[:, hs]
            if latent:
                qh = inv_rms(qh) * (qh * qcos + wide2[:, hs] * qsin)
            else:
                qh = qh * inv_rms(qh) * gq_ref[...]
            qs[h, r, :] = (qh * C_SCALE).astype(bf16)
        ckv_n = _rms_rows(zs[r, C_Q_LORA:C_Q_LORA + C_KV_LORA], gkva_ref[...])
        kslot = zs[r, C_Q_LORA + C_KV_LORA:ODD_IN_PAD]
        if not latent:
            ckvo_ref[r, :] = ckv_n
            kpeo_ref[r, :] = kslot
        expand_keys(ckv_n, kslot, c + n_ctx // L1_ROWS, r if latent else None)
        return carry

    lax.fori_loop(0, n // L1_ROWS, prepare, 0)

    low = lax.broadcasted_iota(jnp.int32, (2 * C_V, L1_ROWS), 0) < C_V
    n_kblocks = (n_ctx + n) // L1_ROWS
    pairs_per_step = 2 if latent else 4

    def attend(c, carry):
        r = pl.ds(pl.multiple_of(c * L1_ROWS, L1_ROWS), L1_ROWS)

        def values_t(pair, eb):
            if not latent:
                return jnp.dot(vt[pair, c], eb, preferred_element_type=f32)
            o_t = None
            for b in range(n_kblocks):
                pv = jnp.dot(vt[pair, b], eb[b * L1_ROWS:(b + 1) * L1_ROWS, :], preferred_element_type=f32)
                o_t = pv if o_t is None else o_t + pv
            return o_t

        def pairs_step(i, carry2):
            pairs = [i * pairs_per_step + j for j in range(pairs_per_step)]
            heads = [2 * p + hh for p in pairs for hh in range(2)]
            scores = [lax.dot_general(ks[h] if latent else ks[h, r, :], qs[h, r, :], nt_dims,
                                      preferred_element_type=f32) for h in heads]
            exps = [jnp.exp(s - jnp.max(s, 0, keepdims=True)) for s in scores]
            dens = [jnp.sum(e, 0, keepdims=True) for e in exps]
            outs = [values_t(h // 2, e.astype(bf16)) / den for h, e, den in zip(heads, exps, dens)]
            for j, pair in enumerate(pairs):
                cat[pair, r, :] = jnp.where(low, outs[2 * j], outs[2 * j + 1]).T.astype(bf16)
            return carry2

        return lax.fori_loop(0, C_PAIRS // pairs_per_step, pairs_step, carry)

    lax.fori_loop(0, n // L1_ROWS, attend, 0)

    def output(c, carry):
        r = pl.ds(pl.multiple_of(c * PROJ_ROWS, PROJ_ROWS), PROJ_ROWS)
        heads = jnp.concatenate([cat[pair, r, :] for pair in range(C_PAIRS)], axis=1)
        y = jnp.dot(heads, wo_ref[...], preferred_element_type=f32)
        xo_ref[r, :] = x_ref[r, :] + mod_ref[2:3, :] * y
        return carry

    lax.fori_loop(0, n // PROJ_ROWS, output, 0)


def _slot_cols(w, heads, width, lo, hi, lane0):
    k = w.shape[0]
    w3 = w.reshape(k, heads, width)[:, :, lo:hi]
    out = jnp.zeros((k, heads, SLOT), w.dtype).at[:, :, lane0:lane0 + (hi - lo)].set(w3)
    return out.reshape(k, heads * SLOT)


def _l1_mixer(xc, xl, g_norm, mods, p, cache_ckv, cache_kpe):
    w_in = jnp.zeros((D, ODD_IN_PAD), f32).at[:, :C_Q_LORA + C_KV_LORA].set(
        p['w_in'][:, :C_Q_LORA + C_KV_LORA]).at[
        :, C_Q_LORA + C_KV_LORA + C_NOPE:C_Q_LORA + C_KV_LORA + C_QK].set(p['w_in'][:, C_Q_LORA + C_KV_LORA:])
    wuq = _slot_cols(p['w_uq'], C_HEADS, C_QK, 0, C_QK, 0).astype(bf16)
    wuk = _slot_cols(p['w_ukv'], C_HEADS, C_NOPE + C_V, 0, C_NOPE, 0).astype(bf16)
    wuv_t = p['w_ukv'].reshape(C_KV_LORA, C_HEADS, C_NOPE + C_V)[:, :, C_NOPE:].reshape(
        C_KV_LORA, C_HEADS * C_V).T.astype(bf16)
    gq = jnp.zeros((1, SLOT), f32).at[0, :C_QK].set(p['g_q'])
    gk = jnp.zeros((1, SLOT), f32).at[0, :C_QK].set(p['g_k'])
    consts = (g_norm.reshape(1, D), w_in.astype(bf16), p['g_qa'].reshape(1, C_Q_LORA), wuq, gq,
              p['g_kva'].reshape(1, C_KV_LORA), wuk, wuv_t, gk, p['w_o'].astype(bf16))
    c_specs = [_const_spec(a.shape) for a in consts]
    row = pl.BlockSpec((BLOCK_ROWS, D), lambda b: (b, 0))
    n_ctx_blocks = T_CTX // BLOCK_ROWS

    def scratch(n_keys):
        return [pltpu.VMEM((BLOCK_ROWS, ODD_IN_PAD), f32), pltpu.VMEM((C_PAIRS, BLOCK_ROWS, LANES), bf16),
                pltpu.VMEM((C_HEADS, BLOCK_ROWS, SLOT), bf16), pltpu.VMEM((C_HEADS, n_keys, SLOT), bf16),
                pltpu.VMEM((C_PAIRS, n_keys // L1_ROWS, LANES, L1_ROWS), bf16),
                pltpu.VMEM((L1_ROWS, C_SLOTS), f32)]

    xo_ctx, ckv_new, kpe_slot = pl.pallas_call(
        functools.partial(_l1_kernel, latent=False),
        grid=(n_ctx_blocks,),
        in_specs=[row, c_specs[0], pl.BlockSpec((8, D), lambda b: (0, 0))] + c_specs[1:],
        out_specs=[row, pl.BlockSpec((BLOCK_ROWS, C_KV_LORA), lambda b: (b, 0)),
                   pl.BlockSpec((BLOCK_ROWS, SLOT), lambda b: (b, 0))],
        out_shape=[jax.ShapeDtypeStruct((T_CTX, D), f32), jax.ShapeDtypeStruct((T_CTX, C_KV_LORA), f32),
                   jax.ShapeDtypeStruct((T_CTX, SLOT), f32)],
        scratch_shapes=scratch(BLOCK_ROWS),
        compiler_params=_cp(("parallel",)), name="l1_mixer_ctx",
    )(xc, consts[0], mods, *consts[1:])

    cos, sin = _rope_tables(DEC_SEQ, C_ROPE, SLOT, C_NOPE)
    pair_swap = lambda a: a.reshape(a.shape[:-1] + (a.shape[-1] // 2, 2))[..., ::-1].reshape(a.shape)
    rope = (pair_swap(wuq), gq * cos, pair_swap(gq) * sin, gk * cos, pair_swap(gk) * sin)
    ckpe = jnp.zeros((DEC_BATCH, PAST, SLOT), f32).at[:, :, C_NOPE:C_QK].set(cache_kpe)
    xo_lat = pl.pallas_call(
        functools.partial(_l1_kernel, latent=True),
        grid=(DEC_BATCH,),
        in_specs=[pl.BlockSpec((BLOCK_ROWS, D), lambda b: (b, 0), pipeline_mode=pl.Buffered(1)), c_specs[0],
                  pl.BlockSpec((8, D), lambda b: (1 + b, 0))] + c_specs[1:] + [_const_spec(a.shape) for a in rope] + [
                  pl.BlockSpec((None, PAST, C_KV_LORA), lambda b: (b, 0, 0)),
                  pl.BlockSpec((None, PAST, SLOT), lambda b: (b, 0, 0))],
        out_specs=row,
        out_shape=jax.ShapeDtypeStruct((T_LAT, D), f32),
        scratch_shapes=scratch(PAST + BLOCK_ROWS) + [pltpu.VMEM((L1_ROWS, C_SLOTS), f32)],
        compiler_params=_cp(("parallel",)), name="l1_mixer_lat",
    )(xl, consts[0], mods, *consts[1:], *rope, cache_ckv, ckpe)
    return xo_ctx, xo_lat, ckv_new, kpe_slot[:, C_NOPE:C_QK]


ROUTER_ROWS = 40


def _router_kernel(xc_ref, xl_ref, gn_ref, mod_ref, whi_ref, wlo_ref, br_ref, h_ref, route_ref):
    h = _rms_rows(_token_rows(xc_ref, xl_ref), gn_ref[...]) * (1.0 + mod_ref[4:5, :]) + mod_ref[3:4, :]
    _store_token_major(h_ref, h)
    h_hi, h_lo = _split_bf16(h)
    nt = (((1,), (1,)), ((), ()))
    logits = (lax.dot_general(whi_ref[...], h_hi, nt, preferred_element_type=f32)
              + lax.dot_general(whi_ref[...], h_lo, nt, preferred_element_type=f32)
              + lax.dot_general(wlo_ref[...], h_hi, nt, preferred_element_type=f32))
    logits = logits[0:ROUTER_ROWS, :] + br_ref[0:ROUTER_ROWS, :]
    row_i = lax.broadcasted_iota(jnp.int32, logits.shape, 0)
    row = row_i.astype(f32)
    big = 1e6
    is_g = (row_i >= N_EXPERTS) & (row_i < N_EXPERTS + N_GROUPS)
    lg = jnp.where(is_g, logits, -jnp.inf)
    mg = jnp.max(lg, 0, keepdims=True)
    gsel = jnp.min(jnp.where(lg == mg, row, big), 0, keepdims=True) - N_EXPERTS
    pg_sel = 1.0 / jnp.sum(jnp.where(is_g, jnp.exp(lg - mg), 0.0), 0, keepdims=True)
    in_grp = (row_i < N_EXPERTS) & ((row_i >> 3).astype(f32) == gsel)
    le = jnp.where(in_grp, logits, -jnp.inf)
    m1 = jnp.max(le, 0, keepdims=True)
    i1 = jnp.min(jnp.where(le == m1, row, big), 0, keepdims=True)
    le2 = jnp.where(row == i1, -jnp.inf, le)
    m2 = jnp.max(le2, 0, keepdims=True)
    i2 = jnp.min(jnp.where(le2 == m2, row, big), 0, keepdims=True)
    e2 = jnp.exp(m2 - m1)
    w1 = pg_sel / (1.0 + e2)
    w2 = pg_sel * e2 / (1.0 + e2)
    sub = lax.broadcasted_iota(jnp.int32, route_ref.shape, 0)
    route_ref[...] = jnp.where(sub == 0, i1, jnp.where(sub == 1, i2, jnp.where(sub == 2, w1,
                                                                                jnp.where(sub == 3, w2, 0.0))))


def _router(xc, xl, g_norm, mods, p):
    wr = jnp.zeros((128, D), f32).at[:N_EXPERTS].set(p['w_re'].T).at[
        N_EXPERTS:N_EXPERTS + N_GROUPS].set(p['w_rg'].T)
    w_hi, w_lo = _split_bf16(wr)
    br =jnp.zeros((128, 1), f32).at[:N_EXPERTS, 0].set(p['b_re']).at[
        N_EXPERTS:N_EXPERTS + N_GROUPS, 0].set(p['b_rg'])
    return pl.pallas_call(
        _router_kernel,
        grid=(T // ROW_TILE,),
        in_specs=_token_specs(D) + [
                  _const_spec((1, D)),
                  pl.BlockSpec((8, D), lambda i: (_sample_of_tile(i, ROW_TILE), 0)),
                  _const_spec((128, D)), _const_spec((128, D)), _const_spec((128, 1))],
        out_specs=[_token_major_spec(ROW_TILE, lambda i: (i, 0)), pl.BlockSpec((8, ROW_TILE), lambda i: (0, i))],
        out_shape=[jax.ShapeDtypeStruct((T * SUBS, LANES), f32), jax.ShapeDtypeStruct((8, T), f32)],
        compiler_params=_cp(("parallel",)), name="router",
    )(xc, xl, g_norm.reshape(1, D), mods, w_hi, w_lo, br)


PLAN_FIRST_TILE, PLAN_TILES, PLAN_N_USED = 0, 1, 2


def _plan_kernel(rt_ref, pos_ref, plan_ref, rank):
    n_blk = T // 128
    e_col = lax.broadcasted_iota(jnp.int32, (N_EXPERTS, 128), 0).astype(f32)
    ri = lax.broadcasted_iota(jnp.int32, (128, 128), 0)
    ci = lax.broadcasted_iota(jnp.int32, (128, 128), 1)
    before = jnp.where(ri < ci, 1.0, 0.0).astype(bf16)

    def picks(b):
        cs = slice(b * 128, (b + 1) * 128)
        return rt_ref[0:1, cs] == e_col, rt_ref[1:2, cs] == e_col

    counts = jnp.zeros((N_EXPERTS, 1), f32)
    for b in range(n_blk):
        m0, m1 = picks(b)
        m = jnp.where(m0, 1.0, 0.0) + jnp.where(m1, 1.0, 0.0)
        rank[:, b * 128:(b + 1) * 128] = jnp.dot(m.astype(bf16), before, preferred_element_type=f32) + counts
        counts = counts + jnp.sum(m, axis=1, keepdims=True)

    tiles = jnp.floor((counts + (MOE_TILE - 1.0)) * (1.0 / MOE_TILE))
    er = lax.broadcasted_iota(jnp.int32, (N_EXPERTS, N_EXPERTS), 0)
    ec = lax.broadcasted_iota(jnp.int32, (N_EXPERTS, N_EXPERTS), 1)
    earlier = jnp.where(ec < er, 1.0, 0.0).astype(bf16)
    tile_start = jnp.dot(earlier, jnp.broadcast_to(tiles, (N_EXPERTS, 128)).astype(bf16),
                         preferred_element_type=f32)
    row_start = tile_start * MOE_TILE

    sub = lax.broadcasted_iota(jnp.int32, (8, 128), 0)
    for b in range(n_blk):
        m0, m1 = picks(b)
        base = rank[:, b * 128:(b + 1) * 128] + row_start
        p0 = jnp.sum(jnp.where(m0, base, 0.0), axis=0, keepdims=True)
        p1 = jnp.sum(jnp.where(m1, base, 0.0), axis=0, keepdims=True)
        pos_ref[:, b * 128:(b + 1) * 128] = jnp.where(sub == 0, p0, jnp.where(sub == 1, p1, 0.0)).astype(jnp.int32)

    tile_end = tile_start + tiles
    n_used = jnp.max(tile_end, axis=0, keepdims=True)
    diag =(lax.broadcasted_iota(jnp.int32, (N_EXPERTS, 128), 0)
            == lax.broadcasted_iota(jnp.int32, (N_EXPERTS, 128), 1))
    first = jnp.sum(jnp.where(diag, tile_start, 0.0), axis=0, keepdims=True)
    count = jnp.sum(jnp.where(diag, tiles, 0.0), axis=0, keepdims=True)
    rows = jnp.where(sub == PLAN_FIRST_TILE, first,
                     jnp.where(sub == PLAN_TILES, count, jnp.where(sub == PLAN_N_USED, n_used, 0.0)))
    plan_ref[...] = rows.astype(jnp.int32)


def _slot_code(t, k):
    return t * SUBS + k * (SUBS // 2)


def _code_offset(code):
    return pl.multiple_of(code & ~(SUBS - 1), SUBS)


def _code_gate_index(code):
    return code >> 2


PAD_CODE = T * SUBS


def _invert_kernel(pos_ref, plan_ref, code_ref):
    def pad_tile(tile, carry):
        for u in range(MOE_TILE):
            code_ref[tile * MOE_TILE + u] = PAD_CODE
        return carry

    def pad_last_tile(e, carry):
        return pad_tile(jnp.maximum(plan_ref[PLAN_FIRST_TILE, e] + plan_ref[PLAN_TILES, e] - 1, 0), carry)
    lax.fori_loop(0, N_EXPERTS, pad_last_tile, 0)
    lax.fori_loop(plan_ref[PLAN_N_USED, 0], MOE_TILES, pad_tile, 0)

    group = 32
    for k in range(2):
        def place(i, carry):
            t0 = i * group
            slots = [pos_ref[k * T + t0 + u] for u in range(group)]
            for u, s in enumerate(slots):
                code_ref[s] = _slot_code(t0 + u, k)
            return carry
        lax.fori_loop(0, T // group, place, 0)


def _route_plan(route_t):
    pos, plan = pl.pallas_call(
        _plan_kernel,
        out_shape=[jax.ShapeDtypeStruct((8, T), jnp.int32), jax.ShapeDtypeStruct((8, LANES), jnp.int32)],
        scratch_shapes=[pltpu.VMEM((N_EXPERTS, T), f32)],
        compiler_params=_cp(None), name="route_plan",
    )(route_t)
    smem = pl.BlockSpec(memory_space=pltpu.SMEM)
    codes = pl.pallas_call(
        _invert_kernel,
        in_specs=[smem, smem], out_specs=smem,
        out_shape=jax.ShapeDtypeStruct((MOE_ROWS,), jnp.int32),
        name="route_invert",
    )(pos[0:2].reshape(2 * T), plan)
    gates = jnp.pad(route_t[2:4].T.reshape(2 * T), (0, 8))
    return plan[PLAN_FIRST_TILE, :N_EXPERTS], plan[PLAN_TILES, :N_EXPERTS], codes, gates


def _tile_index(i):
    return jnp.minimum(i, MOE_TILES - 1)


TM_ROWS = T * SUBS
SCATTER_GROUP = 16
STAGES = 4
LAST_EXPERT = N_EXPERTS - 1


def _tile_rows(g):
    return pl.ds(pl.multiple_of(g * MOE_TILE, MOE_TILE), MOE_TILE)


def _expert_tile_pairs(first, count, tile_step):
    def pair(pp, carry):
        for parity in range(2):
            g = 2 * pp + parity
            pl.when((g >= first) & (g < first + count))(functools.partial(tile_step, g, parity))
        return carry
    lax.fori_loop(first // 2, (first + count + 1) // 2, pair, 0)


def _gather_tile(code_ref, tile, xs, gbuf):
    base = tile * MOE_TILE
    for r in range(MOE_TILE):
        gbuf[r * SUBS:(r + 1) * SUBS, :] = xs[pl.ds(_code_offset(code_ref[base + r]), SUBS), :]


def _moe_up_kernel(first_ref, count_ref, code_ref, h_hbm, w1_ref, w3_ref, hh_hbm,
                   xs, gbuf_a, gbuf_b, w13, obuf, sem_x, sem_o):
    e = pl.program_id(0)
    first, count = first_ref[e], count_ref[e]
    gbufs = (gbuf_a, gbuf_b)

    def out_copy(slot, g):
        return pltpu.make_async_copy(obuf.at[slot], hh_hbm.at[_tile_rows(g), :], sem_o.at[slot])

    @pl.when(e == 0)
    def _():
        cp = pltpu.make_async_copy(h_hbm, xs.at[pl.ds(0, TM_ROWS), :], sem_x)
        cp.start()
        xs[TM_ROWS:TM_ROWS + SUBS, :] = jnp.zeros((SUBS, LANES), f32)
        cp.wait()
        _gather_tile(code_ref, 0, xs, gbuf_a)

    @pl.when(count > 0)
    def _():
        w13[:, :D_EXPERT] = w1_ref[0].astype(bf16)
        w13[:, D_EXPERT:] = w3_ref[0].astype(bf16)

    def tile_step(g, parity):
        _gather_tile(code_ref, _tile_index(g + 1), xs, gbufs[1 - parity])
        x3 = jnp.swapaxes(gbufs[parity][...].reshape(MOE_TILE, SUBS, LANES), 0, 1)
        x = jnp.concatenate([x3[s] for s in range(SUBS)], axis=1).astype(bf16)
        h13 = jnp.dot(x, w13[...], preferred_element_type=f32)
        hh = (_silu(h13[:, :D_EXPERT]) * h13[:, D_EXPERT:]).astype(bf16)

        slot = g % STAGES

        @pl.when(g >= STAGES)
        def _():
            out_copy(slot, g).wait()
        obuf[slot] = hh
        out_copy(slot, g).start()

    _expert_tile_pairs(first, count, tile_step)

    @pl.when(e == LAST_EXPERT)
    def _():
        n_used = first + count
        for slot in range(STAGES):
            pl.when(n_used > slot)(lambda slot=slot: out_copy(slot, 0).wait())
        obuf[0] = jnp.zeros((MOE_TILE, D_EXPERT), bf16)

        def zero_tile(g, carry):
            cp = out_copy(0, g)
            cp.start()
            cp.wait()
            return carry
        lax.fori_loop(n_used, MOE_TILES, zero_tile, 0)


def _scatter_tile(code_ref, gate_ref, tile, ybuf, acc):
    base = tile * MOE_TILE
    for g0 in range(0, MOE_TILE, SCATTER_GROUP):
        rows = range(g0, g0 + SCATTER_GROUP)
        codes = [code_ref[base + r] for r in rows]
        new = [acc[pl.ds(_code_offset(c), SUBS), :]
               + gate_ref[_code_gate_index(c)] * ybuf[r * SUBS:(r + 1) * SUBS, :]
               for r, c in zip(rows, codes)]
        for c, v in zip(codes, new):
            acc[pl.ds(_code_offset(c), SUBS), :] = v


RES_ROWS = 256


def _residual_out(x_hbm, y_hbm, tok0, sample_of_chunk, mod_ref, acc, rin, rout, sem_r, sem_w):
    n_chunks = x_hbm.shape[0] // RES_ROWS

    def rows(c):
        return pl.ds(pl.multiple_of(c * RES_ROWS, RES_ROWS), RES_ROWS)

    def in_copy(slot, c):
        return pltpu.make_async_copy(x_hbm.at[rows(c), :], rin.at[slot], sem_r.at[slot])

    def out_copy(slot, c):
        return pltpu.make_async_copy(rout.at[slot], y_hbm.at[rows(c), :], sem_w.at[slot])

    for c in range(STAGES - 1):
        in_copy(c, c).start()

    def ring(cc, carry):
        for slot in range(STAGES):
            c = STAGES * cc + slot
            in_copy(slot, c).wait()
            ahead = c + STAGES - 1

            @pl.when(ahead < n_chunks)
            def _():
                in_copy((slot + STAGES - 1) % STAGES, ahead).start()

            @pl.when(c >= STAGES)
            def _():
                out_copy(slot, c).wait()
            delta = _load_token_major(acc, RES_ROWS, tok0 + c * RES_ROWS)
            gate = mod_ref[pl.ds(sample_of_chunk(c) * 8 + 5, 1), :]
            rout[slot] = rin[slot] + gate * delta
            out_copy(slot, c).start()
        return carry

    lax.fori_loop(0, n_chunks // STAGES, ring, 0)
    for slot in range(STAGES):
        out_copy(slot, 0).wait()


def _moe_down_kernel(first_ref, count_ref, code_ref, gate_ref, hh_hbm, w2_ref, xc_hbm, xl_hbm, mod_ref,
                     yc_hbm, yl_hbm, acc, ybuf_a, ybuf_b, w2b, ibuf, rin, rout, sem_i, sem_r, sem_w):
    e = pl.program_id(0)
    first, count = first_ref[e], count_ref[e]
    n_used = first_ref[LAST_EXPERT] + count_ref[LAST_EXPERT]
    ybufs = (ybuf_a, ybuf_b)

    def in_copy(slot, g):
        return pltpu.make_async_copy(hh_hbm.at[_tile_rows(g), :], ibuf.at[slot], sem_i.at[slot])

    @pl.when(e == 0)
    def _():
        for g in range(STAGES - 1):
            in_copy(g, g).start()

        def zero(c, carry):
            acc[pl.ds(pl.multiple_of(c * 1024, 1024), 1024), :] = jnp.zeros((1024, LANES), f32)
            return carry
        lax.fori_loop(0, TM_ROWS // 1024, zero, 0)
        acc[TM_ROWS:TM_ROWS + SUBS, :] = jnp.zeros((SUBS, LANES), f32)
        ybuf_b[...] = jnp.zeros_like(ybuf_b)

    @pl.when(count > 0)
    def _():
        w2b[...] = w2_ref[0].astype(bf16)

    def tile_step(g, parity):
        slot = g % STAGES
        in_copy(slot, g).wait()
        ahead = g + STAGES - 1

        @pl.when(ahead < n_used)
        def _():
            in_copy(ahead % STAGES, ahead).start()
        _store_token_major(ybufs[parity], jnp.dot(ibuf[slot], w2b[...], preferred_element_type=f32))
        _scatter_tile(code_ref, gate_ref, jnp.maximum(g - 1, 0), ybufs[1 - parity], acc)

    _expert_tile_pairs(first, count, tile_step)

    @pl.when(e == LAST_EXPERT)
    def _():
        for parity in range(2):
            pl.when((n_used > 0) & ((n_used - 1) % 2 == parity))(
                functools.partial(_scatter_tile, code_ref, gate_ref, n_used - 1, ybufs[parity], acc))
        _residual_out(xc_hbm, yc_hbm, 0, lambda c: 0, mod_ref, acc, rin, rout, sem_r, sem_w)
        _residual_out(xl_hbm, yl_hbm, T_CTX, lambda c: 1 + c // (DEC_SEQ // RES_ROWS),
                      mod_ref, acc, rin, rout, sem_r, sem_w)


def _moe(h_tm, route_t, p, xc, xl, mods):
    first_tile, n_tiles, codes, gates = _route_plan(route_t)
    tile_rows = pltpu.VMEM((MOE_TILE * SUBS, LANES), f32)
    staging = pltpu.VMEM((STAGES, MOE_TILE, D_EXPERT), bf16)
    res_rows = pltpu.VMEM((STAGES, RES_ROWS, D), f32)
    hbm = pl.BlockSpec(memory_space=pl.ANY)
    hh = pl.pallas_call(
        _moe_up_kernel,
        grid_spec=pltpu.PrefetchScalarGridSpec(
            num_scalar_prefetch=3, grid=(N_EXPERTS,),
            in_specs=[pl.BlockSpec(memory_space=pl.ANY),
                      pl.BlockSpec((1, D, D_EXPERT), lambda e, f, n, c: (e, 0, 0)),
                      pl.BlockSpec((1, D, D_EXPERT), lambda e, f, n, c: (e, 0, 0))],
            out_specs=pl.BlockSpec(memory_space=pl.ANY),
            scratch_shapes=[pltpu.VMEM((TM_ROWS + SUBS, LANES), f32), tile_rows, tile_rows,
                            pltpu.VMEM((D, 2 * D_EXPERT), bf16), staging,
                            pltpu.SemaphoreType.DMA(()), pltpu.SemaphoreType.DMA((STAGES,))]),
        out_shape=jax.ShapeDtypeStruct((MOE_ROWS, D_EXPERT), bf16),
        compiler_params=_cp(("arbitrary",)), name="moe_up",
    )(first_tile, n_tiles, codes, h_tm, p['w1'], p['w3'])
    return pl.pallas_call(
        _moe_down_kernel,
        grid_spec=pltpu.PrefetchScalarGridSpec(
            num_scalar_prefetch=4, grid=(N_EXPERTS,),
            in_specs=[hbm, pl.BlockSpec((1, D_EXPERT, D), lambda e, f, n, c, g: (e, 0, 0)), hbm, hbm,
                      pl.BlockSpec((24, D), lambda e, f, n, c, g: (0, 0), pipeline_mode=pl.Buffered(1))],
            out_specs=[hbm, hbm],
            scratch_shapes=[pltpu.VMEM((TM_ROWS + SUBS, LANES), f32), tile_rows, tile_rows,
                            pltpu.VMEM((D_EXPERT, D), bf16), staging, res_rows, res_rows,
                            pltpu.SemaphoreType.DMA((STAGES,)), pltpu.SemaphoreType.DMA((STAGES,)),
                            pltpu.SemaphoreType.DMA((STAGES,))]),
        out_shape=[jax.ShapeDtypeStruct((T_CTX, D), f32), jax.ShapeDtypeStruct((T_LAT, D), f32)],
        compiler_params=_cp(("arbitrary",)), name="moe_down",
    )(first_tile, n_tiles, codes, gates, hh, p['w2'], xc, xl, mods)


def kernel(x_prompt, x_sample, cache_l0_k, cache_l0_v, cache_l1_ckv, cache_l1_kpe, c, c_ctx, l0_g_norm1, l0_g_norm2, l0_w_ada, l0_b_ada, l0_w_in, l0_g_vnorm, l0_w_s, l0_b_s, l0_g_q, l0_g_k, l0_sink, l0_w_o, l0_w_rg, l0_b_rg, l0_w_re, l0_b_re, l0_w1, l0_w3, l0_w2, l1_g_norm1, l1_g_norm2, l1_w_ada, l1_b_ada, l1_w_in, l1_g_qa, l1_w_uq, l1_g_kva, l1_w_ukv, l1_g_q, l1_g_k, l1_w_o, l1_w_rg, l1_b_rg, l1_w_re, l1_b_re, l1_w1, l1_w3, l1_w2):
    p0 = dict(w_in=l0_w_in, g_vnorm=l0_g_vnorm, w_s=l0_w_s, b_s=l0_b_s, g_q=l0_g_q, g_k=l0_g_k, sink=l0_sink,
              w_o=l0_w_o, w_rg=l0_w_rg, b_rg=l0_b_rg, w_re=l0_w_re, b_re=l0_b_re, w1=l0_w1, w3=l0_w3, w2=l0_w2)
    p1 = dict(w_in=l1_w_in, g_qa=l1_g_qa, w_uq=l1_w_uq, g_kva=l1_g_kva, w_ukv=l1_w_ukv, g_q=l1_g_q, g_k=l1_g_k,
              w_o=l1_w_o, w_rg=l1_w_rg, b_rg=l1_b_rg, w_re=l1_w_re, b_re=l1_b_re, w1=l1_w1, w3=l1_w3, w2=l1_w2)

    cond8 = jnp.zeros((8, D), f32).at[0].set(c_ctx).at[1:1 + DEC_BATCH].set(c)
    mods0 = _mod_rows(_adaln(cond8, l0_w_ada, l0_b_ada))
    mods1 = _mod_rows(_adaln(cond8, l1_w_ada, l1_b_ada))

    xc0 = x_prompt.reshape(T_CTX, D)
    xl0 = x_sample.reshape(T_LAT, D)

    xc0m, xl0m, k_new, v_new = _l0_mixer(xc0, xl0, l0_g_norm1, mods0, p0, cache_l0_k, cache_l0_v)
    h0, route0 = _router(xc0m, xl0m, l0_g_norm2, mods0, p0)
    xc1, xl1 = _moe(h0, route0, p0, xc0m, xl0m, mods0)

    xc1m, xl1m, ckv_new, kpe_new = _l1_mixer(xc1, xl1, l1_g_norm1, mods1, p1, cache_l1_ckv, cache_l1_kpe)
    h1, route1 = _router(xc1m, xl1m, l1_g_norm2, mods1, p1)
    y_prompt, y_sample = _moe(h1, route1, p1, xc1m, xl1m, mods1)
    return (y_prompt.reshape(BATCH, SEQ, D), y_sample.reshape(DEC_BATCH, DEC_SEQ, D), k_new, v_new,
            ckv_new.reshape(BATCH, SEQ, C_KV_LORA), kpe_new.reshape(BATCH, SEQ, C_ROPE))
```

```python
import functools

import jax
import jax.numpy as jnp
import numpy as np
from jax import lax
from jax.experimental import pallas as pl
from jax.experimental.pallas import tpu as pltpu

f32 = jnp.float32
bf16 = jnp.bfloat16

D = 1024
BATCH, SEQ = 32, 256
DEC_BATCH, DEC_SEQ = 2, 1024
PAST = 512
T_CTX = BATCH * SEQ
T_LAT = DEC_BATCH * DEC_SEQ
T = T_CTX + T_LAT
GRID_W = 64
CHUNK = 128
WINDOW = 128
ROPE_THETA = 10000.0
EPS = 1e-6
NEG_INF = -1e30
LANES = 128
SUBS = D // LANES

A_WIDTH = 512
A_GROUPS = 4
B_HEADS, B_KV, B_GROUP, B_HD = 8, 2, 4, 64
B_SCALE = B_HD ** -0.5

C_HEADS, C_Q_LORA, C_KV_LORA, C_NOPE, C_ROPE, C_V = 16, 384, 256, 64, 32, 64
C_QK = C_NOPE + C_ROPE
C_SCALE = C_QK ** -0.5
ODD_IN_PAD = 768
SLOT = 128

N_GROUPS, EPG, N_EXPERTS, D_EXPERT = 4, 8, 32, 256

ROW_TILE = 512
BLOCK_ROWS = 1024
PROJ_ROWS = 512
MOE_TILE = 256
MOE_ROWS = 2 * T + N_EXPERTS * MOE_TILE
MOE_TILES = MOE_ROWS // MOE_TILE
VMEM_CAP = 56 * 1024 * 1024


def _cp(sem, vmem=VMEM_CAP):
    return pltpu.CompilerParams(dimension_semantics=sem, vmem_limit_bytes=vmem)


def _const_spec(shape):
    nd = len(shape)
    return pl.BlockSpec(shape, lambda *_: (0,) * nd, pipeline_mode=pl.Buffered(1))


def _sample_of_tile(i, tile):
    n_ctx = T_CTX // tile
    per_lat = DEC_SEQ // tile
    return jnp.where(i < n_ctx, 0, 1 + (i - n_ctx) // per_lat)


def _silu(x):
    return x * jax.nn.sigmoid(x)


def _rms_rows(x, g):
    return x * lax.rsqrt(jnp.mean(x * x, -1, keepdims=True) + EPS) * g


def _swap_pairs(x):
    lane = lax.broadcasted_iota(jnp.int32, x.shape, x.ndim - 1)
    nxt = pltpu.roll(x, x.shape[-1] - 1, x.ndim - 1)
    prv = pltpu.roll(x, 1, x.ndim - 1)
    return jnp.where((lane & 1) == 0, nxt, prv)


def _split_bf16(x):
    hi = x.astype(bf16)
    return hi, (x - hi.astype(f32)).astype(bf16)


def _adaln_kernel(c_ref, w_ref, b_ref, o_ref):
    s_hi, s_lo = _split_bf16(_silu(c_ref[...]))
    w_hi, w_lo = _split_bf16(w_ref[...])
    o_ref[...] = (jnp.dot(s_hi, w_hi, preferred_element_type=f32) + jnp.dot(s_lo, w_hi, preferred_element_type=f32)
                  + jnp.dot(s_hi, w_lo, preferred_element_type=f32) + b_ref[...])


def _adaln(cond8, w, b):
    n = w.shape[1]
    tn = 1536
    return pl.pallas_call(
        _adaln_kernel,
        grid=(n // tn,),
        in_specs=[_const_spec((8, D)), pl.BlockSpec((D, tn), lambda j: (0, j)),
                  pl.BlockSpec((1, tn), lambda j: (0, j))],
        out_specs=pl.BlockSpec((8, tn), lambda j: (0, j)),
        out_shape=jax.ShapeDtypeStruct((8, n), f32),
        compiler_params=_cp(("arbitrary",)),
        name="adaln",
    )(cond8, w, b.reshape(1, n))


def _mod_rows(m8):
    m = m8[:3].reshape(3, 6, D)
    return jnp.pad(m, ((0, 0), (0, 2), (0, 0))).reshape(24, D)


N_CTX_TILES = T_CTX // ROW_TILE


def _token_specs(width):
    return [pl.BlockSpec((ROW_TILE, width), lambda i: (jnp.minimum(i, N_CTX_TILES - 1), 0)),
            pl.BlockSpec((ROW_TILE, width), lambda i: (jnp.maximum(i - N_CTX_TILES, 0), 0))]


def _token_rows(xc_ref, xl_ref):
    return jnp.where(pl.program_id(0) < N_CTX_TILES, xc_ref[...], xl_ref[...])


def _store_token_major(ref, x):
    n = x.shape[0]
    for s in range(SUBS):
        ref[pl.ds(s, n, stride=SUBS), :] = x[:, s * LANES:(s + 1) * LANES]


def _load_token_major(ref, n, row0=0):
    return jnp.concatenate([ref[pl.ds(row0 * SUBS + s, n, stride=SUBS), :] for s in range(SUBS)], axis=1)


def _token_major_spec(rows, index_map):
    return pl.BlockSpec((rows * SUBS, LANES), index_map)


def _rope_tables(n, rot_dim, lanes, lane0, copies=1):
    rows_count = n // GRID_W
    rows = np.repeat(np.arange(rows_count), GRID_W).astype(np.float64)
    cols = np.tile(np.arange(GRID_W), rows_count).astype(np.float64)
    d_axis = rot_dim // 2
    inv = ROPE_THETA ** (-np.arange(0, d_axis, 2, dtype=np.float64) / d_axis)
    ang = np.concatenate([rows[:, None] * inv, cols[:, None] * inv], -1)
    c = np.ones((n, lanes), np.float32)
    s = np.zeros((n, lanes), np.float32)
    for j in range(copies):
        lo = lane0 + j * rot_dim
        c[:, lo:lo + rot_dim] = np.repeat(np.cos(ang), 2, axis=1)
        s[:, lo:lo + rot_dim] = np.repeat(np.sin(ang), 2, axis=1) * np.tile(np.array([-1.0, 1.0]), rot_dim // 2)
    return jnp.asarray(c), jnp.asarray(s)


L0_Q0 = 2 * A_WIDTH
L0_K0 = L0_Q0 + B_HEADS * LANES
L0_V0 = L0_K0 + B_KV * B_HD
L0_IN = L0_V0 + B_KV * B_HD


def _l0_kernel(*refs, latent):
    if latent:
        (sink_ref, x_ref, gn_ref, mod_ref, win_ref, gvn_ref, ws_ref, bsb_ref, gq_ref, gk_ref, wo_ref,
         cos_ref, sin_ref, kc_ref, vc_ref, xo_ref, zs, cat, qs, ks, vt, kcb, vct) = refs
        key_off = WINDOW
    else:
        (sink_ref, x_ref, gn_ref, mod_ref, win_ref, gvn_ref, ws_ref, bsb_ref, gq_ref, gk_ref, wo_ref,
         xo_ref, ko_ref, vo_ref, zs, cat, qs, ks, vt, kf, vf) = refs
        key_off = 0
    n = BLOCK_ROWS
    n_chunks = n // CHUNK
    low = lax.broadcasted_iota(jnp.int32, (CHUNK, LANES), 1) < B_HD

    if latent:
        zpad = jnp.zeros((WINDOW, LANES), bf16)
        for c0 in (0, 1 + n_chunks):
            ks[c0 * CHUNK:(c0 + 1) * CHUNK, :] = zpad
            vt[c0] = zpad
        kcb[...] = kc_ref[...].astype(bf16)
        for i in range(PAST // CHUNK):
            vct[i] = vc_ref[i * CHUNK:(i + 1) * CHUNK, :].T.astype(bf16)

    def project(c, carry):
        r = pl.ds(pl.multiple_of(c * PROJ_ROWS, PROJ_ROWS), PROJ_ROWS)
        h = _rms_rows(x_ref[r, :], gn_ref[...]) * (1.0 + mod_ref[1:2, :]) + mod_ref[0:1, :]
        zs[r, :] = jnp.dot(h.astype(bf16), win_ref[...], preferred_element_type=f32)
        return carry

    lax.fori_loop(0, n // PROJ_ROWS, project, 0)

    def prepare(c, carry):
        r = pl.ds(pl.multiple_of(c * CHUNK, CHUNK), CHUNK)
        u = jax.nn.gelu(zs[r, 0:A_WIDTH])
        v = jax.nn.gelu(zs[r, A_WIDTH:2 * A_WIDTH])
        mu = jnp.mean(v, -1, keepdims=True)
        var = jnp.mean(jnp.square(v - mu), -1, keepdims=True)
        vn = ((v - mu) * lax.rsqrt(var + EPS) * gvn_ref[...]).astype(bf16)
        for g in range(A_GROUPS):
            cs = slice(g * CHUNK, (g + 1) * CHUNK)
            mixed = jnp.dot(ws_ref[g], vn[:, cs], preferred_element_type=f32) + bsb_ref[g]
            cat[r, cs] = (u[:, cs] * mixed).astype(bf16)
        if latent:
            cs_, sn_ = cos_ref[r, :], sin_ref[r, :]
        for h in range(B_HEADS):
            hs = slice(h * LANES, (h + 1) * LANES)
            qh = zs[r, L0_Q0 + h * LANES:L0_Q0 + (h + 1) * LANES]
            qh = qh * lax.rsqrt(jnp.sum(qh * qh, -1, keepdims=True) * (1.0 / B_HD) + EPS) * gq_ref[:, hs]
            if latent:
                qh = qh * cs_ + _swap_pairs(qh) * sn_
            qs[r, hs] = qh.astype(bf16)
        k = zs[r, L0_K0:L0_K0 + LANES]
        k2 = k * k
        s0 = jnp.sum(jnp.where(low, k2, 0.0), -1, keepdims=True)
        s1 = jnp.sum(jnp.where(low, 0.0, k2), -1, keepdims=True)
        k = k * lax.rsqrt(jnp.where(low, s0, s1) * (1.0 / B_HD) + EPS) * gk_ref[...]
        vv = zs[r, L0_V0:L0_V0 + LANES]
        if latent:
            k = k * cs_ + _swap_pairs(k) * sn_
        else:
            kf[r, :] = k
            vf[r, :] = vv
        kr = pl.ds(pl.multiple_of(c * CHUNK + key_off, CHUNK), CHUNK)
        ks[kr, :] = k.astype(bf16)
        vt[c + key_off // CHUNK] = vv.T.astype(bf16)
        return carry

    lax.fori_loop(0, n_chunks, prepare, 0)

    def attend(r, rows, key_sets):
        q = jnp.concatenate([qs[r, h * LANES:(h + 1) * LANES] for h in range(B_HEADS)], axis=0)
        sk = jnp.concatenate([jnp.full((1, rows), sink_ref[h], f32) for h in range(B_HEADS)], axis=1)
        scores = []
        m = sk
        for k, _, keep in key_sets:
            s = lax.dot_general(k, q, (((1,), (1,)), ((), ())), preferred_element_type=f32)
            if keep is not None:
                s = jnp.where(keep, s, NEG_INF)
            scores.append(s)
            m = jnp.maximum(m, jnp.max(s, 0, keepdims=True))
        den = jnp.exp(sk - m)
        ot = None
        for s, (_, vts, _) in zip(scores, key_sets):
            e = jnp.exp(s - m)
            den = den + jnp.sum(e, 0, keepdims=True)
            eb = e.astype(bf16)
            for i, v_t in enumerate(vts):
                pv = jnp.dot(v_t, eb[i * CHUNK:(i + 1) * CHUNK, :], preferred_element_type=f32)
                ot = pv if ot is None else ot + pv
        ot = ot * (1.0 / den)
        for pair in range(B_HEADS // 2):
            f0 = (2 * pair // B_GROUP) * B_HD
            pair_t = jnp.concatenate([ot[f0:f0 + B_HD, 2 * pair * rows:(2 * pair + 1) * rows],
                                      ot[f0:f0 + B_HD, (2 * pair + 1) * rows:(2 * pair + 2) * rows]], axis=0)
            cat[r, A_WIDTH + pair * LANES:A_WIDTH + (pair + 1) * LANES] = pair_t.T.astype(bf16)

    if latent:
        span = CHUNK + 2 * WINDOW

        def attend_block(c, carry):
            start = pl.multiple_of(c * CHUNK, CHUNK)
            kr = pl.ds(start, span)
            kj = lax.broadcasted_iota(jnp.int32, (span, B_HEADS * CHUNK), 0)
            qi = lax.broadcasted_iota(jnp.int32, (span, B_HEADS * CHUNK), 1) & (CHUNK - 1)
            kpos = start - WINDOW + kj
            keep = (jnp.abs(kj - WINDOW - qi) <= WINDOW) & (kpos >= 0) & (kpos < n)
            attend(pl.ds(start, CHUNK), CHUNK,
                   [(ks[kr, :], [vt[c + i] for i in range(span // CHUNK)], keep),
                    (kcb[...], [vct[i] for i in range(PAST // CHUNK)], None)])
            return carry

        lax.fori_loop(0, n_chunks, attend_block, 0)
    else:
        def attend_seq(sq, carry):
            r = pl.ds(pl.multiple_of(sq * SEQ, SEQ), SEQ)
            attend(r, SEQ, [(ks[r, :], [vt[sq * (SEQ // CHUNK) + i] for i in range(SEQ // CHUNK)], None)])
            ko_ref[sq] = kf[r, :].T
            vo_ref[sq] = vf[r, :].T
            return carry

        lax.fori_loop(0, n // SEQ, attend_seq, 0)

    def output(c, carry):
        r = pl.ds(pl.multiple_of(c * PROJ_ROWS, PROJ_ROWS), PROJ_ROWS)
        y = jnp.dot(cat[r, :], wo_ref[...], preferred_element_type=f32)
        xo_ref[r, :] = x_ref[r, :] + mod_ref[2:3, :] * y
        return carry

    lax.fori_loop(0, n // PROJ_ROWS, output, 0)


def _l0_mixer(xc, xl, g_norm, mods, p, cache_k, cache_v):
    w = p['w_in']
    q = w[:, 2 * A_WIDTH:2 * A_WIDTH + B_HEADS * B_HD].reshape(D, B_HEADS, B_HD)
    zero = jnp.zeros((D, B_GROUP, B_HD), f32)
    q_slots = jnp.concatenate([jnp.concatenate([q[:, :B_GROUP], zero], axis=2),
                               jnp.concatenate([zero, q[:, B_GROUP:]], axis=2)], axis=1).reshape(D, B_HEADS * LANES)
    win = jnp.concatenate([w[:, :2 * A_WIDTH], q_slots, w[:, 2 * A_WIDTH + B_HEADS * B_HD:]], axis=1).astype(bf16)
    zg = jnp.zeros((B_HD,), f32)
    gq = jnp.concatenate([jnp.tile(jnp.concatenate([p['g_q'], zg]), B_GROUP),
                          jnp.tile(jnp.concatenate([zg, p['g_q']]), B_GROUP)]).reshape(1, B_HEADS * LANES) * B_SCALE
    gk = jnp.tile(p['g_k'], B_KV).reshape(1, LANES)
    gvn = p['g_vnorm'].reshape(1, A_WIDTH)
    ws = p['w_s'].astype(bf16)
    bsb = jnp.broadcast_to(p['b_s'][:, :, None], (A_GROUPS, CHUNK, CHUNK))
    wo = p['w_o'].astype(bf16)
    weights = (g_norm.reshape(1, D),)
    consts = (win, gvn, ws, bsb, gq, gk, wo)
    c_specs = [_const_spec(a.shape) for a in consts]
    smem = pl.BlockSpec(memory_space=pltpu.SMEM)
    row = pl.BlockSpec((BLOCK_ROWS, D), lambda b: (b, 0))
    kv = pl.BlockSpec((BLOCK_ROWS // SEQ, LANES, SEQ), lambda b: (b, 0, 0))

    def scratch(pad):
        return [pltpu.VMEM((BLOCK_ROWS, L0_IN), f32), pltpu.VMEM((BLOCK_ROWS, D), bf16),
                pltpu.VMEM((BLOCK_ROWS, B_HEADS * LANES), bf16), pltpu.VMEM((BLOCK_ROWS + pad, LANES), bf16),
                pltpu.VMEM(((BLOCK_ROWS + pad) // CHUNK, LANES, CHUNK), bf16)]

    kv_shape = jax.ShapeDtypeStruct((BATCH, LANES, SEQ), f32)
    state = pltpu.VMEM((BLOCK_ROWS, LANES), f32)
    xo_ctx, k_t, v_t = pl.pallas_call(
        functools.partial(_l0_kernel, latent=False),
        grid=(T_CTX // BLOCK_ROWS,),
        in_specs=[smem, row, _const_spec((1, D)), pl.BlockSpec((8, D), lambda b: (0, 0))] + c_specs,
        out_specs=[row, kv, kv],
        out_shape=[jax.ShapeDtypeStruct((T_CTX, D), f32), kv_shape, kv_shape],
        scratch_shapes=scratch(0) + [state, state],
        compiler_params=_cp(("parallel",)), name="l0_mixer_ctx",
    )(p['sink'], xc, *weights, mods, *consts)
    k_new = k_t.reshape(BATCH, B_KV, B_HD, SEQ).transpose(0, 3, 1, 2)
    v_new = v_t.reshape(BATCH, B_KV, B_HD, SEQ).transpose(0, 3, 1, 2)

    cos, sin = _rope_tables(DEC_SEQ, B_HD, LANES, 0, copies=LANES // B_HD)
    cache = pl.BlockSpec((None, PAST, LANES), lambda b: (b, 0, 0))
    past = [pltpu.VMEM((PAST, LANES), bf16), pltpu.VMEM((PAST // CHUNK, LANES, CHUNK), bf16)]
    xo_lat = pl.pallas_call(
        functools.partial(_l0_kernel, latent=True),
        grid=(DEC_BATCH,),
        in_specs=[smem, row, _const_spec((1, D)), pl.BlockSpec((8, D), lambda b: (1 + b, 0))] + c_specs + [
                  _const_spec(cos.shape), _const_spec(sin.shape), cache, cache],
        out_specs=row,
        out_shape=jax.ShapeDtypeStruct((T_LAT, D), f32),
        scratch_shapes=scratch(2 * WINDOW) + past,
        compiler_params=_cp(("parallel",)), name="l0_mixer_lat",
    )(p['sink'], xl, *weights, mods, *consts, cos, sin,
      cache_k.reshape(DEC_BATCH, PAST, LANES), cache_v.reshape(DEC_BATCH, PAST, LANES))
    return xo_ctx, xo_lat, k_new, v_new


C_SLOTS = C_HEADS * SLOT
C_PAIRS = C_HEADS // 2
L1_ROWS = 256


def _l1_kernel(*refs, latent):
    if latent:
        (x_ref, gn_ref, mod_ref, win_ref, gqa_ref, wuq_ref, gq_ref, gkva_ref, wuk_ref, wuvt_ref, gk_ref,
         wo_ref, wuqs_ref, qcos_ref, qsin_ref, kcos_ref, ksin_ref, cckv_ref, ckpe_ref, xo_ref,
         zs, cat, qs, ks, vt, wide, wide2) = refs
        n_ctx = PAST
    else:
        (x_ref, gn_ref, mod_ref, win_ref, gqa_ref, wuq_ref, gq_ref, gkva_ref, wuk_ref, wuvt_ref, gk_ref,
         wo_ref, xo_ref, ckvo_ref, kpeo_ref, zs, cat, qs, ks, vt, wide) = refs
        n_ctx = 0
    n = BLOCK_ROWS
    nt_dims = (((1,), (1,)), ((), ()))

    def inv_rms(v):
        return lax.rsqrt(jnp.sum(v * v, -1, keepdims=True) * (1.0 / C_QK) + EPS)

    def expand_keys(ckv_n, kslot, kb, rope_rows):
        cb = ckv_n.astype(bf16)
        key_rows = pl.ds(pl.multiple_of(kb * L1_ROWS, L1_ROWS), L1_ROWS)
        wide[...] = jnp.dot(cb, wuk_ref[...], preferred_element_type=f32)
        if rope_rows is not None:
            kcos = kcos_ref[rope_rows, :]
            turned = _swap_pairs(kslot) * ksin_ref[rope_rows, :]
        for h in range(C_HEADS):
            kh = wide[:, h * SLOT:(h + 1) * SLOT] + kslot
            if rope_rows is not None:
                kh = inv_rms(kh) * (kh * kcos + turned)
            else:
                kh = kh * inv_rms(kh) * gk_ref[...]
            ks[h, key_rows, :] = kh.astype(bf16)
        v_t = lax.dot_general(wuvt_ref[...], cb, nt_dims, preferred_element_type=f32).astype(bf16)
        for pair in range(C_PAIRS):
            vt[pair, kb] = v_t[pair * LANES:(pair + 1) * LANES, :]

    if latent:
        def past_keys(c, carry):
            r = pl.ds(pl.multiple_of(c * L1_ROWS, L1_ROWS), L1_ROWS)
            expand_keys(cckv_ref[r, :], ckpe_ref[r, :], c, None)
            return carry

        lax.fori_loop(0, PAST // L1_ROWS, past_keys, 0)

    def project(c, carry):
        r = pl.ds(pl.multiple_of(c * PROJ_ROWS, PROJ_ROWS), PROJ_ROWS)
        h = _rms_rows(x_ref[r, :], gn_ref[...]) * (1.0 + mod_ref[1:2, :]) + mod_ref[0:1, :]
        zs[r, :] = jnp.dot(h.astype(bf16), win_ref[...], preferred_element_type=f32)
        return carry

    lax.fori_loop(0, n // PROJ_ROWS, project, 0)

    def prepare(c, carry):
        r = pl.ds(pl.multiple_of(c * L1_ROWS, L1_ROWS), L1_ROWS)
        qa = _rms_rows(zs[r, 0:C_Q_LORA], gqa_ref[...]).astype(bf16)
        wide[...] = jnp.dot(qa, wuq_ref[...], preferred_element_type=f32)
        if latent:
            wide2[...] = jnp.dot(qa, wuqs_ref[...], preferred_element_type=f32)
            qcos, qsin = qcos_ref[r, :], qsin_ref[r, :]
        for h in range(C_HEADS):
            hs = slice(h * SLOT, (h + 1) * SLOT)
            qh = wide[:, hs]
            if latent:
                qh = inv_rms(qh) * (qh * qcos + wide2[:, hs] * qsin)
            else:
                qh = qh * inv_rms(qh) * gq_ref[...]
            qs[h, r, :] = qh.astype(bf16)
        ckv_n = _rms_rows(zs[r, C_Q_LORA:C_Q_LORA + C_KV_LORA], gkva_ref[...])
        kslot = zs[r, C_Q_LORA + C_KV_LORA:ODD_IN_PAD]
        if not latent:
            ckvo_ref[r, :] = ckv_n
            kpeo_ref[r, :] = kslot
        expand_keys(ckv_n, kslot, c + n_ctx // L1_ROWS, r if latent else None)
        return carry

    lax.fori_loop(0, n // L1_ROWS, prepare, 0)

    low = lax.broadcasted_iota(jnp.int32, (2 * C_V, L1_ROWS), 0) < C_V
    n_kblocks = (n_ctx + n) // L1_ROWS
    pairs_per_step = 2 if latent else 4

    def attend(c, carry):
        r = pl.ds(pl.multiple_of(c * L1_ROWS, L1_ROWS), L1_ROWS)

        def values_t(pair, eb):
            if not latent:
                return jnp.dot(vt[pair, c], eb, preferred_element_type=f32)
            o_t = None
            for b in range(n_kblocks):
                pv = jnp.dot(vt[pair, b], eb[b * L1_ROWS:(b + 1) * L1_ROWS, :], preferred_element_type=f32)
                o_t = pv if o_t is None else o_t + pv
            return o_t

        def pairs_step(i, carry2):
            pairs = [i * pairs_per_step + j for j in range(pairs_per_step)]
            heads = [2 * p + hh for p in pairs for hh in range(2)]
            scores = [lax.dot_general(ks[h] if latent else ks[h, r, :], qs[h, r, :], nt_dims,
                                      preferred_element_type=f32) for h in heads]
            exps = [jnp.exp(s - jnp.max(s, 0, keepdims=True)) for s in scores]
            dens = [jnp.sum(e, 0, keepdims=True) for e in exps]
            outs = [values_t(h // 2, e.astype(bf16)) / den for h, e, den in zip(heads, exps, dens)]
            for j, pair in enumerate(pairs):
                cat[pair, r, :] = jnp.where(low, outs[2 * j], outs[2 * j + 1]).T.astype(bf16)
            return carry2

        return lax.fori_loop(0, C_PAIRS // pairs_per_step, pairs_step, carry)

    lax.fori_loop(0, n // L1_ROWS, attend, 0)

    def output(c, carry):
        r = pl.ds(pl.multiple_of(c * PROJ_ROWS, PROJ_ROWS), PROJ_ROWS)
        heads = jnp.concatenate([cat[pair, r, :] for pair in range(C_PAIRS)], axis=1)
        y = jnp.dot(heads, wo_ref[...], preferred_element_type=f32)
        xo_ref[r, :] = x_ref[r, :] + mod_ref[2:3, :] * y
        return carry

    lax.fori_loop(0, n // PROJ_ROWS, output, 0)


def _slot_cols(w, heads, width, lo, hi, lane0):
    k = w.shape[0]
    w3 = w.reshape(k, heads, width)[:, :, lo:hi]
    out = jnp.zeros((k, heads, SLOT), w.dtype).at[:, :, lane0:lane0 + (hi - lo)].set(w3)
    return out.reshape(k, heads * SLOT)


def _l1_mixer(xc, xl, g_norm, mods, p, cache_ckv, cache_kpe):
    w_in = jnp.zeros((D, ODD_IN_PAD), f32).at[:, :C_Q_LORA + C_KV_LORA].set(
        p['w_in'][:, :C_Q_LORA + C_KV_LORA]).at[
        :, C_Q_LORA + C_KV_LORA + C_NOPE:C_Q_LORA + C_KV_LORA + C_QK].set(p['w_in'][:, C_Q_LORA + C_KV_LORA:])
    wuq = _slot_cols(p['w_uq'], C_HEADS, C_QK, 0, C_QK, 0).astype(bf16)
    wuk = _slot_cols(p['w_ukv'], C_HEADS, C_NOPE + C_V, 0, C_NOPE, 0).astype(bf16)
    wuv_t = p['w_ukv'].reshape(C_KV_LORA, C_HEADS, C_NOPE + C_V)[:, :, C_NOPE:].reshape(
        C_KV_LORA, C_HEADS * C_V).T.astype(bf16)
    gq = jnp.zeros((1, SLOT), f32).at[0, :C_QK].set(p['g_q'] * C_SCALE)
    gk = jnp.zeros((1, SLOT), f32).at[0, :C_QK].set(p['g_k'])
    consts = (g_norm.reshape(1, D), w_in.astype(bf16), p['g_qa'].reshape(1, C_Q_LORA), wuq, gq,
              p['g_kva'].reshape(1, C_KV_LORA), wuk, wuv_t, gk, p['w_o'].astype(bf16))
    c_specs = [_const_spec(a.shape) for a in consts]
    row = pl.BlockSpec((BLOCK_ROWS, D), lambda b: (b, 0))
    n_ctx_blocks = T_CTX // BLOCK_ROWS

    def scratch(n_keys):
        return [pltpu.VMEM((BLOCK_ROWS, ODD_IN_PAD), f32), pltpu.VMEM((C_PAIRS, BLOCK_ROWS, LANES), bf16),
                pltpu.VMEM((C_HEADS, BLOCK_ROWS, SLOT), bf16), pltpu.VMEM((C_HEADS, n_keys, SLOT), bf16),
                pltpu.VMEM((C_PAIRS, n_keys // L1_ROWS, LANES, L1_ROWS), bf16),
                pltpu.VMEM((L1_ROWS, C_SLOTS), f32)]

    xo_ctx, ckv_new, kpe_slot = pl.pallas_call(
        functools.partial(_l1_kernel, latent=False),
        grid=(n_ctx_blocks,),
        in_specs=[row, c_specs[0], pl.BlockSpec((8, D), lambda b: (0, 0))] + c_specs[1:],
        out_specs=[row, pl.BlockSpec((BLOCK_ROWS, C_KV_LORA), lambda b: (b, 0)),
                   pl.BlockSpec((BLOCK_ROWS, SLOT), lambda b: (b, 0))],
        out_shape=[jax.ShapeDtypeStruct((T_CTX, D), f32), jax.ShapeDtypeStruct((T_CTX, C_KV_LORA), f32),
                   jax.ShapeDtypeStruct((T_CTX, SLOT), f32)],
        scratch_shapes=scratch(BLOCK_ROWS),
        compiler_params=_cp(("parallel",)), name="l1_mixer_ctx",
    )(xc, consts[0], mods, *consts[1:])

    cos, sin = _rope_tables(DEC_SEQ, C_ROPE, SLOT, C_NOPE)
    def pair_swap(a):
        pairs = a.reshape(a.shape[:-1] + (a.shape[-1] // 2, 2))
        return jnp.stack([pairs[..., 1], pairs[..., 0]], axis=-1).reshape(a.shape)
    rope = (pair_swap(wuq), gq * cos, pair_swap(gq) * sin, gk * cos, pair_swap(gk) * sin)
    ckpe = jnp.zeros((DEC_BATCH, PAST, SLOT), f32).at[:, :, C_NOPE:C_QK].set(cache_kpe)
    xo_lat = pl.pallas_call(
        functools.partial(_l1_kernel, latent=True),
        grid=(DEC_BATCH,),
        in_specs=[pl.BlockSpec((BLOCK_ROWS, D), lambda b: (b, 0), pipeline_mode=pl.Buffered(1)), c_specs[0],
                  pl.BlockSpec((8, D), lambda b: (1 + b, 0))] + c_specs[1:] + [_const_spec(a.shape) for a in rope] + [
                  pl.BlockSpec((None, PAST, C_KV_LORA), lambda b: (b, 0, 0)),
                  pl.BlockSpec((None, PAST, SLOT), lambda b: (b, 0, 0))],
        out_specs=row,
        out_shape=jax.ShapeDtypeStruct((T_LAT, D), f32),
        scratch_shapes=scratch(PAST + BLOCK_ROWS) + [pltpu.VMEM((L1_ROWS, C_SLOTS), f32)],
        compiler_params=_cp(("parallel",)), name="l1_mixer_lat",
    )(xl, consts[0], mods, *consts[1:], *rope, cache_ckv, ckpe)
    return xo_ctx, xo_lat, ckv_new, kpe_slot[:, C_NOPE:C_QK]


ROUTER_ROWS = 40


def _router_kernel(xc_ref, xl_ref, gn_ref, mod_ref, whi_ref, wlo_ref, br_ref, h_ref, route_ref):
    h = _rms_rows(_token_rows(xc_ref, xl_ref), gn_ref[...]) * (1.0 + mod_ref[4:5, :]) + mod_ref[3:4, :]
    _store_token_major(h_ref, h)
    h_hi, h_lo = _split_bf16(h)
    nt = (((1,), (1,)), ((), ()))
    logits = (lax.dot_general(whi_ref[...], h_hi, nt, preferred_element_type=f32)
              + lax.dot_general(whi_ref[...], h_lo, nt, preferred_element_type=f32)
              + lax.dot_general(wlo_ref[...], h_hi, nt, preferred_element_type=f32))
    logits = logits[0:ROUTER_ROWS, :] + br_ref[0:ROUTER_ROWS, :]
    row_i = lax.broadcasted_iota(jnp.int32, logits.shape, 0)
    row = row_i.astype(f32)
    big = 1e6
    is_g = (row_i >= N_EXPERTS) & (row_i < N_EXPERTS + N_GROUPS)
    lg = jnp.where(is_g, logits, -jnp.inf)
    mg = jnp.max(lg, 0, keepdims=True)
    gsel = jnp.min(jnp.where(lg == mg, row, big), 0, keepdims=True) - N_EXPERTS
    pg_sel = 1.0 / jnp.sum(jnp.where(is_g, jnp.exp(lg - mg), 0.0), 0, keepdims=True)
    in_grp = (row_i < N_EXPERTS) & ((row_i >> 3).astype(f32) == gsel)
    le = jnp.where(in_grp, logits, -jnp.inf)
    m1 = jnp.max(le, 0, keepdims=True)
    i1 = jnp.min(jnp.where(le == m1, row, big), 0, keepdims=True)
    le2 = jnp.where(row == i1, -jnp.inf, le)
    m2 = jnp.max(le2, 0, keepdims=True)
    i2 = jnp.min(jnp.where(le2 == m2, row, big), 0, keepdims=True)
    e2 = jnp.exp(m2 - m1)
    w1 = pg_sel / (1.0 + e2)
    w2 = pg_sel * e2 / (1.0 + e2)
    sub = lax.broadcasted_iota(jnp.int32, route_ref.shape, 0)
    route_ref[...] = jnp.where(sub == 0, i1, jnp.where(sub == 1, i2, jnp.where(sub == 2, w1,
                                                                                jnp.where(sub == 3, w2, 0.0))))


def _router(xc, xl, g_norm, mods, p):
    wr = jnp.zeros((128, D), f32).at[:N_EXPERTS].set(p['w_re'].T).at[
        N_EXPERTS:N_EXPERTS + N_GROUPS].set(p['w_rg'].T)
    w_hi, w_lo = _split_bf16(wr)
    br =jnp.zeros((128, 1), f32).at[:N_EXPERTS, 0].set(p['b_re']).at[
        N_EXPERTS:N_EXPERTS + N_GROUPS, 0].set(p['b_rg'])
    return pl.pallas_call(
        _router_kernel,
        grid=(T // ROW_TILE,),
        in_specs=_token_specs(D) + [
                  _const_spec((1, D)),
                  pl.BlockSpec((8, D), lambda i: (_sample_of_tile(i, ROW_TILE), 0)),
                  _const_spec((128, D)), _const_spec((128, D)), _const_spec((128, 1))],
        out_specs=[_token_major_spec(ROW_TILE, lambda i: (i, 0)), pl.BlockSpec((8, ROW_TILE), lambda i: (0, i))],
        out_shape=[jax.ShapeDtypeStruct((T * SUBS, LANES), f32), jax.ShapeDtypeStruct((8, T), f32)],
        compiler_params=_cp(("parallel",)), name="router",
    )(xc, xl, g_norm.reshape(1, D), mods, w_hi, w_lo, br)


PLAN_FIRST_TILE, PLAN_TILES, PLAN_N_USED = 0, 1, 2


def _plan_kernel(rt_ref, pos_ref, plan_ref, rank):
    n_blk = T // 128
    e_col = lax.broadcasted_iota(jnp.int32, (N_EXPERTS, 128), 0).astype(f32)
    ri = lax.broadcasted_iota(jnp.int32, (128, 128), 0)
    ci = lax.broadcasted_iota(jnp.int32, (128, 128), 1)
    before = jnp.where(ri < ci, 1.0, 0.0).astype(bf16)

    def picks(b):
        cs = slice(b * 128, (b + 1) * 128)
        return rt_ref[0:1, cs] == e_col, rt_ref[1:2, cs] == e_col

    counts = jnp.zeros((N_EXPERTS, 1), f32)
    for b in range(n_blk):
        m0, m1 = picks(b)
        m = jnp.where(m0, 1.0, 0.0) + jnp.where(m1, 1.0, 0.0)
        rank[:, b * 128:(b + 1) * 128] = jnp.dot(m.astype(bf16), before, preferred_element_type=f32) + counts
        counts = counts + jnp.sum(m, axis=1, keepdims=True)

    tiles = jnp.floor((counts + (MOE_TILE - 1.0)) * (1.0 / MOE_TILE))
    er = lax.broadcasted_iota(jnp.int32, (N_EXPERTS, N_EXPERTS), 0)
    ec = lax.broadcasted_iota(jnp.int32, (N_EXPERTS, N_EXPERTS), 1)
    earlier = jnp.where(ec < er, 1.0, 0.0).astype(bf16)
    tile_start = jnp.dot(earlier, jnp.broadcast_to(tiles, (N_EXPERTS, 128)).astype(bf16),
                         preferred_element_type=f32)
    row_start = tile_start * MOE_TILE

    sub = lax.broadcasted_iota(jnp.int32, (8, 128), 0)
    for b in range(n_blk):
        m0, m1 = picks(b)
        base = rank[:, b * 128:(b + 1) * 128] + row_start
        p0 = jnp.sum(jnp.where(m0, base, 0.0), axis=0, keepdims=True)
        p1 = jnp.sum(jnp.where(m1, base, 0.0), axis=0, keepdims=True)
        pos_ref[:, b * 128:(b + 1) * 128] = jnp.where(sub == 0, p0, jnp.where(sub == 1, p1, 0.0)).astype(jnp.int32)

    tile_end = tile_start + tiles
    n_used = jnp.max(tile_end, axis=0, keepdims=True)
    diag =(lax.broadcasted_iota(jnp.int32, (N_EXPERTS, 128), 0)
            == lax.broadcasted_iota(jnp.int32, (N_EXPERTS, 128), 1))
    first = jnp.sum(jnp.where(diag, tile_start, 0.0), axis=0, keepdims=True)
    count = jnp.sum(jnp.where(diag, tiles, 0.0), axis=0, keepdims=True)
    rows = jnp.where(sub == PLAN_FIRST_TILE, first,
                     jnp.where(sub == PLAN_TILES, count, jnp.where(sub == PLAN_N_USED, n_used, 0.0)))
    plan_ref[...] = rows.astype(jnp.int32)


def _slot_code(t, k):
    return t * SUBS + k * (SUBS // 2)


def _code_offset(code):
    return pl.multiple_of(code & ~(SUBS - 1), SUBS)


def _code_gate_index(code):
    return code >> 2


PAD_CODE = T * SUBS


def _invert_kernel(pos_ref, plan_ref, code_ref):
    def pad_tile(tile, carry):
        for u in range(MOE_TILE):
            code_ref[tile * MOE_TILE + u] = PAD_CODE
        return carry

    def pad_last_tile(e, carry):
        return pad_tile(jnp.maximum(plan_ref[PLAN_FIRST_TILE, e] + plan_ref[PLAN_TILES, e] - 1, 0), carry)
    lax.fori_loop(0, N_EXPERTS, pad_last_tile, 0)
    lax.fori_loop(plan_ref[PLAN_N_USED, 0], MOE_TILES, pad_tile, 0)

    group = 16
    for k in range(2):
        def place(i, carry):
            t0 = i * group
            slots = [pos_ref[k * T + t0 + u] for u in range(group)]
            for u, s in enumerate(slots):
                code_ref[s] = _slot_code(t0 + u, k)
            return carry
        lax.fori_loop(0, T // group, place, 0)


def _route_plan(route_t):
    pos, plan = pl.pallas_call(
        _plan_kernel,
        out_shape=[jax.ShapeDtypeStruct((8, T), jnp.int32), jax.ShapeDtypeStruct((8, LANES), jnp.int32)],
        scratch_shapes=[pltpu.VMEM((N_EXPERTS, T), f32)],
        compiler_params=_cp(None), name="route_plan",
    )(route_t)
    smem = pl.BlockSpec(memory_space=pltpu.SMEM)
    codes = pl.pallas_call(
        _invert_kernel,
        in_specs=[smem, smem], out_specs=smem,
        out_shape=jax.ShapeDtypeStruct((MOE_ROWS,), jnp.int32),
        name="route_invert",
    )(pos[0:2].reshape(2 * T), plan)
    gates = jnp.pad(route_t[2:4].T.reshape(2 * T), (0, 8))
    return plan[PLAN_FIRST_TILE, :N_EXPERTS], plan[PLAN_TILES, :N_EXPERTS], codes, gates


def _tile_index(i):
    return jnp.minimum(i, MOE_TILES - 1)


TM_ROWS = T * SUBS
SCATTER_GROUP = 16
STAGES = 4
LAST_EXPERT = N_EXPERTS - 1


def _tile_rows(g):
    return pl.ds(pl.multiple_of(g * MOE_TILE, MOE_TILE), MOE_TILE)


def _expert_tile_pairs(first, count, tile_step):
    def pair(pp, carry):
        for parity in range(2):
            g = 2 * pp + parity
            pl.when((g >= first) & (g < first + count))(functools.partial(tile_step, g, parity))
        return carry
    lax.fori_loop(first // 2, (first + count + 1) // 2, pair, 0)


def _gather_tile(code_ref, tile, xs, gbuf):
    base = tile * MOE_TILE
    for r in range(MOE_TILE):
        gbuf[r * SUBS:(r + 1) * SUBS, :] = xs[pl.ds(_code_offset(code_ref[base + r]), SUBS), :]


def _moe_up_kernel(first_ref, count_ref, code_ref, h_hbm, w1_ref, w3_ref, hh_hbm,
                   xs, gbuf_a, gbuf_b, w13, obuf, sem_x, sem_o):
    e = pl.program_id(0)
    first, count = first_ref[e], count_ref[e]
    gbufs = (gbuf_a, gbuf_b)

    def out_copy(slot, g):
        return pltpu.make_async_copy(obuf.at[slot], hh_hbm.at[_tile_rows(g), :], sem_o.at[slot])

    @pl.when(e == 0)
    def _():
        cp = pltpu.make_async_copy(h_hbm, xs.at[pl.ds(0, TM_ROWS), :], sem_x)
        cp.start()
        xs[TM_ROWS:TM_ROWS + SUBS, :] = jnp.zeros((SUBS, LANES), f32)
        cp.wait()
        _gather_tile(code_ref, 0, xs, gbuf_a)

    @pl.when(count > 0)
    def _():
        w13[:, :D_EXPERT] = w1_ref[0].astype(bf16)
        w13[:, D_EXPERT:] = w3_ref[0].astype(bf16)

    def tile_step(g, parity):
        _gather_tile(code_ref, _tile_index(g + 1), xs, gbufs[1 - parity])
        x3 = jnp.swapaxes(gbufs[parity][...].reshape(MOE_TILE, SUBS, LANES), 0, 1)
        x = jnp.concatenate([x3[s] for s in range(SUBS)], axis=1).astype(bf16)
        h13 = jnp.dot(x, w13[...], preferred_element_type=f32)
        hh = (_silu(h13[:, :D_EXPERT]) * h13[:, D_EXPERT:]).astype(bf16)

        slot = g % STAGES

        @pl.when(g >= STAGES)
        def _():
            out_copy(slot, g).wait()
        obuf[slot] = hh
        out_copy(slot, g).start()

    _expert_tile_pairs(first, count, tile_step)

    @pl.when(e == LAST_EXPERT)
    def _():
        n_used = first + count
        for slot in range(STAGES):
            pl.when(n_used > slot)(lambda slot=slot: out_copy(slot, 0).wait())
        obuf[0] = jnp.zeros((MOE_TILE, D_EXPERT), bf16)

        def zero_tile(g, carry):
            cp = out_copy(0, g)
            cp.start()
            cp.wait()
            return carry
        lax.fori_loop(n_used, MOE_TILES, zero_tile, 0)


def _scatter_tile(code_ref, gate_ref, tile, ybuf, acc):
    base = tile * MOE_TILE
    for g0 in range(0, MOE_TILE, SCATTER_GROUP):
        rows = range(g0, g0 + SCATTER_GROUP)
        codes = [code_ref[base + r] for r in rows]
        new = [acc[pl.ds(_code_offset(c), SUBS), :]
               + gate_ref[_code_gate_index(c)] * ybuf[r * SUBS:(r + 1) * SUBS, :]
               for r, c in zip(rows, codes)]
        for c, v in zip(codes, new):
            acc[pl.ds(_code_offset(c), SUBS), :] = v


RES_ROWS = 256


def _residual_out(x_hbm, y_hbm, tok0, sample_of_chunk, mod_ref, acc, rin, rout, sem_r, sem_w):
    n_chunks = x_hbm.shape[0] // RES_ROWS

    def rows(c):
        return pl.ds(pl.multiple_of(c * RES_ROWS, RES_ROWS), RES_ROWS)

    def in_copy(slot, c):
        return pltpu.make_async_copy(x_hbm.at[rows(c), :], rin.at[slot], sem_r.at[slot])

    def out_copy(slot, c):
        return pltpu.make_async_copy(rout.at[slot], y_hbm.at[rows(c), :], sem_w.at[slot])

    for c in range(STAGES - 1):
        in_copy(c, c).start()

    def ring(cc, carry):
        for slot in range(STAGES):
            c = STAGES * cc + slot
            in_copy(slot, c).wait()
            ahead = c + STAGES - 1

            @pl.when(ahead < n_chunks)
            def _():
                in_copy((slot + STAGES - 1) % STAGES, ahead).start()

            @pl.when(c >= STAGES)
            def _():
                out_copy(slot, c).wait()
            delta = _load_token_major(acc, RES_ROWS, tok0 + c * RES_ROWS)
            gate = mod_ref[pl.ds(sample_of_chunk(c) * 8 + 5, 1), :]
            rout[slot] = rin[slot] + gate * delta
            out_copy(slot, c).start()
        return carry

    lax.fori_loop(0, n_chunks // STAGES, ring, 0)
    for slot in range(STAGES):
        out_copy(slot, 0).wait()


def _moe_down_kernel(first_ref, count_ref, code_ref, gate_ref, hh_hbm, w2_ref, xc_hbm, xl_hbm, mod_ref,
                     yc_hbm, yl_hbm, acc, ybuf_a, ybuf_b, w2b, ibuf, rin, rout, sem_i, sem_r, sem_w):
    e = pl.program_id(0)
    first, count = first_ref[e], count_ref[e]
    n_used = first_ref[LAST_EXPERT] + count_ref[LAST_EXPERT]
    ybufs = (ybuf_a, ybuf_b)

    def in_copy(slot, g):
        return pltpu.make_async_copy(hh_hbm.at[_tile_rows(g), :], ibuf.at[slot], sem_i.at[slot])

    @pl.when(e == 0)
    def _():
        for g in range(STAGES - 1):
            in_copy(g, g).start()

        def zero(c, carry):
            acc[pl.ds(pl.multiple_of(c * 1024, 1024), 1024), :] = jnp.zeros((1024, LANES), f32)
            return carry
        lax.fori_loop(0, TM_ROWS // 1024, zero, 0)
        acc[TM_ROWS:TM_ROWS + SUBS, :] = jnp.zeros((SUBS, LANES), f32)
        ybuf_b[...] = jnp.zeros_like(ybuf_b)

    @pl.when(count > 0)
    def _():
        w2b[...] = w2_ref[0].astype(bf16)

    def tile_step(g, parity):
        slot = g % STAGES
        in_copy(slot, g).wait()
        ahead = g + STAGES - 1

        @pl.when(ahead < n_used)
        def _():
            in_copy(ahead % STAGES, ahead).start()
        _store_token_major(ybufs[parity], jnp.dot(ibuf[slot], w2b[...], preferred_element_type=f32))
        _scatter_tile(code_ref, gate_ref, jnp.maximum(g - 1, 0), ybufs[1 - parity], acc)

    _expert_tile_pairs(first, count, tile_step)

    @pl.when(e == LAST_EXPERT)
    def _():
        for parity in range(2):
            pl.when((n_used > 0) & ((n_used - 1) % 2 == parity))(
                functools.partial(_scatter_tile, code_ref, gate_ref, n_used - 1, ybufs[parity], acc))
        _residual_out(xc_hbm, yc_hbm, 0, lambda c: 0, mod_ref, acc, rin, rout, sem_r, sem_w)
        _residual_out(xl_hbm, yl_hbm, T_CTX, lambda c: 1 + c // (DEC_SEQ // RES_ROWS),
                      mod_ref, acc, rin, rout, sem_r, sem_w)


def _moe(h_tm, route_t, p, xc, xl, mods):
    first_tile, n_tiles, codes, gates = _route_plan(route_t)
    tile_rows = pltpu.VMEM((MOE_TILE * SUBS, LANES), f32)
    staging = pltpu.VMEM((STAGES, MOE_TILE, D_EXPERT), bf16)
    res_rows = pltpu.VMEM((STAGES, RES_ROWS, D), f32)
    hbm = pl.BlockSpec(memory_space=pl.ANY)
    hh = pl.pallas_call(
        _moe_up_kernel,
        grid_spec=pltpu.PrefetchScalarGridSpec(
            num_scalar_prefetch=3, grid=(N_EXPERTS,),
            in_specs=[pl.BlockSpec(memory_space=pl.ANY),
                      pl.BlockSpec((1, D, D_EXPERT), lambda e, f, n, c: (e, 0, 0)),
                      pl.BlockSpec((1, D, D_EXPERT), lambda e, f, n, c: (e, 0, 0))],
            out_specs=pl.BlockSpec(memory_space=pl.ANY),
            scratch_shapes=[pltpu.VMEM((TM_ROWS + SUBS, LANES), f32), tile_rows, tile_rows,
                            pltpu.VMEM((D, 2 * D_EXPERT), bf16), staging,
                            pltpu.SemaphoreType.DMA(()), pltpu.SemaphoreType.DMA((STAGES,))]),
        out_shape=jax.ShapeDtypeStruct((MOE_ROWS, D_EXPERT), bf16),
        compiler_params=_cp(("arbitrary",)), name="moe_up",
    )(first_tile, n_tiles, codes, h_tm, p['w1'], p['w3'])
    return pl.pallas_call(
        _moe_down_kernel,
        grid_spec=pltpu.PrefetchScalarGridSpec(
            num_scalar_prefetch=4, grid=(N_EXPERTS,),
            in_specs=[hbm, pl.BlockSpec((1, D_EXPERT, D), lambda e, f, n, c, g: (e, 0, 0)), hbm, hbm,
                      pl.BlockSpec((24, D), lambda e, f, n, c, g: (0, 0), pipeline_mode=pl.Buffered(1))],
            out_specs=[hbm, hbm],
            scratch_shapes=[pltpu.VMEM((TM_ROWS + SUBS, LANES), f32), tile_rows, tile_rows,
                            pltpu.VMEM((D_EXPERT, D), bf16), staging, res_rows, res_rows,
                            pltpu.SemaphoreType.DMA((STAGES,)), pltpu.SemaphoreType.DMA((STAGES,)),
                            pltpu.SemaphoreType.DMA((STAGES,))]),
        out_shape=[jax.ShapeDtypeStruct((T_CTX, D), f32), jax.ShapeDtypeStruct((T_LAT, D), f32)],
        compiler_params=_cp(("arbitrary",)), name="moe_down",
    )(first_tile, n_tiles, codes, gates, hh, p['w2'], xc, xl, mods)


def kernel(x_prompt, x_sample, cache_l0_k, cache_l0_v, cache_l1_ckv, cache_l1_kpe, c, c_ctx, l0_g_norm1, l0_g_norm2, l0_w_ada, l0_b_ada, l0_w_in, l0_g_vnorm, l0_w_s, l0_b_s, l0_g_q, l0_g_k, l0_sink, l0_w_o, l0_w_rg, l0_b_rg, l0_w_re, l0_b_re, l0_w1, l0_w3, l0_w2, l1_g_norm1, l1_g_norm2, l1_w_ada, l1_b_ada, l1_w_in, l1_g_qa, l1_w_uq, l1_g_kva, l1_w_ukv, l1_g_q, l1_g_k, l1_w_o, l1_w_rg, l1_b_rg, l1_w_re, l1_b_re, l1_w1, l1_w3, l1_w2):
    p0 = dict(w_in=l0_w_in, g_vnorm=l0_g_vnorm, w_s=l0_w_s, b_s=l0_b_s, g_q=l0_g_q, g_k=l0_g_k, sink=l0_sink,
              w_o=l0_w_o, w_rg=l0_w_rg, b_rg=l0_b_rg, w_re=l0_w_re, b_re=l0_b_re, w1=l0_w1, w3=l0_w3, w2=l0_w2)
    p1 = dict(w_in=l1_w_in, g_qa=l1_g_qa, w_uq=l1_w_uq, g_kva=l1_g_kva, w_ukv=l1_w_ukv, g_q=l1_g_q, g_k=l1_g_k,
              w_o=l1_w_o, w_rg=l1_w_rg, b_rg=l1_b_rg, w_re=l1_w_re, b_re=l1_b_re, w1=l1_w1, w3=l1_w3, w2=l1_w2)

    cond8 = jnp.zeros((8, D), f32).at[0].set(c_ctx).at[1:1 + DEC_BATCH].set(c)
    mods0 = _mod_rows(_adaln(cond8, l0_w_ada, l0_b_ada))
    mods1 = _mod_rows(_adaln(cond8, l1_w_ada, l1_b_ada))

    xc0 = x_prompt.reshape(T_CTX, D)
    xl0 = x_sample.reshape(T_LAT, D)

    xc0m, xl0m, k_new, v_new = _l0_mixer(xc0, xl0, l0_g_norm1, mods0, p0, cache_l0_k, cache_l0_v)
    h0, route0 = _router(xc0m, xl0m, l0_g_norm2, mods0, p0)
    xc1, xl1 = _moe(h0, route0, p0, xc0m, xl0m, mods0)

    xc1m, xl1m, ckv_new, kpe_new = _l1_mixer(xc1, xl1, l1_g_norm1, mods1, p1, cache_l1_ckv, cache_l1_kpe)
    h1, route1 = _router(xc1m, xl1m, l1_g_norm2, mods1, p1)
    y_prompt, y_sample = _moe(h1, route1, p1, xc1m, xl1m, mods1)
    return (y_prompt.reshape(BATCH, SEQ, D), y_sample.reshape(DEC_BATCH, DEC_SEQ, D), k_new, v_new,
            ckv_new.reshape(BATCH, SEQ, C_KV_LORA), kpe_new.reshape(BATCH, SEQ, C_ROPE))
```

```python
import functools

import jax
import jax.numpy as jnp
import numpy as np
from jax import lax
from jax.experimental import pallas as pl
from jax.experimental.pallas import tpu as pltpu

f32 = jnp.float32
bf16 = jnp.bfloat16

D = 1024
BATCH, SEQ = 32, 256
DEC_BATCH, DEC_SEQ = 2, 1024
PAST = 512
T_CTX = BATCH * SEQ
T_LAT = DEC_BATCH * DEC_SEQ
T = T_CTX + T_LAT
GRID_W = 64
CHUNK = 128
WINDOW = 128
ROPE_THETA = 10000.0
EPS = 1e-6
NEG_INF = -1e30
LANES = 128
SUBS = D // LANES

A_WIDTH = 512
A_GROUPS = 4
B_HEADS, B_KV, B_GROUP, B_HD = 8, 2, 4, 64
B_SCALE = B_HD ** -0.5

C_HEADS, C_Q_LORA, C_KV_LORA, C_NOPE, C_ROPE, C_V = 16, 384, 256, 64, 32, 64
C_QK = C_NOPE + C_ROPE
C_SCALE = C_QK ** -0.5
ODD_IN_PAD = 768
SLOT = 128

N_GROUPS, EPG, N_EXPERTS, D_EXPERT = 4, 8, 32, 256

ROW_TILE = 512
BLOCK_ROWS = 1024
PROJ_ROWS = 512
MOE_TILE = 256
MOE_ROWS = 2 * T + N_EXPERTS * MOE_TILE
MOE_TILES = MOE_ROWS // MOE_TILE
VMEM_CAP = 56 * 1024 * 1024


def _cp(sem, vmem=VMEM_CAP):
    return pltpu.CompilerParams(dimension_semantics=sem, vmem_limit_bytes=vmem)


def _const_spec(shape):
    nd = len(shape)
    return pl.BlockSpec(shape, lambda *_: (0,) * nd, pipeline_mode=pl.Buffered(1))


def _sample_of_tile(i, tile):
    n_ctx = T_CTX // tile
    per_lat = DEC_SEQ // tile
    return jnp.where(i < n_ctx, 0, 1 + (i - n_ctx) // per_lat)


def _silu(x):
    return x * jax.nn.sigmoid(x)


def _rms_rows(x, g):
    return x * lax.rsqrt(jnp.mean(x * x, -1, keepdims=True) + EPS) * g


def _swap_pairs(x):
    lane = lax.broadcasted_iota(jnp.int32, x.shape, x.ndim - 1)
    nxt = pltpu.roll(x, x.shape[-1] - 1, x.ndim - 1)
    prv = pltpu.roll(x, 1, x.ndim - 1)
    return jnp.where((lane & 1) == 0, nxt, prv)


def _split_bf16(x):
    hi = x.astype(bf16)
    return hi, (x - hi.astype(f32)).astype(bf16)


def _adaln_kernel(c_ref, w_ref, b_ref, o_ref):
    s_hi, s_lo = _split_bf16(_silu(c_ref[...]))
    w_hi, w_lo = _split_bf16(w_ref[...])
    o_ref[...] = (jnp.dot(s_hi, w_hi, preferred_element_type=f32) + jnp.dot(s_lo, w_hi, preferred_element_type=f32)
                  + jnp.dot(s_hi, w_lo, preferred_element_type=f32) + b_ref[...])


def _adaln(cond8, w, b):
    n = w.shape[1]
    tn = 1536
    return pl.pallas_call(
        _adaln_kernel,
        grid=(n // tn,),
        in_specs=[_const_spec((8, D)), pl.BlockSpec((D, tn), lambda j: (0, j)),
                  pl.BlockSpec((1, tn), lambda j: (0, j))],
        out_specs=pl.BlockSpec((8, tn), lambda j: (0, j)),
        out_shape=jax.ShapeDtypeStruct((8, n), f32),
        compiler_params=_cp(("arbitrary",)),
        name="adaln",
    )(cond8, w, b.reshape(1, n))


def _mod_rows(m8):
    m = m8[:3].reshape(3, 6, D)
    return jnp.pad(m, ((0, 0), (0, 2), (0, 0))).reshape(24, D)


N_CTX_TILES = T_CTX // ROW_TILE


def _token_specs(width):
    return [pl.BlockSpec((ROW_TILE, width), lambda i: (jnp.minimum(i, N_CTX_TILES - 1), 0)),
            pl.BlockSpec((ROW_TILE, width), lambda i: (jnp.maximum(i - N_CTX_TILES, 0), 0))]


def _token_rows(xc_ref, xl_ref):
    return jnp.where(pl.program_id(0) < N_CTX_TILES, xc_ref[...], xl_ref[...])


def _store_token_major(ref, x):
    n = x.shape[0]
    for s in range(SUBS):
        ref[pl.ds(s, n, stride=SUBS), :] = x[:, s * LANES:(s + 1) * LANES]


def _load_token_major(ref, n, row0=0):
    return jnp.concatenate([ref[pl.ds(row0 * SUBS + s, n, stride=SUBS), :] for s in range(SUBS)], axis=1)


def _token_major_spec(rows, index_map):
    return pl.BlockSpec((rows * SUBS, LANES), index_map)


def _rope_tables(n, rot_dim, lanes, lane0, copies=1):
    rows_count = n // GRID_W
    rows = np.repeat(np.arange(rows_count), GRID_W).astype(np.float64)
    cols = np.tile(np.arange(GRID_W), rows_count).astype(np.float64)
    d_axis = rot_dim // 2
    inv = ROPE_THETA ** (-np.arange(0, d_axis, 2, dtype=np.float64) / d_axis)
    ang = np.concatenate([rows[:, None] * inv, cols[:, None] * inv], -1)
    c = np.ones((n, lanes), np.float32)
    s = np.zeros((n, lanes), np.float32)
    for j in range(copies):
        lo = lane0 + j * rot_dim
        c[:, lo:lo + rot_dim] = np.repeat(np.cos(ang), 2, axis=1)
        s[:, lo:lo + rot_dim] = np.repeat(np.sin(ang), 2, axis=1) * np.tile(np.array([-1.0, 1.0]), rot_dim // 2)
    return jnp.asarray(c), jnp.asarray(s)


L0_Q0 = 2 * A_WIDTH
L0_K0 = L0_Q0 + B_HEADS * LANES
L0_V0 = L0_K0 + B_KV * B_HD
L0_IN = L0_V0 + B_KV * B_HD


def _l0_kernel(*refs, latent):
    if latent:
        (sink_ref, x_ref, gn_ref, mod_ref, win_ref, gvn_ref, ws_ref, bsb_ref, gq_ref, gk_ref, wo_ref,
         cos_ref, sin_ref, kc_ref, vc_ref, xo_ref, zs, cat, qs, ks, vt, kcb, vct) = refs
        key_off = WINDOW
    else:
        (sink_ref, x_ref, gn_ref, mod_ref, win_ref, gvn_ref, ws_ref, bsb_ref, gq_ref, gk_ref, wo_ref,
         xo_ref, ko_ref, vo_ref, zs, cat, qs, ks, vt, kf, vf) = refs
        key_off = 0
    n = BLOCK_ROWS
    n_chunks = n // CHUNK
    low = lax.broadcasted_iota(jnp.int32, (CHUNK, LANES), 1) < B_HD

    if latent:
        zpad = jnp.zeros((WINDOW, LANES), bf16)
        for c0 in (0, 1 + n_chunks):
            ks[c0 * CHUNK:(c0 + 1) * CHUNK, :] = zpad
            vt[c0] = zpad
        kcb[...] = kc_ref[...].astype(bf16)
        for i in range(PAST // CHUNK):
            vct[i] = vc_ref[i * CHUNK:(i + 1) * CHUNK, :].T.astype(bf16)

    def project(c, carry):
        r = pl.ds(pl.multiple_of(c * PROJ_ROWS, PROJ_ROWS), PROJ_ROWS)
        h = _rms_rows(x_ref[r, :], gn_ref[...]) * (1.0 + mod_ref[1:2, :]) + mod_ref[0:1, :]
        zs[r, :] = jnp.dot(h.astype(bf16), win_ref[...], preferred_element_type=f32)
        return carry

    lax.fori_loop(0, n // PROJ_ROWS, project, 0)

    def prepare(c, carry):
        r = pl.ds(pl.multiple_of(c * CHUNK, CHUNK), CHUNK)
        u = jax.nn.gelu(zs[r, 0:A_WIDTH])
        v = jax.nn.gelu(zs[r, A_WIDTH:2 * A_WIDTH])
        mu = jnp.mean(v, -1, keepdims=True)
        var = jnp.mean(jnp.square(v - mu), -1, keepdims=True)
        vn = ((v - mu) * lax.rsqrt(var + EPS) * gvn_ref[...]).astype(bf16)
        for g in range(A_GROUPS):
            cs = slice(g * CHUNK, (g + 1) * CHUNK)
            mixed = jnp.dot(ws_ref[g], vn[:, cs], preferred_element_type=f32) + bsb_ref[g]
            cat[r, cs] = (u[:, cs] * mixed).astype(bf16)
        if latent:
            cs_, sn_ = cos_ref[r, :], sin_ref[r, :]
        for h in range(B_HEADS):
            hs = slice(h * LANES, (h + 1) * LANES)
            qh = zs[r, L0_Q0 + h * LANES:L0_Q0 + (h + 1) * LANES]
            qh = qh * lax.rsqrt(jnp.sum(qh * qh, -1, keepdims=True) * (1.0 / B_HD) + EPS) * gq_ref[:, hs]
            if latent:
                qh = qh * cs_ + _swap_pairs(qh) * sn_
            qs[r, hs] = qh.astype(bf16)
        k = zs[r, L0_K0:L0_K0 + LANES]
        k2 = k * k
        s0 = jnp.sum(jnp.where(low, k2, 0.0), -1, keepdims=True)
        s1 = jnp.sum(jnp.where(low, 0.0, k2), -1, keepdims=True)
        k = k * lax.rsqrt(jnp.where(low, s0, s1) * (1.0 / B_HD) + EPS) * gk_ref[...]
        vv = zs[r, L0_V0:L0_V0 + LANES]
        if latent:
            k = k * cs_ + _swap_pairs(k) * sn_
        else:
            kf[r, :] = k
            vf[r, :] = vv
        kr = pl.ds(pl.multiple_of(c * CHUNK + key_off, CHUNK), CHUNK)
        ks[kr, :] = k.astype(bf16)
        vt[c + key_off // CHUNK] = vv.T.astype(bf16)
        return carry

    lax.fori_loop(0, n_chunks, prepare, 0)

    def attend(r, rows, key_sets):
        q = jnp.concatenate([qs[r, h * LANES:(h + 1) * LANES] for h in range(B_HEADS)], axis=0)
        sk = jnp.concatenate([jnp.full((1, rows), sink_ref[h], f32) for h in range(B_HEADS)], axis=1)
        scores = []
        m = sk
        for k, _, keep in key_sets:
            s = lax.dot_general(k, q, (((1,), (1,)), ((), ())), preferred_element_type=f32)
            if keep is not None:
                s = jnp.where(keep, s, NEG_INF)
            scores.append(s)
            m = jnp.maximum(m, jnp.max(s, 0, keepdims=True))
        den = jnp.exp(sk - m)
        ot = None
        for s, (_, vts, _) in zip(scores, key_sets):
            e = jnp.exp(s - m)
            den = den + jnp.sum(e, 0, keepdims=True)
            eb = e.astype(bf16)
            for i, v_t in enumerate(vts):
                pv = jnp.dot(v_t, eb[i * CHUNK:(i + 1) * CHUNK, :], preferred_element_type=f32)
                ot = pv if ot is None else ot + pv
        ot = ot * (1.0 / den)
        for pair in range(B_HEADS // 2):
            f0 = (2 * pair // B_GROUP) * B_HD
            pair_t = jnp.concatenate([ot[f0:f0 + B_HD, 2 * pair * rows:(2 * pair + 1) * rows],
                                      ot[f0:f0 + B_HD, (2 * pair + 1) * rows:(2 * pair + 2) * rows]], axis=0)
            cat[r, A_WIDTH + pair * LANES:A_WIDTH + (pair + 1) * LANES] = pair_t.T.astype(bf16)

    if latent:
        span = CHUNK + 2 * WINDOW

        def attend_block(c, carry):
            start = pl.multiple_of(c * CHUNK, CHUNK)
            kr = pl.ds(start, span)
            kj = lax.broadcasted_iota(jnp.int32, (span, B_HEADS * CHUNK), 0)
            qi = lax.broadcasted_iota(jnp.int32, (span, B_HEADS * CHUNK), 1) & (CHUNK - 1)
            kpos = start - WINDOW + kj
            keep = (jnp.abs(kj - WINDOW - qi) <= WINDOW) & (kpos >= 0) & (kpos < n)
            attend(pl.ds(start, CHUNK), CHUNK,
                   [(ks[kr, :], [vt[c + i] for i in range(span // CHUNK)], keep),
                    (kcb[...], [vct[i] for i in range(PAST // CHUNK)], None)])
            return carry

        lax.fori_loop(0, n_chunks, attend_block, 0)
    else:
        def attend_seq(sq, carry):
            r = pl.ds(pl.multiple_of(sq * SEQ, SEQ), SEQ)
            attend(r, SEQ, [(ks[r, :], [vt[sq * (SEQ // CHUNK) + i] for i in range(SEQ // CHUNK)], None)])
            ko_ref[sq] = kf[r, :].T
            vo_ref[sq] = vf[r, :].T
            return carry

        lax.fori_loop(0, n // SEQ, attend_seq, 0)

    def output(c, carry):
        r = pl.ds(pl.multiple_of(c * PROJ_ROWS, PROJ_ROWS), PROJ_ROWS)
        y = jnp.dot(cat[r, :], wo_ref[...], preferred_element_type=f32)
        xo_ref[r, :] = x_ref[r, :] + mod_ref[2:3, :] * y
        return carry

    lax.fori_loop(0, n // PROJ_ROWS, output, 0)


def _l0_mixer(xc, xl, g_norm, mods, p, cache_k, cache_v):
    w = p['w_in']
    q = w[:, 2 * A_WIDTH:2 * A_WIDTH + B_HEADS * B_HD].reshape(D, B_HEADS, B_HD)
    zero = jnp.zeros((D, B_GROUP, B_HD), f32)
    q_slots = jnp.concatenate([jnp.concatenate([q[:, :B_GROUP], zero], axis=2),
                               jnp.concatenate([zero, q[:, B_GROUP:]], axis=2)], axis=1).reshape(D, B_HEADS * LANES)
    win = jnp.concatenate([w[:, :2 * A_WIDTH], q_slots, w[:, 2 * A_WIDTH + B_HEADS * B_HD:]], axis=1).astype(bf16)
    zg = jnp.zeros((B_HD,), f32)
    gq = jnp.concatenate([jnp.tile(jnp.concatenate([p['g_q'], zg]), B_GROUP),
                          jnp.tile(jnp.concatenate([zg, p['g_q']]), B_GROUP)]).reshape(1, B_HEADS * LANES) * B_SCALE
    gk = jnp.tile(p['g_k'], B_KV).reshape(1, LANES)
    gvn = p['g_vnorm'].reshape(1, A_WIDTH)
    ws = p['w_s'].astype(bf16)
    bsb = jnp.broadcast_to(p['b_s'][:, :, None], (A_GROUPS, CHUNK, CHUNK))
    wo = p['w_o'].astype(bf16)
    weights = (g_norm.reshape(1, D),)
    consts = (win, gvn, ws, bsb, gq, gk, wo)
    c_specs = [_const_spec(a.shape) for a in consts]
    smem = pl.BlockSpec(memory_space=pltpu.SMEM)
    row = pl.BlockSpec((BLOCK_ROWS, D), lambda b: (b, 0))
    kv = pl.BlockSpec((BLOCK_ROWS // SEQ, LANES, SEQ), lambda b: (b, 0, 0))

    def scratch(pad):
        return [pltpu.VMEM((BLOCK_ROWS, L0_IN), f32), pltpu.VMEM((BLOCK_ROWS, D), bf16),
                pltpu.VMEM((BLOCK_ROWS, B_HEADS * LANES), bf16), pltpu.VMEM((BLOCK_ROWS + pad, LANES), bf16),
                pltpu.VMEM(((BLOCK_ROWS + pad) // CHUNK, LANES, CHUNK), bf16)]

    kv_shape = jax.ShapeDtypeStruct((BATCH, LANES, SEQ), f32)
    state = pltpu.VMEM((BLOCK_ROWS, LANES), f32)
    xo_ctx, k_t, v_t = pl.pallas_call(
        functools.partial(_l0_kernel, latent=False),
        grid=(T_CTX // BLOCK_ROWS,),
        in_specs=[smem, row, _const_spec((1, D)), pl.BlockSpec((8, D), lambda b: (0, 0))] + c_specs,
        out_specs=[row, kv, kv],
        out_shape=[jax.ShapeDtypeStruct((T_CTX, D), f32), kv_shape, kv_shape],
        scratch_shapes=scratch(0) + [state, state],
        compiler_params=_cp(("parallel",)), name="l0_mixer_ctx",
    )(p['sink'], xc, *weights, mods, *consts)
    k_new = k_t.reshape(BATCH, B_KV, B_HD, SEQ).transpose(0, 3, 1, 2)
    v_new = v_t.reshape(BATCH, B_KV, B_HD, SEQ).transpose(0, 3, 1, 2)

    cos, sin = _rope_tables(DEC_SEQ, B_HD, LANES, 0, copies=LANES // B_HD)
    cache = pl.BlockSpec((None, PAST, LANES), lambda b: (b, 0, 0))
    past = [pltpu.VMEM((PAST, LANES), bf16), pltpu.VMEM((PAST // CHUNK, LANES, CHUNK), bf16)]
    xo_lat = pl.pallas_call(
        functools.partial(_l0_kernel, latent=True),
        grid=(DEC_BATCH,),
        in_specs=[smem, row, _const_spec((1, D)), pl.BlockSpec((8, D), lambda b: (1 + b, 0))] + c_specs + [
                  _const_spec(cos.shape), _const_spec(sin.shape), cache, cache],
        out_specs=row,
        out_shape=jax.ShapeDtypeStruct((T_LAT, D), f32),
        scratch_shapes=scratch(2 * WINDOW) + past,
        compiler_params=_cp(("parallel",)), name="l0_mixer_lat",
    )(p['sink'], xl, *weights, mods, *consts, cos, sin,
      cache_k.reshape(DEC_BATCH, PAST, LANES), cache_v.reshape(DEC_BATCH, PAST, LANES))
    return xo_ctx, xo_lat, k_new, v_new


C_SLOTS = C_HEADS * SLOT
C_PAIRS = C_HEADS // 2
L1_ROWS = 256


def _l1_kernel(*refs, latent):
    if latent:
        (x_ref, gn_ref, mod_ref, win_ref, gqa_ref, wuq_ref, gq_ref, gkva_ref, wuk_ref, wuvt_ref, gk_ref,
         wo_ref, wuqs_ref, qcos_ref, qsin_ref, kcos_ref, ksin_ref, cckv_ref, ckpe_ref, xo_ref,
         zs, cat, qs, ks, vt, wide, wide2) = refs
        n_ctx = PAST
    else:
        (x_ref, gn_ref, mod_ref, win_ref, gqa_ref, wuq_ref, gq_ref, gkva_ref, wuk_ref, wuvt_ref, gk_ref,
         wo_ref, xo_ref, ckvo_ref, kpeo_ref, zs, cat, qs, ks, vt, wide) = refs
        n_ctx = 0
    n = BLOCK_ROWS
    nt_dims = (((1,), (1,)), ((), ()))

    def inv_rms(v):
        return lax.rsqrt(jnp.sum(v * v, -1, keepdims=True) * (1.0 / C_QK) + EPS)

    def expand_keys(ckv_n, kslot, kb, rope_rows):
        cb = ckv_n.astype(bf16)
        key_rows = pl.ds(pl.multiple_of(kb * L1_ROWS, L1_ROWS), L1_ROWS)
        wide[...] = jnp.dot(cb, wuk_ref[...], preferred_element_type=f32)
        if rope_rows is not None:
            kcos = kcos_ref[rope_rows, :]
            turned = _swap_pairs(kslot) * ksin_ref[rope_rows, :]
        for h in range(C_HEADS):
            kh = wide[:, h * SLOT:(h + 1) * SLOT] + kslot
            if rope_rows is not None:
                kh = inv_rms(kh) * (kh * kcos + turned)
            else:
                kh = kh * inv_rms(kh) * gk_ref[...]
            ks[h, key_rows, :] = kh.astype(bf16)
        v_t = lax.dot_general(wuvt_ref[...], cb, nt_dims, preferred_element_type=f32).astype(bf16)
        for pair in range(C_PAIRS):
            vt[pair, kb] = v_t[pair * LANES:(pair + 1) * LANES, :]

    if latent:
        def past_keys(c, carry):
            r = pl.ds(pl.multiple_of(c * L1_ROWS, L1_ROWS), L1_ROWS)
            expand_keys(cckv_ref[r, :], ckpe_ref[r, :], c, None)
            return carry

        lax.fori_loop(0, PAST // L1_ROWS, past_keys, 0)

    def project(c, carry):
        r = pl.ds(pl.multiple_of(c * PROJ_ROWS, PROJ_ROWS), PROJ_ROWS)
        h = _rms_rows(x_ref[r, :], gn_ref[...]) * (1.0 + mod_ref[1:2, :]) + mod_ref[0:1, :]
        zs[r, :] = jnp.dot(h.astype(bf16), win_ref[...], preferred_element_type=f32)
        return carry

    lax.fori_loop(0, n // PROJ_ROWS, project, 0)

    def prepare(c, carry):
        r = pl.ds(pl.multiple_of(c * L1_ROWS, L1_ROWS), L1_ROWS)
        qa = _rms_rows(zs[r, 0:C_Q_LORA], gqa_ref[...]).astype(bf16)
        wide[...] = jnp.dot(qa, wuq_ref[...], preferred_element_type=f32)
        if latent:
            wide2[...] = jnp.dot(qa, wuqs_ref[...], preferred_element_type=f32)
            qcos, qsin = qcos_ref[r, :], qsin_ref[r, :]
        for h in range(C_HEADS):
            hs = slice(h * SLOT, (h + 1) * SLOT)
            qh = wide[:, hs]
            if latent:
                qh = inv_rms(qh) * (qh * qcos + wide2[:, hs] * qsin)
            else:
                qh = qh * inv_rms(qh) * gq_ref[...]
            qs[h, r, :] = qh.astype(bf16)
        ckv_n = _rms_rows(zs[r, C_Q_LORA:C_Q_LORA + C_KV_LORA], gkva_ref[...])
        kslot = zs[r, C_Q_LORA + C_KV_LORA:ODD_IN_PAD]
        if not latent:
            ckvo_ref[r, :] = ckv_n
            kpeo_ref[c] = kslot.T[C_NOPE:C_QK, :]
        expand_keys(ckv_n, kslot, c + n_ctx // L1_ROWS, r if latent else None)
        return carry

    lax.fori_loop(0, n // L1_ROWS, prepare, 0)

    low = lax.broadcasted_iota(jnp.int32, (2 * C_V, L1_ROWS), 0) < C_V
    n_kblocks = (n_ctx + n) // L1_ROWS
    pairs_per_step = 2 if latent else 4

    def attend(c, carry):
        r = pl.ds(pl.multiple_of(c * L1_ROWS, L1_ROWS), L1_ROWS)

        def values_t(pair, eb):
            if not latent:
                return jnp.dot(vt[pair, c], eb, preferred_element_type=f32)
            o_t = None
            for b in range(n_kblocks):
                pv = jnp.dot(vt[pair, b], eb[b * L1_ROWS:(b + 1) * L1_ROWS, :], preferred_element_type=f32)
                o_t = pv if o_t is None else o_t + pv
            return o_t

        def pairs_step(i, carry2):
            pairs = [i * pairs_per_step + j for j in range(pairs_per_step)]
            heads = [2 * p + hh for p in pairs for hh in range(2)]
            scores = [lax.dot_general(ks[h] if latent else ks[h, r, :], qs[h, r, :], nt_dims,
                                      preferred_element_type=f32) for h in heads]
            exps = [jnp.exp(s - jnp.max(s, 0, keepdims=True)) for s in scores]
            dens = [jnp.sum(e, 0, keepdims=True) for e in exps]
            outs = [values_t(h // 2, e.astype(bf16)) / den for h, e, den in zip(heads, exps, dens)]
            for j, pair in enumerate(pairs):
                cat[pair, r, :] = jnp.where(low, outs[2 * j], outs[2 * j + 1]).T.astype(bf16)
            return carry2

        return lax.fori_loop(0, C_PAIRS // pairs_per_step, pairs_step, carry)

    lax.fori_loop(0, n // L1_ROWS, attend, 0)

    def output(c, carry):
        r = pl.ds(pl.multiple_of(c * PROJ_ROWS, PROJ_ROWS), PROJ_ROWS)
        heads = jnp.concatenate([cat[pair, r, :] for pair in range(C_PAIRS)], axis=1)
        y = jnp.dot(heads, wo_ref[...], preferred_element_type=f32)
        xo_ref[r, :] = x_ref[r, :] + mod_ref[2:3, :] * y
        return carry

    lax.fori_loop(0, n // PROJ_ROWS, output, 0)


def _slot_cols(w, heads, width, lo, hi, lane0):
    k = w.shape[0]
    w3 = w.reshape(k, heads, width)[:, :, lo:hi]
    out = jnp.zeros((k, heads, SLOT), w.dtype).at[:, :, lane0:lane0 + (hi - lo)].set(w3)
    return out.reshape(k, heads * SLOT)


def _l1_mixer(xc, xl, g_norm, mods, p, cache_ckv, cache_kpe):
    w_in = jnp.zeros((D, ODD_IN_PAD), f32).at[:, :C_Q_LORA + C_KV_LORA].set(
        p['w_in'][:, :C_Q_LORA + C_KV_LORA]).at[
        :, C_Q_LORA + C_KV_LORA + C_NOPE:C_Q_LORA + C_KV_LORA + C_QK].set(p['w_in'][:, C_Q_LORA + C_KV_LORA:])
    wuq = _slot_cols(p['w_uq'], C_HEADS, C_QK, 0, C_QK, 0).astype(bf16)
    wuk = _slot_cols(p['w_ukv'], C_HEADS, C_NOPE + C_V, 0, C_NOPE, 0).astype(bf16)
    wuv_t = p['w_ukv'].reshape(C_KV_LORA, C_HEADS, C_NOPE + C_V)[:, :, C_NOPE:].reshape(
        C_KV_LORA, C_HEADS * C_V).T.astype(bf16)
    gq = jnp.zeros((1, SLOT), f32).at[0, :C_QK].set(p['g_q'] * C_SCALE)
    gk = jnp.zeros((1, SLOT), f32).at[0, :C_QK].set(p['g_k'])
    consts = (g_norm.reshape(1, D), w_in.astype(bf16), p['g_qa'].reshape(1, C_Q_LORA), wuq, gq,
              p['g_kva'].reshape(1, C_KV_LORA), wuk, wuv_t, gk, p['w_o'].astype(bf16))
    c_specs = [_const_spec(a.shape) for a in consts]
    row = pl.BlockSpec((BLOCK_ROWS, D), lambda b: (b, 0))
    n_ctx_blocks = T_CTX // BLOCK_ROWS

    def scratch(n_keys):
        return [pltpu.VMEM((BLOCK_ROWS, ODD_IN_PAD), f32), pltpu.VMEM((C_PAIRS, BLOCK_ROWS, LANES), bf16),
                pltpu.VMEM((C_HEADS, BLOCK_ROWS, SLOT), bf16), pltpu.VMEM((C_HEADS, n_keys, SLOT), bf16),
                pltpu.VMEM((C_PAIRS, n_keys // L1_ROWS, LANES, L1_ROWS), bf16),
                pltpu.VMEM((L1_ROWS, C_SLOTS), f32)]

    xo_ctx, ckv_new, kpe_t = pl.pallas_call(
        functools.partial(_l1_kernel, latent=False),
        grid=(n_ctx_blocks,),
        in_specs=[row, c_specs[0], pl.BlockSpec((8, D), lambda b: (0, 0))] + c_specs[1:],
        out_specs=[row, pl.BlockSpec((BLOCK_ROWS, C_KV_LORA), lambda b: (b, 0)),
                   pl.BlockSpec((BLOCK_ROWS // SEQ, C_ROPE, SEQ), lambda b: (b, 0, 0))],
        out_shape=[jax.ShapeDtypeStruct((T_CTX, D), f32), jax.ShapeDtypeStruct((T_CTX, C_KV_LORA), f32),
                   jax.ShapeDtypeStruct((BATCH, C_ROPE, SEQ), f32)],
        scratch_shapes=scratch(BLOCK_ROWS),
        compiler_params=_cp(("parallel",)), name="l1_mixer_ctx",
    )(xc, consts[0], mods, *consts[1:])

    cos, sin = _rope_tables(DEC_SEQ, C_ROPE, SLOT, C_NOPE)
    def pair_swap(a):
        pairs = a.reshape(a.shape[:-1] + (a.shape[-1] // 2, 2))
        return jnp.stack([pairs[..., 1], pairs[..., 0]], axis=-1).reshape(a.shape)

    w_rope = p['w_uq'].reshape(C_Q_LORA, C_HEADS, C_QK)[:, :, C_NOPE:]
    wuq_swapped = jnp.zeros((C_Q_LORA, C_HEADS, SLOT), f32).at[:, :, C_NOPE:C_QK].set(pair_swap(w_rope)).reshape(
        C_Q_LORA, C_SLOTS).astype(bf16)
    rope = (wuq_swapped, gq * cos, pair_swap(gq) * sin, gk * cos, pair_swap(gk) * sin)
    ckpe = jnp.zeros((DEC_BATCH, PAST, SLOT), f32).at[:, :, C_NOPE:C_QK].set(cache_kpe)
    xo_lat = pl.pallas_call(
        functools.partial(_l1_kernel, latent=True),
        grid=(DEC_BATCH,),
        in_specs=[pl.BlockSpec((BLOCK_ROWS, D), lambda b: (b, 0), pipeline_mode=pl.Buffered(1)), c_specs[0],
                  pl.BlockSpec((8, D), lambda b: (1 + b, 0))] + c_specs[1:] + [_const_spec(a.shape) for a in rope] + [
                  pl.BlockSpec((None, PAST, C_KV_LORA), lambda b: (b, 0, 0)),
                  pl.BlockSpec((None, PAST, SLOT), lambda b: (b, 0, 0))],
        out_specs=row,
        out_shape=jax.ShapeDtypeStruct((T_LAT, D), f32),
        scratch_shapes=scratch(PAST + BLOCK_ROWS) + [pltpu.VMEM((L1_ROWS, C_SLOTS), f32)],
        compiler_params=_cp(("parallel",)), name="l1_mixer_lat",
    )(xl, consts[0], mods, *consts[1:], *rope, cache_ckv, ckpe)
    return xo_ctx, xo_lat, ckv_new, kpe_t.transpose(0, 2, 1)


ROUTER_ROWS = 40


def _router_kernel(xc_ref, xl_ref, gn_ref, mod_ref, whi_ref, wlo_ref, br_ref, h_ref, route_ref):
    h = _rms_rows(_token_rows(xc_ref, xl_ref), gn_ref[...]) * (1.0 + mod_ref[4:5, :]) + mod_ref[3:4, :]
    _store_token_major(h_ref, h)
    h_hi, h_lo = _split_bf16(h)
    nt = (((1,), (1,)), ((), ()))
    logits = (lax.dot_general(whi_ref[...], h_hi, nt, preferred_element_type=f32)
              + lax.dot_general(whi_ref[...], h_lo, nt, preferred_element_type=f32)
              + lax.dot_general(wlo_ref[...], h_hi, nt, preferred_element_type=f32))
    logits = logits[0:ROUTER_ROWS, :] + br_ref[0:ROUTER_ROWS, :]
    row_i = lax.broadcasted_iota(jnp.int32, logits.shape, 0)
    row = row_i.astype(f32)
    big = 1e6
    is_g = (row_i >= N_EXPERTS) & (row_i < N_EXPERTS + N_GROUPS)
    lg = jnp.where(is_g, logits, -jnp.inf)
    mg = jnp.max(lg, 0, keepdims=True)
    gsel = jnp.min(jnp.where(lg == mg, row, big), 0, keepdims=True) - N_EXPERTS
    pg_sel = 1.0 / jnp.sum(jnp.where(is_g, jnp.exp(lg - mg), 0.0), 0, keepdims=True)
    in_grp = (row_i < N_EXPERTS) & ((row_i >> 3).astype(f32) == gsel)
    le = jnp.where(in_grp, logits, -jnp.inf)
    m1 = jnp.max(le, 0, keepdims=True)
    i1 = jnp.min(jnp.where(le == m1, row, big), 0, keepdims=True)
    le2 = jnp.where(row == i1, -jnp.inf, le)
    m2 = jnp.max(le2, 0, keepdims=True)
    i2 = jnp.min(jnp.where(le2 == m2, row, big), 0, keepdims=True)
    e2 = jnp.exp(m2 - m1)
    w1 = pg_sel / (1.0 + e2)
    w2 = pg_sel * e2 / (1.0 + e2)
    sub = lax.broadcasted_iota(jnp.int32, route_ref.shape, 0)
    route_ref[...] = jnp.where(sub == 0, i1, jnp.where(sub == 1, i2, jnp.where(sub == 2, w1,
                                                                                jnp.where(sub == 3, w2, 0.0))))


def _router(xc, xl, g_norm, mods, p):
    wr = jnp.zeros((128, D), f32).at[:N_EXPERTS].set(p['w_re'].T).at[
        N_EXPERTS:N_EXPERTS + N_GROUPS].set(p['w_rg'].T)
    w_hi, w_lo = _split_bf16(wr)
    br =jnp.zeros((128, 1), f32).at[:N_EXPERTS, 0].set(p['b_re']).at[
        N_EXPERTS:N_EXPERTS + N_GROUPS, 0].set(p['b_rg'])
    return pl.pallas_call(
        _router_kernel,
        grid=(T // ROW_TILE,),
        in_specs=_token_specs(D) + [
                  _const_spec((1, D)),
                  pl.BlockSpec((8, D), lambda i: (_sample_of_tile(i, ROW_TILE), 0)),
                  _const_spec((128, D)), _const_spec((128, D)), _const_spec((128, 1))],
        out_specs=[_token_major_spec(ROW_TILE, lambda i: (i, 0)), pl.BlockSpec((8, ROW_TILE), lambda i: (0, i))],
        out_shape=[jax.ShapeDtypeStruct((T * SUBS, LANES), f32), jax.ShapeDtypeStruct((8, T), f32)],
        compiler_params=_cp(("parallel",)), name="router",
    )(xc, xl, g_norm.reshape(1, D), mods, w_hi, w_lo, br)


PLAN_FIRST_TILE, PLAN_TILES, PLAN_N_USED = 0, 1, 2


def _plan_kernel(rt_ref, pos_ref, plan_ref, rank):
    n_blk = T // 128
    e_col = lax.broadcasted_iota(jnp.int32, (N_EXPERTS, 128), 0).astype(f32)
    ri = lax.broadcasted_iota(jnp.int32, (128, 128), 0)
    ci = lax.broadcasted_iota(jnp.int32, (128, 128), 1)
    before = jnp.where(ri < ci, 1.0, 0.0).astype(bf16)

    def picks(b):
        cs = slice(b * 128, (b + 1) * 128)
        return rt_ref[0:1, cs] == e_col, rt_ref[1:2, cs] == e_col

    counts = jnp.zeros((N_EXPERTS, 1), f32)
    for b in range(n_blk):
        m0, m1 = picks(b)
        m = jnp.where(m0, 1.0, 0.0) + jnp.where(m1, 1.0, 0.0)
        rank[:, b * 128:(b + 1) * 128] = jnp.dot(m.astype(bf16), before, preferred_element_type=f32) + counts
        counts = counts + jnp.sum(m, axis=1, keepdims=True)

    tiles = jnp.floor((counts + (MOE_TILE - 1.0)) * (1.0 / MOE_TILE))
    er = lax.broadcasted_iota(jnp.int32, (N_EXPERTS, N_EXPERTS), 0)
    ec = lax.broadcasted_iota(jnp.int32, (N_EXPERTS, N_EXPERTS), 1)
    earlier = jnp.where(ec < er, 1.0, 0.0).astype(bf16)
    tile_start = jnp.dot(earlier, jnp.broadcast_to(tiles, (N_EXPERTS, 128)).astype(bf16),
                         preferred_element_type=f32)
    row_start = tile_start * MOE_TILE

    sub = lax.broadcasted_iota(jnp.int32, (8, 128), 0)
    for b in range(n_blk):
        m0, m1 = picks(b)
        base = rank[:, b * 128:(b + 1) * 128] + row_start
        p0 = jnp.sum(jnp.where(m0, base, 0.0), axis=0, keepdims=True)
        p1 = jnp.sum(jnp.where(m1, base, 0.0), axis=0, keepdims=True)
        pos_ref[:, b * 128:(b + 1) * 128] = jnp.where(sub == 0, p0, jnp.where(sub == 1, p1, 0.0)).astype(jnp.int32)

    tile_end = tile_start + tiles
    n_used = jnp.max(tile_end, axis=0, keepdims=True)
    diag =(lax.broadcasted_iota(jnp.int32, (N_EXPERTS, 128), 0)
            == lax.broadcasted_iota(jnp.int32, (N_EXPERTS, 128), 1))
    first = jnp.sum(jnp.where(diag, tile_start, 0.0), axis=0, keepdims=True)
    count = jnp.sum(jnp.where(diag, tiles, 0.0), axis=0, keepdims=True)
    rows = jnp.where(sub == PLAN_FIRST_TILE, first,
                     jnp.where(sub == PLAN_TILES, count, jnp.where(sub == PLAN_N_USED, n_used, 0.0)))
    plan_ref[...] = rows.astype(jnp.int32)


def _slot_code(t, k):
    return t * SUBS + k * (SUBS // 2)


def _code_offset(code):
    return pl.multiple_of(code & ~(SUBS - 1), SUBS)


def _code_gate_index(code):
    return code >> 2


PAD_CODE = T * SUBS


def _invert_kernel(pos_ref, plan_ref, code_ref):
    def pad_tile(tile, carry):
        for u in range(MOE_TILE):
            code_ref[tile * MOE_TILE + u] = PAD_CODE
        return carry

    def pad_last_tile(e, carry):
        return pad_tile(jnp.maximum(plan_ref[PLAN_FIRST_TILE, e] + plan_ref[PLAN_TILES, e] - 1, 0), carry)
    lax.fori_loop(0, N_EXPERTS, pad_last_tile, 0)
    lax.fori_loop(plan_ref[PLAN_N_USED, 0], MOE_TILES, pad_tile, 0)

    group = 16
    for k in range(2):
        def place(i, carry):
            t0 = i * group
            slots = [pos_ref[k * T + t0 + u] for u in range(group)]
            for u, s in enumerate(slots):
                code_ref[s] = _slot_code(t0 + u, k)
            return carry
        lax.fori_loop(0, T // group, place, 0)


def _route_plan(route_t):
    pos, plan = pl.pallas_call(
        _plan_kernel,
        out_shape=[jax.ShapeDtypeStruct((8, T), jnp.int32), jax.ShapeDtypeStruct((8, LANES), jnp.int32)],
        scratch_shapes=[pltpu.VMEM((N_EXPERTS, T), f32)],
        compiler_params=_cp(None), name="route_plan",
    )(route_t)
    smem = pl.BlockSpec(memory_space=pltpu.SMEM)
    codes = pl.pallas_call(
        _invert_kernel,
        in_specs=[smem, smem], out_specs=smem,
        out_shape=jax.ShapeDtypeStruct((MOE_ROWS,), jnp.int32),
        name="route_invert",
    )(pos[0:2].reshape(2 * T), plan)
    gates = jnp.pad(route_t[2:4].T.reshape(2 * T), (0, 8))
    return plan[PLAN_FIRST_TILE, :N_EXPERTS], plan[PLAN_TILES, :N_EXPERTS], codes, gates


def _tile_index(i):
    return jnp.minimum(i, MOE_TILES - 1)


TM_ROWS = T * SUBS
SCATTER_GROUP = 16
STAGES = 4
LAST_EXPERT = N_EXPERTS - 1


def _tile_rows(g):
    return pl.ds(pl.multiple_of(g * MOE_TILE, MOE_TILE), MOE_TILE)


def _expert_tile_pairs(first, count, tile_step):
    def pair(pp, carry):
        for parity in range(2):
            g = 2 * pp + parity
            pl.when((g >= first) & (g < first + count))(functools.partial(tile_step, g, parity))
        return carry
    lax.fori_loop(first // 2, (first + count + 1) // 2, pair, 0)


def _gather_tile(code_ref, tile, xs, gbuf):
    base = tile * MOE_TILE
    for r in range(MOE_TILE):
        gbuf[r * SUBS:(r + 1) * SUBS, :] = xs[pl.ds(_code_offset(code_ref[base + r]), SUBS), :]


def _moe_up_kernel(first_ref, count_ref, code_ref, h_hbm, w1_ref, w3_ref, hh_hbm,
                   xs, gbuf_a, gbuf_b, w13, obuf, sem_x, sem_o):
    e = pl.program_id(0)
    first, count = first_ref[e], count_ref[e]
    gbufs = (gbuf_a, gbuf_b)

    def out_copy(slot, g):
        return pltpu.make_async_copy(obuf.at[slot], hh_hbm.at[_tile_rows(g), :], sem_o.at[slot])

    @pl.when(e == 0)
    def _():
        cp = pltpu.make_async_copy(h_hbm, xs.at[pl.ds(0, TM_ROWS), :], sem_x)
        cp.start()
        xs[TM_ROWS:TM_ROWS + SUBS, :] = jnp.zeros((SUBS, LANES), f32)
        cp.wait()
        _gather_tile(code_ref, 0, xs, gbuf_a)

    @pl.when(count > 0)
    def _():
        w13[:, :D_EXPERT] = w1_ref[0].astype(bf16)
        w13[:, D_EXPERT:] = w3_ref[0].astype(bf16)

    def tile_step(g, parity):
        _gather_tile(code_ref, _tile_index(g + 1), xs, gbufs[1 - parity])
        x3 = jnp.swapaxes(gbufs[parity][...].reshape(MOE_TILE, SUBS, LANES), 0, 1)
        x = jnp.concatenate([x3[s] for s in range(SUBS)], axis=1).astype(bf16)
        h13 = jnp.dot(x, w13[...], preferred_element_type=f32)
        hh = (_silu(h13[:, :D_EXPERT]) * h13[:, D_EXPERT:]).astype(bf16)

        slot = g % STAGES

        @pl.when(g >= STAGES)
        def _():
            out_copy(slot, g).wait()
        obuf[slot] = hh
        out_copy(slot, g).start()

    _expert_tile_pairs(first, count, tile_step)

    @pl.when(e == LAST_EXPERT)
    def _():
        n_used = first + count
        for slot in range(STAGES):
            pl.when(n_used > slot)(lambda slot=slot: out_copy(slot, 0).wait())
        obuf[0] = jnp.zeros((MOE_TILE, D_EXPERT), bf16)

        def zero_tile(g, carry):
            cp = out_copy(0, g)
            cp.start()
            cp.wait()
            return carry
        lax.fori_loop(n_used, MOE_TILES, zero_tile, 0)


def _scatter_tile(code_ref, gate_ref, tile, ybuf, acc):
    base = tile * MOE_TILE
    for g0 in range(0, MOE_TILE, SCATTER_GROUP):
        rows = range(g0, g0 + SCATTER_GROUP)
        codes = [code_ref[base + r] for r in rows]
        new = [acc[pl.ds(_code_offset(c), SUBS), :]
               + gate_ref[_code_gate_index(c)] * ybuf[r * SUBS:(r + 1) * SUBS, :]
               for r, c in zip(rows, codes)]
        for c, v in zip(codes, new):
            acc[pl.ds(_code_offset(c), SUBS), :] = v


RES_ROWS = 256


def _residual_out(x_hbm, y_hbm, tok0, sample_of_chunk, mod_ref, acc, rin, rout, sem_r, sem_w):
    n_chunks = x_hbm.shape[0] // RES_ROWS

    def rows(c):
        return pl.ds(pl.multiple_of(c * RES_ROWS, RES_ROWS), RES_ROWS)

    def in_copy(slot, c):
        return pltpu.make_async_copy(x_hbm.at[rows(c), :], rin.at[slot], sem_r.at[slot])

    def out_copy(slot, c):
        return pltpu.make_async_copy(rout.at[slot], y_hbm.at[rows(c), :], sem_w.at[slot])

    for c in range(STAGES - 1):
        in_copy(c, c).start()

    def ring(cc, carry):
        for slot in range(STAGES):
            c = STAGES * cc + slot
            in_copy(slot, c).wait()
            ahead = c + STAGES - 1

            @pl.when(ahead < n_chunks)
            def _():
                in_copy((slot + STAGES - 1) % STAGES, ahead).start()

            @pl.when(c >= STAGES)
            def _():
                out_copy(slot, c).wait()
            delta = _load_token_major(acc, RES_ROWS, tok0 + c * RES_ROWS)
            gate = mod_ref[pl.ds(sample_of_chunk(c) * 8 + 5, 1), :]
            rout[slot] = rin[slot] + gate * delta
            out_copy(slot, c).start()
        return carry

    lax.fori_loop(0, n_chunks // STAGES, ring, 0)
    for slot in range(STAGES):
        out_copy(slot, 0).wait()


def _moe_down_kernel(first_ref, count_ref, code_ref, gate_ref, hh_hbm, w2_ref, xc_hbm, xl_hbm, mod_ref,
                     yc_hbm, yl_hbm, acc, ybuf_a, ybuf_b, w2b, ibuf, rin, rout, sem_i, sem_r, sem_w):
    e = pl.program_id(0)
    first, count = first_ref[e], count_ref[e]
    n_used = first_ref[LAST_EXPERT] + count_ref[LAST_EXPERT]
    ybufs = (ybuf_a, ybuf_b)

    def in_copy(slot, g):
        return pltpu.make_async_copy(hh_hbm.at[_tile_rows(g), :], ibuf.at[slot], sem_i.at[slot])

    @pl.when(e == 0)
    def _():
        for g in range(STAGES - 1):
            in_copy(g, g).start()

        def zero(c, carry):
            acc[pl.ds(pl.multiple_of(c * 1024, 1024), 1024), :] = jnp.zeros((1024, LANES), f32)
            return carry
        lax.fori_loop(0, TM_ROWS // 1024, zero, 0)
        acc[TM_ROWS:TM_ROWS + SUBS, :] = jnp.zeros((SUBS, LANES), f32)
        ybuf_b[...] = jnp.zeros_like(ybuf_b)

    @pl.when(count > 0)
    def _():
        w2b[...] = w2_ref[0].astype(bf16)

    def tile_step(g, parity):
        slot = g % STAGES
        in_copy(slot, g).wait()
        ahead = g + STAGES - 1

        @pl.when(ahead < n_used)
        def _():
            in_copy(ahead % STAGES, ahead).start()
        _store_token_major(ybufs[parity], jnp.dot(ibuf[slot], w2b[...], preferred_element_type=f32))
        _scatter_tile(code_ref, gate_ref, jnp.maximum(g - 1, 0), ybufs[1 - parity], acc)

    _expert_tile_pairs(first, count, tile_step)

    @pl.when(e == LAST_EXPERT)
    def _():
        for parity in range(2):
            pl.when((n_used > 0) & ((n_used - 1) % 2 == parity))(
                functools.partial(_scatter_tile, code_ref, gate_ref, n_used - 1, ybufs[parity], acc))
        _residual_out(xc_hbm, yc_hbm, 0, lambda c: 0, mod_ref, acc, rin, rout, sem_r, sem_w)
        _residual_out(xl_hbm, yl_hbm, T_CTX, lambda c: 1 + c // (DEC_SEQ // RES_ROWS),
                      mod_ref, acc, rin, rout, sem_r, sem_w)


def _moe(h_tm, route_t, p, xc, xl, mods):
    first_tile, n_tiles, codes, gates = _route_plan(route_t)
    tile_rows = pltpu.VMEM((MOE_TILE * SUBS, LANES), f32)
    staging = pltpu.VMEM((STAGES, MOE_TILE, D_EXPERT), bf16)
    res_rows = pltpu.VMEM((STAGES, RES_ROWS, D), f32)
    hbm = pl.BlockSpec(memory_space=pl.ANY)
    hh = pl.pallas_call(
        _moe_up_kernel,
        grid_spec=pltpu.PrefetchScalarGridSpec(
            num_scalar_prefetch=3, grid=(N_EXPERTS,),
            in_specs=[pl.BlockSpec(memory_space=pl.ANY),
                      pl.BlockSpec((1, D, D_EXPERT), lambda e, f, n, c: (e, 0, 0)),
                      pl.BlockSpec((1, D, D_EXPERT), lambda e, f, n, c: (e, 0, 0))],
            out_specs=pl.BlockSpec(memory_space=pl.ANY),
            scratch_shapes=[pltpu.VMEM((TM_ROWS + SUBS, LANES), f32), tile_rows, tile_rows,
                            pltpu.VMEM((D, 2 * D_EXPERT), bf16), staging,
                            pltpu.SemaphoreType.DMA(()), pltpu.SemaphoreType.DMA((STAGES,))]),
        out_shape=jax.ShapeDtypeStruct((MOE_ROWS, D_EXPERT), bf16),
        compiler_params=_cp(("arbitrary",)), name="moe_up",
    )(first_tile, n_tiles, codes, h_tm, p['w1'], p['w3'])
    return pl.pallas_call(
        _moe_down_kernel,
        grid_spec=pltpu.PrefetchScalarGridSpec(
            num_scalar_prefetch=4, grid=(N_EXPERTS,),
            in_specs=[hbm, pl.BlockSpec((1, D_EXPERT, D), lambda e, f, n, c, g: (e, 0, 0)), hbm, hbm,
                      pl.BlockSpec((24, D), lambda e, f, n, c, g: (0, 0), pipeline_mode=pl.Buffered(1))],
            out_specs=[hbm, hbm],
            scratch_shapes=[pltpu.VMEM((TM_ROWS + SUBS, LANES), f32), tile_rows, tile_rows,
                            pltpu.VMEM((D_EXPERT, D), bf16), staging, res_rows, res_rows,
                            pltpu.SemaphoreType.DMA((STAGES,)), pltpu.SemaphoreType.DMA((STAGES,)),
                            pltpu.SemaphoreType.DMA((STAGES,))]),
        out_shape=[jax.ShapeDtypeStruct((T_CTX, D), f32), jax.ShapeDtypeStruct((T_LAT, D), f32)],
        compiler_params=_cp(("arbitrary",)), name="moe_down",
    )(first_tile, n_tiles, codes, gates, hh, p['w2'], xc, xl, mods)


def kernel(x_prompt, x_sample, cache_l0_k, cache_l0_v, cache_l1_ckv, cache_l1_kpe, c, c_ctx, l0_g_norm1, l0_g_norm2, l0_w_ada, l0_b_ada, l0_w_in, l0_g_vnorm, l0_w_s, l0_b_s, l0_g_q, l0_g_k, l0_sink, l0_w_o, l0_w_rg, l0_b_rg, l0_w_re, l0_b_re, l0_w1, l0_w3, l0_w2, l1_g_norm1, l1_g_norm2, l1_w_ada, l1_b_ada, l1_w_in, l1_g_qa, l1_w_uq, l1_g_kva, l1_w_ukv, l1_g_q, l1_g_k, l1_w_o, l1_w_rg, l1_b_rg, l1_w_re, l1_b_re, l1_w1, l1_w3, l1_w2):
    p0 = dict(w_in=l0_w_in, g_vnorm=l0_g_vnorm, w_s=l0_w_s, b_s=l0_b_s, g_q=l0_g_q, g_k=l0_g_k, sink=l0_sink,
              w_o=l0_w_o, w_rg=l0_w_rg, b_rg=l0_b_rg, w_re=l0_w_re, b_re=l0_b_re, w1=l0_w1, w3=l0_w3, w2=l0_w2)
    p1 = dict(w_in=l1_w_in, g_qa=l1_g_qa, w_uq=l1_w_uq, g_kva=l1_g_kva, w_ukv=l1_w_ukv, g_q=l1_g_q, g_k=l1_g_k,
              w_o=l1_w_o, w_rg=l1_w_rg, b_rg=l1_b_rg, w_re=l1_w_re, b_re=l1_b_re, w1=l1_w1, w3=l1_w3, w2=l1_w2)

    cond8 = jnp.zeros((8, D), f32).at[0].set(c_ctx).at[1:1 + DEC_BATCH].set(c)
    mods0 = _mod_rows(_adaln(cond8, l0_w_ada, l0_b_ada))
    mods1 = _mod_rows(_adaln(cond8, l1_w_ada, l1_b_ada))

    xc0 = x_prompt.reshape(T_CTX, D)
    xl0 = x_sample.reshape(T_LAT, D)

    xc0m, xl0m, k_new, v_new = _l0_mixer(xc0, xl0, l0_g_norm1, mods0, p0, cache_l0_k, cache_l0_v)
    h0, route0 = _router(xc0m, xl0m, l0_g_norm2, mods0, p0)
    xc1, xl1 = _moe(h0, route0, p0, xc0m, xl0m, mods0)

    xc1m, xl1m, ckv_new, kpe_new = _l1_mixer(xc1, xl1, l1_g_norm1, mods1, p1, cache_l1_ckv, cache_l1_kpe)
    h1, route1 = _router(xc1m, xl1m, l1_g_norm2, mods1, p1)
    y_prompt, y_sample = _moe(h1, route1, p1, xc1m, xl1m, mods1)
    return (y_prompt.reshape(BATCH, SEQ, D), y_sample.reshape(DEC_BATCH, DEC_SEQ, D), k_new, v_new,
            ckv_new.reshape(BATCH, SEQ, C_KV_LORA), kpe_new.reshape(BATCH, SEQ, C_ROPE))
```

```python
import functools

import jax
import jax.numpy as jnp
import numpy as np
from jax import lax
from jax.experimental import pallas as pl
from jax.experimental.pallas import tpu as pltpu

f32 = jnp.float32
bf16 = jnp.bfloat16

D = 1024
BATCH, SEQ = 32, 256
DEC_BATCH, DEC_SEQ = 2, 1024
PAST = 512
T_CTX = BATCH * SEQ
T_LAT = DEC_BATCH * DEC_SEQ
T = T_CTX + T_LAT
GRID_W = 64
CHUNK = 128
WINDOW = 128
ROPE_THETA = 10000.0
EPS = 1e-6
NEG_INF = -1e30
LANES = 128
SUBS = D // LANES

A_WIDTH = 512
A_GROUPS = 4
B_HEADS, B_KV, B_GROUP, B_HD = 8, 2, 4, 64
B_SCALE = B_HD ** -0.5

C_HEADS, C_Q_LORA, C_KV_LORA, C_NOPE, C_ROPE, C_V = 16, 384, 256, 64, 32, 64
C_QK = C_NOPE + C_ROPE
C_SCALE = C_QK ** -0.5
ODD_IN_PAD = 768
SLOT = 128

N_GROUPS, N_EXPERTS, D_EXPERT = 4, 32, 256

N_SAMPLES = 1 + DEC_BATCH
MOD_ROWS = 8
SHIFT1, SCALE1, GATE1, SHIFT2, SCALE2, GATE2 = range(6)

ROW_TILE = 1024
BLOCK_ROWS = 1024
PROJ_ROWS = 512
ADALN_COLS = 1536
MOE_TILE = 256
MOE_ROWS = 2 * T + N_EXPERTS * MOE_TILE
MOE_TILES = MOE_ROWS // MOE_TILE
VMEM_CAP = 56 * 1024 * 1024


def _cp(sem, vmem=VMEM_CAP):
    return pltpu.CompilerParams(dimension_semantics=sem, vmem_limit_bytes=vmem)


def _const_spec(shape):
    nd = len(shape)
    return pl.BlockSpec(shape, lambda *_: (0,) * nd, pipeline_mode=pl.Buffered(1))


def _sample_of_tile(i, tile):
    n_ctx = T_CTX // tile
    per_lat = DEC_SEQ // tile
    return jnp.where(i < n_ctx, 0, 1 + (i - n_ctx) // per_lat)


def _mod(mod_ref, row):
    return mod_ref[row:row + 1, :]


def _silu(x):
    return x * jax.nn.sigmoid(x)


def _rms_rows(x, g):
    return x * lax.rsqrt(jnp.mean(x * x, -1, keepdims=True) + EPS) * g


def _swap_pairs(x):
    lane = lax.broadcasted_iota(jnp.int32, x.shape, x.ndim - 1)
    nxt = pltpu.roll(x, x.shape[-1] - 1, x.ndim - 1)
    prv = pltpu.roll(x, 1, x.ndim - 1)
    return jnp.where((lane & 1) == 0, nxt, prv)


def _split_bf16(x):
    hi = x.astype(bf16)
    return hi, (x - hi.astype(f32)).astype(bf16)


def _adaln_kernel(c_ref, w_ref, b_ref, o_ref):
    s_hi, s_lo = _split_bf16(_silu(c_ref[...]))
    w_hi, w_lo = _split_bf16(w_ref[...])
    o_ref[...] = (jnp.dot(s_hi, w_hi, preferred_element_type=f32) + jnp.dot(s_lo, w_hi, preferred_element_type=f32)
                  + jnp.dot(s_hi, w_lo, preferred_element_type=f32) + b_ref[...])


def _adaln(cond8, w, b):
    n = w.shape[1]
    return pl.pallas_call(
        _adaln_kernel,
        grid=(n // ADALN_COLS,),
        in_specs=[_const_spec((MOD_ROWS, D)), pl.BlockSpec((D, ADALN_COLS), lambda j: (0, j)),
                  pl.BlockSpec((1, ADALN_COLS), lambda j: (0, j))],
        out_specs=pl.BlockSpec((MOD_ROWS, ADALN_COLS), lambda j: (0, j)),
        out_shape=jax.ShapeDtypeStruct((MOD_ROWS, n), f32),
        compiler_params=_cp(("arbitrary",)),
        name="adaln",
    )(cond8, w, b.reshape(1, n))


def _mod_rows(m8):
    m = m8[:N_SAMPLES].reshape(N_SAMPLES, 6, D)
    return jnp.pad(m, ((0, 0), (0, MOD_ROWS - 6), (0, 0))).reshape(N_SAMPLES * MOD_ROWS, D)


N_CTX_TILES = T_CTX // ROW_TILE


def _token_specs(width):
    return [pl.BlockSpec((ROW_TILE, width), lambda i: (jnp.minimum(i, N_CTX_TILES - 1), 0)),
            pl.BlockSpec((ROW_TILE, width), lambda i: (jnp.maximum(i - N_CTX_TILES, 0), 0))]


def _token_rows(xc_ref, xl_ref):
    return jnp.where(pl.program_id(0) < N_CTX_TILES, xc_ref[...], xl_ref[...])


def _store_token_major(ref, x):
    n = x.shape[0]
    for s in range(SUBS):
        ref[pl.ds(s, n, stride=SUBS), :] = x[:, s * LANES:(s + 1) * LANES]


def _load_token_major(ref, n, row0=0):
    return jnp.concatenate([ref[pl.ds(row0 * SUBS + s, n, stride=SUBS), :] for s in range(SUBS)], axis=1)


def _token_major_spec(rows, index_map):
    return pl.BlockSpec((rows * SUBS, LANES), index_map)


def _rope_tables(n, rot_dim, lanes, lane0, copies=1):
    rows_count = n // GRID_W
    rows = np.repeat(np.arange(rows_count), GRID_W).astype(np.float64)
    cols = np.tile(np.arange(GRID_W), rows_count).astype(np.float64)
    d_axis = rot_dim // 2
    inv = ROPE_THETA ** (-np.arange(0, d_axis, 2, dtype=np.float64) / d_axis)
    ang = np.concatenate([rows[:, None] * inv, cols[:, None] * inv], -1)
    c = np.ones((n, lanes), np.float32)
    s = np.zeros((n, lanes), np.float32)
    for j in range(copies):
        lo = lane0 + j * rot_dim
        c[:, lo:lo + rot_dim] = np.repeat(np.cos(ang), 2, axis=1)
        s[:, lo:lo + rot_dim] = np.repeat(np.sin(ang), 2, axis=1) * np.tile(np.array([-1.0, 1.0]), rot_dim // 2)
    return jnp.asarray(c), jnp.asarray(s)


L0_Q0 = 2 * A_WIDTH
L0_K0 = L0_Q0 + B_HEADS * LANES
L0_V0 = L0_K0 + B_KV * B_HD
L0_IN = L0_V0 + B_KV * B_HD


def _l0_kernel(*refs, latent):
    if latent:
        (sink_ref, x_ref, gn_ref, mod_ref, win_ref, gvn_ref, ws_ref, bsb_ref, gq_ref, gk_ref, wo_ref,
         cos_ref, sin_ref, kc_ref, vc_ref, xo_ref, zs, cat, qs, ks, vt, kcb, vct) = refs
        key_off = WINDOW
    else:
        (sink_ref, x_ref, gn_ref, mod_ref, win_ref, gvn_ref, ws_ref, bsb_ref, gq_ref, gk_ref, wo_ref,
         xo_ref, ko_ref, vo_ref, zs, cat, qs, ks, vt, kf, vf) = refs
        key_off = 0
    n = BLOCK_ROWS
    n_chunks = n // CHUNK
    low = lax.broadcasted_iota(jnp.int32, (CHUNK, LANES), 1) < B_HD

    if latent:
        zpad = jnp.zeros((WINDOW, LANES), bf16)
        for c0 in (0, 1 + n_chunks):
            ks[c0 * CHUNK:(c0 + 1) * CHUNK, :] = zpad
            vt[c0] = zpad
        kcb[...] = kc_ref[...].astype(bf16)
        for i in range(PAST // CHUNK):
            vct[i] = vc_ref[i * CHUNK:(i + 1) * CHUNK, :].T.astype(bf16)

    def project(c, carry):
        r = pl.ds(pl.multiple_of(c * PROJ_ROWS, PROJ_ROWS), PROJ_ROWS)
        h = _rms_rows(x_ref[r, :], gn_ref[...]) * (1.0 + _mod(mod_ref, SCALE1)) + _mod(mod_ref, SHIFT1)
        zs[r, :] = jnp.dot(h.astype(bf16), win_ref[...], preferred_element_type=f32)
        return carry

    lax.fori_loop(0, n // PROJ_ROWS, project, 0)

    def prepare(c, carry):
        r = pl.ds(pl.multiple_of(c * CHUNK, CHUNK), CHUNK)
        u = jax.nn.gelu(zs[r, 0:A_WIDTH])
        v = jax.nn.gelu(zs[r, A_WIDTH:2 * A_WIDTH])
        mu = jnp.mean(v, -1, keepdims=True)
        var = jnp.mean(jnp.square(v - mu), -1, keepdims=True)
        vn = ((v - mu) * lax.rsqrt(var + EPS) * gvn_ref[...]).astype(bf16)
        for g in range(A_GROUPS):
            cs = slice(g * CHUNK, (g + 1) * CHUNK)
            mixed = jnp.dot(ws_ref[g], vn[:, cs], preferred_element_type=f32) + bsb_ref[g]
            cat[r, cs] = (u[:, cs] * mixed).astype(bf16)
        if latent:
            cs_, sn_ = cos_ref[r, :], sin_ref[r, :]
        for h in range(B_HEADS):
            hs = slice(h * LANES, (h + 1) * LANES)
            qh = zs[r, L0_Q0 + h * LANES:L0_Q0 + (h + 1) * LANES]
            qh = qh * lax.rsqrt(jnp.sum(qh * qh, -1, keepdims=True) * (1.0 / B_HD) + EPS) * gq_ref[:, hs]
            if latent:
                qh = qh * cs_ + _swap_pairs(qh) * sn_
            qs[r, hs] = qh.astype(bf16)
        k = zs[r, L0_K0:L0_K0 + LANES]
        k2 = k * k
        s0 = jnp.sum(jnp.where(low, k2, 0.0), -1, keepdims=True)
        s1 = jnp.sum(jnp.where(low, 0.0, k2), -1, keepdims=True)
        k = k * lax.rsqrt(jnp.where(low, s0, s1) * (1.0 / B_HD) + EPS) * gk_ref[...]
        vv = zs[r, L0_V0:L0_V0 + LANES]
        if latent:
            k = k * cs_ + _swap_pairs(k) * sn_
        else:
            kf[r, :] = k
            vf[r, :] = vv
        kr = pl.ds(pl.multiple_of(c * CHUNK + key_off, CHUNK), CHUNK)
        ks[kr, :] = k.astype(bf16)
        vt[c + key_off // CHUNK] = vv.T.astype(bf16)
        return carry

    lax.fori_loop(0, n_chunks, prepare, 0)

    def attend(r, rows, key_sets):
        q = jnp.concatenate([qs[r, h * LANES:(h + 1) * LANES] for h in range(B_HEADS)], axis=0)
        sk = jnp.concatenate([jnp.full((1, rows), sink_ref[h], f32) for h in range(B_HEADS)], axis=1)
        scores = []
        m = sk
        for k, _, keep in key_sets:
            s = lax.dot_general(k, q, (((1,), (1,)), ((), ())), preferred_element_type=f32)
            if keep is not None:
                s = jnp.where(keep, s, NEG_INF)
            scores.append(s)
            m = jnp.maximum(m, jnp.max(s, 0, keepdims=True))
        den = jnp.exp(sk - m)
        ot = None
        for s, (_, vts, _) in zip(scores, key_sets):
            e = jnp.exp(s - m)
            den = den + jnp.sum(e, 0, keepdims=True)
            eb = e.astype(bf16)
            for i, v_t in enumerate(vts):
                pv = jnp.dot(v_t, eb[i * CHUNK:(i + 1) * CHUNK, :], preferred_element_type=f32)
                ot = pv if ot is None else ot + pv
        ot = ot * (1.0 / den)
        for pair in range(B_HEADS // 2):
            f0 = (2 * pair // B_GROUP) * B_HD
            pair_t = jnp.concatenate([ot[f0:f0 + B_HD, 2 * pair * rows:(2 * pair + 1) * rows],
                                      ot[f0:f0 + B_HD, (2 * pair + 1) * rows:(2 * pair + 2) * rows]], axis=0)
            cat[r, A_WIDTH + pair * LANES:A_WIDTH + (pair + 1) * LANES] = pair_t.T.astype(bf16)

    if latent:
        span = CHUNK + 2 * WINDOW

        def attend_block(c, carry):
            start = pl.multiple_of(c * CHUNK, CHUNK)
            kr = pl.ds(start, span)
            kj = lax.broadcasted_iota(jnp.int32, (span, B_HEADS * CHUNK), 0)
            qi = lax.broadcasted_iota(jnp.int32, (span, B_HEADS * CHUNK), 1) & (CHUNK - 1)
            kpos = start - WINDOW + kj
            keep = (jnp.abs(kj - WINDOW - qi) <= WINDOW) & (kpos >= 0) & (kpos < n)
            attend(pl.ds(start, CHUNK), CHUNK,
                   [(ks[kr, :], [vt[c + i] for i in range(span // CHUNK)], keep),
                    (kcb[...], [vct[i] for i in range(PAST // CHUNK)], None)])
            return carry

        lax.fori_loop(0, n_chunks, attend_block, 0)
    else:
        def attend_seq(sq, carry):
            r = pl.ds(pl.multiple_of(sq * SEQ, SEQ), SEQ)
            attend(r, SEQ, [(ks[r, :], [vt[sq * (SEQ // CHUNK) + i] for i in range(SEQ // CHUNK)], None)])
            ko_ref[sq] = kf[r, :].T
            vo_ref[sq] = vf[r, :].T
            return carry

        lax.fori_loop(0, n // SEQ, attend_seq, 0)

    def output(c, carry):
        r = pl.ds(pl.multiple_of(c * PROJ_ROWS, PROJ_ROWS), PROJ_ROWS)
        y = jnp.dot(cat[r, :], wo_ref[...], preferred_element_type=f32)
        xo_ref[r, :] = x_ref[r, :] + _mod(mod_ref, GATE1) * y
        return carry

    lax.fori_loop(0, n // PROJ_ROWS, output, 0)


def _l0_mixer(xc, xl, g_norm, mods, p, cache_k, cache_v):
    w = p['w_in']
    q = w[:, 2 * A_WIDTH:2 * A_WIDTH + B_HEADS * B_HD].reshape(D, B_HEADS, B_HD)
    zero = jnp.zeros((D, B_GROUP, B_HD), f32)
    q_slots = jnp.concatenate([jnp.concatenate([q[:, :B_GROUP], zero], axis=2),
                               jnp.concatenate([zero, q[:, B_GROUP:]], axis=2)], axis=1).reshape(D, B_HEADS * LANES)
    win = jnp.concatenate([w[:, :2 * A_WIDTH], q_slots, w[:, 2 * A_WIDTH + B_HEADS * B_HD:]], axis=1).astype(bf16)
    zg = jnp.zeros((B_HD,), f32)
    gq = jnp.concatenate([jnp.tile(jnp.concatenate([p['g_q'], zg]), B_GROUP),
                          jnp.tile(jnp.concatenate([zg, p['g_q']]), B_GROUP)]).reshape(1, B_HEADS * LANES) * B_SCALE
    gk = jnp.tile(p['g_k'], B_KV).reshape(1, LANES)
    gvn = p['g_vnorm'].reshape(1, A_WIDTH)
    ws = p['w_s'].astype(bf16)
    bsb = jnp.broadcast_to(p['b_s'][:, :, None], (A_GROUPS, CHUNK, CHUNK))
    wo = p['w_o'].astype(bf16)
    weights = (g_norm.reshape(1, D),)
    consts = (win, gvn, ws, bsb, gq, gk, wo)
    c_specs = [_const_spec(a.shape) for a in consts]
    smem = pl.BlockSpec(memory_space=pltpu.SMEM)
    row = pl.BlockSpec((BLOCK_ROWS, D), lambda b: (b, 0))
    kv = pl.BlockSpec((BLOCK_ROWS // SEQ, LANES, SEQ), lambda b: (b, 0, 0))

    def scratch(pad):
        return [pltpu.VMEM((BLOCK_ROWS, L0_IN), f32), pltpu.VMEM((BLOCK_ROWS, D), bf16),
                pltpu.VMEM((BLOCK_ROWS, B_HEADS * LANES), bf16), pltpu.VMEM((BLOCK_ROWS + pad, LANES), bf16),
                pltpu.VMEM(((BLOCK_ROWS + pad) // CHUNK, LANES, CHUNK), bf16)]

    kv_shape = jax.ShapeDtypeStruct((BATCH, LANES, SEQ), f32)
    state = pltpu.VMEM((BLOCK_ROWS, LANES), f32)
    xo_ctx, k_t, v_t = pl.pallas_call(
        functools.partial(_l0_kernel, latent=False),
        grid=(T_CTX // BLOCK_ROWS,),
        in_specs=[smem, row, _const_spec((1, D)), pl.BlockSpec((MOD_ROWS, D), lambda b: (0, 0))] + c_specs,
        out_specs=[row, kv, kv],
        out_shape=[jax.ShapeDtypeStruct((T_CTX, D), f32), kv_shape, kv_shape],
        scratch_shapes=scratch(0) + [state, state],
        compiler_params=_cp(("parallel",)), name="l0_mixer_ctx",
    )(p['sink'], xc, *weights, mods, *consts)
    k_new = k_t.reshape(BATCH, B_KV, B_HD, SEQ).transpose(0, 3, 1, 2)
    v_new = v_t.reshape(BATCH, B_KV, B_HD, SEQ).transpose(0, 3, 1, 2)

    cos, sin = _rope_tables(DEC_SEQ, B_HD, LANES, 0, copies=LANES // B_HD)
    cache = pl.BlockSpec((None, PAST, LANES), lambda b: (b, 0, 0))
    past = [pltpu.VMEM((PAST, LANES), bf16), pltpu.VMEM((PAST // CHUNK, LANES, CHUNK), bf16)]
    xo_lat = pl.pallas_call(
        functools.partial(_l0_kernel, latent=True),
        grid=(DEC_BATCH,),
        in_specs=[smem, row, _const_spec((1, D)), pl.BlockSpec((MOD_ROWS, D), lambda b: (1 + b, 0))] + c_specs + [
                  _const_spec(cos.shape), _const_spec(sin.shape), cache, cache],
        out_specs=row,
        out_shape=jax.ShapeDtypeStruct((T_LAT, D), f32),
        scratch_shapes=scratch(2 * WINDOW) + past,
        compiler_params=_cp(("parallel",)), name="l0_mixer_lat",
    )(p['sink'], xl, *weights, mods, *consts, cos, sin,
      cache_k.reshape(DEC_BATCH, PAST, LANES), cache_v.reshape(DEC_BATCH, PAST, LANES))
    return xo_ctx, xo_lat, k_new, v_new


C_SLOTS = C_HEADS * SLOT
C_PAIRS = C_HEADS // 2
L1_ROWS = 256


def _l1_kernel(*refs, latent):
    if latent:
        (x_ref, gn_ref, mod_ref, win_ref, gqa_ref, wuq_ref, gq_ref, gkva_ref, wuk_ref, wuvt_ref, gk_ref,
         wo_ref, wuqs_ref, qcos_ref, qsin_ref, kcos_ref, ksin_ref, cckv_ref, ckpe_ref, xo_ref,
         zs, cat, qs, ks, vt, wide, wide2) = refs
        n_ctx = PAST
    else:
        (x_ref, gn_ref, mod_ref, win_ref, gqa_ref, wuq_ref, gq_ref, gkva_ref, wuk_ref, wuvt_ref, gk_ref,
         wo_ref, xo_ref, ckvo_ref, kpeo_ref, zs, cat, qs, ks, vt, wide) = refs
        n_ctx = 0
    n = BLOCK_ROWS
    nt_dims = (((1,), (1,)), ((), ()))

    def inv_rms(v):
        return lax.rsqrt(jnp.sum(v * v, -1, keepdims=True) * (1.0 / C_QK) + EPS)

    def expand_keys(ckv_n, kslot, kb, rope_rows):
        cb = ckv_n.astype(bf16)
        key_rows = pl.ds(pl.multiple_of(kb * L1_ROWS, L1_ROWS), L1_ROWS)
        wide[...] = jnp.dot(cb, wuk_ref[...], preferred_element_type=f32)
        if rope_rows is not None:
            kcos = kcos_ref[rope_rows, :]
            turned = _swap_pairs(kslot) * ksin_ref[rope_rows, :]
        for h in range(C_HEADS):
            kh = wide[:, h * SLOT:(h + 1) * SLOT] + kslot
            if rope_rows is not None:
                kh = inv_rms(kh) * (kh * kcos + turned)
            else:
                kh = kh * inv_rms(kh) * gk_ref[...]
            ks[h, key_rows, :] = kh.astype(bf16)
        v_t = lax.dot_general(wuvt_ref[...], cb, nt_dims, preferred_element_type=f32).astype(bf16)
        for pair in range(C_PAIRS):
            vt[pair, kb] = v_t[pair * LANES:(pair + 1) * LANES, :]

    if latent:
        def past_keys(c, carry):
            r = pl.ds(pl.multiple_of(c * L1_ROWS, L1_ROWS), L1_ROWS)
            expand_keys(cckv_ref[r, :], ckpe_ref[r, :], c, None)
            return carry

        lax.fori_loop(0, PAST // L1_ROWS, past_keys, 0)

    def project(c, carry):
        r = pl.ds(pl.multiple_of(c * PROJ_ROWS, PROJ_ROWS), PROJ_ROWS)
        h = _rms_rows(x_ref[r, :], gn_ref[...]) * (1.0 + _mod(mod_ref, SCALE1)) + _mod(mod_ref, SHIFT1)
        zs[r, :] = jnp.dot(h.astype(bf16), win_ref[...], preferred_element_type=f32)
        return carry

    lax.fori_loop(0, n // PROJ_ROWS, project, 0)

    def prepare(c, carry):
        r = pl.ds(pl.multiple_of(c * L1_ROWS, L1_ROWS), L1_ROWS)
        qa = _rms_rows(zs[r, 0:C_Q_LORA], gqa_ref[...]).astype(bf16)
        wide[...] = jnp.dot(qa, wuq_ref[...], preferred_element_type=f32)
        if latent:
            wide2[...] = jnp.dot(qa, wuqs_ref[...], preferred_element_type=f32)
            qcos, qsin = qcos_ref[r, :], qsin_ref[r, :]
        for h in range(C_HEADS):
            hs = slice(h * SLOT, (h + 1) * SLOT)
            qh = wide[:, hs]
            if latent:
                qh = inv_rms(qh) * (qh * qcos + wide2[:, hs] * qsin)
            else:
                qh = qh * inv_rms(qh) * gq_ref[...]
            qs[h, r, :] = qh.astype(bf16)
        ckv_n = _rms_rows(zs[r, C_Q_LORA:C_Q_LORA + C_KV_LORA], gkva_ref[...])
        kslot = zs[r, C_Q_LORA + C_KV_LORA:ODD_IN_PAD]
        if not latent:
            ckvo_ref[r, :] = ckv_n
            kpeo_ref[c] = kslot.T[C_NOPE:C_QK, :]
        expand_keys(ckv_n, kslot, c + n_ctx // L1_ROWS, r if latent else None)
        return carry

    lax.fori_loop(0, n // L1_ROWS, prepare, 0)

    low = lax.broadcasted_iota(jnp.int32, (2 * C_V, L1_ROWS), 0) < C_V
    n_kblocks = (n_ctx + n) // L1_ROWS
    pairs_per_step = 2 if latent else 4

    def attend(c, carry):
        r = pl.ds(pl.multiple_of(c * L1_ROWS, L1_ROWS), L1_ROWS)

        def values_t(pair, eb):
            if not latent:
                return jnp.dot(vt[pair, c], eb, preferred_element_type=f32)
            o_t = None
            for b in range(n_kblocks):
                pv = jnp.dot(vt[pair, b], eb[b * L1_ROWS:(b + 1) * L1_ROWS, :], preferred_element_type=f32)
                o_t = pv if o_t is None else o_t + pv
            return o_t

        def pairs_step(i, carry2):
            pairs = [i * pairs_per_step + j for j in range(pairs_per_step)]
            heads = [2 * p + hh for p in pairs for hh in range(2)]
            scores = [lax.dot_general(ks[h] if latent else ks[h, r, :], qs[h, r, :], nt_dims,
                                      preferred_element_type=f32) for h in heads]
            exps = [jnp.exp(s - jnp.max(s, 0, keepdims=True)) for s in scores]
            dens = [jnp.sum(e, 0, keepdims=True) for e in exps]
            outs = [values_t(h // 2, e.astype(bf16)) / den for h, e, den in zip(heads, exps, dens)]
            for j, pair in enumerate(pairs):
                cat[pair, r, :] = jnp.where(low, outs[2 * j], outs[2 * j + 1]).T.astype(bf16)
            return carry2

        return lax.fori_loop(0, C_PAIRS // pairs_per_step, pairs_step, carry)

    lax.fori_loop(0, n // L1_ROWS, attend, 0)

    def output(c, carry):
        r = pl.ds(pl.multiple_of(c * PROJ_ROWS, PROJ_ROWS), PROJ_ROWS)
        heads = jnp.concatenate([cat[pair, r, :] for pair in range(C_PAIRS)], axis=1)
        y = jnp.dot(heads, wo_ref[...], preferred_element_type=f32)
        xo_ref[r, :] = x_ref[r, :] + _mod(mod_ref, GATE1) * y
        return carry

    lax.fori_loop(0, n // PROJ_ROWS, output, 0)


def _slot_cols(w, heads, width, lo, hi, lane0):
    k = w.shape[0]
    w3 = w.reshape(k, heads, width)[:, :, lo:hi]
    out = jnp.zeros((k, heads, SLOT), w.dtype).at[:, :, lane0:lane0 + (hi - lo)].set(w3)
    return out.reshape(k, heads * SLOT)


def _l1_mixer(xc, xl, g_norm, mods, p, cache_ckv, cache_kpe):
    w_in = jnp.zeros((D, ODD_IN_PAD), f32).at[:, :C_Q_LORA + C_KV_LORA].set(
        p['w_in'][:, :C_Q_LORA + C_KV_LORA]).at[
        :, C_Q_LORA + C_KV_LORA + C_NOPE:C_Q_LORA + C_KV_LORA + C_QK].set(p['w_in'][:, C_Q_LORA + C_KV_LORA:])
    wuq = _slot_cols(p['w_uq'], C_HEADS, C_QK, 0, C_QK, 0).astype(bf16)
    wuk = _slot_cols(p['w_ukv'], C_HEADS, C_NOPE + C_V, 0, C_NOPE, 0).astype(bf16)
    wuv_t = p['w_ukv'].reshape(C_KV_LORA, C_HEADS, C_NOPE + C_V)[:, :, C_NOPE:].reshape(
        C_KV_LORA, C_HEADS * C_V).T.astype(bf16)
    gq = jnp.zeros((1, SLOT), f32).at[0, :C_QK].set(p['g_q'] * C_SCALE)
    gk = jnp.zeros((1, SLOT), f32).at[0, :C_QK].set(p['g_k'])
    consts = (g_norm.reshape(1, D), w_in.astype(bf16), p['g_qa'].reshape(1, C_Q_LORA), wuq, gq,
              p['g_kva'].reshape(1, C_KV_LORA), wuk, wuv_t, gk, p['w_o'].astype(bf16))
    c_specs = [_const_spec(a.shape) for a in consts]
    row = pl.BlockSpec((BLOCK_ROWS, D), lambda b: (b, 0))
    n_ctx_blocks = T_CTX // BLOCK_ROWS

    def scratch(n_keys):
        return [pltpu.VMEM((BLOCK_ROWS, ODD_IN_PAD), f32), pltpu.VMEM((C_PAIRS, BLOCK_ROWS, LANES), bf16),
                pltpu.VMEM((C_HEADS, BLOCK_ROWS, SLOT), bf16), pltpu.VMEM((C_HEADS, n_keys, SLOT), bf16),
                pltpu.VMEM((C_PAIRS, n_keys // L1_ROWS, LANES, L1_ROWS), bf16),
                pltpu.VMEM((L1_ROWS, C_SLOTS), f32)]

    xo_ctx, ckv_new, kpe_t = pl.pallas_call(
        functools.partial(_l1_kernel, latent=False),
        grid=(n_ctx_blocks,),
        in_specs=[row, c_specs[0], pl.BlockSpec((MOD_ROWS, D), lambda b: (0, 0))] + c_specs[1:],
        out_specs=[row, pl.BlockSpec((BLOCK_ROWS, C_KV_LORA), lambda b: (b, 0)),
                   pl.BlockSpec((BLOCK_ROWS // SEQ, C_ROPE, SEQ), lambda b: (b, 0, 0))],
        out_shape=[jax.ShapeDtypeStruct((T_CTX, D), f32), jax.ShapeDtypeStruct((T_CTX, C_KV_LORA), f32),
                   jax.ShapeDtypeStruct((BATCH, C_ROPE, SEQ), f32)],
        scratch_shapes=scratch(BLOCK_ROWS),
        compiler_params=_cp(("parallel",)), name="l1_mixer_ctx",
    )(xc, consts[0], mods, *consts[1:])

    cos, sin = _rope_tables(DEC_SEQ, C_ROPE, SLOT, C_NOPE)
    def pair_swap(a):
        pairs = a.reshape(a.shape[:-1] + (a.shape[-1] // 2, 2))
        return jnp.stack([pairs[..., 1], pairs[..., 0]], axis=-1).reshape(a.shape)

    w_rope = p['w_uq'].reshape(C_Q_LORA, C_HEADS, C_QK)[:, :, C_NOPE:]
    wuq_swapped = jnp.zeros((C_Q_LORA, C_HEADS, SLOT), f32).at[:, :, C_NOPE:C_QK].set(pair_swap(w_rope)).reshape(
        C_Q_LORA, C_SLOTS).astype(bf16)
    rope = (wuq_swapped, gq * cos, pair_swap(gq) * sin, gk * cos, pair_swap(gk) * sin)
    ckpe = jnp.zeros((DEC_BATCH, PAST, SLOT), f32).at[:, :, C_NOPE:C_QK].set(cache_kpe)
    xo_lat = pl.pallas_call(
        functools.partial(_l1_kernel, latent=True),
        grid=(DEC_BATCH,),
        in_specs=[pl.BlockSpec((BLOCK_ROWS, D), lambda b: (b, 0), pipeline_mode=pl.Buffered(1)), c_specs[0],
                  pl.BlockSpec((MOD_ROWS, D), lambda b: (1 + b, 0))] + c_specs[1:] + [_const_spec(a.shape) for a in rope] + [
                  pl.BlockSpec((None, PAST, C_KV_LORA), lambda b: (b, 0, 0)),
                  pl.BlockSpec((None, PAST, SLOT), lambda b: (b, 0, 0))],
        out_specs=row,
        out_shape=jax.ShapeDtypeStruct((T_LAT, D), f32),
        scratch_shapes=scratch(PAST + BLOCK_ROWS) + [pltpu.VMEM((L1_ROWS, C_SLOTS), f32)],
        compiler_params=_cp(("parallel",)), name="l1_mixer_lat",
    )(xl, consts[0], mods, *consts[1:], *rope, cache_ckv, ckpe)
    return xo_ctx, xo_lat, ckv_new, kpe_t.transpose(0, 2, 1)


ROUTER_ROWS = 40
ROUTE_ROWS = 8


def _router_kernel(xc_ref, xl_ref, gn_ref, mod_ref, whi_ref, wlo_ref, br_ref, h_ref, route_ref):
    h = (_rms_rows(_token_rows(xc_ref, xl_ref), gn_ref[...]) * (1.0 + _mod(mod_ref, SCALE2))
         + _mod(mod_ref, SHIFT2))
    _store_token_major(h_ref, h)
    h_hi, h_lo = _split_bf16(h)
    nt = (((1,), (1,)), ((), ()))
    logits = (lax.dot_general(whi_ref[...], h_hi, nt, preferred_element_type=f32)
              + lax.dot_general(whi_ref[...], h_lo, nt, preferred_element_type=f32)
              + lax.dot_general(wlo_ref[...], h_hi, nt, preferred_element_type=f32))
    logits = logits[0:ROUTER_ROWS, :] + br_ref[0:ROUTER_ROWS, :]
    row_i = lax.broadcasted_iota(jnp.int32, logits.shape, 0)
    row = row_i.astype(f32)
    big = 1e6
    is_g = (row_i >= N_EXPERTS) & (row_i < N_EXPERTS + N_GROUPS)
    lg = jnp.where(is_g, logits, -jnp.inf)
    mg = jnp.max(lg, 0, keepdims=True)
    gsel = jnp.min(jnp.where(lg == mg, row, big), 0, keepdims=True) - N_EXPERTS
    pg_sel = 1.0 / jnp.sum(jnp.where(is_g, jnp.exp(lg - mg), 0.0), 0, keepdims=True)
    in_grp = (row_i < N_EXPERTS) & ((row_i >> 3).astype(f32) == gsel)
    le = jnp.where(in_grp, logits, -jnp.inf)
    m1 = jnp.max(le, 0, keepdims=True)
    i1 = jnp.min(jnp.where(le == m1, row, big), 0, keepdims=True)
    le2 = jnp.where(row == i1, -jnp.inf, le)
    m2 = jnp.max(le2, 0, keepdims=True)
    i2 = jnp.min(jnp.where(le2 == m2, row, big), 0, keepdims=True)
    e2 = jnp.exp(m2 - m1)
    w1 = pg_sel / (1.0 + e2)
    w2 = pg_sel * e2 / (1.0 + e2)
    sub = lax.broadcasted_iota(jnp.int32, route_ref.shape, 0)
    route_ref[...] = jnp.where(sub == 0, i1, jnp.where(sub == 1, i2, jnp.where(sub == 2, w1,
                                                                                jnp.where(sub == 3, w2, 0.0))))


def _router(xc, xl, g_norm, mods, p):
    wr = jnp.zeros((LANES, D), f32).at[:N_EXPERTS].set(p['w_re'].T).at[
        N_EXPERTS:N_EXPERTS + N_GROUPS].set(p['w_rg'].T)
    w_hi, w_lo = _split_bf16(wr)
    br = jnp.zeros((LANES, 1), f32).at[:N_EXPERTS, 0].set(p['b_re']).at[
        N_EXPERTS:N_EXPERTS + N_GROUPS, 0].set(p['b_rg'])
    return pl.pallas_call(
        _router_kernel,
        grid=(T // ROW_TILE,),
        in_specs=_token_specs(D) + [
                  _const_spec((1, D)),
                  pl.BlockSpec((MOD_ROWS, D), lambda i: (_sample_of_tile(i, ROW_TILE), 0)),
                  _const_spec((LANES, D)), _const_spec((LANES, D)), _const_spec((LANES, 1))],
        out_specs=[_token_major_spec(ROW_TILE, lambda i: (i, 0)),
                   pl.BlockSpec((ROUTE_ROWS, ROW_TILE), lambda i: (0, i))],
        out_shape=[jax.ShapeDtypeStruct((T * SUBS, LANES), f32), jax.ShapeDtypeStruct((ROUTE_ROWS, T), f32)],
        compiler_params=_cp(("parallel",)), name="router",
    )(xc, xl, g_norm.reshape(1, D), mods, w_hi, w_lo, br)


PLAN_FIRST_TILE, PLAN_TILES, PLAN_N_USED = 0, 1, 2


def _plan_kernel(rt_ref, pos_ref, plan_ref, rank):
    n_blk = T // 128
    e_col = lax.broadcasted_iota(jnp.int32, (N_EXPERTS, 128), 0).astype(f32)
    ri = lax.broadcasted_iota(jnp.int32, (128, 128), 0)
    ci = lax.broadcasted_iota(jnp.int32, (128, 128), 1)
    before = jnp.where(ri < ci, 1.0, 0.0).astype(bf16)

    def picks(b):
        cs = slice(b * 128, (b + 1) * 128)
        return rt_ref[0:1, cs] == e_col, rt_ref[1:2, cs] == e_col

    counts = jnp.zeros((N_EXPERTS, 1), f32)
    for b in range(n_blk):
        m0, m1 = picks(b)
        m = jnp.where(m0, 1.0, 0.0) + jnp.where(m1, 1.0, 0.0)
        rank[:, b * 128:(b + 1) * 128] = jnp.dot(m.astype(bf16), before, preferred_element_type=f32) + counts
        counts = counts + jnp.sum(m, axis=1, keepdims=True)

    tiles = jnp.floor((counts + (MOE_TILE - 1.0)) * (1.0 / MOE_TILE))
    er = lax.broadcasted_iota(jnp.int32, (N_EXPERTS, N_EXPERTS), 0)
    ec = lax.broadcasted_iota(jnp.int32, (N_EXPERTS, N_EXPERTS), 1)
    earlier = jnp.where(ec < er, 1.0, 0.0).astype(bf16)
    tile_start = jnp.dot(earlier, jnp.broadcast_to(tiles, (N_EXPERTS, 128)).astype(bf16),
                         preferred_element_type=f32)
    row_start = tile_start * MOE_TILE

    sub = lax.broadcasted_iota(jnp.int32, (8, 128), 0)
    for b in range(n_blk):
        m0, m1 = picks(b)
        base = rank[:, b * 128:(b + 1) * 128] + row_start
        p0 = jnp.sum(jnp.where(m0, base, 0.0), axis=0, keepdims=True)
        p1 = jnp.sum(jnp.where(m1, base, 0.0), axis=0, keepdims=True)
        pos_ref[:, b * 128:(b + 1) * 128] = jnp.where(sub == 0, p0, jnp.where(sub == 1, p1, 0.0)).astype(jnp.int32)

    tile_end = tile_start + tiles
    n_used = jnp.max(tile_end, axis=0, keepdims=True)
    diag =(lax.broadcasted_iota(jnp.int32, (N_EXPERTS, 128), 0)
            == lax.broadcasted_iota(jnp.int32, (N_EXPERTS, 128), 1))
    first = jnp.sum(jnp.where(diag, tile_start, 0.0), axis=0, keepdims=True)
    count = jnp.sum(jnp.where(diag, tiles, 0.0), axis=0, keepdims=True)
    rows = jnp.where(sub == PLAN_FIRST_TILE, first,
                     jnp.where(sub == PLAN_TILES, count, jnp.where(sub == PLAN_N_USED, n_used, 0.0)))
    plan_ref[...] = rows.astype(jnp.int32)


def _slot_code(t, k):
    return t * SUBS + k * (SUBS // 2)


def _code_offset(code):
    return pl.multiple_of(code & ~(SUBS - 1), SUBS)


def _code_gate_index(code):
    return code >> 2


PAD_CODE = T * SUBS


def _invert_kernel(pos_ref, plan_ref, code_ref):
    def pad_tile(tile, carry):
        for u in range(MOE_TILE):
            code_ref[tile * MOE_TILE + u] = PAD_CODE
        return carry

    def pad_last_tile(e, carry):
        return pad_tile(jnp.maximum(plan_ref[PLAN_FIRST_TILE, e] + plan_ref[PLAN_TILES, e] - 1, 0), carry)
    lax.fori_loop(0, N_EXPERTS, pad_last_tile, 0)
    lax.fori_loop(plan_ref[PLAN_N_USED, 0], MOE_TILES, pad_tile, 0)

    group = 16
    for k in range(2):
        def place(i, carry):
            t0 = i * group
            slots = [pos_ref[k * T + t0 + u] for u in range(group)]
            for u, s in enumerate(slots):
                code_ref[s] = _slot_code(t0 + u, k)
            return carry
        lax.fori_loop(0, T // group, place, 0)


def _route_plan(route_t):
    pos, plan = pl.pallas_call(
        _plan_kernel,
        out_shape=[jax.ShapeDtypeStruct((8, T), jnp.int32), jax.ShapeDtypeStruct((8, LANES), jnp.int32)],
        scratch_shapes=[pltpu.VMEM((N_EXPERTS, T), f32)],
        compiler_params=_cp(None), name="route_plan",
    )(route_t)
    smem = pl.BlockSpec(memory_space=pltpu.SMEM)
    codes = pl.pallas_call(
        _invert_kernel,
        in_specs=[smem, smem], out_specs=smem,
        out_shape=jax.ShapeDtypeStruct((MOE_ROWS,), jnp.int32),
        name="route_invert",
    )(pos[0:2].reshape(2 * T), plan)
    gates = jnp.pad(route_t[2:4].T.reshape(2 * T), (0, 8))
    return plan[PLAN_FIRST_TILE, :N_EXPERTS], plan[PLAN_TILES, :N_EXPERTS], codes, gates


def _tile_index(i):
    return jnp.minimum(i, MOE_TILES - 1)


TM_ROWS = T * SUBS
SCATTER_GROUP = 8
STAGES = 4
LAST_EXPERT = N_EXPERTS - 1


def _tile_rows(g):
    return pl.ds(pl.multiple_of(g * MOE_TILE, MOE_TILE), MOE_TILE)


def _expert_tile_pairs(first, count, tile_step):
    def pair(pp, carry):
        for parity in range(2):
            g = 2 * pp + parity
            pl.when((g >= first) & (g < first + count))(functools.partial(tile_step, g, parity))
        return carry
    lax.fori_loop(first // 2, (first + count + 1) // 2, pair, 0)


def _gather_tile(code_ref, tile, xs, gbuf):
    base = tile * MOE_TILE
    for r in range(MOE_TILE):
        gbuf[r * SUBS:(r + 1) * SUBS, :] = xs[pl.ds(_code_offset(code_ref[base + r]), SUBS), :]


def _moe_up_kernel(first_ref, count_ref, code_ref, h_hbm, w1_ref, w3_ref, hh_hbm,
                   xs, gbuf_a, gbuf_b, w13, obuf, sem_x, sem_o):
    e = pl.program_id(0)
    first, count = first_ref[e], count_ref[e]
    gbufs = (gbuf_a, gbuf_b)

    def out_copy(slot, g):
        return pltpu.make_async_copy(obuf.at[slot], hh_hbm.at[_tile_rows(g), :], sem_o.at[slot])

    @pl.when(e == 0)
    def _():
        cp = pltpu.make_async_copy(h_hbm, xs.at[pl.ds(0, TM_ROWS), :], sem_x)
        cp.start()
        xs[TM_ROWS:TM_ROWS + SUBS, :] = jnp.zeros((SUBS, LANES), f32)
        cp.wait()
        _gather_tile(code_ref, 0, xs, gbuf_a)

    @pl.when(count > 0)
    def _():
        w13[:, :D_EXPERT] = w1_ref[0].astype(bf16)
        w13[:, D_EXPERT:] = w3_ref[0].astype(bf16)

    def tile_step(g, parity):
        _gather_tile(code_ref, _tile_index(g + 1), xs, gbufs[1 - parity])
        x3 = jnp.swapaxes(gbufs[parity][...].reshape(MOE_TILE, SUBS, LANES), 0, 1)
        x = jnp.concatenate([x3[s] for s in range(SUBS)], axis=1).astype(bf16)
        h13 = jnp.dot(x, w13[...], preferred_element_type=f32)
        hh = (_silu(h13[:, :D_EXPERT]) * h13[:, D_EXPERT:]).astype(bf16)

        slot = g % STAGES

        @pl.when(g >= STAGES)
        def _():
            out_copy(slot, g).wait()
        obuf[slot] = hh
        out_copy(slot, g).start()

    _expert_tile_pairs(first, count, tile_step)

    @pl.when(e == LAST_EXPERT)
    def _():
        n_used = first + count
        for slot in range(STAGES):
            pl.when(n_used > slot)(lambda slot=slot: out_copy(slot, 0).wait())
        obuf[0] = jnp.zeros((MOE_TILE, D_EXPERT), bf16)

        def zero_tile(g, carry):
            cp = out_copy(0, g)
            cp.start()
            cp.wait()
            return carry
        lax.fori_loop(n_used, MOE_TILES, zero_tile, 0)


def _scatter_tile(code_ref, gate_ref, tile, ybuf, acc):
    base = tile * MOE_TILE
    for g0 in range(0, MOE_TILE, SCATTER_GROUP):
        rows = range(g0, g0 + SCATTER_GROUP)
        codes = [code_ref[base + r] for r in rows]
        new = [acc[pl.ds(_code_offset(c), SUBS), :]
               + gate_ref[_code_gate_index(c)] * ybuf[r * SUBS:(r + 1) * SUBS, :]
               for r, c in zip(rows, codes)]
        for c, v in zip(codes, new):
            acc[pl.ds(_code_offset(c), SUBS), :] = v


RES_ROWS = 256


def _residual_out(x_hbm, y_hbm, tok0, sample_of_chunk, mod_ref, acc, rin, rout, sem_r, sem_w):
    n_chunks = x_hbm.shape[0] // RES_ROWS

    def rows(c):
        return pl.ds(pl.multiple_of(c * RES_ROWS, RES_ROWS), RES_ROWS)

    def in_copy(slot, c):
        return pltpu.make_async_copy(x_hbm.at[rows(c), :], rin.at[slot], sem_r.at[slot])

    def out_copy(slot, c):
        return pltpu.make_async_copy(rout.at[slot], y_hbm.at[rows(c), :], sem_w.at[slot])

    for c in range(STAGES - 1):
        in_copy(c, c).start()

    def ring(cc, carry):
        for slot in range(STAGES):
            c = STAGES * cc + slot
            in_copy(slot, c).wait()
            ahead = c + STAGES - 1

            @pl.when(ahead < n_chunks)
            def _():
                in_copy((slot + STAGES - 1) % STAGES, ahead).start()

            @pl.when(c >= STAGES)
            def _():
                out_copy(slot, c).wait()
            delta = _load_token_major(acc, RES_ROWS, tok0 + c * RES_ROWS)
            gate = mod_ref[pl.ds(sample_of_chunk(c) * MOD_ROWS + GATE2, 1), :]
            rout[slot] = rin[slot] + gate * delta
            out_copy(slot, c).start()
        return carry

    lax.fori_loop(0, n_chunks // STAGES, ring, 0)
    for slot in range(STAGES):
        out_copy(slot, 0).wait()


def _moe_down_kernel(first_ref, count_ref, code_ref, gate_ref, hh_hbm, w2_ref, xc_hbm, xl_hbm, mod_ref,
                     yc_hbm, yl_hbm, acc, ybuf_a, ybuf_b, w2b, ibuf, rin, rout, sem_i, sem_r, sem_w):
    e = pl.program_id(0)
    first, count = first_ref[e], count_ref[e]
    n_used = first_ref[LAST_EXPERT] + count_ref[LAST_EXPERT]
    ybufs = (ybuf_a, ybuf_b)

    def in_copy(slot, g):
        return pltpu.make_async_copy(hh_hbm.at[_tile_rows(g), :], ibuf.at[slot], sem_i.at[slot])

    @pl.when(e == 0)
    def _():
        for g in range(STAGES - 1):
            in_copy(g, g).start()

        def zero(c, carry):
            acc[pl.ds(pl.multiple_of(c * 1024, 1024), 1024), :] = jnp.zeros((1024, LANES), f32)
            return carry
        lax.fori_loop(0, TM_ROWS // 1024, zero, 0)
        acc[TM_ROWS:TM_ROWS + SUBS, :] = jnp.zeros((SUBS, LANES), f32)
        ybuf_b[...] = jnp.zeros_like(ybuf_b)

    @pl.when(count > 0)
    def _():
        w2b[...] = w2_ref[0].astype(bf16)

    def tile_step(g, parity):
        slot = g % STAGES
        in_copy(slot, g).wait()
        ahead = g + STAGES - 1

        @pl.when(ahead < n_used)
        def _():
            in_copy(ahead % STAGES, ahead).start()
        _store_token_major(ybufs[parity], jnp.dot(ibuf[slot], w2b[...], preferred_element_type=f32))
        _scatter_tile(code_ref, gate_ref, jnp.maximum(g - 1, 0), ybufs[1 - parity], acc)

    _expert_tile_pairs(first, count, tile_step)

    @pl.when(e == LAST_EXPERT)
    def _():
        for parity in range(2):
            pl.when((n_used > 0) & ((n_used - 1) % 2 == parity))(
                functools.partial(_scatter_tile, code_ref, gate_ref, n_used - 1, ybufs[parity], acc))
        _residual_out(xc_hbm, yc_hbm, 0, lambda c: 0, mod_ref, acc, rin, rout, sem_r, sem_w)
        _residual_out(xl_hbm, yl_hbm, T_CTX, lambda c: 1 + c // (DEC_SEQ // RES_ROWS),
                      mod_ref, acc, rin, rout, sem_r, sem_w)


def _moe(h_tm, route_t, p, xc, xl, mods):
    first_tile, n_tiles, codes, gates = _route_plan(route_t)
    tile_rows = pltpu.VMEM((MOE_TILE * SUBS, LANES), f32)
    staging = pltpu.VMEM((STAGES, MOE_TILE, D_EXPERT), bf16)
    res_rows = pltpu.VMEM((STAGES, RES_ROWS, D), f32)
    hbm = pl.BlockSpec(memory_space=pl.ANY)
    hh = pl.pallas_call(
        _moe_up_kernel,
        grid_spec=pltpu.PrefetchScalarGridSpec(
            num_scalar_prefetch=3, grid=(N_EXPERTS,),
            in_specs=[pl.BlockSpec(memory_space=pl.ANY),
                      pl.BlockSpec((1, D, D_EXPERT), lambda e, f, n, c: (e, 0, 0)),
                      pl.BlockSpec((1, D, D_EXPERT), lambda e, f, n, c: (e, 0, 0))],
            out_specs=pl.BlockSpec(memory_space=pl.ANY),
            scratch_shapes=[pltpu.VMEM((TM_ROWS + SUBS, LANES), f32), tile_rows, tile_rows,
                            pltpu.VMEM((D, 2 * D_EXPERT), bf16), staging,
                            pltpu.SemaphoreType.DMA(()), pltpu.SemaphoreType.DMA((STAGES,))]),
        out_shape=jax.ShapeDtypeStruct((MOE_ROWS, D_EXPERT), bf16),
        compiler_params=_cp(("arbitrary",)), name="moe_up",
    )(first_tile, n_tiles, codes, h_tm, p['w1'], p['w3'])
    return pl.pallas_call(
        _moe_down_kernel,
        grid_spec=pltpu.PrefetchScalarGridSpec(
            num_scalar_prefetch=4, grid=(N_EXPERTS,),
            in_specs=[hbm, pl.BlockSpec((1, D_EXPERT, D), lambda e, f, n, c, g: (e, 0, 0)), hbm, hbm,
                      pl.BlockSpec((N_SAMPLES * MOD_ROWS, D), lambda e, f, n, c, g: (0, 0),
                                   pipeline_mode=pl.Buffered(1))],
            out_specs=[hbm, hbm],
            scratch_shapes=[pltpu.VMEM((TM_ROWS + SUBS, LANES), f32), tile_rows, tile_rows,
                            pltpu.VMEM((D_EXPERT, D), bf16), staging, res_rows, res_rows,
                            pltpu.SemaphoreType.DMA((STAGES,)), pltpu.SemaphoreType.DMA((STAGES,)),
                            pltpu.SemaphoreType.DMA((STAGES,))]),
        out_shape=[jax.ShapeDtypeStruct((T_CTX, D), f32), jax.ShapeDtypeStruct((T_LAT, D), f32)],
        compiler_params=_cp(("arbitrary",)), name="moe_down",
    )(first_tile, n_tiles, codes, gates, hh, p['w2'], xc, xl, mods)


def kernel(x_prompt, x_sample, cache_l0_k, cache_l0_v, cache_l1_ckv, cache_l1_kpe, c, c_ctx, l0_g_norm1, l0_g_norm2, l0_w_ada, l0_b_ada, l0_w_in, l0_g_vnorm, l0_w_s, l0_b_s, l0_g_q, l0_g_k, l0_sink, l0_w_o, l0_w_rg, l0_b_rg, l0_w_re, l0_b_re, l0_w1, l0_w3, l0_w2, l1_g_norm1, l1_g_norm2, l1_w_ada, l1_b_ada, l1_w_in, l1_g_qa, l1_w_uq, l1_g_kva, l1_w_ukv, l1_g_q, l1_g_k, l1_w_o, l1_w_rg, l1_b_rg, l1_w_re, l1_b_re, l1_w1, l1_w3, l1_w2):
    p0 = dict(w_in=l0_w_in, g_vnorm=l0_g_vnorm, w_s=l0_w_s, b_s=l0_b_s, g_q=l0_g_q, g_k=l0_g_k, sink=l0_sink,
              w_o=l0_w_o, w_rg=l0_w_rg, b_rg=l0_b_rg, w_re=l0_w_re, b_re=l0_b_re, w1=l0_w1, w3=l0_w3, w2=l0_w2)
    p1 = dict(w_in=l1_w_in, g_qa=l1_g_qa, w_uq=l1_w_uq, g_kva=l1_g_kva, w_ukv=l1_w_ukv, g_q=l1_g_q, g_k=l1_g_k,
              w_o=l1_w_o, w_rg=l1_w_rg, b_rg=l1_b_rg, w_re=l1_w_re, b_re=l1_b_re, w1=l1_w1, w3=l1_w3, w2=l1_w2)

    cond8 = jnp.zeros((MOD_ROWS, D), f32).at[0].set(c_ctx).at[1:1 + DEC_BATCH].set(c)
    mods0 = _mod_rows(_adaln(cond8, l0_w_ada, l0_b_ada))
    mods1 = _mod_rows(_adaln(cond8, l1_w_ada, l1_b_ada))

    xc0 = x_prompt.reshape(T_CTX, D)
    xl0 = x_sample.reshape(T_LAT, D)

    xc0m, xl0m, k_new, v_new = _l0_mixer(xc0, xl0, l0_g_norm1, mods0, p0, cache_l0_k, cache_l0_v)
    h0, route0 = _router(xc0m, xl0m, l0_g_norm2, mods0, p0)
    xc1, xl1 = _moe(h0, route0, p0, xc0m, xl0m, mods0)

    xc1m, xl1m, ckv_new, kpe_new = _l1_mixer(xc1, xl1, l1_g_norm1, mods1, p1, cache_l1_ckv, cache_l1_kpe)
    h1, route1 = _router(xc1m, xl1m, l1_g_norm2, mods1, p1)
    y_prompt, y_sample = _moe(h1, route1, p1, xc1m, xl1m, mods1)
    return (y_prompt.reshape(BATCH, SEQ, D), y_sample.reshape(DEC_BATCH, DEC_SEQ, D), k_new, v_new,
            ckv_new.reshape(BATCH, SEQ, C_KV_LORA), kpe_new.reshape(BATCH, SEQ, C_ROPE))
```

```python
import functools

import jax
import jax.numpy as jnp
import numpy as np
from jax import lax
from jax.experimental import pallas as pl
from jax.experimental.pallas import tpu as pltpu

f32 = jnp.float32
bf16 = jnp.bfloat16

D = 1024
BATCH, SEQ = 32, 256
DEC_BATCH, DEC_SEQ = 2, 1024
PAST = 512
T_CTX = BATCH * SEQ
T_LAT = DEC_BATCH * DEC_SEQ
T = T_CTX + T_LAT
GRID_W = 64
CHUNK = 128
WINDOW = 128
ROPE_THETA = 10000.0
EPS = 1e-6
NEG_INF = -1e30
LANES = 128
SUBS = D // LANES

A_WIDTH = 512
A_GROUPS = 4
B_HEADS, B_KV, B_GROUP, B_HD = 8, 2, 4, 64
B_SCALE = B_HD ** -0.5

C_HEADS, C_Q_LORA, C_KV_LORA, C_NOPE, C_ROPE, C_V = 16, 384, 256, 64, 32, 64
C_QK = C_NOPE + C_ROPE
C_SCALE = C_QK ** -0.5
ODD_IN_PAD = 768
SLOT = 128

N_GROUPS, N_EXPERTS, D_EXPERT = 4, 32, 256

N_SAMPLES = 1 + DEC_BATCH
MOD_ROWS = 8
SHIFT1, SCALE1, GATE1, SHIFT2, SCALE2, GATE2 = range(6)

ROW_TILE = 1024
BLOCK_ROWS = 1024
PROJ_ROWS = 512
ADALN_COLS = 1536
MOE_TILE = 256
MOE_ROWS = 2 * T + N_EXPERTS * MOE_TILE
MOE_TILES = MOE_ROWS // MOE_TILE
VMEM_CAP = 56 * 1024 * 1024


def _cp(sem, vmem=VMEM_CAP):
    return pltpu.CompilerParams(dimension_semantics=sem, vmem_limit_bytes=vmem)


def _const_spec(shape):
    nd = len(shape)
    return pl.BlockSpec(shape, lambda *_: (0,) * nd, pipeline_mode=pl.Buffered(1))


def _sample_of_tile(i, tile):
    n_ctx = T_CTX // tile
    per_lat = DEC_SEQ // tile
    return jnp.where(i < n_ctx, 0, 1 + (i - n_ctx) // per_lat)


def _mod(mod_ref, row):
    return mod_ref[row:row + 1, :]


def _silu(x):
    return x * jax.nn.sigmoid(x)


def _rms_rows(x, g):
    return x * lax.rsqrt(jnp.mean(x * x, -1, keepdims=True) + EPS) * g


def _swap_pairs(x):
    lane = lax.broadcasted_iota(jnp.int32, x.shape, x.ndim - 1)
    nxt = pltpu.roll(x, x.shape[-1] - 1, x.ndim - 1)
    prv = pltpu.roll(x, 1, x.ndim - 1)
    return jnp.where((lane & 1) == 0, nxt, prv)


def _split_bf16(x):
    hi = x.astype(bf16)
    return hi, (x - hi.astype(f32)).astype(bf16)


def _adaln_kernel(c_ref, w_ref, b_ref, o_ref):
    s_hi, s_lo = _split_bf16(_silu(c_ref[...]))
    w_hi, w_lo = _split_bf16(w_ref[...])
    o_ref[...] = (jnp.dot(s_hi, w_hi, preferred_element_type=f32) + jnp.dot(s_lo, w_hi, preferred_element_type=f32)
                  + jnp.dot(s_hi, w_lo, preferred_element_type=f32) + b_ref[...])


def _adaln(cond8, w, b):
    n = w.shape[1]
    return pl.pallas_call(
        _adaln_kernel,
        grid=(n // ADALN_COLS,),
        in_specs=[_const_spec((MOD_ROWS, D)), pl.BlockSpec((D, ADALN_COLS), lambda j: (0, j)),
                  pl.BlockSpec((1, ADALN_COLS), lambda j: (0, j))],
        out_specs=pl.BlockSpec((MOD_ROWS, ADALN_COLS), lambda j: (0, j)),
        out_shape=jax.ShapeDtypeStruct((MOD_ROWS, n), f32),
        compiler_params=_cp(("arbitrary",)),
        name="adaln",
    )(cond8, w, b.reshape(1, n))


def _mod_rows(m8):
    m = m8[:N_SAMPLES].reshape(N_SAMPLES, 6, D)
    return jnp.pad(m, ((0, 0), (0, MOD_ROWS - 6), (0, 0))).reshape(N_SAMPLES * MOD_ROWS, D)


N_CTX_TILES = T_CTX // ROW_TILE


def _token_specs(width):
    return [pl.BlockSpec((ROW_TILE, width), lambda i: (jnp.minimum(i, N_CTX_TILES - 1), 0)),
            pl.BlockSpec((ROW_TILE, width), lambda i: (jnp.maximum(i - N_CTX_TILES, 0), 0))]


def _token_rows(xc_ref, xl_ref):
    return jnp.where(pl.program_id(0) < N_CTX_TILES, xc_ref[...], xl_ref[...])


def _store_token_major(ref, x):
    n = x.shape[0]
    for s in range(SUBS):
        ref[pl.ds(s, n, stride=SUBS), :] = x[:, s * LANES:(s + 1) * LANES]


def _load_token_major(ref, n, row0=0):
    return jnp.concatenate([ref[pl.ds(row0 * SUBS + s, n, stride=SUBS), :] for s in range(SUBS)], axis=1)


def _token_major_spec(rows, index_map):
    return pl.BlockSpec((rows * SUBS, LANES), index_map)


def _rope_tables(n, rot_dim, lanes, lane0, copies=1):
    rows_count = n // GRID_W
    rows = np.repeat(np.arange(rows_count), GRID_W).astype(np.float64)
    cols = np.tile(np.arange(GRID_W), rows_count).astype(np.float64)
    d_axis = rot_dim // 2
    inv = ROPE_THETA ** (-np.arange(0, d_axis, 2, dtype=np.float64) / d_axis)
    ang = np.concatenate([rows[:, None] * inv, cols[:, None] * inv], -1)
    c = np.ones((n, lanes), np.float32)
    s = np.zeros((n, lanes), np.float32)
    for j in range(copies):
        lo = lane0 + j * rot_dim
        c[:, lo:lo + rot_dim] = np.repeat(np.cos(ang), 2, axis=1)
        s[:, lo:lo + rot_dim] = np.repeat(np.sin(ang), 2, axis=1) * np.tile(np.array([-1.0, 1.0]), rot_dim // 2)
    return jnp.asarray(c), jnp.asarray(s)


L0_Q0 = 2 * A_WIDTH
L0_K0 = L0_Q0 + B_HEADS * LANES
L0_V0 = L0_K0 + B_KV * B_HD
L0_IN = L0_V0 + B_KV * B_HD


def _l0_kernel(*refs, latent):
    if latent:
        (sink_ref, x_ref, gn_ref, mod_ref, win_ref, gvn_ref, ws_ref, bsb_ref, gq_ref, gk_ref, wo_ref,
         cos_ref, sin_ref, kc_ref, vc_ref, xo_ref, zs, cat, qs, ks, vt, kcb, vct) = refs
        key_off = WINDOW
    else:
        (sink_ref, x_ref, gn_ref, mod_ref, win_ref, gvn_ref, ws_ref, bsb_ref, gq_ref, gk_ref, wo_ref,
         xo_ref, ko_ref, vo_ref, zs, cat, qs, ks, vt, kf, vf) = refs
        key_off = 0
    n = BLOCK_ROWS
    n_chunks = n // CHUNK
    low = lax.broadcasted_iota(jnp.int32, (CHUNK, LANES), 1) < B_HD

    if latent:
        zpad = jnp.zeros((WINDOW, LANES), bf16)
        for c0 in (0, 1 + n_chunks):
            ks[c0 * CHUNK:(c0 + 1) * CHUNK, :] = zpad
            vt[c0] = zpad
        kcb[...] = kc_ref[...].astype(bf16)
        for i in range(PAST // CHUNK):
            vct[i] = vc_ref[i * CHUNK:(i + 1) * CHUNK, :].T.astype(bf16)

    def project(c, carry):
        r = pl.ds(pl.multiple_of(c * PROJ_ROWS, PROJ_ROWS), PROJ_ROWS)
        h = _rms_rows(x_ref[r, :], gn_ref[...]) * (1.0 + _mod(mod_ref, SCALE1)) + _mod(mod_ref, SHIFT1)
        zs[r, :] = jnp.dot(h.astype(bf16), win_ref[...], preferred_element_type=f32)
        return carry

    lax.fori_loop(0, n // PROJ_ROWS, project, 0)

    def prepare(c, carry):
        r = pl.ds(pl.multiple_of(c * CHUNK, CHUNK), CHUNK)
        u = jax.nn.gelu(zs[r, 0:A_WIDTH])
        v = jax.nn.gelu(zs[r, A_WIDTH:2 * A_WIDTH])
        mu = jnp.mean(v, -1, keepdims=True)
        var = jnp.mean(jnp.square(v - mu), -1, keepdims=True)
        vn = ((v - mu) * lax.rsqrt(var + EPS) * gvn_ref[...]).astype(bf16)
        for g in range(A_GROUPS):
            cs = slice(g * CHUNK, (g + 1) * CHUNK)
            mixed = jnp.dot(ws_ref[g], vn[:, cs], preferred_element_type=f32) + bsb_ref[g]
            cat[r, cs] = (u[:, cs] * mixed).astype(bf16)
        if latent:
            cs_, sn_ = cos_ref[r, :], sin_ref[r, :]
        for h in range(B_HEADS):
            hs = slice(h * LANES, (h + 1) * LANES)
            qh = zs[r, L0_Q0 + h * LANES:L0_Q0 + (h + 1) * LANES]
            qh = qh * lax.rsqrt(jnp.sum(qh * qh, -1, keepdims=True) * (1.0 / B_HD) + EPS) * gq_ref[:, hs]
            if latent:
                qh = qh * cs_ + _swap_pairs(qh) * sn_
            qs[r, hs] = qh.astype(bf16)
        k = zs[r, L0_K0:L0_K0 + LANES]
        k2 = k * k
        s0 = jnp.sum(jnp.where(low, k2, 0.0), -1, keepdims=True)
        s1 = jnp.sum(jnp.where(low, 0.0, k2), -1, keepdims=True)
        k = k * lax.rsqrt(jnp.where(low, s0, s1) * (1.0 / B_HD) + EPS) * gk_ref[...]
        vv = zs[r, L0_V0:L0_V0 + LANES]
        if latent:
            k = k * cs_ + _swap_pairs(k) * sn_
        else:
            kf[r, :] = k
            vf[r, :] = vv
        kr = pl.ds(pl.multiple_of(c * CHUNK + key_off, CHUNK), CHUNK)
        ks[kr, :] = k.astype(bf16)
        vt[c + key_off // CHUNK] = vv.T.astype(bf16)
        return carry

    lax.fori_loop(0, n_chunks, prepare, 0)

    def attend(r, rows, key_sets):
        q = jnp.concatenate([qs[r, h * LANES:(h + 1) * LANES] for h in range(B_HEADS)], axis=0)
        sk = jnp.concatenate([jnp.full((1, rows), sink_ref[h], f32) for h in range(B_HEADS)], axis=1)
        scores = []
        m = sk
        for k, _, keep in key_sets:
            s = lax.dot_general(k, q, (((1,), (1,)), ((), ())), preferred_element_type=f32)
            if keep is not None:
                s = jnp.where(keep, s, NEG_INF)
            scores.append(s)
            m = jnp.maximum(m, jnp.max(s, 0, keepdims=True))
        den = jnp.exp(sk - m)
        ot = None
        for s, (_, vts, _) in zip(scores, key_sets):
            e = jnp.exp(s - m)
            den = den + jnp.sum(e, 0, keepdims=True)
            eb = e.astype(bf16)
            for i, v_t in enumerate(vts):
                pv = jnp.dot(v_t, eb[i * CHUNK:(i + 1) * CHUNK, :], preferred_element_type=f32)
                ot = pv if ot is None else ot + pv
        ot = ot * (1.0 / den)
        for pair in range(B_HEADS // 2):
            f0 = (2 * pair // B_GROUP) * B_HD
            pair_t = jnp.concatenate([ot[f0:f0 + B_HD, 2 * pair * rows:(2 * pair + 1) * rows],
                                      ot[f0:f0 + B_HD, (2 * pair + 1) * rows:(2 * pair + 2) * rows]], axis=0)
            cat[r, A_WIDTH + pair * LANES:A_WIDTH + (pair + 1) * LANES] = pair_t.T.astype(bf16)

    if latent:
        span = CHUNK + 2 * WINDOW

        def attend_block(c, carry):
            start = pl.multiple_of(c * CHUNK, CHUNK)
            kr = pl.ds(start, span)
            kj = lax.broadcasted_iota(jnp.int32, (span, B_HEADS * CHUNK), 0)
            qi = lax.broadcasted_iota(jnp.int32, (span, B_HEADS * CHUNK), 1) & (CHUNK - 1)
            kpos = start - WINDOW + kj
            keep = (jnp.abs(kj - WINDOW - qi) <= WINDOW) & (kpos >= 0) & (kpos < n)
            attend(pl.ds(start, CHUNK), CHUNK,
                   [(ks[kr, :], [vt[c + i] for i in range(span // CHUNK)], keep),
                    (kcb[...], [vct[i] for i in range(PAST // CHUNK)], None)])
            return carry

        lax.fori_loop(0, n_chunks, attend_block, 0)
    else:
        def attend_seq(sq, carry):
            r = pl.ds(pl.multiple_of(sq * SEQ, SEQ), SEQ)
            attend(r, SEQ, [(ks[r, :], [vt[sq * (SEQ // CHUNK) + i] for i in range(SEQ // CHUNK)], None)])
            ko_ref[sq] = kf[r, :].T
            vo_ref[sq] = vf[r, :].T
            return carry

        lax.fori_loop(0, n // SEQ, attend_seq, 0)

    def output(c, carry):
        r = pl.ds(pl.multiple_of(c * PROJ_ROWS, PROJ_ROWS), PROJ_ROWS)
        y = jnp.dot(cat[r, :], wo_ref[...], preferred_element_type=f32)
        xo_ref[r, :] = x_ref[r, :] + _mod(mod_ref, GATE1) * y
        return carry

    lax.fori_loop(0, n // PROJ_ROWS, output, 0)


def _l0_mixer(xc, xl, g_norm, mods, p, cache_k, cache_v):
    w = p['w_in']
    q = w[:, 2 * A_WIDTH:2 * A_WIDTH + B_HEADS * B_HD].reshape(D, B_HEADS, B_HD)
    zero = jnp.zeros((D, B_GROUP, B_HD), f32)
    q_slots = jnp.concatenate([jnp.concatenate([q[:, :B_GROUP], zero], axis=2),
                               jnp.concatenate([zero, q[:, B_GROUP:]], axis=2)], axis=1).reshape(D, B_HEADS * LANES)
    win = jnp.concatenate([w[:, :2 * A_WIDTH], q_slots, w[:, 2 * A_WIDTH + B_HEADS * B_HD:]], axis=1).astype(bf16)
    zg = jnp.zeros((B_HD,), f32)
    gq = jnp.concatenate([jnp.tile(jnp.concatenate([p['g_q'], zg]), B_GROUP),
                          jnp.tile(jnp.concatenate([zg, p['g_q']]), B_GROUP)]).reshape(1, B_HEADS * LANES) * B_SCALE
    gk = jnp.tile(p['g_k'], B_KV).reshape(1, LANES)
    gvn = p['g_vnorm'].reshape(1, A_WIDTH)
    ws = p['w_s'].astype(bf16)
    bsb = jnp.broadcast_to(p['b_s'][:, :, None], (A_GROUPS, CHUNK, CHUNK))
    wo = p['w_o'].astype(bf16)
    weights = (g_norm.reshape(1, D),)
    consts = (win, gvn, ws, bsb, gq, gk, wo)
    c_specs = [_const_spec(a.shape) for a in consts]
    smem = pl.BlockSpec(memory_space=pltpu.SMEM)
    row = pl.BlockSpec((BLOCK_ROWS, D), lambda b: (b, 0))
    kv = pl.BlockSpec((BLOCK_ROWS // SEQ, LANES, SEQ), lambda b: (b, 0, 0))

    def scratch(pad):
        return [pltpu.VMEM((BLOCK_ROWS, L0_IN), f32), pltpu.VMEM((BLOCK_ROWS, D), bf16),
                pltpu.VMEM((BLOCK_ROWS, B_HEADS * LANES), bf16), pltpu.VMEM((BLOCK_ROWS + pad, LANES), bf16),
                pltpu.VMEM(((BLOCK_ROWS + pad) // CHUNK, LANES, CHUNK), bf16)]

    kv_shape = jax.ShapeDtypeStruct((BATCH, LANES, SEQ), f32)
    state = pltpu.VMEM((BLOCK_ROWS, LANES), f32)
    xo_ctx, k_t, v_t = pl.pallas_call(
        functools.partial(_l0_kernel, latent=False),
        grid=(T_CTX // BLOCK_ROWS,),
        in_specs=[smem, row, _const_spec((1, D)), pl.BlockSpec((MOD_ROWS, D), lambda b: (0, 0))] + c_specs,
        out_specs=[row, kv, kv],
        out_shape=[jax.ShapeDtypeStruct((T_CTX, D), f32), kv_shape, kv_shape],
        scratch_shapes=scratch(0) + [state, state],
        compiler_params=_cp(("parallel",)), name="l0_mixer_ctx",
    )(p['sink'], xc, *weights, mods, *consts)
    k_new = k_t.reshape(BATCH, B_KV, B_HD, SEQ).transpose(0, 3, 1, 2)
    v_new = v_t.reshape(BATCH, B_KV, B_HD, SEQ).transpose(0, 3, 1, 2)

    cos, sin = _rope_tables(DEC_SEQ, B_HD, LANES, 0, copies=LANES // B_HD)
    cache = pl.BlockSpec((None, PAST, LANES), lambda b: (b, 0, 0))
    past = [pltpu.VMEM((PAST, LANES), bf16), pltpu.VMEM((PAST // CHUNK, LANES, CHUNK), bf16)]
    xo_lat = pl.pallas_call(
        functools.partial(_l0_kernel, latent=True),
        grid=(DEC_BATCH,),
        in_specs=[smem, row, _const_spec((1, D)), pl.BlockSpec((MOD_ROWS, D), lambda b: (1 + b, 0))] + c_specs + [
                  _const_spec(cos.shape), _const_spec(sin.shape), cache, cache],
        out_specs=row,
        out_shape=jax.ShapeDtypeStruct((T_LAT, D), f32),
        scratch_shapes=scratch(2 * WINDOW) + past,
        compiler_params=_cp(("parallel",)), name="l0_mixer_lat",
    )(p['sink'], xl, *weights, mods, *consts, cos, sin,
      cache_k.reshape(DEC_BATCH, PAST, LANES), cache_v.reshape(DEC_BATCH, PAST, LANES))
    return xo_ctx, xo_lat, k_new, v_new


C_SLOTS = C_HEADS * SLOT
C_PAIRS = C_HEADS // 2
L1_ROWS = 256


def _l1_kernel(*refs, latent):
    if latent:
        (x_ref, gn_ref, mod_ref, win_ref, gqa_ref, wuq_ref, gq_ref, gkva_ref, wuk_ref, wuvt_ref, gk_ref,
         wo_ref, wuqs_ref, qcos_ref, qsin_ref, kcos_ref, ksin_ref, cckv_ref, ckpe_ref, xo_ref,
         zs, cat, qs, ks, vt, wide, wide2) = refs
        n_ctx = PAST
    else:
        (x_ref, gn_ref, mod_ref, win_ref, gqa_ref, wuq_ref, gq_ref, gkva_ref, wuk_ref, wuvt_ref, gk_ref,
         wo_ref, xo_ref, ckvo_ref, kpeo_ref, zs, cat, qs, ks, vt, wide) = refs
        n_ctx = 0
    n = BLOCK_ROWS
    nt_dims = (((1,), (1,)), ((), ()))

    def inv_rms(v):
        return lax.rsqrt(jnp.sum(v * v, -1, keepdims=True) * (1.0 / C_QK) + EPS)

    def expand_keys(ckv_n, kslot, kb, rope_rows):
        cb = ckv_n.astype(bf16)
        key_rows = pl.ds(pl.multiple_of(kb * L1_ROWS, L1_ROWS), L1_ROWS)
        wide[...] = jnp.dot(cb, wuk_ref[...], preferred_element_type=f32)
        if rope_rows is not None:
            kcos = kcos_ref[rope_rows, :]
            turned = _swap_pairs(kslot) * ksin_ref[rope_rows, :]
        for h in range(C_HEADS):
            kh = wide[:, h * SLOT:(h + 1) * SLOT] + kslot
            if rope_rows is not None:
                kh = inv_rms(kh) * (kh * kcos + turned)
            else:
                kh = kh * inv_rms(kh) * gk_ref[...]
            ks[h, key_rows, :] = kh.astype(bf16)
        v_t = lax.dot_general(wuvt_ref[...], cb, nt_dims, preferred_element_type=f32).astype(bf16)
        for pair in range(C_PAIRS):
            vt[pair, kb] = v_t[pair * LANES:(pair + 1) * LANES, :]

    if latent:
        def past_keys(c, carry):
            r = pl.ds(pl.multiple_of(c * L1_ROWS, L1_ROWS), L1_ROWS)
            expand_keys(cckv_ref[r, :], ckpe_ref[r, :], c, None)
            return carry

        lax.fori_loop(0, PAST // L1_ROWS, past_keys, 0)

    def project(c, carry):
        r = pl.ds(pl.multiple_of(c * PROJ_ROWS, PROJ_ROWS), PROJ_ROWS)
        h = _rms_rows(x_ref[r, :], gn_ref[...]) * (1.0 + _mod(mod_ref, SCALE1)) + _mod(mod_ref, SHIFT1)
        zs[r, :] = jnp.dot(h.astype(bf16), win_ref[...], preferred_element_type=f32)
        return carry

    lax.fori_loop(0, n // PROJ_ROWS, project, 0)

    def prepare(c, carry):
        r = pl.ds(pl.multiple_of(c * L1_ROWS, L1_ROWS), L1_ROWS)
        qa = _rms_rows(zs[r, 0:C_Q_LORA], gqa_ref[...]).astype(bf16)
        wide[...] = jnp.dot(qa, wuq_ref[...], preferred_element_type=f32)
        if latent:
            wide2[...] = jnp.dot(qa, wuqs_ref[...], preferred_element_type=f32)
            qcos, qsin = qcos_ref[r, :], qsin_ref[r, :]
        for h in range(C_HEADS):
            hs = slice(h * SLOT, (h + 1) * SLOT)
            qh = wide[:, hs]
            if latent:
                qh = inv_rms(qh) * (qh * qcos + wide2[:, hs] * qsin)
            else:
                qh = qh * inv_rms(qh) * gq_ref[...]
            qs[h, r, :] = qh.astype(bf16)
        ckv_n = _rms_rows(zs[r, C_Q_LORA:C_Q_LORA + C_KV_LORA], gkva_ref[...])
        kslot = zs[r, C_Q_LORA + C_KV_LORA:ODD_IN_PAD]
        if not latent:
            ckvo_ref[r, :] = ckv_n
            kpeo_ref[c] = kslot.T[C_NOPE:C_QK, :]
        expand_keys(ckv_n, kslot, c + n_ctx // L1_ROWS, r if latent else None)
        return carry

    lax.fori_loop(0, n // L1_ROWS, prepare, 0)

    low = lax.broadcasted_iota(jnp.int32, (2 * C_V, L1_ROWS), 0) < C_V
    n_kblocks = (n_ctx + n) // L1_ROWS
    pairs_per_step = 2 if latent else 4

    def attend(c, carry):
        r = pl.ds(pl.multiple_of(c * L1_ROWS, L1_ROWS), L1_ROWS)

        def values_t(pair, eb):
            if not latent:
                return jnp.dot(vt[pair, c], eb, preferred_element_type=f32)
            o_t = None
            for b in range(n_kblocks):
                pv = jnp.dot(vt[pair, b], eb[b * L1_ROWS:(b + 1) * L1_ROWS, :], preferred_element_type=f32)
                o_t = pv if o_t is None else o_t + pv
            return o_t

        def pairs_step(i, carry2):
            pairs = [i * pairs_per_step + j for j in range(pairs_per_step)]
            heads = [2 * p + hh for p in pairs for hh in range(2)]
            scores = [lax.dot_general(ks[h] if latent else ks[h, r, :], qs[h, r, :], nt_dims,
                                      preferred_element_type=f32) for h in heads]
            exps = [jnp.exp(s - jnp.max(s, 0, keepdims=True)) for s in scores]
            dens = [jnp.sum(e, 0, keepdims=True) for e in exps]
            outs = [values_t(h // 2, e.astype(bf16)) / den for h, e, den in zip(heads, exps, dens)]
            for j, pair in enumerate(pairs):
                cat[pair, r, :] = jnp.where(low, outs[2 * j], outs[2 * j + 1]).T.astype(bf16)
            return carry2

        return lax.fori_loop(0, C_PAIRS // pairs_per_step, pairs_step, carry)

    lax.fori_loop(0, n // L1_ROWS, attend, 0)

    def output(c, carry):
        r = pl.ds(pl.multiple_of(c * PROJ_ROWS, PROJ_ROWS), PROJ_ROWS)
        heads = jnp.concatenate([cat[pair, r, :] for pair in range(C_PAIRS)], axis=1)
        y = jnp.dot(heads, wo_ref[...], preferred_element_type=f32)
        xo_ref[r, :] = x_ref[r, :] + _mod(mod_ref, GATE1) * y
        return carry

    lax.fori_loop(0, n // PROJ_ROWS, output, 0)


def _slot_cols(w, heads, width, lo, hi, lane0):
    k = w.shape[0]
    w3 = w.reshape(k, heads, width)[:, :, lo:hi]
    out = jnp.zeros((k, heads, SLOT), w.dtype).at[:, :, lane0:lane0 + (hi - lo)].set(w3)
    return out.reshape(k, heads * SLOT)


def _l1_mixer(xc, xl, g_norm, mods, p, cache_ckv, cache_kpe):
    w_in = jnp.zeros((D, ODD_IN_PAD), f32).at[:, :C_Q_LORA + C_KV_LORA].set(
        p['w_in'][:, :C_Q_LORA + C_KV_LORA]).at[
        :, C_Q_LORA + C_KV_LORA + C_NOPE:C_Q_LORA + C_KV_LORA + C_QK].set(p['w_in'][:, C_Q_LORA + C_KV_LORA:])
    wuq = _slot_cols(p['w_uq'], C_HEADS, C_QK, 0, C_QK, 0).astype(bf16)
    wuk = _slot_cols(p['w_ukv'], C_HEADS, C_NOPE + C_V, 0, C_NOPE, 0).astype(bf16)
    wuv_t = p['w_ukv'].reshape(C_KV_LORA, C_HEADS, C_NOPE + C_V)[:, :, C_NOPE:].reshape(
        C_KV_LORA, C_HEADS * C_V).T.astype(bf16)
    gq = jnp.zeros((1, SLOT), f32).at[0, :C_QK].set(p['g_q'] * C_SCALE)
    gk = jnp.zeros((1, SLOT), f32).at[0, :C_QK].set(p['g_k'])
    consts = (g_norm.reshape(1, D), w_in.astype(bf16), p['g_qa'].reshape(1, C_Q_LORA), wuq, gq,
              p['g_kva'].reshape(1, C_KV_LORA), wuk, wuv_t, gk, p['w_o'].astype(bf16))
    c_specs = [_const_spec(a.shape) for a in consts]
    row = pl.BlockSpec((BLOCK_ROWS, D), lambda b: (b, 0))
    n_ctx_blocks = T_CTX // BLOCK_ROWS

    def scratch(n_keys):
        return [pltpu.VMEM((BLOCK_ROWS, ODD_IN_PAD), f32), pltpu.VMEM((C_PAIRS, BLOCK_ROWS, LANES), bf16),
                pltpu.VMEM((C_HEADS, BLOCK_ROWS, SLOT), bf16), pltpu.VMEM((C_HEADS, n_keys, SLOT), bf16),
                pltpu.VMEM((C_PAIRS, n_keys // L1_ROWS, LANES, L1_ROWS), bf16),
                pltpu.VMEM((L1_ROWS, C_SLOTS), f32)]

    xo_ctx, ckv_new, kpe_t = pl.pallas_call(
        functools.partial(_l1_kernel, latent=False),
        grid=(n_ctx_blocks,),
        in_specs=[row, c_specs[0], pl.BlockSpec((MOD_ROWS, D), lambda b: (0, 0))] + c_specs[1:],
        out_specs=[row, pl.BlockSpec((BLOCK_ROWS, C_KV_LORA), lambda b: (b, 0)),
                   pl.BlockSpec((BLOCK_ROWS // SEQ, C_ROPE, SEQ), lambda b: (b, 0, 0))],
        out_shape=[jax.ShapeDtypeStruct((T_CTX, D), f32), jax.ShapeDtypeStruct((T_CTX, C_KV_LORA), f32),
                   jax.ShapeDtypeStruct((BATCH, C_ROPE, SEQ), f32)],
        scratch_shapes=scratch(BLOCK_ROWS),
        compiler_params=_cp(("parallel",)), name="l1_mixer_ctx",
    )(xc, consts[0], mods, *consts[1:])

    cos, sin = _rope_tables(DEC_SEQ, C_ROPE, SLOT, C_NOPE)
    def pair_swap(a):
        pairs = a.reshape(a.shape[:-1] + (a.shape[-1] // 2, 2))
        return jnp.stack([pairs[..., 1], pairs[..., 0]], axis=-1).reshape(a.shape)

    w_rope = p['w_uq'].reshape(C_Q_LORA, C_HEADS, C_QK)[:, :, C_NOPE:]
    wuq_swapped = jnp.zeros((C_Q_LORA, C_HEADS, SLOT), f32).at[:, :, C_NOPE:C_QK].set(pair_swap(w_rope)).reshape(
        C_Q_LORA, C_SLOTS).astype(bf16)
    rope = (wuq_swapped, gq * cos, pair_swap(gq) * sin, gk * cos, pair_swap(gk) * sin)
    ckpe = jnp.zeros((DEC_BATCH, PAST, SLOT), f32).at[:, :, C_NOPE:C_QK].set(cache_kpe)
    xo_lat = pl.pallas_call(
        functools.partial(_l1_kernel, latent=True),
        grid=(DEC_BATCH,),
        in_specs=[pl.BlockSpec((BLOCK_ROWS, D), lambda b: (b, 0), pipeline_mode=pl.Buffered(1)), c_specs[0],
                  pl.BlockSpec((MOD_ROWS, D), lambda b: (1 + b, 0))] + c_specs[1:] + [_const_spec(a.shape) for a in rope] + [
                  pl.BlockSpec((None, PAST, C_KV_LORA), lambda b: (b, 0, 0)),
                  pl.BlockSpec((None, PAST, SLOT), lambda b: (b, 0, 0))],
        out_specs=row,
        out_shape=jax.ShapeDtypeStruct((T_LAT, D), f32),
        scratch_shapes=scratch(PAST + BLOCK_ROWS) + [pltpu.VMEM((L1_ROWS, C_SLOTS), f32)],
        compiler_params=_cp(("parallel",)), name="l1_mixer_lat",
    )(xl, consts[0], mods, *consts[1:], *rope, cache_ckv, ckpe)
    return xo_ctx, xo_lat, ckv_new, kpe_t.transpose(0, 2, 1)


ROUTER_ROWS = 40
ROUTE_ROWS = 8


def _router_kernel(xc_ref, xl_ref, gn_ref, mod_ref, whi_ref, wlo_ref, br_ref, h_ref, route_ref):
    h = (_rms_rows(_token_rows(xc_ref, xl_ref), gn_ref[...]) * (1.0 + _mod(mod_ref, SCALE2))
         + _mod(mod_ref, SHIFT2))
    _store_token_major(h_ref, h)
    h_hi, h_lo = _split_bf16(h)
    nt = (((1,), (1,)), ((), ()))
    logits = (lax.dot_general(whi_ref[...], h_hi, nt, preferred_element_type=f32)
              + lax.dot_general(whi_ref[...], h_lo, nt, preferred_element_type=f32)
              + lax.dot_general(wlo_ref[...], h_hi, nt, preferred_element_type=f32))
    logits = logits[0:ROUTER_ROWS, :] + br_ref[0:ROUTER_ROWS, :]
    row_i = lax.broadcasted_iota(jnp.int32, logits.shape, 0)
    row = row_i.astype(f32)
    big = 1e6
    is_g = (row_i >= N_EXPERTS) & (row_i < N_EXPERTS + N_GROUPS)
    lg = jnp.where(is_g, logits, -jnp.inf)
    mg = jnp.max(lg, 0, keepdims=True)
    gsel = jnp.min(jnp.where(lg == mg, row, big), 0, keepdims=True) - N_EXPERTS
    pg_sel = 1.0 / jnp.sum(jnp.where(is_g, jnp.exp(lg - mg), 0.0), 0, keepdims=True)
    in_grp = (row_i < N_EXPERTS) & ((row_i >> 3).astype(f32) == gsel)
    le = jnp.where(in_grp, logits, -jnp.inf)
    m1 = jnp.max(le, 0, keepdims=True)
    i1 = jnp.min(jnp.where(le == m1, row, big), 0, keepdims=True)
    le2 = jnp.where(row == i1, -jnp.inf, le)
    m2 = jnp.max(le2, 0, keepdims=True)
    i2 = jnp.min(jnp.where(le2 == m2, row, big), 0, keepdims=True)
    e2 = jnp.exp(m2 - m1)
    w1 = pg_sel / (1.0 + e2)
    w2 = pg_sel * e2 / (1.0 + e2)
    sub = lax.broadcasted_iota(jnp.int32, route_ref.shape, 0)
    route_ref[...] = jnp.where(sub == 0, i1, jnp.where(sub == 1, i2, jnp.where(sub == 2, w1,
                                                                                jnp.where(sub == 3, w2, 0.0))))


def _router(xc, xl, g_norm, mods, p):
    wr = jnp.zeros((LANES, D), f32).at[:N_EXPERTS].set(p['w_re'].T).at[
        N_EXPERTS:N_EXPERTS + N_GROUPS].set(p['w_rg'].T)
    w_hi, w_lo = _split_bf16(wr)
    br = jnp.zeros((LANES, 1), f32).at[:N_EXPERTS, 0].set(p['b_re']).at[
        N_EXPERTS:N_EXPERTS + N_GROUPS, 0].set(p['b_rg'])
    return pl.pallas_call(
        _router_kernel,
        grid=(T // ROW_TILE,),
        in_specs=_token_specs(D) + [
                  _const_spec((1, D)),
                  pl.BlockSpec((MOD_ROWS, D), lambda i: (_sample_of_tile(i, ROW_TILE), 0)),
                  _const_spec((LANES, D)), _const_spec((LANES, D)), _const_spec((LANES, 1))],
        out_specs=[_token_major_spec(ROW_TILE, lambda i: (i, 0)),
                   pl.BlockSpec((ROUTE_ROWS, ROW_TILE), lambda i: (0, i))],
        out_shape=[jax.ShapeDtypeStruct((T * SUBS, LANES), f32), jax.ShapeDtypeStruct((ROUTE_ROWS, T), f32)],
        compiler_params=_cp(("parallel",)), name="router",
    )(xc, xl, g_norm.reshape(1, D), mods, w_hi, w_lo, br)


PLAN_FIRST_TILE, PLAN_TILES = 0, 1


def _plan_kernel(rt_ref, pos_ref, plan_ref, rank):
    n_blk = T // 128
    e_col = lax.broadcasted_iota(jnp.int32, (N_EXPERTS, 128), 0).astype(f32)
    ri = lax.broadcasted_iota(jnp.int32, (128, 128), 0)
    ci = lax.broadcasted_iota(jnp.int32, (128, 128), 1)
    before = jnp.where(ri < ci, 1.0, 0.0).astype(bf16)

    def picks(b):
        cs = slice(b * 128, (b + 1) * 128)
        return rt_ref[0:1, cs] == e_col, rt_ref[1:2, cs] == e_col

    counts = jnp.zeros((N_EXPERTS, 1), f32)
    for b in range(n_blk):
        m0, m1 = picks(b)
        m = jnp.where(m0, 1.0, 0.0) + jnp.where(m1, 1.0, 0.0)
        rank[:, b * 128:(b + 1) * 128] = jnp.dot(m.astype(bf16), before, preferred_element_type=f32) + counts
        counts = counts + jnp.sum(m, axis=1, keepdims=True)

    tiles = jnp.floor((counts + (MOE_TILE - 1.0)) * (1.0 / MOE_TILE))
    er = lax.broadcasted_iota(jnp.int32, (N_EXPERTS, N_EXPERTS), 0)
    ec = lax.broadcasted_iota(jnp.int32, (N_EXPERTS, N_EXPERTS), 1)
    earlier = jnp.where(ec < er, 1.0, 0.0).astype(bf16)
    tile_start = jnp.dot(earlier, jnp.broadcast_to(tiles, (N_EXPERTS, 128)).astype(bf16),
                         preferred_element_type=f32)
    row_start = tile_start * MOE_TILE

    sub = lax.broadcasted_iota(jnp.int32, (8, 128), 0)
    for b in range(n_blk):
        m0, m1 = picks(b)
        base = rank[:, b * 128:(b + 1) * 128] + row_start
        p0 = jnp.sum(jnp.where(m0, base, 0.0), axis=0, keepdims=True)
        p1 = jnp.sum(jnp.where(m1, base, 0.0), axis=0, keepdims=True)
        pos_ref[:, b * 128:(b + 1) * 128] = jnp.where(sub == 0, p0, jnp.where(sub == 1, p1, 0.0)).astype(jnp.int32)

    diag = (lax.broadcasted_iota(jnp.int32, (N_EXPERTS, 128), 0)
            == lax.broadcasted_iota(jnp.int32, (N_EXPERTS, 128), 1))
    first = jnp.sum(jnp.where(diag, tile_start, 0.0), axis=0, keepdims=True)
    count = jnp.sum(jnp.where(diag, tiles, 0.0), axis=0, keepdims=True)
    rows = jnp.where(sub == PLAN_FIRST_TILE, first, jnp.where(sub == PLAN_TILES, count, 0.0))
    plan_ref[...] = rows.astype(jnp.int32)


def _slot_code(t, k):
    return t * SUBS + k * (SUBS // 2)


def _code_offset(code):
    return pl.multiple_of(code & ~(SUBS - 1), SUBS)


def _code_gate_index(code):
    return code >> 2


PAD_CODE = T * SUBS


def _invert_slots(pos_ref, first_ref, count_ref, code_ref):
    def pad_tile(tile, carry):
        for u in range(MOE_TILE):
            code_ref[tile * MOE_TILE + u] = PAD_CODE
        return carry

    def pad_last_tile(e, carry):
        return pad_tile(jnp.maximum(first_ref[e] + count_ref[e] - 1, 0), carry)
    lax.fori_loop(0, N_EXPERTS, pad_last_tile, 0)
    lax.fori_loop(first_ref[N_EXPERTS - 1] + count_ref[N_EXPERTS - 1], MOE_TILES, pad_tile, 0)

    group = 16
    for k in range(2):
        def place(i, carry):
            t0 = i * group
            slots = [pos_ref[k * T + t0 + u] for u in range(group)]
            for u, s in enumerate(slots):
                code_ref[s] = _slot_code(t0 + u, k)
            return carry
        lax.fori_loop(0, T // group, place, 0)


def _route_plan(route_t):
    pos, plan = pl.pallas_call(
        _plan_kernel,
        out_shape=[jax.ShapeDtypeStruct((8, T), jnp.int32), jax.ShapeDtypeStruct((8, LANES), jnp.int32)],
        scratch_shapes=[pltpu.VMEM((N_EXPERTS, T), f32)],
        compiler_params=_cp(None), name="route_plan",
    )(route_t)
    gates = jnp.pad(route_t[2:4].T.reshape(2 * T), (0, 8))
    return plan[PLAN_FIRST_TILE, :N_EXPERTS], plan[PLAN_TILES, :N_EXPERTS], pos[0:2].reshape(2 * T), gates


def _tile_index(i):
    return jnp.minimum(i, MOE_TILES - 1)


TM_ROWS = T * SUBS
SCATTER_GROUP = 8
STAGES = 4
LAST_EXPERT = N_EXPERTS - 1


def _tile_rows(g):
    return pl.ds(pl.multiple_of(g * MOE_TILE, MOE_TILE), MOE_TILE)


def _expert_tile_pairs(first, count, tile_step):
    def pair(pp, carry):
        for parity in range(2):
            g = 2 * pp + parity
            pl.when((g >= first) & (g < first + count))(functools.partial(tile_step, g, parity))
        return carry
    lax.fori_loop(first // 2, (first + count + 1) // 2, pair, 0)


def _gather_tile(code_ref, tile, xs, gbuf):
    base = tile * MOE_TILE
    for r in range(MOE_TILE):
        gbuf[r * SUBS:(r + 1) * SUBS, :] = xs[pl.ds(_code_offset(code_ref[base + r]), SUBS), :]


def _moe_up_kernel(first_ref, count_ref, pos_ref, h_hbm, w1_ref, w3_ref, hh_hbm, code_ref,
                   xs, gbuf_a, gbuf_b, w13, obuf, sem_x, sem_o):
    e = pl.program_id(0)
    first, count = first_ref[e], count_ref[e]
    gbufs = (gbuf_a, gbuf_b)

    def out_copy(slot, g):
        return pltpu.make_async_copy(obuf.at[slot], hh_hbm.at[_tile_rows(g), :], sem_o.at[slot])

    @pl.when(e == 0)
    def _():
        cp = pltpu.make_async_copy(h_hbm, xs.at[pl.ds(0, TM_ROWS), :], sem_x)
        cp.start()
        _invert_slots(pos_ref, first_ref, count_ref, code_ref)
        xs[TM_ROWS:TM_ROWS + SUBS, :] = jnp.zeros((SUBS, LANES), f32)
        cp.wait()
        _gather_tile(code_ref, 0, xs, gbuf_a)

    @pl.when(count > 0)
    def _():
        w13[:, :D_EXPERT] = w1_ref[0].astype(bf16)
        w13[:, D_EXPERT:] = w3_ref[0].astype(bf16)

    def tile_step(g, parity):
        _gather_tile(code_ref, _tile_index(g + 1), xs, gbufs[1 - parity])
        x3 = jnp.swapaxes(gbufs[parity][...].reshape(MOE_TILE, SUBS, LANES), 0, 1)
        x = jnp.concatenate([x3[s] for s in range(SUBS)], axis=1).astype(bf16)
        h13 = jnp.dot(x, w13[...], preferred_element_type=f32)
        hh = (_silu(h13[:, :D_EXPERT]) * h13[:, D_EXPERT:]).astype(bf16)

        slot = g % STAGES

        @pl.when(g >= STAGES)
        def _():
            out_copy(slot, g).wait()
        obuf[slot] = hh
        out_copy(slot, g).start()

    _expert_tile_pairs(first, count, tile_step)

    @pl.when(e == LAST_EXPERT)
    def _():
        n_used = first + count
        for slot in range(STAGES):
            pl.when(n_used > slot)(lambda slot=slot: out_copy(slot, 0).wait())
        obuf[0] = jnp.zeros((MOE_TILE, D_EXPERT), bf16)

        def zero_tile(g, carry):
            cp = out_copy(0, g)
            cp.start()
            cp.wait()
            return carry
        lax.fori_loop(n_used, MOE_TILES, zero_tile, 0)


def _scatter_tile(code_ref, gate_ref, tile, ybuf, acc):
    base = tile * MOE_TILE
    for g0 in range(0, MOE_TILE, SCATTER_GROUP):
        rows = range(g0, g0 + SCATTER_GROUP)
        codes = [code_ref[base + r] for r in rows]
        new = [acc[pl.ds(_code_offset(c), SUBS), :]
               + gate_ref[_code_gate_index(c)] * ybuf[r * SUBS:(r + 1) * SUBS, :]
               for r, c in zip(rows, codes)]
        for c, v in zip(codes, new):
            acc[pl.ds(_code_offset(c), SUBS), :] = v


RES_ROWS = 256


def _residual_out(x_hbm, y_hbm, tok0, sample_of_chunk, mod_ref, acc, rin, rout, sem_r, sem_w):
    n_chunks = x_hbm.shape[0] // RES_ROWS

    def rows(c):
        return pl.ds(pl.multiple_of(c * RES_ROWS, RES_ROWS), RES_ROWS)

    def in_copy(slot, c):
        return pltpu.make_async_copy(x_hbm.at[rows(c), :], rin.at[slot], sem_r.at[slot])

    def out_copy(slot, c):
        return pltpu.make_async_copy(rout.at[slot], y_hbm.at[rows(c), :], sem_w.at[slot])

    for c in range(STAGES - 1):
        in_copy(c, c).start()

    def ring(cc, carry):
        for slot in range(STAGES):
            c = STAGES * cc + slot
            in_copy(slot, c).wait()
            ahead = c + STAGES - 1

            @pl.when(ahead < n_chunks)
            def _():
                in_copy((slot + STAGES - 1) % STAGES, ahead).start()

            @pl.when(c >= STAGES)
            def _():
                out_copy(slot, c).wait()
            delta = _load_token_major(acc, RES_ROWS, tok0 + c * RES_ROWS)
            gate = mod_ref[pl.ds(sample_of_chunk(c) * MOD_ROWS + GATE2, 1), :]
            rout[slot] = rin[slot] + gate * delta
            out_copy(slot, c).start()
        return carry

    lax.fori_loop(0, n_chunks // STAGES, ring, 0)
    for slot in range(STAGES):
        out_copy(slot, 0).wait()


def _moe_down_kernel(first_ref, count_ref, code_ref, gate_ref, hh_hbm, w2_ref, xc_hbm, xl_hbm, mod_ref,
                     yc_hbm, yl_hbm, acc, ybuf_a, ybuf_b, w2b, ibuf, rin, rout, sem_i, sem_r, sem_w):
    e = pl.program_id(0)
    first, count = first_ref[e], count_ref[e]
    n_used = first_ref[LAST_EXPERT] + count_ref[LAST_EXPERT]
    ybufs = (ybuf_a, ybuf_b)

    def in_copy(slot, g):
        return pltpu.make_async_copy(hh_hbm.at[_tile_rows(g), :], ibuf.at[slot], sem_i.at[slot])

    @pl.when(e == 0)
    def _():
        for g in range(STAGES - 1):
            in_copy(g, g).start()

        def zero(c, carry):
            acc[pl.ds(pl.multiple_of(c * 1024, 1024), 1024), :] = jnp.zeros((1024, LANES), f32)
            return carry
        lax.fori_loop(0, TM_ROWS // 1024, zero, 0)
        acc[TM_ROWS:TM_ROWS + SUBS, :] = jnp.zeros((SUBS, LANES), f32)
        ybuf_b[...] = jnp.zeros_like(ybuf_b)

    @pl.when(count > 0)
    def _():
        w2b[...] = w2_ref[0].astype(bf16)

    def tile_step(g, parity):
        slot = g % STAGES
        in_copy(slot, g).wait()
        ahead = g + STAGES - 1

        @pl.when(ahead < n_used)
        def _():
            in_copy(ahead % STAGES, ahead).start()
        _store_token_major(ybufs[parity], jnp.dot(ibuf[slot], w2b[...], preferred_element_type=f32))
        _scatter_tile(code_ref, gate_ref, jnp.maximum(g - 1, 0), ybufs[1 - parity], acc)

    _expert_tile_pairs(first, count, tile_step)

    @pl.when(e == LAST_EXPERT)
    def _():
        for parity in range(2):
            pl.when((n_used > 0) & ((n_used - 1) % 2 == parity))(
                functools.partial(_scatter_tile, code_ref, gate_ref, n_used - 1, ybufs[parity], acc))
        _residual_out(xc_hbm, yc_hbm, 0, lambda c: 0, mod_ref, acc, rin, rout, sem_r, sem_w)
        _residual_out(xl_hbm, yl_hbm, T_CTX, lambda c: 1 + c // (DEC_SEQ // RES_ROWS),
                      mod_ref, acc, rin, rout, sem_r, sem_w)


def _moe(h_tm, route_t, p, xc, xl, mods):
    first_tile, n_tiles, slots, gates = _route_plan(route_t)
    tile_rows = pltpu.VMEM((MOE_TILE * SUBS, LANES), f32)
    staging = pltpu.VMEM((STAGES, MOE_TILE, D_EXPERT), bf16)
    res_rows = pltpu.VMEM((STAGES, RES_ROWS, D), f32)
    hbm = pl.BlockSpec(memory_space=pl.ANY)
    hh, codes = pl.pallas_call(
        _moe_up_kernel,
        grid_spec=pltpu.PrefetchScalarGridSpec(
            num_scalar_prefetch=3, grid=(N_EXPERTS,),
            in_specs=[hbm,
                      pl.BlockSpec((1, D, D_EXPERT), lambda e, f, n, s: (e, 0, 0)),
                      pl.BlockSpec((1, D, D_EXPERT), lambda e, f, n, s: (e, 0, 0))],
            out_specs=[hbm, pl.BlockSpec(memory_space=pltpu.SMEM)],
            scratch_shapes=[pltpu.VMEM((TM_ROWS + SUBS, LANES), f32), tile_rows, tile_rows,
                            pltpu.VMEM((D, 2 * D_EXPERT), bf16), staging,
                            pltpu.SemaphoreType.DMA(()), pltpu.SemaphoreType.DMA((STAGES,))]),
        out_shape=[jax.ShapeDtypeStruct((MOE_ROWS, D_EXPERT), bf16),
                   jax.ShapeDtypeStruct((MOE_ROWS,), jnp.int32)],
        compiler_params=_cp(("arbitrary",)), name="moe_up",
    )(first_tile, n_tiles, slots, h_tm, p['w1'], p['w3'])
    return pl.pallas_call(
        _moe_down_kernel,
        grid_spec=pltpu.PrefetchScalarGridSpec(
            num_scalar_prefetch=4, grid=(N_EXPERTS,),
            in_specs=[hbm, pl.BlockSpec((1, D_EXPERT, D), lambda e, f, n, c, g: (e, 0, 0)), hbm, hbm,
                      pl.BlockSpec((N_SAMPLES * MOD_ROWS, D), lambda e, f, n, c, g: (0, 0),
                                   pipeline_mode=pl.Buffered(1))],
            out_specs=[hbm, hbm],
            scratch_shapes=[pltpu.VMEM((TM_ROWS + SUBS, LANES), f32), tile_rows, tile_rows,
                            pltpu.VMEM((D_EXPERT, D), bf16), staging, res_rows, res_rows,
                            pltpu.SemaphoreType.DMA((STAGES,)), pltpu.SemaphoreType.DMA((STAGES,)),
                            pltpu.SemaphoreType.DMA((STAGES,))]),
        out_shape=[jax.ShapeDtypeStruct((T_CTX, D), f32), jax.ShapeDtypeStruct((T_LAT, D), f32)],
        compiler_params=_cp(("arbitrary",)), name="moe_down",
    )(first_tile, n_tiles, codes, gates, hh, p['w2'], xc, xl, mods)


def kernel(x_prompt, x_sample, cache_l0_k, cache_l0_v, cache_l1_ckv, cache_l1_kpe, c, c_ctx, l0_g_norm1, l0_g_norm2, l0_w_ada, l0_b_ada, l0_w_in, l0_g_vnorm, l0_w_s, l0_b_s, l0_g_q, l0_g_k, l0_sink, l0_w_o, l0_w_rg, l0_b_rg, l0_w_re, l0_b_re, l0_w1, l0_w3, l0_w2, l1_g_norm1, l1_g_norm2, l1_w_ada, l1_b_ada, l1_w_in, l1_g_qa, l1_w_uq, l1_g_kva, l1_w_ukv, l1_g_q, l1_g_k, l1_w_o, l1_w_rg, l1_b_rg, l1_w_re, l1_b_re, l1_w1, l1_w3, l1_w2):
    p0 = dict(w_in=l0_w_in, g_vnorm=l0_g_vnorm, w_s=l0_w_s, b_s=l0_b_s, g_q=l0_g_q, g_k=l0_g_k, sink=l0_sink,
              w_o=l0_w_o, w_rg=l0_w_rg, b_rg=l0_b_rg, w_re=l0_w_re, b_re=l0_b_re, w1=l0_w1, w3=l0_w3, w2=l0_w2)
    p1 = dict(w_in=l1_w_in, g_qa=l1_g_qa, w_uq=l1_w_uq, g_kva=l1_g_kva, w_ukv=l1_w_ukv, g_q=l1_g_q, g_k=l1_g_k,
              w_o=l1_w_o, w_rg=l1_w_rg, b_rg=l1_b_rg, w_re=l1_w_re, b_re=l1_b_re, w1=l1_w1, w3=l1_w3, w2=l1_w2)

    cond8 = jnp.zeros((MOD_ROWS, D), f32).at[0].set(c_ctx).at[1:1 + DEC_BATCH].set(c)
    mods0 = _mod_rows(_adaln(cond8, l0_w_ada, l0_b_ada))
    mods1 = _mod_rows(_adaln(cond8, l1_w_ada, l1_b_ada))

    xc0 = x_prompt.reshape(T_CTX, D)
    xl0 = x_sample.reshape(T_LAT, D)

    xc0m, xl0m, k_new, v_new = _l0_mixer(xc0, xl0, l0_g_norm1, mods0, p0, cache_l0_k, cache_l0_v)
    h0, route0 = _router(xc0m, xl0m, l0_g_norm2, mods0, p0)
    xc1, xl1 = _moe(h0, route0, p0, xc0m, xl0m, mods0)

    xc1m, xl1m, ckv_new, kpe_new = _l1_mixer(xc1, xl1, l1_g_norm1, mods1, p1, cache_l1_ckv, cache_l1_kpe)
    h1, route1 = _router(xc1m, xl1m, l1_g_norm2, mods1, p1)
    y_prompt, y_sample = _moe(h1, route1, p1, xc1m, xl1m, mods1)
    return (y_prompt.reshape(BATCH, SEQ, D), y_sample.reshape(DEC_BATCH, DEC_SEQ, D), k_new, v_new,
            ckv_new.reshape(BATCH, SEQ, C_KV_LORA), kpe_new.reshape(BATCH, SEQ, C_ROPE))
```

```python
import functools

import jax
import jax.numpy as jnp
import numpy as np
from jax import lax
from jax.experimental import pallas as pl
from jax.experimental.pallas import tpu as pltpu

f32 = jnp.float32
bf16 = jnp.bfloat16

D = 1024
BATCH, SEQ = 32, 256
DEC_BATCH, DEC_SEQ = 2, 1024
PAST = 512
T_CTX = BATCH * SEQ
T_LAT = DEC_BATCH * DEC_SEQ
T = T_CTX + T_LAT
GRID_W = 64
CHUNK = 128
WINDOW = 128
ROPE_THETA = 10000.0
EPS = 1e-6
NEG_INF = -1e30
LANES = 128
SUBS = D // LANES

A_WIDTH = 512
A_GROUPS = 4
B_HEADS, B_KV, B_GROUP, B_HD = 8, 2, 4, 64
B_SCALE = B_HD ** -0.5

C_HEADS, C_Q_LORA, C_KV_LORA, C_NOPE, C_ROPE, C_V = 16, 384, 256, 64, 32, 64
C_QK = C_NOPE + C_ROPE
C_SCALE = C_QK ** -0.5
ODD_IN_PAD = 768
SLOT = 128

N_GROUPS, N_EXPERTS, D_EXPERT = 4, 32, 256

N_SAMPLES = 1 + DEC_BATCH
MOD_ROWS = 8
SHIFT1, SCALE1, GATE1, SHIFT2, SCALE2, GATE2 = range(6)

ROW_TILE = 1024
BLOCK_ROWS = 1024
PROJ_ROWS = 512
ADALN_COLS = 1536
MOE_TILE = 256
MOE_ROWS = 2 * T + N_EXPERTS * MOE_TILE
MOE_TILES = MOE_ROWS // MOE_TILE
VMEM_CAP = 56 * 1024 * 1024


def _cp(sem, vmem=VMEM_CAP):
    return pltpu.CompilerParams(dimension_semantics=sem, vmem_limit_bytes=vmem)


def _const_spec(shape):
    nd = len(shape)
    return pl.BlockSpec(shape, lambda *_: (0,) * nd, pipeline_mode=pl.Buffered(1))


def _sample_of_tile(i, tile):
    n_ctx = T_CTX // tile
    per_lat = DEC_SEQ // tile
    return jnp.where(i < n_ctx, 0, 1 + (i - n_ctx) // per_lat)


def _mod(mod_ref, row):
    return mod_ref[row:row + 1, :]


def _silu(x):
    return x * jax.nn.sigmoid(x)


def _rms_rows(x, g):
    return x * lax.rsqrt(jnp.mean(x * x, -1, keepdims=True) + EPS) * g


def _swap_pairs(x):
    lane = lax.broadcasted_iota(jnp.int32, x.shape, x.ndim - 1)
    nxt = pltpu.roll(x, x.shape[-1] - 1, x.ndim - 1)
    prv = pltpu.roll(x, 1, x.ndim - 1)
    return jnp.where((lane & 1) == 0, nxt, prv)


def _split_bf16(x):
    hi = x.astype(bf16)
    return hi, (x - hi.astype(f32)).astype(bf16)


def _adaln_kernel(c_ref, w_ref, b_ref, o_ref):
    s_hi, s_lo = _split_bf16(_silu(c_ref[...]))
    w_hi, w_lo = _split_bf16(w_ref[...])
    o_ref[...] = (jnp.dot(s_hi, w_hi, preferred_element_type=f32) + jnp.dot(s_lo, w_hi, preferred_element_type=f32)
                  + jnp.dot(s_hi, w_lo, preferred_element_type=f32) + b_ref[...])


def _adaln(cond8, w, b):
    n = w.shape[1]
    return pl.pallas_call(
        _adaln_kernel,
        grid=(n // ADALN_COLS,),
        in_specs=[_const_spec((MOD_ROWS, D)), pl.BlockSpec((D, ADALN_COLS), lambda j: (0, j)),
                  pl.BlockSpec((1, ADALN_COLS), lambda j: (0, j))],
        out_specs=pl.BlockSpec((MOD_ROWS, ADALN_COLS), lambda j: (0, j)),
        out_shape=jax.ShapeDtypeStruct((MOD_ROWS, n), f32),
        compiler_params=_cp(("arbitrary",)),
        name="adaln",
    )(cond8, w, b.reshape(1, n))


def _mod_rows(m8):
    m = m8[:N_SAMPLES].reshape(N_SAMPLES, 6, D)
    return jnp.pad(m, ((0, 0), (0, MOD_ROWS - 6), (0, 0))).reshape(N_SAMPLES * MOD_ROWS, D)


N_CTX_TILES = T_CTX // ROW_TILE


def _token_specs(width):
    return [pl.BlockSpec((ROW_TILE, width), lambda i: (jnp.minimum(i, N_CTX_TILES - 1), 0)),
            pl.BlockSpec((ROW_TILE, width), lambda i: (jnp.maximum(i - N_CTX_TILES, 0), 0))]


def _token_rows(xc_ref, xl_ref):
    return jnp.where(pl.program_id(0) < N_CTX_TILES, xc_ref[...], xl_ref[...])


def _store_token_major(ref, x):
    n = x.shape[0]
    for s in range(SUBS):
        ref[pl.ds(s, n, stride=SUBS), :] = x[:, s * LANES:(s + 1) * LANES]


def _load_token_major(ref, n, row0=0):
    return jnp.concatenate([ref[pl.ds(row0 * SUBS + s, n, stride=SUBS), :] for s in range(SUBS)], axis=1)


def _token_major_spec(rows, index_map):
    return pl.BlockSpec((rows * SUBS, LANES), index_map)


def _rope_tables(n, rot_dim, lanes, lane0, copies=1):
    rows_count = n // GRID_W
    rows = np.repeat(np.arange(rows_count), GRID_W).astype(np.float64)
    cols = np.tile(np.arange(GRID_W), rows_count).astype(np.float64)
    d_axis = rot_dim // 2
    inv = ROPE_THETA ** (-np.arange(0, d_axis, 2, dtype=np.float64) / d_axis)
    ang = np.concatenate([rows[:, None] * inv, cols[:, None] * inv], -1)
    c = np.ones((n, lanes), np.float32)
    s = np.zeros((n, lanes), np.float32)
    for j in range(copies):
        lo = lane0 + j * rot_dim
        c[:, lo:lo + rot_dim] = np.repeat(np.cos(ang), 2, axis=1)
        s[:, lo:lo + rot_dim] = np.repeat(np.sin(ang), 2, axis=1) * np.tile(np.array([-1.0, 1.0]), rot_dim // 2)
    return jnp.asarray(c), jnp.asarray(s)


L0_Q0 = 2 * A_WIDTH
L0_K0 = L0_Q0 + B_GROUP * LANES
L0_V0 = L0_K0 + B_KV * B_HD
L0_IN = L0_V0 + B_KV * B_HD


def _l0_kernel(*refs, latent):
    if latent:
        (sink_ref, x_ref, gn_ref, mod_ref, win_ref, gvn_ref, ws_ref, bsb_ref, gq_ref, gk_ref, wo_ref,
         cos_ref, sin_ref, kc_ref, vc_ref, xo_ref, zs, cat, qs, ks, vt, kcb, vct) = refs
        key_off = WINDOW
    else:
        (sink_ref, x_ref, gn_ref, mod_ref, win_ref, gvn_ref, ws_ref, bsb_ref, gq_ref, gk_ref, wo_ref,
         xo_ref, ko_ref, vo_ref, zs, cat, qs, ks, vt, kf, vf) = refs
        key_off = 0
    n = BLOCK_ROWS
    n_chunks = n // CHUNK
    low = lax.broadcasted_iota(jnp.int32, (CHUNK, LANES), 1) < B_HD

    if latent:
        zpad = jnp.zeros((WINDOW, LANES), bf16)
        for c0 in (0, 1 + n_chunks):
            ks[c0 * CHUNK:(c0 + 1) * CHUNK, :] = zpad
            vt[c0] = zpad
        kcb[...] = kc_ref[...].astype(bf16)
        for i in range(PAST // CHUNK):
            vct[i] = vc_ref[i * CHUNK:(i + 1) * CHUNK, :].T.astype(bf16)

    def project(c, carry):
        r = pl.ds(pl.multiple_of(c * PROJ_ROWS, PROJ_ROWS), PROJ_ROWS)
        h = _rms_rows(x_ref[r, :], gn_ref[...]) * (1.0 + _mod(mod_ref, SCALE1)) + _mod(mod_ref, SHIFT1)
        zs[r, :] = jnp.dot(h.astype(bf16), win_ref[...], preferred_element_type=f32)
        return carry

    lax.fori_loop(0, n // PROJ_ROWS, project, 0)

    def prepare(c, carry):
        r = pl.ds(pl.multiple_of(c * CHUNK, CHUNK), CHUNK)
        u = jax.nn.gelu(zs[r, 0:A_WIDTH])
        v = jax.nn.gelu(zs[r, A_WIDTH:2 * A_WIDTH])
        mu = jnp.mean(v, -1, keepdims=True)
        var = jnp.mean(jnp.square(v - mu), -1, keepdims=True)
        vn = ((v - mu) * lax.rsqrt(var + EPS) * gvn_ref[...]).astype(bf16)
        for g in range(A_GROUPS):
            cs = slice(g * CHUNK, (g + 1) * CHUNK)
            mixed = jnp.dot(ws_ref[g], vn[:, cs], preferred_element_type=f32) + bsb_ref[g]
            cat[r, cs] = (u[:, cs] * mixed).astype(bf16)
        if latent:
            cs_, sn_ = cos_ref[r, :], sin_ref[r, :]
        def half_norm(v, gain):
            sq = v * v
            s0 = jnp.sum(jnp.where(low, sq, 0.0), -1, keepdims=True)
            s1 = jnp.sum(jnp.where(low, 0.0, sq), -1, keepdims=True)
            return v * lax.rsqrt(jnp.where(low, s0, s1) * (1.0 / B_HD) + EPS) * gain

        for j in range(B_GROUP):
            js = slice(j * LANES, (j + 1) * LANES)
            qj = half_norm(zs[r, L0_Q0 + j * LANES:L0_Q0 + (j + 1) * LANES], gq_ref[...])
            if latent:
                qj = qj * cs_ + _swap_pairs(qj) * sn_
            qs[r, js] = qj.astype(bf16)
        k = half_norm(zs[r, L0_K0:L0_K0 + LANES], gk_ref[...])
        vv = zs[r, L0_V0:L0_V0 + LANES]
        if latent:
            k = k * cs_ + _swap_pairs(k) * sn_
        else:
            kf[r, :] = k
            vf[r, :] = vv
        kr = pl.ds(pl.multiple_of(c * CHUNK + key_off, CHUNK), CHUNK)
        ks[kr, :] = k.astype(bf16)
        vt[c + key_off // CHUNK] = vv.T.astype(bf16)
        return carry

    lax.fori_loop(0, n_chunks, prepare, 0)

    def attend(r, rows, key_sets):
        low_q = lax.broadcasted_iota(jnp.int32, (rows, LANES), 1) < B_HD
        slots = [qs[r, j * LANES:(j + 1) * LANES] for j in range(B_GROUP)]
        zero = jnp.zeros((rows, LANES), bf16)
        q = jnp.concatenate([jnp.where(low_q, qj, zero) for qj in slots]
                            + [jnp.where(low_q, zero, qj) for qj in slots], axis=0)
        sk = jnp.concatenate([jnp.full((1, rows), sink_ref[h], f32) for h in range(B_HEADS)], axis=1)
        scores = []
        m = sk
        for k, _, keep in key_sets:
            s = lax.dot_general(k, q, (((1,), (1,)), ((), ())), preferred_element_type=f32)
            if keep is not None:
                s = jnp.where(keep, s, NEG_INF)
            scores.append(s)
            m = jnp.maximum(m, jnp.max(s, 0, keepdims=True))
        den = jnp.exp(sk - m)
        ot = None
        for s, (_, vts, _) in zip(scores, key_sets):
            e = jnp.exp(s - m)
            den = den + jnp.sum(e, 0, keepdims=True)
            eb = e.astype(bf16)
            for i, v_t in enumerate(vts):
                pv = jnp.dot(v_t, eb[i * CHUNK:(i + 1) * CHUNK, :], preferred_element_type=f32)
                ot = pv if ot is None else ot + pv
        ot = ot * (1.0 / den)
        for pair in range(B_HEADS // 2):
            f0 = (2 * pair // B_GROUP) * B_HD
            pair_t = jnp.concatenate([ot[f0:f0 + B_HD, 2 * pair * rows:(2 * pair + 1) * rows],
                                      ot[f0:f0 + B_HD, (2 * pair + 1) * rows:(2 * pair + 2) * rows]], axis=0)
            cat[r, A_WIDTH + pair * LANES:A_WIDTH + (pair + 1) * LANES] = pair_t.T.astype(bf16)

    if latent:
        span = CHUNK + 2 * WINDOW

        def attend_block(c, carry):
            start = pl.multiple_of(c * CHUNK, CHUNK)
            kr = pl.ds(start, span)
            kj = lax.broadcasted_iota(jnp.int32, (span, B_HEADS * CHUNK), 0)
            qi = lax.broadcasted_iota(jnp.int32, (span, B_HEADS * CHUNK), 1) & (CHUNK - 1)
            kpos = start - WINDOW + kj
            keep = (jnp.abs(kj - WINDOW - qi) <= WINDOW) & (kpos >= 0) & (kpos < n)
            attend(pl.ds(start, CHUNK), CHUNK,
                   [(ks[kr, :], [vt[c + i] for i in range(span // CHUNK)], keep),
                    (kcb[...], [vct[i] for i in range(PAST // CHUNK)], None)])
            return carry

        lax.fori_loop(0, n_chunks, attend_block, 0)
    else:
        def attend_seq(sq, carry):
            r = pl.ds(pl.multiple_of(sq * SEQ, SEQ), SEQ)
            attend(r, SEQ, [(ks[r, :], [vt[sq * (SEQ // CHUNK) + i] for i in range(SEQ // CHUNK)], None)])
            ko_ref[sq] = kf[r, :].T
            vo_ref[sq] = vf[r, :].T
            return carry

        lax.fori_loop(0, n // SEQ, attend_seq, 0)

    def output(c, carry):
        r = pl.ds(pl.multiple_of(c * PROJ_ROWS, PROJ_ROWS), PROJ_ROWS)
        y = jnp.dot(cat[r, :], wo_ref[...], preferred_element_type=f32)
        xo_ref[r, :] = x_ref[r, :] + _mod(mod_ref, GATE1) * y
        return carry

    lax.fori_loop(0, n // PROJ_ROWS, output, 0)


def _l0_mixer(xc, xl, g_norm, mods, p, cache_k, cache_v):
    w = p['w_in']
    q_slots = w[:, L0_Q0:L0_Q0 + B_HEADS * B_HD].reshape(D, B_KV, B_GROUP, B_HD).transpose(0, 2, 1, 3).reshape(
        D, B_GROUP * LANES)
    win = jnp.concatenate([w[:, :L0_Q0], q_slots, w[:, L0_Q0 + B_HEADS * B_HD:]], axis=1).astype(bf16)
    gq = jnp.tile(p['g_q'], B_KV).reshape(1, LANES) * B_SCALE
    gk = jnp.tile(p['g_k'], B_KV).reshape(1, LANES)
    gvn = p['g_vnorm'].reshape(1, A_WIDTH)
    ws = p['w_s'].astype(bf16)
    bsb = jnp.broadcast_to(p['b_s'][:, :, None], (A_GROUPS, CHUNK, CHUNK))
    wo = p['w_o'].astype(bf16)
    weights = (g_norm.reshape(1, D),)
    consts = (win, gvn, ws, bsb, gq, gk, wo)
    c_specs = [_const_spec(a.shape) for a in consts]
    smem = pl.BlockSpec(memory_space=pltpu.SMEM)
    row = pl.BlockSpec((BLOCK_ROWS, D), lambda b: (b, 0))
    kv = pl.BlockSpec((BLOCK_ROWS // SEQ, LANES, SEQ), lambda b: (b, 0, 0))

    def scratch(pad):
        return [pltpu.VMEM((BLOCK_ROWS, L0_IN), f32), pltpu.VMEM((BLOCK_ROWS, D), bf16),
                pltpu.VMEM((BLOCK_ROWS, B_GROUP * LANES), bf16), pltpu.VMEM((BLOCK_ROWS + pad, LANES), bf16),
                pltpu.VMEM(((BLOCK_ROWS + pad) // CHUNK, LANES, CHUNK), bf16)]

    kv_shape = jax.ShapeDtypeStruct((BATCH, LANES, SEQ), f32)
    state = pltpu.VMEM((BLOCK_ROWS, LANES), f32)
    xo_ctx, k_t, v_t = pl.pallas_call(
        functools.partial(_l0_kernel, latent=False),
        grid=(T_CTX // BLOCK_ROWS,),
        in_specs=[smem, row, _const_spec((1, D)), pl.BlockSpec((MOD_ROWS, D), lambda b: (0, 0))] + c_specs,
        out_specs=[row, kv, kv],
        out_shape=[jax.ShapeDtypeStruct((T_CTX, D), f32), kv_shape, kv_shape],
        scratch_shapes=scratch(0) + [state, state],
        compiler_params=_cp(("parallel",)), name="l0_mixer_ctx",
    )(p['sink'], xc, *weights, mods, *consts)
    k_new = k_t.reshape(BATCH, B_KV, B_HD, SEQ).transpose(0, 3, 1, 2)
    v_new = v_t.reshape(BATCH, B_KV, B_HD, SEQ).transpose(0, 3, 1, 2)

    cos, sin = _rope_tables(DEC_SEQ, B_HD, LANES, 0, copies=LANES // B_HD)
    cache = pl.BlockSpec((None, PAST, LANES), lambda b: (b, 0, 0))
    past = [pltpu.VMEM((PAST, LANES), bf16), pltpu.VMEM((PAST // CHUNK, LANES, CHUNK), bf16)]
    xo_lat = pl.pallas_call(
        functools.partial(_l0_kernel, latent=True),
        grid=(DEC_BATCH,),
        in_specs=[smem, row, _const_spec((1, D)), pl.BlockSpec((MOD_ROWS, D), lambda b: (1 + b, 0))] + c_specs + [
                  _const_spec(cos.shape), _const_spec(sin.shape), cache, cache],
        out_specs=row,
        out_shape=jax.ShapeDtypeStruct((T_LAT, D), f32),
        scratch_shapes=scratch(2 * WINDOW) + past,
        compiler_params=_cp(("parallel",)), name="l0_mixer_lat",
    )(p['sink'], xl, *weights, mods, *consts, cos, sin,
      cache_k.reshape(DEC_BATCH, PAST, LANES), cache_v.reshape(DEC_BATCH, PAST, LANES))
    return xo_ctx, xo_lat, k_new, v_new


C_SLOTS = C_HEADS * SLOT
C_PAIRS = C_HEADS // 2
L1_ROWS = 256


def _l1_kernel(*refs, latent):
    if latent:
        (x_ref, gn_ref, mod_ref, win_ref, gqa_ref, wuq_ref, gq_ref, gkva_ref, wuk_ref, wuvt_ref, gk_ref,
         wo_ref, wuqs_ref, qcos_ref, qsin_ref, kcos_ref, ksin_ref, cckv_ref, ckpe_ref, xo_ref,
         zs, cat, qs, ks, vt, wide, wide2) = refs
        n_ctx = PAST
    else:
        (x_ref, gn_ref, mod_ref, win_ref, gqa_ref, wuq_ref, gq_ref, gkva_ref, wuk_ref, wuvt_ref, gk_ref,
         wo_ref, xo_ref, ckvo_ref, kpeo_ref, zs, cat, qs, ks, vt, wide) = refs
        n_ctx = 0
    n = BLOCK_ROWS
    nt_dims = (((1,), (1,)), ((), ()))

    def inv_rms(v):
        return lax.rsqrt(jnp.sum(v * v, -1, keepdims=True) * (1.0 / C_QK) + EPS)

    def expand_keys(ckv_n, kslot, kb, rope_rows):
        cb = ckv_n.astype(bf16)
        key_rows = pl.ds(pl.multiple_of(kb * L1_ROWS, L1_ROWS), L1_ROWS)
        wide[...] = jnp.dot(cb, wuk_ref[...], preferred_element_type=f32)
        if rope_rows is not None:
            kcos = kcos_ref[rope_rows, :]
            turned = _swap_pairs(kslot) * ksin_ref[rope_rows, :]
        for h in range(C_HEADS):
            kh = wide[:, h * SLOT:(h + 1) * SLOT] + kslot
            if rope_rows is not None:
                kh = inv_rms(kh) * (kh * kcos + turned)
            else:
                kh = kh * inv_rms(kh) * gk_ref[...]
            ks[h, key_rows, :] = kh.astype(bf16)
        v_t = lax.dot_general(wuvt_ref[...], cb, nt_dims, preferred_element_type=f32).astype(bf16)
        for pair in range(C_PAIRS):
            vt[pair, kb] = v_t[pair * LANES:(pair + 1) * LANES, :]

    if latent:
        def past_keys(c, carry):
            r = pl.ds(pl.multiple_of(c * L1_ROWS, L1_ROWS), L1_ROWS)
            expand_keys(cckv_ref[r, :], ckpe_ref[r, :], c, None)
            return carry

        lax.fori_loop(0, PAST // L1_ROWS, past_keys, 0)

    def project(c, carry):
        r = pl.ds(pl.multiple_of(c * PROJ_ROWS, PROJ_ROWS), PROJ_ROWS)
        h = _rms_rows(x_ref[r, :], gn_ref[...]) * (1.0 + _mod(mod_ref, SCALE1)) + _mod(mod_ref, SHIFT1)
        zs[r, :] = jnp.dot(h.astype(bf16), win_ref[...], preferred_element_type=f32)
        return carry

    lax.fori_loop(0, n // PROJ_ROWS, project, 0)

    def prepare(c, carry):
        r = pl.ds(pl.multiple_of(c * L1_ROWS, L1_ROWS), L1_ROWS)
        qa = _rms_rows(zs[r, 0:C_Q_LORA], gqa_ref[...]).astype(bf16)
        wide[...] = jnp.dot(qa, wuq_ref[...], preferred_element_type=f32)
        if latent:
            wide2[...] = jnp.dot(qa, wuqs_ref[...], preferred_element_type=f32)
            qcos, qsin = qcos_ref[r, :], qsin_ref[r, :]
        for h in range(C_HEADS):
            hs = slice(h * SLOT, (h + 1) * SLOT)
            qh = wide[:, hs]
            if latent:
                qh = inv_rms(qh) * (qh * qcos + wide2[:, hs] * qsin)
            else:
                qh = qh * inv_rms(qh) * gq_ref[...]
            qs[h, r, :] = qh.astype(bf16)
        ckv_n = _rms_rows(zs[r, C_Q_LORA:C_Q_LORA + C_KV_LORA], gkva_ref[...])
        kslot = zs[r, C_Q_LORA + C_KV_LORA:ODD_IN_PAD]
        if not latent:
            ckvo_ref[r, :] = ckv_n
            kpeo_ref[c] = kslot.T[C_NOPE:C_QK, :]
        expand_keys(ckv_n, kslot, c + n_ctx // L1_ROWS, r if latent else None)
        return carry

    lax.fori_loop(0, n // L1_ROWS, prepare, 0)

    low = lax.broadcasted_iota(jnp.int32, (2 * C_V, L1_ROWS), 0) < C_V
    n_kblocks = (n_ctx + n) // L1_ROWS
    pairs_per_step = 2 if latent else 4

    def attend(c, carry):
        r = pl.ds(pl.multiple_of(c * L1_ROWS, L1_ROWS), L1_ROWS)

        def values_t(pair, eb):
            if not latent:
                return jnp.dot(vt[pair, c], eb, preferred_element_type=f32)
            o_t = None
            for b in range(n_kblocks):
                pv = jnp.dot(vt[pair, b], eb[b * L1_ROWS:(b + 1) * L1_ROWS, :], preferred_element_type=f32)
                o_t = pv if o_t is None else o_t + pv
            return o_t

        def pairs_step(i, carry2):
            pairs = [i * pairs_per_step + j for j in range(pairs_per_step)]
            heads = [2 * p + hh for p in pairs for hh in range(2)]
            scores = [lax.dot_general(ks[h] if latent else ks[h, r, :], qs[h, r, :], nt_dims,
                                      preferred_element_type=f32) for h in heads]
            exps = [jnp.exp(s - jnp.max(s, 0, keepdims=True)) for s in scores]
            dens = [jnp.sum(e, 0, keepdims=True) for e in exps]
            outs = [values_t(h // 2, e.astype(bf16)) / den for h, e, den in zip(heads, exps, dens)]
            for j, pair in enumerate(pairs):
                cat[pair, r, :] = jnp.where(low, outs[2 * j], outs[2 * j + 1]).T.astype(bf16)
            return carry2

        return lax.fori_loop(0, C_PAIRS // pairs_per_step, pairs_step, carry)

    lax.fori_loop(0, n // L1_ROWS, attend, 0)

    def output(c, carry):
        r = pl.ds(pl.multiple_of(c * PROJ_ROWS, PROJ_ROWS), PROJ_ROWS)
        heads = jnp.concatenate([cat[pair, r, :] for pair in range(C_PAIRS)], axis=1)
        y = jnp.dot(heads, wo_ref[...], preferred_element_type=f32)
        xo_ref[r, :] = x_ref[r, :] + _mod(mod_ref, GATE1) * y
        return carry

    lax.fori_loop(0, n // PROJ_ROWS, output, 0)


def _slot_cols(w, heads, width, lo, hi, lane0):
    k = w.shape[0]
    w3 = w.reshape(k, heads, width)[:, :, lo:hi]
    out = jnp.zeros((k, heads, SLOT), w.dtype).at[:, :, lane0:lane0 + (hi - lo)].set(w3)
    return out.reshape(k, heads * SLOT)


def _l1_mixer(xc, xl, g_norm, mods, p, cache_ckv, cache_kpe):
    w_in = jnp.zeros((D, ODD_IN_PAD), f32).at[:, :C_Q_LORA + C_KV_LORA].set(
        p['w_in'][:, :C_Q_LORA + C_KV_LORA]).at[
        :, C_Q_LORA + C_KV_LORA + C_NOPE:C_Q_LORA + C_KV_LORA + C_QK].set(p['w_in'][:, C_Q_LORA + C_KV_LORA:])
    wuq = _slot_cols(p['w_uq'], C_HEADS, C_QK, 0, C_QK, 0).astype(bf16)
    wuk = _slot_cols(p['w_ukv'], C_HEADS, C_NOPE + C_V, 0, C_NOPE, 0).astype(bf16)
    wuv_t = p['w_ukv'].reshape(C_KV_LORA, C_HEADS, C_NOPE + C_V)[:, :, C_NOPE:].reshape(
        C_KV_LORA, C_HEADS * C_V).T.astype(bf16)
    gq = jnp.zeros((1, SLOT), f32).at[0, :C_QK].set(p['g_q'] * C_SCALE)
    gk = jnp.zeros((1, SLOT), f32).at[0, :C_QK].set(p['g_k'])
    consts = (g_norm.reshape(1, D), w_in.astype(bf16), p['g_qa'].reshape(1, C_Q_LORA), wuq, gq,
              p['g_kva'].reshape(1, C_KV_LORA), wuk, wuv_t, gk, p['w_o'].astype(bf16))
    c_specs = [_const_spec(a.shape) for a in consts]
    row = pl.BlockSpec((BLOCK_ROWS, D), lambda b: (b, 0))
    n_ctx_blocks = T_CTX // BLOCK_ROWS

    def scratch(n_keys):
        return [pltpu.VMEM((BLOCK_ROWS, ODD_IN_PAD), f32), pltpu.VMEM((C_PAIRS, BLOCK_ROWS, LANES), bf16),
                pltpu.VMEM((C_HEADS, BLOCK_ROWS, SLOT), bf16), pltpu.VMEM((C_HEADS, n_keys, SLOT), bf16),
                pltpu.VMEM((C_PAIRS, n_keys // L1_ROWS, LANES, L1_ROWS), bf16),
                pltpu.VMEM((L1_ROWS, C_SLOTS), f32)]

    xo_ctx, ckv_new, kpe_t = pl.pallas_call(
        functools.partial(_l1_kernel, latent=False),
        grid=(n_ctx_blocks,),
        in_specs=[row, c_specs[0], pl.BlockSpec((MOD_ROWS, D), lambda b: (0, 0))] + c_specs[1:],
        out_specs=[row, pl.BlockSpec((BLOCK_ROWS, C_KV_LORA), lambda b: (b, 0)),
                   pl.BlockSpec((BLOCK_ROWS // SEQ, C_ROPE, SEQ), lambda b: (b, 0, 0))],
        out_shape=[jax.ShapeDtypeStruct((T_CTX, D), f32), jax.ShapeDtypeStruct((T_CTX, C_KV_LORA), f32),
                   jax.ShapeDtypeStruct((BATCH, C_ROPE, SEQ), f32)],
        scratch_shapes=scratch(BLOCK_ROWS),
        compiler_params=_cp(("parallel",)), name="l1_mixer_ctx",
    )(xc, consts[0], mods, *consts[1:])

    cos, sin = _rope_tables(DEC_SEQ, C_ROPE, SLOT, C_NOPE)
    def pair_swap(a):
        pairs = a.reshape(a.shape[:-1] + (a.shape[-1] // 2, 2))
        return jnp.stack([pairs[..., 1], pairs[..., 0]], axis=-1).reshape(a.shape)

    w_rope = p['w_uq'].reshape(C_Q_LORA, C_HEADS, C_QK)[:, :, C_NOPE:]
    wuq_swapped = jnp.zeros((C_Q_LORA, C_HEADS, SLOT), f32).at[:, :, C_NOPE:C_QK].set(pair_swap(w_rope)).reshape(
        C_Q_LORA, C_SLOTS).astype(bf16)
    rope = (wuq_swapped, gq * cos, pair_swap(gq) * sin, gk * cos, pair_swap(gk) * sin)
    ckpe = jnp.zeros((DEC_BATCH, PAST, SLOT), f32).at[:, :, C_NOPE:C_QK].set(cache_kpe)
    xo_lat = pl.pallas_call(
        functools.partial(_l1_kernel, latent=True),
        grid=(DEC_BATCH,),
        in_specs=[pl.BlockSpec((BLOCK_ROWS, D), lambda b: (b, 0), pipeline_mode=pl.Buffered(1)), c_specs[0],
                  pl.BlockSpec((MOD_ROWS, D), lambda b: (1 + b, 0))] + c_specs[1:] + [_const_spec(a.shape) for a in rope] + [
                  pl.BlockSpec((None, PAST, C_KV_LORA), lambda b: (b, 0, 0)),
                  pl.BlockSpec((None, PAST, SLOT), lambda b: (b, 0, 0))],
        out_specs=row,
        out_shape=jax.ShapeDtypeStruct((T_LAT, D), f32),
        scratch_shapes=scratch(PAST + BLOCK_ROWS) + [pltpu.VMEM((L1_ROWS, C_SLOTS), f32)],
        compiler_params=_cp(("parallel",)), name="l1_mixer_lat",
    )(xl, consts[0], mods, *consts[1:], *rope, cache_ckv, ckpe)
    return xo_ctx, xo_lat, ckv_new, kpe_t.transpose(0, 2, 1)


ROUTER_ROWS = 40
ROUTE_ROWS = 8


def _router_kernel(xc_ref, xl_ref, gn_ref, mod_ref, whi_ref, wlo_ref, br_ref, h_ref, route_ref):
    h = (_rms_rows(_token_rows(xc_ref, xl_ref), gn_ref[...]) * (1.0 + _mod(mod_ref, SCALE2))
         + _mod(mod_ref, SHIFT2))
    _store_token_major(h_ref, h)
    h_hi, h_lo = _split_bf16(h)
    nt = (((1,), (1,)), ((), ()))
    logits = (lax.dot_general(whi_ref[...], h_hi, nt, preferred_element_type=f32)
              + lax.dot_general(whi_ref[...], h_lo, nt, preferred_element_type=f32)
              + lax.dot_general(wlo_ref[...], h_hi, nt, preferred_element_type=f32))
    logits = logits[0:ROUTER_ROWS, :] + br_ref[0:ROUTER_ROWS, :]
    row_i = lax.broadcasted_iota(jnp.int32, logits.shape, 0)
    row = row_i.astype(f32)
    big = 1e6
    is_g = (row_i >= N_EXPERTS) & (row_i < N_EXPERTS + N_GROUPS)
    lg = jnp.where(is_g, logits, -jnp.inf)
    mg = jnp.max(lg, 0, keepdims=True)
    gsel = jnp.min(jnp.where(lg == mg, row, big), 0, keepdims=True) - N_EXPERTS
    pg_sel = 1.0 / jnp.sum(jnp.where(is_g, jnp.exp(lg - mg), 0.0), 0, keepdims=True)
    in_grp = (row_i < N_EXPERTS) & ((row_i >> 3).astype(f32) == gsel)
    le = jnp.where(in_grp, logits, -jnp.inf)
    m1 = jnp.max(le, 0, keepdims=True)
    i1 = jnp.min(jnp.where(le == m1, row, big), 0, keepdims=True)
    le2 = jnp.where(row == i1, -jnp.inf, le)
    m2 = jnp.max(le2, 0, keepdims=True)
    i2 = jnp.min(jnp.where(le2 == m2, row, big), 0, keepdims=True)
    e2 = jnp.exp(m2 - m1)
    w1 = pg_sel / (1.0 + e2)
    w2 = pg_sel * e2 / (1.0 + e2)
    sub = lax.broadcasted_iota(jnp.int32, route_ref.shape, 0)
    route_ref[...] = jnp.where(sub == 0, i1, jnp.where(sub == 1, i2, jnp.where(sub == 2, w1,
                                                                                jnp.where(sub == 3, w2, 0.0))))


def _router(xc, xl, g_norm, mods, p):
    wr = jnp.zeros((LANES, D), f32).at[:N_EXPERTS].set(p['w_re'].T).at[
        N_EXPERTS:N_EXPERTS + N_GROUPS].set(p['w_rg'].T)
    w_hi, w_lo = _split_bf16(wr)
    br = jnp.zeros((LANES, 1), f32).at[:N_EXPERTS, 0].set(p['b_re']).at[
        N_EXPERTS:N_EXPERTS + N_GROUPS, 0].set(p['b_rg'])
    return pl.pallas_call(
        _router_kernel,
        grid=(T // ROW_TILE,),
        in_specs=_token_specs(D) + [
                  _const_spec((1, D)),
                  pl.BlockSpec((MOD_ROWS, D), lambda i: (_sample_of_tile(i, ROW_TILE), 0)),
                  _const_spec((LANES, D)), _const_spec((LANES, D)), _const_spec((LANES, 1))],
        out_specs=[_token_major_spec(ROW_TILE, lambda i: (i, 0)),
                   pl.BlockSpec((ROUTE_ROWS, ROW_TILE), lambda i: (0, i))],
        out_shape=[jax.ShapeDtypeStruct((T * SUBS, LANES), f32), jax.ShapeDtypeStruct((ROUTE_ROWS, T), f32)],
        compiler_params=_cp(("parallel",)), name="router",
    )(xc, xl, g_norm.reshape(1, D), mods, w_hi, w_lo, br)


PLAN_FIRST_TILE, PLAN_TILES = 0, 1


def _plan_kernel(rt_ref, pos_ref, plan_ref, rank):
    n_blk = T // 128
    e_col = lax.broadcasted_iota(jnp.int32, (N_EXPERTS, 128), 0).astype(f32)
    ri = lax.broadcasted_iota(jnp.int32, (128, 128), 0)
    ci = lax.broadcasted_iota(jnp.int32, (128, 128), 1)
    before = jnp.where(ri < ci, 1.0, 0.0).astype(bf16)

    def picks(b):
        cs = slice(b * 128, (b + 1) * 128)
        return rt_ref[0:1, cs] == e_col, rt_ref[1:2, cs] == e_col

    counts = jnp.zeros((N_EXPERTS, 1), f32)
    for b in range(n_blk):
        m0, m1 = picks(b)
        m = jnp.where(m0, 1.0, 0.0) + jnp.where(m1, 1.0, 0.0)
        rank[:, b * 128:(b + 1) * 128] = jnp.dot(m.astype(bf16), before, preferred_element_type=f32) + counts
        counts = counts + jnp.sum(m, axis=1, keepdims=True)

    tiles = jnp.floor((counts + (MOE_TILE - 1.0)) * (1.0 / MOE_TILE))
    er = lax.broadcasted_iota(jnp.int32, (N_EXPERTS, N_EXPERTS), 0)
    ec = lax.broadcasted_iota(jnp.int32, (N_EXPERTS, N_EXPERTS), 1)
    earlier = jnp.where(ec < er, 1.0, 0.0).astype(bf16)
    tile_start = jnp.dot(earlier, jnp.broadcast_to(tiles, (N_EXPERTS, 128)).astype(bf16),
                         preferred_element_type=f32)
    row_start = tile_start * MOE_TILE

    sub = lax.broadcasted_iota(jnp.int32, (8, 128), 0)
    for b in range(n_blk):
        m0, m1 = picks(b)
        base = rank[:, b * 128:(b + 1) * 128] + row_start
        p0 = jnp.sum(jnp.where(m0, base, 0.0), axis=0, keepdims=True)
        p1 = jnp.sum(jnp.where(m1, base, 0.0), axis=0, keepdims=True)
        pos_ref[:, b * 128:(b + 1) * 128] = jnp.where(sub == 0, p0, jnp.where(sub == 1, p1, 0.0)).astype(jnp.int32)

    diag = (lax.broadcasted_iota(jnp.int32, (N_EXPERTS, 128), 0)
            == lax.broadcasted_iota(jnp.int32, (N_EXPERTS, 128), 1))
    first = jnp.sum(jnp.where(diag, tile_start, 0.0), axis=0, keepdims=True)
    count = jnp.sum(jnp.where(diag, tiles, 0.0), axis=0, keepdims=True)
    rows = jnp.where(sub == PLAN_FIRST_TILE, first, jnp.where(sub == PLAN_TILES, count, 0.0))
    plan_ref[...] = rows.astype(jnp.int32)


def _slot_code(t, k):
    return t * SUBS + k * (SUBS // 2)


def _code_offset(code):
    return pl.multiple_of(code & ~(SUBS - 1), SUBS)


def _code_gate_index(code):
    return code >> 2


PAD_CODE = T * SUBS


def _invert_slots(pos_ref, first_ref, count_ref, code_ref):
    def pad_tile(tile, carry):
        for u in range(MOE_TILE):
            code_ref[tile * MOE_TILE + u] = PAD_CODE
        return carry

    def pad_last_tile(e, carry):
        return pad_tile(jnp.maximum(first_ref[e] + count_ref[e] - 1, 0), carry)
    lax.fori_loop(0, N_EXPERTS, pad_last_tile, 0)
    lax.fori_loop(first_ref[N_EXPERTS - 1] + count_ref[N_EXPERTS - 1], MOE_TILES, pad_tile, 0)

    group = 16
    for k in range(2):
        def place(i, carry):
            t0 = i * group
            slots = [pos_ref[k * T + t0 + u] for u in range(group)]
            for u, s in enumerate(slots):
                code_ref[s] = _slot_code(t0 + u, k)
            return carry
        lax.fori_loop(0, T // group, place, 0)


def _route_plan(route_t):
    pos, plan = pl.pallas_call(
        _plan_kernel,
        out_shape=[jax.ShapeDtypeStruct((8, T), jnp.int32), jax.ShapeDtypeStruct((8, LANES), jnp.int32)],
        scratch_shapes=[pltpu.VMEM((N_EXPERTS, T), f32)],
        compiler_params=_cp(None), name="route_plan",
    )(route_t)
    gates = jnp.pad(route_t[2:4].T.reshape(2 * T), (0, 8))
    return plan[PLAN_FIRST_TILE, :N_EXPERTS], plan[PLAN_TILES, :N_EXPERTS], pos[0:2].reshape(2 * T), gates


def _tile_index(i):
    return jnp.minimum(i, MOE_TILES - 1)


TM_ROWS = T * SUBS
SCATTER_GROUP = 8
STAGES = 4
LAST_EXPERT = N_EXPERTS - 1


def _tile_rows(g):
    return pl.ds(pl.multiple_of(g * MOE_TILE, MOE_TILE), MOE_TILE)


def _expert_tile_pairs(first, count, tile_step):
    def pair(pp, carry):
        for parity in range(2):
            g = 2 * pp + parity
            pl.when((g >= first) & (g < first + count))(functools.partial(tile_step, g, parity))
        return carry
    lax.fori_loop(first // 2, (first + count + 1) // 2, pair, 0)


def _gather_tile(code_ref, tile, xs, gbuf):
    base = tile * MOE_TILE
    for r in range(MOE_TILE):
        gbuf[r * SUBS:(r + 1) * SUBS, :] = xs[pl.ds(_code_offset(code_ref[base + r]), SUBS), :]


def _moe_up_kernel(first_ref, count_ref, pos_ref, h_hbm, w1_ref, w3_ref, hh_hbm, code_ref,
                   xs, gbuf_a, gbuf_b, w13, obuf, sem_x, sem_o):
    e = pl.program_id(0)
    first, count = first_ref[e], count_ref[e]
    gbufs = (gbuf_a, gbuf_b)

    def out_copy(slot, g):
        return pltpu.make_async_copy(obuf.at[slot], hh_hbm.at[_tile_rows(g), :], sem_o.at[slot])

    @pl.when(e == 0)
    def _():
        cp = pltpu.make_async_copy(h_hbm, xs.at[pl.ds(0, TM_ROWS), :], sem_x)
        cp.start()
        _invert_slots(pos_ref, first_ref, count_ref, code_ref)
        xs[TM_ROWS:TM_ROWS + SUBS, :] = jnp.zeros((SUBS, LANES), f32)
        cp.wait()
        _gather_tile(code_ref, 0, xs, gbuf_a)

    @pl.when(count > 0)
    def _():
        w13[:, :D_EXPERT] = w1_ref[0].astype(bf16)
        w13[:, D_EXPERT:] = w3_ref[0].astype(bf16)

    def tile_step(g, parity):
        _gather_tile(code_ref, _tile_index(g + 1), xs, gbufs[1 - parity])
        x3 = jnp.swapaxes(gbufs[parity][...].reshape(MOE_TILE, SUBS, LANES), 0, 1)
        x = jnp.concatenate([x3[s] for s in range(SUBS)], axis=1).astype(bf16)
        h13 = jnp.dot(x, w13[...], preferred_element_type=f32)
        hh = (_silu(h13[:, :D_EXPERT]) * h13[:, D_EXPERT:]).astype(bf16)

        slot = g % STAGES

        @pl.when(g >= STAGES)
        def _():
            out_copy(slot, g).wait()
        obuf[slot] = hh
        out_copy(slot, g).start()

    _expert_tile_pairs(first, count, tile_step)

    @pl.when(e == LAST_EXPERT)
    def _():
        n_used = first + count
        for slot in range(STAGES):
            pl.when(n_used > slot)(lambda slot=slot: out_copy(slot, 0).wait())
        obuf[0] = jnp.zeros((MOE_TILE, D_EXPERT), bf16)

        def zero_tile(g, carry):
            cp = out_copy(0, g)
            cp.start()
            cp.wait()
            return carry
        lax.fori_loop(n_used, MOE_TILES, zero_tile, 0)


def _scatter_tile(code_ref, gate_ref, tile, ybuf, acc):
    base = tile * MOE_TILE
    for g0 in range(0, MOE_TILE, SCATTER_GROUP):
        rows = range(g0, g0 + SCATTER_GROUP)
        codes = [code_ref[base + r] for r in rows]
        new = [acc[pl.ds(_code_offset(c), SUBS), :]
               + gate_ref[_code_gate_index(c)] * ybuf[r * SUBS:(r + 1) * SUBS, :]
               for r, c in zip(rows, codes)]
        for c, v in zip(codes, new):
            acc[pl.ds(_code_offset(c), SUBS), :] = v


RES_ROWS = 256


def _residual_out(x_hbm, y_hbm, tok0, sample_of_chunk, mod_ref, acc, rin, rout, sem_r, sem_w):
    n_chunks = x_hbm.shape[0] // RES_ROWS

    def rows(c):
        return pl.ds(pl.multiple_of(c * RES_ROWS, RES_ROWS), RES_ROWS)

    def in_copy(slot, c):
        return pltpu.make_async_copy(x_hbm.at[rows(c), :], rin.at[slot], sem_r.at[slot])

    def out_copy(slot, c):
        return pltpu.make_async_copy(rout.at[slot], y_hbm.at[rows(c), :], sem_w.at[slot])

    for c in range(STAGES - 1):
        in_copy(c, c).start()

    def ring(cc, carry):
        for slot in range(STAGES):
            c = STAGES * cc + slot
            in_copy(slot, c).wait()
            ahead = c + STAGES - 1

            @pl.when(ahead < n_chunks)
            def _():
                in_copy((slot + STAGES - 1) % STAGES, ahead).start()

            @pl.when(c >= STAGES)
            def _():
                out_copy(slot, c).wait()
            delta = _load_token_major(acc, RES_ROWS, tok0 + c * RES_ROWS)
            gate = mod_ref[pl.ds(sample_of_chunk(c) * MOD_ROWS + GATE2, 1), :]
            rout[slot] = rin[slot] + gate * delta
            out_copy(slot, c).start()
        return carry

    lax.fori_loop(0, n_chunks // STAGES, ring, 0)
    for slot in range(STAGES):
        out_copy(slot, 0).wait()


def _moe_down_kernel(first_ref, count_ref, code_ref, gate_ref, hh_hbm, w2_ref, xc_hbm, xl_hbm, mod_ref,
                     yc_hbm, yl_hbm, acc, ybuf_a, ybuf_b, w2b, ibuf, rin, rout, sem_i, sem_r, sem_w):
    e = pl.program_id(0)
    first, count = first_ref[e], count_ref[e]
    n_used = first_ref[LAST_EXPERT] + count_ref[LAST_EXPERT]
    ybufs = (ybuf_a, ybuf_b)

    def in_copy(slot, g):
        return pltpu.make_async_copy(hh_hbm.at[_tile_rows(g), :], ibuf.at[slot], sem_i.at[slot])

    @pl.when(e == 0)
    def _():
        for g in range(STAGES - 1):
            in_copy(g, g).start()

        def zero(c, carry):
            acc[pl.ds(pl.multiple_of(c * 1024, 1024), 1024), :] = jnp.zeros((1024, LANES), f32)
            return carry
        lax.fori_loop(0, TM_ROWS // 1024, zero, 0)
        acc[TM_ROWS:TM_ROWS + SUBS, :] = jnp.zeros((SUBS, LANES), f32)
        ybuf_b[...] = jnp.zeros_like(ybuf_b)

    @pl.when(count > 0)
    def _():
        w2b[...] = w2_ref[0].astype(bf16)

    def tile_step(g, parity):
        slot = g % STAGES
        in_copy(slot, g).wait()
        ahead = g + STAGES - 1

        @pl.when(ahead < n_used)
        def _():
            in_copy(ahead % STAGES, ahead).start()
        _store_token_major(ybufs[parity], jnp.dot(ibuf[slot], w2b[...], preferred_element_type=f32))
        _scatter_tile(code_ref, gate_ref, jnp.maximum(g - 1, 0), ybufs[1 - parity], acc)

    _expert_tile_pairs(first, count, tile_step)

    @pl.when(e == LAST_EXPERT)
    def _():
        for parity in range(2):
            pl.when((n_used > 0) & ((n_used - 1) % 2 == parity))(
                functools.partial(_scatter_tile, code_ref, gate_ref, n_used - 1, ybufs[parity], acc))
        _residual_out(xc_hbm, yc_hbm, 0, lambda c: 0, mod_ref, acc, rin, rout, sem_r, sem_w)
        _residual_out(xl_hbm, yl_hbm, T_CTX, lambda c: 1 + c // (DEC_SEQ // RES_ROWS),
                      mod_ref, acc, rin, rout, sem_r, sem_w)


def _moe(h_tm, route_t, p, xc, xl, mods):
    first_tile, n_tiles, slots, gates = _route_plan(route_t)
    tile_rows = pltpu.VMEM((MOE_TILE * SUBS, LANES), f32)
    staging = pltpu.VMEM((STAGES, MOE_TILE, D_EXPERT), bf16)
    res_rows = pltpu.VMEM((STAGES, RES_ROWS, D), f32)
    hbm = pl.BlockSpec(memory_space=pl.ANY)
    hh, codes = pl.pallas_call(
        _moe_up_kernel,
        grid_spec=pltpu.PrefetchScalarGridSpec(
            num_scalar_prefetch=3, grid=(N_EXPERTS,),
            in_specs=[hbm,
                      pl.BlockSpec((1, D, D_EXPERT), lambda e, f, n, s: (e, 0, 0)),
                      pl.BlockSpec((1, D, D_EXPERT), lambda e, f, n, s: (e, 0, 0))],
            out_specs=[hbm, pl.BlockSpec(memory_space=pltpu.SMEM)],
            scratch_shapes=[pltpu.VMEM((TM_ROWS + SUBS, LANES), f32), tile_rows, tile_rows,
                            pltpu.VMEM((D, 2 * D_EXPERT), bf16), staging,
                            pltpu.SemaphoreType.DMA(()), pltpu.SemaphoreType.DMA((STAGES,))]),
        out_shape=[jax.ShapeDtypeStruct((MOE_ROWS, D_EXPERT), bf16),
                   jax.ShapeDtypeStruct((MOE_ROWS,), jnp.int32)],
        compiler_params=_cp(("arbitrary",)), name="moe_up",
    )(first_tile, n_tiles, slots, h_tm, p['w1'], p['w3'])
    return pl.pallas_call(
        _moe_down_kernel,
        grid_spec=pltpu.PrefetchScalarGridSpec(
            num_scalar_prefetch=4, grid=(N_EXPERTS,),
            in_specs=[hbm, pl.BlockSpec((1, D_EXPERT, D), lambda e, f, n, c, g: (e, 0, 0)), hbm, hbm,
                      pl.BlockSpec((N_SAMPLES * MOD_ROWS, D), lambda e, f, n, c, g: (0, 0),
                                   pipeline_mode=pl.Buffered(1))],
            out_specs=[hbm, hbm],
            scratch_shapes=[pltpu.VMEM((TM_ROWS + SUBS, LANES), f32), tile_rows, tile_rows,
                            pltpu.VMEM((D_EXPERT, D), bf16), staging, res_rows, res_rows,
                            pltpu.SemaphoreType.DMA((STAGES,)), pltpu.SemaphoreType.DMA((STAGES,)),
                            pltpu.SemaphoreType.DMA((STAGES,))]),
        out_shape=[jax.ShapeDtypeStruct((T_CTX, D), f32), jax.ShapeDtypeStruct((T_LAT, D), f32)],
        compiler_params=_cp(("arbitrary",)), name="moe_down",
    )(first_tile, n_tiles, codes, gates, hh, p['w2'], xc, xl, mods)


def kernel(x_prompt, x_sample, cache_l0_k, cache_l0_v, cache_l1_ckv, cache_l1_kpe, c, c_ctx, l0_g_norm1, l0_g_norm2, l0_w_ada, l0_b_ada, l0_w_in, l0_g_vnorm, l0_w_s, l0_b_s, l0_g_q, l0_g_k, l0_sink, l0_w_o, l0_w_rg, l0_b_rg, l0_w_re, l0_b_re, l0_w1, l0_w3, l0_w2, l1_g_norm1, l1_g_norm2, l1_w_ada, l1_b_ada, l1_w_in, l1_g_qa, l1_w_uq, l1_g_kva, l1_w_ukv, l1_g_q, l1_g_k, l1_w_o, l1_w_rg, l1_b_rg, l1_w_re, l1_b_re, l1_w1, l1_w3, l1_w2):
    p0 = dict(w_in=l0_w_in, g_vnorm=l0_g_vnorm, w_s=l0_w_s, b_s=l0_b_s, g_q=l0_g_q, g_k=l0_g_k, sink=l0_sink,
              w_o=l0_w_o, w_rg=l0_w_rg, b_rg=l0_b_rg, w_re=l0_w_re, b_re=l0_b_re, w1=l0_w1, w3=l0_w3, w2=l0_w2)
    p1 = dict(w_in=l1_w_in, g_qa=l1_g_qa, w_uq=l1_w_uq, g_kva=l1_g_kva, w_ukv=l1_w_ukv, g_q=l1_g_q, g_k=l1_g_k,
              w_o=l1_w_o, w_rg=l1_w_rg, b_rg=l1_b_rg, w_re=l1_w_re, b_re=l1_b_re, w1=l1_w1, w3=l1_w3, w2=l1_w2)

    cond8 = jnp.zeros((MOD_ROWS, D), f32).at[0].set(c_ctx).at[1:1 + DEC_BATCH].set(c)
    mods0 = _mod_rows(_adaln(cond8, l0_w_ada, l0_b_ada))
    mods1 = _mod_rows(_adaln(cond8, l1_w_ada, l1_b_ada))

    xc0 = x_prompt.reshape(T_CTX, D)
    xl0 = x_sample.reshape(T_LAT, D)

    xc0m, xl0m, k_new, v_new = _l0_mixer(xc0, xl0, l0_g_norm1, mods0, p0, cache_l0_k, cache_l0_v)
    h0, route0 = _router(xc0m, xl0m, l0_g_norm2, mods0, p0)
    xc1, xl1 = _moe(h0, route0, p0, xc0m, xl0m, mods0)

    xc1m, xl1m, ckv_new, kpe_new = _l1_mixer(xc1, xl1, l1_g_norm1, mods1, p1, cache_l1_ckv, cache_l1_kpe)
    h1, route1 = _router(xc1m, xl1m, l1_g_norm2, mods1, p1)
    y_prompt, y_sample = _moe(h1, route1, p1, xc1m, xl1m, mods1)
    return (y_prompt.reshape(BATCH, SEQ, D), y_sample.reshape(DEC_BATCH, DEC_SEQ, D), k_new, v_new,
            ckv_new.reshape(BATCH, SEQ, C_KV_LORA), kpe_new.reshape(BATCH, SEQ, C_ROPE))
```

```python
import functools

import jax
import jax.numpy as jnp
import numpy as np
from jax import lax
from jax.experimental import pallas as pl
from jax.experimental.pallas import tpu as pltpu

f32 = jnp.float32
bf16 = jnp.bfloat16

D = 1024
BATCH, SEQ = 32, 256
DEC_BATCH, DEC_SEQ = 2, 1024
PAST = 512
T_CTX = BATCH * SEQ
T_LAT = DEC_BATCH * DEC_SEQ
T = T_CTX + T_LAT
GRID_W = 64
CHUNK = 128
WINDOW = 128
ROPE_THETA = 10000.0
EPS = 1e-6
NEG_INF = -1e30
LANES = 128
SUBS = D // LANES

A_WIDTH = 512
A_GROUPS = 4
B_HEADS, B_KV, B_GROUP, B_HD = 8, 2, 4, 64
B_SCALE = B_HD ** -0.5

C_HEADS, C_Q_LORA, C_KV_LORA, C_NOPE, C_ROPE, C_V = 16, 384, 256, 64, 32, 64
C_QK = C_NOPE + C_ROPE
C_SCALE = C_QK ** -0.5
ODD_IN_PAD = 768
SLOT = 128

N_GROUPS, N_EXPERTS, D_EXPERT = 4, 32, 256

N_SAMPLES = 1 + DEC_BATCH
MOD_ROWS = 8
SHIFT1, SCALE1, GATE1, SHIFT2, SCALE2, GATE2 = range(6)

ROW_TILE = 1024
BLOCK_ROWS = 1024
PROJ_ROWS = 512
ADALN_COLS = 1536
MOE_TILE = 256
MOE_ROWS = 2 * T + N_EXPERTS * MOE_TILE
MOE_TILES = MOE_ROWS // MOE_TILE
VMEM_CAP = 56 * 1024 * 1024


def _cp(sem, vmem=VMEM_CAP):
    return pltpu.CompilerParams(dimension_semantics=sem, vmem_limit_bytes=vmem)


def _const_spec(shape):
    nd = len(shape)
    return pl.BlockSpec(shape, lambda *_: (0,) * nd, pipeline_mode=pl.Buffered(1))


def _sample_of_tile(i, tile):
    n_ctx = T_CTX // tile
    per_lat = DEC_SEQ // tile
    return jnp.where(i < n_ctx, 0, 1 + (i - n_ctx) // per_lat)


def _mod(mod_ref, row):
    return mod_ref[row:row + 1, :]


def _silu(x):
    return x * jax.nn.sigmoid(x)


def _rms_rows(x, g):
    return x * lax.rsqrt(jnp.mean(x * x, -1, keepdims=True) + EPS) * g


def _swap_pairs(x):
    lane = lax.broadcasted_iota(jnp.int32, x.shape, x.ndim - 1)
    nxt = pltpu.roll(x, x.shape[-1] - 1, x.ndim - 1)
    prv = pltpu.roll(x, 1, x.ndim - 1)
    return jnp.where((lane & 1) == 0, nxt, prv)


def _split_bf16(x):
    hi = x.astype(bf16)
    return hi, (x - hi.astype(f32)).astype(bf16)


def _adaln_kernel(c_ref, w_ref, b_ref, o_ref):
    s_hi, s_lo = _split_bf16(_silu(c_ref[...]))
    w_hi, w_lo = _split_bf16(w_ref[...])
    o_ref[...] = (jnp.dot(s_hi, w_hi, preferred_element_type=f32) + jnp.dot(s_lo, w_hi, preferred_element_type=f32)
                  + jnp.dot(s_hi, w_lo, preferred_element_type=f32) + b_ref[...])


def _adaln(cond8, w, b):
    n = w.shape[1]
    return pl.pallas_call(
        _adaln_kernel,
        grid=(n // ADALN_COLS,),
        in_specs=[_const_spec((MOD_ROWS, D)), pl.BlockSpec((D, ADALN_COLS), lambda j: (0, j)),
                  pl.BlockSpec((1, ADALN_COLS), lambda j: (0, j))],
        out_specs=pl.BlockSpec((MOD_ROWS, ADALN_COLS), lambda j: (0, j)),
        out_shape=jax.ShapeDtypeStruct((MOD_ROWS, n), f32),
        compiler_params=_cp(("arbitrary",)),
        name="adaln",
    )(cond8, w, b.reshape(1, n))


def _mod_rows(m8):
    m = m8[:N_SAMPLES].reshape(N_SAMPLES, 6, D)
    return jnp.pad(m, ((0, 0), (0, MOD_ROWS - 6), (0, 0))).reshape(N_SAMPLES * MOD_ROWS, D)


N_CTX_TILES = T_CTX // ROW_TILE


def _token_specs(width):
    return [pl.BlockSpec((ROW_TILE, width), lambda i: (jnp.minimum(i, N_CTX_TILES - 1), 0)),
            pl.BlockSpec((ROW_TILE, width), lambda i: (jnp.maximum(i - N_CTX_TILES, 0), 0))]


def _token_rows(xc_ref, xl_ref):
    return jnp.where(pl.program_id(0) < N_CTX_TILES, xc_ref[...], xl_ref[...])


def _store_token_major(ref, x):
    n = x.shape[0]
    for s in range(SUBS):
        ref[pl.ds(s, n, stride=SUBS), :] = x[:, s * LANES:(s + 1) * LANES]


def _load_token_major(ref, n, row0=0):
    return jnp.concatenate([ref[pl.ds(row0 * SUBS + s, n, stride=SUBS), :] for s in range(SUBS)], axis=1)


def _token_major_spec(rows, index_map):
    return pl.BlockSpec((rows * SUBS, LANES), index_map)


def _rope_tables(n, rot_dim, lanes, lane0, copies=1):
    rows_count = n // GRID_W
    rows = np.repeat(np.arange(rows_count), GRID_W).astype(np.float64)
    cols = np.tile(np.arange(GRID_W), rows_count).astype(np.float64)
    d_axis = rot_dim // 2
    inv = ROPE_THETA ** (-np.arange(0, d_axis, 2, dtype=np.float64) / d_axis)
    ang = np.concatenate([rows[:, None] * inv, cols[:, None] * inv], -1)
    c = np.ones((n, lanes), np.float32)
    s = np.zeros((n, lanes), np.float32)
    for j in range(copies):
        lo = lane0 + j * rot_dim
        c[:, lo:lo + rot_dim] = np.repeat(np.cos(ang), 2, axis=1)
        s[:, lo:lo + rot_dim] = np.repeat(np.sin(ang), 2, axis=1) * np.tile(np.array([-1.0, 1.0]), rot_dim // 2)
    return jnp.asarray(c), jnp.asarray(s)


L0_Q0 = 2 * A_WIDTH
L0_K0 = L0_Q0 + B_GROUP * LANES
L0_V0 = L0_K0 + B_KV * B_HD
L0_IN = L0_V0 + B_KV * B_HD


def _l0_kernel(*refs, latent):
    if latent:
        (sink_ref, x_ref, gn_ref, mod_ref, win_ref, gvn_ref, ws_ref, bsb_ref, gq_ref, gk_ref, wo_ref,
         cos_ref, sin_ref, kc_ref, vc_ref, xo_ref, zs, cat, qs, ks, vt, kcb, vct) = refs
        key_off = WINDOW
    else:
        (sink_ref, x_ref, gn_ref, mod_ref, win_ref, gvn_ref, ws_ref, bsb_ref, gq_ref, gk_ref, wo_ref,
         xo_ref, ko_ref, vo_ref, zs, cat, qs, ks, vt, kf, vf) = refs
        key_off = 0
    n = BLOCK_ROWS
    n_chunks = n // CHUNK
    low = lax.broadcasted_iota(jnp.int32, (CHUNK, LANES), 1) < B_HD

    if latent:
        zpad = jnp.zeros((WINDOW, LANES), bf16)
        for c0 in (0, 1 + n_chunks):
            ks[c0 * CHUNK:(c0 + 1) * CHUNK, :] = zpad
            vt[c0] = zpad
        kcb[...] = kc_ref[...].astype(bf16)
        for i in range(PAST // CHUNK):
            vct[i] = vc_ref[i * CHUNK:(i + 1) * CHUNK, :].T.astype(bf16)

    def project(c, carry):
        r = pl.ds(pl.multiple_of(c * PROJ_ROWS, PROJ_ROWS), PROJ_ROWS)
        h = _rms_rows(x_ref[r, :], gn_ref[...]) * (1.0 + _mod(mod_ref, SCALE1)) + _mod(mod_ref, SHIFT1)
        zs[r, :] = jnp.dot(h.astype(bf16), win_ref[...], preferred_element_type=f32)
        return carry

    lax.fori_loop(0, n // PROJ_ROWS, project, 0)

    def prepare(c, carry):
        r = pl.ds(pl.multiple_of(c * CHUNK, CHUNK), CHUNK)
        u = jax.nn.gelu(zs[r, 0:A_WIDTH])
        v = jax.nn.gelu(zs[r, A_WIDTH:2 * A_WIDTH])
        mu = jnp.mean(v, -1, keepdims=True)
        var = jnp.mean(jnp.square(v - mu), -1, keepdims=True)
        vn = ((v - mu) * lax.rsqrt(var + EPS) * gvn_ref[...]).astype(bf16)
        for g in range(A_GROUPS):
            cs = slice(g * CHUNK, (g + 1) * CHUNK)
            mixed = jnp.dot(ws_ref[g], vn[:, cs], preferred_element_type=f32) + bsb_ref[g]
            cat[r, cs] = (u[:, cs] * mixed).astype(bf16)
        if latent:
            cs_, sn_ = cos_ref[r, :], sin_ref[r, :]
        def half_norm(v, gain):
            sq = v * v
            s0 = jnp.sum(jnp.where(low, sq, 0.0), -1, keepdims=True)
            s1 = jnp.sum(jnp.where(low, 0.0, sq), -1, keepdims=True)
            return v * lax.rsqrt(jnp.where(low, s0, s1) * (1.0 / B_HD) + EPS) * gain

        for j in range(B_GROUP):
            js = slice(j * LANES, (j + 1) * LANES)
            qj = half_norm(zs[r, L0_Q0 + j * LANES:L0_Q0 + (j + 1) * LANES], gq_ref[...])
            if latent:
                qj = qj * cs_ + _swap_pairs(qj) * sn_
            qs[r, js] = qj.astype(bf16)
        k = half_norm(zs[r, L0_K0:L0_K0 + LANES], gk_ref[...])
        vv = zs[r, L0_V0:L0_V0 + LANES]
        if latent:
            k = k * cs_ + _swap_pairs(k) * sn_
        else:
            kf[r, :] = k
            vf[r, :] = vv
        kr = pl.ds(pl.multiple_of(c * CHUNK + key_off, CHUNK), CHUNK)
        ks[kr, :] = k.astype(bf16)
        vt[c + key_off // CHUNK] = vv.T.astype(bf16)
        return carry

    lax.fori_loop(0, n_chunks, prepare, 0)

    def attend(r, rows, key_sets):
        low_q = lax.broadcasted_iota(jnp.int32, (rows, LANES), 1) < B_HD
        slots = [qs[r, j * LANES:(j + 1) * LANES] for j in range(B_GROUP)]
        zero = jnp.zeros((rows, LANES), bf16)
        q = jnp.concatenate([jnp.where(low_q, qj, zero) for qj in slots]
                            + [jnp.where(low_q, zero, qj) for qj in slots], axis=0)
        sk = jnp.concatenate([jnp.full((1, rows), sink_ref[h], f32) for h in range(B_HEADS)], axis=1)
        scores = []
        m = sk
        for k, _, keep in key_sets:
            s = lax.dot_general(k, q, (((1,), (1,)), ((), ())), preferred_element_type=f32)
            if keep is not None:
                s = jnp.where(keep, s, NEG_INF)
            scores.append(s)
            m = jnp.maximum(m, jnp.max(s, 0, keepdims=True))
        den = jnp.exp(sk - m)
        ot = None
        for s, (_, vts, _) in zip(scores, key_sets):
            e = jnp.exp(s - m)
            den = den + jnp.sum(e, 0, keepdims=True)
            eb = e.astype(bf16)
            for i, v_t in enumerate(vts):
                pv = jnp.dot(v_t, eb[i * CHUNK:(i + 1) * CHUNK, :], preferred_element_type=f32)
                ot = pv if ot is None else ot + pv
        ot = ot * (1.0 / den)
        for pair in range(B_HEADS // 2):
            f0 = (2 * pair // B_GROUP) * B_HD
            pair_t = jnp.concatenate([ot[f0:f0 + B_HD, 2 * pair * rows:(2 * pair + 1) * rows],
                                      ot[f0:f0 + B_HD, (2 * pair + 1) * rows:(2 * pair + 2) * rows]], axis=0)
            cat[r, A_WIDTH + pair * LANES:A_WIDTH + (pair + 1) * LANES] = pair_t.T.astype(bf16)

    if latent:
        span = CHUNK + 2 * WINDOW

        def attend_block(c, carry):
            start = pl.multiple_of(c * CHUNK, CHUNK)
            kr = pl.ds(start, span)
            kj = lax.broadcasted_iota(jnp.int32, (span, B_HEADS * CHUNK), 0)
            qi = lax.broadcasted_iota(jnp.int32, (span, B_HEADS * CHUNK), 1) & (CHUNK - 1)
            kpos = start - WINDOW + kj
            keep = (jnp.abs(kj - WINDOW - qi) <= WINDOW) & (kpos >= 0) & (kpos < n)
            attend(pl.ds(start, CHUNK), CHUNK,
                   [(ks[kr, :], [vt[c + i] for i in range(span // CHUNK)], keep),
                    (kcb[...], [vct[i] for i in range(PAST // CHUNK)], None)])
            return carry

        lax.fori_loop(0, n_chunks, attend_block, 0)
    else:
        def attend_seq(sq, carry):
            r = pl.ds(pl.multiple_of(sq * SEQ, SEQ), SEQ)
            attend(r, SEQ, [(ks[r, :], [vt[sq * (SEQ // CHUNK) + i] for i in range(SEQ // CHUNK)], None)])
            ko_ref[sq] = kf[r, :].T
            vo_ref[sq] = vf[r, :].T
            return carry

        lax.fori_loop(0, n // SEQ, attend_seq, 0)

    def output(c, carry):
        r = pl.ds(pl.multiple_of(c * PROJ_ROWS, PROJ_ROWS), PROJ_ROWS)
        y = jnp.dot(cat[r, :], wo_ref[...], preferred_element_type=f32)
        xo_ref[r, :] = x_ref[r, :] + _mod(mod_ref, GATE1) * y
        return carry

    lax.fori_loop(0, n // PROJ_ROWS, output, 0)


def _l0_mixer(xc, xl, g_norm, mods, p, cache_k, cache_v):
    w = p['w_in']
    q_slots = w[:, L0_Q0:L0_Q0 + B_HEADS * B_HD].reshape(D, B_KV, B_GROUP, B_HD).transpose(0, 2, 1, 3).reshape(
        D, B_GROUP * LANES)
    win = jnp.concatenate([w[:, :L0_Q0], q_slots, w[:, L0_Q0 + B_HEADS * B_HD:]], axis=1).astype(bf16)
    gq = jnp.tile(p['g_q'], B_KV).reshape(1, LANES) * B_SCALE
    gk = jnp.tile(p['g_k'], B_KV).reshape(1, LANES)
    gvn = p['g_vnorm'].reshape(1, A_WIDTH)
    ws = p['w_s'].astype(bf16)
    bsb = jnp.broadcast_to(p['b_s'][:, :, None], (A_GROUPS, CHUNK, CHUNK))
    wo = p['w_o'].astype(bf16)
    weights = (g_norm.reshape(1, D),)
    consts = (win, gvn, ws, bsb, gq, gk, wo)
    c_specs = [_const_spec(a.shape) for a in consts]
    smem = pl.BlockSpec(memory_space=pltpu.SMEM)
    row = pl.BlockSpec((BLOCK_ROWS, D), lambda b: (b, 0))
    kv = pl.BlockSpec((BLOCK_ROWS // SEQ, LANES, SEQ), lambda b: (b, 0, 0))

    def scratch(pad):
        return [pltpu.VMEM((BLOCK_ROWS, L0_IN), f32), pltpu.VMEM((BLOCK_ROWS, D), bf16),
                pltpu.VMEM((BLOCK_ROWS, B_GROUP * LANES), bf16), pltpu.VMEM((BLOCK_ROWS + pad, LANES), bf16),
                pltpu.VMEM(((BLOCK_ROWS + pad) // CHUNK, LANES, CHUNK), bf16)]

    kv_shape = jax.ShapeDtypeStruct((BATCH, LANES, SEQ), f32)
    state = pltpu.VMEM((BLOCK_ROWS, LANES), f32)
    xo_ctx, k_t, v_t = pl.pallas_call(
        functools.partial(_l0_kernel, latent=False),
        grid=(T_CTX // BLOCK_ROWS,),
        in_specs=[smem, row, _const_spec((1, D)), pl.BlockSpec((MOD_ROWS, D), lambda b: (0, 0))] + c_specs,
        out_specs=[row, kv, kv],
        out_shape=[jax.ShapeDtypeStruct((T_CTX, D), f32), kv_shape, kv_shape],
        scratch_shapes=scratch(0) + [state, state],
        compiler_params=_cp(("parallel",)), name="l0_mixer_ctx",
    )(p['sink'], xc, *weights, mods, *consts)
    k_new = k_t.reshape(BATCH, B_KV, B_HD, SEQ).transpose(0, 3, 1, 2)
    v_new = v_t.reshape(BATCH, B_KV, B_HD, SEQ).transpose(0, 3, 1, 2)

    cos, sin = _rope_tables(DEC_SEQ, B_HD, LANES, 0, copies=LANES // B_HD)
    cache = pl.BlockSpec((None, PAST, LANES), lambda b: (b, 0, 0))
    past = [pltpu.VMEM((PAST, LANES), bf16), pltpu.VMEM((PAST // CHUNK, LANES, CHUNK), bf16)]
    xo_lat = pl.pallas_call(
        functools.partial(_l0_kernel, latent=True),
        grid=(DEC_BATCH,),
        in_specs=[smem, row, _const_spec((1, D)), pl.BlockSpec((MOD_ROWS, D), lambda b: (1 + b, 0))] + c_specs + [
                  _const_spec(cos.shape), _const_spec(sin.shape), cache, cache],
        out_specs=row,
        out_shape=jax.ShapeDtypeStruct((T_LAT, D), f32),
        scratch_shapes=scratch(2 * WINDOW) + past,
        compiler_params=_cp(("parallel",)), name="l0_mixer_lat",
    )(p['sink'], xl, *weights, mods, *consts, cos, sin,
      cache_k.reshape(DEC_BATCH, PAST, LANES), cache_v.reshape(DEC_BATCH, PAST, LANES))
    return xo_ctx, xo_lat, k_new, v_new


C_SLOTS = C_HEADS * SLOT
C_PAIRS = C_HEADS // 2
L1_ROWS = 256


def _l1_kernel(*refs, latent):
    if latent:
        (x_ref, gn_ref, mod_ref, win_ref, gqa_ref, wuq_ref, gq_ref, gkva_ref, wuk_ref, wuvt_ref, gk_ref,
         wo_ref, wuqs_ref, qcos_ref, qsin_ref, kcos_ref, ksin_ref, cckv_ref, ckpe_ref, xo_ref,
         zs, cat, qs, ks, vt, wide, wide2) = refs
        n_ctx = PAST
    else:
        (x_ref, gn_ref, mod_ref, win_ref, gqa_ref, wuq_ref, gq_ref, gkva_ref, wuk_ref, wuvt_ref, gk_ref,
         wo_ref, xo_ref, ckvo_ref, kpeo_ref, zs, cat, qs, ks, vt, wide) = refs
        n_ctx = 0
    n = BLOCK_ROWS
    nt_dims = (((1,), (1,)), ((), ()))

    def inv_rms(v):
        return lax.rsqrt(jnp.sum(v * v, -1, keepdims=True) * (1.0 / C_QK) + EPS)

    def expand_keys(ckv_n, kslot, kb, rope_rows):
        cb = ckv_n.astype(bf16)
        key_rows = pl.ds(pl.multiple_of(kb * L1_ROWS, L1_ROWS), L1_ROWS)
        wide[...] = jnp.dot(cb, wuk_ref[...], preferred_element_type=f32)
        if rope_rows is not None:
            kcos = kcos_ref[rope_rows, :]
            turned = _swap_pairs(kslot) * ksin_ref[rope_rows, :]
        for h in range(C_HEADS):
            kh = wide[:, h * SLOT:(h + 1) * SLOT] + kslot
            if rope_rows is not None:
                kh = inv_rms(kh) * (kh * kcos + turned)
            else:
                kh = kh * inv_rms(kh) * gk_ref[...]
            ks[h, key_rows, :] = kh.astype(bf16)
        v_t = lax.dot_general(wuvt_ref[...], cb, nt_dims, preferred_element_type=f32).astype(bf16)
        for pair in range(C_PAIRS):
            vt[pair, kb] = v_t[pair * LANES:(pair + 1) * LANES, :]

    if latent:
        def past_keys(c, carry):
            r = pl.ds(pl.multiple_of(c * L1_ROWS, L1_ROWS), L1_ROWS)
            expand_keys(cckv_ref[r, :], ckpe_ref[r, :], c, None)
            return carry

        lax.fori_loop(0, PAST // L1_ROWS, past_keys, 0)

    def project(c, carry):
        r = pl.ds(pl.multiple_of(c * PROJ_ROWS, PROJ_ROWS), PROJ_ROWS)
        h = _rms_rows(x_ref[r, :], gn_ref[...]) * (1.0 + _mod(mod_ref, SCALE1)) + _mod(mod_ref, SHIFT1)
        zs[r, :] = jnp.dot(h.astype(bf16), win_ref[...], preferred_element_type=f32)
        return carry

    lax.fori_loop(0, n // PROJ_ROWS, project, 0)

    def prepare(c, carry):
        r = pl.ds(pl.multiple_of(c * L1_ROWS, L1_ROWS), L1_ROWS)
        qa = _rms_rows(zs[r, 0:C_Q_LORA], gqa_ref[...]).astype(bf16)
        wide[...] = jnp.dot(qa, wuq_ref[...], preferred_element_type=f32)
        if latent:
            wide2[...] = jnp.dot(qa, wuqs_ref[...], preferred_element_type=f32)
            qcos, qsin = qcos_ref[r, :], qsin_ref[r, :]
        for h in range(C_HEADS):
            hs = slice(h * SLOT, (h + 1) * SLOT)
            qh = wide[:, hs]
            if latent:
                qh = inv_rms(qh) * (qh * qcos + wide2[:, hs] * qsin)
            else:
                qh = qh * inv_rms(qh) * gq_ref[...]
            qs[h, r, :] = qh.astype(bf16)
        ckv_n = _rms_rows(zs[r, C_Q_LORA:C_Q_LORA + C_KV_LORA], gkva_ref[...])
        kslot = zs[r, C_Q_LORA + C_KV_LORA:ODD_IN_PAD]
        if not latent:
            ckvo_ref[r, :] = ckv_n
            kpeo_ref[c] = kslot.T[C_NOPE:C_QK, :]
        expand_keys(ckv_n, kslot, c + n_ctx // L1_ROWS, r if latent else None)
        return carry

    lax.fori_loop(0, n // L1_ROWS, prepare, 0)

    low = lax.broadcasted_iota(jnp.int32, (2 * C_V, L1_ROWS), 0) < C_V
    n_kblocks = (n_ctx + n) // L1_ROWS
    pairs_per_step = 4 if latent else 8

    def attend(c, carry):
        r = pl.ds(pl.multiple_of(c * L1_ROWS, L1_ROWS), L1_ROWS)

        def values_t(pair, eb):
            if not latent:
                return jnp.dot(vt[pair, c], eb, preferred_element_type=f32)
            o_t = None
            for b in range(n_kblocks):
                pv = jnp.dot(vt[pair, b], eb[b * L1_ROWS:(b + 1) * L1_ROWS, :], preferred_element_type=f32)
                o_t = pv if o_t is None else o_t + pv
            return o_t

        def pairs_step(i, carry2):
            pairs = [i * pairs_per_step + j for j in range(pairs_per_step)]
            heads = [2 * p + hh for p in pairs for hh in range(2)]
            scores = [lax.dot_general(ks[h] if latent else ks[h, r, :], qs[h, r, :], nt_dims,
                                      preferred_element_type=f32) for h in heads]
            exps = [jnp.exp(s - jnp.max(s, 0, keepdims=True)) for s in scores]
            dens = [jnp.sum(e, 0, keepdims=True) for e in exps]
            outs = [values_t(h // 2, e.astype(bf16)) / den for h, e, den in zip(heads, exps, dens)]
            for j, pair in enumerate(pairs):
                cat[pair, r, :] = jnp.where(low, outs[2 * j], outs[2 * j + 1]).T.astype(bf16)
            return carry2

        return lax.fori_loop(0, C_PAIRS // pairs_per_step, pairs_step, carry)

    lax.fori_loop(0, n // L1_ROWS, attend, 0)

    def output(c, carry):
        r = pl.ds(pl.multiple_of(c * PROJ_ROWS, PROJ_ROWS), PROJ_ROWS)
        heads = jnp.concatenate([cat[pair, r, :] for pair in range(C_PAIRS)], axis=1)
        y = jnp.dot(heads, wo_ref[...], preferred_element_type=f32)
        xo_ref[r, :] = x_ref[r, :] + _mod(mod_ref, GATE1) * y
        return carry

    lax.fori_loop(0, n // PROJ_ROWS, output, 0)


def _slot_cols(w, heads, width, lo, hi, lane0):
    k = w.shape[0]
    w3 = w.reshape(k, heads, width)[:, :, lo:hi]
    out = jnp.zeros((k, heads, SLOT), w.dtype).at[:, :, lane0:lane0 + (hi - lo)].set(w3)
    return out.reshape(k, heads * SLOT)


def _l1_mixer(xc, xl, g_norm, mods, p, cache_ckv, cache_kpe):
    w_in = jnp.zeros((D, ODD_IN_PAD), f32).at[:, :C_Q_LORA + C_KV_LORA].set(
        p['w_in'][:, :C_Q_LORA + C_KV_LORA]).at[
        :, C_Q_LORA + C_KV_LORA + C_NOPE:C_Q_LORA + C_KV_LORA + C_QK].set(p['w_in'][:, C_Q_LORA + C_KV_LORA:])
    wuq = _slot_cols(p['w_uq'], C_HEADS, C_QK, 0, C_QK, 0).astype(bf16)
    wuk = _slot_cols(p['w_ukv'], C_HEADS, C_NOPE + C_V, 0, C_NOPE, 0).astype(bf16)
    wuv_t = p['w_ukv'].reshape(C_KV_LORA, C_HEADS, C_NOPE + C_V)[:, :, C_NOPE:].reshape(
        C_KV_LORA, C_HEADS * C_V).T.astype(bf16)
    gq = jnp.zeros((1, SLOT), f32).at[0, :C_QK].set(p['g_q'] * C_SCALE)
    gk = jnp.zeros((1, SLOT), f32).at[0, :C_QK].set(p['g_k'])
    consts = (g_norm.reshape(1, D), w_in.astype(bf16), p['g_qa'].reshape(1, C_Q_LORA), wuq, gq,
              p['g_kva'].reshape(1, C_KV_LORA), wuk, wuv_t, gk, p['w_o'].astype(bf16))
    c_specs = [_const_spec(a.shape) for a in consts]
    row = pl.BlockSpec((BLOCK_ROWS, D), lambda b: (b, 0))
    n_ctx_blocks = T_CTX // BLOCK_ROWS

    def scratch(n_keys):
        return [pltpu.VMEM((BLOCK_ROWS, ODD_IN_PAD), f32), pltpu.VMEM((C_PAIRS, BLOCK_ROWS, LANES), bf16),
                pltpu.VMEM((C_HEADS, BLOCK_ROWS, SLOT), bf16), pltpu.VMEM((C_HEADS, n_keys, SLOT), bf16),
                pltpu.VMEM((C_PAIRS, n_keys // L1_ROWS, LANES, L1_ROWS), bf16),
                pltpu.VMEM((L1_ROWS, C_SLOTS), f32)]

    xo_ctx, ckv_new, kpe_t = pl.pallas_call(
        functools.partial(_l1_kernel, latent=False),
        grid=(n_ctx_blocks,),
        in_specs=[row, c_specs[0], pl.BlockSpec((MOD_ROWS, D), lambda b: (0, 0))] + c_specs[1:],
        out_specs=[row, pl.BlockSpec((BLOCK_ROWS, C_KV_LORA), lambda b: (b, 0)),
                   pl.BlockSpec((BLOCK_ROWS // SEQ, C_ROPE, SEQ), lambda b: (b, 0, 0))],
        out_shape=[jax.ShapeDtypeStruct((T_CTX, D), f32), jax.ShapeDtypeStruct((T_CTX, C_KV_LORA), f32),
                   jax.ShapeDtypeStruct((BATCH, C_ROPE, SEQ), f32)],
        scratch_shapes=scratch(BLOCK_ROWS),
        compiler_params=_cp(("parallel",)), name="l1_mixer_ctx",
    )(xc, consts[0], mods, *consts[1:])

    cos, sin = _rope_tables(DEC_SEQ, C_ROPE, SLOT, C_NOPE)
    def pair_swap(a):
        pairs = a.reshape(a.shape[:-1] + (a.shape[-1] // 2, 2))
        return jnp.stack([pairs[..., 1], pairs[..., 0]], axis=-1).reshape(a.shape)

    w_rope = p['w_uq'].reshape(C_Q_LORA, C_HEADS, C_QK)[:, :, C_NOPE:]
    wuq_swapped = jnp.zeros((C_Q_LORA, C_HEADS, SLOT), f32).at[:, :, C_NOPE:C_QK].set(pair_swap(w_rope)).reshape(
        C_Q_LORA, C_SLOTS).astype(bf16)
    rope = (wuq_swapped, gq * cos, pair_swap(gq) * sin, gk * cos, pair_swap(gk) * sin)
    ckpe = jnp.zeros((DEC_BATCH, PAST, SLOT), f32).at[:, :, C_NOPE:C_QK].set(cache_kpe)
    xo_lat = pl.pallas_call(
        functools.partial(_l1_kernel, latent=True),
        grid=(DEC_BATCH,),
        in_specs=[pl.BlockSpec((BLOCK_ROWS, D), lambda b: (b, 0), pipeline_mode=pl.Buffered(1)), c_specs[0],
                  pl.BlockSpec((MOD_ROWS, D), lambda b: (1 + b, 0))] + c_specs[1:] + [_const_spec(a.shape) for a in rope] + [
                  pl.BlockSpec((None, PAST, C_KV_LORA), lambda b: (b, 0, 0)),
                  pl.BlockSpec((None, PAST, SLOT), lambda b: (b, 0, 0))],
        out_specs=row,
        out_shape=jax.ShapeDtypeStruct((T_LAT, D), f32),
        scratch_shapes=scratch(PAST + BLOCK_ROWS) + [pltpu.VMEM((L1_ROWS, C_SLOTS), f32)],
        compiler_params=_cp(("parallel",)), name="l1_mixer_lat",
    )(xl, consts[0], mods, *consts[1:], *rope, cache_ckv, ckpe)
    return xo_ctx, xo_lat, ckv_new, kpe_t.transpose(0, 2, 1)


ROUTER_ROWS = 40
ROUTE_ROWS = 8


def _router_kernel(xc_ref, xl_ref, gn_ref, mod_ref, whi_ref, wlo_ref, br_ref, h_ref, route_ref):
    h = (_rms_rows(_token_rows(xc_ref, xl_ref), gn_ref[...]) * (1.0 + _mod(mod_ref, SCALE2))
         + _mod(mod_ref, SHIFT2))
    _store_token_major(h_ref, h)
    h_hi, h_lo = _split_bf16(h)
    nt = (((1,), (1,)), ((), ()))
    logits = (lax.dot_general(whi_ref[...], h_hi, nt, preferred_element_type=f32)
              + lax.dot_general(whi_ref[...], h_lo, nt, preferred_element_type=f32)
              + lax.dot_general(wlo_ref[...], h_hi, nt, preferred_element_type=f32))
    logits = logits[0:ROUTER_ROWS, :] + br_ref[0:ROUTER_ROWS, :]
    row_i = lax.broadcasted_iota(jnp.int32, logits.shape, 0)
    row = row_i.astype(f32)
    big = 1e6
    is_g = (row_i >= N_EXPERTS) & (row_i < N_EXPERTS + N_GROUPS)
    lg = jnp.where(is_g, logits, -jnp.inf)
    mg = jnp.max(lg, 0, keepdims=True)
    gsel = jnp.min(jnp.where(lg == mg, row, big), 0, keepdims=True) - N_EXPERTS
    pg_sel = 1.0 / jnp.sum(jnp.where(is_g, jnp.exp(lg - mg), 0.0), 0, keepdims=True)
    in_grp = (row_i < N_EXPERTS) & ((row_i >> 3).astype(f32) == gsel)
    le = jnp.where(in_grp, logits, -jnp.inf)
    m1 = jnp.max(le, 0, keepdims=True)
    i1 = jnp.min(jnp.where(le == m1, row, big), 0, keepdims=True)
    le2 = jnp.where(row == i1, -jnp.inf, le)
    m2 = jnp.max(le2, 0, keepdims=True)
    i2 = jnp.min(jnp.where(le2 == m2, row, big), 0, keepdims=True)
    e2 = jnp.exp(m2 - m1)
    w1 = pg_sel / (1.0 + e2)
    w2 = pg_sel * e2 / (1.0 + e2)
    sub = lax.broadcasted_iota(jnp.int32, route_ref.shape, 0)
    route_ref[...] = jnp.where(sub == 0, i1, jnp.where(sub == 1, i2, jnp.where(sub == 2, w1,
                                                                                jnp.where(sub == 3, w2, 0.0))))


def _router(xc, xl, g_norm, mods, p):
    wr = jnp.zeros((LANES, D), f32).at[:N_EXPERTS].set(p['w_re'].T).at[
        N_EXPERTS:N_EXPERTS + N_GROUPS].set(p['w_rg'].T)
    w_hi, w_lo = _split_bf16(wr)
    br = jnp.zeros((LANES, 1), f32).at[:N_EXPERTS, 0].set(p['b_re']).at[
        N_EXPERTS:N_EXPERTS + N_GROUPS, 0].set(p['b_rg'])
    return pl.pallas_call(
        _router_kernel,
        grid=(T // ROW_TILE,),
        in_specs=_token_specs(D) + [
                  _const_spec((1, D)),
                  pl.BlockSpec((MOD_ROWS, D), lambda i: (_sample_of_tile(i, ROW_TILE), 0)),
                  _const_spec((LANES, D)), _const_spec((LANES, D)), _const_spec((LANES, 1))],
        out_specs=[_token_major_spec(ROW_TILE, lambda i: (i, 0)),
                   pl.BlockSpec((ROUTE_ROWS, ROW_TILE), lambda i: (0, i))],
        out_shape=[jax.ShapeDtypeStruct((T * SUBS, LANES), f32), jax.ShapeDtypeStruct((ROUTE_ROWS, T), f32)],
        compiler_params=_cp(("parallel",)), name="router",
    )(xc, xl, g_norm.reshape(1, D), mods, w_hi, w_lo, br)


PLAN_FIRST_TILE, PLAN_TILES = 0, 1


def _plan_kernel(rt_ref, pos_ref, plan_ref, rank):
    n_blk = T // 128
    e_col = lax.broadcasted_iota(jnp.int32, (N_EXPERTS, 128), 0).astype(f32)
    ri = lax.broadcasted_iota(jnp.int32, (128, 128), 0)
    ci = lax.broadcasted_iota(jnp.int32, (128, 128), 1)
    before = jnp.where(ri < ci, 1.0, 0.0).astype(bf16)

    def picks(b):
        cs = slice(b * 128, (b + 1) * 128)
        return rt_ref[0:1, cs] == e_col, rt_ref[1:2, cs] == e_col

    counts = jnp.zeros((N_EXPERTS, 1), f32)
    for b in range(n_blk):
        m0, m1 = picks(b)
        m = jnp.where(m0, 1.0, 0.0) + jnp.where(m1, 1.0, 0.0)
        rank[:, b * 128:(b + 1) * 128] = jnp.dot(m.astype(bf16), before, preferred_element_type=f32) + counts
        counts = counts + jnp.sum(m, axis=1, keepdims=True)

    tiles = jnp.floor((counts + (MOE_TILE - 1.0)) * (1.0 / MOE_TILE))
    er = lax.broadcasted_iota(jnp.int32, (N_EXPERTS, N_EXPERTS), 0)
    ec = lax.broadcasted_iota(jnp.int32, (N_EXPERTS, N_EXPERTS), 1)
    earlier = jnp.where(ec < er, 1.0, 0.0).astype(bf16)
    tile_start = jnp.dot(earlier, jnp.broadcast_to(tiles, (N_EXPERTS, 128)).astype(bf16),
                         preferred_element_type=f32)
    row_start = tile_start * MOE_TILE

    sub = lax.broadcasted_iota(jnp.int32, (8, 128), 0)
    for b in range(n_blk):
        m0, m1 = picks(b)
        base = rank[:, b * 128:(b + 1) * 128] + row_start
        p0 = jnp.sum(jnp.where(m0, base, 0.0), axis=0, keepdims=True)
        p1 = jnp.sum(jnp.where(m1, base, 0.0), axis=0, keepdims=True)
        pos_ref[:, b * 128:(b + 1) * 128] = jnp.where(sub == 0, p0, jnp.where(sub == 1, p1, 0.0)).astype(jnp.int32)

    diag = (lax.broadcasted_iota(jnp.int32, (N_EXPERTS, 128), 0)
            == lax.broadcasted_iota(jnp.int32, (N_EXPERTS, 128), 1))
    first = jnp.sum(jnp.where(diag, tile_start, 0.0), axis=0, keepdims=True)
    count = jnp.sum(jnp.where(diag, tiles, 0.0), axis=0, keepdims=True)
    rows = jnp.where(sub == PLAN_FIRST_TILE, first, jnp.where(sub == PLAN_TILES, count, 0.0))
    plan_ref[...] = rows.astype(jnp.int32)


def _slot_code(t, k):
    return t * SUBS + k * (SUBS // 2)


def _code_offset(code):
    return pl.multiple_of(code & ~(SUBS - 1), SUBS)


def _code_gate_index(code):
    return code >> 2


PAD_CODE = T * SUBS


def _invert_slots(pos_ref, first_ref, count_ref, code_ref):
    def pad_tile(tile, carry):
        for u in range(MOE_TILE):
            code_ref[tile * MOE_TILE + u] = PAD_CODE
        return carry

    def pad_last_tile(e, carry):
        return pad_tile(jnp.maximum(first_ref[e] + count_ref[e] - 1, 0), carry)
    lax.fori_loop(0, N_EXPERTS, pad_last_tile, 0)
    lax.fori_loop(first_ref[N_EXPERTS - 1] + count_ref[N_EXPERTS - 1], MOE_TILES, pad_tile, 0)

    group = 16
    for k in range(2):
        def place(i, carry):
            t0 = i * group
            slots = [pos_ref[k * T + t0 + u] for u in range(group)]
            for u, s in enumerate(slots):
                code_ref[s] = _slot_code(t0 + u, k)
            return carry
        lax.fori_loop(0, T // group, place, 0)


def _route_plan(route_t):
    pos, plan = pl.pallas_call(
        _plan_kernel,
        out_shape=[jax.ShapeDtypeStruct((8, T), jnp.int32), jax.ShapeDtypeStruct((8, LANES), jnp.int32)],
        scratch_shapes=[pltpu.VMEM((N_EXPERTS, T), f32)],
        compiler_params=_cp(None), name="route_plan",
    )(route_t)
    gates = jnp.pad(route_t[2:4].T.reshape(2 * T), (0, 8))
    return plan[PLAN_FIRST_TILE, :N_EXPERTS], plan[PLAN_TILES, :N_EXPERTS], pos[0:2].reshape(2 * T), gates


def _tile_index(i):
    return jnp.minimum(i, MOE_TILES - 1)


TM_ROWS = T * SUBS
SCATTER_GROUP = 8
STAGES = 4
LAST_EXPERT = N_EXPERTS - 1


def _tile_rows(g):
    return pl.ds(pl.multiple_of(g * MOE_TILE, MOE_TILE), MOE_TILE)


def _expert_tile_pairs(first, count, tile_step):
    def pair(pp, carry):
        for parity in range(2):
            g = 2 * pp + parity
            pl.when((g >= first) & (g < first + count))(functools.partial(tile_step, g, parity))
        return carry
    lax.fori_loop(first // 2, (first + count + 1) // 2, pair, 0)


def _gather_tile(code_ref, tile, xs, gbuf):
    base = tile * MOE_TILE
    for r in range(MOE_TILE):
        gbuf[r * SUBS:(r + 1) * SUBS, :] = xs[pl.ds(_code_offset(code_ref[base + r]), SUBS), :]


def _moe_up_kernel(first_ref, count_ref, pos_ref, h_hbm, w1_ref, w3_ref, hh_hbm, code_ref,
                   xs, gbuf_a, gbuf_b, w13, obuf, sem_x, sem_o):
    e = pl.program_id(0)
    first, count = first_ref[e], count_ref[e]
    gbufs = (gbuf_a, gbuf_b)

    def out_copy(slot, g):
        return pltpu.make_async_copy(obuf.at[slot], hh_hbm.at[_tile_rows(g), :], sem_o.at[slot])

    @pl.when(e == 0)
    def _():
        cp = pltpu.make_async_copy(h_hbm, xs.at[pl.ds(0, TM_ROWS), :], sem_x)
        cp.start()
        _invert_slots(pos_ref, first_ref, count_ref, code_ref)
        xs[TM_ROWS:TM_ROWS + SUBS, :] = jnp.zeros((SUBS, LANES), f32)
        cp.wait()
        _gather_tile(code_ref, 0, xs, gbuf_a)

    @pl.when(count > 0)
    def _():
        w13[:, :D_EXPERT] = w1_ref[0].astype(bf16)
        w13[:, D_EXPERT:] = w3_ref[0].astype(bf16)

    def tile_step(g, parity):
        _gather_tile(code_ref, _tile_index(g + 1), xs, gbufs[1 - parity])
        x3 = jnp.swapaxes(gbufs[parity][...].reshape(MOE_TILE, SUBS, LANES), 0, 1)
        x = jnp.concatenate([x3[s] for s in range(SUBS)], axis=1).astype(bf16)
        h13 = jnp.dot(x, w13[...], preferred_element_type=f32)
        hh = (_silu(h13[:, :D_EXPERT]) * h13[:, D_EXPERT:]).astype(bf16)

        slot = g % STAGES

        @pl.when(g >= STAGES)
        def _():
            out_copy(slot, g).wait()
        obuf[slot] = hh
        out_copy(slot, g).start()

    _expert_tile_pairs(first, count, tile_step)

    @pl.when(e == LAST_EXPERT)
    def _():
        n_used = first + count
        for slot in range(STAGES):
            pl.when(n_used > slot)(lambda slot=slot: out_copy(slot, 0).wait())
        obuf[0] = jnp.zeros((MOE_TILE, D_EXPERT), bf16)

        def zero_tile(g, carry):
            cp = out_copy(0, g)
            cp.start()
            cp.wait()
            return carry
        lax.fori_loop(n_used, MOE_TILES, zero_tile, 0)


def _scatter_tile(code_ref, gate_ref, tile, ybuf, acc):
    base = tile * MOE_TILE
    for g0 in range(0, MOE_TILE, SCATTER_GROUP):
        rows = range(g0, g0 + SCATTER_GROUP)
        codes = [code_ref[base + r] for r in rows]
        new = [acc[pl.ds(_code_offset(c), SUBS), :]
               + gate_ref[_code_gate_index(c)] * ybuf[r * SUBS:(r + 1) * SUBS, :]
               for r, c in zip(rows, codes)]
        for c, v in zip(codes, new):
            acc[pl.ds(_code_offset(c), SUBS), :] = v


RES_ROWS = 256


def _residual_out(x_hbm, y_hbm, tok0, sample_of_chunk, mod_ref, acc, rin, rout, sem_r, sem_w):
    n_chunks = x_hbm.shape[0] // RES_ROWS

    def rows(c):
        return pl.ds(pl.multiple_of(c * RES_ROWS, RES_ROWS), RES_ROWS)

    def in_copy(slot, c):
        return pltpu.make_async_copy(x_hbm.at[rows(c), :], rin.at[slot], sem_r.at[slot])

    def out_copy(slot, c):
        return pltpu.make_async_copy(rout.at[slot], y_hbm.at[rows(c), :], sem_w.at[slot])

    for c in range(STAGES - 1):
        in_copy(c, c).start()

    def ring(cc, carry):
        for slot in range(STAGES):
            c = STAGES * cc + slot
            in_copy(slot, c).wait()
            ahead = c + STAGES - 1

            @pl.when(ahead < n_chunks)
            def _():
                in_copy((slot + STAGES - 1) % STAGES, ahead).start()

            @pl.when(c >= STAGES)
            def _():
                out_copy(slot, c).wait()
            delta = _load_token_major(acc, RES_ROWS, tok0 + c * RES_ROWS)
            gate = mod_ref[pl.ds(sample_of_chunk(c) * MOD_ROWS + GATE2, 1), :]
            rout[slot] = rin[slot] + gate * delta
            out_copy(slot, c).start()
        return carry

    lax.fori_loop(0, n_chunks // STAGES, ring, 0)
    for slot in range(STAGES):
        out_copy(slot, 0).wait()


def _moe_down_kernel(first_ref, count_ref, code_ref, gate_ref, hh_hbm, w2_ref, xc_hbm, xl_hbm, mod_ref,
                     yc_hbm, yl_hbm, acc, ybuf_a, ybuf_b, w2b, ibuf, rin, rout, sem_i, sem_r, sem_w):
    e = pl.program_id(0)
    first, count = first_ref[e], count_ref[e]
    n_used = first_ref[LAST_EXPERT] + count_ref[LAST_EXPERT]
    ybufs = (ybuf_a, ybuf_b)

    def in_copy(slot, g):
        return pltpu.make_async_copy(hh_hbm.at[_tile_rows(g), :], ibuf.at[slot], sem_i.at[slot])

    @pl.when(e == 0)
    def _():
        for g in range(STAGES - 1):
            in_copy(g, g).start()

        def zero(c, carry):
            acc[pl.ds(pl.multiple_of(c * 1024, 1024), 1024), :] = jnp.zeros((1024, LANES), f32)
            return carry
        lax.fori_loop(0, TM_ROWS // 1024, zero, 0)
        acc[TM_ROWS:TM_ROWS + SUBS, :] = jnp.zeros((SUBS, LANES), f32)
        ybuf_b[...] = jnp.zeros_like(ybuf_b)

    @pl.when(count > 0)
    def _():
        w2b[...] = w2_ref[0].astype(bf16)

    def tile_step(g, parity):
        slot = g % STAGES
        in_copy(slot, g).wait()
        ahead = g + STAGES - 1

        @pl.when(ahead < n_used)
        def _():
            in_copy(ahead % STAGES, ahead).start()
        _store_token_major(ybufs[parity], jnp.dot(ibuf[slot], w2b[...], preferred_element_type=f32))
        _scatter_tile(code_ref, gate_ref, jnp.maximum(g - 1, 0), ybufs[1 - parity], acc)

    _expert_tile_pairs(first, count, tile_step)

    @pl.when(e == LAST_EXPERT)
    def _():
        for parity in range(2):
            pl.when((n_used > 0) & ((n_used - 1) % 2 == parity))(
                functools.partial(_scatter_tile, code_ref, gate_ref, n_used - 1, ybufs[parity], acc))
        _residual_out(xc_hbm, yc_hbm, 0, lambda c: 0, mod_ref, acc, rin, rout, sem_r, sem_w)
        _residual_out(xl_hbm, yl_hbm, T_CTX, lambda c: 1 + c // (DEC_SEQ // RES_ROWS),
                      mod_ref, acc, rin, rout, sem_r, sem_w)


def _moe(h_tm, route_t, p, xc, xl, mods):
    first_tile, n_tiles, slots, gates = _route_plan(route_t)
    tile_rows = pltpu.VMEM((MOE_TILE * SUBS, LANES), f32)
    staging = pltpu.VMEM((STAGES, MOE_TILE, D_EXPERT), bf16)
    res_rows = pltpu.VMEM((STAGES, RES_ROWS, D), f32)
    hbm = pl.BlockSpec(memory_space=pl.ANY)
    hh, codes = pl.pallas_call(
        _moe_up_kernel,
        grid_spec=pltpu.PrefetchScalarGridSpec(
            num_scalar_prefetch=3, grid=(N_EXPERTS,),
            in_specs=[hbm,
                      pl.BlockSpec((1, D, D_EXPERT), lambda e, f, n, s: (e, 0, 0)),
                      pl.BlockSpec((1, D, D_EXPERT), lambda e, f, n, s: (e, 0, 0))],
            out_specs=[hbm, pl.BlockSpec(memory_space=pltpu.SMEM)],
            scratch_shapes=[pltpu.VMEM((TM_ROWS + SUBS, LANES), f32), tile_rows, tile_rows,
                            pltpu.VMEM((D, 2 * D_EXPERT), bf16), staging,
                            pltpu.SemaphoreType.DMA(()), pltpu.SemaphoreType.DMA((STAGES,))]),
        out_shape=[jax.ShapeDtypeStruct((MOE_ROWS, D_EXPERT), bf16),
                   jax.ShapeDtypeStruct((MOE_ROWS,), jnp.int32)],
        compiler_params=_cp(("arbitrary",)), name="moe_up",
    )(first_tile, n_tiles, slots, h_tm, p['w1'], p['w3'])
    return pl.pallas_call(
        _moe_down_kernel,
        grid_spec=pltpu.PrefetchScalarGridSpec(
            num_scalar_prefetch=4, grid=(N_EXPERTS,),
            in_specs=[hbm, pl.BlockSpec((1, D_EXPERT, D), lambda e, f, n, c, g: (e, 0, 0)), hbm, hbm,
                      pl.BlockSpec((N_SAMPLES * MOD_ROWS, D), lambda e, f, n, c, g: (0, 0),
                                   pipeline_mode=pl.Buffered(1))],
            out_specs=[hbm, hbm],
            scratch_shapes=[pltpu.VMEM((TM_ROWS + SUBS, LANES), f32), tile_rows, tile_rows,
                            pltpu.VMEM((D_EXPERT, D), bf16), staging, res_rows, res_rows,
                            pltpu.SemaphoreType.DMA((STAGES,)), pltpu.SemaphoreType.DMA((STAGES,)),
                            pltpu.SemaphoreType.DMA((STAGES,))]),
        out_shape=[jax.ShapeDtypeStruct((T_CTX, D), f32), jax.ShapeDtypeStruct((T_LAT, D), f32)],
        compiler_params=_cp(("arbitrary",)), name="moe_down",
    )(first_tile, n_tiles, codes, gates, hh, p['w2'], xc, xl, mods)


def kernel(x_prompt, x_sample, cache_l0_k, cache_l0_v, cache_l1_ckv, cache_l1_kpe, c, c_ctx, l0_g_norm1, l0_g_norm2, l0_w_ada, l0_b_ada, l0_w_in, l0_g_vnorm, l0_w_s, l0_b_s, l0_g_q, l0_g_k, l0_sink, l0_w_o, l0_w_rg, l0_b_rg, l0_w_re, l0_b_re, l0_w1, l0_w3, l0_w2, l1_g_norm1, l1_g_norm2, l1_w_ada, l1_b_ada, l1_w_in, l1_g_qa, l1_w_uq, l1_g_kva, l1_w_ukv, l1_g_q, l1_g_k, l1_w_o, l1_w_rg, l1_b_rg, l1_w_re, l1_b_re, l1_w1, l1_w3, l1_w2):
    p0 = dict(w_in=l0_w_in, g_vnorm=l0_g_vnorm, w_s=l0_w_s, b_s=l0_b_s, g_q=l0_g_q, g_k=l0_g_k, sink=l0_sink,
              w_o=l0_w_o, w_rg=l0_w_rg, b_rg=l0_b_rg, w_re=l0_w_re, b_re=l0_b_re, w1=l0_w1, w3=l0_w3, w2=l0_w2)
    p1 = dict(w_in=l1_w_in, g_qa=l1_g_qa, w_uq=l1_w_uq, g_kva=l1_g_kva, w_ukv=l1_w_ukv, g_q=l1_g_q, g_k=l1_g_k,
              w_o=l1_w_o, w_rg=l1_w_rg, b_rg=l1_b_rg, w_re=l1_w_re, b_re=l1_b_re, w1=l1_w1, w3=l1_w3, w2=l1_w2)

    cond8 = jnp.zeros((MOD_ROWS, D), f32).at[0].set(c_ctx).at[1:1 + DEC_BATCH].set(c)
    mods0 = _mod_rows(_adaln(cond8, l0_w_ada, l0_b_ada))
    mods1 = _mod_rows(_adaln(cond8, l1_w_ada, l1_b_ada))

    xc0 = x_prompt.reshape(T_CTX, D)
    xl0 = x_sample.reshape(T_LAT, D)

    xc0m, xl0m, k_new, v_new = _l0_mixer(xc0, xl0, l0_g_norm1, mods0, p0, cache_l0_k, cache_l0_v)
    h0, route0 = _router(xc0m, xl0m, l0_g_norm2, mods0, p0)
    xc1, xl1 = _moe(h0, route0, p0, xc0m, xl0m, mods0)

    xc1m, xl1m, ckv_new, kpe_new = _l1_mixer(xc1, xl1, l1_g_norm1, mods1, p1, cache_l1_ckv, cache_l1_kpe)
    h1, route1 = _router(xc1m, xl1m, l1_g_norm2, mods1, p1)
    y_prompt, y_sample = _moe(h1, route1, p1, xc1m, xl1m, mods1)
    return (y_prompt.reshape(BATCH, SEQ, D), y_sample.reshape(DEC_BATCH, DEC_SEQ, D), k_new, v_new,
            ckv_new.reshape(BATCH, SEQ, C_KV_LORA), kpe_new.reshape(BATCH, SEQ, C_ROPE))
```

```python
import functools

import jax
import jax.numpy as jnp
import numpy as np
from jax import lax
from jax.experimental import pallas as pl
from jax.experimental.pallas import tpu as pltpu

f32 = jnp.float32
bf16 = jnp.bfloat16

D = 1024
BATCH, SEQ = 32, 256
DEC_BATCH, DEC_SEQ = 2, 1024
PAST = 512
T_CTX = BATCH * SEQ
T_LAT = DEC_BATCH * DEC_SEQ
T = T_CTX + T_LAT
GRID_W = 64
CHUNK = 128
WINDOW = 128
ROPE_THETA = 10000.0
EPS = 1e-6
NEG_INF = -1e30
LANES = 128
SUBS = D // LANES

A_WIDTH = 512
A_GROUPS = 4
B_HEADS, B_KV, B_GROUP, B_HD = 8, 2, 4, 64
B_SCALE = B_HD ** -0.5

C_HEADS, C_Q_LORA, C_KV_LORA, C_NOPE, C_ROPE, C_V = 16, 384, 256, 64, 32, 64
C_QK = C_NOPE + C_ROPE
C_SCALE = C_QK ** -0.5
ODD_IN_PAD = 768
SLOT = 128

N_GROUPS, N_EXPERTS, D_EXPERT = 4, 32, 256

N_SAMPLES = 1 + DEC_BATCH
MOD_ROWS = 8
SHIFT1, SCALE1, GATE1, SHIFT2, SCALE2, GATE2 = range(6)

ROW_TILE = 1024
BLOCK_ROWS = 1024
PROJ_ROWS = 512
ADALN_COLS = 1536
MOE_TILE = 256
MOE_ROWS = 2 * T + N_EXPERTS * MOE_TILE
MOE_TILES = MOE_ROWS // MOE_TILE
VMEM_CAP = 56 * 1024 * 1024


def _cp(sem, vmem=VMEM_CAP):
    return pltpu.CompilerParams(dimension_semantics=sem, vmem_limit_bytes=vmem)


def _const_spec(shape):
    nd = len(shape)
    return pl.BlockSpec(shape, lambda *_: (0,) * nd, pipeline_mode=pl.Buffered(1))


def _sample_of_tile(i, tile):
    n_ctx = T_CTX // tile
    per_lat = DEC_SEQ // tile
    return jnp.where(i < n_ctx, 0, 1 + (i - n_ctx) // per_lat)


def _mod(mod_ref, row):
    return mod_ref[row:row + 1, :]


def _silu(x):
    return x * jax.nn.sigmoid(x)


def _rms_rows(x, g):
    return x * lax.rsqrt(jnp.mean(x * x, -1, keepdims=True) + EPS) * g


def _swap_pairs(x):
    lane = lax.broadcasted_iota(jnp.int32, x.shape, x.ndim - 1)
    nxt = pltpu.roll(x, x.shape[-1] - 1, x.ndim - 1)
    prv = pltpu.roll(x, 1, x.ndim - 1)
    return jnp.where((lane & 1) == 0, nxt, prv)


def _split_bf16(x):
    hi = x.astype(bf16)
    return hi, (x - hi.astype(f32)).astype(bf16)


def _adaln_kernel(c_ref, w_ref, b_ref, o_ref):
    s_hi, s_lo = _split_bf16(_silu(c_ref[...]))
    w_hi, w_lo = _split_bf16(w_ref[...])
    o_ref[...] = (jnp.dot(s_hi, w_hi, preferred_element_type=f32) + jnp.dot(s_lo, w_hi, preferred_element_type=f32)
                  + jnp.dot(s_hi, w_lo, preferred_element_type=f32) + b_ref[...])


def _adaln(cond8, w, b):
    n = w.shape[1]
    return pl.pallas_call(
        _adaln_kernel,
        grid=(n // ADALN_COLS,),
        in_specs=[_const_spec((MOD_ROWS, D)), pl.BlockSpec((D, ADALN_COLS), lambda j: (0, j)),
                  pl.BlockSpec((1, ADALN_COLS), lambda j: (0, j))],
        out_specs=pl.BlockSpec((MOD_ROWS, ADALN_COLS), lambda j: (0, j)),
        out_shape=jax.ShapeDtypeStruct((MOD_ROWS, n), f32),
        compiler_params=_cp(("arbitrary",)),
        name="adaln",
    )(cond8, w, b.reshape(1, n))


def _mod_rows(m8):
    m = m8[:N_SAMPLES].reshape(N_SAMPLES, 6, D)
    return jnp.pad(m, ((0, 0), (0, MOD_ROWS - 6), (0, 0))).reshape(N_SAMPLES * MOD_ROWS, D)


N_CTX_TILES = T_CTX // ROW_TILE


def _token_specs(width):
    return [pl.BlockSpec((ROW_TILE, width), lambda i: (jnp.minimum(i, N_CTX_TILES - 1), 0)),
            pl.BlockSpec((ROW_TILE, width), lambda i: (jnp.maximum(i - N_CTX_TILES, 0), 0))]


def _token_rows(xc_ref, xl_ref):
    return jnp.where(pl.program_id(0) < N_CTX_TILES, xc_ref[...], xl_ref[...])


def _store_token_major(ref, x):
    n = x.shape[0]
    for s in range(SUBS):
        ref[pl.ds(s, n, stride=SUBS), :] = x[:, s * LANES:(s + 1) * LANES]


def _load_token_major(ref, n, row0=0):
    return jnp.concatenate([ref[pl.ds(row0 * SUBS + s, n, stride=SUBS), :] for s in range(SUBS)], axis=1)


def _token_major_spec(rows, index_map):
    return pl.BlockSpec((rows * SUBS, LANES), index_map)


def _rope_tables(n, rot_dim, lanes, lane0, copies=1):
    rows_count = n // GRID_W
    rows = np.repeat(np.arange(rows_count), GRID_W).astype(np.float64)
    cols = np.tile(np.arange(GRID_W), rows_count).astype(np.float64)
    d_axis = rot_dim // 2
    inv = ROPE_THETA ** (-np.arange(0, d_axis, 2, dtype=np.float64) / d_axis)
    ang = np.concatenate([rows[:, None] * inv, cols[:, None] * inv], -1)
    c = np.ones((n, lanes), np.float32)
    s = np.zeros((n, lanes), np.float32)
    for j in range(copies):
        lo = lane0 + j * rot_dim
        c[:, lo:lo + rot_dim] = np.repeat(np.cos(ang), 2, axis=1)
        s[:, lo:lo + rot_dim] = np.repeat(np.sin(ang), 2, axis=1) * np.tile(np.array([-1.0, 1.0]), rot_dim // 2)
    return jnp.asarray(c), jnp.asarray(s)


L0_Q0 = 2 * A_WIDTH
L0_K0 = L0_Q0 + B_GROUP * LANES
L0_V0 = L0_K0 + B_KV * B_HD
L0_IN = L0_V0 + B_KV * B_HD


def _l0_kernel(*refs, latent):
    if latent:
        (sink_ref, x_ref, gn_ref, mod_ref, win_ref, gvn_ref, ws_ref, bsb_ref, gq_ref, gk_ref, wo_ref,
         cos_ref, sin_ref, kc_ref, vc_ref, xo_ref, zs, cat, qs, ks, vt, kcb, vct) = refs
        key_off = WINDOW
    else:
        (sink_ref, x_ref, gn_ref, mod_ref, win_ref, gvn_ref, ws_ref, bsb_ref, gq_ref, gk_ref, wo_ref,
         xo_ref, ko_ref, vo_ref, zs, cat, qs, ks, vt, kf, vf) = refs
        key_off = 0
    n = BLOCK_ROWS
    n_chunks = n // CHUNK
    low = lax.broadcasted_iota(jnp.int32, (CHUNK, LANES), 1) < B_HD

    if latent:
        zpad = jnp.zeros((WINDOW, LANES), bf16)
        for c0 in (0, 1 + n_chunks):
            ks[c0 * CHUNK:(c0 + 1) * CHUNK, :] = zpad
            vt[c0] = zpad
        kcb[...] = kc_ref[...].astype(bf16)
        for i in range(PAST // CHUNK):
            vct[i] = vc_ref[i * CHUNK:(i + 1) * CHUNK, :].T.astype(bf16)

    def project(c, carry):
        r = pl.ds(pl.multiple_of(c * PROJ_ROWS, PROJ_ROWS), PROJ_ROWS)
        h = _rms_rows(x_ref[r, :], gn_ref[...]) * (1.0 + _mod(mod_ref, SCALE1)) + _mod(mod_ref, SHIFT1)
        zs[r, :] = jnp.dot(h.astype(bf16), win_ref[...], preferred_element_type=f32)
        return carry

    lax.fori_loop(0, n // PROJ_ROWS, project, 0)

    def prepare(c, carry):
        r = pl.ds(pl.multiple_of(c * CHUNK, CHUNK), CHUNK)
        u = jax.nn.gelu(zs[r, 0:A_WIDTH])
        v = jax.nn.gelu(zs[r, A_WIDTH:2 * A_WIDTH])
        mu = jnp.mean(v, -1, keepdims=True)
        var = jnp.mean(jnp.square(v - mu), -1, keepdims=True)
        vn = ((v - mu) * lax.rsqrt(var + EPS) * gvn_ref[...]).astype(bf16)
        for g in range(A_GROUPS):
            cs = slice(g * CHUNK, (g + 1) * CHUNK)
            mixed = jnp.dot(ws_ref[g], vn[:, cs], preferred_element_type=f32) + bsb_ref[g]
            cat[r, cs] = (u[:, cs] * mixed).astype(bf16)
        if latent:
            cs_, sn_ = cos_ref[r, :], sin_ref[r, :]
        def half_norm(v, gain):
            sq = v * v
            s0 = jnp.sum(jnp.where(low, sq, 0.0), -1, keepdims=True)
            s1 = jnp.sum(jnp.where(low, 0.0, sq), -1, keepdims=True)
            return v * lax.rsqrt(jnp.where(low, s0, s1) * (1.0 / B_HD) + EPS) * gain

        for j in range(B_GROUP):
            js = slice(j * LANES, (j + 1) * LANES)
            qj = half_norm(zs[r, L0_Q0 + j * LANES:L0_Q0 + (j + 1) * LANES], gq_ref[...])
            if latent:
                qj = qj * cs_ + _swap_pairs(qj) * sn_
            qs[r, js] = qj.astype(bf16)
        k = half_norm(zs[r, L0_K0:L0_K0 + LANES], gk_ref[...])
        vv = zs[r, L0_V0:L0_V0 + LANES]
        if latent:
            k = k * cs_ + _swap_pairs(k) * sn_
        else:
            kf[r, :] = k
            vf[r, :] = vv
        kr = pl.ds(pl.multiple_of(c * CHUNK + key_off, CHUNK), CHUNK)
        ks[kr, :] = k.astype(bf16)
        vt[c + key_off // CHUNK] = vv.T.astype(bf16)
        return carry

    lax.fori_loop(0, n_chunks, prepare, 0)

    def attend(jobs):
        scored = []
        for r, rows, key_sets in jobs:
            low_q = lax.broadcasted_iota(jnp.int32, (rows, LANES), 1) < B_HD
            slots = [qs[r, j * LANES:(j + 1) * LANES] for j in range(B_GROUP)]
            zero = jnp.zeros((rows, LANES), bf16)
            q = jnp.concatenate([jnp.where(low_q, qj, zero) for qj in slots]
                                + [jnp.where(low_q, zero, qj) for qj in slots], axis=0)
            sk = jnp.concatenate([jnp.full((1, rows), sink_ref[h], f32) for h in range(B_HEADS)], axis=1)
            scores = []
            m = sk
            for k, _, keep in key_sets:
                s = lax.dot_general(k, q, (((1,), (1,)), ((), ())), preferred_element_type=f32)
                if keep is not None:
                    s = jnp.where(keep, s, NEG_INF)
                scores.append(s)
                m = jnp.maximum(m, jnp.max(s, 0, keepdims=True))
            scored.append((sk, scores, m))
        outs = []
        for (r, rows, key_sets), (sk, scores, m) in zip(jobs, scored):
            den = jnp.exp(sk - m)
            ot = None
            for s, (_, vts, _) in zip(scores, key_sets):
                e = jnp.exp(s - m)
                den = den + jnp.sum(e, 0, keepdims=True)
                eb = e.astype(bf16)
                for i, v_t in enumerate(vts):
                    pv = jnp.dot(v_t, eb[i * CHUNK:(i + 1) * CHUNK, :], preferred_element_type=f32)
                    ot = pv if ot is None else ot + pv
            outs.append(ot * (1.0 / den))
        for (r, rows, _), ot in zip(jobs, outs):
            for pair in range(B_HEADS // 2):
                f0 = (2 * pair // B_GROUP) * B_HD
                pair_t = jnp.concatenate([ot[f0:f0 + B_HD, 2 * pair * rows:(2 * pair + 1) * rows],
                                          ot[f0:f0 + B_HD, (2 * pair + 1) * rows:(2 * pair + 2) * rows]], axis=0)
                cat[r, A_WIDTH + pair * LANES:A_WIDTH + (pair + 1) * LANES] = pair_t.T.astype(bf16)

    jobs_per_step = 2 if latent else 4
    if latent:
        span = CHUNK + 2 * WINDOW

        def block_job(c):
            start = pl.multiple_of(c * CHUNK, CHUNK)
            kr = pl.ds(start, span)
            kj = lax.broadcasted_iota(jnp.int32, (span, B_HEADS * CHUNK), 0)
            qi = lax.broadcasted_iota(jnp.int32, (span, B_HEADS * CHUNK), 1) & (CHUNK - 1)
            kpos = start - WINDOW + kj
            keep = (jnp.abs(kj - WINDOW - qi) <= WINDOW) & (kpos >= 0) & (kpos < n)
            return (pl.ds(start, CHUNK), CHUNK,
                    [(ks[kr, :], [vt[c + i] for i in range(span // CHUNK)], keep),
                     (kcb[...], [vct[i] for i in range(PAST // CHUNK)], None)])

        def attend_blocks(i, carry):
            attend([block_job(i * jobs_per_step + j) for j in range(jobs_per_step)])
            return carry

        lax.fori_loop(0, n_chunks // jobs_per_step, attend_blocks, 0)
    else:
        def seq_job(sq):
            r = pl.ds(pl.multiple_of(sq * SEQ, SEQ), SEQ)
            return r, SEQ, [(ks[r, :], [vt[sq * (SEQ // CHUNK) + i] for i in range(SEQ // CHUNK)], None)]

        def attend_seqs(i, carry):
            seqs = [i * jobs_per_step + j for j in range(jobs_per_step)]
            attend([seq_job(sq) for sq in seqs])
            for sq in seqs:
                r = pl.ds(pl.multiple_of(sq * SEQ, SEQ), SEQ)
                ko_ref[sq] = kf[r, :].T
                vo_ref[sq] = vf[r, :].T
            return carry

        lax.fori_loop(0, n // SEQ // jobs_per_step, attend_seqs, 0)

    def output(c, carry):
        r = pl.ds(pl.multiple_of(c * PROJ_ROWS, PROJ_ROWS), PROJ_ROWS)
        y = jnp.dot(cat[r, :], wo_ref[...], preferred_element_type=f32)
        xo_ref[r, :] = x_ref[r, :] + _mod(mod_ref, GATE1) * y
        return carry

    lax.fori_loop(0, n // PROJ_ROWS, output, 0)


def _l0_mixer(xc, xl, g_norm, mods, p, cache_k, cache_v):
    w = p['w_in']
    q_slots = w[:, L0_Q0:L0_Q0 + B_HEADS * B_HD].reshape(D, B_KV, B_GROUP, B_HD).transpose(0, 2, 1, 3).reshape(
        D, B_GROUP * LANES)
    win = jnp.concatenate([w[:, :L0_Q0], q_slots, w[:, L0_Q0 + B_HEADS * B_HD:]], axis=1).astype(bf16)
    gq = jnp.tile(p['g_q'], B_KV).reshape(1, LANES) * B_SCALE
    gk = jnp.tile(p['g_k'], B_KV).reshape(1, LANES)
    gvn = p['g_vnorm'].reshape(1, A_WIDTH)
    ws = p['w_s'].astype(bf16)
    bsb = jnp.broadcast_to(p['b_s'][:, :, None], (A_GROUPS, CHUNK, CHUNK))
    wo = p['w_o'].astype(bf16)
    weights = (g_norm.reshape(1, D),)
    consts = (win, gvn, ws, bsb, gq, gk, wo)
    c_specs = [_const_spec(a.shape) for a in consts]
    smem = pl.BlockSpec(memory_space=pltpu.SMEM)
    row = pl.BlockSpec((BLOCK_ROWS, D), lambda b: (b, 0))
    kv = pl.BlockSpec((BLOCK_ROWS // SEQ, LANES, SEQ), lambda b: (b, 0, 0))

    def scratch(pad):
        return [pltpu.VMEM((BLOCK_ROWS, L0_IN), f32), pltpu.VMEM((BLOCK_ROWS, D), bf16),
                pltpu.VMEM((BLOCK_ROWS, B_GROUP * LANES), bf16), pltpu.VMEM((BLOCK_ROWS + pad, LANES), bf16),
                pltpu.VMEM(((BLOCK_ROWS + pad) // CHUNK, LANES, CHUNK), bf16)]

    kv_shape = jax.ShapeDtypeStruct((BATCH, LANES, SEQ), f32)
    state = pltpu.VMEM((BLOCK_ROWS, LANES), f32)
    xo_ctx, k_t, v_t = pl.pallas_call(
        functools.partial(_l0_kernel, latent=False),
        grid=(T_CTX // BLOCK_ROWS,),
        in_specs=[smem, row, _const_spec((1, D)), pl.BlockSpec((MOD_ROWS, D), lambda b: (0, 0))] + c_specs,
        out_specs=[row, kv, kv],
        out_shape=[jax.ShapeDtypeStruct((T_CTX, D), f32), kv_shape, kv_shape],
        scratch_shapes=scratch(0) + [state, state],
        compiler_params=_cp(("parallel",)), name="l0_mixer_ctx",
    )(p['sink'], xc, *weights, mods, *consts)
    k_new = k_t.reshape(BATCH, B_KV, B_HD, SEQ).transpose(0, 3, 1, 2)
    v_new = v_t.reshape(BATCH, B_KV, B_HD, SEQ).transpose(0, 3, 1, 2)

    cos, sin = _rope_tables(DEC_SEQ, B_HD, LANES, 0, copies=LANES // B_HD)
    cache = pl.BlockSpec((None, PAST, LANES), lambda b: (b, 0, 0))
    past = [pltpu.VMEM((PAST, LANES), bf16), pltpu.VMEM((PAST // CHUNK, LANES, CHUNK), bf16)]
    xo_lat = pl.pallas_call(
        functools.partial(_l0_kernel, latent=True),
        grid=(DEC_BATCH,),
        in_specs=[smem, row, _const_spec((1, D)), pl.BlockSpec((MOD_ROWS, D), lambda b: (1 + b, 0))] + c_specs + [
                  _const_spec(cos.shape), _const_spec(sin.shape), cache, cache],
        out_specs=row,
        out_shape=jax.ShapeDtypeStruct((T_LAT, D), f32),
        scratch_shapes=scratch(2 * WINDOW) + past,
        compiler_params=_cp(("parallel",)), name="l0_mixer_lat",
    )(p['sink'], xl, *weights, mods, *consts, cos, sin,
      cache_k.reshape(DEC_BATCH, PAST, LANES), cache_v.reshape(DEC_BATCH, PAST, LANES))
    return xo_ctx, xo_lat, k_new, v_new


C_SLOTS = C_HEADS * SLOT
C_PAIRS = C_HEADS // 2
L1_ROWS = 256


def _l1_kernel(*refs, latent):
    if latent:
        (x_ref, gn_ref, mod_ref, win_ref, gqa_ref, wuq_ref, gq_ref, gkva_ref, wuk_ref, wuvt_ref, gk_ref,
         wo_ref, wuqs_ref, qcos_ref, qsin_ref, kcos_ref, ksin_ref, cckv_ref, ckpe_ref, xo_ref,
         zs, cat, qs, ks, vt, wide, wide2) = refs
        n_ctx = PAST
    else:
        (x_ref, gn_ref, mod_ref, win_ref, gqa_ref, wuq_ref, gq_ref, gkva_ref, wuk_ref, wuvt_ref, gk_ref,
         wo_ref, xo_ref, ckvo_ref, kpeo_ref, zs, cat, qs, ks, vt, wide) = refs
        n_ctx = 0
    n = BLOCK_ROWS
    nt_dims = (((1,), (1,)), ((), ()))

    def inv_rms(v):
        return lax.rsqrt(jnp.sum(v * v, -1, keepdims=True) * (1.0 / C_QK) + EPS)

    def expand_keys(ckv_n, kslot, kb, rope_rows):
        cb = ckv_n.astype(bf16)
        key_rows = pl.ds(pl.multiple_of(kb * L1_ROWS, L1_ROWS), L1_ROWS)
        wide[...] = jnp.dot(cb, wuk_ref[...], preferred_element_type=f32)
        if rope_rows is not None:
            kcos = kcos_ref[rope_rows, :]
            turned = _swap_pairs(kslot) * ksin_ref[rope_rows, :]
        for h in range(C_HEADS):
            kh = wide[:, h * SLOT:(h + 1) * SLOT] + kslot
            if rope_rows is not None:
                kh = inv_rms(kh) * (kh * kcos + turned)
            else:
                kh = kh * inv_rms(kh) * gk_ref[...]
            ks[h, key_rows, :] = kh.astype(bf16)
        v_t = lax.dot_general(wuvt_ref[...], cb, nt_dims, preferred_element_type=f32).astype(bf16)
        for pair in range(C_PAIRS):
            vt[pair, kb] = v_t[pair * LANES:(pair + 1) * LANES, :]

    if latent:
        def past_keys(c, carry):
            r = pl.ds(pl.multiple_of(c * L1_ROWS, L1_ROWS), L1_ROWS)
            expand_keys(cckv_ref[r, :], ckpe_ref[r, :], c, None)
            return carry

        lax.fori_loop(0, PAST // L1_ROWS, past_keys, 0)

    def project(c, carry):
        r = pl.ds(pl.multiple_of(c * PROJ_ROWS, PROJ_ROWS), PROJ_ROWS)
        h = _rms_rows(x_ref[r, :], gn_ref[...]) * (1.0 + _mod(mod_ref, SCALE1)) + _mod(mod_ref, SHIFT1)
        zs[r, :] = jnp.dot(h.astype(bf16), win_ref[...], preferred_element_type=f32)
        return carry

    lax.fori_loop(0, n // PROJ_ROWS, project, 0)

    def prepare(c, carry):
        r = pl.ds(pl.multiple_of(c * L1_ROWS, L1_ROWS), L1_ROWS)
        qa = _rms_rows(zs[r, 0:C_Q_LORA], gqa_ref[...]).astype(bf16)
        wide[...] = jnp.dot(qa, wuq_ref[...], preferred_element_type=f32)
        if latent:
            wide2[...] = jnp.dot(qa, wuqs_ref[...], preferred_element_type=f32)
            qcos, qsin = qcos_ref[r, :], qsin_ref[r, :]
        for h in range(C_HEADS):
            hs = slice(h * SLOT, (h + 1) * SLOT)
            qh = wide[:, hs]
            if latent:
                qh = inv_rms(qh) * (qh * qcos + wide2[:, hs] * qsin)
            else:
                qh = qh * inv_rms(qh) * gq_ref[...]
            qs[h, r, :] = qh.astype(bf16)
        ckv_n = _rms_rows(zs[r, C_Q_LORA:C_Q_LORA + C_KV_LORA], gkva_ref[...])
        kslot = zs[r, C_Q_LORA + C_KV_LORA:ODD_IN_PAD]
        if not latent:
            ckvo_ref[r, :] = ckv_n
            kpeo_ref[c] = kslot.T[C_NOPE:C_QK, :]
        expand_keys(ckv_n, kslot, c + n_ctx // L1_ROWS, r if latent else None)
        return carry

    lax.fori_loop(0, n // L1_ROWS, prepare, 0)

    low = lax.broadcasted_iota(jnp.int32, (2 * C_V, L1_ROWS), 0) < C_V
    n_kblocks = (n_ctx + n) // L1_ROWS
    pairs_per_step = 4 if latent else 8

    def attend(c, carry):
        r = pl.ds(pl.multiple_of(c * L1_ROWS, L1_ROWS), L1_ROWS)

        def values_t(pair, eb):
            if not latent:
                return jnp.dot(vt[pair, c], eb, preferred_element_type=f32)
            o_t = None
            for b in range(n_kblocks):
                pv = jnp.dot(vt[pair, b], eb[b * L1_ROWS:(b + 1) * L1_ROWS, :], preferred_element_type=f32)
                o_t = pv if o_t is None else o_t + pv
            return o_t

        def pairs_step(i, carry2):
            pairs = [i * pairs_per_step + j for j in range(pairs_per_step)]
            heads = [2 * p + hh for p in pairs for hh in range(2)]
            scores = [lax.dot_general(ks[h] if latent else ks[h, r, :], qs[h, r, :], nt_dims,
                                      preferred_element_type=f32) for h in heads]
            exps = [jnp.exp(s - jnp.max(s, 0, keepdims=True)) for s in scores]
            dens = [jnp.sum(e, 0, keepdims=True) for e in exps]
            outs = [values_t(h // 2, e.astype(bf16)) / den for h, e, den in zip(heads, exps, dens)]
            for j, pair in enumerate(pairs):
                cat[pair, r, :] = jnp.where(low, outs[2 * j], outs[2 * j + 1]).T.astype(bf16)
            return carry2

        return lax.fori_loop(0, C_PAIRS // pairs_per_step, pairs_step, carry)

    lax.fori_loop(0, n // L1_ROWS, attend, 0)

    def output(c, carry):
        r = pl.ds(pl.multiple_of(c * PROJ_ROWS, PROJ_ROWS), PROJ_ROWS)
        heads = jnp.concatenate([cat[pair, r, :] for pair in range(C_PAIRS)], axis=1)
        y = jnp.dot(heads, wo_ref[...], preferred_element_type=f32)
        xo_ref[r, :] = x_ref[r, :] + _mod(mod_ref, GATE1) * y
        return carry

    lax.fori_loop(0, n // PROJ_ROWS, output, 0)


def _slot_cols(w, heads, width, lo, hi, lane0):
    k = w.shape[0]
    w3 = w.reshape(k, heads, width)[:, :, lo:hi]
    out = jnp.zeros((k, heads, SLOT), w.dtype).at[:, :, lane0:lane0 + (hi - lo)].set(w3)
    return out.reshape(k, heads * SLOT)


def _l1_mixer(xc, xl, g_norm, mods, p, cache_ckv, cache_kpe):
    w_in = jnp.zeros((D, ODD_IN_PAD), f32).at[:, :C_Q_LORA + C_KV_LORA].set(
        p['w_in'][:, :C_Q_LORA + C_KV_LORA]).at[
        :, C_Q_LORA + C_KV_LORA + C_NOPE:C_Q_LORA + C_KV_LORA + C_QK].set(p['w_in'][:, C_Q_LORA + C_KV_LORA:])
    wuq = _slot_cols(p['w_uq'], C_HEADS, C_QK, 0, C_QK, 0).astype(bf16)
    wuk = _slot_cols(p['w_ukv'], C_HEADS, C_NOPE + C_V, 0, C_NOPE, 0).astype(bf16)
    wuv_t = p['w_ukv'].reshape(C_KV_LORA, C_HEADS, C_NOPE + C_V)[:, :, C_NOPE:].reshape(
        C_KV_LORA, C_HEADS * C_V).T.astype(bf16)
    gq = jnp.zeros((1, SLOT), f32).at[0, :C_QK].set(p['g_q'] * C_SCALE)
    gk = jnp.zeros((1, SLOT), f32).at[0, :C_QK].set(p['g_k'])
    consts = (g_norm.reshape(1, D), w_in.astype(bf16), p['g_qa'].reshape(1, C_Q_LORA), wuq, gq,
              p['g_kva'].reshape(1, C_KV_LORA), wuk, wuv_t, gk, p['w_o'].astype(bf16))
    c_specs = [_const_spec(a.shape) for a in consts]
    row = pl.BlockSpec((BLOCK_ROWS, D), lambda b: (b, 0))
    n_ctx_blocks = T_CTX // BLOCK_ROWS

    def scratch(n_keys):
        return [pltpu.VMEM((BLOCK_ROWS, ODD_IN_PAD), f32), pltpu.VMEM((C_PAIRS, BLOCK_ROWS, LANES), bf16),
                pltpu.VMEM((C_HEADS, BLOCK_ROWS, SLOT), bf16), pltpu.VMEM((C_HEADS, n_keys, SLOT), bf16),
                pltpu.VMEM((C_PAIRS, n_keys // L1_ROWS, LANES, L1_ROWS), bf16),
                pltpu.VMEM((L1_ROWS, C_SLOTS), f32)]

    xo_ctx, ckv_new, kpe_t = pl.pallas_call(
        functools.partial(_l1_kernel, latent=False),
        grid=(n_ctx_blocks,),
        in_specs=[row, c_specs[0], pl.BlockSpec((MOD_ROWS, D), lambda b: (0, 0))] + c_specs[1:],
        out_specs=[row, pl.BlockSpec((BLOCK_ROWS, C_KV_LORA), lambda b: (b, 0)),
                   pl.BlockSpec((BLOCK_ROWS // SEQ, C_ROPE, SEQ), lambda b: (b, 0, 0))],
        out_shape=[jax.ShapeDtypeStruct((T_CTX, D), f32), jax.ShapeDtypeStruct((T_CTX, C_KV_LORA), f32),
                   jax.ShapeDtypeStruct((BATCH, C_ROPE, SEQ), f32)],
        scratch_shapes=scratch(BLOCK_ROWS),
        compiler_params=_cp(("parallel",)), name="l1_mixer_ctx",
    )(xc, consts[0], mods, *consts[1:])

    cos, sin = _rope_tables(DEC_SEQ, C_ROPE, SLOT, C_NOPE)
    def pair_swap(a):
        pairs = a.reshape(a.shape[:-1] + (a.shape[-1] // 2, 2))
        return jnp.stack([pairs[..., 1], pairs[..., 0]], axis=-1).reshape(a.shape)

    w_rope = p['w_uq'].reshape(C_Q_LORA, C_HEADS, C_QK)[:, :, C_NOPE:]
    wuq_swapped = jnp.zeros((C_Q_LORA, C_HEADS, SLOT), f32).at[:, :, C_NOPE:C_QK].set(pair_swap(w_rope)).reshape(
        C_Q_LORA, C_SLOTS).astype(bf16)
    rope = (wuq_swapped, gq * cos, pair_swap(gq) * sin, gk * cos, pair_swap(gk) * sin)
    ckpe = jnp.zeros((DEC_BATCH, PAST, SLOT), f32).at[:, :, C_NOPE:C_QK].set(cache_kpe)
    xo_lat = pl.pallas_call(
        functools.partial(_l1_kernel, latent=True),
        grid=(DEC_BATCH,),
        in_specs=[pl.BlockSpec((BLOCK_ROWS, D), lambda b: (b, 0), pipeline_mode=pl.Buffered(1)), c_specs[0],
                  pl.BlockSpec((MOD_ROWS, D), lambda b: (1 + b, 0))] + c_specs[1:] + [_const_spec(a.shape) for a in rope] + [
                  pl.BlockSpec((None, PAST, C_KV_LORA), lambda b: (b, 0, 0)),
                  pl.BlockSpec((None, PAST, SLOT), lambda b: (b, 0, 0))],
        out_specs=row,
        out_shape=jax.ShapeDtypeStruct((T_LAT, D), f32),
        scratch_shapes=scratch(PAST + BLOCK_ROWS) + [pltpu.VMEM((L1_ROWS, C_SLOTS), f32)],
        compiler_params=_cp(("parallel",)), name="l1_mixer_lat",
    )(xl, consts[0], mods, *consts[1:], *rope, cache_ckv, ckpe)
    return xo_ctx, xo_lat, ckv_new, kpe_t.transpose(0, 2, 1)


ROUTER_ROWS = 40
ROUTE_ROWS = 8


def _router_kernel(xc_ref, xl_ref, gn_ref, mod_ref, whi_ref, wlo_ref, br_ref, h_ref, route_ref):
    h = (_rms_rows(_token_rows(xc_ref, xl_ref), gn_ref[...]) * (1.0 + _mod(mod_ref, SCALE2))
         + _mod(mod_ref, SHIFT2))
    _store_token_major(h_ref, h)
    h_hi, h_lo = _split_bf16(h)
    nt = (((1,), (1,)), ((), ()))
    logits = (lax.dot_general(whi_ref[...], h_hi, nt, preferred_element_type=f32)
              + lax.dot_general(whi_ref[...], h_lo, nt, preferred_element_type=f32)
              + lax.dot_general(wlo_ref[...], h_hi, nt, preferred_element_type=f32))
    logits = logits[0:ROUTER_ROWS, :] + br_ref[0:ROUTER_ROWS, :]
    row_i = lax.broadcasted_iota(jnp.int32, logits.shape, 0)
    row = row_i.astype(f32)
    big = 1e6
    is_g = (row_i >= N_EXPERTS) & (row_i < N_EXPERTS + N_GROUPS)
    lg = jnp.where(is_g, logits, -jnp.inf)
    mg = jnp.max(lg, 0, keepdims=True)
    gsel = jnp.min(jnp.where(lg == mg, row, big), 0, keepdims=True) - N_EXPERTS
    pg_sel = 1.0 / jnp.sum(jnp.where(is_g, jnp.exp(lg - mg), 0.0), 0, keepdims=True)
    in_grp = (row_i < N_EXPERTS) & ((row_i >> 3).astype(f32) == gsel)
    le = jnp.where(in_grp, logits, -jnp.inf)
    m1 = jnp.max(le, 0, keepdims=True)
    i1 = jnp.min(jnp.where(le == m1, row, big), 0, keepdims=True)
    le2 = jnp.where(row == i1, -jnp.inf, le)
    m2 = jnp.max(le2, 0, keepdims=True)
    i2 = jnp.min(jnp.where(le2 == m2, row, big), 0, keepdims=True)
    e2 = jnp.exp(m2 - m1)
    w1 = pg_sel / (1.0 + e2)
    w2 = pg_sel * e2 / (1.0 + e2)
    sub = lax.broadcasted_iota(jnp.int32, route_ref.shape, 0)
    route_ref[...] = jnp.where(sub == 0, i1, jnp.where(sub == 1, i2, jnp.where(sub == 2, w1,
                                                                                jnp.where(sub == 3, w2, 0.0))))


def _router(xc, xl, g_norm, mods, p):
    wr = jnp.zeros((LANES, D), f32).at[:N_EXPERTS].set(p['w_re'].T).at[
        N_EXPERTS:N_EXPERTS + N_GROUPS].set(p['w_rg'].T)
    w_hi, w_lo = _split_bf16(wr)
    br = jnp.zeros((LANES, 1), f32).at[:N_EXPERTS, 0].set(p['b_re']).at[
        N_EXPERTS:N_EXPERTS + N_GROUPS, 0].set(p['b_rg'])
    return pl.pallas_call(
        _router_kernel,
        grid=(T // ROW_TILE,),
        in_specs=_token_specs(D) + [
                  _const_spec((1, D)),
                  pl.BlockSpec((MOD_ROWS, D), lambda i: (_sample_of_tile(i, ROW_TILE), 0)),
                  _const_spec((LANES, D)), _const_spec((LANES, D)), _const_spec((LANES, 1))],
        out_specs=[_token_major_spec(ROW_TILE, lambda i: (i, 0)),
                   pl.BlockSpec((ROUTE_ROWS, ROW_TILE), lambda i: (0, i))],
        out_shape=[jax.ShapeDtypeStruct((T * SUBS, LANES), f32), jax.ShapeDtypeStruct((ROUTE_ROWS, T), f32)],
        compiler_params=_cp(("parallel",)), name="router",
    )(xc, xl, g_norm.reshape(1, D), mods, w_hi, w_lo, br)


PLAN_FIRST_TILE, PLAN_TILES = 0, 1


def _plan_kernel(rt_ref, pos_ref, plan_ref, rank):
    n_blk = T // 128
    e_col = lax.broadcasted_iota(jnp.int32, (N_EXPERTS, 128), 0).astype(f32)
    ri = lax.broadcasted_iota(jnp.int32, (128, 128), 0)
    ci = lax.broadcasted_iota(jnp.int32, (128, 128), 1)
    before = jnp.where(ri < ci, 1.0, 0.0).astype(bf16)

    def picks(b):
        cs = slice(b * 128, (b + 1) * 128)
        return rt_ref[0:1, cs] == e_col, rt_ref[1:2, cs] == e_col

    counts = jnp.zeros((N_EXPERTS, 1), f32)
    for b in range(n_blk):
        m0, m1 = picks(b)
        m = jnp.where(m0, 1.0, 0.0) + jnp.where(m1, 1.0, 0.0)
        rank[:, b * 128:(b + 1) * 128] = jnp.dot(m.astype(bf16), before, preferred_element_type=f32) + counts
        counts = counts + jnp.sum(m, axis=1, keepdims=True)

    tiles = jnp.floor((counts + (MOE_TILE - 1.0)) * (1.0 / MOE_TILE))
    er = lax.broadcasted_iota(jnp.int32, (N_EXPERTS, N_EXPERTS), 0)
    ec = lax.broadcasted_iota(jnp.int32, (N_EXPERTS, N_EXPERTS), 1)
    earlier = jnp.where(ec < er, 1.0, 0.0).astype(bf16)
    tile_start = jnp.dot(earlier, jnp.broadcast_to(tiles, (N_EXPERTS, 128)).astype(bf16),
                         preferred_element_type=f32)
    row_start = tile_start * MOE_TILE

    sub = lax.broadcasted_iota(jnp.int32, (8, 128), 0)
    for b in range(n_blk):
        m0, m1 = picks(b)
        base = rank[:, b * 128:(b + 1) * 128] + row_start
        p0 = jnp.sum(jnp.where(m0, base, 0.0), axis=0, keepdims=True)
        p1 = jnp.sum(jnp.where(m1, base, 0.0), axis=0, keepdims=True)
        pos_ref[:, b * 128:(b + 1) * 128] = jnp.where(sub == 0, p0, jnp.where(sub == 1, p1, 0.0)).astype(jnp.int32)

    diag = (lax.broadcasted_iota(jnp.int32, (N_EXPERTS, 128), 0)
            == lax.broadcasted_iota(jnp.int32, (N_EXPERTS, 128), 1))
    first = jnp.sum(jnp.where(diag, tile_start, 0.0), axis=0, keepdims=True)
    count = jnp.sum(jnp.where(diag, tiles, 0.0), axis=0, keepdims=True)
    rows = jnp.where(sub == PLAN_FIRST_TILE, first, jnp.where(sub == PLAN_TILES, count, 0.0))
    plan_ref[...] = rows.astype(jnp.int32)


def _slot_code(t, k):
    return t * SUBS + k * (SUBS // 2)


def _code_offset(code):
    return pl.multiple_of(code & ~(SUBS - 1), SUBS)


def _code_gate_index(code):
    return code >> 2


PAD_CODE = T * SUBS


def _invert_slots(pos_ref, first_ref, count_ref, code_ref):
    def pad_tile(tile, carry):
        for u in range(MOE_TILE):
            code_ref[tile * MOE_TILE + u] = PAD_CODE
        return carry

    def pad_last_tile(e, carry):
        return pad_tile(jnp.maximum(first_ref[e] + count_ref[e] - 1, 0), carry)
    lax.fori_loop(0, N_EXPERTS, pad_last_tile, 0)
    lax.fori_loop(first_ref[N_EXPERTS - 1] + count_ref[N_EXPERTS - 1], MOE_TILES, pad_tile, 0)

    group = 16
    for k in range(2):
        def place(i, carry):
            t0 = i * group
            slots = [pos_ref[k * T + t0 + u] for u in range(group)]
            for u, s in enumerate(slots):
                code_ref[s] = _slot_code(t0 + u, k)
            return carry
        lax.fori_loop(0, T // group, place, 0)


def _route_plan(route_t):
    pos, plan = pl.pallas_call(
        _plan_kernel,
        out_shape=[jax.ShapeDtypeStruct((8, T), jnp.int32), jax.ShapeDtypeStruct((8, LANES), jnp.int32)],
        scratch_shapes=[pltpu.VMEM((N_EXPERTS, T), f32)],
        compiler_params=_cp(None), name="route_plan",
    )(route_t)
    gates = jnp.pad(route_t[2:4].T.reshape(2 * T), (0, 8))
    return plan[PLAN_FIRST_TILE, :N_EXPERTS], plan[PLAN_TILES, :N_EXPERTS], pos[0:2].reshape(2 * T), gates


def _tile_index(i):
    return jnp.minimum(i, MOE_TILES - 1)


TM_ROWS = T * SUBS
SCATTER_GROUP = 8
STAGES = 4
LAST_EXPERT = N_EXPERTS - 1


def _tile_rows(g):
    return pl.ds(pl.multiple_of(g * MOE_TILE, MOE_TILE), MOE_TILE)


def _expert_tile_pairs(first, count, tile_step):
    def pair(pp, carry):
        for parity in range(2):
            g = 2 * pp + parity
            pl.when((g >= first) & (g < first + count))(functools.partial(tile_step, g, parity))
        return carry
    lax.fori_loop(first // 2, (first + count + 1) // 2, pair, 0)


def _gather_tile(code_ref, tile, xs, gbuf):
    base = tile * MOE_TILE
    for r in range(MOE_TILE):
        gbuf[r * SUBS:(r + 1) * SUBS, :] = xs[pl.ds(_code_offset(code_ref[base + r]), SUBS), :]


def _moe_up_kernel(first_ref, count_ref, pos_ref, h_hbm, w1_ref, w3_ref, hh_hbm, code_ref,
                   xs, gbuf_a, gbuf_b, w13, obuf, sem_x, sem_o):
    e = pl.program_id(0)
    first, count = first_ref[e], count_ref[e]
    gbufs = (gbuf_a, gbuf_b)

    def out_copy(slot, g):
        return pltpu.make_async_copy(obuf.at[slot], hh_hbm.at[_tile_rows(g), :], sem_o.at[slot])

    @pl.when(e == 0)
    def _():
        cp = pltpu.make_async_copy(h_hbm, xs.at[pl.ds(0, TM_ROWS), :], sem_x)
        cp.start()
        _invert_slots(pos_ref, first_ref, count_ref, code_ref)
        xs[TM_ROWS:TM_ROWS + SUBS, :] = jnp.zeros((SUBS, LANES), f32)
        cp.wait()
        _gather_tile(code_ref, 0, xs, gbuf_a)

    @pl.when(count > 0)
    def _():
        w13[:, :D_EXPERT] = w1_ref[0].astype(bf16)
        w13[:, D_EXPERT:] = w3_ref[0].astype(bf16)

    def tile_step(g, parity):
        _gather_tile(code_ref, _tile_index(g + 1), xs, gbufs[1 - parity])
        x3 = jnp.swapaxes(gbufs[parity][...].reshape(MOE_TILE, SUBS, LANES), 0, 1)
        x = jnp.concatenate([x3[s] for s in range(SUBS)], axis=1).astype(bf16)
        h13 = jnp.dot(x, w13[...], preferred_element_type=f32)
        hh = (_silu(h13[:, :D_EXPERT]) * h13[:, D_EXPERT:]).astype(bf16)

        slot = g % STAGES

        @pl.when(g >= STAGES)
        def _():
            out_copy(slot, g).wait()
        obuf[slot] = hh
        out_copy(slot, g).start()

    _expert_tile_pairs(first, count, tile_step)

    @pl.when(e == LAST_EXPERT)
    def _():
        n_used = first + count
        for slot in range(STAGES):
            pl.when(n_used > slot)(lambda slot=slot: out_copy(slot, 0).wait())
        obuf[0] = jnp.zeros((MOE_TILE, D_EXPERT), bf16)

        def zero_tile(g, carry):
            cp = out_copy(0, g)
            cp.start()
            cp.wait()
            return carry
        lax.fori_loop(n_used, MOE_TILES, zero_tile, 0)


def _scatter_tile(code_ref, gate_ref, tile, ybuf, acc):
    base = tile * MOE_TILE
    for g0 in range(0, MOE_TILE, SCATTER_GROUP):
        rows = range(g0, g0 + SCATTER_GROUP)
        codes = [code_ref[base + r] for r in rows]
        new = [acc[pl.ds(_code_offset(c), SUBS), :]
               + gate_ref[_code_gate_index(c)] * ybuf[r * SUBS:(r + 1) * SUBS, :]
               for r, c in zip(rows, codes)]
        for c, v in zip(codes, new):
            acc[pl.ds(_code_offset(c), SUBS), :] = v


RES_ROWS = 256


def _residual_out(x_hbm, y_hbm, tok0, sample_of_chunk, mod_ref, acc, rin, rout, sem_r, sem_w):
    n_chunks = x_hbm.shape[0] // RES_ROWS

    def rows(c):
        return pl.ds(pl.multiple_of(c * RES_ROWS, RES_ROWS), RES_ROWS)

    def in_copy(slot, c):
        return pltpu.make_async_copy(x_hbm.at[rows(c), :], rin.at[slot], sem_r.at[slot])

    def out_copy(slot, c):
        return pltpu.make_async_copy(rout.at[slot], y_hbm.at[rows(c), :], sem_w.at[slot])

    for c in range(STAGES - 1):
        in_copy(c, c).start()

    def ring(cc, carry):
        for slot in range(STAGES):
            c = STAGES * cc + slot
            in_copy(slot, c).wait()
            ahead = c + STAGES - 1

            @pl.when(ahead < n_chunks)
            def _():
                in_copy((slot + STAGES - 1) % STAGES, ahead).start()

            @pl.when(c >= STAGES)
            def _():
                out_copy(slot, c).wait()
            delta = _load_token_major(acc, RES_ROWS, tok0 + c * RES_ROWS)
            gate = mod_ref[pl.ds(sample_of_chunk(c) * MOD_ROWS + GATE2, 1), :]
            rout[slot] = rin[slot] + gate * delta
            out_copy(slot, c).start()
        return carry

    lax.fori_loop(0, n_chunks // STAGES, ring, 0)
    for slot in range(STAGES):
        out_copy(slot, 0).wait()


def _moe_down_kernel(first_ref, count_ref, code_ref, gate_ref, hh_hbm, w2_ref, xc_hbm, xl_hbm, mod_ref,
                     yc_hbm, yl_hbm, acc, ybuf_a, ybuf_b, w2b, ibuf, rin, rout, sem_i, sem_r, sem_w):
    e = pl.program_id(0)
    first, count = first_ref[e], count_ref[e]
    n_used = first_ref[LAST_EXPERT] + count_ref[LAST_EXPERT]
    ybufs = (ybuf_a, ybuf_b)

    def in_copy(slot, g):
        return pltpu.make_async_copy(hh_hbm.at[_tile_rows(g), :], ibuf.at[slot], sem_i.at[slot])

    @pl.when(e == 0)
    def _():
        for g in range(STAGES - 1):
            in_copy(g, g).start()

        def zero(c, carry):
            acc[pl.ds(pl.multiple_of(c * 1024, 1024), 1024), :] = jnp.zeros((1024, LANES), f32)
            return carry
        lax.fori_loop(0, TM_ROWS // 1024, zero, 0)
        acc[TM_ROWS:TM_ROWS + SUBS, :] = jnp.zeros((SUBS, LANES), f32)
        ybuf_b[...] = jnp.zeros_like(ybuf_b)

    @pl.when(count > 0)
    def _():
        w2b[...] = w2_ref[0].astype(bf16)

    def tile_step(g, parity):
        slot = g % STAGES
        in_copy(slot, g).wait()
        ahead = g + STAGES - 1

        @pl.when(ahead < n_used)
        def _():
            in_copy(ahead % STAGES, ahead).start()
        _store_token_major(ybufs[parity], jnp.dot(ibuf[slot], w2b[...], preferred_element_type=f32))
        _scatter_tile(code_ref, gate_ref, jnp.maximum(g - 1, 0), ybufs[1 - parity], acc)

    _expert_tile_pairs(first, count, tile_step)

    @pl.when(e == LAST_EXPERT)
    def _():
        for parity in range(2):
            pl.when((n_used > 0) & ((n_used - 1) % 2 == parity))(
                functools.partial(_scatter_tile, code_ref, gate_ref, n_used - 1, ybufs[parity], acc))
        _residual_out(xc_hbm, yc_hbm, 0, lambda c: 0, mod_ref, acc, rin, rout, sem_r, sem_w)
        _residual_out(xl_hbm, yl_hbm, T_CTX, lambda c: 1 + c // (DEC_SEQ // RES_ROWS),
                      mod_ref, acc, rin, rout, sem_r, sem_w)


def _moe(h_tm, route_t, p, xc, xl, mods):
    first_tile, n_tiles, slots, gates = _route_plan(route_t)
    tile_rows = pltpu.VMEM((MOE_TILE * SUBS, LANES), f32)
    staging = pltpu.VMEM((STAGES, MOE_TILE, D_EXPERT), bf16)
    res_rows = pltpu.VMEM((STAGES, RES_ROWS, D), f32)
    hbm = pl.BlockSpec(memory_space=pl.ANY)
    hh, codes = pl.pallas_call(
        _moe_up_kernel,
        grid_spec=pltpu.PrefetchScalarGridSpec(
            num_scalar_prefetch=3, grid=(N_EXPERTS,),
            in_specs=[hbm,
                      pl.BlockSpec((1, D, D_EXPERT), lambda e, f, n, s: (e, 0, 0)),
                      pl.BlockSpec((1, D, D_EXPERT), lambda e, f, n, s: (e, 0, 0))],
            out_specs=[hbm, pl.BlockSpec(memory_space=pltpu.SMEM)],
            scratch_shapes=[pltpu.VMEM((TM_ROWS + SUBS, LANES), f32), tile_rows, tile_rows,
                            pltpu.VMEM((D, 2 * D_EXPERT), bf16), staging,
                            pltpu.SemaphoreType.DMA(()), pltpu.SemaphoreType.DMA((STAGES,))]),
        out_shape=[jax.ShapeDtypeStruct((MOE_ROWS, D_EXPERT), bf16),
                   jax.ShapeDtypeStruct((MOE_ROWS,), jnp.int32)],
        compiler_params=_cp(("arbitrary",)), name="moe_up",
    )(first_tile, n_tiles, slots, h_tm, p['w1'], p['w3'])
    return pl.pallas_call(
        _moe_down_kernel,
        grid_spec=pltpu.PrefetchScalarGridSpec(
            num_scalar_prefetch=4, grid=(N_EXPERTS,),
            in_specs=[hbm, pl.BlockSpec((1, D_EXPERT, D), lambda e, f, n, c, g: (e, 0, 0)), hbm, hbm,
                      pl.BlockSpec((N_SAMPLES * MOD_ROWS, D), lambda e, f, n, c, g: (0, 0),
                                   pipeline_mode=pl.Buffered(1))],
            out_specs=[hbm, hbm],
            scratch_shapes=[pltpu.VMEM((TM_ROWS + SUBS, LANES), f32), tile_rows, tile_rows,
                            pltpu.VMEM((D_EXPERT, D), bf16), staging, res_rows, res_rows,
                            pltpu.SemaphoreType.DMA((STAGES,)), pltpu.SemaphoreType.DMA((STAGES,)),
                            pltpu.SemaphoreType.DMA((STAGES,))]),
        out_shape=[jax.ShapeDtypeStruct((T_CTX, D), f32), jax.ShapeDtypeStruct((T_LAT, D), f32)],
        compiler_params=_cp(("arbitrary",)), name="moe_down",
    )(first_tile, n_tiles, codes, gates, hh, p['w2'], xc, xl, mods)


def kernel(x_prompt, x_sample, cache_l0_k, cache_l0_v, cache_l1_ckv, cache_l1_kpe, c, c_ctx, l0_g_norm1, l0_g_norm2, l0_w_ada, l0_b_ada, l0_w_in, l0_g_vnorm, l0_w_s, l0_b_s, l0_g_q, l0_g_k, l0_sink, l0_w_o, l0_w_rg, l0_b_rg, l0_w_re, l0_b_re, l0_w1, l0_w3, l0_w2, l1_g_norm1, l1_g_norm2, l1_w_ada, l1_b_ada, l1_w_in, l1_g_qa, l1_w_uq, l1_g_kva, l1_w_ukv, l1_g_q, l1_g_k, l1_w_o, l1_w_rg, l1_b_rg, l1_w_re, l1_b_re, l1_w1, l1_w3, l1_w2):
    p0 = dict(w_in=l0_w_in, g_vnorm=l0_g_vnorm, w_s=l0_w_s, b_s=l0_b_s, g_q=l0_g_q, g_k=l0_g_k, sink=l0_sink,
              w_o=l0_w_o, w_rg=l0_w_rg, b_rg=l0_b_rg, w_re=l0_w_re, b_re=l0_b_re, w1=l0_w1, w3=l0_w3, w2=l0_w2)
    p1 = dict(w_in=l1_w_in, g_qa=l1_g_qa, w_uq=l1_w_uq, g_kva=l1_g_kva, w_ukv=l1_w_ukv, g_q=l1_g_q, g_k=l1_g_k,
              w_o=l1_w_o, w_rg=l1_w_rg, b_rg=l1_b_rg, w_re=l1_w_re, b_re=l1_b_re, w1=l1_w1, w3=l1_w3, w2=l1_w2)

    cond8 = jnp.zeros((MOD_ROWS, D), f32).at[0].set(c_ctx).at[1:1 + DEC_BATCH].set(c)
    mods0 = _mod_rows(_adaln(cond8, l0_w_ada, l0_b_ada))
    mods1 = _mod_rows(_adaln(cond8, l1_w_ada, l1_b_ada))

    xc0 = x_prompt.reshape(T_CTX, D)
    xl0 = x_sample.reshape(T_LAT, D)

    xc0m, xl0m, k_new, v_new = _l0_mixer(xc0, xl0, l0_g_norm1, mods0, p0, cache_l0_k, cache_l0_v)
    h0, route0 = _router(xc0m, xl0m, l0_g_norm2, mods0, p0)
    xc1, xl1 = _moe(h0, route0, p0, xc0m, xl0m, mods0)

    xc1m, xl1m, ckv_new, kpe_new = _l1_mixer(xc1, xl1, l1_g_norm1, mods1, p1, cache_l1_ckv, cache_l1_kpe)
    h1, route1 = _router(xc1m, xl1m, l1_g_norm2, mods1, p1)
    y_prompt, y_sample = _moe(h1, route1, p1, xc1m, xl1m, mods1)
    return (y_prompt.reshape(BATCH, SEQ, D), y_sample.reshape(DEC_BATCH, DEC_SEQ, D), k_new, v_new,
            ckv_new.reshape(BATCH, SEQ, C_KV_LORA), kpe_new.reshape(BATCH, SEQ, C_ROPE))
```

```python
import functools

import jax
import jax.numpy as jnp
import numpy as np
from jax import lax
from jax.experimental import pallas as pl
from jax.experimental.pallas import tpu as pltpu

f32 = jnp.float32
bf16 = jnp.bfloat16

D = 1024
BATCH, SEQ = 32, 256
DEC_BATCH, DEC_SEQ = 2, 1024
PAST = 512
T_CTX = BATCH * SEQ
T_LAT = DEC_BATCH * DEC_SEQ
T = T_CTX + T_LAT
GRID_W = 64
CHUNK = 128
WINDOW = 128
ROPE_THETA = 10000.0
EPS = 1e-6
NEG_INF = -1e30
LANES = 128
SUBS = D // LANES

A_WIDTH = 512
A_GROUPS = 4
B_HEADS, B_KV, B_GROUP, B_HD = 8, 2, 4, 64
B_SCALE = B_HD ** -0.5

C_HEADS, C_Q_LORA, C_KV_LORA, C_NOPE, C_ROPE, C_V = 16, 384, 256, 64, 32, 64
C_QK = C_NOPE + C_ROPE
C_SCALE = C_QK ** -0.5
ODD_IN_PAD = 768
SLOT = 128

N_GROUPS, N_EXPERTS, D_EXPERT = 4, 32, 256

N_SAMPLES = 1 + DEC_BATCH
MOD_ROWS = 8
SHIFT1, SCALE1, GATE1, SHIFT2, SCALE2, GATE2 = range(6)

ROW_TILE = 1024
BLOCK_ROWS = 1024
PROJ_ROWS = 512
ADALN_COLS = 1536
MOE_TILE = 256
MOE_ROWS = 2 * T + N_EXPERTS * MOE_TILE
MOE_TILES = MOE_ROWS // MOE_TILE
VMEM_CAP = 56 * 1024 * 1024


def _cp(sem, vmem=VMEM_CAP):
    return pltpu.CompilerParams(dimension_semantics=sem, vmem_limit_bytes=vmem)


def _const_spec(shape):
    nd = len(shape)
    return pl.BlockSpec(shape, lambda *_: (0,) * nd, pipeline_mode=pl.Buffered(1))


def _sample_of_tile(i, tile):
    n_ctx = T_CTX // tile
    per_lat = DEC_SEQ // tile
    return jnp.where(i < n_ctx, 0, 1 + (i - n_ctx) // per_lat)


def _mod(mod_ref, row):
    return mod_ref[row:row + 1, :]


def _silu(x):
    return x * jax.nn.sigmoid(x)


def _rms_rows(x, g):
    return x * lax.rsqrt(jnp.mean(x * x, -1, keepdims=True) + EPS) * g


def _swap_pairs(x):
    lane = lax.broadcasted_iota(jnp.int32, x.shape, x.ndim - 1)
    nxt = pltpu.roll(x, x.shape[-1] - 1, x.ndim - 1)
    prv = pltpu.roll(x, 1, x.ndim - 1)
    return jnp.where((lane & 1) == 0, nxt, prv)


def _split_bf16(x):
    hi = x.astype(bf16)
    return hi, (x - hi.astype(f32)).astype(bf16)


def _adaln_kernel(c_ref, w_ref, b_ref, o_ref):
    s_hi, s_lo = _split_bf16(_silu(c_ref[...]))
    w_hi, w_lo = _split_bf16(w_ref[...])
    o_ref[...] = (jnp.dot(s_hi, w_hi, preferred_element_type=f32) + jnp.dot(s_lo, w_hi, preferred_element_type=f32)
                  + jnp.dot(s_hi, w_lo, preferred_element_type=f32) + b_ref[...])


def _adaln(cond8, w, b):
    n = w.shape[1]
    return pl.pallas_call(
        _adaln_kernel,
        grid=(n // ADALN_COLS,),
        in_specs=[_const_spec((MOD_ROWS, D)), pl.BlockSpec((D, ADALN_COLS), lambda j: (0, j)),
                  pl.BlockSpec((1, ADALN_COLS), lambda j: (0, j))],
        out_specs=pl.BlockSpec((MOD_ROWS, ADALN_COLS), lambda j: (0, j)),
        out_shape=jax.ShapeDtypeStruct((MOD_ROWS, n), f32),
        compiler_params=_cp(("arbitrary",)),
        name="adaln",
    )(cond8, w, b.reshape(1, n))


def _mod_rows(m8):
    m = m8[:N_SAMPLES].reshape(N_SAMPLES, 6, D)
    return jnp.pad(m, ((0, 0), (0, MOD_ROWS - 6), (0, 0))).reshape(N_SAMPLES * MOD_ROWS, D)


N_CTX_TILES = T_CTX // ROW_TILE


def _token_specs(width):
    return [pl.BlockSpec((ROW_TILE, width), lambda i: (jnp.minimum(i, N_CTX_TILES - 1), 0)),
            pl.BlockSpec((ROW_TILE, width), lambda i: (jnp.maximum(i - N_CTX_TILES, 0), 0))]


def _token_rows(xc_ref, xl_ref):
    return jnp.where(pl.program_id(0) < N_CTX_TILES, xc_ref[...], xl_ref[...])


def _store_token_major(ref, x):
    n = x.shape[0]
    for s in range(SUBS):
        ref[pl.ds(s, n, stride=SUBS), :] = x[:, s * LANES:(s + 1) * LANES]


def _load_token_major(ref, n, row0=0):
    return jnp.concatenate([ref[pl.ds(row0 * SUBS + s, n, stride=SUBS), :] for s in range(SUBS)], axis=1)


def _token_major_spec(rows, index_map):
    return pl.BlockSpec((rows * SUBS, LANES), index_map)


def _rope_tables(n, rot_dim, lanes, lane0, copies=1):
    rows_count = n // GRID_W
    rows = np.repeat(np.arange(rows_count), GRID_W).astype(np.float64)
    cols = np.tile(np.arange(GRID_W), rows_count).astype(np.float64)
    d_axis = rot_dim // 2
    inv = ROPE_THETA ** (-np.arange(0, d_axis, 2, dtype=np.float64) / d_axis)
    ang = np.concatenate([rows[:, None] * inv, cols[:, None] * inv], -1)
    c = np.ones((n, lanes), np.float32)
    s = np.zeros((n, lanes), np.float32)
    for j in range(copies):
        lo = lane0 + j * rot_dim
        c[:, lo:lo + rot_dim] = np.repeat(np.cos(ang), 2, axis=1)
        s[:, lo:lo + rot_dim] = np.repeat(np.sin(ang), 2, axis=1) * np.tile(np.array([-1.0, 1.0]), rot_dim // 2)
    return jnp.asarray(c), jnp.asarray(s)


L0_Q0 = 2 * A_WIDTH
L0_K0 = L0_Q0 + B_GROUP * LANES
L0_V0 = L0_K0 + B_KV * B_HD
L0_IN = L0_V0 + B_KV * B_HD


def _l0_kernel(*refs, latent):
    if latent:
        (sink_ref, x_ref, gn_ref, mod_ref, win_ref, gvn_ref, ws_ref, bsb_ref, gq_ref, gk_ref, wo_ref,
         cos_ref, sin_ref, kc_ref, vc_ref, xo_ref, zs, cat, qs, ks, vt, kcb, vct) = refs
        key_off = WINDOW
    else:
        (sink_ref, x_ref, gn_ref, mod_ref, win_ref, gvn_ref, ws_ref, bsb_ref, gq_ref, gk_ref, wo_ref,
         xo_ref, ko_ref, vo_ref, zs, cat, qs, ks, vt, kf, vf) = refs
        key_off = 0
    n = BLOCK_ROWS
    n_chunks = n // CHUNK
    low = lax.broadcasted_iota(jnp.int32, (CHUNK, LANES), 1) < B_HD

    if latent:
        zpad = jnp.zeros((WINDOW, LANES), bf16)
        for c0 in (0, 1 + n_chunks):
            ks[c0 * CHUNK:(c0 + 1) * CHUNK, :] = zpad
            vt[c0] = zpad
        kcb[...] = kc_ref[...].astype(bf16)
        for i in range(PAST // CHUNK):
            vct[i] = vc_ref[i * CHUNK:(i + 1) * CHUNK, :].T.astype(bf16)

    def project(c, carry):
        r = pl.ds(pl.multiple_of(c * PROJ_ROWS, PROJ_ROWS), PROJ_ROWS)
        h = _rms_rows(x_ref[r, :], gn_ref[...]) * (1.0 + _mod(mod_ref, SCALE1)) + _mod(mod_ref, SHIFT1)
        zs[r, :] = jnp.dot(h.astype(bf16), win_ref[...], preferred_element_type=f32)
        return carry

    lax.fori_loop(0, n // PROJ_ROWS, project, 0)

    def prepare(c, carry):
        r = pl.ds(pl.multiple_of(c * CHUNK, CHUNK), CHUNK)
        u = jax.nn.gelu(zs[r, 0:A_WIDTH])
        v = jax.nn.gelu(zs[r, A_WIDTH:2 * A_WIDTH])
        mu = jnp.mean(v, -1, keepdims=True)
        var = jnp.mean(jnp.square(v - mu), -1, keepdims=True)
        vn = ((v - mu) * lax.rsqrt(var + EPS) * gvn_ref[...]).astype(bf16)
        for g in range(A_GROUPS):
            cs = slice(g * CHUNK, (g + 1) * CHUNK)
            mixed = jnp.dot(ws_ref[g], vn[:, cs], preferred_element_type=f32) + bsb_ref[g]
            cat[r, cs] = (u[:, cs] * mixed).astype(bf16)
        if latent:
            cs_, sn_ = cos_ref[r, :], sin_ref[r, :]
        def half_norm(v, gain):
            sq = v * v
            s0 = jnp.sum(jnp.where(low, sq, 0.0), -1, keepdims=True)
            s1 = jnp.sum(jnp.where(low, 0.0, sq), -1, keepdims=True)
            return v * lax.rsqrt(jnp.where(low, s0, s1) * (1.0 / B_HD) + EPS) * gain

        for j in range(B_GROUP):
            js = slice(j * LANES, (j + 1) * LANES)
            qj = half_norm(zs[r, L0_Q0 + j * LANES:L0_Q0 + (j + 1) * LANES], gq_ref[...])
            if latent:
                qj = qj * cs_ + _swap_pairs(qj) * sn_
            qs[r, js] = qj.astype(bf16)
        k = half_norm(zs[r, L0_K0:L0_K0 + LANES], gk_ref[...])
        vv = zs[r, L0_V0:L0_V0 + LANES]
        if latent:
            k = k * cs_ + _swap_pairs(k) * sn_
        else:
            kf[r, :] = k
            vf[r, :] = vv
        kr = pl.ds(pl.multiple_of(c * CHUNK + key_off, CHUNK), CHUNK)
        ks[kr, :] = k.astype(bf16)
        vt[c + key_off // CHUNK] = vv.T.astype(bf16)
        return carry

    lax.fori_loop(0, n_chunks // 2, lambda i, carry: prepare(2 * i + 1, prepare(2 * i, carry)), 0)

    def attend(jobs):
        scored = []
        for r, rows, key_sets in jobs:
            low_q = lax.broadcasted_iota(jnp.int32, (rows, LANES), 1) < B_HD
            slots = [qs[r, j * LANES:(j + 1) * LANES] for j in range(B_GROUP)]
            zero = jnp.zeros((rows, LANES), bf16)
            q = jnp.concatenate([jnp.where(low_q, qj, zero) for qj in slots]
                                + [jnp.where(low_q, zero, qj) for qj in slots], axis=0)
            sk = jnp.concatenate([jnp.full((1, rows), sink_ref[h], f32) for h in range(B_HEADS)], axis=1)
            scores = []
            m = sk
            for k, _, keep in key_sets:
                s = lax.dot_general(k, q, (((1,), (1,)), ((), ())), preferred_element_type=f32)
                if keep is not None:
                    s = jnp.where(keep, s, NEG_INF)
                scores.append(s)
                m = jnp.maximum(m, jnp.max(s, 0, keepdims=True))
            scored.append((sk, scores, m))
        outs = []
        for (r, rows, key_sets), (sk, scores, m) in zip(jobs, scored):
            den = jnp.exp(sk - m)
            ot = None
            for s, (_, vts, _) in zip(scores, key_sets):
                e = jnp.exp(s - m)
                den = den + jnp.sum(e, 0, keepdims=True)
                eb = e.astype(bf16)
                for i, v_t in enumerate(vts):
                    pv = jnp.dot(v_t, eb[i * CHUNK:(i + 1) * CHUNK, :], preferred_element_type=f32)
                    ot = pv if ot is None else ot + pv
            outs.append(ot * (1.0 / den))
        for (r, rows, _), ot in zip(jobs, outs):
            for pair in range(B_HEADS // 2):
                f0 = (2 * pair // B_GROUP) * B_HD
                pair_t = jnp.concatenate([ot[f0:f0 + B_HD, 2 * pair * rows:(2 * pair + 1) * rows],
                                          ot[f0:f0 + B_HD, (2 * pair + 1) * rows:(2 * pair + 2) * rows]], axis=0)
                cat[r, A_WIDTH + pair * LANES:A_WIDTH + (pair + 1) * LANES] = pair_t.T.astype(bf16)

    jobs_per_step = 2 if latent else 4
    if latent:
        span = CHUNK + 2 * WINDOW

        def block_job(c):
            start = pl.multiple_of(c * CHUNK, CHUNK)
            kr = pl.ds(start, span)
            kj = lax.broadcasted_iota(jnp.int32, (span, B_HEADS * CHUNK), 0)
            qi = lax.broadcasted_iota(jnp.int32, (span, B_HEADS * CHUNK), 1) & (CHUNK - 1)
            kpos = start - WINDOW + kj
            keep = (jnp.abs(kj - WINDOW - qi) <= WINDOW) & (kpos >= 0) & (kpos < n)
            return (pl.ds(start, CHUNK), CHUNK,
                    [(ks[kr, :], [vt[c + i] for i in range(span // CHUNK)], keep),
                     (kcb[...], [vct[i] for i in range(PAST // CHUNK)], None)])

        def attend_blocks(i, carry):
            attend([block_job(i * jobs_per_step + j) for j in range(jobs_per_step)])
            return carry

        lax.fori_loop(0, n_chunks // jobs_per_step, attend_blocks, 0)
    else:
        def seq_job(sq):
            r = pl.ds(pl.multiple_of(sq * SEQ, SEQ), SEQ)
            return r, SEQ, [(ks[r, :], [vt[sq * (SEQ // CHUNK) + i] for i in range(SEQ // CHUNK)], None)]

        def attend_seqs(i, carry):
            seqs = [i * jobs_per_step + j for j in range(jobs_per_step)]
            attend([seq_job(sq) for sq in seqs])
            for sq in seqs:
                r = pl.ds(pl.multiple_of(sq * SEQ, SEQ), SEQ)
                ko_ref[sq] = kf[r, :].T
                vo_ref[sq] = vf[r, :].T
            return carry

        lax.fori_loop(0, n // SEQ // jobs_per_step, attend_seqs, 0)

    def output(c, carry):
        r = pl.ds(pl.multiple_of(c * PROJ_ROWS, PROJ_ROWS), PROJ_ROWS)
        y = jnp.dot(cat[r, :], wo_ref[...], preferred_element_type=f32)
        xo_ref[r, :] = x_ref[r, :] + _mod(mod_ref, GATE1) * y
        return carry

    lax.fori_loop(0, n // PROJ_ROWS, output, 0)


def _l0_mixer(xc, xl, g_norm, mods, p, cache_k, cache_v):
    w = p['w_in']
    q_slots = w[:, L0_Q0:L0_Q0 + B_HEADS * B_HD].reshape(D, B_KV, B_GROUP, B_HD).transpose(0, 2, 1, 3).reshape(
        D, B_GROUP * LANES)
    win = jnp.concatenate([w[:, :L0_Q0], q_slots, w[:, L0_Q0 + B_HEADS * B_HD:]], axis=1).astype(bf16)
    gq = jnp.tile(p['g_q'], B_KV).reshape(1, LANES) * B_SCALE
    gk = jnp.tile(p['g_k'], B_KV).reshape(1, LANES)
    gvn = p['g_vnorm'].reshape(1, A_WIDTH)
    ws = p['w_s'].astype(bf16)
    bsb = jnp.broadcast_to(p['b_s'][:, :, None], (A_GROUPS, CHUNK, CHUNK))
    wo = p['w_o'].astype(bf16)
    weights = (g_norm.reshape(1, D),)
    consts = (win, gvn, ws, bsb, gq, gk, wo)
    c_specs = [_const_spec(a.shape) for a in consts]
    smem = pl.BlockSpec(memory_space=pltpu.SMEM)
    row = pl.BlockSpec((BLOCK_ROWS, D), lambda b: (b, 0))
    kv = pl.BlockSpec((BLOCK_ROWS // SEQ, LANES, SEQ), lambda b: (b, 0, 0))

    def scratch(pad):
        return [pltpu.VMEM((BLOCK_ROWS, L0_IN), f32), pltpu.VMEM((BLOCK_ROWS, D), bf16),
                pltpu.VMEM((BLOCK_ROWS, B_GROUP * LANES), bf16), pltpu.VMEM((BLOCK_ROWS + pad, LANES), bf16),
                pltpu.VMEM(((BLOCK_ROWS + pad) // CHUNK, LANES, CHUNK), bf16)]

    kv_shape = jax.ShapeDtypeStruct((BATCH, LANES, SEQ), f32)
    state = pltpu.VMEM((BLOCK_ROWS, LANES), f32)
    xo_ctx, k_t, v_t = pl.pallas_call(
        functools.partial(_l0_kernel, latent=False),
        grid=(T_CTX // BLOCK_ROWS,),
        in_specs=[smem, row, _const_spec((1, D)), pl.BlockSpec((MOD_ROWS, D), lambda b: (0, 0))] + c_specs,
        out_specs=[row, kv, kv],
        out_shape=[jax.ShapeDtypeStruct((T_CTX, D), f32), kv_shape, kv_shape],
        scratch_shapes=scratch(0) + [state, state],
        compiler_params=_cp(("parallel",)), name="l0_mixer_ctx",
    )(p['sink'], xc, *weights, mods, *consts)
    k_new = k_t.reshape(BATCH, B_KV, B_HD, SEQ).transpose(0, 3, 1, 2)
    v_new = v_t.reshape(BATCH, B_KV, B_HD, SEQ).transpose(0, 3, 1, 2)

    cos, sin = _rope_tables(DEC_SEQ, B_HD, LANES, 0, copies=LANES // B_HD)
    cache = pl.BlockSpec((None, PAST, LANES), lambda b: (b, 0, 0))
    past = [pltpu.VMEM((PAST, LANES), bf16), pltpu.VMEM((PAST // CHUNK, LANES, CHUNK), bf16)]
    xo_lat = pl.pallas_call(
        functools.partial(_l0_kernel, latent=True),
        grid=(DEC_BATCH,),
        in_specs=[smem, row, _const_spec((1, D)), pl.BlockSpec((MOD_ROWS, D), lambda b: (1 + b, 0))] + c_specs + [
                  _const_spec(cos.shape), _const_spec(sin.shape), cache, cache],
        out_specs=row,
        out_shape=jax.ShapeDtypeStruct((T_LAT, D), f32),
        scratch_shapes=scratch(2 * WINDOW) + past,
        compiler_params=_cp(("parallel",)), name="l0_mixer_lat",
    )(p['sink'], xl, *weights, mods, *consts, cos, sin,
      cache_k.reshape(DEC_BATCH, PAST, LANES), cache_v.reshape(DEC_BATCH, PAST, LANES))
    return xo_ctx, xo_lat, k_new, v_new


C_SLOTS = C_HEADS * SLOT
C_PAIRS = C_HEADS // 2
L1_ROWS = 256


def _l1_kernel(*refs, latent):
    if latent:
        (x_ref, gn_ref, mod_ref, win_ref, gqa_ref, wuq_ref, gq_ref, gkva_ref, wuk_ref, wuvt_ref, gk_ref,
         wo_ref, wuqs_ref, qcos_ref, qsin_ref, kcos_ref, ksin_ref, cckv_ref, ckpe_ref, xo_ref,
         zs, cat, qs, ks, vt, wide, wide2) = refs
        n_ctx = PAST
    else:
        (x_ref, gn_ref, mod_ref, win_ref, gqa_ref, wuq_ref, gq_ref, gkva_ref, wuk_ref, wuvt_ref, gk_ref,
         wo_ref, xo_ref, ckvo_ref, kpeo_ref, zs, cat, qs, ks, vt, wide) = refs
        n_ctx = 0
    n = BLOCK_ROWS
    nt_dims = (((1,), (1,)), ((), ()))

    def inv_rms(v):
        return lax.rsqrt(jnp.sum(v * v, -1, keepdims=True) * (1.0 / C_QK) + EPS)

    def expand_keys(ckv_n, kslot, kb, rope_rows):
        cb = ckv_n.astype(bf16)
        key_rows = pl.ds(pl.multiple_of(kb * L1_ROWS, L1_ROWS), L1_ROWS)
        wide[...] = jnp.dot(cb, wuk_ref[...], preferred_element_type=f32)
        if rope_rows is not None:
            kcos = kcos_ref[rope_rows, :]
            turned = _swap_pairs(kslot) * ksin_ref[rope_rows, :]
        for h in range(C_HEADS):
            kh = wide[:, h * SLOT:(h + 1) * SLOT] + kslot
            if rope_rows is not None:
                kh = inv_rms(kh) * (kh * kcos + turned)
            else:
                kh = kh * inv_rms(kh) * gk_ref[...]
            ks[h, key_rows, :] = kh.astype(bf16)
        v_t = lax.dot_general(wuvt_ref[...], cb, nt_dims, preferred_element_type=f32).astype(bf16)
        for pair in range(C_PAIRS):
            vt[pair, kb] = v_t[pair * LANES:(pair + 1) * LANES, :]

    if latent:
        def past_keys(c, carry):
            r = pl.ds(pl.multiple_of(c * L1_ROWS, L1_ROWS), L1_ROWS)
            expand_keys(cckv_ref[r, :], ckpe_ref[r, :], c, None)
            return carry

        lax.fori_loop(0, PAST // L1_ROWS, past_keys, 0)

    def project(c, carry):
        r = pl.ds(pl.multiple_of(c * PROJ_ROWS, PROJ_ROWS), PROJ_ROWS)
        h = _rms_rows(x_ref[r, :], gn_ref[...]) * (1.0 + _mod(mod_ref, SCALE1)) + _mod(mod_ref, SHIFT1)
        zs[r, :] = jnp.dot(h.astype(bf16), win_ref[...], preferred_element_type=f32)
        return carry

    lax.fori_loop(0, n // PROJ_ROWS, project, 0)

    def prepare(c, carry):
        r = pl.ds(pl.multiple_of(c * L1_ROWS, L1_ROWS), L1_ROWS)
        qa = _rms_rows(zs[r, 0:C_Q_LORA], gqa_ref[...]).astype(bf16)
        wide[...] = jnp.dot(qa, wuq_ref[...], preferred_element_type=f32)
        if latent:
            wide2[...] = jnp.dot(qa, wuqs_ref[...], preferred_element_type=f32)
            qcos, qsin = qcos_ref[r, :], qsin_ref[r, :]
        for h in range(C_HEADS):
            hs = slice(h * SLOT, (h + 1) * SLOT)
            qh = wide[:, hs]
            if latent:
                qh = inv_rms(qh) * (qh * qcos + wide2[:, hs] * qsin)
            else:
                qh = qh * inv_rms(qh) * gq_ref[...]
            qs[h, r, :] = qh.astype(bf16)
        ckv_n = _rms_rows(zs[r, C_Q_LORA:C_Q_LORA + C_KV_LORA], gkva_ref[...])
        kslot = zs[r, C_Q_LORA + C_KV_LORA:ODD_IN_PAD]
        if not latent:
            ckvo_ref[r, :] = ckv_n
            kpeo_ref[c] = kslot.T[C_NOPE:C_QK, :]
        expand_keys(ckv_n, kslot, c + n_ctx // L1_ROWS, r if latent else None)
        return carry

    lax.fori_loop(0, n // L1_ROWS, prepare, 0)

    low = lax.broadcasted_iota(jnp.int32, (2 * C_V, L1_ROWS), 0) < C_V
    n_kblocks = (n_ctx + n) // L1_ROWS
    pairs_per_step = 4 if latent else 8

    blocks_per_step = 1 if latent else 2

    def attend(cc, carry):
        blocks = [cc * blocks_per_step + j for j in range(blocks_per_step)]
        rows = [pl.ds(pl.multiple_of(c * L1_ROWS, L1_ROWS), L1_ROWS) for c in blocks]

        def values_t(pair, c, eb):
            if not latent:
                return jnp.dot(vt[pair, c], eb, preferred_element_type=f32)
            o_t = None
            for b in range(n_kblocks):
                pv = jnp.dot(vt[pair, b], eb[b * L1_ROWS:(b + 1) * L1_ROWS, :], preferred_element_type=f32)
                o_t = pv if o_t is None else o_t + pv
            return o_t

        def pairs_step(i, carry2):
            pairs = [i * pairs_per_step + j for j in range(pairs_per_step)]
            units = [(c, r, 2 * p + hh) for c, r in zip(blocks, rows) for p in pairs for hh in range(2)]
            scores = [lax.dot_general(ks[h] if latent else ks[h, r, :], qs[h, r, :], nt_dims,
                                      preferred_element_type=f32) for _, r, h in units]
            exps = [jnp.exp(s - jnp.max(s, 0, keepdims=True)) for s in scores]
            dens = [jnp.sum(e, 0, keepdims=True) for e in exps]
            outs = [values_t(h // 2, c, e.astype(bf16)) / den for (c, _, h), e, den in zip(units, exps, dens)]
            for u in range(0, len(units), 2):
                _, r, h = units[u]
                cat[h // 2, r, :] = jnp.where(low, outs[u], outs[u + 1]).T.astype(bf16)
            return carry2

        return lax.fori_loop(0, C_PAIRS // pairs_per_step, pairs_step, carry)

    lax.fori_loop(0, n // L1_ROWS // blocks_per_step, attend, 0)

    def output(c, carry):
        r = pl.ds(pl.multiple_of(c * PROJ_ROWS, PROJ_ROWS), PROJ_ROWS)
        heads = jnp.concatenate([cat[pair, r, :] for pair in range(C_PAIRS)], axis=1)
        y = jnp.dot(heads, wo_ref[...], preferred_element_type=f32)
        xo_ref[r, :] = x_ref[r, :] + _mod(mod_ref, GATE1) * y
        return carry

    lax.fori_loop(0, n // PROJ_ROWS, output, 0)


def _slot_cols(w, heads, width, lo, hi, lane0):
    k = w.shape[0]
    w3 = w.reshape(k, heads, width)[:, :, lo:hi]
    out = jnp.zeros((k, heads, SLOT), w.dtype).at[:, :, lane0:lane0 + (hi - lo)].set(w3)
    return out.reshape(k, heads * SLOT)


def _l1_mixer(xc, xl, g_norm, mods, p, cache_ckv, cache_kpe):
    w_in = jnp.zeros((D, ODD_IN_PAD), f32).at[:, :C_Q_LORA + C_KV_LORA].set(
        p['w_in'][:, :C_Q_LORA + C_KV_LORA]).at[
        :, C_Q_LORA + C_KV_LORA + C_NOPE:C_Q_LORA + C_KV_LORA + C_QK].set(p['w_in'][:, C_Q_LORA + C_KV_LORA:])
    wuq = _slot_cols(p['w_uq'], C_HEADS, C_QK, 0, C_QK, 0).astype(bf16)
    wuk = _slot_cols(p['w_ukv'], C_HEADS, C_NOPE + C_V, 0, C_NOPE, 0).astype(bf16)
    wuv_t = p['w_ukv'].reshape(C_KV_LORA, C_HEADS, C_NOPE + C_V)[:, :, C_NOPE:].reshape(
        C_KV_LORA, C_HEADS * C_V).T.astype(bf16)
    gq = jnp.zeros((1, SLOT), f32).at[0, :C_QK].set(p['g_q'] * C_SCALE)
    gk = jnp.zeros((1, SLOT), f32).at[0, :C_QK].set(p['g_k'])
    consts = (g_norm.reshape(1, D), w_in.astype(bf16), p['g_qa'].reshape(1, C_Q_LORA), wuq, gq,
              p['g_kva'].reshape(1, C_KV_LORA), wuk, wuv_t, gk, p['w_o'].astype(bf16))
    c_specs = [_const_spec(a.shape) for a in consts]
    row = pl.BlockSpec((BLOCK_ROWS, D), lambda b: (b, 0))
    n_ctx_blocks = T_CTX // BLOCK_ROWS

    def scratch(n_keys):
        return [pltpu.VMEM((BLOCK_ROWS, ODD_IN_PAD), f32), pltpu.VMEM((C_PAIRS, BLOCK_ROWS, LANES), bf16),
                pltpu.VMEM((C_HEADS, BLOCK_ROWS, SLOT), bf16), pltpu.VMEM((C_HEADS, n_keys, SLOT), bf16),
                pltpu.VMEM((C_PAIRS, n_keys // L1_ROWS, LANES, L1_ROWS), bf16),
                pltpu.VMEM((L1_ROWS, C_SLOTS), f32)]

    xo_ctx, ckv_new, kpe_t = pl.pallas_call(
        functools.partial(_l1_kernel, latent=False),
        grid=(n_ctx_blocks,),
        in_specs=[row, c_specs[0], pl.BlockSpec((MOD_ROWS, D), lambda b: (0, 0))] + c_specs[1:],
        out_specs=[row, pl.BlockSpec((BLOCK_ROWS, C_KV_LORA), lambda b: (b, 0)),
                   pl.BlockSpec((BLOCK_ROWS // SEQ, C_ROPE, SEQ), lambda b: (b, 0, 0))],
        out_shape=[jax.ShapeDtypeStruct((T_CTX, D), f32), jax.ShapeDtypeStruct((T_CTX, C_KV_LORA), f32),
                   jax.ShapeDtypeStruct((BATCH, C_ROPE, SEQ), f32)],
        scratch_shapes=scratch(BLOCK_ROWS),
        compiler_params=_cp(("parallel",)), name="l1_mixer_ctx",
    )(xc, consts[0], mods, *consts[1:])

    cos, sin = _rope_tables(DEC_SEQ, C_ROPE, SLOT, C_NOPE)
    def pair_swap(a):
        pairs = a.reshape(a.shape[:-1] + (a.shape[-1] // 2, 2))
        return jnp.stack([pairs[..., 1], pairs[..., 0]], axis=-1).reshape(a.shape)

    w_rope = p['w_uq'].reshape(C_Q_LORA, C_HEADS, C_QK)[:, :, C_NOPE:]
    wuq_swapped = jnp.zeros((C_Q_LORA, C_HEADS, SLOT), f32).at[:, :, C_NOPE:C_QK].set(pair_swap(w_rope)).reshape(
        C_Q_LORA, C_SLOTS).astype(bf16)
    rope = (wuq_swapped, gq * cos, pair_swap(gq) * sin, gk * cos, pair_swap(gk) * sin)
    ckpe = jnp.zeros((DEC_BATCH, PAST, SLOT), f32).at[:, :, C_NOPE:C_QK].set(cache_kpe)
    xo_lat = pl.pallas_call(
        functools.partial(_l1_kernel, latent=True),
        grid=(DEC_BATCH,),
        in_specs=[pl.BlockSpec((BLOCK_ROWS, D), lambda b: (b, 0), pipeline_mode=pl.Buffered(1)), c_specs[0],
                  pl.BlockSpec((MOD_ROWS, D), lambda b: (1 + b, 0))] + c_specs[1:] + [_const_spec(a.shape) for a in rope] + [
                  pl.BlockSpec((None, PAST, C_KV_LORA), lambda b: (b, 0, 0)),
                  pl.BlockSpec((None, PAST, SLOT), lambda b: (b, 0, 0))],
        out_specs=row,
        out_shape=jax.ShapeDtypeStruct((T_LAT, D), f32),
        scratch_shapes=scratch(PAST + BLOCK_ROWS) + [pltpu.VMEM((L1_ROWS, C_SLOTS), f32)],
        compiler_params=_cp(("parallel",)), name="l1_mixer_lat",
    )(xl, consts[0], mods, *consts[1:], *rope, cache_ckv, ckpe)
    return xo_ctx, xo_lat, ckv_new, kpe_t.transpose(0, 2, 1)


ROUTER_ROWS = 40
ROUTE_ROWS = 8


def _router_kernel(xc_ref, xl_ref, gn_ref, mod_ref, whi_ref, wlo_ref, br_ref, h_ref, route_ref):
    h = (_rms_rows(_token_rows(xc_ref, xl_ref), gn_ref[...]) * (1.0 + _mod(mod_ref, SCALE2))
         + _mod(mod_ref, SHIFT2))
    _store_token_major(h_ref, h)
    h_hi, h_lo = _split_bf16(h)
    nt = (((1,), (1,)), ((), ()))
    logits = (lax.dot_general(whi_ref[...], h_hi, nt, preferred_element_type=f32)
              + lax.dot_general(whi_ref[...], h_lo, nt, preferred_element_type=f32)
              + lax.dot_general(wlo_ref[...], h_hi, nt, preferred_element_type=f32))
    logits = logits[0:ROUTER_ROWS, :] + br_ref[0:ROUTER_ROWS, :]
    row_i = lax.broadcasted_iota(jnp.int32, logits.shape, 0)
    row = row_i.astype(f32)
    big = 1e6
    is_g = (row_i >= N_EXPERTS) & (row_i < N_EXPERTS + N_GROUPS)
    lg = jnp.where(is_g, logits, -jnp.inf)
    mg = jnp.max(lg, 0, keepdims=True)
    gsel = jnp.min(jnp.where(lg == mg, row, big), 0, keepdims=True) - N_EXPERTS
    pg_sel = 1.0 / jnp.sum(jnp.where(is_g, jnp.exp(lg - mg), 0.0), 0, keepdims=True)
    in_grp = (row_i < N_EXPERTS) & ((row_i >> 3).astype(f32) == gsel)
    le = jnp.where(in_grp, logits, -jnp.inf)
    m1 = jnp.max(le, 0, keepdims=True)
    i1 = jnp.min(jnp.where(le == m1, row, big), 0, keepdims=True)
    le2 = jnp.where(row == i1, -jnp.inf, le)
    m2 = jnp.max(le2, 0, keepdims=True)
    i2 = jnp.min(jnp.where(le2 == m2, row, big), 0, keepdims=True)
    e2 = jnp.exp(m2 - m1)
    w1 = pg_sel / (1.0 + e2)
    w2 = pg_sel * e2 / (1.0 + e2)
    sub = lax.broadcasted_iota(jnp.int32, route_ref.shape, 0)
    route_ref[...] = jnp.where(sub == 0, i1, jnp.where(sub == 1, i2, jnp.where(sub == 2, w1,
                                                                                jnp.where(sub == 3, w2, 0.0))))


def _router(xc, xl, g_norm, mods, p):
    wr = jnp.zeros((LANES, D), f32).at[:N_EXPERTS].set(p['w_re'].T).at[
        N_EXPERTS:N_EXPERTS + N_GROUPS].set(p['w_rg'].T)
    w_hi, w_lo = _split_bf16(wr)
    br = jnp.zeros((LANES, 1), f32).at[:N_EXPERTS, 0].set(p['b_re']).at[
        N_EXPERTS:N_EXPERTS + N_GROUPS, 0].set(p['b_rg'])
    return pl.pallas_call(
        _router_kernel,
        grid=(T // ROW_TILE,),
        in_specs=_token_specs(D) + [
                  _const_spec((1, D)),
                  pl.BlockSpec((MOD_ROWS, D), lambda i: (_sample_of_tile(i, ROW_TILE), 0)),
                  _const_spec((LANES, D)), _const_spec((LANES, D)), _const_spec((LANES, 1))],
        out_specs=[_token_major_spec(ROW_TILE, lambda i: (i, 0)),
                   pl.BlockSpec((ROUTE_ROWS, ROW_TILE), lambda i: (0, i))],
        out_shape=[jax.ShapeDtypeStruct((T * SUBS, LANES), f32), jax.ShapeDtypeStruct((ROUTE_ROWS, T), f32)],
        compiler_params=_cp(("parallel",)), name="router",
    )(xc, xl, g_norm.reshape(1, D), mods, w_hi, w_lo, br)


PLAN_FIRST_TILE, PLAN_TILES = 0, 1


def _plan_kernel(rt_ref, pos_ref, plan_ref, rank):
    n_blk = T // 128
    e_col = lax.broadcasted_iota(jnp.int32, (N_EXPERTS, 128), 0).astype(f32)
    ri = lax.broadcasted_iota(jnp.int32, (128, 128), 0)
    ci = lax.broadcasted_iota(jnp.int32, (128, 128), 1)
    before = jnp.where(ri < ci, 1.0, 0.0).astype(bf16)

    def picks(b):
        cs = slice(b * 128, (b + 1) * 128)
        return rt_ref[0:1, cs] == e_col, rt_ref[1:2, cs] == e_col

    counts = jnp.zeros((N_EXPERTS, 1), f32)
    for b in range(n_blk):
        m0, m1 = picks(b)
        m = jnp.where(m0, 1.0, 0.0) + jnp.where(m1, 1.0, 0.0)
        rank[:, b * 128:(b + 1) * 128] = jnp.dot(m.astype(bf16), before, preferred_element_type=f32) + counts
        counts = counts + jnp.sum(m, axis=1, keepdims=True)

    tiles = jnp.floor((counts + (MOE_TILE - 1.0)) * (1.0 / MOE_TILE))
    er = lax.broadcasted_iota(jnp.int32, (N_EXPERTS, N_EXPERTS), 0)
    ec = lax.broadcasted_iota(jnp.int32, (N_EXPERTS, N_EXPERTS), 1)
    earlier = jnp.where(ec < er, 1.0, 0.0).astype(bf16)
    tile_start = jnp.dot(earlier, jnp.broadcast_to(tiles, (N_EXPERTS, 128)).astype(bf16),
                         preferred_element_type=f32)
    row_start = tile_start * MOE_TILE

    sub = lax.broadcasted_iota(jnp.int32, (8, 128), 0)
    for b in range(n_blk):
        m0, m1 = picks(b)
        base = rank[:, b * 128:(b + 1) * 128] + row_start
        p0 = jnp.sum(jnp.where(m0, base, 0.0), axis=0, keepdims=True)
        p1 = jnp.sum(jnp.where(m1, base, 0.0), axis=0, keepdims=True)
        pos_ref[:, b * 128:(b + 1) * 128] = jnp.where(sub == 0, p0, jnp.where(sub == 1, p1, 0.0)).astype(jnp.int32)

    diag = (lax.broadcasted_iota(jnp.int32, (N_EXPERTS, 128), 0)
            == lax.broadcasted_iota(jnp.int32, (N_EXPERTS, 128), 1))
    first = jnp.sum(jnp.where(diag, tile_start, 0.0), axis=0, keepdims=True)
    count = jnp.sum(jnp.where(diag, tiles, 0.0), axis=0, keepdims=True)
    rows = jnp.where(sub == PLAN_FIRST_TILE, first, jnp.where(sub == PLAN_TILES, count, 0.0))
    plan_ref[...] = rows.astype(jnp.int32)


def _slot_code(t, k):
    return t * SUBS + k * (SUBS // 2)


def _code_offset(code):
    return pl.multiple_of(code & ~(SUBS - 1), SUBS)


def _code_gate_index(code):
    return code >> 2


PAD_CODE = T * SUBS


def _invert_slots(pos_ref, first_ref, count_ref, code_ref):
    def pad_tile(tile, carry):
        for u in range(MOE_TILE):
            code_ref[tile * MOE_TILE + u] = PAD_CODE
        return carry

    def pad_last_tile(e, carry):
        return pad_tile(jnp.maximum(first_ref[e] + count_ref[e] - 1, 0), carry)
    lax.fori_loop(0, N_EXPERTS, pad_last_tile, 0)
    lax.fori_loop(first_ref[N_EXPERTS - 1] + count_ref[N_EXPERTS - 1], MOE_TILES, pad_tile, 0)

    group = 16
    for k in range(2):
        def place(i, carry):
            t0 = i * group
            slots = [pos_ref[k * T + t0 + u] for u in range(group)]
            for u, s in enumerate(slots):
                code_ref[s] = _slot_code(t0 + u, k)
            return carry
        lax.fori_loop(0, T // group, place, 0)


def _route_plan(route_t):
    pos, plan = pl.pallas_call(
        _plan_kernel,
        out_shape=[jax.ShapeDtypeStruct((8, T), jnp.int32), jax.ShapeDtypeStruct((8, LANES), jnp.int32)],
        scratch_shapes=[pltpu.VMEM((N_EXPERTS, T), f32)],
        compiler_params=_cp(None), name="route_plan",
    )(route_t)
    gates = jnp.pad(route_t[2:4].T.reshape(2 * T), (0, 8))
    return plan[PLAN_FIRST_TILE, :N_EXPERTS], plan[PLAN_TILES, :N_EXPERTS], pos[0:2].reshape(2 * T), gates


def _tile_index(i):
    return jnp.minimum(i, MOE_TILES - 1)


TM_ROWS = T * SUBS
SCATTER_GROUP = 8
STAGES = 4
LAST_EXPERT = N_EXPERTS - 1


def _tile_rows(g):
    return pl.ds(pl.multiple_of(g * MOE_TILE, MOE_TILE), MOE_TILE)


def _expert_tile_pairs(first, count, tile_step):
    def pair(pp, carry):
        for parity in range(2):
            g = 2 * pp + parity
            pl.when((g >= first) & (g < first + count))(functools.partial(tile_step, g, parity))
        return carry
    lax.fori_loop(first // 2, (first + count + 1) // 2, pair, 0)


def _gather_tile(code_ref, tile, xs, gbuf):
    base = tile * MOE_TILE
    for r in range(MOE_TILE):
        gbuf[r * SUBS:(r + 1) * SUBS, :] = xs[pl.ds(_code_offset(code_ref[base + r]), SUBS), :]


def _moe_up_kernel(first_ref, count_ref, pos_ref, h_hbm, w1_ref, w3_ref, hh_hbm, code_ref,
                   xs, gbuf_a, gbuf_b, w13, obuf, sem_x, sem_o):
    e = pl.program_id(0)
    first, count = first_ref[e], count_ref[e]
    gbufs = (gbuf_a, gbuf_b)

    def out_copy(slot, g):
        return pltpu.make_async_copy(obuf.at[slot], hh_hbm.at[_tile_rows(g), :], sem_o.at[slot])

    @pl.when(e == 0)
    def _():
        cp = pltpu.make_async_copy(h_hbm, xs.at[pl.ds(0, TM_ROWS), :], sem_x)
        cp.start()
        _invert_slots(pos_ref, first_ref, count_ref, code_ref)
        xs[TM_ROWS:TM_ROWS + SUBS, :] = jnp.zeros((SUBS, LANES), f32)
        cp.wait()
        _gather_tile(code_ref, 0, xs, gbuf_a)

    @pl.when(count > 0)
    def _():
        w13[:, :D_EXPERT] = w1_ref[0].astype(bf16)
        w13[:, D_EXPERT:] = w3_ref[0].astype(bf16)

    def tile_step(g, parity):
        _gather_tile(code_ref, _tile_index(g + 1), xs, gbufs[1 - parity])
        x3 = jnp.swapaxes(gbufs[parity][...].reshape(MOE_TILE, SUBS, LANES), 0, 1)
        x = jnp.concatenate([x3[s] for s in range(SUBS)], axis=1).astype(bf16)
        h13 = jnp.dot(x, w13[...], preferred_element_type=f32)
        hh = (_silu(h13[:, :D_EXPERT]) * h13[:, D_EXPERT:]).astype(bf16)

        slot = g % STAGES

        @pl.when(g >= STAGES)
        def _():
            out_copy(slot, g).wait()
        obuf[slot] = hh
        out_copy(slot, g).start()

    _expert_tile_pairs(first, count, tile_step)

    @pl.when(e == LAST_EXPERT)
    def _():
        n_used = first + count
        for slot in range(STAGES):
            pl.when(n_used > slot)(lambda slot=slot: out_copy(slot, 0).wait())
        obuf[0] = jnp.zeros((MOE_TILE, D_EXPERT), bf16)

        def zero_tile(g, carry):
            cp = out_copy(0, g)
            cp.start()
            cp.wait()
            return carry
        lax.fori_loop(n_used, MOE_TILES, zero_tile, 0)


def _scatter_tile(code_ref, gate_ref, tile, ybuf, acc):
    base = tile * MOE_TILE
    for g0 in range(0, MOE_TILE, SCATTER_GROUP):
        rows = range(g0, g0 + SCATTER_GROUP)
        codes = [code_ref[base + r] for r in rows]
        new = [acc[pl.ds(_code_offset(c), SUBS), :]
               + gate_ref[_code_gate_index(c)] * ybuf[r * SUBS:(r + 1) * SUBS, :]
               for r, c in zip(rows, codes)]
        for c, v in zip(codes, new):
            acc[pl.ds(_code_offset(c), SUBS), :] = v


RES_ROWS = 256


def _residual_out(x_hbm, y_hbm, tok0, sample_of_chunk, mod_ref, acc, rin, rout, sem_r, sem_w):
    n_chunks = x_hbm.shape[0] // RES_ROWS

    def rows(c):
        return pl.ds(pl.multiple_of(c * RES_ROWS, RES_ROWS), RES_ROWS)

    def in_copy(slot, c):
        return pltpu.make_async_copy(x_hbm.at[rows(c), :], rin.at[slot], sem_r.at[slot])

    def out_copy(slot, c):
        return pltpu.make_async_copy(rout.at[slot], y_hbm.at[rows(c), :], sem_w.at[slot])

    for c in range(STAGES - 1):
        in_copy(c, c).start()

    def ring(cc, carry):
        for slot in range(STAGES):
            c = STAGES * cc + slot
            in_copy(slot, c).wait()
            ahead = c + STAGES - 1

            @pl.when(ahead < n_chunks)
            def _():
                in_copy((slot + STAGES - 1) % STAGES, ahead).start()

            @pl.when(c >= STAGES)
            def _():
                out_copy(slot, c).wait()
            delta = _load_token_major(acc, RES_ROWS, tok0 + c * RES_ROWS)
            gate = mod_ref[pl.ds(sample_of_chunk(c) * MOD_ROWS + GATE2, 1), :]
            rout[slot] = rin[slot] + gate * delta
            out_copy(slot, c).start()
        return carry

    lax.fori_loop(0, n_chunks // STAGES, ring, 0)
    for slot in range(STAGES):
        out_copy(slot, 0).wait()


def _moe_down_kernel(first_ref, count_ref, code_ref, gate_ref, hh_hbm, w2_ref, xc_hbm, xl_hbm, mod_ref,
                     yc_hbm, yl_hbm, acc, ybuf_a, ybuf_b, w2b, ibuf, rin, rout, sem_i, sem_r, sem_w):
    e = pl.program_id(0)
    first, count = first_ref[e], count_ref[e]
    n_used = first_ref[LAST_EXPERT] + count_ref[LAST_EXPERT]
    ybufs = (ybuf_a, ybuf_b)

    def in_copy(slot, g):
        return pltpu.make_async_copy(hh_hbm.at[_tile_rows(g), :], ibuf.at[slot], sem_i.at[slot])

    @pl.when(e == 0)
    def _():
        for g in range(STAGES - 1):
            in_copy(g, g).start()

        def zero(c, carry):
            acc[pl.ds(pl.multiple_of(c * 1024, 1024), 1024), :] = jnp.zeros((1024, LANES), f32)
            return carry
        lax.fori_loop(0, TM_ROWS // 1024, zero, 0)
        acc[TM_ROWS:TM_ROWS + SUBS, :] = jnp.zeros((SUBS, LANES), f32)
        ybuf_b[...] = jnp.zeros_like(ybuf_b)

    @pl.when(count > 0)
    def _():
        w2b[...] = w2_ref[0].astype(bf16)

    def tile_step(g, parity):
        slot = g % STAGES
        in_copy(slot, g).wait()
        ahead = g + STAGES - 1

        @pl.when(ahead < n_used)
        def _():
            in_copy(ahead % STAGES, ahead).start()
        _store_token_major(ybufs[parity], jnp.dot(ibuf[slot], w2b[...], preferred_element_type=f32))
        _scatter_tile(code_ref, gate_ref, jnp.maximum(g - 1, 0), ybufs[1 - parity], acc)

    _expert_tile_pairs(first, count, tile_step)

    @pl.when(e == LAST_EXPERT)
    def _():
        for parity in range(2):
            pl.when((n_used > 0) & ((n_used - 1) % 2 == parity))(
                functools.partial(_scatter_tile, code_ref, gate_ref, n_used - 1, ybufs[parity], acc))
        _residual_out(xc_hbm, yc_hbm, 0, lambda c: 0, mod_ref, acc, rin, rout, sem_r, sem_w)
        _residual_out(xl_hbm, yl_hbm, T_CTX, lambda c: 1 + c // (DEC_SEQ // RES_ROWS),
                      mod_ref, acc, rin, rout, sem_r, sem_w)


def _moe(h_tm, route_t, p, xc, xl, mods):
    first_tile, n_tiles, slots, gates = _route_plan(route_t)
    tile_rows = pltpu.VMEM((MOE_TILE * SUBS, LANES), f32)
    staging = pltpu.VMEM((STAGES, MOE_TILE, D_EXPERT), bf16)
    res_rows = pltpu.VMEM((STAGES, RES_ROWS, D), f32)
    hbm = pl.BlockSpec(memory_space=pl.ANY)
    hh, codes = pl.pallas_call(
        _moe_up_kernel,
        grid_spec=pltpu.PrefetchScalarGridSpec(
            num_scalar_prefetch=3, grid=(N_EXPERTS,),
            in_specs=[hbm,
                      pl.BlockSpec((1, D, D_EXPERT), lambda e, f, n, s: (e, 0, 0)),
                      pl.BlockSpec((1, D, D_EXPERT), lambda e, f, n, s: (e, 0, 0))],
            out_specs=[hbm, pl.BlockSpec(memory_space=pltpu.SMEM)],
            scratch_shapes=[pltpu.VMEM((TM_ROWS + SUBS, LANES), f32), tile_rows, tile_rows,
                            pltpu.VMEM((D, 2 * D_EXPERT), bf16), staging,
                            pltpu.SemaphoreType.DMA(()), pltpu.SemaphoreType.DMA((STAGES,))]),
        out_shape=[jax.ShapeDtypeStruct((MOE_ROWS, D_EXPERT), bf16),
                   jax.ShapeDtypeStruct((MOE_ROWS,), jnp.int32)],
        compiler_params=_cp(("arbitrary",)), name="moe_up",
    )(first_tile, n_tiles, slots, h_tm, p['w1'], p['w3'])
    return pl.pallas_call(
        _moe_down_kernel,
        grid_spec=pltpu.PrefetchScalarGridSpec(
            num_scalar_prefetch=4, grid=(N_EXPERTS,),
            in_specs=[hbm, pl.BlockSpec((1, D_EXPERT, D), lambda e, f, n, c, g: (e, 0, 0)), hbm, hbm,
                      pl.BlockSpec((N_SAMPLES * MOD_ROWS, D), lambda e, f, n, c, g: (0, 0),
                                   pipeline_mode=pl.Buffered(1))],
            out_specs=[hbm, hbm],
            scratch_shapes=[pltpu.VMEM((TM_ROWS + SUBS, LANES), f32), tile_rows, tile_rows,
                            pltpu.VMEM((D_EXPERT, D), bf16), staging, res_rows, res_rows,
                            pltpu.SemaphoreType.DMA((STAGES,)), pltpu.SemaphoreType.DMA((STAGES,)),
                            pltpu.SemaphoreType.DMA((STAGES,))]),
        out_shape=[jax.ShapeDtypeStruct((T_CTX, D), f32), jax.ShapeDtypeStruct((T_LAT, D), f32)],
        compiler_params=_cp(("arbitrary",)), name="moe_down",
    )(first_tile, n_tiles, codes, gates, hh, p['w2'], xc, xl, mods)


def kernel(x_prompt, x_sample, cache_l0_k, cache_l0_v, cache_l1_ckv, cache_l1_kpe, c, c_ctx, l0_g_norm1, l0_g_norm2, l0_w_ada, l0_b_ada, l0_w_in, l0_g_vnorm, l0_w_s, l0_b_s, l0_g_q, l0_g_k, l0_sink, l0_w_o, l0_w_rg, l0_b_rg, l0_w_re, l0_b_re, l0_w1, l0_w3, l0_w2, l1_g_norm1, l1_g_norm2, l1_w_ada, l1_b_ada, l1_w_in, l1_g_qa, l1_w_uq, l1_g_kva, l1_w_ukv, l1_g_q, l1_g_k, l1_w_o, l1_w_rg, l1_b_rg, l1_w_re, l1_b_re, l1_w1, l1_w3, l1_w2):
    p0 = dict(w_in=l0_w_in, g_vnorm=l0_g_vnorm, w_s=l0_w_s, b_s=l0_b_s, g_q=l0_g_q, g_k=l0_g_k, sink=l0_sink,
              w_o=l0_w_o, w_rg=l0_w_rg, b_rg=l0_b_rg, w_re=l0_w_re, b_re=l0_b_re, w1=l0_w1, w3=l0_w3, w2=l0_w2)
    p1 = dict(w_in=l1_w_in, g_qa=l1_g_qa, w_uq=l1_w_uq, g_kva=l1_g_kva, w_ukv=l1_w_ukv, g_q=l1_g_q, g_k=l1_g_k,
              w_o=l1_w_o, w_rg=l1_w_rg, b_rg=l1_b_rg, w_re=l1_w_re, b_re=l1_b_re, w1=l1_w1, w3=l1_w3, w2=l1_w2)

    cond8 = jnp.zeros((MOD_ROWS, D), f32).at[0].set(c_ctx).at[1:1 + DEC_BATCH].set(c)
    mods0 = _mod_rows(_adaln(cond8, l0_w_ada, l0_b_ada))
    mods1 = _mod_rows(_adaln(cond8, l1_w_ada, l1_b_ada))

    xc0 = x_prompt.reshape(T_CTX, D)
    xl0 = x_sample.reshape(T_LAT, D)

    xc0m, xl0m, k_new, v_new = _l0_mixer(xc0, xl0, l0_g_norm1, mods0, p0, cache_l0_k, cache_l0_v)
    h0, route0 = _router(xc0m, xl0m, l0_g_norm2, mods0, p0)
    xc1, xl1 = _moe(h0, route0, p0, xc0m, xl0m, mods0)

    xc1m, xl1m, ckv_new, kpe_new = _l1_mixer(xc1, xl1, l1_g_norm1, mods1, p1, cache_l1_ckv, cache_l1_kpe)
    h1, route1 = _router(xc1m, xl1m, l1_g_norm2, mods1, p1)
    y_prompt, y_sample = _moe(h1, route1, p1, xc1m, xl1m, mods1)
    return (y_prompt.reshape(BATCH, SEQ, D), y_sample.reshape(DEC_BATCH, DEC_SEQ, D), k_new, v_new,
            ckv_new.reshape(BATCH, SEQ, C_KV_LORA), kpe_new.reshape(BATCH, SEQ, C_ROPE))
```

```python
import functools

import jax
import jax.numpy as jnp
import numpy as np
from jax import lax
from jax.experimental import pallas as pl
from jax.experimental.pallas import tpu as pltpu

f32 = jnp.float32
bf16 = jnp.bfloat16

D = 1024
BATCH, SEQ = 32, 256
DEC_BATCH, DEC_SEQ = 2, 1024
PAST = 512
T_CTX = BATCH * SEQ
T_LAT = DEC_BATCH * DEC_SEQ
T = T_CTX + T_LAT
GRID_W = 64
CHUNK = 128
WINDOW = 128
ROPE_THETA = 10000.0
EPS = 1e-6
NEG_INF = -1e30
LANES = 128
SUBS = D // LANES

A_WIDTH = 512
A_GROUPS = 4
B_HEADS, B_KV, B_GROUP, B_HD = 8, 2, 4, 64
LOG2E = 1.4426950408889634
B_SCALE = B_HD ** -0.5 * LOG2E

C_HEADS, C_Q_LORA, C_KV_LORA, C_NOPE, C_ROPE, C_V = 16, 384, 256, 64, 32, 64
C_QK = C_NOPE + C_ROPE
C_SCALE = C_QK ** -0.5 * LOG2E
ODD_IN_PAD = 768
SLOT = 128

N_GROUPS, N_EXPERTS, D_EXPERT = 4, 32, 256

N_SAMPLES = 1 + DEC_BATCH
MOD_ROWS = 8
SHIFT1, SCALE1, GATE1, SHIFT2, SCALE2, GATE2 = range(6)

ROW_TILE = 1024
BLOCK_ROWS = 1024
PROJ_ROWS = 512
ADALN_COLS = 1536
MOE_TILE = 256
MOE_ROWS = 2 * T + N_EXPERTS * MOE_TILE
MOE_TILES = MOE_ROWS // MOE_TILE
VMEM_CAP = 56 * 1024 * 1024


def _cp(sem, vmem=VMEM_CAP):
    return pltpu.CompilerParams(dimension_semantics=sem, vmem_limit_bytes=vmem)


def _const_spec(shape):
    nd = len(shape)
    return pl.BlockSpec(shape, lambda *_: (0,) * nd, pipeline_mode=pl.Buffered(1))


def _sample_of_tile(i, tile):
    n_ctx = T_CTX // tile
    per_lat = DEC_SEQ // tile
    return jnp.where(i < n_ctx, 0, 1 + (i - n_ctx) // per_lat)


def _mod(mod_ref, row):
    return mod_ref[row:row + 1, :]


def _silu(x):
    return x * jax.nn.sigmoid(x)


def _rms_rows(x, g):
    return x * lax.rsqrt(jnp.mean(x * x, -1, keepdims=True) + EPS) * g


def _swap_pairs(x):
    lane = lax.broadcasted_iota(jnp.int32, x.shape, x.ndim - 1)
    nxt = pltpu.roll(x, x.shape[-1] - 1, x.ndim - 1)
    prv = pltpu.roll(x, 1, x.ndim - 1)
    return jnp.where((lane & 1) == 0, nxt, prv)


def _split_bf16(x):
    hi = x.astype(bf16)
    return hi, (x - hi.astype(f32)).astype(bf16)


def _adaln_kernel(c_ref, w_ref, b_ref, o_ref):
    s_hi, s_lo = _split_bf16(_silu(c_ref[...]))
    w_hi, w_lo = _split_bf16(w_ref[...])
    o_ref[...] = (jnp.dot(s_hi, w_hi, preferred_element_type=f32) + jnp.dot(s_lo, w_hi, preferred_element_type=f32)
                  + jnp.dot(s_hi, w_lo, preferred_element_type=f32) + b_ref[...])


def _adaln(cond8, w, b):
    n = w.shape[1]
    return pl.pallas_call(
        _adaln_kernel,
        grid=(n // ADALN_COLS,),
        in_specs=[_const_spec((MOD_ROWS, D)), pl.BlockSpec((D, ADALN_COLS), lambda j: (0, j)),
                  pl.BlockSpec((1, ADALN_COLS), lambda j: (0, j))],
        out_specs=pl.BlockSpec((MOD_ROWS, ADALN_COLS), lambda j: (0, j)),
        out_shape=jax.ShapeDtypeStruct((MOD_ROWS, n), f32),
        compiler_params=_cp(("arbitrary",)),
        name="adaln",
    )(cond8, w, b.reshape(1, n))


def _mod_rows(m8):
    m = m8[:N_SAMPLES].reshape(N_SAMPLES, 6, D)
    return jnp.pad(m, ((0, 0), (0, MOD_ROWS - 6), (0, 0))).reshape(N_SAMPLES * MOD_ROWS, D)


N_CTX_TILES = T_CTX // ROW_TILE


def _token_specs(width):
    return [pl.BlockSpec((ROW_TILE, width), lambda i: (jnp.minimum(i, N_CTX_TILES - 1), 0)),
            pl.BlockSpec((ROW_TILE, width), lambda i: (jnp.maximum(i - N_CTX_TILES, 0), 0))]


def _token_rows(xc_ref, xl_ref):
    return jnp.where(pl.program_id(0) < N_CTX_TILES, xc_ref[...], xl_ref[...])


def _store_token_major(ref, x):
    n = x.shape[0]
    for s in range(SUBS):
        ref[pl.ds(s, n, stride=SUBS), :] = x[:, s * LANES:(s + 1) * LANES]


def _load_token_major(ref, n, row0=0):
    return jnp.concatenate([ref[pl.ds(row0 * SUBS + s, n, stride=SUBS), :] for s in range(SUBS)], axis=1)


def _token_major_spec(rows, index_map):
    return pl.BlockSpec((rows * SUBS, LANES), index_map)


def _rope_tables(n, rot_dim, lanes, lane0, copies=1):
    rows_count = n // GRID_W
    rows = np.repeat(np.arange(rows_count), GRID_W).astype(np.float64)
    cols = np.tile(np.arange(GRID_W), rows_count).astype(np.float64)
    d_axis = rot_dim // 2
    inv = ROPE_THETA ** (-np.arange(0, d_axis, 2, dtype=np.float64) / d_axis)
    ang = np.concatenate([rows[:, None] * inv, cols[:, None] * inv], -1)
    c = np.ones((n, lanes), np.float32)
    s = np.zeros((n, lanes), np.float32)
    for j in range(copies):
        lo = lane0 + j * rot_dim
        c[:, lo:lo + rot_dim] = np.repeat(np.cos(ang), 2, axis=1)
        s[:, lo:lo + rot_dim] = np.repeat(np.sin(ang), 2, axis=1) * np.tile(np.array([-1.0, 1.0]), rot_dim // 2)
    return jnp.asarray(c), jnp.asarray(s)


L0_Q0 = 2 * A_WIDTH
L0_K0 = L0_Q0 + B_GROUP * LANES
L0_V0 = L0_K0 + B_KV * B_HD
L0_IN = L0_V0 + B_KV * B_HD


def _l0_kernel(*refs, latent):
    if latent:
        (sink_ref, x_ref, gn_ref, mod_ref, win_ref, gvn_ref, ws_ref, bsb_ref, gq_ref, gk_ref, wo_ref,
         cos_ref, sin_ref, kc_ref, vc_ref, xo_ref, zs, cat, qs, ks, vt, kcb, vct) = refs
        key_off = WINDOW
    else:
        (sink_ref, x_ref, gn_ref, mod_ref, win_ref, gvn_ref, ws_ref, bsb_ref, gq_ref, gk_ref, wo_ref,
         xo_ref, ko_ref, vo_ref, zs, cat, qs, ks, vt, kf, vf) = refs
        key_off = 0
    n = BLOCK_ROWS
    n_chunks = n // CHUNK
    low = lax.broadcasted_iota(jnp.int32, (CHUNK, LANES), 1) < B_HD

    if latent:
        zpad = jnp.zeros((WINDOW, LANES), bf16)
        for c0 in (0, 1 + n_chunks):
            ks[c0 * CHUNK:(c0 + 1) * CHUNK, :] = zpad
            vt[c0] = zpad
        kcb[...] = kc_ref[...].astype(bf16)
        for i in range(PAST // CHUNK):
            vct[i] = vc_ref[i * CHUNK:(i + 1) * CHUNK, :].T.astype(bf16)

    def project(c, carry):
        r = pl.ds(pl.multiple_of(c * PROJ_ROWS, PROJ_ROWS), PROJ_ROWS)
        h = _rms_rows(x_ref[r, :], gn_ref[...]) * (1.0 + _mod(mod_ref, SCALE1)) + _mod(mod_ref, SHIFT1)
        zs[r, :] = jnp.dot(h.astype(bf16), win_ref[...], preferred_element_type=f32)
        return carry

    lax.fori_loop(0, n // PROJ_ROWS, project, 0)

    def prepare(c, carry):
        r = pl.ds(pl.multiple_of(c * CHUNK, CHUNK), CHUNK)
        u = jax.nn.gelu(zs[r, 0:A_WIDTH])
        v = jax.nn.gelu(zs[r, A_WIDTH:2 * A_WIDTH])
        mu = jnp.mean(v, -1, keepdims=True)
        var = jnp.mean(jnp.square(v - mu), -1, keepdims=True)
        vn = ((v - mu) * lax.rsqrt(var + EPS) * gvn_ref[...]).astype(bf16)
        for g in range(A_GROUPS):
            cs = slice(g * CHUNK, (g + 1) * CHUNK)
            mixed = jnp.dot(ws_ref[g], vn[:, cs], preferred_element_type=f32) + bsb_ref[g]
            cat[r, cs] = (u[:, cs] * mixed).astype(bf16)
        if latent:
            cs_, sn_ = cos_ref[r, :], sin_ref[r, :]
        def half_norm(v, gain):
            sq = v * v
            s0 = jnp.sum(jnp.where(low, sq, 0.0), -1, keepdims=True)
            s1 = jnp.sum(jnp.where(low, 0.0, sq), -1, keepdims=True)
            return v * lax.rsqrt(jnp.where(low, s0, s1) * (1.0 / B_HD) + EPS) * gain

        for j in range(B_GROUP):
            js = slice(j * LANES, (j + 1) * LANES)
            qj = half_norm(zs[r, L0_Q0 + j * LANES:L0_Q0 + (j + 1) * LANES], gq_ref[...])
            if latent:
                qj = qj * cs_ + _swap_pairs(qj) * sn_
            qs[r, js] = qj.astype(bf16)
        k = half_norm(zs[r, L0_K0:L0_K0 + LANES], gk_ref[...])
        vv = zs[r, L0_V0:L0_V0 + LANES]
        if latent:
            k = k * cs_ + _swap_pairs(k) * sn_
        else:
            kf[r, :] = k
            vf[r, :] = vv
        kr = pl.ds(pl.multiple_of(c * CHUNK + key_off, CHUNK), CHUNK)
        ks[kr, :] = k.astype(bf16)
        vt[c + key_off // CHUNK] = vv.T.astype(bf16)
        return carry

    lax.fori_loop(0, n_chunks // 2, lambda i, carry: prepare(2 * i + 1, prepare(2 * i, carry)), 0)

    def attend(jobs):
        scored = []
        for r, rows, key_sets in jobs:
            low_q = lax.broadcasted_iota(jnp.int32, (rows, LANES), 1) < B_HD
            slots = [qs[r, j * LANES:(j + 1) * LANES] for j in range(B_GROUP)]
            zero = jnp.zeros((rows, LANES), bf16)
            q = jnp.concatenate([jnp.where(low_q, qj, zero) for qj in slots]
                                + [jnp.where(low_q, zero, qj) for qj in slots], axis=0)
            sk = jnp.concatenate([jnp.full((1, rows), sink_ref[h] * LOG2E, f32) for h in range(B_HEADS)], axis=1)
            scores = []
            m = sk
            for k, _, keep in key_sets:
                s = lax.dot_general(k, q, (((1,), (1,)), ((), ())), preferred_element_type=f32)
                if keep is not None:
                    s = jnp.where(keep, s, NEG_INF)
                scores.append(s)
                m = jnp.maximum(m, jnp.max(s, 0, keepdims=True))
            scored.append((sk, scores, m))
        outs = []
        for (r, rows, key_sets), (sk, scores, m) in zip(jobs, scored):
            den = jnp.exp2(sk - m)
            ot = None
            for s, (_, vts, _) in zip(scores, key_sets):
                e = jnp.exp2(s - m)
                den = den + jnp.sum(e, 0, keepdims=True)
                eb = e.astype(bf16)
                for i, v_t in enumerate(vts):
                    pv = jnp.dot(v_t, eb[i * CHUNK:(i + 1) * CHUNK, :], preferred_element_type=f32)
                    ot = pv if ot is None else ot + pv
            outs.append(ot * (1.0 / den))
        for (r, rows, _), ot in zip(jobs, outs):
            for pair in range(B_HEADS // 2):
                f0 = (2 * pair // B_GROUP) * B_HD
                pair_t = jnp.concatenate([ot[f0:f0 + B_HD, 2 * pair * rows:(2 * pair + 1) * rows],
                                          ot[f0:f0 + B_HD, (2 * pair + 1) * rows:(2 * pair + 2) * rows]], axis=0)
                cat[r, A_WIDTH + pair * LANES:A_WIDTH + (pair + 1) * LANES] = pair_t.T.astype(bf16)

    jobs_per_step = 2 if latent else 4
    if latent:
        span = CHUNK + 2 * WINDOW

        def block_job(c):
            start = pl.multiple_of(c * CHUNK, CHUNK)
            kr = pl.ds(start, span)
            kj = lax.broadcasted_iota(jnp.int32, (span, B_HEADS * CHUNK), 0)
            qi = lax.broadcasted_iota(jnp.int32, (span, B_HEADS * CHUNK), 1) & (CHUNK - 1)
            kpos = start - WINDOW + kj
            keep = (jnp.abs(kj - WINDOW - qi) <= WINDOW) & (kpos >= 0) & (kpos < n)
            return (pl.ds(start, CHUNK), CHUNK,
                    [(ks[kr, :], [vt[c + i] for i in range(span // CHUNK)], keep),
                     (kcb[...], [vct[i] for i in range(PAST // CHUNK)], None)])

        def attend_blocks(i, carry):
            attend([block_job(i * jobs_per_step + j) for j in range(jobs_per_step)])
            return carry

        lax.fori_loop(0, n_chunks // jobs_per_step, attend_blocks, 0)
    else:
        def seq_job(sq):
            r = pl.ds(pl.multiple_of(sq * SEQ, SEQ), SEQ)
            return r, SEQ, [(ks[r, :], [vt[sq * (SEQ // CHUNK) + i] for i in range(SEQ // CHUNK)], None)]

        def attend_seqs(i, carry):
            seqs = [i * jobs_per_step + j for j in range(jobs_per_step)]
            attend([seq_job(sq) for sq in seqs])
            for sq in seqs:
                r = pl.ds(pl.multiple_of(sq * SEQ, SEQ), SEQ)
                ko_ref[sq] = kf[r, :].T
                vo_ref[sq] = vf[r, :].T
            return carry

        lax.fori_loop(0, n // SEQ // jobs_per_step, attend_seqs, 0)

    def output(c, carry):
        r = pl.ds(pl.multiple_of(c * PROJ_ROWS, PROJ_ROWS), PROJ_ROWS)
        y = jnp.dot(cat[r, :], wo_ref[...], preferred_element_type=f32)
        xo_ref[r, :] = x_ref[r, :] + _mod(mod_ref, GATE1) * y
        return carry

    lax.fori_loop(0, n // PROJ_ROWS, output, 0)


def _l0_mixer(xc, xl, g_norm, mods, p, cache_k, cache_v):
    w = p['w_in']
    q_slots = w[:, L0_Q0:L0_Q0 + B_HEADS * B_HD].reshape(D, B_KV, B_GROUP, B_HD).transpose(0, 2, 1, 3).reshape(
        D, B_GROUP * LANES)
    win = jnp.concatenate([w[:, :L0_Q0], q_slots, w[:, L0_Q0 + B_HEADS * B_HD:]], axis=1).astype(bf16)
    gq = jnp.tile(p['g_q'], B_KV).reshape(1, LANES) * B_SCALE
    gk = jnp.tile(p['g_k'], B_KV).reshape(1, LANES)
    gvn = p['g_vnorm'].reshape(1, A_WIDTH)
    ws = p['w_s'].astype(bf16)
    bsb = jnp.broadcast_to(p['b_s'][:, :, None], (A_GROUPS, CHUNK, CHUNK))
    wo = p['w_o'].astype(bf16)
    weights = (g_norm.reshape(1, D),)
    consts = (win, gvn, ws, bsb, gq, gk, wo)
    c_specs = [_const_spec(a.shape) for a in consts]
    smem = pl.BlockSpec(memory_space=pltpu.SMEM)
    row = pl.BlockSpec((BLOCK_ROWS, D), lambda b: (b, 0))
    kv = pl.BlockSpec((BLOCK_ROWS // SEQ, LANES, SEQ), lambda b: (b, 0, 0))

    def scratch(pad):
        return [pltpu.VMEM((BLOCK_ROWS, L0_IN), f32), pltpu.VMEM((BLOCK_ROWS, D), bf16),
                pltpu.VMEM((BLOCK_ROWS, B_GROUP * LANES), bf16), pltpu.VMEM((BLOCK_ROWS + pad, LANES), bf16),
                pltpu.VMEM(((BLOCK_ROWS + pad) // CHUNK, LANES, CHUNK), bf16)]

    kv_shape = jax.ShapeDtypeStruct((BATCH, LANES, SEQ), f32)
    state = pltpu.VMEM((BLOCK_ROWS, LANES), f32)
    xo_ctx, k_t, v_t = pl.pallas_call(
        functools.partial(_l0_kernel, latent=False),
        grid=(T_CTX // BLOCK_ROWS,),
        in_specs=[smem, row, _const_spec((1, D)), pl.BlockSpec((MOD_ROWS, D), lambda b: (0, 0))] + c_specs,
        out_specs=[row, kv, kv],
        out_shape=[jax.ShapeDtypeStruct((T_CTX, D), f32), kv_shape, kv_shape],
        scratch_shapes=scratch(0) + [state, state],
        compiler_params=_cp(("parallel",)), name="l0_mixer_ctx",
    )(p['sink'], xc, *weights, mods, *consts)
    k_new = k_t.reshape(BATCH, B_KV, B_HD, SEQ).transpose(0, 3, 1, 2)
    v_new = v_t.reshape(BATCH, B_KV, B_HD, SEQ).transpose(0, 3, 1, 2)

    cos, sin = _rope_tables(DEC_SEQ, B_HD, LANES, 0, copies=LANES // B_HD)
    cache = pl.BlockSpec((None, PAST, LANES), lambda b: (b, 0, 0))
    past = [pltpu.VMEM((PAST, LANES), bf16), pltpu.VMEM((PAST // CHUNK, LANES, CHUNK), bf16)]
    xo_lat = pl.pallas_call(
        functools.partial(_l0_kernel, latent=True),
        grid=(DEC_BATCH,),
        in_specs=[smem, row, _const_spec((1, D)), pl.BlockSpec((MOD_ROWS, D), lambda b: (1 + b, 0))] + c_specs + [
                  _const_spec(cos.shape), _const_spec(sin.shape), cache, cache],
        out_specs=row,
        out_shape=jax.ShapeDtypeStruct((T_LAT, D), f32),
        scratch_shapes=scratch(2 * WINDOW) + past,
        compiler_params=_cp(("parallel",)), name="l0_mixer_lat",
    )(p['sink'], xl, *weights, mods, *consts, cos, sin,
      cache_k.reshape(DEC_BATCH, PAST, LANES), cache_v.reshape(DEC_BATCH, PAST, LANES))
    return xo_ctx, xo_lat, k_new, v_new


C_SLOTS = C_HEADS * SLOT
C_PAIRS = C_HEADS // 2
L1_ROWS = 256


def _l1_kernel(*refs, latent):
    if latent:
        (x_ref, gn_ref, mod_ref, win_ref, gqa_ref, wuq_ref, gq_ref, gkva_ref, wuk_ref, wuvt_ref, gk_ref,
         wo_ref, wuqs_ref, qcos_ref, qsin_ref, kcos_ref, ksin_ref, cckv_ref, ckpe_ref, xo_ref,
         zs, cat, qs, ks, vt, wide, wide2) = refs
        n_ctx = PAST
    else:
        (x_ref, gn_ref, mod_ref, win_ref, gqa_ref, wuq_ref, gq_ref, gkva_ref, wuk_ref, wuvt_ref, gk_ref,
         wo_ref, xo_ref, ckvo_ref, kpeo_ref, zs, cat, qs, ks, vt, wide) = refs
        n_ctx = 0
    n = BLOCK_ROWS
    nt_dims = (((1,), (1,)), ((), ()))

    def inv_rms(v):
        return lax.rsqrt(jnp.sum(v * v, -1, keepdims=True) * (1.0 / C_QK) + EPS)

    def expand_keys(ckv_n, kslot, kb, rope_rows):
        cb = ckv_n.astype(bf16)
        key_rows = pl.ds(pl.multiple_of(kb * L1_ROWS, L1_ROWS), L1_ROWS)
        wide[...] = jnp.dot(cb, wuk_ref[...], preferred_element_type=f32)
        if rope_rows is not None:
            kcos = kcos_ref[rope_rows, :]
            turned = _swap_pairs(kslot) * ksin_ref[rope_rows, :]
        for h in range(C_HEADS):
            kh = wide[:, h * SLOT:(h + 1) * SLOT] + kslot
            if rope_rows is not None:
                kh = inv_rms(kh) * (kh * kcos + turned)
            else:
                kh = kh * inv_rms(kh) * gk_ref[...]
            ks[h, key_rows, :] = kh.astype(bf16)
        v_t = lax.dot_general(wuvt_ref[...], cb, nt_dims, preferred_element_type=f32).astype(bf16)
        for pair in range(C_PAIRS):
            vt[pair, kb] = v_t[pair * LANES:(pair + 1) * LANES, :]

    if latent:
        def past_keys(c, carry):
            r = pl.ds(pl.multiple_of(c * L1_ROWS, L1_ROWS), L1_ROWS)
            expand_keys(cckv_ref[r, :], ckpe_ref[r, :], c, None)
            return carry

        lax.fori_loop(0, PAST // L1_ROWS, past_keys, 0)

    def project(c, carry):
        r = pl.ds(pl.multiple_of(c * PROJ_ROWS, PROJ_ROWS), PROJ_ROWS)
        h = _rms_rows(x_ref[r, :], gn_ref[...]) * (1.0 + _mod(mod_ref, SCALE1)) + _mod(mod_ref, SHIFT1)
        zs[r, :] = jnp.dot(h.astype(bf16), win_ref[...], preferred_element_type=f32)
        return carry

    lax.fori_loop(0, n // PROJ_ROWS, project, 0)

    def prepare(c, carry):
        r = pl.ds(pl.multiple_of(c * L1_ROWS, L1_ROWS), L1_ROWS)
        qa = _rms_rows(zs[r, 0:C_Q_LORA], gqa_ref[...]).astype(bf16)
        wide[...] = jnp.dot(qa, wuq_ref[...], preferred_element_type=f32)
        if latent:
            wide2[...] = jnp.dot(qa, wuqs_ref[...], preferred_element_type=f32)
            qcos, qsin = qcos_ref[r, :], qsin_ref[r, :]
        for h in range(C_HEADS):
            hs = slice(h * SLOT, (h + 1) * SLOT)
            qh = wide[:, hs]
            if latent:
                qh = inv_rms(qh) * (qh * qcos + wide2[:, hs] * qsin)
            else:
                qh = qh * inv_rms(qh) * gq_ref[...]
            qs[h, r, :] = qh.astype(bf16)
        ckv_n = _rms_rows(zs[r, C_Q_LORA:C_Q_LORA + C_KV_LORA], gkva_ref[...])
        kslot = zs[r, C_Q_LORA + C_KV_LORA:ODD_IN_PAD]
        if not latent:
            ckvo_ref[r, :] = ckv_n
            kpeo_ref[c] = kslot.T[C_NOPE:C_QK, :]
        expand_keys(ckv_n, kslot, c + n_ctx // L1_ROWS, r if latent else None)
        return carry

    lax.fori_loop(0, n // L1_ROWS, prepare, 0)

    low = lax.broadcasted_iota(jnp.int32, (2 * C_V, L1_ROWS), 0) < C_V
    n_kblocks = (n_ctx + n) // L1_ROWS
    pairs_per_step = 4 if latent else 8

    blocks_per_step = 1 if latent else 2

    def attend(cc, carry):
        blocks = [cc * blocks_per_step + j for j in range(blocks_per_step)]
        rows = [pl.ds(pl.multiple_of(c * L1_ROWS, L1_ROWS), L1_ROWS) for c in blocks]

        def values_t(pair, c, eb):
            if not latent:
                return jnp.dot(vt[pair, c], eb, preferred_element_type=f32)
            o_t = None
            for b in range(n_kblocks):
                pv = jnp.dot(vt[pair, b], eb[b * L1_ROWS:(b + 1) * L1_ROWS, :], preferred_element_type=f32)
                o_t = pv if o_t is None else o_t + pv
            return o_t

        def pairs_step(i, carry2):
            pairs = [i * pairs_per_step + j for j in range(pairs_per_step)]
            units = [(c, r, 2 * p + hh) for c, r in zip(blocks, rows) for p in pairs for hh in range(2)]
            scores = [lax.dot_general(ks[h] if latent else ks[h, r, :], qs[h, r, :], nt_dims,
                                      preferred_element_type=f32) for _, r, h in units]
            exps = [jnp.exp2(s - jnp.max(s, 0, keepdims=True)) for s in scores]
            dens = [jnp.sum(e, 0, keepdims=True) for e in exps]
            outs = [values_t(h // 2, c, e.astype(bf16)) / den for (c, _, h), e, den in zip(units, exps, dens)]
            for u in range(0, len(units), 2):
                _, r, h = units[u]
                cat[h // 2, r, :] = jnp.where(low, outs[u], outs[u + 1]).T.astype(bf16)
            return carry2

        return lax.fori_loop(0, C_PAIRS // pairs_per_step, pairs_step, carry)

    lax.fori_loop(0, n // L1_ROWS // blocks_per_step, attend, 0)

    def output(c, carry):
        r = pl.ds(pl.multiple_of(c * PROJ_ROWS, PROJ_ROWS), PROJ_ROWS)
        heads = jnp.concatenate([cat[pair, r, :] for pair in range(C_PAIRS)], axis=1)
        y = jnp.dot(heads, wo_ref[...], preferred_element_type=f32)
        xo_ref[r, :] = x_ref[r, :] + _mod(mod_ref, GATE1) * y
        return carry

    lax.fori_loop(0, n // PROJ_ROWS, output, 0)


def _slot_cols(w, heads, width, lo, hi, lane0):
    k = w.shape[0]
    w3 = w.reshape(k, heads, width)[:, :, lo:hi]
    out = jnp.zeros((k, heads, SLOT), w.dtype).at[:, :, lane0:lane0 + (hi - lo)].set(w3)
    return out.reshape(k, heads * SLOT)


def _l1_mixer(xc, xl, g_norm, mods, p, cache_ckv, cache_kpe):
    w_in = jnp.zeros((D, ODD_IN_PAD), f32).at[:, :C_Q_LORA + C_KV_LORA].set(
        p['w_in'][:, :C_Q_LORA + C_KV_LORA]).at[
        :, C_Q_LORA + C_KV_LORA + C_NOPE:C_Q_LORA + C_KV_LORA + C_QK].set(p['w_in'][:, C_Q_LORA + C_KV_LORA:])
    wuq = _slot_cols(p['w_uq'], C_HEADS, C_QK, 0, C_QK, 0).astype(bf16)
    wuk = _slot_cols(p['w_ukv'], C_HEADS, C_NOPE + C_V, 0, C_NOPE, 0).astype(bf16)
    wuv_t = p['w_ukv'].reshape(C_KV_LORA, C_HEADS, C_NOPE + C_V)[:, :, C_NOPE:].reshape(
        C_KV_LORA, C_HEADS * C_V).T.astype(bf16)
    gq = jnp.zeros((1, SLOT), f32).at[0, :C_QK].set(p['g_q'] * C_SCALE)
    gk = jnp.zeros((1, SLOT), f32).at[0, :C_QK].set(p['g_k'])
    consts = (g_norm.reshape(1, D), w_in.astype(bf16), p['g_qa'].reshape(1, C_Q_LORA), wuq, gq,
              p['g_kva'].reshape(1, C_KV_LORA), wuk, wuv_t, gk, p['w_o'].astype(bf16))
    c_specs = [_const_spec(a.shape) for a in consts]
    row = pl.BlockSpec((BLOCK_ROWS, D), lambda b: (b, 0))
    n_ctx_blocks = T_CTX // BLOCK_ROWS

    def scratch(n_keys):
        return [pltpu.VMEM((BLOCK_ROWS, ODD_IN_PAD), f32), pltpu.VMEM((C_PAIRS, BLOCK_ROWS, LANES), bf16),
                pltpu.VMEM((C_HEADS, BLOCK_ROWS, SLOT), bf16), pltpu.VMEM((C_HEADS, n_keys, SLOT), bf16),
                pltpu.VMEM((C_PAIRS, n_keys // L1_ROWS, LANES, L1_ROWS), bf16),
                pltpu.VMEM((L1_ROWS, C_SLOTS), f32)]

    xo_ctx, ckv_new, kpe_t = pl.pallas_call(
        functools.partial(_l1_kernel, latent=False),
        grid=(n_ctx_blocks,),
        in_specs=[row, c_specs[0], pl.BlockSpec((MOD_ROWS, D), lambda b: (0, 0))] + c_specs[1:],
        out_specs=[row, pl.BlockSpec((BLOCK_ROWS, C_KV_LORA), lambda b: (b, 0)),
                   pl.BlockSpec((BLOCK_ROWS // SEQ, C_ROPE, SEQ), lambda b: (b, 0, 0))],
        out_shape=[jax.ShapeDtypeStruct((T_CTX, D), f32), jax.ShapeDtypeStruct((T_CTX, C_KV_LORA), f32),
                   jax.ShapeDtypeStruct((BATCH, C_ROPE, SEQ), f32)],
        scratch_shapes=scratch(BLOCK_ROWS),
        compiler_params=_cp(("parallel",)), name="l1_mixer_ctx",
    )(xc, consts[0], mods, *consts[1:])

    cos, sin = _rope_tables(DEC_SEQ, C_ROPE, SLOT, C_NOPE)
    def pair_swap(a):
        pairs = a.reshape(a.shape[:-1] + (a.shape[-1] // 2, 2))
        return jnp.stack([pairs[..., 1], pairs[..., 0]], axis=-1).reshape(a.shape)

    w_rope = p['w_uq'].reshape(C_Q_LORA, C_HEADS, C_QK)[:, :, C_NOPE:]
    wuq_swapped = jnp.zeros((C_Q_LORA, C_HEADS, SLOT), f32).at[:, :, C_NOPE:C_QK].set(pair_swap(w_rope)).reshape(
        C_Q_LORA, C_SLOTS).astype(bf16)
    rope = (wuq_swapped, gq * cos, pair_swap(gq) * sin, gk * cos, pair_swap(gk) * sin)
    ckpe = jnp.zeros((DEC_BATCH, PAST, SLOT), f32).at[:, :, C_NOPE:C_QK].set(cache_kpe)
    xo_lat = pl.pallas_call(
        functools.partial(_l1_kernel, latent=True),
        grid=(DEC_BATCH,),
        in_specs=[pl.BlockSpec((BLOCK_ROWS, D), lambda b: (b, 0), pipeline_mode=pl.Buffered(1)), c_specs[0],
                  pl.BlockSpec((MOD_ROWS, D), lambda b: (1 + b, 0))] + c_specs[1:] + [_const_spec(a.shape) for a in rope] + [
                  pl.BlockSpec((None, PAST, C_KV_LORA), lambda b: (b, 0, 0)),
                  pl.BlockSpec((None, PAST, SLOT), lambda b: (b, 0, 0))],
        out_specs=row,
        out_shape=jax.ShapeDtypeStruct((T_LAT, D), f32),
        scratch_shapes=scratch(PAST + BLOCK_ROWS) + [pltpu.VMEM((L1_ROWS, C_SLOTS), f32)],
        compiler_params=_cp(("parallel",)), name="l1_mixer_lat",
    )(xl, consts[0], mods, *consts[1:], *rope, cache_ckv, ckpe)
    return xo_ctx, xo_lat, ckv_new, kpe_t.transpose(0, 2, 1)


ROUTER_ROWS = 40
ROUTE_ROWS = 8


def _router_kernel(xc_ref, xl_ref, gn_ref, mod_ref, whi_ref, wlo_ref, br_ref, h_ref, route_ref):
    h = (_rms_rows(_token_rows(xc_ref, xl_ref), gn_ref[...]) * (1.0 + _mod(mod_ref, SCALE2))
         + _mod(mod_ref, SHIFT2))
    _store_token_major(h_ref, h)
    h_hi, h_lo = _split_bf16(h)
    nt = (((1,), (1,)), ((), ()))
    logits = (lax.dot_general(whi_ref[...], h_hi, nt, preferred_element_type=f32)
              + lax.dot_general(whi_ref[...], h_lo, nt, preferred_element_type=f32)
              + lax.dot_general(wlo_ref[...], h_hi, nt, preferred_element_type=f32))
    logits = logits[0:ROUTER_ROWS, :] + br_ref[0:ROUTER_ROWS, :]
    row_i = lax.broadcasted_iota(jnp.int32, logits.shape, 0)
    row = row_i.astype(f32)
    big = 1e6
    is_g = (row_i >= N_EXPERTS) & (row_i < N_EXPERTS + N_GROUPS)
    lg = jnp.where(is_g, logits, -jnp.inf)
    mg = jnp.max(lg, 0, keepdims=True)
    gsel = jnp.min(jnp.where(lg == mg, row, big), 0, keepdims=True) - N_EXPERTS
    pg_sel = 1.0 / jnp.sum(jnp.where(is_g, jnp.exp(lg - mg), 0.0), 0, keepdims=True)
    in_grp = (row_i < N_EXPERTS) & ((row_i >> 3).astype(f32) == gsel)
    le = jnp.where(in_grp, logits, -jnp.inf)
    m1 = jnp.max(le, 0, keepdims=True)
    i1 = jnp.min(jnp.where(le == m1, row, big), 0, keepdims=True)
    le2 = jnp.where(row == i1, -jnp.inf, le)
    m2 = jnp.max(le2, 0, keepdims=True)
    i2 = jnp.min(jnp.where(le2 == m2, row, big), 0, keepdims=True)
    e2 = jnp.exp(m2 - m1)
    w1 = pg_sel / (1.0 + e2)
    w2 = pg_sel * e2 / (1.0 + e2)
    sub = lax.broadcasted_iota(jnp.int32, route_ref.shape, 0)
    route_ref[...] = jnp.where(sub == 0, i1, jnp.where(sub == 1, i2, jnp.where(sub == 2, w1,
                                                                                jnp.where(sub == 3, w2, 0.0))))


def _router(xc, xl, g_norm, mods, p):
    wr = jnp.zeros((LANES, D), f32).at[:N_EXPERTS].set(p['w_re'].T).at[
        N_EXPERTS:N_EXPERTS + N_GROUPS].set(p['w_rg'].T)
    w_hi, w_lo = _split_bf16(wr)
    br = jnp.zeros((LANES, 1), f32).at[:N_EXPERTS, 0].set(p['b_re']).at[
        N_EXPERTS:N_EXPERTS + N_GROUPS, 0].set(p['b_rg'])
    return pl.pallas_call(
        _router_kernel,
        grid=(T // ROW_TILE,),
        in_specs=_token_specs(D) + [
                  _const_spec((1, D)),
                  pl.BlockSpec((MOD_ROWS, D), lambda i: (_sample_of_tile(i, ROW_TILE), 0)),
                  _const_spec((LANES, D)), _const_spec((LANES, D)), _const_spec((LANES, 1))],
        out_specs=[_token_major_spec(ROW_TILE, lambda i: (i, 0)),
                   pl.BlockSpec((ROUTE_ROWS, ROW_TILE), lambda i: (0, i))],
        out_shape=[jax.ShapeDtypeStruct((T * SUBS, LANES), f32), jax.ShapeDtypeStruct((ROUTE_ROWS, T), f32)],
        compiler_params=_cp(("parallel",)), name="router",
    )(xc, xl, g_norm.reshape(1, D), mods, w_hi, w_lo, br)


PLAN_FIRST_TILE, PLAN_TILES = 0, 1


def _plan_kernel(rt_ref, pos_ref, plan_ref, rank):
    n_blk = T // 128
    e_col = lax.broadcasted_iota(jnp.int32, (N_EXPERTS, 128), 0).astype(f32)
    ri = lax.broadcasted_iota(jnp.int32, (128, 128), 0)
    ci = lax.broadcasted_iota(jnp.int32, (128, 128), 1)
    before = jnp.where(ri < ci, 1.0, 0.0).astype(bf16)

    def picks(b):
        cs = slice(b * 128, (b + 1) * 128)
        return rt_ref[0:1, cs] == e_col, rt_ref[1:2, cs] == e_col

    counts = jnp.zeros((N_EXPERTS, 1), f32)
    for b in range(n_blk):
        m0, m1 = picks(b)
        m = jnp.where(m0, 1.0, 0.0) + jnp.where(m1, 1.0, 0.0)
        rank[:, b * 128:(b + 1) * 128] = jnp.dot(m.astype(bf16), before, preferred_element_type=f32) + counts
        counts = counts + jnp.sum(m, axis=1, keepdims=True)

    tiles = jnp.floor((counts + (MOE_TILE - 1.0)) * (1.0 / MOE_TILE))
    er = lax.broadcasted_iota(jnp.int32, (N_EXPERTS, N_EXPERTS), 0)
    ec = lax.broadcasted_iota(jnp.int32, (N_EXPERTS, N_EXPERTS), 1)
    earlier = jnp.where(ec < er, 1.0, 0.0).astype(bf16)
    tile_start = jnp.dot(earlier, jnp.broadcast_to(tiles, (N_EXPERTS, 128)).astype(bf16),
                         preferred_element_type=f32)
    row_start = tile_start * MOE_TILE

    sub = lax.broadcasted_iota(jnp.int32, (8, 128), 0)
    for b in range(n_blk):
        m0, m1 = picks(b)
        base = rank[:, b * 128:(b + 1) * 128] + row_start
        p0 = jnp.sum(jnp.where(m0, base, 0.0), axis=0, keepdims=True)
        p1 = jnp.sum(jnp.where(m1, base, 0.0), axis=0, keepdims=True)
        pos_ref[:, b * 128:(b + 1) * 128] = jnp.where(sub == 0, p0, jnp.where(sub == 1, p1, 0.0)).astype(jnp.int32)

    diag = (lax.broadcasted_iota(jnp.int32, (N_EXPERTS, 128), 0)
            == lax.broadcasted_iota(jnp.int32, (N_EXPERTS, 128), 1))
    first = jnp.sum(jnp.where(diag, tile_start, 0.0), axis=0, keepdims=True)
    count = jnp.sum(jnp.where(diag, tiles, 0.0), axis=0, keepdims=True)
    rows = jnp.where(sub == PLAN_FIRST_TILE, first, jnp.where(sub == PLAN_TILES, count, 0.0))
    plan_ref[...] = rows.astype(jnp.int32)


def _slot_code(t, k):
    return t * SUBS + k * (SUBS // 2)


def _code_offset(code):
    return pl.multiple_of(code & ~(SUBS - 1), SUBS)


def _code_gate_index(code):
    return code >> 2


PAD_CODE = T * SUBS


def _invert_slots(pos_ref, first_ref, count_ref, code_ref):
    def pad_tile(tile, carry):
        for u in range(MOE_TILE):
            code_ref[tile * MOE_TILE + u] = PAD_CODE
        return carry

    def pad_last_tile(e, carry):
        return pad_tile(jnp.maximum(first_ref[e] + count_ref[e] - 1, 0), carry)
    lax.fori_loop(0, N_EXPERTS, pad_last_tile, 0)
    lax.fori_loop(first_ref[N_EXPERTS - 1] + count_ref[N_EXPERTS - 1], MOE_TILES, pad_tile, 0)

    group = 16
    for k in range(2):
        def place(i, carry):
            t0 = i * group
            slots = [pos_ref[k * T + t0 + u] for u in range(group)]
            for u, s in enumerate(slots):
                code_ref[s] = _slot_code(t0 + u, k)
            return carry
        lax.fori_loop(0, T // group, place, 0)


def _route_plan(route_t):
    pos, plan = pl.pallas_call(
        _plan_kernel,
        out_shape=[jax.ShapeDtypeStruct((8, T), jnp.int32), jax.ShapeDtypeStruct((8, LANES), jnp.int32)],
        scratch_shapes=[pltpu.VMEM((N_EXPERTS, T), f32)],
        compiler_params=_cp(None), name="route_plan",
    )(route_t)
    gates = jnp.pad(route_t[2:4].T.reshape(2 * T), (0, 8))
    return plan[PLAN_FIRST_TILE, :N_EXPERTS], plan[PLAN_TILES, :N_EXPERTS], pos[0:2].reshape(2 * T), gates


def _tile_index(i):
    return jnp.minimum(i, MOE_TILES - 1)


TM_ROWS = T * SUBS
SCATTER_GROUP = 8
STAGES = 4
LAST_EXPERT = N_EXPERTS - 1


def _tile_rows(g):
    return pl.ds(pl.multiple_of(g * MOE_TILE, MOE_TILE), MOE_TILE)


def _expert_tile_pairs(first, count, tile_step):
    def pair(pp, carry):
        for parity in range(2):
            g = 2 * pp + parity
            pl.when((g >= first) & (g < first + count))(functools.partial(tile_step, g, parity))
        return carry
    lax.fori_loop(first // 2, (first + count + 1) // 2, pair, 0)


def _gather_tile(code_ref, tile, xs, gbuf):
    base = tile * MOE_TILE
    for r in range(MOE_TILE):
        gbuf[r * SUBS:(r + 1) * SUBS, :] = xs[pl.ds(_code_offset(code_ref[base + r]), SUBS), :]


def _moe_up_kernel(first_ref, count_ref, pos_ref, h_hbm, w1_ref, w3_ref, hh_hbm, code_ref,
                   xs, gbuf_a, gbuf_b, w13, obuf, sem_x, sem_o):
    e = pl.program_id(0)
    first, count = first_ref[e], count_ref[e]
    gbufs = (gbuf_a, gbuf_b)

    def out_copy(slot, g):
        return pltpu.make_async_copy(obuf.at[slot], hh_hbm.at[_tile_rows(g), :], sem_o.at[slot])

    @pl.when(e == 0)
    def _():
        cp = pltpu.make_async_copy(h_hbm, xs.at[pl.ds(0, TM_ROWS), :], sem_x)
        cp.start()
        _invert_slots(pos_ref, first_ref, count_ref, code_ref)
        xs[TM_ROWS:TM_ROWS + SUBS, :] = jnp.zeros((SUBS, LANES), f32)
        cp.wait()
        _gather_tile(code_ref, 0, xs, gbuf_a)

    @pl.when(count > 0)
    def _():
        w13[:, :D_EXPERT] = w1_ref[0].astype(bf16)
        w13[:, D_EXPERT:] = w3_ref[0].astype(bf16)

    def tile_step(g, parity):
        _gather_tile(code_ref, _tile_index(g + 1), xs, gbufs[1 - parity])
        x3 = jnp.swapaxes(gbufs[parity][...].reshape(MOE_TILE, SUBS, LANES), 0, 1)
        x = jnp.concatenate([x3[s] for s in range(SUBS)], axis=1).astype(bf16)
        h13 = jnp.dot(x, w13[...], preferred_element_type=f32)
        hh = (_silu(h13[:, :D_EXPERT]) * h13[:, D_EXPERT:]).astype(bf16)

        slot = g % STAGES

        @pl.when(g >= STAGES)
        def _():
            out_copy(slot, g).wait()
        obuf[slot] = hh
        out_copy(slot, g).start()

    _expert_tile_pairs(first, count, tile_step)

    @pl.when(e == LAST_EXPERT)
    def _():
        n_used = first + count
        for slot in range(STAGES):
            pl.when(n_used > slot)(lambda slot=slot: out_copy(slot, 0).wait())
        obuf[0] = jnp.zeros((MOE_TILE, D_EXPERT), bf16)

        def zero_tile(g, carry):
            cp = out_copy(0, g)
            cp.start()
            cp.wait()
            return carry
        lax.fori_loop(n_used, MOE_TILES, zero_tile, 0)


def _scatter_tile(code_ref, gate_ref, tile, ybuf, acc):
    base = tile * MOE_TILE
    for g0 in range(0, MOE_TILE, SCATTER_GROUP):
        rows = range(g0, g0 + SCATTER_GROUP)
        codes = [code_ref[base + r] for r in rows]
        new = [acc[pl.ds(_code_offset(c), SUBS), :]
               + gate_ref[_code_gate_index(c)] * ybuf[r * SUBS:(r + 1) * SUBS, :]
               for r, c in zip(rows, codes)]
        for c, v in zip(codes, new):
            acc[pl.ds(_code_offset(c), SUBS), :] = v


RES_ROWS = 256


def _residual_out(x_hbm, y_hbm, tok0, sample_of_chunk, mod_ref, acc, rin, rout, sem_r, sem_w):
    n_chunks = x_hbm.shape[0] // RES_ROWS

    def rows(c):
        return pl.ds(pl.multiple_of(c * RES_ROWS, RES_ROWS), RES_ROWS)

    def in_copy(slot, c):
        return pltpu.make_async_copy(x_hbm.at[rows(c), :], rin.at[slot], sem_r.at[slot])

    def out_copy(slot, c):
        return pltpu.make_async_copy(rout.at[slot], y_hbm.at[rows(c), :], sem_w.at[slot])

    for c in range(STAGES - 1):
        in_copy(c, c).start()

    def ring(cc, carry):
        for slot in range(STAGES):
            c = STAGES * cc + slot
            in_copy(slot, c).wait()
            ahead = c + STAGES - 1

            @pl.when(ahead < n_chunks)
            def _():
                in_copy((slot + STAGES - 1) % STAGES, ahead).start()

            @pl.when(c >= STAGES)
            def _():
                out_copy(slot, c).wait()
            delta = _load_token_major(acc, RES_ROWS, tok0 + c * RES_ROWS)
            gate = mod_ref[pl.ds(sample_of_chunk(c) * MOD_ROWS + GATE2, 1), :]
            rout[slot] = rin[slot] + gate * delta
            out_copy(slot, c).start()
        return carry

    lax.fori_loop(0, n_chunks // STAGES, ring, 0)
    for slot in range(STAGES):
        out_copy(slot, 0).wait()


def _moe_down_kernel(first_ref, count_ref, code_ref, gate_ref, hh_hbm, w2_ref, xc_hbm, xl_hbm, mod_ref,
                     yc_hbm, yl_hbm, acc, ybuf_a, ybuf_b, w2b, ibuf, rin, rout, sem_i, sem_r, sem_w):
    e = pl.program_id(0)
    first, count = first_ref[e], count_ref[e]
    n_used = first_ref[LAST_EXPERT] + count_ref[LAST_EXPERT]
    ybufs = (ybuf_a, ybuf_b)

    def in_copy(slot, g):
        return pltpu.make_async_copy(hh_hbm.at[_tile_rows(g), :], ibuf.at[slot], sem_i.at[slot])

    @pl.when(e == 0)
    def _():
        for g in range(STAGES - 1):
            in_copy(g, g).start()

        def zero(c, carry):
            acc[pl.ds(pl.multiple_of(c * 1024, 1024), 1024), :] = jnp.zeros((1024, LANES), f32)
            return carry
        lax.fori_loop(0, TM_ROWS // 1024, zero, 0)
        acc[TM_ROWS:TM_ROWS + SUBS, :] = jnp.zeros((SUBS, LANES), f32)
        ybuf_b[...] = jnp.zeros_like(ybuf_b)

    @pl.when(count > 0)
    def _():
        w2b[...] = w2_ref[0].astype(bf16)

    def tile_step(g, parity):
        slot = g % STAGES
        in_copy(slot, g).wait()
        ahead = g + STAGES - 1

        @pl.when(ahead < n_used)
        def _():
            in_copy(ahead % STAGES, ahead).start()
        _store_token_major(ybufs[parity], jnp.dot(ibuf[slot], w2b[...], preferred_element_type=f32))
        _scatter_tile(code_ref, gate_ref, jnp.maximum(g - 1, 0), ybufs[1 - parity], acc)

    _expert_tile_pairs(first, count, tile_step)

    @pl.when(e == LAST_EXPERT)
    def _():
        for parity in range(2):
            pl.when((n_used > 0) & ((n_used - 1) % 2 == parity))(
                functools.partial(_scatter_tile, code_ref, gate_ref, n_used - 1, ybufs[parity], acc))
        _residual_out(xc_hbm, yc_hbm, 0, lambda c: 0, mod_ref, acc, rin, rout, sem_r, sem_w)
        _residual_out(xl_hbm, yl_hbm, T_CTX, lambda c: 1 + c // (DEC_SEQ // RES_ROWS),
                      mod_ref, acc, rin, rout, sem_r, sem_w)


def _moe(h_tm, route_t, p, xc, xl, mods):
    first_tile, n_tiles, slots, gates = _route_plan(route_t)
    tile_rows = pltpu.VMEM((MOE_TILE * SUBS, LANES), f32)
    staging = pltpu.VMEM((STAGES, MOE_TILE, D_EXPERT), bf16)
    res_rows = pltpu.VMEM((STAGES, RES_ROWS, D), f32)
    hbm = pl.BlockSpec(memory_space=pl.ANY)
    hh, codes = pl.pallas_call(
        _moe_up_kernel,
        grid_spec=pltpu.PrefetchScalarGridSpec(
            num_scalar_prefetch=3, grid=(N_EXPERTS,),
            in_specs=[hbm,
                      pl.BlockSpec((1, D, D_EXPERT), lambda e, f, n, s: (e, 0, 0)),
                      pl.BlockSpec((1, D, D_EXPERT), lambda e, f, n, s: (e, 0, 0))],
            out_specs=[hbm, pl.BlockSpec(memory_space=pltpu.SMEM)],
            scratch_shapes=[pltpu.VMEM((TM_ROWS + SUBS, LANES), f32), tile_rows, tile_rows,
                            pltpu.VMEM((D, 2 * D_EXPERT), bf16), staging,
                            pltpu.SemaphoreType.DMA(()), pltpu.SemaphoreType.DMA((STAGES,))]),
        out_shape=[jax.ShapeDtypeStruct((MOE_ROWS, D_EXPERT), bf16),
                   jax.ShapeDtypeStruct((MOE_ROWS,), jnp.int32)],
        compiler_params=_cp(("arbitrary",)), name="moe_up",
    )(first_tile, n_tiles, slots, h_tm, p['w1'], p['w3'])
    return pl.pallas_call(
        _moe_down_kernel,
        grid_spec=pltpu.PrefetchScalarGridSpec(
            num_scalar_prefetch=4, grid=(N_EXPERTS,),
            in_specs=[hbm, pl.BlockSpec((1, D_EXPERT, D), lambda e, f, n, c, g: (e, 0, 0)), hbm, hbm,
                      pl.BlockSpec((N_SAMPLES * MOD_ROWS, D), lambda e, f, n, c, g: (0, 0),
                                   pipeline_mode=pl.Buffered(1))],
            out_specs=[hbm, hbm],
            scratch_shapes=[pltpu.VMEM((TM_ROWS + SUBS, LANES), f32), tile_rows, tile_rows,
                            pltpu.VMEM((D_EXPERT, D), bf16), staging, res_rows, res_rows,
                            pltpu.SemaphoreType.DMA((STAGES,)), pltpu.SemaphoreType.DMA((STAGES,)),
                            pltpu.SemaphoreType.DMA((STAGES,))]),
        out_shape=[jax.ShapeDtypeStruct((T_CTX, D), f32), jax.ShapeDtypeStruct((T_LAT, D), f32)],
        compiler_params=_cp(("arbitrary",)), name="moe_down",
    )(first_tile, n_tiles, codes, gates, hh, p['w2'], xc, xl, mods)


def kernel(x_prompt, x_sample, cache_l0_k, cache_l0_v, cache_l1_ckv, cache_l1_kpe, c, c_ctx, l0_g_norm1, l0_g_norm2, l0_w_ada, l0_b_ada, l0_w_in, l0_g_vnorm, l0_w_s, l0_b_s, l0_g_q, l0_g_k, l0_sink, l0_w_o, l0_w_rg, l0_b_rg, l0_w_re, l0_b_re, l0_w1, l0_w3, l0_w2, l1_g_norm1, l1_g_norm2, l1_w_ada, l1_b_ada, l1_w_in, l1_g_qa, l1_w_uq, l1_g_kva, l1_w_ukv, l1_g_q, l1_g_k, l1_w_o, l1_w_rg, l1_b_rg, l1_w_re, l1_b_re, l1_w1, l1_w3, l1_w2):
    p0 = dict(w_in=l0_w_in, g_vnorm=l0_g_vnorm, w_s=l0_w_s, b_s=l0_b_s, g_q=l0_g_q, g_k=l0_g_k, sink=l0_sink,
              w_o=l0_w_o, w_rg=l0_w_rg, b_rg=l0_b_rg, w_re=l0_w_re, b_re=l0_b_re, w1=l0_w1, w3=l0_w3, w2=l0_w2)
    p1 = dict(w_in=l1_w_in, g_qa=l1_g_qa, w_uq=l1_w_uq, g_kva=l1_g_kva, w_ukv=l1_w_ukv, g_q=l1_g_q, g_k=l1_g_k,
              w_o=l1_w_o, w_rg=l1_w_rg, b_rg=l1_b_rg, w_re=l1_w_re, b_re=l1_b_re, w1=l1_w1, w3=l1_w3, w2=l1_w2)

    cond8 = jnp.zeros((MOD_ROWS, D), f32).at[0].set(c_ctx).at[1:1 + DEC_BATCH].set(c)
    mods0 = _mod_rows(_adaln(cond8, l0_w_ada, l0_b_ada))
    mods1 = _mod_rows(_adaln(cond8, l1_w_ada, l1_b_ada))

    xc0 = x_prompt.reshape(T_CTX, D)
    xl0 = x_sample.reshape(T_LAT, D)

    xc0m, xl0m, k_new, v_new = _l0_mixer(xc0, xl0, l0_g_norm1, mods0, p0, cache_l0_k, cache_l0_v)
    h0, route0 = _router(xc0m, xl0m, l0_g_norm2, mods0, p0)
    xc1, xl1 = _moe(h0, route0, p0, xc0m, xl0m, mods0)

    xc1m, xl1m, ckv_new, kpe_new = _l1_mixer(xc1, xl1, l1_g_norm1, mods1, p1, cache_l1_ckv, cache_l1_kpe)
    h1, route1 = _router(xc1m, xl1m, l1_g_norm2, mods1, p1)
    y_prompt, y_sample = _moe(h1, route1, p1, xc1m, xl1m, mods1)
    return (y_prompt.reshape(BATCH, SEQ, D), y_sample.reshape(DEC_BATCH, DEC_SEQ, D), k_new, v_new,
            ckv_new.reshape(BATCH, SEQ, C_KV_LORA), kpe_new.reshape(BATCH, SEQ, C_ROPE))
```

```python
import functools

import jax
import jax.numpy as jnp
import numpy as np
from jax import lax
from jax.experimental import pallas as pl
from jax.experimental.pallas import tpu as pltpu

f32 = jnp.float32
bf16 = jnp.bfloat16

D = 1024
BATCH, SEQ = 32, 256
DEC_BATCH, DEC_SEQ = 2, 1024
PAST = 512
T_CTX = BATCH * SEQ
T_LAT = DEC_BATCH * DEC_SEQ
T = T_CTX + T_LAT
GRID_W = 64
CHUNK = 128
WINDOW = 128
ROPE_THETA = 10000.0
EPS = 1e-6
NEG_INF = -1e30
LANES = 128
SUBS = D // LANES

A_WIDTH = 512
A_GROUPS = 4
B_HEADS, B_KV, B_GROUP, B_HD = 8, 2, 4, 64
LOG2E = 1.4426950408889634
B_SCALE = B_HD ** -0.5 * LOG2E

C_HEADS, C_Q_LORA, C_KV_LORA, C_NOPE, C_ROPE, C_V = 16, 384, 256, 64, 32, 64
C_QK = C_NOPE + C_ROPE
C_SCALE = C_QK ** -0.5 * LOG2E
ODD_IN_PAD = 768
SLOT = 128

N_GROUPS, N_EXPERTS, D_EXPERT = 4, 32, 256

N_SAMPLES = 1 + DEC_BATCH
MOD_ROWS = 8
SHIFT1, SCALE1, GATE1, SHIFT2, SCALE2, GATE2 = range(6)

ROW_TILE = 1024
BLOCK_ROWS = 1024
PROJ_ROWS = 512
ADALN_COLS = 1536
MOE_TILE = 256
MOE_ROWS = 2 * T + N_EXPERTS * MOE_TILE
MOE_TILES = MOE_ROWS // MOE_TILE
VMEM_CAP = 56 * 1024 * 1024


def _cp(sem, vmem=VMEM_CAP):
    return pltpu.CompilerParams(dimension_semantics=sem, vmem_limit_bytes=vmem)


def _const_spec(shape):
    nd = len(shape)
    return pl.BlockSpec(shape, lambda *_: (0,) * nd, pipeline_mode=pl.Buffered(1))


def _sample_of_tile(i, tile):
    n_ctx = T_CTX // tile
    per_lat = DEC_SEQ // tile
    return jnp.where(i < n_ctx, 0, 1 + (i - n_ctx) // per_lat)


def _mod(mod_ref, row):
    return mod_ref[row:row + 1, :]


def _silu(x):
    return x * jax.nn.sigmoid(x)


def _rms_rows(x, g):
    return x * lax.rsqrt(jnp.mean(x * x, -1, keepdims=True) + EPS) * g


def _swap_pairs(x):
    lane = lax.broadcasted_iota(jnp.int32, x.shape, x.ndim - 1)
    nxt = pltpu.roll(x, x.shape[-1] - 1, x.ndim - 1)
    prv = pltpu.roll(x, 1, x.ndim - 1)
    return jnp.where((lane & 1) == 0, nxt, prv)


def _split_bf16(x):
    hi = x.astype(bf16)
    return hi, (x - hi.astype(f32)).astype(bf16)


def _adaln_kernel(c_ref, w_ref, b_ref, o_ref):
    s_hi, s_lo = _split_bf16(_silu(c_ref[...]))
    w_hi, w_lo = _split_bf16(w_ref[...])
    o_ref[...] = (jnp.dot(s_hi, w_hi, preferred_element_type=f32) + jnp.dot(s_lo, w_hi, preferred_element_type=f32)
                  + jnp.dot(s_hi, w_lo, preferred_element_type=f32) + b_ref[...])


def _adaln(cond8, w, b):
    n = w.shape[1]
    return pl.pallas_call(
        _adaln_kernel,
        grid=(n // ADALN_COLS,),
        in_specs=[_const_spec((MOD_ROWS, D)), pl.BlockSpec((D, ADALN_COLS), lambda j: (0, j)),
                  pl.BlockSpec((1, ADALN_COLS), lambda j: (0, j))],
        out_specs=pl.BlockSpec((MOD_ROWS, ADALN_COLS), lambda j: (0, j)),
        out_shape=jax.ShapeDtypeStruct((MOD_ROWS, n), f32),
        compiler_params=_cp(("arbitrary",)),
        name="adaln",
    )(cond8, w, b.reshape(1, n))


def _mod_rows(m8):
    m = m8[:N_SAMPLES].reshape(N_SAMPLES, 6, D)
    return jnp.pad(m, ((0, 0), (0, MOD_ROWS - 6), (0, 0))).reshape(N_SAMPLES * MOD_ROWS, D)


N_CTX_TILES = T_CTX // ROW_TILE


def _token_specs(width):
    return [pl.BlockSpec((ROW_TILE, width), lambda i: (jnp.minimum(i, N_CTX_TILES - 1), 0)),
            pl.BlockSpec((ROW_TILE, width), lambda i: (jnp.maximum(i - N_CTX_TILES, 0), 0))]


def _store_token_major(ref, x, row0=0):
    n = x.shape[0]
    for s in range(SUBS):
        ref[pl.ds(row0 * SUBS + s, n, stride=SUBS), :] = x[:, s * LANES:(s + 1) * LANES]


def _load_token_major(ref, n, row0=0):
    return jnp.concatenate([ref[pl.ds(row0 * SUBS + s, n, stride=SUBS), :] for s in range(SUBS)], axis=1)


def _token_major_spec(rows, index_map):
    return pl.BlockSpec((rows * SUBS, LANES), index_map)


def _rope_tables(n, rot_dim, lanes, lane0, copies=1):
    rows_count = n // GRID_W
    rows = np.repeat(np.arange(rows_count), GRID_W).astype(np.float64)
    cols = np.tile(np.arange(GRID_W), rows_count).astype(np.float64)
    d_axis = rot_dim // 2
    inv = ROPE_THETA ** (-np.arange(0, d_axis, 2, dtype=np.float64) / d_axis)
    ang = np.concatenate([rows[:, None] * inv, cols[:, None] * inv], -1)
    c = np.ones((n, lanes), np.float32)
    s = np.zeros((n, lanes), np.float32)
    for j in range(copies):
        lo = lane0 + j * rot_dim
        c[:, lo:lo + rot_dim] = np.repeat(np.cos(ang), 2, axis=1)
        s[:, lo:lo + rot_dim] = np.repeat(np.sin(ang), 2, axis=1) * np.tile(np.array([-1.0, 1.0]), rot_dim // 2)
    return jnp.asarray(c), jnp.asarray(s)


L0_Q0 = 2 * A_WIDTH
L0_K0 = L0_Q0 + B_GROUP * LANES
L0_V0 = L0_K0 + B_KV * B_HD
L0_IN = L0_V0 + B_KV * B_HD


def _l0_kernel(*refs, latent):
    if latent:
        (sink_ref, x_ref, gn_ref, mod_ref, win_ref, gvn_ref, ws_ref, bsb_ref, gq_ref, gk_ref, wo_ref,
         cos_ref, sin_ref, kc_ref, vc_ref, xo_ref, zs, cat, qs, ks, vt, kcb, vct) = refs
        key_off = WINDOW
    else:
        (sink_ref, x_ref, gn_ref, mod_ref, win_ref, gvn_ref, ws_ref, bsb_ref, gq_ref, gk_ref, wo_ref,
         xo_ref, ko_ref, vo_ref, zs, cat, qs, ks, vt, kf, vf) = refs
        key_off = 0
    n = BLOCK_ROWS
    n_chunks = n // CHUNK
    low = lax.broadcasted_iota(jnp.int32, (CHUNK, LANES), 1) < B_HD

    if latent:
        zpad = jnp.zeros((WINDOW, LANES), bf16)
        for c0 in (0, 1 + n_chunks):
            ks[c0 * CHUNK:(c0 + 1) * CHUNK, :] = zpad
            vt[c0] = zpad
        kcb[...] = kc_ref[...].astype(bf16)
        for i in range(PAST // CHUNK):
            vct[i] = vc_ref[i * CHUNK:(i + 1) * CHUNK, :].T.astype(bf16)

    def project(c, carry):
        r = pl.ds(pl.multiple_of(c * PROJ_ROWS, PROJ_ROWS), PROJ_ROWS)
        h = _rms_rows(x_ref[r, :], gn_ref[...]) * (1.0 + _mod(mod_ref, SCALE1)) + _mod(mod_ref, SHIFT1)
        zs[r, :] = jnp.dot(h.astype(bf16), win_ref[...], preferred_element_type=f32)
        return carry

    lax.fori_loop(0, n // PROJ_ROWS, project, 0)

    def prepare(c, carry):
        r = pl.ds(pl.multiple_of(c * CHUNK, CHUNK), CHUNK)
        u = jax.nn.gelu(zs[r, 0:A_WIDTH])
        v = jax.nn.gelu(zs[r, A_WIDTH:2 * A_WIDTH])
        mu = jnp.mean(v, -1, keepdims=True)
        var = jnp.mean(jnp.square(v - mu), -1, keepdims=True)
        vn = ((v - mu) * lax.rsqrt(var + EPS) * gvn_ref[...]).astype(bf16)
        for g in range(A_GROUPS):
            cs = slice(g * CHUNK, (g + 1) * CHUNK)
            mixed = jnp.dot(ws_ref[g], vn[:, cs], preferred_element_type=f32) + bsb_ref[g]
            cat[r, cs] = (u[:, cs] * mixed).astype(bf16)
        if latent:
            cs_, sn_ = cos_ref[r, :], sin_ref[r, :]
        def half_norm(v, gain):
            sq = v * v
            s0 = jnp.sum(jnp.where(low, sq, 0.0), -1, keepdims=True)
            s1 = jnp.sum(jnp.where(low, 0.0, sq), -1, keepdims=True)
            return v * lax.rsqrt(jnp.where(low, s0, s1) * (1.0 / B_HD) + EPS) * gain

        for j in range(B_GROUP):
            js = slice(j * LANES, (j + 1) * LANES)
            qj = half_norm(zs[r, L0_Q0 + j * LANES:L0_Q0 + (j + 1) * LANES], gq_ref[...])
            if latent:
                qj = qj * cs_ + _swap_pairs(qj) * sn_
            qs[r, js] = qj.astype(bf16)
        k = half_norm(zs[r, L0_K0:L0_K0 + LANES], gk_ref[...])
        vv = zs[r, L0_V0:L0_V0 + LANES]
        if latent:
            k = k * cs_ + _swap_pairs(k) * sn_
        else:
            kf[r, :] = k
            vf[r, :] = vv
        kr = pl.ds(pl.multiple_of(c * CHUNK + key_off, CHUNK), CHUNK)
        ks[kr, :] = k.astype(bf16)
        vt[c + key_off // CHUNK] = vv.T.astype(bf16)
        return carry

    lax.fori_loop(0, n_chunks // 2, lambda i, carry: prepare(2 * i + 1, prepare(2 * i, carry)), 0)

    def attend(jobs):
        scored = []
        for r, rows, key_sets in jobs:
            low_q = lax.broadcasted_iota(jnp.int32, (rows, LANES), 1) < B_HD
            slots = [qs[r, j * LANES:(j + 1) * LANES] for j in range(B_GROUP)]
            zero = jnp.zeros((rows, LANES), bf16)
            q = jnp.concatenate([jnp.where(low_q, qj, zero) for qj in slots]
                                + [jnp.where(low_q, zero, qj) for qj in slots], axis=0)
            sk = jnp.concatenate([jnp.full((1, rows), sink_ref[h] * LOG2E, f32) for h in range(B_HEADS)], axis=1)
            scores = []
            m = sk
            for k, _, keep in key_sets:
                s = lax.dot_general(k, q, (((1,), (1,)), ((), ())), preferred_element_type=f32)
                if keep is not None:
                    s = jnp.where(keep, s, NEG_INF)
                scores.append(s)
                m = jnp.maximum(m, jnp.max(s, 0, keepdims=True))
            scored.append((sk, scores, m))
        outs = []
        for (r, rows, key_sets), (sk, scores, m) in zip(jobs, scored):
            den = jnp.exp2(sk - m)
            ot = None
            for s, (_, vts, _) in zip(scores, key_sets):
                e = jnp.exp2(s - m)
                den = den + jnp.sum(e, 0, keepdims=True)
                eb = e.astype(bf16)
                for i, v_t in enumerate(vts):
                    pv = jnp.dot(v_t, eb[i * CHUNK:(i + 1) * CHUNK, :], preferred_element_type=f32)
                    ot = pv if ot is None else ot + pv
            outs.append(ot * (1.0 / den))
        for (r, rows, _), ot in zip(jobs, outs):
            for pair in range(B_HEADS // 2):
                f0 = (2 * pair // B_GROUP) * B_HD
                pair_t = jnp.concatenate([ot[f0:f0 + B_HD, 2 * pair * rows:(2 * pair + 1) * rows],
                                          ot[f0:f0 + B_HD, (2 * pair + 1) * rows:(2 * pair + 2) * rows]], axis=0)
                cat[r, A_WIDTH + pair * LANES:A_WIDTH + (pair + 1) * LANES] = pair_t.T.astype(bf16)

    jobs_per_step = 2 if latent else 4
    if latent:
        span = CHUNK + 2 * WINDOW

        def block_job(c):
            start = pl.multiple_of(c * CHUNK, CHUNK)
            kr = pl.ds(start, span)
            kj = lax.broadcasted_iota(jnp.int32, (span, B_HEADS * CHUNK), 0)
            qi = lax.broadcasted_iota(jnp.int32, (span, B_HEADS * CHUNK), 1) & (CHUNK - 1)
            kpos = start - WINDOW + kj
            keep = (jnp.abs(kj - WINDOW - qi) <= WINDOW) & (kpos >= 0) & (kpos < n)
            return (pl.ds(start, CHUNK), CHUNK,
                    [(ks[kr, :], [vt[c + i] for i in range(span // CHUNK)], keep),
                     (kcb[...], [vct[i] for i in range(PAST // CHUNK)], None)])

        def attend_blocks(i, carry):
            attend([block_job(i * jobs_per_step + j) for j in range(jobs_per_step)])
            return carry

        lax.fori_loop(0, n_chunks // jobs_per_step, attend_blocks, 0)
    else:
        def seq_job(sq):
            r = pl.ds(pl.multiple_of(sq * SEQ, SEQ), SEQ)
            return r, SEQ, [(ks[r, :], [vt[sq * (SEQ // CHUNK) + i] for i in range(SEQ // CHUNK)], None)]

        def attend_seqs(i, carry):
            seqs = [i * jobs_per_step + j for j in range(jobs_per_step)]
            attend([seq_job(sq) for sq in seqs])
            for sq in seqs:
                r = pl.ds(pl.multiple_of(sq * SEQ, SEQ), SEQ)
                ko_ref[sq] = kf[r, :].T
                vo_ref[sq] = vf[r, :].T
            return carry

        lax.fori_loop(0, n // SEQ // jobs_per_step, attend_seqs, 0)

    def output(c, carry):
        r = pl.ds(pl.multiple_of(c * PROJ_ROWS, PROJ_ROWS), PROJ_ROWS)
        y = jnp.dot(cat[r, :], wo_ref[...], preferred_element_type=f32)
        xo_ref[r, :] = x_ref[r, :] + _mod(mod_ref, GATE1) * y
        return carry

    lax.fori_loop(0, n // PROJ_ROWS, output, 0)


def _l0_mixer(xc, xl, g_norm, mods, p, cache_k, cache_v):
    w = p['w_in']
    q_slots = w[:, L0_Q0:L0_Q0 + B_HEADS * B_HD].reshape(D, B_KV, B_GROUP, B_HD).transpose(0, 2, 1, 3).reshape(
        D, B_GROUP * LANES)
    win = jnp.concatenate([w[:, :L0_Q0], q_slots, w[:, L0_Q0 + B_HEADS * B_HD:]], axis=1).astype(bf16)
    gq = jnp.tile(p['g_q'], B_KV).reshape(1, LANES) * B_SCALE
    gk = jnp.tile(p['g_k'], B_KV).reshape(1, LANES)
    gvn = p['g_vnorm'].reshape(1, A_WIDTH)
    ws = p['w_s'].astype(bf16)
    bsb = jnp.broadcast_to(p['b_s'][:, :, None], (A_GROUPS, CHUNK, CHUNK))
    wo = p['w_o'].astype(bf16)
    weights = (g_norm.reshape(1, D),)
    consts = (win, gvn, ws, bsb, gq, gk, wo)
    c_specs = [_const_spec(a.shape) for a in consts]
    smem = pl.BlockSpec(memory_space=pltpu.SMEM)
    row = pl.BlockSpec((BLOCK_ROWS, D), lambda b: (b, 0))
    kv = pl.BlockSpec((BLOCK_ROWS // SEQ, LANES, SEQ), lambda b: (b, 0, 0))

    def scratch(pad):
        return [pltpu.VMEM((BLOCK_ROWS, L0_IN), f32), pltpu.VMEM((BLOCK_ROWS, D), bf16),
                pltpu.VMEM((BLOCK_ROWS, B_GROUP * LANES), bf16), pltpu.VMEM((BLOCK_ROWS + pad, LANES), bf16),
                pltpu.VMEM(((BLOCK_ROWS + pad) // CHUNK, LANES, CHUNK), bf16)]

    kv_shape = jax.ShapeDtypeStruct((BATCH, LANES, SEQ), f32)
    state = pltpu.VMEM((BLOCK_ROWS, LANES), f32)
    xo_ctx, k_t, v_t = pl.pallas_call(
        functools.partial(_l0_kernel, latent=False),
        grid=(T_CTX // BLOCK_ROWS,),
        in_specs=[smem, row, _const_spec((1, D)), pl.BlockSpec((MOD_ROWS, D), lambda b: (0, 0))] + c_specs,
        out_specs=[row, kv, kv],
        out_shape=[jax.ShapeDtypeStruct((T_CTX, D), f32), kv_shape, kv_shape],
        scratch_shapes=scratch(0) + [state, state],
        compiler_params=_cp(("parallel",)), name="l0_mixer_ctx",
    )(p['sink'], xc, *weights, mods, *consts)
    k_new = k_t.reshape(BATCH, B_KV, B_HD, SEQ).transpose(0, 3, 1, 2)
    v_new = v_t.reshape(BATCH, B_KV, B_HD, SEQ).transpose(0, 3, 1, 2)

    cos, sin = _rope_tables(DEC_SEQ, B_HD, LANES, 0, copies=LANES // B_HD)
    cache = pl.BlockSpec((None, PAST, LANES), lambda b: (b, 0, 0))
    past = [pltpu.VMEM((PAST, LANES), bf16), pltpu.VMEM((PAST // CHUNK, LANES, CHUNK), bf16)]
    xo_lat = pl.pallas_call(
        functools.partial(_l0_kernel, latent=True),
        grid=(DEC_BATCH,),
        in_specs=[smem, row, _const_spec((1, D)), pl.BlockSpec((MOD_ROWS, D), lambda b: (1 + b, 0))] + c_specs + [
                  _const_spec(cos.shape), _const_spec(sin.shape), cache, cache],
        out_specs=row,
        out_shape=jax.ShapeDtypeStruct((T_LAT, D), f32),
        scratch_shapes=scratch(2 * WINDOW) + past,
        compiler_params=_cp(("parallel",)), name="l0_mixer_lat",
    )(p['sink'], xl, *weights, mods, *consts, cos, sin,
      cache_k.reshape(DEC_BATCH, PAST, LANES), cache_v.reshape(DEC_BATCH, PAST, LANES))
    return xo_ctx, xo_lat, k_new, v_new


C_SLOTS = C_HEADS * SLOT
C_PAIRS = C_HEADS // 2
L1_ROWS = 256


def _l1_kernel(*refs, latent):
    if latent:
        (x_ref, gn_ref, mod_ref, win_ref, gqa_ref, wuq_ref, gq_ref, gkva_ref, wuk_ref, wuvt_ref, gk_ref,
         wo_ref, wuqs_ref, qcos_ref, qsin_ref, kcos_ref, ksin_ref, cckv_ref, ckpe_ref, xo_ref,
         zs, cat, qs, ks, vt, wide, wide2) = refs
        n_ctx = PAST
    else:
        (x_ref, gn_ref, mod_ref, win_ref, gqa_ref, wuq_ref, gq_ref, gkva_ref, wuk_ref, wuvt_ref, gk_ref,
         wo_ref, xo_ref, ckvo_ref, kpeo_ref, zs, cat, qs, ks, vt, wide) = refs
        n_ctx = 0
    n = BLOCK_ROWS
    nt_dims = (((1,), (1,)), ((), ()))

    def inv_rms(v):
        return lax.rsqrt(jnp.sum(v * v, -1, keepdims=True) * (1.0 / C_QK) + EPS)

    def expand_keys(ckv_n, kslot, kb, rope_rows):
        cb = ckv_n.astype(bf16)
        key_rows = pl.ds(pl.multiple_of(kb * L1_ROWS, L1_ROWS), L1_ROWS)
        wide[...] = jnp.dot(cb, wuk_ref[...], preferred_element_type=f32)
        if rope_rows is not None:
            kcos = kcos_ref[rope_rows, :]
            turned = _swap_pairs(kslot) * ksin_ref[rope_rows, :]
        for h in range(C_HEADS):
            kh = wide[:, h * SLOT:(h + 1) * SLOT] + kslot
            if rope_rows is not None:
                kh = inv_rms(kh) * (kh * kcos + turned)
            else:
                kh = kh * inv_rms(kh) * gk_ref[...]
            ks[h, key_rows, :] = kh.astype(bf16)
        v_t = lax.dot_general(wuvt_ref[...], cb, nt_dims, preferred_element_type=f32).astype(bf16)
        for pair in range(C_PAIRS):
            vt[pair, kb] = v_t[pair * LANES:(pair + 1) * LANES, :]

    if latent:
        def past_keys(c, carry):
            r = pl.ds(pl.multiple_of(c * L1_ROWS, L1_ROWS), L1_ROWS)
            expand_keys(cckv_ref[r, :], ckpe_ref[r, :], c, None)
            return carry

        lax.fori_loop(0, PAST // L1_ROWS, past_keys, 0)

    def project(c, carry):
        r = pl.ds(pl.multiple_of(c * PROJ_ROWS, PROJ_ROWS), PROJ_ROWS)
        h = _rms_rows(x_ref[r, :], gn_ref[...]) * (1.0 + _mod(mod_ref, SCALE1)) + _mod(mod_ref, SHIFT1)
        zs[r, :] = jnp.dot(h.astype(bf16), win_ref[...], preferred_element_type=f32)
        return carry

    lax.fori_loop(0, n // PROJ_ROWS, project, 0)

    def prepare(c, carry):
        r = pl.ds(pl.multiple_of(c * L1_ROWS, L1_ROWS), L1_ROWS)
        qa = _rms_rows(zs[r, 0:C_Q_LORA], gqa_ref[...]).astype(bf16)
        wide[...] = jnp.dot(qa, wuq_ref[...], preferred_element_type=f32)
        if latent:
            wide2[...] = jnp.dot(qa, wuqs_ref[...], preferred_element_type=f32)
            qcos, qsin = qcos_ref[r, :], qsin_ref[r, :]
        for h in range(C_HEADS):
            hs = slice(h * SLOT, (h + 1) * SLOT)
            qh = wide[:, hs]
            if latent:
                qh = inv_rms(qh) * (qh * qcos + wide2[:, hs] * qsin)
            else:
                qh = qh * inv_rms(qh) * gq_ref[...]
            qs[h, r, :] = qh.astype(bf16)
        ckv_n = _rms_rows(zs[r, C_Q_LORA:C_Q_LORA + C_KV_LORA], gkva_ref[...])
        kslot = zs[r, C_Q_LORA + C_KV_LORA:ODD_IN_PAD]
        if not latent:
            ckvo_ref[r, :] = ckv_n
            kpeo_ref[c] = kslot.T[C_NOPE:C_QK, :]
        expand_keys(ckv_n, kslot, c + n_ctx // L1_ROWS, r if latent else None)
        return carry

    lax.fori_loop(0, n // L1_ROWS, prepare, 0)

    low = lax.broadcasted_iota(jnp.int32, (2 * C_V, L1_ROWS), 0) < C_V
    n_kblocks = (n_ctx + n) // L1_ROWS
    pairs_per_step = 4 if latent else 8

    blocks_per_step = 1 if latent else 2

    def attend(cc, carry):
        blocks = [cc * blocks_per_step + j for j in range(blocks_per_step)]
        rows = [pl.ds(pl.multiple_of(c * L1_ROWS, L1_ROWS), L1_ROWS) for c in blocks]

        def values_t(pair, c, eb):
            if not latent:
                return jnp.dot(vt[pair, c], eb, preferred_element_type=f32)
            o_t = None
            for b in range(n_kblocks):
                pv = jnp.dot(vt[pair, b], eb[b * L1_ROWS:(b + 1) * L1_ROWS, :], preferred_element_type=f32)
                o_t = pv if o_t is None else o_t + pv
            return o_t

        def pairs_step(i, carry2):
            pairs = [i * pairs_per_step + j for j in range(pairs_per_step)]
            units = [(c, r, 2 * p + hh) for c, r in zip(blocks, rows) for p in pairs for hh in range(2)]
            scores = [lax.dot_general(ks[h] if latent else ks[h, r, :], qs[h, r, :], nt_dims,
                                      preferred_element_type=f32) for _, r, h in units]
            exps = [jnp.exp2(s - jnp.max(s, 0, keepdims=True)) for s in scores]
            dens = [jnp.sum(e, 0, keepdims=True) for e in exps]
            outs = [values_t(h // 2, c, e.astype(bf16)) / den for (c, _, h), e, den in zip(units, exps, dens)]
            for u in range(0, len(units), 2):
                _, r, h = units[u]
                cat[h // 2, r, :] = jnp.where(low, outs[u], outs[u + 1]).T.astype(bf16)
            return carry2

        return lax.fori_loop(0, C_PAIRS // pairs_per_step, pairs_step, carry)

    lax.fori_loop(0, n // L1_ROWS // blocks_per_step, attend, 0)

    def output(c, carry):
        r = pl.ds(pl.multiple_of(c * PROJ_ROWS, PROJ_ROWS), PROJ_ROWS)
        heads = jnp.concatenate([cat[pair, r, :] for pair in range(C_PAIRS)], axis=1)
        y = jnp.dot(heads, wo_ref[...], preferred_element_type=f32)
        xo_ref[r, :] = x_ref[r, :] + _mod(mod_ref, GATE1) * y
        return carry

    lax.fori_loop(0, n // PROJ_ROWS, output, 0)


def _slot_cols(w, heads, width, lo, hi, lane0):
    k = w.shape[0]
    w3 = w.reshape(k, heads, width)[:, :, lo:hi]
    out = jnp.zeros((k, heads, SLOT), w.dtype).at[:, :, lane0:lane0 + (hi - lo)].set(w3)
    return out.reshape(k, heads * SLOT)


def _l1_mixer(xc, xl, g_norm, mods, p, cache_ckv, cache_kpe):
    w_in = jnp.zeros((D, ODD_IN_PAD), f32).at[:, :C_Q_LORA + C_KV_LORA].set(
        p['w_in'][:, :C_Q_LORA + C_KV_LORA]).at[
        :, C_Q_LORA + C_KV_LORA + C_NOPE:C_Q_LORA + C_KV_LORA + C_QK].set(p['w_in'][:, C_Q_LORA + C_KV_LORA:])
    wuq = _slot_cols(p['w_uq'], C_HEADS, C_QK, 0, C_QK, 0).astype(bf16)
    wuk = _slot_cols(p['w_ukv'], C_HEADS, C_NOPE + C_V, 0, C_NOPE, 0).astype(bf16)
    wuv_t = p['w_ukv'].reshape(C_KV_LORA, C_HEADS, C_NOPE + C_V)[:, :, C_NOPE:].reshape(
        C_KV_LORA, C_HEADS * C_V).T.astype(bf16)
    gq = jnp.zeros((1, SLOT), f32).at[0, :C_QK].set(p['g_q'] * C_SCALE)
    gk = jnp.zeros((1, SLOT), f32).at[0, :C_QK].set(p['g_k'])
    consts = (g_norm.reshape(1, D), w_in.astype(bf16), p['g_qa'].reshape(1, C_Q_LORA), wuq, gq,
              p['g_kva'].reshape(1, C_KV_LORA), wuk, wuv_t, gk, p['w_o'].astype(bf16))
    c_specs = [_const_spec(a.shape) for a in consts]
    row = pl.BlockSpec((BLOCK_ROWS, D), lambda b: (b, 0))
    n_ctx_blocks = T_CTX // BLOCK_ROWS

    def scratch(n_keys):
        return [pltpu.VMEM((BLOCK_ROWS, ODD_IN_PAD), f32), pltpu.VMEM((C_PAIRS, BLOCK_ROWS, LANES), bf16),
                pltpu.VMEM((C_HEADS, BLOCK_ROWS, SLOT), bf16), pltpu.VMEM((C_HEADS, n_keys, SLOT), bf16),
                pltpu.VMEM((C_PAIRS, n_keys // L1_ROWS, LANES, L1_ROWS), bf16),
                pltpu.VMEM((L1_ROWS, C_SLOTS), f32)]

    xo_ctx, ckv_new, kpe_t = pl.pallas_call(
        functools.partial(_l1_kernel, latent=False),
        grid=(n_ctx_blocks,),
        in_specs=[row, c_specs[0], pl.BlockSpec((MOD_ROWS, D), lambda b: (0, 0))] + c_specs[1:],
        out_specs=[row, pl.BlockSpec((BLOCK_ROWS, C_KV_LORA), lambda b: (b, 0)),
                   pl.BlockSpec((BLOCK_ROWS // SEQ, C_ROPE, SEQ), lambda b: (b, 0, 0))],
        out_shape=[jax.ShapeDtypeStruct((T_CTX, D), f32), jax.ShapeDtypeStruct((T_CTX, C_KV_LORA), f32),
                   jax.ShapeDtypeStruct((BATCH, C_ROPE, SEQ), f32)],
        scratch_shapes=scratch(BLOCK_ROWS),
        compiler_params=_cp(("parallel",)), name="l1_mixer_ctx",
    )(xc, consts[0], mods, *consts[1:])

    cos, sin = _rope_tables(DEC_SEQ, C_ROPE, SLOT, C_NOPE)
    def pair_swap(a):
        pairs = a.reshape(a.shape[:-1] + (a.shape[-1] // 2, 2))
        return jnp.stack([pairs[..., 1], pairs[..., 0]], axis=-1).reshape(a.shape)

    w_rope = p['w_uq'].reshape(C_Q_LORA, C_HEADS, C_QK)[:, :, C_NOPE:]
    wuq_swapped = jnp.zeros((C_Q_LORA, C_HEADS, SLOT), f32).at[:, :, C_NOPE:C_QK].set(pair_swap(w_rope)).reshape(
        C_Q_LORA, C_SLOTS).astype(bf16)
    rope = (wuq_swapped, gq * cos, pair_swap(gq) * sin, gk * cos, pair_swap(gk) * sin)
    ckpe = jnp.zeros((DEC_BATCH, PAST, SLOT), f32).at[:, :, C_NOPE:C_QK].set(cache_kpe)
    xo_lat = pl.pallas_call(
        functools.partial(_l1_kernel, latent=True),
        grid=(DEC_BATCH,),
        in_specs=[pl.BlockSpec((BLOCK_ROWS, D), lambda b: (b, 0), pipeline_mode=pl.Buffered(1)), c_specs[0],
                  pl.BlockSpec((MOD_ROWS, D), lambda b: (1 + b, 0))] + c_specs[1:] + [_const_spec(a.shape) for a in rope] + [
                  pl.BlockSpec((None, PAST, C_KV_LORA), lambda b: (b, 0, 0)),
                  pl.BlockSpec((None, PAST, SLOT), lambda b: (b, 0, 0))],
        out_specs=row,
        out_shape=jax.ShapeDtypeStruct((T_LAT, D), f32),
        scratch_shapes=scratch(PAST + BLOCK_ROWS) + [pltpu.VMEM((L1_ROWS, C_SLOTS), f32)],
        compiler_params=_cp(("parallel",)), name="l1_mixer_lat",
    )(xl, consts[0], mods, *consts[1:], *rope, cache_ckv, ckpe)
    return xo_ctx, xo_lat, ckv_new, kpe_t.transpose(0, 2, 1)


ROUTER_ROWS = 40
ROUTE_ROWS = 8
ROUTER_CHUNK = 256
W_LO_ROW0 = 64


def _router_kernel(xc_ref, xl_ref, gn_ref, mod_ref, whi_ref, wst_ref, br_ref, h_ref, route_ref, lg_ref):
    nt = (((1,), (1,)), ((), ()))

    def moe_input(x_ref):
        def chunk(c, carry):
            row0 = pl.multiple_of(c * ROUTER_CHUNK, ROUTER_CHUNK)
            h = (_rms_rows(x_ref[pl.ds(row0, ROUTER_CHUNK), :], gn_ref[...]) * (1.0 + _mod(mod_ref, SCALE2))
                 + _mod(mod_ref, SHIFT2))
            _store_token_major(h_ref, h, row0)
            h_hi, h_lo = _split_bf16(h)
            by_hi = lax.dot_general(wst_ref[...], h_hi, nt, preferred_element_type=f32)
            by_lo = lax.dot_general(whi_ref[...], h_lo, nt, preferred_element_type=f32)
            lg_ref[c] = (by_hi[0:ROUTER_ROWS, :] + by_lo[0:ROUTER_ROWS, :]
                         + by_hi[W_LO_ROW0:W_LO_ROW0 + ROUTER_ROWS, :])
            return carry

        lax.fori_loop(0, ROW_TILE // ROUTER_CHUNK, chunk, 0, unroll=True)

    is_ctx = pl.program_id(0) < N_CTX_TILES
    pl.when(is_ctx)(lambda: moe_input(xc_ref))
    pl.when(jnp.logical_not(is_ctx))(lambda: moe_input(xl_ref))
    logits = (jnp.concatenate([lg_ref[c] for c in range(ROW_TILE // ROUTER_CHUNK)], axis=1)
              + br_ref[0:ROUTER_ROWS, :])
    row_i = lax.broadcasted_iota(jnp.int32, logits.shape, 0)
    row = row_i.astype(f32)
    big = 1e6
    is_g = (row_i >= N_EXPERTS) & (row_i < N_EXPERTS + N_GROUPS)
    lg = jnp.where(is_g, logits, -jnp.inf)
    mg = jnp.max(lg, 0, keepdims=True)
    gsel = jnp.min(jnp.where(lg == mg, row, big), 0, keepdims=True) - N_EXPERTS
    pg_sel = 1.0 / jnp.sum(jnp.where(is_g, jnp.exp(lg - mg), 0.0), 0, keepdims=True)
    in_grp = (row_i < N_EXPERTS) & ((row_i >> 3).astype(f32) == gsel)
    le = jnp.where(in_grp, logits, -jnp.inf)
    m1 = jnp.max(le, 0, keepdims=True)
    i1 = jnp.min(jnp.where(le == m1, row, big), 0, keepdims=True)
    le2 = jnp.where(row == i1, -jnp.inf, le)
    m2 = jnp.max(le2, 0, keepdims=True)
    i2 = jnp.min(jnp.where(le2 == m2, row, big), 0, keepdims=True)
    e2 = jnp.exp(m2 - m1)
    w1 = pg_sel / (1.0 + e2)
    w2 = pg_sel * e2 / (1.0 + e2)
    sub = lax.broadcasted_iota(jnp.int32, route_ref.shape, 0)
    route_ref[...] = jnp.where(sub == 0, i1, jnp.where(sub == 1, i2, jnp.where(sub == 2, w1,
                                                                                jnp.where(sub == 3, w2, 0.0))))


def _router(xc, xl, g_norm, mods, p):
    wr = jnp.zeros((LANES, D), f32).at[:N_EXPERTS].set(p['w_re'].T).at[
        N_EXPERTS:N_EXPERTS + N_GROUPS].set(p['w_rg'].T)
    w_hi, w_lo = _split_bf16(wr)
    w_stack = w_hi.at[W_LO_ROW0:W_LO_ROW0 + ROUTER_ROWS].set(w_lo[:ROUTER_ROWS])
    br = jnp.zeros((LANES, 1), f32).at[:N_EXPERTS, 0].set(p['b_re']).at[
        N_EXPERTS:N_EXPERTS + N_GROUPS, 0].set(p['b_rg'])
    return pl.pallas_call(
        _router_kernel,
        grid=(T // ROW_TILE,),
        in_specs=_token_specs(D) + [
                  _const_spec((1, D)),
                  pl.BlockSpec((MOD_ROWS, D), lambda i: (_sample_of_tile(i, ROW_TILE), 0)),
                  _const_spec((LANES, D)), _const_spec((LANES, D)), _const_spec((LANES, 1))],
        out_specs=[_token_major_spec(ROW_TILE, lambda i: (i, 0)),
                   pl.BlockSpec((ROUTE_ROWS, ROW_TILE), lambda i: (0, i))],
        out_shape=[jax.ShapeDtypeStruct((T * SUBS, LANES), f32), jax.ShapeDtypeStruct((ROUTE_ROWS, T), f32)],
        scratch_shapes=[pltpu.VMEM((ROW_TILE // ROUTER_CHUNK, ROUTER_ROWS, ROUTER_CHUNK), f32)],
        compiler_params=_cp(("parallel",)), name="router",
    )(xc, xl, g_norm.reshape(1, D), mods, w_hi, w_stack, br)


PLAN_FIRST_TILE, PLAN_TILES = 0, 1


def _plan_kernel(rt_ref, pos_ref, plan_ref, rank):
    n_blk = T // 128
    e_col = lax.broadcasted_iota(jnp.int32, (N_EXPERTS, 128), 0).astype(f32)
    ri = lax.broadcasted_iota(jnp.int32, (128, 128), 0)
    ci = lax.broadcasted_iota(jnp.int32, (128, 128), 1)
    before = jnp.where(ri < ci, 1.0, 0.0).astype(bf16)

    def picks(b):
        cs = slice(b * 128, (b + 1) * 128)
        return rt_ref[0:1, cs] == e_col, rt_ref[1:2, cs] == e_col

    counts = jnp.zeros((N_EXPERTS, 1), f32)
    for b in range(n_blk):
        m0, m1 = picks(b)
        m = jnp.where(m0, 1.0, 0.0) + jnp.where(m1, 1.0, 0.0)
        rank[:, b * 128:(b + 1) * 128] = jnp.dot(m.astype(bf16), before, preferred_element_type=f32) + counts
        counts = counts + jnp.sum(m, axis=1, keepdims=True)

    tiles = jnp.floor((counts + (MOE_TILE - 1.0)) * (1.0 / MOE_TILE))
    er = lax.broadcasted_iota(jnp.int32, (N_EXPERTS, N_EXPERTS), 0)
    ec = lax.broadcasted_iota(jnp.int32, (N_EXPERTS, N_EXPERTS), 1)
    earlier = jnp.where(ec < er, 1.0, 0.0).astype(bf16)
    tile_start = jnp.dot(earlier, jnp.broadcast_to(tiles, (N_EXPERTS, 128)).astype(bf16),
                         preferred_element_type=f32)
    row_start = tile_start * MOE_TILE

    sub = lax.broadcasted_iota(jnp.int32, (8, 128), 0)
    for b in range(n_blk):
        m0, m1 = picks(b)
        base = rank[:, b * 128:(b + 1) * 128] + row_start
        p0 = jnp.sum(jnp.where(m0, base, 0.0), axis=0, keepdims=True)
        p1 = jnp.sum(jnp.where(m1, base, 0.0), axis=0, keepdims=True)
        pos_ref[:, b * 128:(b + 1) * 128] = jnp.where(sub == 0, p0, jnp.where(sub == 1, p1, 0.0)).astype(jnp.int32)

    diag = (lax.broadcasted_iota(jnp.int32, (N_EXPERTS, 128), 0)
            == lax.broadcasted_iota(jnp.int32, (N_EXPERTS, 128), 1))
    first = jnp.sum(jnp.where(diag, tile_start, 0.0), axis=0, keepdims=True)
    count = jnp.sum(jnp.where(diag, tiles, 0.0), axis=0, keepdims=True)
    rows = jnp.where(sub == PLAN_FIRST_TILE, first, jnp.where(sub == PLAN_TILES, count, 0.0))
    plan_ref[...] = rows.astype(jnp.int32)


def _slot_code(t, k):
    return t * SUBS + k * (SUBS // 2)


def _code_offset(code):
    return pl.multiple_of(code & ~(SUBS - 1), SUBS)


def _code_gate_index(code):
    return code >> 2


PAD_CODE = T * SUBS


def _invert_slots(pos_ref, first_ref, count_ref, code_ref):
    def pad_tile(tile, carry):
        for u in range(MOE_TILE):
            code_ref[tile * MOE_TILE + u] = PAD_CODE
        return carry

    def pad_last_tile(e, carry):
        return pad_tile(jnp.maximum(first_ref[e] + count_ref[e] - 1, 0), carry)
    lax.fori_loop(0, N_EXPERTS, pad_last_tile, 0)
    lax.fori_loop(first_ref[N_EXPERTS - 1] + count_ref[N_EXPERTS - 1], MOE_TILES, pad_tile, 0)

    group = 16
    for k in range(2):
        def place(i, carry):
            t0 = i * group
            slots = [pos_ref[k * T + t0 + u] for u in range(group)]
            for u, s in enumerate(slots):
                code_ref[s] = _slot_code(t0 + u, k)
            return carry
        lax.fori_loop(0, T // group, place, 0)


def _route_plan(route_t):
    pos, plan = pl.pallas_call(
        _plan_kernel,
        out_shape=[jax.ShapeDtypeStruct((8, T), jnp.int32), jax.ShapeDtypeStruct((8, LANES), jnp.int32)],
        scratch_shapes=[pltpu.VMEM((N_EXPERTS, T), f32)],
        compiler_params=_cp(None), name="route_plan",
    )(route_t)
    gates = jnp.pad(route_t[2:4].T.reshape(2 * T), (0, 8))
    return plan[PLAN_FIRST_TILE, :N_EXPERTS], plan[PLAN_TILES, :N_EXPERTS], pos[0:2].reshape(2 * T), gates


def _tile_index(i):
    return jnp.minimum(i, MOE_TILES - 1)


TM_ROWS = T * SUBS
SCATTER_GROUP = 8
STAGES = 4
LAST_EXPERT = N_EXPERTS - 1


def _tile_rows(g):
    return pl.ds(pl.multiple_of(g * MOE_TILE, MOE_TILE), MOE_TILE)


def _expert_tile_pairs(first, count, tile_step):
    def pair(pp, carry):
        for parity in range(2):
            g = 2 * pp + parity
            pl.when((g >= first) & (g < first + count))(functools.partial(tile_step, g, parity))
        return carry
    lax.fori_loop(first // 2, (first + count + 1) // 2, pair, 0)


def _gather_tile(code_ref, tile, xs, gbuf):
    base = tile * MOE_TILE
    for r in range(MOE_TILE):
        gbuf[r * SUBS:(r + 1) * SUBS, :] = xs[pl.ds(_code_offset(code_ref[base + r]), SUBS), :]


def _moe_up_kernel(first_ref, count_ref, pos_ref, h_hbm, w1_ref, w3_ref, hh_hbm, code_ref,
                   xs, gbuf_a, gbuf_b, w13, obuf, sem_x, sem_o):
    e = pl.program_id(0)
    first, count = first_ref[e], count_ref[e]
    gbufs = (gbuf_a, gbuf_b)

    def out_copy(slot, g):
        return pltpu.make_async_copy(obuf.at[slot], hh_hbm.at[_tile_rows(g), :], sem_o.at[slot])

    @pl.when(e == 0)
    def _():
        cp = pltpu.make_async_copy(h_hbm, xs.at[pl.ds(0, TM_ROWS), :], sem_x)
        cp.start()
        _invert_slots(pos_ref, first_ref, count_ref, code_ref)
        xs[TM_ROWS:TM_ROWS + SUBS, :] = jnp.zeros((SUBS, LANES), f32)
        cp.wait()
        _gather_tile(code_ref, 0, xs, gbuf_a)

    @pl.when(count > 0)
    def _():
        w13[:, :D_EXPERT] = w1_ref[0].astype(bf16)
        w13[:, D_EXPERT:] = w3_ref[0].astype(bf16)

    def tile_step(g, parity):
        _gather_tile(code_ref, _tile_index(g + 1), xs, gbufs[1 - parity])
        x3 = jnp.swapaxes(gbufs[parity][...].reshape(MOE_TILE, SUBS, LANES), 0, 1)
        x = jnp.concatenate([x3[s] for s in range(SUBS)], axis=1).astype(bf16)
        h13 = jnp.dot(x, w13[...], preferred_element_type=f32)
        hh = (_silu(h13[:, :D_EXPERT]) * h13[:, D_EXPERT:]).astype(bf16)

        slot = g % STAGES

        @pl.when(g >= STAGES)
        def _():
            out_copy(slot, g).wait()
        obuf[slot] = hh
        out_copy(slot, g).start()

    _expert_tile_pairs(first, count, tile_step)

    @pl.when(e == LAST_EXPERT)
    def _():
        n_used = first + count
        for slot in range(STAGES):
            pl.when(n_used > slot)(lambda slot=slot: out_copy(slot, 0).wait())
        obuf[0] = jnp.zeros((MOE_TILE, D_EXPERT), bf16)

        def zero_tile(g, carry):
            cp = out_copy(0, g)
            cp.start()
            cp.wait()
            return carry
        lax.fori_loop(n_used, MOE_TILES, zero_tile, 0)


def _scatter_tile(code_ref, gate_ref, tile, ybuf, acc):
    base = tile * MOE_TILE
    for g0 in range(0, MOE_TILE, SCATTER_GROUP):
        rows = range(g0, g0 + SCATTER_GROUP)
        codes = [code_ref[base + r] for r in rows]
        new = [acc[pl.ds(_code_offset(c), SUBS), :]
               + gate_ref[_code_gate_index(c)] * ybuf[r * SUBS:(r + 1) * SUBS, :]
               for r, c in zip(rows, codes)]
        for c, v in zip(codes, new):
            acc[pl.ds(_code_offset(c), SUBS), :] = v


RES_ROWS = 256


def _residual_out(x_hbm, y_hbm, tok0, sample_of_chunk, mod_ref, acc, rin, rout, sem_r, sem_w):
    n_chunks = x_hbm.shape[0] // RES_ROWS

    def rows(c):
        return pl.ds(pl.multiple_of(c * RES_ROWS, RES_ROWS), RES_ROWS)

    def in_copy(slot, c):
        return pltpu.make_async_copy(x_hbm.at[rows(c), :], rin.at[slot], sem_r.at[slot])

    def out_copy(slot, c):
        return pltpu.make_async_copy(rout.at[slot], y_hbm.at[rows(c), :], sem_w.at[slot])

    for c in range(STAGES - 1):
        in_copy(c, c).start()

    def ring(cc, carry):
        for slot in range(STAGES):
            c = STAGES * cc + slot
            in_copy(slot, c).wait()
            ahead = c + STAGES - 1

            @pl.when(ahead < n_chunks)
            def _():
                in_copy((slot + STAGES - 1) % STAGES, ahead).start()

            @pl.when(c >= STAGES)
            def _():
                out_copy(slot, c).wait()
            delta = _load_token_major(acc, RES_ROWS, tok0 + c * RES_ROWS)
            gate = mod_ref[pl.ds(sample_of_chunk(c) * MOD_ROWS + GATE2, 1), :]
            rout[slot] = rin[slot] + gate * delta
            out_copy(slot, c).start()
        return carry

    lax.fori_loop(0, n_chunks // STAGES, ring, 0)
    for slot in range(STAGES):
        out_copy(slot, 0).wait()


def _moe_down_kernel(first_ref, count_ref, code_ref, gate_ref, hh_hbm, w2_ref, xc_hbm, xl_hbm, mod_ref,
                     yc_hbm, yl_hbm, acc, ybuf_a, ybuf_b, w2b, ibuf, rin, rout, sem_i, sem_r, sem_w):
    e = pl.program_id(0)
    first, count = first_ref[e], count_ref[e]
    n_used = first_ref[LAST_EXPERT] + count_ref[LAST_EXPERT]
    ybufs = (ybuf_a, ybuf_b)

    def in_copy(slot, g):
        return pltpu.make_async_copy(hh_hbm.at[_tile_rows(g), :], ibuf.at[slot], sem_i.at[slot])

    @pl.when(e == 0)
    def _():
        for g in range(STAGES - 1):
            in_copy(g, g).start()

        def zero(c, carry):
            acc[pl.ds(pl.multiple_of(c * 1024, 1024), 1024), :] = jnp.zeros((1024, LANES), f32)
            return carry
        lax.fori_loop(0, TM_ROWS // 1024, zero, 0)
        acc[TM_ROWS:TM_ROWS + SUBS, :] = jnp.zeros((SUBS, LANES), f32)
        ybuf_b[...] = jnp.zeros_like(ybuf_b)

    @pl.when(count > 0)
    def _():
        w2b[...] = w2_ref[0].astype(bf16)

    def tile_step(g, parity):
        slot = g % STAGES
        in_copy(slot, g).wait()
        ahead = g + STAGES - 1

        @pl.when(ahead < n_used)
        def _():
            in_copy(ahead % STAGES, ahead).start()
        _store_token_major(ybufs[parity], jnp.dot(ibuf[slot], w2b[...], preferred_element_type=f32))
        _scatter_tile(code_ref, gate_ref, jnp.maximum(g - 1, 0), ybufs[1 - parity], acc)

    _expert_tile_pairs(first, count, tile_step)

    @pl.when(e == LAST_EXPERT)
    def _():
        for parity in range(2):
            pl.when((n_used > 0) & ((n_used - 1) % 2 == parity))(
                functools.partial(_scatter_tile, code_ref, gate_ref, n_used - 1, ybufs[parity], acc))
        _residual_out(xc_hbm, yc_hbm, 0, lambda c: 0, mod_ref, acc, rin, rout, sem_r, sem_w)
        _residual_out(xl_hbm, yl_hbm, T_CTX, lambda c: 1 + c // (DEC_SEQ // RES_ROWS),
                      mod_ref, acc, rin, rout, sem_r, sem_w)


def _moe(h_tm, route_t, p, xc, xl, mods):
    first_tile, n_tiles, slots, gates = _route_plan(route_t)
    tile_rows = pltpu.VMEM((MOE_TILE * SUBS, LANES), f32)
    staging = pltpu.VMEM((STAGES, MOE_TILE, D_EXPERT), bf16)
    res_rows = pltpu.VMEM((STAGES, RES_ROWS, D), f32)
    hbm = pl.BlockSpec(memory_space=pl.ANY)
    hh, codes = pl.pallas_call(
        _moe_up_kernel,
        grid_spec=pltpu.PrefetchScalarGridSpec(
            num_scalar_prefetch=3, grid=(N_EXPERTS,),
            in_specs=[hbm,
                      pl.BlockSpec((1, D, D_EXPERT), lambda e, f, n, s: (e, 0, 0)),
                      pl.BlockSpec((1, D, D_EXPERT), lambda e, f, n, s: (e, 0, 0))],
            out_specs=[hbm, pl.BlockSpec(memory_space=pltpu.SMEM)],
            scratch_shapes=[pltpu.VMEM((TM_ROWS + SUBS, LANES), f32), tile_rows, tile_rows,
                            pltpu.VMEM((D, 2 * D_EXPERT), bf16), staging,
                            pltpu.SemaphoreType.DMA(()), pltpu.SemaphoreType.DMA((STAGES,))]),
        out_shape=[jax.ShapeDtypeStruct((MOE_ROWS, D_EXPERT), bf16),
                   jax.ShapeDtypeStruct((MOE_ROWS,), jnp.int32)],
        compiler_params=_cp(("arbitrary",)), name="moe_up",
    )(first_tile, n_tiles, slots, h_tm, p['w1'], p['w3'])
    return pl.pallas_call(
        _moe_down_kernel,
        grid_spec=pltpu.PrefetchScalarGridSpec(
            num_scalar_prefetch=4, grid=(N_EXPERTS,),
            in_specs=[hbm, pl.BlockSpec((1, D_EXPERT, D), lambda e, f, n, c, g: (e, 0, 0)), hbm, hbm,
                      pl.BlockSpec((N_SAMPLES * MOD_ROWS, D), lambda e, f, n, c, g: (0, 0),
                                   pipeline_mode=pl.Buffered(1))],
            out_specs=[hbm, hbm],
            scratch_shapes=[pltpu.VMEM((TM_ROWS + SUBS, LANES), f32), tile_rows, tile_rows,
                            pltpu.VMEM((D_EXPERT, D), bf16), staging, res_rows, res_rows,
                            pltpu.SemaphoreType.DMA((STAGES,)), pltpu.SemaphoreType.DMA((STAGES,)),
                            pltpu.SemaphoreType.DMA((STAGES,))]),
        out_shape=[jax.ShapeDtypeStruct((T_CTX, D), f32), jax.ShapeDtypeStruct((T_LAT, D), f32)],
        compiler_params=_cp(("arbitrary",)), name="moe_down",
    )(first_tile, n_tiles, codes, gates, hh, p['w2'], xc, xl, mods)


def kernel(x_prompt, x_sample, cache_l0_k, cache_l0_v, cache_l1_ckv, cache_l1_kpe, c, c_ctx, l0_g_norm1, l0_g_norm2, l0_w_ada, l0_b_ada, l0_w_in, l0_g_vnorm, l0_w_s, l0_b_s, l0_g_q, l0_g_k, l0_sink, l0_w_o, l0_w_rg, l0_b_rg, l0_w_re, l0_b_re, l0_w1, l0_w3, l0_w2, l1_g_norm1, l1_g_norm2, l1_w_ada, l1_b_ada, l1_w_in, l1_g_qa, l1_w_uq, l1_g_kva, l1_w_ukv, l1_g_q, l1_g_k, l1_w_o, l1_w_rg, l1_b_rg, l1_w_re, l1_b_re, l1_w1, l1_w3, l1_w2):
    p0 = dict(w_in=l0_w_in, g_vnorm=l0_g_vnorm, w_s=l0_w_s, b_s=l0_b_s, g_q=l0_g_q, g_k=l0_g_k, sink=l0_sink,
              w_o=l0_w_o, w_rg=l0_w_rg, b_rg=l0_b_rg, w_re=l0_w_re, b_re=l0_b_re, w1=l0_w1, w3=l0_w3, w2=l0_w2)
    p1 = dict(w_in=l1_w_in, g_qa=l1_g_qa, w_uq=l1_w_uq, g_kva=l1_g_kva, w_ukv=l1_w_ukv, g_q=l1_g_q, g_k=l1_g_k,
              w_o=l1_w_o, w_rg=l1_w_rg, b_rg=l1_b_rg, w_re=l1_w_re, b_re=l1_b_re, w1=l1_w1, w3=l1_w3, w2=l1_w2)

    cond8 = jnp.zeros((MOD_ROWS, D), f32).at[0].set(c_ctx).at[1:1 + DEC_BATCH].set(c)
    mods0 = _mod_rows(_adaln(cond8, l0_w_ada, l0_b_ada))
    mods1 = _mod_rows(_adaln(cond8, l1_w_ada, l1_b_ada))

    xc0 = x_prompt.reshape(T_CTX, D)
    xl0 = x_sample.reshape(T_LAT, D)

    xc0m, xl0m, k_new, v_new = _l0_mixer(xc0, xl0, l0_g_norm1, mods0, p0, cache_l0_k, cache_l0_v)
    h0, route0 = _router(xc0m, xl0m, l0_g_norm2, mods0, p0)
    xc1, xl1 = _moe(h0, route0, p0, xc0m, xl0m, mods0)

    xc1m, xl1m, ckv_new, kpe_new = _l1_mixer(xc1, xl1, l1_g_norm1, mods1, p1, cache_l1_ckv, cache_l1_kpe)
    h1, route1 = _router(xc1m, xl1m, l1_g_norm2, mods1, p1)
    y_prompt, y_sample = _moe(h1, route1, p1, xc1m, xl1m, mods1)
    return (y_prompt.reshape(BATCH, SEQ, D), y_sample.reshape(DEC_BATCH, DEC_SEQ, D), k_new, v_new,
            ckv_new.reshape(BATCH, SEQ, C_KV_LORA), kpe_new.reshape(BATCH, SEQ, C_ROPE))
```

```python
import functools

import jax
import jax.numpy as jnp
import numpy as np
from jax import lax
from jax.experimental import pallas as pl
from jax.experimental.pallas import tpu as pltpu

f32 = jnp.float32
bf16 = jnp.bfloat16

D = 1024
BATCH, SEQ = 32, 256
DEC_BATCH, DEC_SEQ = 2, 1024
PAST = 512
T_CTX = BATCH * SEQ
T_LAT = DEC_BATCH * DEC_SEQ
T = T_CTX + T_LAT
GRID_W = 64
CHUNK = 128
WINDOW = 128
ROPE_THETA = 10000.0
EPS = 1e-6
NEG_INF = -1e30
LANES = 128
SUBS = D // LANES

A_WIDTH = 512
A_GROUPS = 4
B_HEADS, B_KV, B_GROUP, B_HD = 8, 2, 4, 64
LOG2E = 1.4426950408889634
B_SCALE = B_HD ** -0.5 * LOG2E

C_HEADS, C_Q_LORA, C_KV_LORA, C_NOPE, C_ROPE, C_V = 16, 384, 256, 64, 32, 64
C_QK = C_NOPE + C_ROPE
C_SCALE = C_QK ** -0.5 * LOG2E
ODD_IN_PAD = 768
SLOT = 128

N_GROUPS, N_EXPERTS, D_EXPERT = 4, 32, 256

N_SAMPLES = 1 + DEC_BATCH
MOD_ROWS = 8
SHIFT1, SCALE1, GATE1, SHIFT2, SCALE2, GATE2 = range(6)

ROW_TILE = 1024
BLOCK_ROWS = 1024
PROJ_ROWS = 512
ADALN_COLS = 1536
MOE_TILE = 256
MOE_ROWS = 2 * T + N_EXPERTS * MOE_TILE
MOE_TILES = MOE_ROWS // MOE_TILE
VMEM_CAP = 56 * 1024 * 1024


def _cp(sem, vmem=VMEM_CAP):
    return pltpu.CompilerParams(dimension_semantics=sem, vmem_limit_bytes=vmem)


def _const_spec(shape):
    nd = len(shape)
    return pl.BlockSpec(shape, lambda *_: (0,) * nd, pipeline_mode=pl.Buffered(1))


def _sample_of_tile(i, tile):
    n_ctx = T_CTX // tile
    per_lat = DEC_SEQ // tile
    return jnp.where(i < n_ctx, 0, 1 + (i - n_ctx) // per_lat)


def _mod(mod_ref, row):
    return mod_ref[row:row + 1, :]


def _silu(x):
    return x * jax.nn.sigmoid(x)


def _rms_rows(x, g):
    return x * lax.rsqrt(jnp.mean(x * x, -1, keepdims=True) + EPS) * g


def _swap_pairs(x):
    lane = lax.broadcasted_iota(jnp.int32, x.shape, x.ndim - 1)
    nxt = pltpu.roll(x, x.shape[-1] - 1, x.ndim - 1)
    prv = pltpu.roll(x, 1, x.ndim - 1)
    return jnp.where((lane & 1) == 0, nxt, prv)


def _split_bf16(x):
    hi = x.astype(bf16)
    return hi, (x - hi.astype(f32)).astype(bf16)


def _adaln_kernel(c_ref, w_ref, b_ref, o_ref):
    s_hi, s_lo = _split_bf16(_silu(c_ref[...]))
    w_hi, w_lo = _split_bf16(w_ref[...])
    o_ref[...] = (jnp.dot(s_hi, w_hi, preferred_element_type=f32) + jnp.dot(s_lo, w_hi, preferred_element_type=f32)
                  + jnp.dot(s_hi, w_lo, preferred_element_type=f32) + b_ref[...])


def _adaln(cond8, w, b):
    n = w.shape[1]
    return pl.pallas_call(
        _adaln_kernel,
        grid=(n // ADALN_COLS,),
        in_specs=[_const_spec((MOD_ROWS, D)), pl.BlockSpec((D, ADALN_COLS), lambda j: (0, j)),
                  pl.BlockSpec((1, ADALN_COLS), lambda j: (0, j))],
        out_specs=pl.BlockSpec((MOD_ROWS, ADALN_COLS), lambda j: (0, j)),
        out_shape=jax.ShapeDtypeStruct((MOD_ROWS, n), f32),
        compiler_params=_cp(("arbitrary",)),
        name="adaln",
    )(cond8, w, b.reshape(1, n))


def _mod_rows(m8):
    m = m8[:N_SAMPLES].reshape(N_SAMPLES, 6, D)
    return jnp.pad(m, ((0, 0), (0, MOD_ROWS - 6), (0, 0))).reshape(N_SAMPLES * MOD_ROWS, D)


N_CTX_TILES = T_CTX // ROW_TILE


def _token_specs(width):
    return [pl.BlockSpec((ROW_TILE, width), lambda i: (jnp.minimum(i, N_CTX_TILES - 1), 0)),
            pl.BlockSpec((ROW_TILE, width), lambda i: (jnp.maximum(i - N_CTX_TILES, 0), 0))]


def _store_token_major(ref, x, row0=0):
    n = x.shape[0]
    for s in range(SUBS):
        ref[pl.ds(row0 * SUBS + s, n, stride=SUBS), :] = x[:, s * LANES:(s + 1) * LANES]


def _load_token_major(ref, n, row0=0):
    return jnp.concatenate([ref[pl.ds(row0 * SUBS + s, n, stride=SUBS), :] for s in range(SUBS)], axis=1)


def _token_major_spec(rows, index_map):
    return pl.BlockSpec((rows * SUBS, LANES), index_map)


def _rope_tables(n, rot_dim, lanes, lane0, copies=1):
    rows_count = n // GRID_W
    rows = np.repeat(np.arange(rows_count), GRID_W).astype(np.float64)
    cols = np.tile(np.arange(GRID_W), rows_count).astype(np.float64)
    d_axis = rot_dim // 2
    inv = ROPE_THETA ** (-np.arange(0, d_axis, 2, dtype=np.float64) / d_axis)
    ang = np.concatenate([rows[:, None] * inv, cols[:, None] * inv], -1)
    c = np.ones((n, lanes), np.float32)
    s = np.zeros((n, lanes), np.float32)
    for j in range(copies):
        lo = lane0 + j * rot_dim
        c[:, lo:lo + rot_dim] = np.repeat(np.cos(ang), 2, axis=1)
        s[:, lo:lo + rot_dim] = np.repeat(np.sin(ang), 2, axis=1) * np.tile(np.array([-1.0, 1.0]), rot_dim // 2)
    return jnp.asarray(c), jnp.asarray(s)


L0_Q0 = 2 * A_WIDTH
L0_K0 = L0_Q0 + B_GROUP * LANES
L0_V0 = L0_K0 + B_KV * B_HD
L0_IN = L0_V0 + B_KV * B_HD


def _l0_kernel(*refs, latent):
    if latent:
        (sink_ref, x_ref, gn_ref, mod_ref, win_ref, gvn_ref, ws_ref, bsb_ref, gq_ref, gk_ref, wo_ref,
         cos_ref, sin_ref, kc_ref, vc_ref, xo_ref, zs, cat, qs, ks, vt, kcb, vct) = refs
        key_off = WINDOW
    else:
        (sink_ref, x_ref, gn_ref, mod_ref, win_ref, gvn_ref, ws_ref, bsb_ref, gq_ref, gk_ref, wo_ref,
         xo_ref, ko_ref, vo_ref, zs, cat, qs, ks, vt, kf, vf) = refs
        key_off = 0
    n = BLOCK_ROWS
    n_chunks = n // CHUNK
    low = lax.broadcasted_iota(jnp.int32, (CHUNK, LANES), 1) < B_HD

    if latent:
        zpad = jnp.zeros((WINDOW, LANES), bf16)
        for c0 in (0, 1 + n_chunks):
            ks[c0 * CHUNK:(c0 + 1) * CHUNK, :] = zpad
            vt[c0] = zpad
        kcb[...] = kc_ref[...].astype(bf16)
        for i in range(PAST // CHUNK):
            vct[i] = vc_ref[i * CHUNK:(i + 1) * CHUNK, :].T.astype(bf16)

    def project(c, carry):
        r = pl.ds(pl.multiple_of(c * PROJ_ROWS, PROJ_ROWS), PROJ_ROWS)
        h = _rms_rows(x_ref[r, :], gn_ref[...]) * (1.0 + _mod(mod_ref, SCALE1)) + _mod(mod_ref, SHIFT1)
        zs[r, :] = jnp.dot(h.astype(bf16), win_ref[...], preferred_element_type=f32)
        return carry

    lax.fori_loop(0, n // PROJ_ROWS, project, 0)

    def prepare(c, carry):
        r = pl.ds(pl.multiple_of(c * CHUNK, CHUNK), CHUNK)
        u = jax.nn.gelu(zs[r, 0:A_WIDTH])
        v = jax.nn.gelu(zs[r, A_WIDTH:2 * A_WIDTH])
        mu = jnp.mean(v, -1, keepdims=True)
        var = jnp.mean(jnp.square(v - mu), -1, keepdims=True)
        vn = ((v - mu) * lax.rsqrt(var + EPS) * gvn_ref[...]).astype(bf16)
        for g in range(A_GROUPS):
            cs = slice(g * CHUNK, (g + 1) * CHUNK)
            mixed = jnp.dot(ws_ref[g], vn[:, cs], preferred_element_type=f32) + bsb_ref[g]
            cat[r, cs] = (u[:, cs] * mixed).astype(bf16)
        if latent:
            cs_, sn_ = cos_ref[r, :], sin_ref[r, :]
        def half_norm(v, gain):
            sq = v * v
            s0 = jnp.sum(jnp.where(low, sq, 0.0), -1, keepdims=True)
            s1 = jnp.sum(jnp.where(low, 0.0, sq), -1, keepdims=True)
            return v * lax.rsqrt(jnp.where(low, s0, s1) * (1.0 / B_HD) + EPS) * gain

        for j in range(B_GROUP):
            js = slice(j * LANES, (j + 1) * LANES)
            qj = half_norm(zs[r, L0_Q0 + j * LANES:L0_Q0 + (j + 1) * LANES], gq_ref[...])
            if latent:
                qj = qj * cs_ + _swap_pairs(qj) * sn_
            qs[r, js] = qj.astype(bf16)
        k = half_norm(zs[r, L0_K0:L0_K0 + LANES], gk_ref[...])
        vv = zs[r, L0_V0:L0_V0 + LANES]
        if latent:
            k = k * cs_ + _swap_pairs(k) * sn_
        else:
            kf[r, :] = k
            vf[r, :] = vv
        kr = pl.ds(pl.multiple_of(c * CHUNK + key_off, CHUNK), CHUNK)
        ks[kr, :] = k.astype(bf16)
        vt[c + key_off // CHUNK] = vv.T.astype(bf16)
        return carry

    lax.fori_loop(0, n_chunks // 2, lambda i, carry: prepare(2 * i + 1, prepare(2 * i, carry)), 0)

    def attend(jobs):
        scored = []
        for r, rows, key_sets in jobs:
            low_q = lax.broadcasted_iota(jnp.int32, (rows, LANES), 1) < B_HD
            slots = [qs[r, j * LANES:(j + 1) * LANES] for j in range(B_GROUP)]
            zero = jnp.zeros((rows, LANES), bf16)
            q = jnp.concatenate([jnp.where(low_q, qj, zero) for qj in slots]
                                + [jnp.where(low_q, zero, qj) for qj in slots], axis=0)
            sk = jnp.concatenate([jnp.full((1, rows), sink_ref[h] * LOG2E, f32) for h in range(B_HEADS)], axis=1)
            scores = []
            m = sk
            for k, _, keep in key_sets:
                s = lax.dot_general(k, q, (((1,), (1,)), ((), ())), preferred_element_type=f32)
                if keep is not None:
                    s = jnp.where(keep, s, NEG_INF)
                scores.append(s)
                m = jnp.maximum(m, jnp.max(s, 0, keepdims=True))
            scored.append((sk, scores, m))
        outs = []
        for (r, rows, key_sets), (sk, scores, m) in zip(jobs, scored):
            den = jnp.exp2(sk - m)
            ot = None
            for s, (_, vts, _) in zip(scores, key_sets):
                e = jnp.exp2(s - m)
                den = den + jnp.sum(e, 0, keepdims=True)
                eb = e.astype(bf16)
                for i, v_t in enumerate(vts):
                    pv = jnp.dot(v_t, eb[i * CHUNK:(i + 1) * CHUNK, :], preferred_element_type=f32)
                    ot = pv if ot is None else ot + pv
            outs.append(ot * (1.0 / den))
        for (r, rows, _), ot in zip(jobs, outs):
            for pair in range(B_HEADS // 2):
                f0 = (2 * pair // B_GROUP) * B_HD
                pair_t = jnp.concatenate([ot[f0:f0 + B_HD, 2 * pair * rows:(2 * pair + 1) * rows],
                                          ot[f0:f0 + B_HD, (2 * pair + 1) * rows:(2 * pair + 2) * rows]], axis=0)
                cat[r, A_WIDTH + pair * LANES:A_WIDTH + (pair + 1) * LANES] = pair_t.T.astype(bf16)

    jobs_per_step = 2 if latent else 4
    if latent:
        span = CHUNK + 2 * WINDOW

        def block_job(c):
            start = pl.multiple_of(c * CHUNK, CHUNK)
            kr = pl.ds(start, span)
            kj = lax.broadcasted_iota(jnp.int32, (span, B_HEADS * CHUNK), 0)
            qi = lax.broadcasted_iota(jnp.int32, (span, B_HEADS * CHUNK), 1) & (CHUNK - 1)
            kpos = start - WINDOW + kj
            keep = (jnp.abs(kj - WINDOW - qi) <= WINDOW) & (kpos >= 0) & (kpos < n)
            return (pl.ds(start, CHUNK), CHUNK,
                    [(ks[kr, :], [vt[c + i] for i in range(span // CHUNK)], keep),
                     (kcb[...], [vct[i] for i in range(PAST // CHUNK)], None)])

        def attend_blocks(i, carry):
            attend([block_job(i * jobs_per_step + j) for j in range(jobs_per_step)])
            return carry

        lax.fori_loop(0, n_chunks // jobs_per_step, attend_blocks, 0)
    else:
        def seq_job(sq):
            r = pl.ds(pl.multiple_of(sq * SEQ, SEQ), SEQ)
            return r, SEQ, [(ks[r, :], [vt[sq * (SEQ // CHUNK) + i] for i in range(SEQ // CHUNK)], None)]

        def attend_seqs(i, carry):
            seqs = [i * jobs_per_step + j for j in range(jobs_per_step)]
            attend([seq_job(sq) for sq in seqs])
            for sq in seqs:
                r = pl.ds(pl.multiple_of(sq * SEQ, SEQ), SEQ)
                ko_ref[sq] = kf[r, :].T
                vo_ref[sq] = vf[r, :].T
            return carry

        lax.fori_loop(0, n // SEQ // jobs_per_step, attend_seqs, 0)

    def output(c, carry):
        r = pl.ds(pl.multiple_of(c * PROJ_ROWS, PROJ_ROWS), PROJ_ROWS)
        y = jnp.dot(cat[r, :], wo_ref[...], preferred_element_type=f32)
        xo_ref[r, :] = x_ref[r, :] + _mod(mod_ref, GATE1) * y
        return carry

    lax.fori_loop(0, n // PROJ_ROWS, output, 0)


def _l0_mixer(xc, xl, g_norm, mods, p, cache_k, cache_v):
    w = p['w_in']
    q_slots = w[:, L0_Q0:L0_Q0 + B_HEADS * B_HD].reshape(D, B_KV, B_GROUP, B_HD).transpose(0, 2, 1, 3).reshape(
        D, B_GROUP * LANES)
    win = jnp.concatenate([w[:, :L0_Q0], q_slots, w[:, L0_Q0 + B_HEADS * B_HD:]], axis=1).astype(bf16)
    gq = jnp.tile(p['g_q'], B_KV).reshape(1, LANES) * B_SCALE
    gk = jnp.tile(p['g_k'], B_KV).reshape(1, LANES)
    gvn = p['g_vnorm'].reshape(1, A_WIDTH)
    ws = p['w_s'].astype(bf16)
    bsb = jnp.broadcast_to(p['b_s'][:, :, None], (A_GROUPS, CHUNK, CHUNK))
    wo = p['w_o'].astype(bf16)
    weights = (g_norm.reshape(1, D),)
    consts = (win, gvn, ws, bsb, gq, gk, wo)
    c_specs = [_const_spec(a.shape) for a in consts]
    smem = pl.BlockSpec(memory_space=pltpu.SMEM)
    row = pl.BlockSpec((BLOCK_ROWS, D), lambda b: (b, 0))
    kv = pl.BlockSpec((BLOCK_ROWS // SEQ, LANES, SEQ), lambda b: (b, 0, 0))

    def scratch(pad):
        return [pltpu.VMEM((BLOCK_ROWS, L0_IN), f32), pltpu.VMEM((BLOCK_ROWS, D), bf16),
                pltpu.VMEM((BLOCK_ROWS, B_GROUP * LANES), bf16), pltpu.VMEM((BLOCK_ROWS + pad, LANES), bf16),
                pltpu.VMEM(((BLOCK_ROWS + pad) // CHUNK, LANES, CHUNK), bf16)]

    kv_shape = jax.ShapeDtypeStruct((BATCH, LANES, SEQ), f32)
    state = pltpu.VMEM((BLOCK_ROWS, LANES), f32)
    xo_ctx, k_t, v_t = pl.pallas_call(
        functools.partial(_l0_kernel, latent=False),
        grid=(T_CTX // BLOCK_ROWS,),
        in_specs=[smem, row, _const_spec((1, D)), pl.BlockSpec((MOD_ROWS, D), lambda b: (0, 0))] + c_specs,
        out_specs=[row, kv, kv],
        out_shape=[jax.ShapeDtypeStruct((T_CTX, D), f32), kv_shape, kv_shape],
        scratch_shapes=scratch(0) + [state, state],
        compiler_params=_cp(("parallel",)), name="l0_mixer_ctx",
    )(p['sink'], xc, *weights, mods, *consts)
    k_new = k_t.reshape(BATCH, B_KV, B_HD, SEQ).transpose(0, 3, 1, 2)
    v_new = v_t.reshape(BATCH, B_KV, B_HD, SEQ).transpose(0, 3, 1, 2)

    cos, sin = _rope_tables(DEC_SEQ, B_HD, LANES, 0, copies=LANES // B_HD)
    cache = pl.BlockSpec((None, PAST, LANES), lambda b: (b, 0, 0))
    past = [pltpu.VMEM((PAST, LANES), bf16), pltpu.VMEM((PAST // CHUNK, LANES, CHUNK), bf16)]
    xo_lat = pl.pallas_call(
        functools.partial(_l0_kernel, latent=True),
        grid=(DEC_BATCH,),
        in_specs=[smem, row, _const_spec((1, D)), pl.BlockSpec((MOD_ROWS, D), lambda b: (1 + b, 0))] + c_specs + [
                  _const_spec(cos.shape), _const_spec(sin.shape), cache, cache],
        out_specs=row,
        out_shape=jax.ShapeDtypeStruct((T_LAT, D), f32),
        scratch_shapes=scratch(2 * WINDOW) + past,
        compiler_params=_cp(("parallel",)), name="l0_mixer_lat",
    )(p['sink'], xl, *weights, mods, *consts, cos, sin,
      cache_k.reshape(DEC_BATCH, PAST, LANES), cache_v.reshape(DEC_BATCH, PAST, LANES))
    return xo_ctx, xo_lat, k_new, v_new


C_SLOTS = C_HEADS * SLOT
C_PAIRS = C_HEADS // 2
L1_ROWS = 256


def _l1_kernel(*refs, latent):
    if latent:
        (x_ref, gn_ref, mod_ref, win_ref, gqa_ref, wuq_ref, gq_ref, gkva_ref, wuk_ref, wuvt_ref, gk_ref,
         wo_ref, wuqs_ref, qcos_ref, qsin_ref, kcos_ref, ksin_ref, cckv_ref, ckpe_ref, xo_ref,
         zs, cat, qs, ks, vt, wide, wide2) = refs
        n_ctx = PAST
    else:
        (x_ref, gn_ref, mod_ref, win_ref, gqa_ref, wuq_ref, gq_ref, gkva_ref, wuk_ref, wuvt_ref, gk_ref,
         wo_ref, xo_ref, ckvo_ref, kpeo_ref, zs, cat, qs, ks, vt, wide) = refs
        n_ctx = 0
    n = BLOCK_ROWS
    nt_dims = (((1,), (1,)), ((), ()))

    def inv_rms(v):
        return lax.rsqrt(jnp.sum(v * v, -1, keepdims=True) * (1.0 / C_QK) + EPS)

    def expand_keys(ckv_n, kslot, kb, rope_rows):
        cb = ckv_n.astype(bf16)
        key_rows = pl.ds(pl.multiple_of(kb * L1_ROWS, L1_ROWS), L1_ROWS)
        wide[...] = jnp.dot(cb, wuk_ref[...], preferred_element_type=f32)
        if rope_rows is not None:
            kcos = kcos_ref[rope_rows, :]
            turned = _swap_pairs(kslot) * ksin_ref[rope_rows, :]
        for h in range(C_HEADS):
            kh = wide[:, h * SLOT:(h + 1) * SLOT] + kslot
            if rope_rows is not None:
                kh = inv_rms(kh) * (kh * kcos + turned)
            else:
                kh = kh * inv_rms(kh) * gk_ref[...]
            ks[h, key_rows, :] = kh.astype(bf16)
        v_t = lax.dot_general(wuvt_ref[...], cb, nt_dims, preferred_element_type=f32).astype(bf16)
        for pair in range(C_PAIRS):
            vt[pair, kb] = v_t[pair * LANES:(pair + 1) * LANES, :]

    if latent:
        def past_keys(c, carry):
            r = pl.ds(pl.multiple_of(c * L1_ROWS, L1_ROWS), L1_ROWS)
            expand_keys(cckv_ref[r, :], ckpe_ref[r, :], c, None)
            return carry

        lax.fori_loop(0, PAST // L1_ROWS, past_keys, 0)

    def project(c, carry):
        r = pl.ds(pl.multiple_of(c * PROJ_ROWS, PROJ_ROWS), PROJ_ROWS)
        h = _rms_rows(x_ref[r, :], gn_ref[...]) * (1.0 + _mod(mod_ref, SCALE1)) + _mod(mod_ref, SHIFT1)
        zs[r, :] = jnp.dot(h.astype(bf16), win_ref[...], preferred_element_type=f32)
        return carry

    lax.fori_loop(0, n // PROJ_ROWS, project, 0)

    def prepare(c, carry):
        r = pl.ds(pl.multiple_of(c * L1_ROWS, L1_ROWS), L1_ROWS)
        qa = _rms_rows(zs[r, 0:C_Q_LORA], gqa_ref[...]).astype(bf16)
        wide[...] = jnp.dot(qa, wuq_ref[...], preferred_element_type=f32)
        if latent:
            wide2[...] = jnp.dot(qa, wuqs_ref[...], preferred_element_type=f32)
            qcos, qsin = qcos_ref[r, :], qsin_ref[r, :]
        for h in range(C_HEADS):
            hs = slice(h * SLOT, (h + 1) * SLOT)
            qh = wide[:, hs]
            if latent:
                qh = inv_rms(qh) * (qh * qcos + wide2[:, hs] * qsin)
            else:
                qh = qh * inv_rms(qh) * gq_ref[...]
            qs[h, r, :] = qh.astype(bf16)
        ckv_n = _rms_rows(zs[r, C_Q_LORA:C_Q_LORA + C_KV_LORA], gkva_ref[...])
        kslot = zs[r, C_Q_LORA + C_KV_LORA:ODD_IN_PAD]
        if not latent:
            ckvo_ref[r, :] = ckv_n
            kpeo_ref[c] = kslot.T[C_NOPE:C_QK, :]
        expand_keys(ckv_n, kslot, c + n_ctx // L1_ROWS, r if latent else None)
        return carry

    lax.fori_loop(0, n // L1_ROWS, prepare, 0)

    low = lax.broadcasted_iota(jnp.int32, (2 * C_V, L1_ROWS), 0) < C_V
    n_kblocks = (n_ctx + n) // L1_ROWS
    pairs_per_step = 4 if latent else 8

    blocks_per_step = 1 if latent else 2

    def attend(cc, carry):
        blocks = [cc * blocks_per_step + j for j in range(blocks_per_step)]
        rows = [pl.ds(pl.multiple_of(c * L1_ROWS, L1_ROWS), L1_ROWS) for c in blocks]

        def values_t(pair, c, eb):
            if not latent:
                return jnp.dot(vt[pair, c], eb, preferred_element_type=f32)
            o_t = None
            for b in range(n_kblocks):
                pv = jnp.dot(vt[pair, b], eb[b * L1_ROWS:(b + 1) * L1_ROWS, :], preferred_element_type=f32)
                o_t = pv if o_t is None else o_t + pv
            return o_t

        def pairs_step(i, carry2):
            pairs = [i * pairs_per_step + j for j in range(pairs_per_step)]
            units = [(c, r, 2 * p + hh) for c, r in zip(blocks, rows) for p in pairs for hh in range(2)]
            scores = [lax.dot_general(ks[h] if latent else ks[h, r, :], qs[h, r, :], nt_dims,
                                      preferred_element_type=f32) for _, r, h in units]
            exps = [jnp.exp2(s - jnp.max(s, 0, keepdims=True)) for s in scores]
            dens = [jnp.sum(e, 0, keepdims=True) for e in exps]
            outs = [values_t(h // 2, c, e.astype(bf16)) / den for (c, _, h), e, den in zip(units, exps, dens)]
            for u in range(0, len(units), 2):
                _, r, h = units[u]
                cat[h // 2, r, :] = jnp.where(low, outs[u], outs[u + 1]).T.astype(bf16)
            return carry2

        return lax.fori_loop(0, C_PAIRS // pairs_per_step, pairs_step, carry)

    lax.fori_loop(0, n // L1_ROWS // blocks_per_step, attend, 0)

    def output(c, carry):
        r = pl.ds(pl.multiple_of(c * PROJ_ROWS, PROJ_ROWS), PROJ_ROWS)
        heads = jnp.concatenate([cat[pair, r, :] for pair in range(C_PAIRS)], axis=1)
        y = jnp.dot(heads, wo_ref[...], preferred_element_type=f32)
        xo_ref[r, :] = x_ref[r, :] + _mod(mod_ref, GATE1) * y
        return carry

    lax.fori_loop(0, n // PROJ_ROWS, output, 0)


def _slot_cols(w, heads, width, lo, hi, lane0):
    k = w.shape[0]
    w3 = w.reshape(k, heads, width)[:, :, lo:hi]
    out = jnp.zeros((k, heads, SLOT), w.dtype).at[:, :, lane0:lane0 + (hi - lo)].set(w3)
    return out.reshape(k, heads * SLOT)


def _l1_mixer(xc, xl, g_norm, mods, p, cache_ckv, cache_kpe):
    w_in = jnp.zeros((D, ODD_IN_PAD), f32).at[:, :C_Q_LORA + C_KV_LORA].set(
        p['w_in'][:, :C_Q_LORA + C_KV_LORA]).at[
        :, C_Q_LORA + C_KV_LORA + C_NOPE:C_Q_LORA + C_KV_LORA + C_QK].set(p['w_in'][:, C_Q_LORA + C_KV_LORA:])
    wuq = _slot_cols(p['w_uq'], C_HEADS, C_QK, 0, C_QK, 0).astype(bf16)
    wuk = _slot_cols(p['w_ukv'], C_HEADS, C_NOPE + C_V, 0, C_NOPE, 0).astype(bf16)
    wuv_t = p['w_ukv'].reshape(C_KV_LORA, C_HEADS, C_NOPE + C_V)[:, :, C_NOPE:].reshape(
        C_KV_LORA, C_HEADS * C_V).T.astype(bf16)
    gq = jnp.zeros((1, SLOT), f32).at[0, :C_QK].set(p['g_q'] * C_SCALE)
    gk = jnp.zeros((1, SLOT), f32).at[0, :C_QK].set(p['g_k'])
    consts = (g_norm.reshape(1, D), w_in.astype(bf16), p['g_qa'].reshape(1, C_Q_LORA), wuq, gq,
              p['g_kva'].reshape(1, C_KV_LORA), wuk, wuv_t, gk, p['w_o'].astype(bf16))
    c_specs = [_const_spec(a.shape) for a in consts]
    row = pl.BlockSpec((BLOCK_ROWS, D), lambda b: (b, 0))
    n_ctx_blocks = T_CTX // BLOCK_ROWS

    def scratch(n_keys):
        return [pltpu.VMEM((BLOCK_ROWS, ODD_IN_PAD), f32), pltpu.VMEM((C_PAIRS, BLOCK_ROWS, LANES), bf16),
                pltpu.VMEM((C_HEADS, BLOCK_ROWS, SLOT), bf16), pltpu.VMEM((C_HEADS, n_keys, SLOT), bf16),
                pltpu.VMEM((C_PAIRS, n_keys // L1_ROWS, LANES, L1_ROWS), bf16),
                pltpu.VMEM((L1_ROWS, C_SLOTS), f32)]

    xo_ctx, ckv_new, kpe_t = pl.pallas_call(
        functools.partial(_l1_kernel, latent=False),
        grid=(n_ctx_blocks,),
        in_specs=[row, c_specs[0], pl.BlockSpec((MOD_ROWS, D), lambda b: (0, 0))] + c_specs[1:],
        out_specs=[row, pl.BlockSpec((BLOCK_ROWS, C_KV_LORA), lambda b: (b, 0)),
                   pl.BlockSpec((BLOCK_ROWS // SEQ, C_ROPE, SEQ), lambda b: (b, 0, 0))],
        out_shape=[jax.ShapeDtypeStruct((T_CTX, D), f32), jax.ShapeDtypeStruct((T_CTX, C_KV_LORA), f32),
                   jax.ShapeDtypeStruct((BATCH, C_ROPE, SEQ), f32)],
        scratch_shapes=scratch(BLOCK_ROWS),
        compiler_params=_cp(("parallel",)), name="l1_mixer_ctx",
    )(xc, consts[0], mods, *consts[1:])

    cos, sin = _rope_tables(DEC_SEQ, C_ROPE, SLOT, C_NOPE)
    def pair_swap(a):
        pairs = a.reshape(a.shape[:-1] + (a.shape[-1] // 2, 2))
        return jnp.stack([pairs[..., 1], pairs[..., 0]], axis=-1).reshape(a.shape)

    w_rope = p['w_uq'].reshape(C_Q_LORA, C_HEADS, C_QK)[:, :, C_NOPE:]
    wuq_swapped = jnp.zeros((C_Q_LORA, C_HEADS, SLOT), f32).at[:, :, C_NOPE:C_QK].set(pair_swap(w_rope)).reshape(
        C_Q_LORA, C_SLOTS).astype(bf16)
    rope = (wuq_swapped, gq * cos, pair_swap(gq) * sin, gk * cos, pair_swap(gk) * sin)
    ckpe = jnp.zeros((DEC_BATCH, PAST, SLOT), f32).at[:, :, C_NOPE:C_QK].set(cache_kpe)
    xo_lat = pl.pallas_call(
        functools.partial(_l1_kernel, latent=True),
        grid=(DEC_BATCH,),
        in_specs=[pl.BlockSpec((BLOCK_ROWS, D), lambda b: (b, 0), pipeline_mode=pl.Buffered(1)), c_specs[0],
                  pl.BlockSpec((MOD_ROWS, D), lambda b: (1 + b, 0))] + c_specs[1:] + [_const_spec(a.shape) for a in rope] + [
                  pl.BlockSpec((None, PAST, C_KV_LORA), lambda b: (b, 0, 0)),
                  pl.BlockSpec((None, PAST, SLOT), lambda b: (b, 0, 0))],
        out_specs=row,
        out_shape=jax.ShapeDtypeStruct((T_LAT, D), f32),
        scratch_shapes=scratch(PAST + BLOCK_ROWS) + [pltpu.VMEM((L1_ROWS, C_SLOTS), f32)],
        compiler_params=_cp(("parallel",)), name="l1_mixer_lat",
    )(xl, consts[0], mods, *consts[1:], *rope, cache_ckv, ckpe)
    return xo_ctx, xo_lat, ckv_new, kpe_t.transpose(0, 2, 1)


ROUTER_ROWS = 40
ROUTE_ROWS = 8
ROUTER_CHUNK = 256
W_LO_ROW0 = 64


def _router_kernel(xc_ref, xl_ref, gn_ref, mod_ref, whi_ref, wst_ref, br_ref, h_ref, route_ref, lg_ref):
    nt = (((1,), (1,)), ((), ()))

    def moe_input(x_ref):
        def chunk(c, carry):
            row0 = pl.multiple_of(c * ROUTER_CHUNK, ROUTER_CHUNK)
            h = (_rms_rows(x_ref[pl.ds(row0, ROUTER_CHUNK), :], gn_ref[...]) * (1.0 + _mod(mod_ref, SCALE2))
                 + _mod(mod_ref, SHIFT2))
            tiles = jnp.swapaxes(jnp.stack([h[:, s * LANES:(s + 1) * LANES] for s in range(SUBS)], axis=0), 0, 1)
            h_ref[pl.ds(row0 * SUBS, ROUTER_CHUNK * SUBS), :] = tiles.reshape(ROUTER_CHUNK * SUBS, LANES)
            h_hi, h_lo = _split_bf16(h)
            by_hi = lax.dot_general(wst_ref[...], h_hi, nt, preferred_element_type=f32)
            by_lo = lax.dot_general(whi_ref[...], h_lo, nt, preferred_element_type=f32)
            lg_ref[c] = (by_hi[0:ROUTER_ROWS, :] + by_lo[0:ROUTER_ROWS, :]
                         + by_hi[W_LO_ROW0:W_LO_ROW0 + ROUTER_ROWS, :])
            return carry

        lax.fori_loop(0, ROW_TILE // ROUTER_CHUNK, chunk, 0, unroll=True)

    is_ctx = pl.program_id(0) < N_CTX_TILES
    pl.when(is_ctx)(lambda: moe_input(xc_ref))
    pl.when(jnp.logical_not(is_ctx))(lambda: moe_input(xl_ref))
    logits = (jnp.concatenate([lg_ref[c] for c in range(ROW_TILE // ROUTER_CHUNK)], axis=1)
              + br_ref[0:ROUTER_ROWS, :])
    row_i = lax.broadcasted_iota(jnp.int32, logits.shape, 0)
    row = row_i.astype(f32)
    big = 1e6
    is_g = (row_i >= N_EXPERTS) & (row_i < N_EXPERTS + N_GROUPS)
    lg = jnp.where(is_g, logits, -jnp.inf)
    mg = jnp.max(lg, 0, keepdims=True)
    gsel = jnp.min(jnp.where(lg == mg, row, big), 0, keepdims=True) - N_EXPERTS
    pg_sel = 1.0 / jnp.sum(jnp.where(is_g, jnp.exp(lg - mg), 0.0), 0, keepdims=True)
    in_grp = (row_i < N_EXPERTS) & ((row_i >> 3).astype(f32) == gsel)
    le = jnp.where(in_grp, logits, -jnp.inf)
    m1 = jnp.max(le, 0, keepdims=True)
    i1 = jnp.min(jnp.where(le == m1, row, big), 0, keepdims=True)
    le2 = jnp.where(row == i1, -jnp.inf, le)
    m2 = jnp.max(le2, 0, keepdims=True)
    i2 = jnp.min(jnp.where(le2 == m2, row, big), 0, keepdims=True)
    e2 = jnp.exp(m2 - m1)
    w1 = pg_sel / (1.0 + e2)
    w2 = pg_sel * e2 / (1.0 + e2)
    sub = lax.broadcasted_iota(jnp.int32, route_ref.shape, 0)
    route_ref[...] = jnp.where(sub == 0, i1, jnp.where(sub == 1, i2, jnp.where(sub == 2, w1,
                                                                                jnp.where(sub == 3, w2, 0.0))))


def _router(xc, xl, g_norm, mods, p):
    wr = jnp.zeros((LANES, D), f32).at[:N_EXPERTS].set(p['w_re'].T).at[
        N_EXPERTS:N_EXPERTS + N_GROUPS].set(p['w_rg'].T)
    w_hi, w_lo = _split_bf16(wr)
    w_stack = w_hi.at[W_LO_ROW0:W_LO_ROW0 + ROUTER_ROWS].set(w_lo[:ROUTER_ROWS])
    br = jnp.zeros((LANES, 1), f32).at[:N_EXPERTS, 0].set(p['b_re']).at[
        N_EXPERTS:N_EXPERTS + N_GROUPS, 0].set(p['b_rg'])
    return pl.pallas_call(
        _router_kernel,
        grid=(T // ROW_TILE,),
        in_specs=_token_specs(D) + [
                  _const_spec((1, D)),
                  pl.BlockSpec((MOD_ROWS, D), lambda i: (_sample_of_tile(i, ROW_TILE), 0)),
                  _const_spec((LANES, D)), _const_spec((LANES, D)), _const_spec((LANES, 1))],
        out_specs=[_token_major_spec(ROW_TILE, lambda i: (i, 0)),
                   pl.BlockSpec((ROUTE_ROWS, ROW_TILE), lambda i: (0, i))],
        out_shape=[jax.ShapeDtypeStruct((T * SUBS, LANES), f32), jax.ShapeDtypeStruct((ROUTE_ROWS, T), f32)],
        scratch_shapes=[pltpu.VMEM((ROW_TILE // ROUTER_CHUNK, ROUTER_ROWS, ROUTER_CHUNK), f32)],
        compiler_params=_cp(("parallel",)), name="router",
    )(xc, xl, g_norm.reshape(1, D), mods, w_hi, w_stack, br)


PLAN_FIRST_TILE, PLAN_TILES = 0, 1


def _plan_kernel(rt_ref, pos_ref, plan_ref, rank):
    n_blk = T // 128
    e_col = lax.broadcasted_iota(jnp.int32, (N_EXPERTS, 128), 0).astype(f32)
    ri = lax.broadcasted_iota(jnp.int32, (128, 128), 0)
    ci = lax.broadcasted_iota(jnp.int32, (128, 128), 1)
    before = jnp.where(ri < ci, 1.0, 0.0).astype(bf16)

    def picks(b):
        cs = slice(b * 128, (b + 1) * 128)
        return rt_ref[0:1, cs] == e_col, rt_ref[1:2, cs] == e_col

    counts = jnp.zeros((N_EXPERTS, 1), f32)
    for b in range(n_blk):
        m0, m1 = picks(b)
        m = jnp.where(m0, 1.0, 0.0) + jnp.where(m1, 1.0, 0.0)
        rank[:, b * 128:(b + 1) * 128] = jnp.dot(m.astype(bf16), before, preferred_element_type=f32) + counts
        counts = counts + jnp.sum(m, axis=1, keepdims=True)

    tiles = jnp.floor((counts + (MOE_TILE - 1.0)) * (1.0 / MOE_TILE))
    er = lax.broadcasted_iota(jnp.int32, (N_EXPERTS, N_EXPERTS), 0)
    ec = lax.broadcasted_iota(jnp.int32, (N_EXPERTS, N_EXPERTS), 1)
    earlier = jnp.where(ec < er, 1.0, 0.0).astype(bf16)
    tile_start = jnp.dot(earlier, jnp.broadcast_to(tiles, (N_EXPERTS, 128)).astype(bf16),
                         preferred_element_type=f32)
    row_start = tile_start * MOE_TILE

    sub = lax.broadcasted_iota(jnp.int32, (8, 128), 0)
    for b in range(n_blk):
        m0, m1 = picks(b)
        base = rank[:, b * 128:(b + 1) * 128] + row_start
        p0 = jnp.sum(jnp.where(m0, base, 0.0), axis=0, keepdims=True)
        p1 = jnp.sum(jnp.where(m1, base, 0.0), axis=0, keepdims=True)
        pos_ref[:, b * 128:(b + 1) * 128] = jnp.where(sub == 0, p0, jnp.where(sub == 1, p1, 0.0)).astype(jnp.int32)

    diag = (lax.broadcasted_iota(jnp.int32, (N_EXPERTS, 128), 0)
            == lax.broadcasted_iota(jnp.int32, (N_EXPERTS, 128), 1))
    first = jnp.sum(jnp.where(diag, tile_start, 0.0), axis=0, keepdims=True)
    count = jnp.sum(jnp.where(diag, tiles, 0.0), axis=0, keepdims=True)
    rows = jnp.where(sub == PLAN_FIRST_TILE, first, jnp.where(sub == PLAN_TILES, count, 0.0))
    plan_ref[...] = rows.astype(jnp.int32)


def _slot_code(t, k):
    return t * SUBS + k * (SUBS // 2)


def _code_offset(code):
    return pl.multiple_of(code & ~(SUBS - 1), SUBS)


def _code_gate_index(code):
    return code >> 2


PAD_CODE = T * SUBS


def _invert_slots(pos_ref, first_ref, count_ref, code_ref):
    def pad_tile(tile, carry):
        for u in range(MOE_TILE):
            code_ref[tile * MOE_TILE + u] = PAD_CODE
        return carry

    def pad_last_tile(e, carry):
        return pad_tile(jnp.maximum(first_ref[e] + count_ref[e] - 1, 0), carry)
    lax.fori_loop(0, N_EXPERTS, pad_last_tile, 0)
    lax.fori_loop(first_ref[N_EXPERTS - 1] + count_ref[N_EXPERTS - 1], MOE_TILES, pad_tile, 0)

    group = 16
    for k in range(2):
        def place(i, carry):
            t0 = i * group
            slots = [pos_ref[k * T + t0 + u] for u in range(group)]
            for u, s in enumerate(slots):
                code_ref[s] = _slot_code(t0 + u, k)
            return carry
        lax.fori_loop(0, T // group, place, 0)


def _route_plan(route_t):
    pos, plan = pl.pallas_call(
        _plan_kernel,
        out_shape=[jax.ShapeDtypeStruct((8, T), jnp.int32), jax.ShapeDtypeStruct((8, LANES), jnp.int32)],
        scratch_shapes=[pltpu.VMEM((N_EXPERTS, T), f32)],
        compiler_params=_cp(None), name="route_plan",
    )(route_t)
    gates = jnp.pad(route_t[2:4].T.reshape(2 * T), (0, 8))
    return plan[PLAN_FIRST_TILE, :N_EXPERTS], plan[PLAN_TILES, :N_EXPERTS], pos[0:2].reshape(2 * T), gates


def _tile_index(i):
    return jnp.minimum(i, MOE_TILES - 1)


TM_ROWS = T * SUBS
SCATTER_GROUP = 8
STAGES = 4
LAST_EXPERT = N_EXPERTS - 1


def _tile_rows(g):
    return pl.ds(pl.multiple_of(g * MOE_TILE, MOE_TILE), MOE_TILE)


def _expert_tile_pairs(first, count, tile_step):
    def pair(pp, carry):
        for parity in range(2):
            g = 2 * pp + parity
            pl.when((g >= first) & (g < first + count))(functools.partial(tile_step, g, parity))
        return carry
    lax.fori_loop(first // 2, (first + count + 1) // 2, pair, 0)


def _gather_tile(code_ref, tile, xs, gbuf):
    base = tile * MOE_TILE
    for r in range(MOE_TILE):
        gbuf[r * SUBS:(r + 1) * SUBS, :] = xs[pl.ds(_code_offset(code_ref[base + r]), SUBS), :]


def _moe_up_kernel(first_ref, count_ref, pos_ref, h_hbm, w1_ref, w3_ref, hh_hbm, code_ref,
                   xs, gbuf_a, gbuf_b, w13, obuf, sem_x, sem_o):
    e = pl.program_id(0)
    first, count = first_ref[e], count_ref[e]
    gbufs = (gbuf_a, gbuf_b)

    def out_copy(slot, g):
        return pltpu.make_async_copy(obuf.at[slot], hh_hbm.at[_tile_rows(g), :], sem_o.at[slot])

    @pl.when(e == 0)
    def _():
        cp = pltpu.make_async_copy(h_hbm, xs.at[pl.ds(0, TM_ROWS), :], sem_x)
        cp.start()
        _invert_slots(pos_ref, first_ref, count_ref, code_ref)
        xs[TM_ROWS:TM_ROWS + SUBS, :] = jnp.zeros((SUBS, LANES), f32)
        cp.wait()
        _gather_tile(code_ref, 0, xs, gbuf_a)

    @pl.when(count > 0)
    def _():
        w13[:, :D_EXPERT] = w1_ref[0].astype(bf16)
        w13[:, D_EXPERT:] = w3_ref[0].astype(bf16)

    def tile_step(g, parity):
        _gather_tile(code_ref, _tile_index(g + 1), xs, gbufs[1 - parity])
        x3 = jnp.swapaxes(gbufs[parity][...].reshape(MOE_TILE, SUBS, LANES), 0, 1)
        x = jnp.concatenate([x3[s] for s in range(SUBS)], axis=1).astype(bf16)
        h13 = jnp.dot(x, w13[...], preferred_element_type=f32)
        hh = (_silu(h13[:, :D_EXPERT]) * h13[:, D_EXPERT:]).astype(bf16)

        slot = g % STAGES

        @pl.when(g >= STAGES)
        def _():
            out_copy(slot, g).wait()
        obuf[slot] = hh
        out_copy(slot, g).start()

    _expert_tile_pairs(first, count, tile_step)

    @pl.when(e == LAST_EXPERT)
    def _():
        n_used = first + count
        for slot in range(STAGES):
            pl.when(n_used > slot)(lambda slot=slot: out_copy(slot, 0).wait())
        obuf[0] = jnp.zeros((MOE_TILE, D_EXPERT), bf16)

        def zero_tile(g, carry):
            cp = out_copy(0, g)
            cp.start()
            cp.wait()
            return carry
        lax.fori_loop(n_used, MOE_TILES, zero_tile, 0)


def _scatter_tile(code_ref, gate_ref, tile, ybuf, acc):
    base = tile * MOE_TILE
    for g0 in range(0, MOE_TILE, SCATTER_GROUP):
        rows = range(g0, g0 + SCATTER_GROUP)
        codes = [code_ref[base + r] for r in rows]
        new = [acc[pl.ds(_code_offset(c), SUBS), :]
               + gate_ref[_code_gate_index(c)] * ybuf[r * SUBS:(r + 1) * SUBS, :]
               for r, c in zip(rows, codes)]
        for c, v in zip(codes, new):
            acc[pl.ds(_code_offset(c), SUBS), :] = v


RES_ROWS = 256


def _residual_out(x_hbm, y_hbm, tok0, sample_of_chunk, mod_ref, acc, rin, rout, sem_r, sem_w):
    n_chunks = x_hbm.shape[0] // RES_ROWS

    def rows(c):
        return pl.ds(pl.multiple_of(c * RES_ROWS, RES_ROWS), RES_ROWS)

    def in_copy(slot, c):
        return pltpu.make_async_copy(x_hbm.at[rows(c), :], rin.at[slot], sem_r.at[slot])

    def out_copy(slot, c):
        return pltpu.make_async_copy(rout.at[slot], y_hbm.at[rows(c), :], sem_w.at[slot])

    for c in range(STAGES - 1):
        in_copy(c, c).start()

    def ring(cc, carry):
        for slot in range(STAGES):
            c = STAGES * cc + slot
            in_copy(slot, c).wait()
            ahead = c + STAGES - 1

            @pl.when(ahead < n_chunks)
            def _():
                in_copy((slot + STAGES - 1) % STAGES, ahead).start()

            @pl.when(c >= STAGES)
            def _():
                out_copy(slot, c).wait()
            delta = _load_token_major(acc, RES_ROWS, tok0 + c * RES_ROWS)
            gate = mod_ref[pl.ds(sample_of_chunk(c) * MOD_ROWS + GATE2, 1), :]
            rout[slot] = rin[slot] + gate * delta
            out_copy(slot, c).start()
        return carry

    lax.fori_loop(0, n_chunks // STAGES, ring, 0)
    for slot in range(STAGES):
        out_copy(slot, 0).wait()


def _moe_down_kernel(first_ref, count_ref, code_ref, gate_ref, hh_hbm, w2_ref, xc_hbm, xl_hbm, mod_ref,
                     yc_hbm, yl_hbm, acc, ybuf_a, ybuf_b, w2b, ibuf, rin, rout, sem_i, sem_r, sem_w):
    e = pl.program_id(0)
    first, count = first_ref[e], count_ref[e]
    n_used = first_ref[LAST_EXPERT] + count_ref[LAST_EXPERT]
    ybufs = (ybuf_a, ybuf_b)

    def in_copy(slot, g):
        return pltpu.make_async_copy(hh_hbm.at[_tile_rows(g), :], ibuf.at[slot], sem_i.at[slot])

    @pl.when(e == 0)
    def _():
        for g in range(STAGES - 1):
            in_copy(g, g).start()

        def zero(c, carry):
            acc[pl.ds(pl.multiple_of(c * 1024, 1024), 1024), :] = jnp.zeros((1024, LANES), f32)
            return carry
        lax.fori_loop(0, TM_ROWS // 1024, zero, 0)
        acc[TM_ROWS:TM_ROWS + SUBS, :] = jnp.zeros((SUBS, LANES), f32)
        ybuf_b[...] = jnp.zeros_like(ybuf_b)

    @pl.when(count > 0)
    def _():
        w2b[...] = w2_ref[0].astype(bf16)

    def tile_step(g, parity):
        slot = g % STAGES
        in_copy(slot, g).wait()
        ahead = g + STAGES - 1

        @pl.when(ahead < n_used)
        def _():
            in_copy(ahead % STAGES, ahead).start()
        _store_token_major(ybufs[parity], jnp.dot(ibuf[slot], w2b[...], preferred_element_type=f32))
        _scatter_tile(code_ref, gate_ref, jnp.maximum(g - 1, 0), ybufs[1 - parity], acc)

    _expert_tile_pairs(first, count, tile_step)

    @pl.when(e == LAST_EXPERT)
    def _():
        for parity in range(2):
            pl.when((n_used > 0) & ((n_used - 1) % 2 == parity))(
                functools.partial(_scatter_tile, code_ref, gate_ref, n_used - 1, ybufs[parity], acc))
        _residual_out(xc_hbm, yc_hbm, 0, lambda c: 0, mod_ref, acc, rin, rout, sem_r, sem_w)
        _residual_out(xl_hbm, yl_hbm, T_CTX, lambda c: 1 + c // (DEC_SEQ // RES_ROWS),
                      mod_ref, acc, rin, rout, sem_r, sem_w)


def _moe(h_tm, route_t, p, xc, xl, mods):
    first_tile, n_tiles, slots, gates = _route_plan(route_t)
    tile_rows = pltpu.VMEM((MOE_TILE * SUBS, LANES), f32)
    staging = pltpu.VMEM((STAGES, MOE_TILE, D_EXPERT), bf16)
    res_rows = pltpu.VMEM((STAGES, RES_ROWS, D), f32)
    hbm = pl.BlockSpec(memory_space=pl.ANY)
    hh, codes = pl.pallas_call(
        _moe_up_kernel,
        grid_spec=pltpu.PrefetchScalarGridSpec(
            num_scalar_prefetch=3, grid=(N_EXPERTS,),
            in_specs=[hbm,
                      pl.BlockSpec((1, D, D_EXPERT), lambda e, f, n, s: (e, 0, 0)),
                      pl.BlockSpec((1, D, D_EXPERT), lambda e, f, n, s: (e, 0, 0))],
            out_specs=[hbm, pl.BlockSpec(memory_space=pltpu.SMEM)],
            scratch_shapes=[pltpu.VMEM((TM_ROWS + SUBS, LANES), f32), tile_rows, tile_rows,
                            pltpu.VMEM((D, 2 * D_EXPERT), bf16), staging,
                            pltpu.SemaphoreType.DMA(()), pltpu.SemaphoreType.DMA((STAGES,))]),
        out_shape=[jax.ShapeDtypeStruct((MOE_ROWS, D_EXPERT), bf16),
                   jax.ShapeDtypeStruct((MOE_ROWS,), jnp.int32)],
        compiler_params=_cp(("arbitrary",)), name="moe_up",
    )(first_tile, n_tiles, slots, h_tm, p['w1'], p['w3'])
    return pl.pallas_call(
        _moe_down_kernel,
        grid_spec=pltpu.PrefetchScalarGridSpec(
            num_scalar_prefetch=4, grid=(N_EXPERTS,),
            in_specs=[hbm, pl.BlockSpec((1, D_EXPERT, D), lambda e, f, n, c, g: (e, 0, 0)), hbm, hbm,
                      pl.BlockSpec((N_SAMPLES * MOD_ROWS, D), lambda e, f, n, c, g: (0, 0),
                                   pipeline_mode=pl.Buffered(1))],
            out_specs=[hbm, hbm],
            scratch_shapes=[pltpu.VMEM((TM_ROWS + SUBS, LANES), f32), tile_rows, tile_rows,
                            pltpu.VMEM((D_EXPERT, D), bf16), staging, res_rows, res_rows,
                            pltpu.SemaphoreType.DMA((STAGES,)), pltpu.SemaphoreType.DMA((STAGES,)),
                            pltpu.SemaphoreType.DMA((STAGES,))]),
        out_shape=[jax.ShapeDtypeStruct((T_CTX, D), f32), jax.ShapeDtypeStruct((T_LAT, D), f32)],
        compiler_params=_cp(("arbitrary",)), name="moe_down",
    )(first_tile, n_tiles, codes, gates, hh, p['w2'], xc, xl, mods)


def kernel(x_prompt, x_sample, cache_l0_k, cache_l0_v, cache_l1_ckv, cache_l1_kpe, c, c_ctx, l0_g_norm1, l0_g_norm2, l0_w_ada, l0_b_ada, l0_w_in, l0_g_vnorm, l0_w_s, l0_b_s, l0_g_q, l0_g_k, l0_sink, l0_w_o, l0_w_rg, l0_b_rg, l0_w_re, l0_b_re, l0_w1, l0_w3, l0_w2, l1_g_norm1, l1_g_norm2, l1_w_ada, l1_b_ada, l1_w_in, l1_g_qa, l1_w_uq, l1_g_kva, l1_w_ukv, l1_g_q, l1_g_k, l1_w_o, l1_w_rg, l1_b_rg, l1_w_re, l1_b_re, l1_w1, l1_w3, l1_w2):
    p0 = dict(w_in=l0_w_in, g_vnorm=l0_g_vnorm, w_s=l0_w_s, b_s=l0_b_s, g_q=l0_g_q, g_k=l0_g_k, sink=l0_sink,
              w_o=l0_w_o, w_rg=l0_w_rg, b_rg=l0_b_rg, w_re=l0_w_re, b_re=l0_b_re, w1=l0_w1, w3=l0_w3, w2=l0_w2)
    p1 = dict(w_in=l1_w_in, g_qa=l1_g_qa, w_uq=l1_w_uq, g_kva=l1_g_kva, w_ukv=l1_w_ukv, g_q=l1_g_q, g_k=l1_g_k,
              w_o=l1_w_o, w_rg=l1_w_rg, b_rg=l1_b_rg, w_re=l1_w_re, b_re=l1_b_re, w1=l1_w1, w3=l1_w3, w2=l1_w2)

    cond8 = jnp.zeros((MOD_ROWS, D), f32).at[0].set(c_ctx).at[1:1 + DEC_BATCH].set(c)
    mods0 = _mod_rows(_adaln(cond8, l0_w_ada, l0_b_ada))
    mods1 = _mod_rows(_adaln(cond8, l1_w_ada, l1_b_ada))

    xc0 = x_prompt.reshape(T_CTX, D)
    xl0 = x_sample.reshape(T_LAT, D)

    xc0m, xl0m, k_new, v_new = _l0_mixer(xc0, xl0, l0_g_norm1, mods0, p0, cache_l0_k, cache_l0_v)
    h0, route0 = _router(xc0m, xl0m, l0_g_norm2, mods0, p0)
    xc1, xl1 = _moe(h0, route0, p0, xc0m, xl0m, mods0)

    xc1m, xl1m, ckv_new, kpe_new = _l1_mixer(xc1, xl1, l1_g_norm1, mods1, p1, cache_l1_ckv, cache_l1_kpe)
    h1, route1 = _router(xc1m, xl1m, l1_g_norm2, mods1, p1)
    y_prompt, y_sample = _moe(h1, route1, p1, xc1m, xl1m, mods1)
    return (y_prompt.reshape(BATCH, SEQ, D), y_sample.reshape(DEC_BATCH, DEC_SEQ, D), k_new, v_new,
            ckv_new.reshape(BATCH, SEQ, C_KV_LORA), kpe_new.reshape(BATCH, SEQ, C_ROPE))
```

```python
import functools

import jax
import jax.numpy as jnp
import numpy as np
from jax import lax
from jax.experimental import pallas as pl
from jax.experimental.pallas import tpu as pltpu

f32 = jnp.float32
bf16 = jnp.bfloat16

D = 1024
BATCH, SEQ = 32, 256
DEC_BATCH, DEC_SEQ = 2, 1024
PAST = 512
T_CTX = BATCH * SEQ
T_LAT = DEC_BATCH * DEC_SEQ
T = T_CTX + T_LAT
GRID_W = 64
CHUNK = 128
WINDOW = 128
ROPE_THETA = 10000.0
EPS = 1e-6
NEG_INF = -1e30
LANES = 128
SUBS = D // LANES

A_WIDTH = 512
A_GROUPS = 4
B_HEADS, B_KV, B_GROUP, B_HD = 8, 2, 4, 64
LOG2E = 1.4426950408889634
B_SCALE = B_HD ** -0.5 * LOG2E

C_HEADS, C_Q_LORA, C_KV_LORA, C_NOPE, C_ROPE, C_V = 16, 384, 256, 64, 32, 64
C_QK = C_NOPE + C_ROPE
C_SCALE = C_QK ** -0.5 * LOG2E
ODD_IN_PAD = 768
SLOT = 128

N_GROUPS, N_EXPERTS, D_EXPERT = 4, 32, 256

N_SAMPLES = 1 + DEC_BATCH
MOD_ROWS = 8
SHIFT1, SCALE1, GATE1, SHIFT2, SCALE2, GATE2 = range(6)

ROW_TILE = 1024
BLOCK_ROWS = 1024
PROJ_ROWS = 512
ADALN_COLS = 1536
MOE_TILE = 256
MOE_ROWS = 2 * T + N_EXPERTS * MOE_TILE
MOE_TILES = MOE_ROWS // MOE_TILE
VMEM_CAP = 56 * 1024 * 1024


def _cp(sem, vmem=VMEM_CAP):
    return pltpu.CompilerParams(dimension_semantics=sem, vmem_limit_bytes=vmem)


def _const_spec(shape):
    nd = len(shape)
    return pl.BlockSpec(shape, lambda *_: (0,) * nd, pipeline_mode=pl.Buffered(1))


def _sample_of_tile(i, tile):
    n_ctx = T_CTX // tile
    per_lat = DEC_SEQ // tile
    return jnp.where(i < n_ctx, 0, 1 + (i - n_ctx) // per_lat)


def _mod(mod_ref, row):
    return mod_ref[row:row + 1, :]


def _silu(x):
    return x * jax.nn.sigmoid(x)


def _rms_rows(x, g):
    return x * lax.rsqrt(jnp.mean(x * x, -1, keepdims=True) + EPS) * g


def _swap_pairs(x):
    lane = lax.broadcasted_iota(jnp.int32, x.shape, x.ndim - 1)
    nxt = pltpu.roll(x, x.shape[-1] - 1, x.ndim - 1)
    prv = pltpu.roll(x, 1, x.ndim - 1)
    return jnp.where((lane & 1) == 0, nxt, prv)


def _split_bf16(x):
    hi = x.astype(bf16)
    return hi, (x - hi.astype(f32)).astype(bf16)


def _adaln_kernel(c_ref, w_ref, b_ref, o_ref):
    s_hi, s_lo = _split_bf16(_silu(c_ref[...]))
    w_hi, w_lo = _split_bf16(w_ref[...])
    o_ref[...] = (jnp.dot(s_hi, w_hi, preferred_element_type=f32) + jnp.dot(s_lo, w_hi, preferred_element_type=f32)
                  + jnp.dot(s_hi, w_lo, preferred_element_type=f32) + b_ref[...])


def _adaln(cond8, w, b):
    n = w.shape[1]
    return pl.pallas_call(
        _adaln_kernel,
        grid=(n // ADALN_COLS,),
        in_specs=[_const_spec((MOD_ROWS, D)), pl.BlockSpec((D, ADALN_COLS), lambda j: (0, j)),
                  pl.BlockSpec((1, ADALN_COLS), lambda j: (0, j))],
        out_specs=pl.BlockSpec((MOD_ROWS, ADALN_COLS), lambda j: (0, j)),
        out_shape=jax.ShapeDtypeStruct((MOD_ROWS, n), f32),
        compiler_params=_cp(("arbitrary",)),
        name="adaln",
    )(cond8, w, b.reshape(1, n))


def _mod_rows(m8):
    m = m8[:N_SAMPLES].reshape(N_SAMPLES, 6, D)
    return jnp.pad(m, ((0, 0), (0, MOD_ROWS - 6), (0, 0))).reshape(N_SAMPLES * MOD_ROWS, D)


N_CTX_TILES = T_CTX // ROW_TILE


def _token_specs(width):
    return [pl.BlockSpec((ROW_TILE, width), lambda i: (jnp.minimum(i, N_CTX_TILES - 1), 0)),
            pl.BlockSpec((ROW_TILE, width), lambda i: (jnp.maximum(i - N_CTX_TILES, 0), 0))]


def _store_token_major(ref, x, row0=0):
    n = x.shape[0]
    for s in range(SUBS):
        ref[pl.ds(row0 * SUBS + s, n, stride=SUBS), :] = x[:, s * LANES:(s + 1) * LANES]


def _load_token_major(ref, n, row0=0):
    return jnp.concatenate([ref[pl.ds(row0 * SUBS + s, n, stride=SUBS), :] for s in range(SUBS)], axis=1)


def _token_major_spec(rows, index_map):
    return pl.BlockSpec((rows * SUBS, LANES), index_map)


def _rope_tables(n, rot_dim, lanes, lane0, copies=1):
    rows_count = n // GRID_W
    rows = np.repeat(np.arange(rows_count), GRID_W).astype(np.float64)
    cols = np.tile(np.arange(GRID_W), rows_count).astype(np.float64)
    d_axis = rot_dim // 2
    inv = ROPE_THETA ** (-np.arange(0, d_axis, 2, dtype=np.float64) / d_axis)
    ang = np.concatenate([rows[:, None] * inv, cols[:, None] * inv], -1)
    c = np.ones((n, lanes), np.float32)
    s = np.zeros((n, lanes), np.float32)
    for j in range(copies):
        lo = lane0 + j * rot_dim
        c[:, lo:lo + rot_dim] = np.repeat(np.cos(ang), 2, axis=1)
        s[:, lo:lo + rot_dim] = np.repeat(np.sin(ang), 2, axis=1) * np.tile(np.array([-1.0, 1.0]), rot_dim // 2)
    return jnp.asarray(c), jnp.asarray(s)


L0_Q0 = 2 * A_WIDTH
L0_K0 = L0_Q0 + B_GROUP * LANES
L0_V0 = L0_K0 + B_KV * B_HD
L0_IN = L0_V0 + B_KV * B_HD


def _l0_kernel(*refs, latent):
    if latent:
        (sink_ref, x_ref, gn_ref, mod_ref, win_ref, gvn_ref, ws_ref, bsb_ref, gq_ref, gk_ref, wo_ref,
         cos_ref, sin_ref, kc_ref, vc_ref, xo_ref, zs, cat, qs, ks, vt, kcb, vct) = refs
        key_off = WINDOW
    else:
        (sink_ref, x_ref, gn_ref, mod_ref, win_ref, gvn_ref, ws_ref, bsb_ref, gq_ref, gk_ref, wo_ref,
         xo_ref, ko_ref, vo_ref, zs, cat, qs, ks, vt, kf, vf) = refs
        key_off = 0
    n = BLOCK_ROWS
    n_chunks = n // CHUNK
    low = lax.broadcasted_iota(jnp.int32, (CHUNK, LANES), 1) < B_HD

    if latent:
        zpad = jnp.zeros((WINDOW, LANES), bf16)
        for c0 in (0, 1 + n_chunks):
            ks[c0 * CHUNK:(c0 + 1) * CHUNK, :] = zpad
            vt[c0] = zpad
        kcb[...] = kc_ref[...].astype(bf16)
        for i in range(PAST // CHUNK):
            vct[i] = vc_ref[i * CHUNK:(i + 1) * CHUNK, :].T.astype(bf16)

    def project(c, carry):
        r = pl.ds(pl.multiple_of(c * PROJ_ROWS, PROJ_ROWS), PROJ_ROWS)
        h = _rms_rows(x_ref[r, :], gn_ref[...]) * (1.0 + _mod(mod_ref, SCALE1)) + _mod(mod_ref, SHIFT1)
        zs[r, :] = jnp.dot(h.astype(bf16), win_ref[...], preferred_element_type=f32)
        return carry

    lax.fori_loop(0, n // PROJ_ROWS, project, 0)

    def prepare(c, carry):
        r = pl.ds(pl.multiple_of(c * CHUNK, CHUNK), CHUNK)
        u = jax.nn.gelu(zs[r, 0:A_WIDTH])
        v = jax.nn.gelu(zs[r, A_WIDTH:2 * A_WIDTH])
        mu = jnp.mean(v, -1, keepdims=True)
        var = jnp.mean(jnp.square(v - mu), -1, keepdims=True)
        vn = ((v - mu) * lax.rsqrt(var + EPS) * gvn_ref[...]).astype(bf16)
        for g in range(A_GROUPS):
            cs = slice(g * CHUNK, (g + 1) * CHUNK)
            mixed = jnp.dot(ws_ref[g], vn[:, cs], preferred_element_type=f32) + bsb_ref[g]
            cat[r, cs] = (u[:, cs] * mixed).astype(bf16)
        if latent:
            cs_, sn_ = cos_ref[r, :], sin_ref[r, :]
        def half_norm(v, gain):
            sq = v * v
            s0 = jnp.sum(jnp.where(low, sq, 0.0), -1, keepdims=True)
            s1 = jnp.sum(jnp.where(low, 0.0, sq), -1, keepdims=True)
            return v * lax.rsqrt(jnp.where(low, s0, s1) * (1.0 / B_HD) + EPS) * gain

        for j in range(B_GROUP):
            js = slice(j * LANES, (j + 1) * LANES)
            qj = half_norm(zs[r, L0_Q0 + j * LANES:L0_Q0 + (j + 1) * LANES], gq_ref[...])
            if latent:
                qj = qj * cs_ + _swap_pairs(qj) * sn_
            qs[r, js] = qj.astype(bf16)
        k = half_norm(zs[r, L0_K0:L0_K0 + LANES], gk_ref[...])
        vv = zs[r, L0_V0:L0_V0 + LANES]
        if latent:
            k = k * cs_ + _swap_pairs(k) * sn_
        else:
            kf[r, :] = k
            vf[r, :] = vv
        kr = pl.ds(pl.multiple_of(c * CHUNK + key_off, CHUNK), CHUNK)
        ks[kr, :] = k.astype(bf16)
        vt[c + key_off // CHUNK] = vv.T.astype(bf16)
        return carry

    lax.fori_loop(0, n_chunks // 2, lambda i, carry: prepare(2 * i + 1, prepare(2 * i, carry)), 0)

    def attend(jobs):
        scored = []
        for r, rows, key_sets in jobs:
            low_q = lax.broadcasted_iota(jnp.int32, (rows, LANES), 1) < B_HD
            slots = [qs[r, j * LANES:(j + 1) * LANES] for j in range(B_GROUP)]
            zero = jnp.zeros((rows, LANES), bf16)
            q = jnp.concatenate([jnp.where(low_q, qj, zero) for qj in slots]
                                + [jnp.where(low_q, zero, qj) for qj in slots], axis=0)
            sk = jnp.concatenate([jnp.full((1, rows), sink_ref[h] * LOG2E, f32) for h in range(B_HEADS)], axis=1)
            scores = []
            m = sk
            for k, _, keep in key_sets:
                s = lax.dot_general(k, q, (((1,), (1,)), ((), ())), preferred_element_type=f32)
                if keep is not None:
                    s = jnp.where(keep, s, NEG_INF)
                scores.append(s)
                m = jnp.maximum(m, jnp.max(s, 0, keepdims=True))
            scored.append((sk, scores, m))
        outs = []
        for (r, rows, key_sets), (sk, scores, m) in zip(jobs, scored):
            den = jnp.exp2(sk - m)
            ot = None
            for s, (_, vts, _) in zip(scores, key_sets):
                e = jnp.exp2(s - m)
                den = den + jnp.sum(e, 0, keepdims=True)
                eb = e.astype(bf16)
                for i, v_t in enumerate(vts):
                    pv = jnp.dot(v_t, eb[i * CHUNK:(i + 1) * CHUNK, :], preferred_element_type=f32)
                    ot = pv if ot is None else ot + pv
            outs.append(ot * (1.0 / den))
        for (r, rows, _), ot in zip(jobs, outs):
            for pair in range(B_HEADS // 2):
                f0 = (2 * pair // B_GROUP) * B_HD
                pair_t = jnp.concatenate([ot[f0:f0 + B_HD, 2 * pair * rows:(2 * pair + 1) * rows],
                                          ot[f0:f0 + B_HD, (2 * pair + 1) * rows:(2 * pair + 2) * rows]], axis=0)
                cat[r, A_WIDTH + pair * LANES:A_WIDTH + (pair + 1) * LANES] = pair_t.T.astype(bf16)

    jobs_per_step = 2 if latent else 4
    if latent:
        span = CHUNK + 2 * WINDOW

        def block_job(c):
            start = pl.multiple_of(c * CHUNK, CHUNK)
            kr = pl.ds(start, span)
            kj = lax.broadcasted_iota(jnp.int32, (span, B_HEADS * CHUNK), 0)
            qi = lax.broadcasted_iota(jnp.int32, (span, B_HEADS * CHUNK), 1) & (CHUNK - 1)
            kpos = start - WINDOW + kj
            keep = (jnp.abs(kj - WINDOW - qi) <= WINDOW) & (kpos >= 0) & (kpos < n)
            return (pl.ds(start, CHUNK), CHUNK,
                    [(ks[kr, :], [vt[c + i] for i in range(span // CHUNK)], keep),
                     (kcb[...], [vct[i] for i in range(PAST // CHUNK)], None)])

        def attend_blocks(i, carry):
            attend([block_job(i * jobs_per_step + j) for j in range(jobs_per_step)])
            return carry

        lax.fori_loop(0, n_chunks // jobs_per_step, attend_blocks, 0)
    else:
        def seq_job(sq):
            r = pl.ds(pl.multiple_of(sq * SEQ, SEQ), SEQ)
            return r, SEQ, [(ks[r, :], [vt[sq * (SEQ // CHUNK) + i] for i in range(SEQ // CHUNK)], None)]

        def attend_seqs(i, carry):
            seqs = [i * jobs_per_step + j for j in range(jobs_per_step)]
            attend([seq_job(sq) for sq in seqs])
            for sq in seqs:
                r = pl.ds(pl.multiple_of(sq * SEQ, SEQ), SEQ)
                ko_ref[sq] = kf[r, :].T
                vo_ref[sq] = vf[r, :].T
            return carry

        lax.fori_loop(0, n // SEQ // jobs_per_step, attend_seqs, 0)

    def output(c, carry):
        r = pl.ds(pl.multiple_of(c * PROJ_ROWS, PROJ_ROWS), PROJ_ROWS)
        y = jnp.dot(cat[r, :], wo_ref[...], preferred_element_type=f32)
        xo_ref[r, :] = x_ref[r, :] + _mod(mod_ref, GATE1) * y
        return carry

    lax.fori_loop(0, n // PROJ_ROWS, output, 0)


def _l0_mixer(xc, xl, g_norm, mods, p, cache_k, cache_v):
    w = p['w_in']
    q_slots = w[:, L0_Q0:L0_Q0 + B_HEADS * B_HD].reshape(D, B_KV, B_GROUP, B_HD).transpose(0, 2, 1, 3).reshape(
        D, B_GROUP * LANES)
    win = jnp.concatenate([w[:, :L0_Q0], q_slots, w[:, L0_Q0 + B_HEADS * B_HD:]], axis=1).astype(bf16)
    gq = jnp.tile(p['g_q'], B_KV).reshape(1, LANES) * B_SCALE
    gk = jnp.tile(p['g_k'], B_KV).reshape(1, LANES)
    gvn = p['g_vnorm'].reshape(1, A_WIDTH)
    ws = p['w_s'].astype(bf16)
    bsb = jnp.broadcast_to(p['b_s'][:, :, None], (A_GROUPS, CHUNK, CHUNK))
    wo = p['w_o'].astype(bf16)
    weights = (g_norm.reshape(1, D),)
    consts = (win, gvn, ws, bsb, gq, gk, wo)
    c_specs = [_const_spec(a.shape) for a in consts]
    smem = pl.BlockSpec(memory_space=pltpu.SMEM)
    row = pl.BlockSpec((BLOCK_ROWS, D), lambda b: (b, 0))
    kv = pl.BlockSpec((BLOCK_ROWS // SEQ, LANES, SEQ), lambda b: (b, 0, 0))

    def scratch(pad):
        return [pltpu.VMEM((BLOCK_ROWS, L0_IN), f32), pltpu.VMEM((BLOCK_ROWS, D), bf16),
                pltpu.VMEM((BLOCK_ROWS, B_GROUP * LANES), bf16), pltpu.VMEM((BLOCK_ROWS + pad, LANES), bf16),
                pltpu.VMEM(((BLOCK_ROWS + pad) // CHUNK, LANES, CHUNK), bf16)]

    kv_shape = jax.ShapeDtypeStruct((BATCH, LANES, SEQ), f32)
    state = pltpu.VMEM((BLOCK_ROWS, LANES), f32)
    xo_ctx, k_t, v_t = pl.pallas_call(
        functools.partial(_l0_kernel, latent=False),
        grid=(T_CTX // BLOCK_ROWS,),
        in_specs=[smem, row, _const_spec((1, D)), pl.BlockSpec((MOD_ROWS, D), lambda b: (0, 0))] + c_specs,
        out_specs=[row, kv, kv],
        out_shape=[jax.ShapeDtypeStruct((T_CTX, D), f32), kv_shape, kv_shape],
        scratch_shapes=scratch(0) + [state, state],
        compiler_params=_cp(("parallel",)), name="l0_mixer_ctx",
    )(p['sink'], xc, *weights, mods, *consts)
    k_new = k_t.reshape(BATCH, B_KV, B_HD, SEQ).transpose(0, 3, 1, 2)
    v_new = v_t.reshape(BATCH, B_KV, B_HD, SEQ).transpose(0, 3, 1, 2)

    cos, sin = _rope_tables(DEC_SEQ, B_HD, LANES, 0, copies=LANES // B_HD)
    cache = pl.BlockSpec((None, PAST, LANES), lambda b: (b, 0, 0))
    past = [pltpu.VMEM((PAST, LANES), bf16), pltpu.VMEM((PAST // CHUNK, LANES, CHUNK), bf16)]
    xo_lat = pl.pallas_call(
        functools.partial(_l0_kernel, latent=True),
        grid=(DEC_BATCH,),
        in_specs=[smem, row, _const_spec((1, D)), pl.BlockSpec((MOD_ROWS, D), lambda b: (1 + b, 0))] + c_specs + [
                  _const_spec(cos.shape), _const_spec(sin.shape), cache, cache],
        out_specs=row,
        out_shape=jax.ShapeDtypeStruct((T_LAT, D), f32),
        scratch_shapes=scratch(2 * WINDOW) + past,
        compiler_params=_cp(("parallel",)), name="l0_mixer_lat",
    )(p['sink'], xl, *weights, mods, *consts, cos, sin,
      cache_k.reshape(DEC_BATCH, PAST, LANES), cache_v.reshape(DEC_BATCH, PAST, LANES))
    return xo_ctx, xo_lat, k_new, v_new


C_SLOTS = C_HEADS * SLOT
C_PAIRS = C_HEADS // 2
L1_ROWS = 256


def _l1_kernel(*refs, latent):
    if latent:
        (x_ref, gn_ref, mod_ref, win_ref, gqa_ref, wuq_ref, gq_ref, gkva_ref, wuk_ref, wuvt_ref, gk_ref,
         wo_ref, wuqs_ref, qcos_ref, qsin_ref, kcos_ref, ksin_ref, cckv_ref, ckpe_ref, xo_ref,
         zs, cat, qs, ks, vt, wide, wide2) = refs
        n_ctx = PAST
    else:
        (x_ref, gn_ref, mod_ref, win_ref, gqa_ref, wuq_ref, gq_ref, gkva_ref, wuk_ref, wuvt_ref, gk_ref,
         wo_ref, xo_ref, ckvo_ref, kpeo_ref, zs, cat, qs, ks, vt, wide) = refs
        n_ctx = 0
    n = BLOCK_ROWS
    nt_dims = (((1,), (1,)), ((), ()))

    def inv_rms(v):
        return lax.rsqrt(jnp.sum(v * v, -1, keepdims=True) * (1.0 / C_QK) + EPS)

    def expand_keys(ckv_n, kslot, kb, rope_rows):
        cb = ckv_n.astype(bf16)
        key_rows = pl.ds(pl.multiple_of(kb * L1_ROWS, L1_ROWS), L1_ROWS)
        wide[...] = jnp.dot(cb, wuk_ref[...], preferred_element_type=f32)
        if rope_rows is not None:
            kcos = kcos_ref[rope_rows, :]
            turned = _swap_pairs(kslot) * ksin_ref[rope_rows, :]
        for h in range(C_HEADS):
            kh = wide[:, h * SLOT:(h + 1) * SLOT] + kslot
            if rope_rows is not None:
                kh = inv_rms(kh) * (kh * kcos + turned)
            else:
                kh = kh * inv_rms(kh) * gk_ref[...]
            ks[h, key_rows, :] = kh.astype(bf16)
        v_t = lax.dot_general(wuvt_ref[...], cb, nt_dims, preferred_element_type=f32).astype(bf16)
        for pair in range(C_PAIRS):
            vt[pair, kb] = v_t[pair * LANES:(pair + 1) * LANES, :]

    if latent:
        def past_keys(c, carry):
            r = pl.ds(pl.multiple_of(c * L1_ROWS, L1_ROWS), L1_ROWS)
            expand_keys(cckv_ref[r, :], ckpe_ref[r, :], c, None)
            return carry

        lax.fori_loop(0, PAST // L1_ROWS, past_keys, 0)

    def project(c, carry):
        r = pl.ds(pl.multiple_of(c * PROJ_ROWS, PROJ_ROWS), PROJ_ROWS)
        h = _rms_rows(x_ref[r, :], gn_ref[...]) * (1.0 + _mod(mod_ref, SCALE1)) + _mod(mod_ref, SHIFT1)
        zs[r, :] = jnp.dot(h.astype(bf16), win_ref[...], preferred_element_type=f32)
        return carry

    lax.fori_loop(0, n // PROJ_ROWS, project, 0)

    def prepare(c, carry):
        r = pl.ds(pl.multiple_of(c * L1_ROWS, L1_ROWS), L1_ROWS)
        qa = _rms_rows(zs[r, 0:C_Q_LORA], gqa_ref[...]).astype(bf16)
        wide[...] = jnp.dot(qa, wuq_ref[...], preferred_element_type=f32)
        if latent:
            wide2[...] = jnp.dot(qa, wuqs_ref[...], preferred_element_type=f32)
            qcos, qsin = qcos_ref[r, :], qsin_ref[r, :]
        for h in range(C_HEADS):
            hs = slice(h * SLOT, (h + 1) * SLOT)
            qh = wide[:, hs]
            if latent:
                qh = inv_rms(qh) * (qh * qcos + wide2[:, hs] * qsin)
            else:
                qh = qh * inv_rms(qh) * gq_ref[...]
            qs[h, r, :] = qh.astype(bf16)
        ckv_n = _rms_rows(zs[r, C_Q_LORA:C_Q_LORA + C_KV_LORA], gkva_ref[...])
        kslot = zs[r, C_Q_LORA + C_KV_LORA:ODD_IN_PAD]
        if not latent:
            ckvo_ref[r, :] = ckv_n
            kpeo_ref[c] = kslot.T[C_NOPE:C_QK, :]
        expand_keys(ckv_n, kslot, c + n_ctx // L1_ROWS, r if latent else None)
        return carry

    lax.fori_loop(0, n // L1_ROWS, prepare, 0)

    low = lax.broadcasted_iota(jnp.int32, (2 * C_V, L1_ROWS), 0) < C_V
    n_kblocks = (n_ctx + n) // L1_ROWS
    pairs_per_step = 4 if latent else 8

    blocks_per_step = 1 if latent else 2

    def attend(cc, carry):
        blocks = [cc * blocks_per_step + j for j in range(blocks_per_step)]
        rows = [pl.ds(pl.multiple_of(c * L1_ROWS, L1_ROWS), L1_ROWS) for c in blocks]

        def values_t(pair, c, eb):
            if not latent:
                return jnp.dot(vt[pair, c], eb, preferred_element_type=f32)
            o_t = None
            for b in range(n_kblocks):
                pv = jnp.dot(vt[pair, b], eb[b * L1_ROWS:(b + 1) * L1_ROWS, :], preferred_element_type=f32)
                o_t = pv if o_t is None else o_t + pv
            return o_t

        def pairs_step(i, carry2):
            pairs = [i * pairs_per_step + j for j in range(pairs_per_step)]
            units = [(c, r, 2 * p + hh) for c, r in zip(blocks, rows) for p in pairs for hh in range(2)]
            scores = [lax.dot_general(ks[h] if latent else ks[h, r, :], qs[h, r, :], nt_dims,
                                      preferred_element_type=f32) for _, r, h in units]
            exps = [jnp.exp2(s - jnp.max(s, 0, keepdims=True)) for s in scores]
            dens = [jnp.sum(e, 0, keepdims=True) for e in exps]
            outs = [values_t(h // 2, c, e.astype(bf16)) / den for (c, _, h), e, den in zip(units, exps, dens)]
            for u in range(0, len(units), 2):
                _, r, h = units[u]
                cat[h // 2, r, :] = jnp.where(low, outs[u], outs[u + 1]).T.astype(bf16)
            return carry2

        return lax.fori_loop(0, C_PAIRS // pairs_per_step, pairs_step, carry)

    lax.fori_loop(0, n // L1_ROWS // blocks_per_step, attend, 0)

    def output(c, carry):
        r = pl.ds(pl.multiple_of(c * PROJ_ROWS, PROJ_ROWS), PROJ_ROWS)
        heads = jnp.concatenate([cat[pair, r, :] for pair in range(C_PAIRS)], axis=1)
        y = jnp.dot(heads, wo_ref[...], preferred_element_type=f32)
        xo_ref[r, :] = x_ref[r, :] + _mod(mod_ref, GATE1) * y
        return carry

    lax.fori_loop(0, n // PROJ_ROWS, output, 0)


def _slot_cols(w, heads, width, lo, hi, lane0):
    k = w.shape[0]
    w3 = w.reshape(k, heads, width)[:, :, lo:hi]
    out = jnp.zeros((k, heads, SLOT), w.dtype).at[:, :, lane0:lane0 + (hi - lo)].set(w3)
    return out.reshape(k, heads * SLOT)


def _l1_mixer(xc, xl, g_norm, mods, p, cache_ckv, cache_kpe):
    w_in = jnp.zeros((D, ODD_IN_PAD), f32).at[:, :C_Q_LORA + C_KV_LORA].set(
        p['w_in'][:, :C_Q_LORA + C_KV_LORA]).at[
        :, C_Q_LORA + C_KV_LORA + C_NOPE:C_Q_LORA + C_KV_LORA + C_QK].set(p['w_in'][:, C_Q_LORA + C_KV_LORA:])
    wuq = _slot_cols(p['w_uq'], C_HEADS, C_QK, 0, C_QK, 0).astype(bf16)
    wuk = _slot_cols(p['w_ukv'], C_HEADS, C_NOPE + C_V, 0, C_NOPE, 0).astype(bf16)
    wuv_t = p['w_ukv'].reshape(C_KV_LORA, C_HEADS, C_NOPE + C_V)[:, :, C_NOPE:].reshape(
        C_KV_LORA, C_HEADS * C_V).T.astype(bf16)
    gq = jnp.zeros((1, SLOT), f32).at[0, :C_QK].set(p['g_q'] * C_SCALE)
    gk = jnp.zeros((1, SLOT), f32).at[0, :C_QK].set(p['g_k'])
    consts = (g_norm.reshape(1, D), w_in.astype(bf16), p['g_qa'].reshape(1, C_Q_LORA), wuq, gq,
              p['g_kva'].reshape(1, C_KV_LORA), wuk, wuv_t, gk, p['w_o'].astype(bf16))
    c_specs = [_const_spec(a.shape) for a in consts]
    row = pl.BlockSpec((BLOCK_ROWS, D), lambda b: (b, 0))
    n_ctx_blocks = T_CTX // BLOCK_ROWS

    def scratch(n_keys):
        return [pltpu.VMEM((BLOCK_ROWS, ODD_IN_PAD), f32), pltpu.VMEM((C_PAIRS, BLOCK_ROWS, LANES), bf16),
                pltpu.VMEM((C_HEADS, BLOCK_ROWS, SLOT), bf16), pltpu.VMEM((C_HEADS, n_keys, SLOT), bf16),
                pltpu.VMEM((C_PAIRS, n_keys // L1_ROWS, LANES, L1_ROWS), bf16),
                pltpu.VMEM((L1_ROWS, C_SLOTS), f32)]

    xo_ctx, ckv_new, kpe_t = pl.pallas_call(
        functools.partial(_l1_kernel, latent=False),
        grid=(n_ctx_blocks,),
        in_specs=[row, c_specs[0], pl.BlockSpec((MOD_ROWS, D), lambda b: (0, 0))] + c_specs[1:],
        out_specs=[row, pl.BlockSpec((BLOCK_ROWS, C_KV_LORA), lambda b: (b, 0)),
                   pl.BlockSpec((BLOCK_ROWS // SEQ, C_ROPE, SEQ), lambda b: (b, 0, 0))],
        out_shape=[jax.ShapeDtypeStruct((T_CTX, D), f32), jax.ShapeDtypeStruct((T_CTX, C_KV_LORA), f32),
                   jax.ShapeDtypeStruct((BATCH, C_ROPE, SEQ), f32)],
        scratch_shapes=scratch(BLOCK_ROWS),
        compiler_params=_cp(("parallel",)), name="l1_mixer_ctx",
    )(xc, consts[0], mods, *consts[1:])

    cos, sin = _rope_tables(DEC_SEQ, C_ROPE, SLOT, C_NOPE)
    def pair_swap(a):
        pairs = a.reshape(a.shape[:-1] + (a.shape[-1] // 2, 2))
        return jnp.stack([pairs[..., 1], pairs[..., 0]], axis=-1).reshape(a.shape)

    w_rope = p['w_uq'].reshape(C_Q_LORA, C_HEADS, C_QK)[:, :, C_NOPE:]
    wuq_swapped = jnp.zeros((C_Q_LORA, C_HEADS, SLOT), f32).at[:, :, C_NOPE:C_QK].set(pair_swap(w_rope)).reshape(
        C_Q_LORA, C_SLOTS).astype(bf16)
    rope = (wuq_swapped, gq * cos, pair_swap(gq) * sin, gk * cos, pair_swap(gk) * sin)
    ckpe = jnp.zeros((DEC_BATCH, PAST, SLOT), f32).at[:, :, C_NOPE:C_QK].set(cache_kpe)
    xo_lat = pl.pallas_call(
        functools.partial(_l1_kernel, latent=True),
        grid=(DEC_BATCH,),
        in_specs=[pl.BlockSpec((BLOCK_ROWS, D), lambda b: (b, 0), pipeline_mode=pl.Buffered(1)), c_specs[0],
                  pl.BlockSpec((MOD_ROWS, D), lambda b: (1 + b, 0))] + c_specs[1:] + [_const_spec(a.shape) for a in rope] + [
                  pl.BlockSpec((None, PAST, C_KV_LORA), lambda b: (b, 0, 0)),
                  pl.BlockSpec((None, PAST, SLOT), lambda b: (b, 0, 0))],
        out_specs=row,
        out_shape=jax.ShapeDtypeStruct((T_LAT, D), f32),
        scratch_shapes=scratch(PAST + BLOCK_ROWS) + [pltpu.VMEM((L1_ROWS, C_SLOTS), f32)],
        compiler_params=_cp(("parallel",)), name="l1_mixer_lat",
    )(xl, consts[0], mods, *consts[1:], *rope, cache_ckv, ckpe)
    return xo_ctx, xo_lat, ckv_new, kpe_t.transpose(0, 2, 1)


ROUTER_ROWS = 40
ROUTE_ROWS = 8
ROUTER_CHUNK = 256
W_LO_ROW0 = 64


def _router_kernel(xc_ref, xl_ref, gn_ref, mod_ref, whi_ref, wst_ref, br_ref, h_ref, route_ref, lg_ref):
    nt = (((1,), (1,)), ((), ()))

    def moe_input(x_ref):
        def chunk(c, carry):
            row0 = pl.multiple_of(c * ROUTER_CHUNK, ROUTER_CHUNK)
            h = (_rms_rows(x_ref[pl.ds(row0, ROUTER_CHUNK), :], gn_ref[...]) * (1.0 + _mod(mod_ref, SCALE2))
                 + _mod(mod_ref, SHIFT2))
            _store_token_major(h_ref, h, row0)
            h_hi, h_lo = _split_bf16(h)
            by_hi = lax.dot_general(wst_ref[...], h_hi, nt, preferred_element_type=f32)
            by_lo = lax.dot_general(whi_ref[...], h_lo, nt, preferred_element_type=f32)
            lg_ref[c] = (by_hi[0:ROUTER_ROWS, :] + by_lo[0:ROUTER_ROWS, :]
                         + by_hi[W_LO_ROW0:W_LO_ROW0 + ROUTER_ROWS, :])
            return carry

        lax.fori_loop(0, ROW_TILE // ROUTER_CHUNK, chunk, 0, unroll=True)

    is_ctx = pl.program_id(0) < N_CTX_TILES
    pl.when(is_ctx)(lambda: moe_input(xc_ref))
    pl.when(jnp.logical_not(is_ctx))(lambda: moe_input(xl_ref))
    logits = (jnp.concatenate([lg_ref[c] for c in range(ROW_TILE // ROUTER_CHUNK)], axis=1)
              + br_ref[0:ROUTER_ROWS, :])
    row_i = lax.broadcasted_iota(jnp.int32, logits.shape, 0)
    row = row_i.astype(f32)
    big = 1e6
    is_g = (row_i >= N_EXPERTS) & (row_i < N_EXPERTS + N_GROUPS)
    lg = jnp.where(is_g, logits, -jnp.inf)
    mg = jnp.max(lg, 0, keepdims=True)
    gsel = jnp.min(jnp.where(lg == mg, row, big), 0, keepdims=True) - N_EXPERTS
    pg_sel = 1.0 / jnp.sum(jnp.where(is_g, jnp.exp(lg - mg), 0.0), 0, keepdims=True)
    in_grp = (row_i < N_EXPERTS) & ((row_i >> 3).astype(f32) == gsel)
    le = jnp.where(in_grp, logits, -jnp.inf)
    m1 = jnp.max(le, 0, keepdims=True)
    i1 = jnp.min(jnp.where(le == m1, row, big), 0, keepdims=True)
    le2 = jnp.where(row == i1, -jnp.inf, le)
    m2 = jnp.max(le2, 0, keepdims=True)
    i2 = jnp.min(jnp.where(le2 == m2, row, big), 0, keepdims=True)
    e2 = jnp.exp(m2 - m1)
    w1 = pg_sel / (1.0 + e2)
    w2 = pg_sel * e2 / (1.0 + e2)
    sub = lax.broadcasted_iota(jnp.int32, route_ref.shape, 0)
    route_ref[...] = jnp.where(sub == 0, i1, jnp.where(sub == 1, i2, jnp.where(sub == 2, w1,
                                                                                jnp.where(sub == 3, w2, 0.0))))


def _router(xc, xl, g_norm, mods, p):
    wr = jnp.zeros((LANES, D), f32).at[:N_EXPERTS].set(p['w_re'].T).at[
        N_EXPERTS:N_EXPERTS + N_GROUPS].set(p['w_rg'].T)
    w_hi, w_lo = _split_bf16(wr)
    w_stack = w_hi.at[W_LO_ROW0:W_LO_ROW0 + ROUTER_ROWS].set(w_lo[:ROUTER_ROWS])
    br = jnp.zeros((LANES, 1), f32).at[:N_EXPERTS, 0].set(p['b_re']).at[
        N_EXPERTS:N_EXPERTS + N_GROUPS, 0].set(p['b_rg'])
    return pl.pallas_call(
        _router_kernel,
        grid=(T // ROW_TILE,),
        in_specs=_token_specs(D) + [
                  _const_spec((1, D)),
                  pl.BlockSpec((MOD_ROWS, D), lambda i: (_sample_of_tile(i, ROW_TILE), 0)),
                  _const_spec((LANES, D)), _const_spec((LANES, D)), _const_spec((LANES, 1))],
        out_specs=[_token_major_spec(ROW_TILE, lambda i: (i, 0)),
                   pl.BlockSpec((ROUTE_ROWS, ROW_TILE), lambda i: (0, i))],
        out_shape=[jax.ShapeDtypeStruct((T * SUBS, LANES), f32), jax.ShapeDtypeStruct((ROUTE_ROWS, T), f32)],
        scratch_shapes=[pltpu.VMEM((ROW_TILE // ROUTER_CHUNK, ROUTER_ROWS, ROUTER_CHUNK), f32)],
        compiler_params=_cp(("parallel",)), name="router",
    )(xc, xl, g_norm.reshape(1, D), mods, w_hi, w_stack, br)


PLAN_FIRST_TILE, PLAN_TILES = 0, 1


def _plan_kernel(rt_ref, pos_ref, plan_ref, rank):
    n_blk = T // 128
    e_col = lax.broadcasted_iota(jnp.int32, (N_EXPERTS, 128), 0).astype(f32)
    ri = lax.broadcasted_iota(jnp.int32, (128, 128), 0)
    ci = lax.broadcasted_iota(jnp.int32, (128, 128), 1)
    before = jnp.where(ri < ci, 1.0, 0.0).astype(bf16)

    def picks(b):
        cs = slice(b * 128, (b + 1) * 128)
        return rt_ref[0:1, cs] == e_col, rt_ref[1:2, cs] == e_col

    counts = jnp.zeros((N_EXPERTS, 1), f32)
    for b in range(n_blk):
        m0, m1 = picks(b)
        m = jnp.where(m0, 1.0, 0.0) + jnp.where(m1, 1.0, 0.0)
        rank[:, b * 128:(b + 1) * 128] = jnp.dot(m.astype(bf16), before, preferred_element_type=f32) + counts
        counts = counts + jnp.sum(m, axis=1, keepdims=True)

    tiles = jnp.floor((counts + (MOE_TILE - 1.0)) * (1.0 / MOE_TILE))
    er = lax.broadcasted_iota(jnp.int32, (N_EXPERTS, N_EXPERTS), 0)
    ec = lax.broadcasted_iota(jnp.int32, (N_EXPERTS, N_EXPERTS), 1)
    earlier = jnp.where(ec < er, 1.0, 0.0).astype(bf16)
    tile_start = jnp.dot(earlier, jnp.broadcast_to(tiles, (N_EXPERTS, 128)).astype(bf16),
                         preferred_element_type=f32)
    row_start = tile_start * MOE_TILE

    sub = lax.broadcasted_iota(jnp.int32, (8, 128), 0)
    for b in range(n_blk):
        m0, m1 = picks(b)
        base = rank[:, b * 128:(b + 1) * 128] + row_start
        p0 = jnp.sum(jnp.where(m0, base, 0.0), axis=0, keepdims=True)
        p1 = jnp.sum(jnp.where(m1, base, 0.0), axis=0, keepdims=True)
        pos_ref[:, b * 128:(b + 1) * 128] = jnp.where(sub == 0, p0, jnp.where(sub == 1, p1, 0.0)).astype(jnp.int32)

    diag = (lax.broadcasted_iota(jnp.int32, (N_EXPERTS, 128), 0)
            == lax.broadcasted_iota(jnp.int32, (N_EXPERTS, 128), 1))
    first = jnp.sum(jnp.where(diag, tile_start, 0.0), axis=0, keepdims=True)
    count = jnp.sum(jnp.where(diag, tiles, 0.0), axis=0, keepdims=True)
    rows = jnp.where(sub == PLAN_FIRST_TILE, first, jnp.where(sub == PLAN_TILES, count, 0.0))
    plan_ref[...] = rows.astype(jnp.int32)


def _slot_code(t, k):
    return t * SUBS + k * (SUBS // 2)


def _code_offset(code):
    return pl.multiple_of(code & ~(SUBS - 1), SUBS)


def _code_gate_index(code):
    return code >> 2


PAD_CODE = T * SUBS


def _invert_slots(pos_ref, first_ref, count_ref, code_ref):
    def pad_tile(tile, carry):
        for u in range(MOE_TILE):
            code_ref[tile * MOE_TILE + u] = PAD_CODE
        return carry

    def pad_last_tile(e, carry):
        return pad_tile(jnp.maximum(first_ref[e] + count_ref[e] - 1, 0), carry)
    lax.fori_loop(0, N_EXPERTS, pad_last_tile, 0)
    lax.fori_loop(first_ref[N_EXPERTS - 1] + count_ref[N_EXPERTS - 1], MOE_TILES, pad_tile, 0)

    group = 16
    for k in range(2):
        def place(i, carry):
            t0 = i * group
            slots = [pos_ref[k * T + t0 + u] for u in range(group)]
            code0 = _slot_code(t0, 0)
            for u, s in enumerate(slots):
                code_ref[s] = code0 + _slot_code(u, k)
            return carry
        lax.fori_loop(0, T // group, place, 0)


def _route_plan(route_t):
    pos, plan = pl.pallas_call(
        _plan_kernel,
        out_shape=[jax.ShapeDtypeStruct((8, T), jnp.int32), jax.ShapeDtypeStruct((8, LANES), jnp.int32)],
        scratch_shapes=[pltpu.VMEM((N_EXPERTS, T), f32)],
        compiler_params=_cp(None), name="route_plan",
    )(route_t)
    gates = jnp.pad(route_t[2:4].T.reshape(2 * T), (0, 8))
    return plan[PLAN_FIRST_TILE, :N_EXPERTS], plan[PLAN_TILES, :N_EXPERTS], pos[0:2].reshape(2 * T), gates


def _tile_index(i):
    return jnp.minimum(i, MOE_TILES - 1)


TM_ROWS = T * SUBS
SCATTER_GROUP = 8
STAGES = 4
LAST_EXPERT = N_EXPERTS - 1


def _tile_rows(g):
    return pl.ds(pl.multiple_of(g * MOE_TILE, MOE_TILE), MOE_TILE)


def _expert_tile_pairs(first, count, tile_step):
    def pair(pp, carry):
        for parity in range(2):
            g = 2 * pp + parity
            pl.when((g >= first) & (g < first + count))(functools.partial(tile_step, g, parity))
        return carry
    lax.fori_loop(first // 2, (first + count + 1) // 2, pair, 0)


def _gather_tile(code_ref, tile, xs, gbuf):
    base = tile * MOE_TILE
    for r in range(MOE_TILE):
        gbuf[r * SUBS:(r + 1) * SUBS, :] = xs[pl.ds(_code_offset(code_ref[base + r]), SUBS), :]


def _moe_up_kernel(first_ref, count_ref, pos_ref, h_hbm, w1_ref, w3_ref, hh_hbm, code_ref,
                   xs, gbuf_a, gbuf_b, w13, obuf, sem_x, sem_o):
    e = pl.program_id(0)
    first, count = first_ref[e], count_ref[e]
    gbufs = (gbuf_a, gbuf_b)

    def out_copy(slot, g):
        return pltpu.make_async_copy(obuf.at[slot], hh_hbm.at[_tile_rows(g), :], sem_o.at[slot])

    @pl.when(e == 0)
    def _():
        cp = pltpu.make_async_copy(h_hbm, xs.at[pl.ds(0, TM_ROWS), :], sem_x)
        cp.start()
        _invert_slots(pos_ref, first_ref, count_ref, code_ref)
        xs[TM_ROWS:TM_ROWS + SUBS, :] = jnp.zeros((SUBS, LANES), f32)
        cp.wait()
        _gather_tile(code_ref, 0, xs, gbuf_a)

    @pl.when(count > 0)
    def _():
        w13[:, :D_EXPERT] = w1_ref[0].astype(bf16)
        w13[:, D_EXPERT:] = w3_ref[0].astype(bf16)

    def tile_step(g, parity):
        _gather_tile(code_ref, _tile_index(g + 1), xs, gbufs[1 - parity])
        x3 = jnp.swapaxes(gbufs[parity][...].reshape(MOE_TILE, SUBS, LANES), 0, 1)
        x = jnp.concatenate([x3[s] for s in range(SUBS)], axis=1).astype(bf16)
        h13 = jnp.dot(x, w13[...], preferred_element_type=f32)
        hh = (_silu(h13[:, :D_EXPERT]) * h13[:, D_EXPERT:]).astype(bf16)

        slot = g % STAGES

        @pl.when(g >= STAGES)
        def _():
            out_copy(slot, g).wait()
        obuf[slot] = hh
        out_copy(slot, g).start()

    _expert_tile_pairs(first, count, tile_step)

    @pl.when(e == LAST_EXPERT)
    def _():
        n_used = first + count
        for slot in range(STAGES):
            pl.when(n_used > slot)(lambda slot=slot: out_copy(slot, 0).wait())
        obuf[0] = jnp.zeros((MOE_TILE, D_EXPERT), bf16)

        def zero_tile(g, carry):
            cp = out_copy(0, g)
            cp.start()
            cp.wait()
            return carry
        lax.fori_loop(n_used, MOE_TILES, zero_tile, 0)


def _scatter_tile(code_ref, gate_ref, tile, ybuf, acc):
    base = tile * MOE_TILE
    for g0 in range(0, MOE_TILE, SCATTER_GROUP):
        rows = range(g0, g0 + SCATTER_GROUP)
        codes = [code_ref[base + r] for r in rows]
        new = [acc[pl.ds(_code_offset(c), SUBS), :]
               + gate_ref[_code_gate_index(c)] * ybuf[r * SUBS:(r + 1) * SUBS, :]
               for r, c in zip(rows, codes)]
        for c, v in zip(codes, new):
            acc[pl.ds(_code_offset(c), SUBS), :] = v


RES_ROWS = 256


def _residual_out(x_hbm, y_hbm, tok0, sample_of_chunk, mod_ref, acc, rin, rout, sem_r, sem_w):
    n_chunks = x_hbm.shape[0] // RES_ROWS

    def rows(c):
        return pl.ds(pl.multiple_of(c * RES_ROWS, RES_ROWS), RES_ROWS)

    def in_copy(slot, c):
        return pltpu.make_async_copy(x_hbm.at[rows(c), :], rin.at[slot], sem_r.at[slot])

    def out_copy(slot, c):
        return pltpu.make_async_copy(rout.at[slot], y_hbm.at[rows(c), :], sem_w.at[slot])

    for c in range(STAGES - 1):
        in_copy(c, c).start()

    def ring(cc, carry):
        for slot in range(STAGES):
            c = STAGES * cc + slot
            in_copy(slot, c).wait()
            ahead = c + STAGES - 1

            @pl.when(ahead < n_chunks)
            def _():
                in_copy((slot + STAGES - 1) % STAGES, ahead).start()

            @pl.when(c >= STAGES)
            def _():
                out_copy(slot, c).wait()
            delta = _load_token_major(acc, RES_ROWS, tok0 + c * RES_ROWS)
            gate = mod_ref[pl.ds(sample_of_chunk(c) * MOD_ROWS + GATE2, 1), :]
            rout[slot] = rin[slot] + gate * delta
            out_copy(slot, c).start()
        return carry

    lax.fori_loop(0, n_chunks // STAGES, ring, 0)
    for slot in range(STAGES):
        out_copy(slot, 0).wait()


def _moe_down_kernel(first_ref, count_ref, code_ref, gate_ref, hh_hbm, w2_ref, xc_hbm, xl_hbm, mod_ref,
                     yc_hbm, yl_hbm, acc, ybuf_a, ybuf_b, w2b, ibuf, rin, rout, sem_i, sem_r, sem_w):
    e = pl.program_id(0)
    first, count = first_ref[e], count_ref[e]
    n_used = first_ref[LAST_EXPERT] + count_ref[LAST_EXPERT]
    ybufs = (ybuf_a, ybuf_b)

    def in_copy(slot, g):
        return pltpu.make_async_copy(hh_hbm.at[_tile_rows(g), :], ibuf.at[slot], sem_i.at[slot])

    @pl.when(e == 0)
    def _():
        for g in range(STAGES - 1):
            in_copy(g, g).start()

        def zero(c, carry):
            acc[pl.ds(pl.multiple_of(c * 1024, 1024), 1024), :] = jnp.zeros((1024, LANES), f32)
            return carry
        lax.fori_loop(0, TM_ROWS // 1024, zero, 0)
        acc[TM_ROWS:TM_ROWS + SUBS, :] = jnp.zeros((SUBS, LANES), f32)
        ybuf_b[...] = jnp.zeros_like(ybuf_b)

    @pl.when(count > 0)
    def _():
        w2b[...] = w2_ref[0].astype(bf16)

    def tile_step(g, parity):
        slot = g % STAGES
        in_copy(slot, g).wait()
        ahead = g + STAGES - 1

        @pl.when(ahead < n_used)
        def _():
            in_copy(ahead % STAGES, ahead).start()
        _store_token_major(ybufs[parity], jnp.dot(ibuf[slot], w2b[...], preferred_element_type=f32))
        _scatter_tile(code_ref, gate_ref, jnp.maximum(g - 1, 0), ybufs[1 - parity], acc)

    _expert_tile_pairs(first, count, tile_step)

    @pl.when(e == LAST_EXPERT)
    def _():
        for parity in range(2):
            pl.when((n_used > 0) & ((n_used - 1) % 2 == parity))(
                functools.partial(_scatter_tile, code_ref, gate_ref, n_used - 1, ybufs[parity], acc))
        _residual_out(xc_hbm, yc_hbm, 0, lambda c: 0, mod_ref, acc, rin, rout, sem_r, sem_w)
        _residual_out(xl_hbm, yl_hbm, T_CTX, lambda c: 1 + c // (DEC_SEQ // RES_ROWS),
                      mod_ref, acc, rin, rout, sem_r, sem_w)


def _moe(h_tm, route_t, p, xc, xl, mods):
    first_tile, n_tiles, slots, gates = _route_plan(route_t)
    tile_rows = pltpu.VMEM((MOE_TILE * SUBS, LANES), f32)
    staging = pltpu.VMEM((STAGES, MOE_TILE, D_EXPERT), bf16)
    res_rows = pltpu.VMEM((STAGES, RES_ROWS, D), f32)
    hbm = pl.BlockSpec(memory_space=pl.ANY)
    hh, codes = pl.pallas_call(
        _moe_up_kernel,
        grid_spec=pltpu.PrefetchScalarGridSpec(
            num_scalar_prefetch=3, grid=(N_EXPERTS,),
            in_specs=[hbm,
                      pl.BlockSpec((1, D, D_EXPERT), lambda e, f, n, s: (e, 0, 0)),
                      pl.BlockSpec((1, D, D_EXPERT), lambda e, f, n, s: (e, 0, 0))],
            out_specs=[hbm, pl.BlockSpec(memory_space=pltpu.SMEM)],
            scratch_shapes=[pltpu.VMEM((TM_ROWS + SUBS, LANES), f32), tile_rows, tile_rows,
                            pltpu.VMEM((D, 2 * D_EXPERT), bf16), staging,
                            pltpu.SemaphoreType.DMA(()), pltpu.SemaphoreType.DMA((STAGES,))]),
        out_shape=[jax.ShapeDtypeStruct((MOE_ROWS, D_EXPERT), bf16),
                   jax.ShapeDtypeStruct((MOE_ROWS,), jnp.int32)],
        compiler_params=_cp(("arbitrary",)), name="moe_up",
    )(first_tile, n_tiles, slots, h_tm, p['w1'], p['w3'])
    return pl.pallas_call(
        _moe_down_kernel,
        grid_spec=pltpu.PrefetchScalarGridSpec(
            num_scalar_prefetch=4, grid=(N_EXPERTS,),
            in_specs=[hbm, pl.BlockSpec((1, D_EXPERT, D), lambda e, f, n, c, g: (e, 0, 0)), hbm, hbm,
                      pl.BlockSpec((N_SAMPLES * MOD_ROWS, D), lambda e, f, n, c, g: (0, 0),
                                   pipeline_mode=pl.Buffered(1))],
            out_specs=[hbm, hbm],
            scratch_shapes=[pltpu.VMEM((TM_ROWS + SUBS, LANES), f32), tile_rows, tile_rows,
                            pltpu.VMEM((D_EXPERT, D), bf16), staging, res_rows, res_rows,
                            pltpu.SemaphoreType.DMA((STAGES,)), pltpu.SemaphoreType.DMA((STAGES,)),
                            pltpu.SemaphoreType.DMA((STAGES,))]),
        out_shape=[jax.ShapeDtypeStruct((T_CTX, D), f32), jax.ShapeDtypeStruct((T_LAT, D), f32)],
        compiler_params=_cp(("arbitrary",)), name="moe_down",
    )(first_tile, n_tiles, codes, gates, hh, p['w2'], xc, xl, mods)


def kernel(x_prompt, x_sample, cache_l0_k, cache_l0_v, cache_l1_ckv, cache_l1_kpe, c, c_ctx, l0_g_norm1, l0_g_norm2, l0_w_ada, l0_b_ada, l0_w_in, l0_g_vnorm, l0_w_s, l0_b_s, l0_g_q, l0_g_k, l0_sink, l0_w_o, l0_w_rg, l0_b_rg, l0_w_re, l0_b_re, l0_w1, l0_w3, l0_w2, l1_g_norm1, l1_g_norm2, l1_w_ada, l1_b_ada, l1_w_in, l1_g_qa, l1_w_uq, l1_g_kva, l1_w_ukv, l1_g_q, l1_g_k, l1_w_o, l1_w_rg, l1_b_rg, l1_w_re, l1_b_re, l1_w1, l1_w3, l1_w2):
    p0 = dict(w_in=l0_w_in, g_vnorm=l0_g_vnorm, w_s=l0_w_s, b_s=l0_b_s, g_q=l0_g_q, g_k=l0_g_k, sink=l0_sink,
              w_o=l0_w_o, w_rg=l0_w_rg, b_rg=l0_b_rg, w_re=l0_w_re, b_re=l0_b_re, w1=l0_w1, w3=l0_w3, w2=l0_w2)
    p1 = dict(w_in=l1_w_in, g_qa=l1_g_qa, w_uq=l1_w_uq, g_kva=l1_g_kva, w_ukv=l1_w_ukv, g_q=l1_g_q, g_k=l1_g_k,
              w_o=l1_w_o, w_rg=l1_w_rg, b_rg=l1_b_rg, w_re=l1_w_re, b_re=l1_b_re, w1=l1_w1, w3=l1_w3, w2=l1_w2)

    cond8 = jnp.zeros((MOD_ROWS, D), f32).at[0].set(c_ctx).at[1:1 + DEC_BATCH].set(c)
    mods0 = _mod_rows(_adaln(cond8, l0_w_ada, l0_b_ada))
    mods1 = _mod_rows(_adaln(cond8, l1_w_ada, l1_b_ada))

    xc0 = x_prompt.reshape(T_CTX, D)
    xl0 = x_sample.reshape(T_LAT, D)

    xc0m, xl0m, k_new, v_new = _l0_mixer(xc0, xl0, l0_g_norm1, mods0, p0, cache_l0_k, cache_l0_v)
    h0, route0 = _router(xc0m, xl0m, l0_g_norm2, mods0, p0)
    xc1, xl1 = _moe(h0, route0, p0, xc0m, xl0m, mods0)

    xc1m, xl1m, ckv_new, kpe_new = _l1_mixer(xc1, xl1, l1_g_norm1, mods1, p1, cache_l1_ckv, cache_l1_kpe)
    h1, route1 = _router(xc1m, xl1m, l1_g_norm2, mods1, p1)
    y_prompt, y_sample = _moe(h1, route1, p1, xc1m, xl1m, mods1)
    return (y_prompt.reshape(BATCH, SEQ, D), y_sample.reshape(DEC_BATCH, DEC_SEQ, D), k_new, v_new,
            ckv_new.reshape(BATCH, SEQ, C_KV_LORA), kpe_new.reshape(BATCH, SEQ, C_ROPE))
```

```python
import functools

import jax
import jax.numpy as jnp
import numpy as np
from jax import lax
from jax.experimental import pallas as pl
from jax.experimental.pallas import tpu as pltpu

f32 = jnp.float32
bf16 = jnp.bfloat16

D = 1024
BATCH, SEQ = 32, 256
DEC_BATCH, DEC_SEQ = 2, 1024
PAST = 512
T_CTX = BATCH * SEQ
T_LAT = DEC_BATCH * DEC_SEQ
T = T_CTX + T_LAT
GRID_W = 64
CHUNK = 128
WINDOW = 128
ROPE_THETA = 10000.0
EPS = 1e-6
NEG_INF = -1e30
LANES = 128
SUBS = D // LANES

A_WIDTH = 512
A_GROUPS = 4
B_HEADS, B_KV, B_GROUP, B_HD = 8, 2, 4, 64
LOG2E = 1.4426950408889634
B_SCALE = B_HD ** -0.5 * LOG2E

C_HEADS, C_Q_LORA, C_KV_LORA, C_NOPE, C_ROPE, C_V = 16, 384, 256, 64, 32, 64
C_QK = C_NOPE + C_ROPE
C_SCALE = C_QK ** -0.5 * LOG2E
ODD_IN_PAD = 768
SLOT = 128

N_GROUPS, N_EXPERTS, D_EXPERT = 4, 32, 256

N_SAMPLES = 1 + DEC_BATCH
MOD_ROWS = 8
SHIFT1, SCALE1, GATE1, SHIFT2, SCALE2, GATE2 = range(6)

ROW_TILE = 1024
BLOCK_ROWS = 1024
PROJ_ROWS = 512
ADALN_COLS = 1536
MOE_TILE = 256
MOE_ROWS = 2 * T + N_EXPERTS * MOE_TILE
MOE_TILES = MOE_ROWS // MOE_TILE
VMEM_CAP = 56 * 1024 * 1024


def _cp(sem, vmem=VMEM_CAP):
    return pltpu.CompilerParams(dimension_semantics=sem, vmem_limit_bytes=vmem)


def _const_spec(shape):
    nd = len(shape)
    return pl.BlockSpec(shape, lambda *_: (0,) * nd, pipeline_mode=pl.Buffered(1))


def _sample_of_tile(i, tile):
    n_ctx = T_CTX // tile
    per_lat = DEC_SEQ // tile
    return jnp.where(i < n_ctx, 0, 1 + (i - n_ctx) // per_lat)


def _mod(mod_ref, row):
    return mod_ref[row:row + 1, :]


def _silu(x):
    return x * jax.nn.sigmoid(x)


def _rms_rows(x, g):
    return x * lax.rsqrt(jnp.mean(x * x, -1, keepdims=True) + EPS) * g


def _swap_pairs(x):
    lane = lax.broadcasted_iota(jnp.int32, x.shape, x.ndim - 1)
    nxt = pltpu.roll(x, x.shape[-1] - 1, x.ndim - 1)
    prv = pltpu.roll(x, 1, x.ndim - 1)
    return jnp.where((lane & 1) == 0, nxt, prv)


def _split_bf16(x):
    hi = x.astype(bf16)
    return hi, (x - hi.astype(f32)).astype(bf16)


def _adaln_kernel(c_ref, w_ref, b_ref, o_ref):
    s_hi, s_lo = _split_bf16(_silu(c_ref[...]))
    w_hi, w_lo = _split_bf16(w_ref[...])
    o_ref[...] = (jnp.dot(s_hi, w_hi, preferred_element_type=f32) + jnp.dot(s_lo, w_hi, preferred_element_type=f32)
                  + jnp.dot(s_hi, w_lo, preferred_element_type=f32) + b_ref[...])


def _adaln(cond8, w, b):
    n = w.shape[1]
    return pl.pallas_call(
        _adaln_kernel,
        grid=(n // ADALN_COLS,),
        in_specs=[_const_spec((MOD_ROWS, D)), pl.BlockSpec((D, ADALN_COLS), lambda j: (0, j)),
                  pl.BlockSpec((1, ADALN_COLS), lambda j: (0, j))],
        out_specs=pl.BlockSpec((MOD_ROWS, ADALN_COLS), lambda j: (0, j)),
        out_shape=jax.ShapeDtypeStruct((MOD_ROWS, n), f32),
        compiler_params=_cp(("arbitrary",)),
        name="adaln",
    )(cond8, w, b.reshape(1, n))


def _mod_rows(m8):
    m = m8[:N_SAMPLES].reshape(N_SAMPLES, 6, D)
    return jnp.pad(m, ((0, 0), (0, MOD_ROWS - 6), (0, 0))).reshape(N_SAMPLES * MOD_ROWS, D)


def _store_token_major(ref, x, row0=0):
    n = x.shape[0]
    for s in range(SUBS):
        ref[pl.ds(row0 * SUBS + s, n, stride=SUBS), :] = x[:, s * LANES:(s + 1) * LANES]


def _load_token_major(ref, n, row0=0):
    return jnp.concatenate([ref[pl.ds(row0 * SUBS + s, n, stride=SUBS), :] for s in range(SUBS)], axis=1)


def _token_major_spec(rows, index_map):
    return pl.BlockSpec((rows * SUBS, LANES), index_map)


def _rope_tables(n, rot_dim, lanes, lane0, copies=1):
    rows_count = n // GRID_W
    rows = np.repeat(np.arange(rows_count), GRID_W).astype(np.float64)
    cols = np.tile(np.arange(GRID_W), rows_count).astype(np.float64)
    d_axis = rot_dim // 2
    inv = ROPE_THETA ** (-np.arange(0, d_axis, 2, dtype=np.float64) / d_axis)
    ang = np.concatenate([rows[:, None] * inv, cols[:, None] * inv], -1)
    c = np.ones((n, lanes), np.float32)
    s = np.zeros((n, lanes), np.float32)
    for j in range(copies):
        lo = lane0 + j * rot_dim
        c[:, lo:lo + rot_dim] = np.repeat(np.cos(ang), 2, axis=1)
        s[:, lo:lo + rot_dim] = np.repeat(np.sin(ang), 2, axis=1) * np.tile(np.array([-1.0, 1.0]), rot_dim // 2)
    return jnp.asarray(c), jnp.asarray(s)


L0_Q0 = 2 * A_WIDTH
L0_K0 = L0_Q0 + B_GROUP * LANES
L0_V0 = L0_K0 + B_KV * B_HD
L0_IN = L0_V0 + B_KV * B_HD


def _l0_kernel(*refs, latent):
    if latent:
        (sink_ref, x_ref, gn_ref, mod_ref, win_ref, gvn_ref, ws_ref, bsb_ref, gq_ref, gk_ref, wo_ref,
         cos_ref, sin_ref, kc_ref, vc_ref, xo_ref, zs, cat, qs, ks, vt, kcb, vct) = refs
        key_off = WINDOW
    else:
        (sink_ref, x_ref, gn_ref, mod_ref, win_ref, gvn_ref, ws_ref, bsb_ref, gq_ref, gk_ref, wo_ref,
         xo_ref, ko_ref, vo_ref, zs, cat, qs, ks, vt, kf, vf) = refs
        key_off = 0
    n = BLOCK_ROWS
    n_chunks = n // CHUNK
    low = lax.broadcasted_iota(jnp.int32, (CHUNK, LANES), 1) < B_HD

    if latent:
        zpad = jnp.zeros((WINDOW, LANES), bf16)
        for c0 in (0, 1 + n_chunks):
            ks[c0 * CHUNK:(c0 + 1) * CHUNK, :] = zpad
            vt[c0] = zpad
        kcb[...] = kc_ref[...].astype(bf16)
        for i in range(PAST // CHUNK):
            vct[i] = vc_ref[i * CHUNK:(i + 1) * CHUNK, :].T.astype(bf16)

    def project(c, carry):
        r = pl.ds(pl.multiple_of(c * PROJ_ROWS, PROJ_ROWS), PROJ_ROWS)
        h = _rms_rows(x_ref[r, :], gn_ref[...]) * (1.0 + _mod(mod_ref, SCALE1)) + _mod(mod_ref, SHIFT1)
        zs[r, :] = jnp.dot(h.astype(bf16), win_ref[...], preferred_element_type=f32)
        return carry

    lax.fori_loop(0, n // PROJ_ROWS, project, 0)

    def prepare(c, carry):
        r = pl.ds(pl.multiple_of(c * CHUNK, CHUNK), CHUNK)
        u = jax.nn.gelu(zs[r, 0:A_WIDTH])
        v = jax.nn.gelu(zs[r, A_WIDTH:2 * A_WIDTH])
        mu = jnp.mean(v, -1, keepdims=True)
        var = jnp.mean(jnp.square(v - mu), -1, keepdims=True)
        vn = ((v - mu) * lax.rsqrt(var + EPS) * gvn_ref[...]).astype(bf16)
        for g in range(A_GROUPS):
            cs = slice(g * CHUNK, (g + 1) * CHUNK)
            mixed = jnp.dot(ws_ref[g], vn[:, cs], preferred_element_type=f32) + bsb_ref[g]
            cat[r, cs] = (u[:, cs] * mixed).astype(bf16)
        if latent:
            cs_, sn_ = cos_ref[r, :], sin_ref[r, :]
        def half_norm(v, gain):
            sq = v * v
            s0 = jnp.sum(jnp.where(low, sq, 0.0), -1, keepdims=True)
            s1 = jnp.sum(jnp.where(low, 0.0, sq), -1, keepdims=True)
            return v * lax.rsqrt(jnp.where(low, s0, s1) * (1.0 / B_HD) + EPS) * gain

        for j in range(B_GROUP):
            js = slice(j * LANES, (j + 1) * LANES)
            qj = half_norm(zs[r, L0_Q0 + j * LANES:L0_Q0 + (j + 1) * LANES], gq_ref[...])
            if latent:
                qj = qj * cs_ + _swap_pairs(qj) * sn_
            qs[r, js] = qj.astype(bf16)
        k = half_norm(zs[r, L0_K0:L0_K0 + LANES], gk_ref[...])
        vv = zs[r, L0_V0:L0_V0 + LANES]
        if latent:
            k = k * cs_ + _swap_pairs(k) * sn_
        else:
            kf[r, :] = k
            vf[r, :] = vv
        kr = pl.ds(pl.multiple_of(c * CHUNK + key_off, CHUNK), CHUNK)
        ks[kr, :] = k.astype(bf16)
        vt[c + key_off // CHUNK] = vv.T.astype(bf16)
        return carry

    lax.fori_loop(0, n_chunks // 2, lambda i, carry: prepare(2 * i + 1, prepare(2 * i, carry)), 0)

    def attend(jobs):
        scored = []
        for r, rows, key_sets in jobs:
            low_q = lax.broadcasted_iota(jnp.int32, (rows, LANES), 1) < B_HD
            slots = [qs[r, j * LANES:(j + 1) * LANES] for j in range(B_GROUP)]
            zero = jnp.zeros((rows, LANES), bf16)
            q = jnp.concatenate([jnp.where(low_q, qj, zero) for qj in slots]
                                + [jnp.where(low_q, zero, qj) for qj in slots], axis=0)
            sk = jnp.concatenate([jnp.full((1, rows), sink_ref[h] * LOG2E, f32) for h in range(B_HEADS)], axis=1)
            scores = []
            m = sk
            for k, _, keep in key_sets:
                s = lax.dot_general(k, q, (((1,), (1,)), ((), ())), preferred_element_type=f32)
                if keep is not None:
                    s = jnp.where(keep, s, NEG_INF)
                scores.append(s)
                m = jnp.maximum(m, jnp.max(s, 0, keepdims=True))
            scored.append((sk, scores, m))
        outs = []
        for (r, rows, key_sets), (sk, scores, m) in zip(jobs, scored):
            den = jnp.exp2(sk - m)
            ot = None
            for s, (_, vts, _) in zip(scores, key_sets):
                e = jnp.exp2(s - m)
                den = den + jnp.sum(e, 0, keepdims=True)
                eb = e.astype(bf16)
                for i, v_t in enumerate(vts):
                    pv = jnp.dot(v_t, eb[i * CHUNK:(i + 1) * CHUNK, :], preferred_element_type=f32)
                    ot = pv if ot is None else ot + pv
            outs.append(ot * (1.0 / den))
        for (r, rows, _), ot in zip(jobs, outs):
            for pair in range(B_HEADS // 2):
                f0 = (2 * pair // B_GROUP) * B_HD
                pair_t = jnp.concatenate([ot[f0:f0 + B_HD, 2 * pair * rows:(2 * pair + 1) * rows],
                                          ot[f0:f0 + B_HD, (2 * pair + 1) * rows:(2 * pair + 2) * rows]], axis=0)
                cat[r, A_WIDTH + pair * LANES:A_WIDTH + (pair + 1) * LANES] = pair_t.T.astype(bf16)

    jobs_per_step = 2 if latent else 4
    if latent:
        span = CHUNK + 2 * WINDOW

        def block_job(c):
            start = pl.multiple_of(c * CHUNK, CHUNK)
            kr = pl.ds(start, span)
            kj = lax.broadcasted_iota(jnp.int32, (span, B_HEADS * CHUNK), 0)
            qi = lax.broadcasted_iota(jnp.int32, (span, B_HEADS * CHUNK), 1) & (CHUNK - 1)
            kpos = start - WINDOW + kj
            keep = (jnp.abs(kj - WINDOW - qi) <= WINDOW) & (kpos >= 0) & (kpos < n)
            return (pl.ds(start, CHUNK), CHUNK,
                    [(ks[kr, :], [vt[c + i] for i in range(span // CHUNK)], keep),
                     (kcb[...], [vct[i] for i in range(PAST // CHUNK)], None)])

        def attend_blocks(i, carry):
            attend([block_job(i * jobs_per_step + j) for j in range(jobs_per_step)])
            return carry

        lax.fori_loop(0, n_chunks // jobs_per_step, attend_blocks, 0)
    else:
        def seq_job(sq):
            r = pl.ds(pl.multiple_of(sq * SEQ, SEQ), SEQ)
            return r, SEQ, [(ks[r, :], [vt[sq * (SEQ // CHUNK) + i] for i in range(SEQ // CHUNK)], None)]

        def attend_seqs(i, carry):
            seqs = [i * jobs_per_step + j for j in range(jobs_per_step)]
            attend([seq_job(sq) for sq in seqs])
            for sq in seqs:
                r = pl.ds(pl.multiple_of(sq * SEQ, SEQ), SEQ)
                ko_ref[sq] = kf[r, :].T
                vo_ref[sq] = vf[r, :].T
            return carry

        lax.fori_loop(0, n // SEQ // jobs_per_step, attend_seqs, 0)

    def output(c, carry):
        r = pl.ds(pl.multiple_of(c * PROJ_ROWS, PROJ_ROWS), PROJ_ROWS)
        y = jnp.dot(cat[r, :], wo_ref[...], preferred_element_type=f32)
        xo_ref[r, :] = x_ref[r, :] + _mod(mod_ref, GATE1) * y
        return carry

    lax.fori_loop(0, n // PROJ_ROWS, output, 0)


def _l0_mixer(xc, xl, g_norm, mods, p, cache_k, cache_v):
    w = p['w_in']
    q_slots = w[:, L0_Q0:L0_Q0 + B_HEADS * B_HD].reshape(D, B_KV, B_GROUP, B_HD).transpose(0, 2, 1, 3).reshape(
        D, B_GROUP * LANES)
    win = jnp.concatenate([w[:, :L0_Q0], q_slots, w[:, L0_Q0 + B_HEADS * B_HD:]], axis=1).astype(bf16)
    gq = jnp.tile(p['g_q'], B_KV).reshape(1, LANES) * B_SCALE
    gk = jnp.tile(p['g_k'], B_KV).reshape(1, LANES)
    gvn = p['g_vnorm'].reshape(1, A_WIDTH)
    ws = p['w_s'].astype(bf16)
    bsb = jnp.broadcast_to(p['b_s'][:, :, None], (A_GROUPS, CHUNK, CHUNK))
    wo = p['w_o'].astype(bf16)
    weights = (g_norm.reshape(1, D),)
    consts = (win, gvn, ws, bsb, gq, gk, wo)
    c_specs = [_const_spec(a.shape) for a in consts]
    smem = pl.BlockSpec(memory_space=pltpu.SMEM)
    row = pl.BlockSpec((BLOCK_ROWS, D), lambda b: (b, 0))
    kv = pl.BlockSpec((BLOCK_ROWS // SEQ, LANES, SEQ), lambda b: (b, 0, 0))

    def scratch(pad):
        return [pltpu.VMEM((BLOCK_ROWS, L0_IN), f32), pltpu.VMEM((BLOCK_ROWS, D), bf16),
                pltpu.VMEM((BLOCK_ROWS, B_GROUP * LANES), bf16), pltpu.VMEM((BLOCK_ROWS + pad, LANES), bf16),
                pltpu.VMEM(((BLOCK_ROWS + pad) // CHUNK, LANES, CHUNK), bf16)]

    kv_shape = jax.ShapeDtypeStruct((BATCH, LANES, SEQ), f32)
    state = pltpu.VMEM((BLOCK_ROWS, LANES), f32)
    xo_ctx, k_t, v_t = pl.pallas_call(
        functools.partial(_l0_kernel, latent=False),
        grid=(T_CTX // BLOCK_ROWS,),
        in_specs=[smem, row, _const_spec((1, D)), pl.BlockSpec((MOD_ROWS, D), lambda b: (0, 0))] + c_specs,
        out_specs=[row, kv, kv],
        out_shape=[jax.ShapeDtypeStruct((T_CTX, D), f32), kv_shape, kv_shape],
        scratch_shapes=scratch(0) + [state, state],
        compiler_params=_cp(("parallel",)), name="l0_mixer_ctx",
    )(p['sink'], xc, *weights, mods, *consts)
    k_new = k_t.reshape(BATCH, B_KV, B_HD, SEQ).transpose(0, 3, 1, 2)
    v_new = v_t.reshape(BATCH, B_KV, B_HD, SEQ).transpose(0, 3, 1, 2)

    cos, sin = _rope_tables(DEC_SEQ, B_HD, LANES, 0, copies=LANES // B_HD)
    cache = pl.BlockSpec((None, PAST, LANES), lambda b: (b, 0, 0))
    past = [pltpu.VMEM((PAST, LANES), bf16), pltpu.VMEM((PAST // CHUNK, LANES, CHUNK), bf16)]
    xo_lat = pl.pallas_call(
        functools.partial(_l0_kernel, latent=True),
        grid=(DEC_BATCH,),
        in_specs=[smem, row, _const_spec((1, D)), pl.BlockSpec((MOD_ROWS, D), lambda b: (1 + b, 0))] + c_specs + [
                  _const_spec(cos.shape), _const_spec(sin.shape), cache, cache],
        out_specs=row,
        out_shape=jax.ShapeDtypeStruct((T_LAT, D), f32),
        scratch_shapes=scratch(2 * WINDOW) + past,
        compiler_params=_cp(("parallel",)), name="l0_mixer_lat",
    )(p['sink'], xl, *weights, mods, *consts, cos, sin,
      cache_k.reshape(DEC_BATCH, PAST, LANES), cache_v.reshape(DEC_BATCH, PAST, LANES))
    return xo_ctx, xo_lat, k_new, v_new


C_SLOTS = C_HEADS * SLOT
C_PAIRS = C_HEADS // 2
L1_ROWS = 256


def _l1_kernel(*refs, latent):
    if latent:
        (x_ref, gn_ref, mod_ref, win_ref, gqa_ref, wuq_ref, gq_ref, gkva_ref, wuk_ref, wuvt_ref, gk_ref,
         wo_ref, wuqs_ref, qcos_ref, qsin_ref, kcos_ref, ksin_ref, cckv_ref, ckpe_ref, xo_ref,
         zs, cat, qs, ks, vt, wide, wide2) = refs
        n_ctx = PAST
    else:
        (x_ref, gn_ref, mod_ref, win_ref, gqa_ref, wuq_ref, gq_ref, gkva_ref, wuk_ref, wuvt_ref, gk_ref,
         wo_ref, xo_ref, ckvo_ref, kpeo_ref, zs, cat, qs, ks, vt, wide) = refs
        n_ctx = 0
    n = BLOCK_ROWS
    nt_dims = (((1,), (1,)), ((), ()))

    def inv_rms(v):
        return lax.rsqrt(jnp.sum(v * v, -1, keepdims=True) * (1.0 / C_QK) + EPS)

    def expand_keys(ckv_n, kslot, kb, rope_rows):
        cb = ckv_n.astype(bf16)
        key_rows = pl.ds(pl.multiple_of(kb * L1_ROWS, L1_ROWS), L1_ROWS)
        wide[...] = jnp.dot(cb, wuk_ref[...], preferred_element_type=f32)
        if rope_rows is not None:
            kcos = kcos_ref[rope_rows, :]
            turned = _swap_pairs(kslot) * ksin_ref[rope_rows, :]
        for h in range(C_HEADS):
            kh = wide[:, h * SLOT:(h + 1) * SLOT] + kslot
            if rope_rows is not None:
                kh = inv_rms(kh) * (kh * kcos + turned)
            else:
                kh = kh * inv_rms(kh) * gk_ref[...]
            ks[h, key_rows, :] = kh.astype(bf16)
        v_t = lax.dot_general(wuvt_ref[...], cb, nt_dims, preferred_element_type=f32).astype(bf16)
        for pair in range(C_PAIRS):
            vt[pair, kb] = v_t[pair * LANES:(pair + 1) * LANES, :]

    if latent:
        def past_keys(c, carry):
            r = pl.ds(pl.multiple_of(c * L1_ROWS, L1_ROWS), L1_ROWS)
            expand_keys(cckv_ref[r, :], ckpe_ref[r, :], c, None)
            return carry

        lax.fori_loop(0, PAST // L1_ROWS, past_keys, 0)

    def project(c, carry):
        r = pl.ds(pl.multiple_of(c * PROJ_ROWS, PROJ_ROWS), PROJ_ROWS)
        h = _rms_rows(x_ref[r, :], gn_ref[...]) * (1.0 + _mod(mod_ref, SCALE1)) + _mod(mod_ref, SHIFT1)
        zs[r, :] = jnp.dot(h.astype(bf16), win_ref[...], preferred_element_type=f32)
        return carry

    lax.fori_loop(0, n // PROJ_ROWS, project, 0)

    def prepare(c, carry):
        r = pl.ds(pl.multiple_of(c * L1_ROWS, L1_ROWS), L1_ROWS)
        qa = _rms_rows(zs[r, 0:C_Q_LORA], gqa_ref[...]).astype(bf16)
        wide[...] = jnp.dot(qa, wuq_ref[...], preferred_element_type=f32)
        if latent:
            wide2[...] = jnp.dot(qa, wuqs_ref[...], preferred_element_type=f32)
            qcos, qsin = qcos_ref[r, :], qsin_ref[r, :]
        for h in range(C_HEADS):
            hs = slice(h * SLOT, (h + 1) * SLOT)
            qh = wide[:, hs]
            if latent:
                qh = inv_rms(qh) * (qh * qcos + wide2[:, hs] * qsin)
            else:
                qh = qh * inv_rms(qh) * gq_ref[...]
            qs[h, r, :] = qh.astype(bf16)
        ckv_n = _rms_rows(zs[r, C_Q_LORA:C_Q_LORA + C_KV_LORA], gkva_ref[...])
        kslot = zs[r, C_Q_LORA + C_KV_LORA:ODD_IN_PAD]
        if not latent:
            ckvo_ref[r, :] = ckv_n
            kpeo_ref[c] = kslot.T[C_NOPE:C_QK, :]
        expand_keys(ckv_n, kslot, c + n_ctx // L1_ROWS, r if latent else None)
        return carry

    lax.fori_loop(0, n // L1_ROWS, prepare, 0)

    low = lax.broadcasted_iota(jnp.int32, (2 * C_V, L1_ROWS), 0) < C_V
    n_kblocks = (n_ctx + n) // L1_ROWS
    pairs_per_step = 4 if latent else 8

    blocks_per_step = 1 if latent else 2

    def attend(cc, carry):
        blocks = [cc * blocks_per_step + j for j in range(blocks_per_step)]
        rows = [pl.ds(pl.multiple_of(c * L1_ROWS, L1_ROWS), L1_ROWS) for c in blocks]

        def values_t(pair, c, eb):
            if not latent:
                return jnp.dot(vt[pair, c], eb, preferred_element_type=f32)
            o_t = None
            for b in range(n_kblocks):
                pv = jnp.dot(vt[pair, b], eb[b * L1_ROWS:(b + 1) * L1_ROWS, :], preferred_element_type=f32)
                o_t = pv if o_t is None else o_t + pv
            return o_t

        def pairs_step(i, carry2):
            pairs = [i * pairs_per_step + j for j in range(pairs_per_step)]
            units = [(c, r, 2 * p + hh) for c, r in zip(blocks, rows) for p in pairs for hh in range(2)]
            scores = [lax.dot_general(ks[h] if latent else ks[h, r, :], qs[h, r, :], nt_dims,
                                      preferred_element_type=f32) for _, r, h in units]
            exps = [jnp.exp2(s - jnp.max(s, 0, keepdims=True)) for s in scores]
            dens = [jnp.sum(e, 0, keepdims=True) for e in exps]
            outs = [values_t(h // 2, c, e.astype(bf16)) / den for (c, _, h), e, den in zip(units, exps, dens)]
            for u in range(0, len(units), 2):
                _, r, h = units[u]
                cat[h // 2, r, :] = jnp.where(low, outs[u], outs[u + 1]).T.astype(bf16)
            return carry2

        return lax.fori_loop(0, C_PAIRS // pairs_per_step, pairs_step, carry)

    lax.fori_loop(0, n // L1_ROWS // blocks_per_step, attend, 0)

    def output(c, carry):
        r = pl.ds(pl.multiple_of(c * PROJ_ROWS, PROJ_ROWS), PROJ_ROWS)
        heads = jnp.concatenate([cat[pair, r, :] for pair in range(C_PAIRS)], axis=1)
        y = jnp.dot(heads, wo_ref[...], preferred_element_type=f32)
        xo_ref[r, :] = x_ref[r, :] + _mod(mod_ref, GATE1) * y
        return carry

    lax.fori_loop(0, n // PROJ_ROWS, output, 0)


def _slot_cols(w, heads, width, lo, hi, lane0):
    k = w.shape[0]
    w3 = w.reshape(k, heads, width)[:, :, lo:hi]
    out = jnp.zeros((k, heads, SLOT), w.dtype).at[:, :, lane0:lane0 + (hi - lo)].set(w3)
    return out.reshape(k, heads * SLOT)


def _l1_mixer(xc, xl, g_norm, mods, p, cache_ckv, cache_kpe):
    w_in = jnp.zeros((D, ODD_IN_PAD), f32).at[:, :C_Q_LORA + C_KV_LORA].set(
        p['w_in'][:, :C_Q_LORA + C_KV_LORA]).at[
        :, C_Q_LORA + C_KV_LORA + C_NOPE:C_Q_LORA + C_KV_LORA + C_QK].set(p['w_in'][:, C_Q_LORA + C_KV_LORA:])
    wuq = _slot_cols(p['w_uq'], C_HEADS, C_QK, 0, C_QK, 0).astype(bf16)
    wuk = _slot_cols(p['w_ukv'], C_HEADS, C_NOPE + C_V, 0, C_NOPE, 0).astype(bf16)
    wuv_t = p['w_ukv'].reshape(C_KV_LORA, C_HEADS, C_NOPE + C_V)[:, :, C_NOPE:].reshape(
        C_KV_LORA, C_HEADS * C_V).T.astype(bf16)
    gq = jnp.zeros((1, SLOT), f32).at[0, :C_QK].set(p['g_q'] * C_SCALE)
    gk = jnp.zeros((1, SLOT), f32).at[0, :C_QK].set(p['g_k'])
    consts = (g_norm.reshape(1, D), w_in.astype(bf16), p['g_qa'].reshape(1, C_Q_LORA), wuq, gq,
              p['g_kva'].reshape(1, C_KV_LORA), wuk, wuv_t, gk, p['w_o'].astype(bf16))
    c_specs = [_const_spec(a.shape) for a in consts]
    row = pl.BlockSpec((BLOCK_ROWS, D), lambda b: (b, 0))
    n_ctx_blocks = T_CTX // BLOCK_ROWS

    def scratch(n_keys):
        return [pltpu.VMEM((BLOCK_ROWS, ODD_IN_PAD), f32), pltpu.VMEM((C_PAIRS, BLOCK_ROWS, LANES), bf16),
                pltpu.VMEM((C_HEADS, BLOCK_ROWS, SLOT), bf16), pltpu.VMEM((C_HEADS, n_keys, SLOT), bf16),
                pltpu.VMEM((C_PAIRS, n_keys // L1_ROWS, LANES, L1_ROWS), bf16),
                pltpu.VMEM((L1_ROWS, C_SLOTS), f32)]

    xo_ctx, ckv_new, kpe_t = pl.pallas_call(
        functools.partial(_l1_kernel, latent=False),
        grid=(n_ctx_blocks,),
        in_specs=[row, c_specs[0], pl.BlockSpec((MOD_ROWS, D), lambda b: (0, 0))] + c_specs[1:],
        out_specs=[row, pl.BlockSpec((BLOCK_ROWS, C_KV_LORA), lambda b: (b, 0)),
                   pl.BlockSpec((BLOCK_ROWS // SEQ, C_ROPE, SEQ), lambda b: (b, 0, 0))],
        out_shape=[jax.ShapeDtypeStruct((T_CTX, D), f32), jax.ShapeDtypeStruct((T_CTX, C_KV_LORA), f32),
                   jax.ShapeDtypeStruct((BATCH, C_ROPE, SEQ), f32)],
        scratch_shapes=scratch(BLOCK_ROWS),
        compiler_params=_cp(("parallel",)), name="l1_mixer_ctx",
    )(xc, consts[0], mods, *consts[1:])

    cos, sin = _rope_tables(DEC_SEQ, C_ROPE, SLOT, C_NOPE)
    def pair_swap(a):
        pairs = a.reshape(a.shape[:-1] + (a.shape[-1] // 2, 2))
        return jnp.stack([pairs[..., 1], pairs[..., 0]], axis=-1).reshape(a.shape)

    w_rope = p['w_uq'].reshape(C_Q_LORA, C_HEADS, C_QK)[:, :, C_NOPE:]
    wuq_swapped = jnp.zeros((C_Q_LORA, C_HEADS, SLOT), f32).at[:, :, C_NOPE:C_QK].set(pair_swap(w_rope)).reshape(
        C_Q_LORA, C_SLOTS).astype(bf16)
    rope = (wuq_swapped, gq * cos, pair_swap(gq) * sin, gk * cos, pair_swap(gk) * sin)
    ckpe = jnp.zeros((DEC_BATCH, PAST, SLOT), f32).at[:, :, C_NOPE:C_QK].set(cache_kpe)
    xo_lat = pl.pallas_call(
        functools.partial(_l1_kernel, latent=True),
        grid=(DEC_BATCH,),
        in_specs=[pl.BlockSpec((BLOCK_ROWS, D), lambda b: (b, 0), pipeline_mode=pl.Buffered(1)), c_specs[0],
                  pl.BlockSpec((MOD_ROWS, D), lambda b: (1 + b, 0))] + c_specs[1:] + [_const_spec(a.shape) for a in rope] + [
                  pl.BlockSpec((None, PAST, C_KV_LORA), lambda b: (b, 0, 0)),
                  pl.BlockSpec((None, PAST, SLOT), lambda b: (b, 0, 0))],
        out_specs=row,
        out_shape=jax.ShapeDtypeStruct((T_LAT, D), f32),
        scratch_shapes=scratch(PAST + BLOCK_ROWS) + [pltpu.VMEM((L1_ROWS, C_SLOTS), f32)],
        compiler_params=_cp(("parallel",)), name="l1_mixer_lat",
    )(xl, consts[0], mods, *consts[1:], *rope, cache_ckv, ckpe)
    return xo_ctx, xo_lat, ckv_new, kpe_t.transpose(0, 2, 1)


ROUTER_ROWS = 40
ROUTE_ROWS = 8
ROUTER_CHUNK = 256
ROUTER_STAGES = ROW_TILE // ROUTER_CHUNK
W_LO_ROW0 = 64


def _router_kernel(xc_hbm, xl_hbm, gn_ref, mod_ref, whi_ref, wst_ref, br_ref, h_hbm, route_ref,
                   lg_ref, rin, rout, sem_r, sem_w):
    nt = (((1,), (1,)), ((), ()))
    step = pl.program_id(0)
    n_chunks, n_ctx_chunks = T // ROUTER_CHUNK, T_CTX // ROUTER_CHUNK

    def in_copy(slot, x_hbm, c):
        rows = pl.ds(pl.multiple_of(c * ROUTER_CHUNK, ROUTER_CHUNK), ROUTER_CHUNK)
        return pltpu.make_async_copy(x_hbm.at[rows, :], rin.at[slot], sem_r.at[slot])

    def out_copy(slot, c):
        rows = pl.ds(pl.multiple_of(c * ROUTER_CHUNK * SUBS, ROUTER_CHUNK * SUBS), ROUTER_CHUNK * SUBS)
        return pltpu.make_async_copy(rout.at[slot], h_hbm.at[rows, :], sem_w.at[slot])

    @pl.when(step == 0)
    def _():
        for c in range(ROUTER_STAGES - 1):
            in_copy(c, xc_hbm, c).start()

    for j in range(ROUTER_STAGES):
        c = step * ROUTER_STAGES + j
        in_copy(j, xc_hbm, 0).wait()
        ahead = c + ROUTER_STAGES - 1
        ahead_slot = (j + ROUTER_STAGES - 1) % ROUTER_STAGES
        pl.when(ahead < n_ctx_chunks)(lambda s=ahead_slot, a=ahead: in_copy(s, xc_hbm, a).start())
        pl.when((ahead >= n_ctx_chunks) & (ahead < n_chunks))(
            lambda s=ahead_slot, a=ahead: in_copy(s, xl_hbm, a - n_ctx_chunks).start())
        pl.when(step > 0)(lambda j=j, c=c: out_copy(j, c).wait())
        h = _rms_rows(rin[j], gn_ref[...]) * (1.0 + _mod(mod_ref, SCALE2)) + _mod(mod_ref, SHIFT2)
        _store_token_major(rout.at[j], h)
        out_copy(j, c).start()
        h_hi, h_lo = _split_bf16(h)
        by_hi = lax.dot_general(wst_ref[...], h_hi, nt, preferred_element_type=f32)
        by_lo = lax.dot_general(whi_ref[...], h_lo, nt, preferred_element_type=f32)
        lg_ref[j] = (by_hi[0:ROUTER_ROWS, :] + by_lo[0:ROUTER_ROWS, :]
                     + by_hi[W_LO_ROW0:W_LO_ROW0 + ROUTER_ROWS, :])

    @pl.when(step == pl.num_programs(0) - 1)
    def _():
        for j in range(ROUTER_STAGES):
            out_copy(j, 0).wait()

    logits = (jnp.concatenate([lg_ref[j] for j in range(ROUTER_STAGES)], axis=1) + br_ref[0:ROUTER_ROWS, :])
    row_i = lax.broadcasted_iota(jnp.int32, logits.shape, 0)
    row = row_i.astype(f32)
    big = 1e6
    is_g = (row_i >= N_EXPERTS) & (row_i < N_EXPERTS + N_GROUPS)
    lg = jnp.where(is_g, logits, -jnp.inf)
    mg = jnp.max(lg, 0, keepdims=True)
    gsel = jnp.min(jnp.where(lg == mg, row, big), 0, keepdims=True) - N_EXPERTS
    pg_sel = 1.0 / jnp.sum(jnp.where(is_g, jnp.exp(lg - mg), 0.0), 0, keepdims=True)
    in_grp = (row_i < N_EXPERTS) & ((row_i >> 3).astype(f32) == gsel)
    le = jnp.where(in_grp, logits, -jnp.inf)
    m1 = jnp.max(le, 0, keepdims=True)
    i1 = jnp.min(jnp.where(le == m1, row, big), 0, keepdims=True)
    le2 = jnp.where(row == i1, -jnp.inf, le)
    m2 = jnp.max(le2, 0, keepdims=True)
    i2 = jnp.min(jnp.where(le2 == m2, row, big), 0, keepdims=True)
    e2 = jnp.exp(m2 - m1)
    w1 = pg_sel / (1.0 + e2)
    w2 = pg_sel * e2 / (1.0 + e2)
    sub = lax.broadcasted_iota(jnp.int32, route_ref.shape, 0)
    route_ref[...] = jnp.where(sub == 0, i1, jnp.where(sub == 1, i2, jnp.where(sub == 2, w1,
                                                                                jnp.where(sub == 3, w2, 0.0))))


def _router(xc, xl, g_norm, mods, p):
    wr = jnp.zeros((LANES, D), f32).at[:N_EXPERTS].set(p['w_re'].T).at[
        N_EXPERTS:N_EXPERTS + N_GROUPS].set(p['w_rg'].T)
    w_hi, w_lo = _split_bf16(wr)
    w_stack = w_hi.at[W_LO_ROW0:W_LO_ROW0 + ROUTER_ROWS].set(w_lo[:ROUTER_ROWS])
    br = jnp.zeros((LANES, 1), f32).at[:N_EXPERTS, 0].set(p['b_re']).at[
        N_EXPERTS:N_EXPERTS + N_GROUPS, 0].set(p['b_rg'])
    hbm = pl.BlockSpec(memory_space=pl.ANY)
    return pl.pallas_call(
        _router_kernel,
        grid=(T // ROW_TILE,),
        in_specs=[hbm, hbm,
                  _const_spec((1, D)),
                  pl.BlockSpec((MOD_ROWS, D), lambda i: (_sample_of_tile(i, ROW_TILE), 0)),
                  _const_spec((LANES, D)), _const_spec((LANES, D)), _const_spec((LANES, 1))],
        out_specs=[hbm, pl.BlockSpec((ROUTE_ROWS, ROW_TILE), lambda i: (0, i))],
        out_shape=[jax.ShapeDtypeStruct((T * SUBS, LANES), f32), jax.ShapeDtypeStruct((ROUTE_ROWS, T), f32)],
        scratch_shapes=[pltpu.VMEM((ROUTER_STAGES, ROUTER_ROWS, ROUTER_CHUNK), f32),
                        pltpu.VMEM((ROUTER_STAGES, ROUTER_CHUNK, D), f32),
                        pltpu.VMEM((ROUTER_STAGES, ROUTER_CHUNK * SUBS, LANES), f32),
                        pltpu.SemaphoreType.DMA((ROUTER_STAGES,)), pltpu.SemaphoreType.DMA((ROUTER_STAGES,))],
        compiler_params=_cp(("arbitrary",)), name="router",
    )(xc, xl, g_norm.reshape(1, D), mods, w_hi, w_stack, br)


PLAN_FIRST_TILE, PLAN_TILES = 0, 1


def _plan_kernel(rt_ref, pos_ref, plan_ref, rank):
    n_blk = T // 128
    e_col = lax.broadcasted_iota(jnp.int32, (N_EXPERTS, 128), 0).astype(f32)
    ri = lax.broadcasted_iota(jnp.int32, (128, 128), 0)
    ci = lax.broadcasted_iota(jnp.int32, (128, 128), 1)
    before = jnp.where(ri < ci, 1.0, 0.0).astype(bf16)

    def picks(b):
        cs = slice(b * 128, (b + 1) * 128)
        return rt_ref[0:1, cs] == e_col, rt_ref[1:2, cs] == e_col

    counts = jnp.zeros((N_EXPERTS, 1), f32)
    for b in range(n_blk):
        m0, m1 = picks(b)
        m = jnp.where(m0, 1.0, 0.0) + jnp.where(m1, 1.0, 0.0)
        rank[:, b * 128:(b + 1) * 128] = jnp.dot(m.astype(bf16), before, preferred_element_type=f32) + counts
        counts = counts + jnp.sum(m, axis=1, keepdims=True)

    tiles = jnp.floor((counts + (MOE_TILE - 1.0)) * (1.0 / MOE_TILE))
    er = lax.broadcasted_iota(jnp.int32, (N_EXPERTS, N_EXPERTS), 0)
    ec = lax.broadcasted_iota(jnp.int32, (N_EXPERTS, N_EXPERTS), 1)
    earlier = jnp.where(ec < er, 1.0, 0.0).astype(bf16)
    tile_start = jnp.dot(earlier, jnp.broadcast_to(tiles, (N_EXPERTS, 128)).astype(bf16),
                         preferred_element_type=f32)
    row_start = tile_start * MOE_TILE

    sub = lax.broadcasted_iota(jnp.int32, (8, 128), 0)
    for b in range(n_blk):
        m0, m1 = picks(b)
        base = rank[:, b * 128:(b + 1) * 128] + row_start
        p0 = jnp.sum(jnp.where(m0, base, 0.0), axis=0, keepdims=True)
        p1 = jnp.sum(jnp.where(m1, base, 0.0), axis=0, keepdims=True)
        pos_ref[:, b * 128:(b + 1) * 128] = jnp.where(sub == 0, p0, jnp.where(sub == 1, p1, 0.0)).astype(jnp.int32)

    diag = (lax.broadcasted_iota(jnp.int32, (N_EXPERTS, 128), 0)
            == lax.broadcasted_iota(jnp.int32, (N_EXPERTS, 128), 1))
    first = jnp.sum(jnp.where(diag, tile_start, 0.0), axis=0, keepdims=True)
    count = jnp.sum(jnp.where(diag, tiles, 0.0), axis=0, keepdims=True)
    rows = jnp.where(sub == PLAN_FIRST_TILE, first, jnp.where(sub == PLAN_TILES, count, 0.0))
    plan_ref[...] = rows.astype(jnp.int32)


def _slot_code(t, k):
    return t * SUBS + k * (SUBS // 2)


def _code_offset(code):
    return pl.multiple_of(code & ~(SUBS - 1), SUBS)


def _code_gate_index(code):
    return code >> 2


PAD_CODE = T * SUBS


def _invert_slots(pos_ref, first_ref, count_ref, code_ref):
    def pad_tile(tile, carry):
        for u in range(MOE_TILE):
            code_ref[tile * MOE_TILE + u] = PAD_CODE
        return carry

    def pad_last_tile(e, carry):
        return pad_tile(jnp.maximum(first_ref[e] + count_ref[e] - 1, 0), carry)
    lax.fori_loop(0, N_EXPERTS, pad_last_tile, 0)
    lax.fori_loop(first_ref[N_EXPERTS - 1] + count_ref[N_EXPERTS - 1], MOE_TILES, pad_tile, 0)

    group = 16
    for k in range(2):
        def place(i, carry):
            t0 = i * group
            slots = [pos_ref[k * T + t0 + u] for u in range(group)]
            code0 = _slot_code(t0, 0)
            for u, s in enumerate(slots):
                code_ref[s] = code0 + _slot_code(u, k)
            return carry
        lax.fori_loop(0, T // group, place, 0)


def _route_plan(route_t):
    pos, plan = pl.pallas_call(
        _plan_kernel,
        out_shape=[jax.ShapeDtypeStruct((8, T), jnp.int32), jax.ShapeDtypeStruct((8, LANES), jnp.int32)],
        scratch_shapes=[pltpu.VMEM((N_EXPERTS, T), f32)],
        compiler_params=_cp(None), name="route_plan",
    )(route_t)
    gates = jnp.pad(route_t[2:4].T.reshape(2 * T), (0, 8))
    return plan[PLAN_FIRST_TILE, :N_EXPERTS], plan[PLAN_TILES, :N_EXPERTS], pos[0:2].reshape(2 * T), gates


def _tile_index(i):
    return jnp.minimum(i, MOE_TILES - 1)


TM_ROWS = T * SUBS
SCATTER_GROUP = 8
STAGES = 4
LAST_EXPERT = N_EXPERTS - 1


def _tile_rows(g):
    return pl.ds(pl.multiple_of(g * MOE_TILE, MOE_TILE), MOE_TILE)


def _expert_tile_pairs(first, count, tile_step):
    def pair(pp, carry):
        for parity in range(2):
            g = 2 * pp + parity
            pl.when((g >= first) & (g < first + count))(functools.partial(tile_step, g, parity))
        return carry
    lax.fori_loop(first // 2, (first + count + 1) // 2, pair, 0)


def _gather_tile(code_ref, tile, xs, gbuf):
    base = tile * MOE_TILE
    for r in range(MOE_TILE):
        gbuf[r * SUBS:(r + 1) * SUBS, :] = xs[pl.ds(_code_offset(code_ref[base + r]), SUBS), :]


def _moe_up_kernel(first_ref, count_ref, pos_ref, h_hbm, w1_ref, w3_ref, hh_hbm, code_ref,
                   xs, gbuf_a, gbuf_b, w13, obuf, sem_x, sem_o):
    e = pl.program_id(0)
    first, count = first_ref[e], count_ref[e]
    gbufs = (gbuf_a, gbuf_b)

    def out_copy(slot, g):
        return pltpu.make_async_copy(obuf.at[slot], hh_hbm.at[_tile_rows(g), :], sem_o.at[slot])

    @pl.when(e == 0)
    def _():
        cp = pltpu.make_async_copy(h_hbm, xs.at[pl.ds(0, TM_ROWS), :], sem_x)
        cp.start()
        _invert_slots(pos_ref, first_ref, count_ref, code_ref)
        xs[TM_ROWS:TM_ROWS + SUBS, :] = jnp.zeros((SUBS, LANES), f32)
        cp.wait()
        _gather_tile(code_ref, 0, xs, gbuf_a)

    @pl.when(count > 0)
    def _():
        w13[:, :D_EXPERT] = w1_ref[0].astype(bf16)
        w13[:, D_EXPERT:] = w3_ref[0].astype(bf16)

    def tile_step(g, parity):
        _gather_tile(code_ref, _tile_index(g + 1), xs, gbufs[1 - parity])
        x3 = jnp.swapaxes(gbufs[parity][...].reshape(MOE_TILE, SUBS, LANES), 0, 1)
        x = jnp.concatenate([x3[s] for s in range(SUBS)], axis=1).astype(bf16)
        h13 = jnp.dot(x, w13[...], preferred_element_type=f32)
        hh = (_silu(h13[:, :D_EXPERT]) * h13[:, D_EXPERT:]).astype(bf16)

        slot = g % STAGES

        @pl.when(g >= STAGES)
        def _():
            out_copy(slot, g).wait()
        obuf[slot] = hh
        out_copy(slot, g).start()

    _expert_tile_pairs(first, count, tile_step)

    @pl.when(e == LAST_EXPERT)
    def _():
        n_used = first + count
        for slot in range(STAGES):
            pl.when(n_used > slot)(lambda slot=slot: out_copy(slot, 0).wait())
        obuf[0] = jnp.zeros((MOE_TILE, D_EXPERT), bf16)

        def zero_tile(g, carry):
            cp = out_copy(0, g)
            cp.start()
            cp.wait()
            return carry
        lax.fori_loop(n_used, MOE_TILES, zero_tile, 0)


def _scatter_tile(code_ref, gate_ref, tile, ybuf, acc):
    base = tile * MOE_TILE
    for g0 in range(0, MOE_TILE, SCATTER_GROUP):
        rows = range(g0, g0 + SCATTER_GROUP)
        codes = [code_ref[base + r] for r in rows]
        new = [acc[pl.ds(_code_offset(c), SUBS), :]
               + gate_ref[_code_gate_index(c)] * ybuf[r * SUBS:(r + 1) * SUBS, :]
               for r, c in zip(rows, codes)]
        for c, v in zip(codes, new):
            acc[pl.ds(_code_offset(c), SUBS), :] = v


RES_ROWS = 256


def _residual_out(x_hbm, y_hbm, tok0, sample_of_chunk, mod_ref, acc, rin, rout, sem_r, sem_w):
    n_chunks = x_hbm.shape[0] // RES_ROWS

    def rows(c):
        return pl.ds(pl.multiple_of(c * RES_ROWS, RES_ROWS), RES_ROWS)

    def in_copy(slot, c):
        return pltpu.make_async_copy(x_hbm.at[rows(c), :], rin.at[slot], sem_r.at[slot])

    def out_copy(slot, c):
        return pltpu.make_async_copy(rout.at[slot], y_hbm.at[rows(c), :], sem_w.at[slot])

    for c in range(STAGES - 1):
        in_copy(c, c).start()

    def ring(cc, carry):
        for slot in range(STAGES):
            c = STAGES * cc + slot
            in_copy(slot, c).wait()
            ahead = c + STAGES - 1

            @pl.when(ahead < n_chunks)
            def _():
                in_copy((slot + STAGES - 1) % STAGES, ahead).start()

            @pl.when(c >= STAGES)
            def _():
                out_copy(slot, c).wait()
            delta = _load_token_major(acc, RES_ROWS, tok0 + c * RES_ROWS)
            gate = mod_ref[pl.ds(sample_of_chunk(c) * MOD_ROWS + GATE2, 1), :]
            rout[slot] = rin[slot] + gate * delta
            out_copy(slot, c).start()
        return carry

    lax.fori_loop(0, n_chunks // STAGES, ring, 0)
    for slot in range(STAGES):
        out_copy(slot, 0).wait()


def _moe_down_kernel(first_ref, count_ref, code_ref, gate_ref, hh_hbm, w2_ref, xc_hbm, xl_hbm, mod_ref,
                     yc_hbm, yl_hbm, acc, ybuf_a, ybuf_b, w2b, ibuf, rin, rout, sem_i, sem_r, sem_w):
    e = pl.program_id(0)
    first, count = first_ref[e], count_ref[e]
    n_used = first_ref[LAST_EXPERT] + count_ref[LAST_EXPERT]
    ybufs = (ybuf_a, ybuf_b)

    def in_copy(slot, g):
        return pltpu.make_async_copy(hh_hbm.at[_tile_rows(g), :], ibuf.at[slot], sem_i.at[slot])

    @pl.when(e == 0)
    def _():
        for g in range(STAGES - 1):
            in_copy(g, g).start()

        def zero(c, carry):
            acc[pl.ds(pl.multiple_of(c * 1024, 1024), 1024), :] = jnp.zeros((1024, LANES), f32)
            return carry
        lax.fori_loop(0, TM_ROWS // 1024, zero, 0)
        acc[TM_ROWS:TM_ROWS + SUBS, :] = jnp.zeros((SUBS, LANES), f32)
        ybuf_b[...] = jnp.zeros_like(ybuf_b)

    @pl.when(count > 0)
    def _():
        w2b[...] = w2_ref[0].astype(bf16)

    def tile_step(g, parity):
        slot = g % STAGES
        in_copy(slot, g).wait()
        ahead = g + STAGES - 1

        @pl.when(ahead < n_used)
        def _():
            in_copy(ahead % STAGES, ahead).start()
        _store_token_major(ybufs[parity], jnp.dot(ibuf[slot], w2b[...], preferred_element_type=f32))
        _scatter_tile(code_ref, gate_ref, jnp.maximum(g - 1, 0), ybufs[1 - parity], acc)

    _expert_tile_pairs(first, count, tile_step)

    @pl.when(e == LAST_EXPERT)
    def _():
        for parity in range(2):
            pl.when((n_used > 0) & ((n_used - 1) % 2 == parity))(
                functools.partial(_scatter_tile, code_ref, gate_ref, n_used - 1, ybufs[parity], acc))
        _residual_out(xc_hbm, yc_hbm, 0, lambda c: 0, mod_ref, acc, rin, rout, sem_r, sem_w)
        _residual_out(xl_hbm, yl_hbm, T_CTX, lambda c: 1 + c // (DEC_SEQ // RES_ROWS),
                      mod_ref, acc, rin, rout, sem_r, sem_w)


def _moe(h_tm, route_t, p, xc, xl, mods):
    first_tile, n_tiles, slots, gates = _route_plan(route_t)
    tile_rows = pltpu.VMEM((MOE_TILE * SUBS, LANES), f32)
    staging = pltpu.VMEM((STAGES, MOE_TILE, D_EXPERT), bf16)
    res_rows = pltpu.VMEM((STAGES, RES_ROWS, D), f32)
    hbm = pl.BlockSpec(memory_space=pl.ANY)
    hh, codes = pl.pallas_call(
        _moe_up_kernel,
        grid_spec=pltpu.PrefetchScalarGridSpec(
            num_scalar_prefetch=3, grid=(N_EXPERTS,),
            in_specs=[hbm,
                      pl.BlockSpec((1, D, D_EXPERT), lambda e, f, n, s: (e, 0, 0)),
                      pl.BlockSpec((1, D, D_EXPERT), lambda e, f, n, s: (e, 0, 0))],
            out_specs=[hbm, pl.BlockSpec(memory_space=pltpu.SMEM)],
            scratch_shapes=[pltpu.VMEM((TM_ROWS + SUBS, LANES), f32), tile_rows, tile_rows,
                            pltpu.VMEM((D, 2 * D_EXPERT), bf16), staging,
                            pltpu.SemaphoreType.DMA(()), pltpu.SemaphoreType.DMA((STAGES,))]),
        out_shape=[jax.ShapeDtypeStruct((MOE_ROWS, D_EXPERT), bf16),
                   jax.ShapeDtypeStruct((MOE_ROWS,), jnp.int32)],
        compiler_params=_cp(("arbitrary",)), name="moe_up",
    )(first_tile, n_tiles, slots, h_tm, p['w1'], p['w3'])
    return pl.pallas_call(
        _moe_down_kernel,
        grid_spec=pltpu.PrefetchScalarGridSpec(
            num_scalar_prefetch=4, grid=(N_EXPERTS,),
            in_specs=[hbm, pl.BlockSpec((1, D_EXPERT, D), lambda e, f, n, c, g: (e, 0, 0)), hbm, hbm,
                      pl.BlockSpec((N_SAMPLES * MOD_ROWS, D), lambda e, f, n, c, g: (0, 0),
                                   pipeline_mode=pl.Buffered(1))],
            out_specs=[hbm, hbm],
            scratch_shapes=[pltpu.VMEM((TM_ROWS + SUBS, LANES), f32), tile_rows, tile_rows,
                            pltpu.VMEM((D_EXPERT, D), bf16), staging, res_rows, res_rows,
                            pltpu.SemaphoreType.DMA((STAGES,)), pltpu.SemaphoreType.DMA((STAGES,)),
                            pltpu.SemaphoreType.DMA((STAGES,))]),
        out_shape=[jax.ShapeDtypeStruct((T_CTX, D), f32), jax.ShapeDtypeStruct((T_LAT, D), f32)],
        compiler_params=_cp(("arbitrary",)), name="moe_down",
    )(first_tile, n_tiles, codes, gates, hh, p['w2'], xc, xl, mods)


def kernel(x_prompt, x_sample, cache_l0_k, cache_l0_v, cache_l1_ckv, cache_l1_kpe, c, c_ctx, l0_g_norm1, l0_g_norm2, l0_w_ada, l0_b_ada, l0_w_in, l0_g_vnorm, l0_w_s, l0_b_s, l0_g_q, l0_g_k, l0_sink, l0_w_o, l0_w_rg, l0_b_rg, l0_w_re, l0_b_re, l0_w1, l0_w3, l0_w2, l1_g_norm1, l1_g_norm2, l1_w_ada, l1_b_ada, l1_w_in, l1_g_qa, l1_w_uq, l1_g_kva, l1_w_ukv, l1_g_q, l1_g_k, l1_w_o, l1_w_rg, l1_b_rg, l1_w_re, l1_b_re, l1_w1, l1_w3, l1_w2):
    p0 = dict(w_in=l0_w_in, g_vnorm=l0_g_vnorm, w_s=l0_w_s, b_s=l0_b_s, g_q=l0_g_q, g_k=l0_g_k, sink=l0_sink,
              w_o=l0_w_o, w_rg=l0_w_rg, b_rg=l0_b_rg, w_re=l0_w_re, b_re=l0_b_re, w1=l0_w1, w3=l0_w3, w2=l0_w2)
    p1 = dict(w_in=l1_w_in, g_qa=l1_g_qa, w_uq=l1_w_uq, g_kva=l1_g_kva, w_ukv=l1_w_ukv, g_q=l1_g_q, g_k=l1_g_k,
              w_o=l1_w_o, w_rg=l1_w_rg, b_rg=l1_b_rg, w_re=l1_w_re, b_re=l1_b_re, w1=l1_w1, w3=l1_w3, w2=l1_w2)

    cond8 = jnp.zeros((MOD_ROWS, D), f32).at[0].set(c_ctx).at[1:1 + DEC_BATCH].set(c)
    mods0 = _mod_rows(_adaln(cond8, l0_w_ada, l0_b_ada))
    mods1 = _mod_rows(_adaln(cond8, l1_w_ada, l1_b_ada))

    xc0 = x_prompt.reshape(T_CTX, D)
    xl0 = x_sample.reshape(T_LAT, D)

    xc0m, xl0m, k_new, v_new = _l0_mixer(xc0, xl0, l0_g_norm1, mods0, p0, cache_l0_k, cache_l0_v)
    h0, route0 = _router(xc0m, xl0m, l0_g_norm2, mods0, p0)
    xc1, xl1 = _moe(h0, route0, p0, xc0m, xl0m, mods0)

    xc1m, xl1m, ckv_new, kpe_new = _l1_mixer(xc1, xl1, l1_g_norm1, mods1, p1, cache_l1_ckv, cache_l1_kpe)
    h1, route1 = _router(xc1m, xl1m, l1_g_norm2, mods1, p1)
    y_prompt, y_sample = _moe(h1, route1, p1, xc1m, xl1m, mods1)
    return (y_prompt.reshape(BATCH, SEQ, D), y_sample.reshape(DEC_BATCH, DEC_SEQ, D), k_new, v_new,
            ckv_new.reshape(BATCH, SEQ, C_KV_LORA), kpe_new.reshape(BATCH, SEQ, C_ROPE))
```
